```python
import functools
import jax
import jax.numpy as jnp
from jax import lax
import numpy as np

D_MODEL = 1024
BATCH = 8
SEQ = 2048
DEPTH = 1
DEC_BATCH = 128
DEC_SEQ = 8
PAST_LEN = 16384
PAGE_SIZE = 128

N_Q_HEADS = 8
N_KV_HEADS = 2
HEAD_DIM = 64
GQA_GROUP = N_Q_HEADS // N_KV_HEADS
ATT_WIDTH = N_Q_HEADS * HEAD_DIM
KV_WIDTH = N_KV_HEADS * HEAD_DIM
WINDOW = 128
ROPE_THETA = 500000.0
ROT_DIM = HEAD_DIM // 4
POOL_WINDOWS = (2, 4, 8, 16)
POOL_GROUPS = 4
POOL_WIDTH = D_MODEL - ATT_WIDTH
POOL_GROUP_DIM = POOL_WIDTH // POOL_GROUPS
POOL_STATE = 15
MIX_WIDTH = ATT_WIDTH + POOL_WIDTH
IN_WIDTH = ATT_WIDTH + 2 * KV_WIDTH + POOL_WIDTH
N_EXPERTS = 64
TOP_K = 8
N_EXPERT_GROUPS = 8
TOPK_GROUPS = 4
D_EXPERT = 256
D_SHARED = 256
ROUTED_SCALE = 2.5
EXPERT_BLOCK = 128
PLE_DIM = 256
EPS = 1e-6

kernel_name = 'hymba_swa_sink_pool_moe_ple_step'


def rmsnorm(x, g):
    xf = x.astype(jnp.float32)
    y = xf * lax.rsqrt(jnp.mean(xf * xf, axis=-1, keepdims=True) + EPS)
    return (y * g.astype(jnp.float32)).astype(x.dtype)


def swiglu(x, w_gate, w_up, w_down):
    return (jax.nn.silu(x @ w_gate) * (x @ w_up)) @ w_down


def partial_rope(x, pos):
    half = ROT_DIM // 2
    inv = ROPE_THETA ** (-jnp.arange(half, dtype=jnp.float32) * 2.0 / ROT_DIM)
    ang = pos.astype(jnp.float32)[:, None] * inv[None, :]
    cos = jnp.cos(ang)[:, None, :]
    sin = jnp.sin(ang)[:, None, :]
    xr = x[..., :ROT_DIM].astype(jnp.float32)
    x1, x2 = xr[..., :half], xr[..., half:]
    rot = jnp.concatenate([x1 * cos - x2 * sin, x2 * cos + x1 * sin], axis=-1)
    return jnp.concatenate([rot.astype(x.dtype), x[..., ROT_DIM:]], axis=-1)


def sink_attention(q, k, v, mask, sinks):
    s = jnp.einsum('...qhgd,...khd->...hgqk', q.astype(jnp.float32), k.astype(jnp.float32)) * (HEAD_DIM ** -0.5)
    s = jnp.where(mask[..., None, None, :, :], s, -jnp.inf)
    sink = sinks.astype(jnp.float32).reshape(N_KV_HEADS, GQA_GROUP)[:, :, None, None]
    m = jnp.maximum(jnp.max(s, axis=-1, keepdims=True), sink)
    e = jnp.exp(s - m)
    p = e / (jnp.sum(e, axis=-1, keepdims=True) + jnp.exp(sink - m))
    o = jnp.einsum('...hgqk,...khd->...qhgd', p, v.astype(jnp.float32))
    return o.astype(q.dtype)


def attend_prompt(q, k, v, sinks):
    B, S = q.shape[:2]
    nb = S // WINDOW
    qb = q.reshape(B, nb, WINDOW, N_KV_HEADS, GQA_GROUP, HEAD_DIM)
    kb = k.reshape(B, nb, WINDOW, N_KV_HEADS, HEAD_DIM)
    vb = v.reshape(B, nb, WINDOW, N_KV_HEADS, HEAD_DIM)
    kk = jnp.concatenate([jnp.concatenate([jnp.zeros_like(kb[:, :1]), kb[:, :-1]], axis=1), kb], axis=2)
    vv = jnp.concatenate([jnp.concatenate([jnp.zeros_like(vb[:, :1]), vb[:, :-1]], axis=1), vb], axis=2)
    qpos = jnp.arange(S, dtype=jnp.int32).reshape(nb, WINDOW)
    kpos = qpos[:, :1] - WINDOW + jnp.arange(2 * WINDOW, dtype=jnp.int32)[None, :]
    diff = qpos[:, :, None] - kpos[:, None, :]
    mask = (diff >= 0) & (diff <= WINDOW) & (kpos[:, None, :] >= 0)
    o = sink_attention(qb, kk, vv, mask, sinks)
    return o.reshape(B, S, ATT_WIDTH), k[:, -WINDOW:], v[:, -WINDOW:]


def attend_sample(q, k, v, sinks, k_cache, v_cache, q_pos):
    B, S = q.shape[:2]
    C = k_cache.shape[1]
    kk = jnp.concatenate([k_cache.astype(k.dtype), k], axis=1)
    vv = jnp.concatenate([v_cache.astype(v.dtype), v], axis=1)
    kpos = jnp.concatenate([q_pos[0] - C + jnp.arange(C, dtype=jnp.int32), q_pos])
    diff = q_pos[:, None] - kpos[None, :]
    mask = (diff >= 0) & (diff <= WINDOW) & (kpos[None, :] >= 0)
    o = sink_attention(q, kk, vv, mask, sinks)
    return o.reshape(B, S, ATT_WIDTH), kk[:, -C:], vv[:, -C:]


def pool_mix(u_ext, q_pos, w_pool, pool_scale):
    B, n = u_ext.shape[0], q_pos.shape[0]
    f = u_ext.astype(jnp.float32)
    cs = jnp.concatenate([jnp.zeros_like(f[:, :1]), jnp.cumsum(f, axis=1)], axis=1)
    hi = cs[:, POOL_STATE + 1:]
    parts = []
    for g, w in enumerate(POOL_WINDOWS):
        c0, c1 = g * POOL_GROUP_DIM, (g + 1) * POOL_GROUP_DIM
        lo = cs[:, POOL_STATE + 1 - w:POOL_STATE + 1 - w + n, c0:c1]
        count = jnp.minimum(q_pos + 1, w).astype(jnp.float32)[:, None]
        parts.append((hi[..., c0:c1] - lo) / count)
    d = (jnp.concatenate(parts, axis=-1) - f[:, POOL_STATE:]).reshape(B, n, POOL_GROUPS, POOL_GROUP_DIM)
    y = jnp.einsum('bsgc,gcd->bsgd', d, w_pool.astype(jnp.float32)).reshape(B, n, POOL_WIDTH)
    return (y * pool_scale.astype(jnp.float32)).astype(u_ext.dtype)


def routed_experts(xf, idx, wts, w_gate, w_up, w_down):
    T, D = xf.shape
    A = T * TOP_K
    n_blocks = -(-A // EXPERT_BLOCK) + N_EXPERTS
    flat_e = idx.reshape(A).astype(jnp.int32)
    flat_tok = jnp.repeat(jnp.arange(T, dtype=jnp.int32), TOP_K)
    flat_w = wts.reshape(A)
    order = jnp.argsort(flat_e)
    e_sorted = flat_e[order]
    counts = jnp.bincount(flat_e, length=N_EXPERTS).astype(jnp.int32)
    padded = (counts + EXPERT_BLOCK - 1) // EXPERT_BLOCK * EXPERT_BLOCK
    pad_end = jnp.cumsum(padded)
    pad_start = pad_end - padded
    start = jnp.cumsum(counts) - counts
    dest = pad_start[e_sorted] + jnp.arange(A, dtype=jnp.int32) - start[e_sorted]
    slots = n_blocks * EXPERT_BLOCK
    slot_tok = jnp.full((slots,), T, jnp.int32).at[dest].set(flat_tok[order])
    slot_w = jnp.zeros((slots,), xf.dtype).at[dest].set(flat_w[order])
    block_start = jnp.arange(n_blocks, dtype=jnp.int32) * EXPERT_BLOCK
    block_e = jnp.minimum(jnp.searchsorted(pad_end, block_start, side='right'), N_EXPERTS - 1)
    x_pad = jnp.concatenate([xf, jnp.zeros((1, D), xf.dtype)], axis=0)

    def body(acc, blk):
        tok, w, e = blk
        xb = x_pad[tok]
        yb = swiglu(xb, w_gate[e], w_up[e], w_down[e]) * w[:, None]
        return acc.at[tok].add(yb), None

    acc, _ = lax.scan(body, jnp.zeros((T + 1, D), xf.dtype),
                      (slot_tok.reshape(n_blocks, EXPERT_BLOCK), slot_w.reshape(n_blocks, EXPERT_BLOCK), block_e))
    return acc[:T]


def moe_ffn(x, w_router, router_bias, w_exp_gate, w_exp_up, w_exp_down, w_sh_gate, w_sh_up, w_sh_down):
    B, S, D = x.shape
    xf = x.reshape(B * S, D)
    T = xf.shape[0]
    scores = jax.nn.sigmoid(xf.astype(jnp.float32) @ w_router.astype(jnp.float32))
    biased = scores + router_bias.astype(jnp.float32)
    grp_score = jnp.sum(lax.top_k(biased.reshape(T, N_EXPERT_GROUPS, -1), 2)[0], axis=-1)
    _, top_g = lax.top_k(grp_score, TOPK_GROUPS)
    gmask = jnp.any(top_g[..., None] == jnp.arange(N_EXPERT_GROUPS)[None, None, :], axis=1)
    emask = jnp.repeat(gmask, N_EXPERTS // N_EXPERT_GROUPS, axis=1)
    _, idx = lax.top_k(jnp.where(emask, biased, -jnp.inf), TOP_K)
    sel = jnp.take_along_axis(scores, idx, axis=1)
    wts = (sel / jnp.sum(sel, axis=-1, keepdims=True) * ROUTED_SCALE).astype(xf.dtype)
    routed = routed_experts(xf, idx, wts, w_exp_gate, w_exp_up, w_exp_down)
    shared = swiglu(xf, w_sh_gate, w_sh_up, w_sh_down)
    return (routed + shared).reshape(B, S, D)


def decoder_layer(h, p_i, q_pos, attend, pool_prefix, g_mix, w_in, w_pool, pool_scale, g_att_out, g_pool_out,
                  w_out, g_ffn, w_router, router_bias, w_exp_gate, w_exp_up, w_exp_down, w_sh_gate, w_sh_up,
                  w_sh_down, g_ple, w_ple_gate, w_ple_proj):
    B, S, _ = h.shape
    z = rmsnorm(h, g_mix) @ w_in
    q = z[..., :ATT_WIDTH].reshape(B, S, N_Q_HEADS, HEAD_DIM)
    k = z[..., ATT_WIDTH:ATT_WIDTH + KV_WIDTH].reshape(B, S, N_KV_HEADS, HEAD_DIM)
    v = z[..., ATT_WIDTH + KV_WIDTH:ATT_WIDTH + 2 * KV_WIDTH].reshape(B, S, N_KV_HEADS, HEAD_DIM)
    u = z[..., ATT_WIDTH + 2 * KV_WIDTH:]
    q = partial_rope(q, q_pos).reshape(B, S, N_KV_HEADS, GQA_GROUP, HEAD_DIM)
    k = partial_rope(k, q_pos)
    o_att, k_state, v_state = attend(q, k, v)
    u_ext = jnp.concatenate([pool_prefix.astype(u.dtype), u], axis=1)
    o_pool = pool_mix(u_ext, q_pos, w_pool, pool_scale)
    mixed = jnp.concatenate([rmsnorm(o_att, g_att_out), rmsnorm(o_pool, g_pool_out)], axis=-1)
    h = h + mixed @ w_out
    h = h + moe_ffn(rmsnorm(h, g_ffn), w_router, router_bias, w_exp_gate, w_exp_up, w_exp_down,
                    w_sh_gate, w_sh_up, w_sh_down)
    h = h + (p_i.astype(h.dtype) @ w_ple_proj) * jax.nn.sigmoid(rmsnorm(h, g_ple) @ w_ple_gate)
    return h, k_state, v_state, u_ext[:, -POOL_STATE:]


def setup_inputs(seed: int = 0) -> dict:
    key = jax.random.key(seed)
    ks = jax.random.split(key, 32)

    def nrm(k, shape, scale):
        return jax.random.normal(k, shape, jnp.float32) * scale

    win = min(WINDOW, PAST_LEN)
    L = DEPTH
    return {
        'x_prompt': nrm(ks[0], (BATCH, SEQ, D_MODEL), 1.0),
        'x_sample': nrm(ks[1], (DEC_BATCH, DEC_SEQ, D_MODEL), 1.0),
        'cache_k': nrm(ks[2], (L, DEC_BATCH, win, N_KV_HEADS, HEAD_DIM), 1.0),
        'cache_v': nrm(ks[3], (L, DEC_BATCH, win, N_KV_HEADS, HEAD_DIM), 1.0),
        'state_pool': nrm(ks[4], (L, DEC_BATCH, POOL_STATE, POOL_WIDTH), 1.0),
        'p_prompt': nrm(ks[5], (L, BATCH, SEQ, PLE_DIM), 1.0),
        'p_sample': nrm(ks[6], (L, DEC_BATCH, DEC_SEQ, PLE_DIM), 1.0),
        'g_mix': 1.0 + nrm(ks[7], (L, D_MODEL), 0.02),
        'w_in': nrm(ks[8], (L, D_MODEL, IN_WIDTH), D_MODEL ** -0.5),
        'attn_sinks': nrm(ks[9], (L, N_Q_HEADS), 1.0),
        'w_pool': nrm(ks[10], (L, POOL_GROUPS, POOL_GROUP_DIM, POOL_GROUP_DIM), POOL_GROUP_DIM ** -0.5),
        'pool_scale': 1.0 + nrm(ks[11], (L, POOL_WIDTH), 0.1),
        'g_att_out': 1.0 + nrm(ks[12], (L, ATT_WIDTH), 0.02),
        'g_pool_out': 1.0 + nrm(ks[13], (L, POOL_WIDTH), 0.02),
        'w_out': nrm(ks[14], (L, MIX_WIDTH, D_MODEL), MIX_WIDTH ** -0.5),
        'g_ffn': 1.0 + nrm(ks[15], (L, D_MODEL), 0.02),
        'w_router': nrm(ks[16], (L, D_MODEL, N_EXPERTS), D_MODEL ** -0.5),
        'router_bias': nrm(ks[17], (L, N_EXPERTS), 0.01),
        'w_exp_gate': nrm(ks[18], (L, N_EXPERTS, D_MODEL, D_EXPERT), D_MODEL ** -0.5),
        'w_exp_up': nrm(ks[19], (L, N_EXPERTS, D_MODEL, D_EXPERT), D_MODEL ** -0.5),
        'w_exp_down': nrm(ks[20], (L, N_EXPERTS, D_EXPERT, D_MODEL), D_EXPERT ** -0.5),
        'w_sh_gate': nrm(ks[21], (L, D_MODEL, D_SHARED), D_MODEL ** -0.5),
        'w_sh_up': nrm(ks[22], (L, D_MODEL, D_SHARED), D_MODEL ** -0.5),
        'w_sh_down': nrm(ks[23], (L, D_SHARED, D_MODEL), D_SHARED ** -0.5),
        'g_ple': 1.0 + nrm(ks[24], (L, D_MODEL), 0.02),
        'w_ple_gate': nrm(ks[25], (L, D_MODEL, D_MODEL), D_MODEL ** -0.5),
        'w_ple_proj': nrm(ks[26], (L, PLE_DIM, D_MODEL), PLE_DIM ** -0.5),
        'g_final': 1.0 + nrm(ks[27], (D_MODEL,), 0.02),
    }


def reference(x_prompt, x_sample, cache_k, cache_v, state_pool, p_prompt, p_sample, g_mix, w_in, attn_sinks,
              w_pool, pool_scale, g_att_out, g_pool_out, w_out, g_ffn, w_router, router_bias, w_exp_gate,
              w_exp_up, w_exp_down, w_sh_gate, w_sh_up, w_sh_down, g_ple, w_ple_gate, w_ple_proj, g_final):
    hp, hs = x_prompt, x_sample
    kp, vp, pp, ksl, vsl, psl = [], [], [], [], [], []
    pos_p = jnp.arange(x_prompt.shape[1], dtype=jnp.int32)
    pos_s = PAST_LEN + jnp.arange(x_sample.shape[1], dtype=jnp.int32)
    for i in range(DEPTH):
        lw = (g_mix[i], w_in[i], w_pool[i], pool_scale[i], g_att_out[i], g_pool_out[i], w_out[i], g_ffn[i],
              w_router[i], router_bias[i], w_exp_gate[i], w_exp_up[i], w_exp_down[i], w_sh_gate[i], w_sh_up[i],
              w_sh_down[i], g_ple[i], w_ple_gate[i], w_ple_proj[i])
        att_p = functools.partial(attend_prompt, sinks=attn_sinks[i])
        prefix_p = jnp.zeros((hp.shape[0], POOL_STATE, POOL_WIDTH), hp.dtype)
        hp, k1, v1, s1 = decoder_layer(hp, p_prompt[i], pos_p, att_p, prefix_p, *lw)
        att_s = functools.partial(attend_sample, sinks=attn_sinks[i], k_cache=cache_k[i], v_cache=cache_v[i],
                                  q_pos=pos_s)
        hs, k2, v2, s2 = decoder_layer(hs, p_sample[i], pos_s, att_s, state_pool[i], *lw)
        kp.append(k1)
        vp.append(v1)
        pp.append(s1)
        ksl.append(k2)
        vsl.append(v2)
        psl.append(s2)
    y_prompt = rmsnorm(hp, g_final)
    y_sample = rmsnorm(hs, g_final)
    return (y_prompt, y_sample, jnp.stack(kp), jnp.stack(vp), jnp.stack(pp), jnp.stack(ksl), jnp.stack(vsl), jnp.stack(psl))
```

```python
import functools

import jax
import jax.numpy as jnp
from jax import lax
from jax.experimental import pallas as pl
from jax.experimental.pallas import tpu as pltpu

F32 = jnp.float32
BF16 = jnp.bfloat16
U32 = jnp.uint32
I32 = jnp.int32

D_MODEL = 1024
BATCH = 8
SEQ = 2048
DEC_BATCH = 128
DEC_SEQ = 8
PAST_LEN = 16384
N_Q_HEADS = 8
N_KV_HEADS = 2
HEAD_DIM = 64
GQA_GROUP = N_Q_HEADS // N_KV_HEADS
ATT_WIDTH = N_Q_HEADS * HEAD_DIM
KV_WIDTH = N_KV_HEADS * HEAD_DIM
WINDOW = 128
ROPE_THETA = 500000.0
ROT_DIM = HEAD_DIM // 4
POOL_WINDOWS = (2, 4, 8, 16)
POOL_GROUPS = 4
POOL_WIDTH = D_MODEL - ATT_WIDTH
POOL_GROUP_DIM = POOL_WIDTH // POOL_GROUPS
POOL_STATE = 15
IN_WIDTH = ATT_WIDTH + 2 * KV_WIDTH + POOL_WIDTH
N_EXPERTS = 64
TOP_K = 8
N_EXPERT_GROUPS = 8
GROUP_SIZE = N_EXPERTS // N_EXPERT_GROUPS
TOPK_GROUPS = 4
D_EXPERT = 256
D_SHARED = 256
ROUTED_SCALE = 2.5
PLE_DIM = 256
EPS = 1e-6

T_P = BATCH * SEQ
T_S = DEC_BATCH * DEC_SEQ
T_ALL = T_P + T_S
HALF = D_MODEL // 2
LANES = 128
VMEM_LIMIT = 48 * 1024 * 1024

BM_IN = 256
BQ = WINDOW
SB = 16
BM_R = 256
BT_RANK = 512
BT_MOVE = 128
BM_E = 512
N_ASSIGN = T_ALL * TOP_K
N_BLOCKS = N_ASSIGN // BM_E + N_EXPERTS
N_SLOTS = N_BLOCKS * BM_E


def _cparams(sem):
    return pltpu.CompilerParams(dimension_semantics=sem, vmem_limit_bytes=VMEM_LIMIT)


def _rms(x, g):
    return x * lax.rsqrt(jnp.mean(x * x, axis=-1, keepdims=True) + EPS) * g


def _sigmoid(x):
    return 1.0 / (1.0 + jnp.exp(-x))


def _pack_bf16_pairs(x):
    h = x.shape[-1] // 2
    lo = pltpu.bitcast(x[:, :h].astype(BF16).astype(F32), U32) >> 16
    hi = pltpu.bitcast(x[:, h:].astype(BF16).astype(F32), U32) & jnp.uint32(0xFFFF0000)
    return lo | hi


def _unpack_lo(p):
    return pltpu.bitcast(p << 16, F32)


def _unpack_hi(p):
    return pltpu.bitcast(p & jnp.uint32(0xFFFF0000), F32)


def _inproj_kernel(x_ref, g_ref, w_ref, c_ref, s1_ref, s2_ref, q_ref, k_ref, v_ref, u_ref):
    xn = _rms(x_ref[...], g_ref[...]).astype(BF16)
    z = jnp.dot(xn, w_ref[...], preferred_element_type=F32)
    c, s1, s2 = c_ref[...], s1_ref[...], s2_ref[...]

    def rope(t):
        return t * c + pltpu.roll(t, LANES - ROT_DIM // 2, 1) * s1 + pltpu.roll(t, ROT_DIM // 2, 1) * s2

    for i in range(ATT_WIDTH // LANES):
        sl = slice(i * LANES, (i + 1) * LANES)
        q_ref[:, sl] = (rope(z[:, sl]) * (HEAD_DIM ** -0.5)).astype(q_ref.dtype)
    k_ref[...] = rope(z[:, ATT_WIDTH:ATT_WIDTH + KV_WIDTH])
    v_ref[...] = z[:, ATT_WIDTH + KV_WIDTH:ATT_WIDTH + 2 * KV_WIDTH]
    u_ref[...] = z[:, ATT_WIDTH + 2 * KV_WIDTH:]


def _rope_tables(pos):
    half = ROT_DIM // 2
    inv = ROPE_THETA ** (-jnp.arange(half, dtype=F32) * 2.0 / ROT_DIM)
    ang = pos.astype(F32)[:, None] * inv[None, :]
    cos, sin = jnp.cos(ang), jnp.sin(ang)
    n = pos.shape[0]
    ones = jnp.ones((n, HEAD_DIM - ROT_DIM), F32)
    zeros = jnp.zeros((n, HEAD_DIM - ROT_DIM), F32)
    zh = jnp.zeros((n, half), F32)
    c = jnp.concatenate([cos, cos, ones], axis=1)
    s1 = jnp.concatenate([-sin, zh, zeros], axis=1)
    s2 = jnp.concatenate([zh, sin, zeros], axis=1)
    tile = lambda a: jnp.concatenate([a] * (LANES // HEAD_DIM), axis=1)
    return tile(c), tile(s1), tile(s2)


def _inproj(x2d, g_mix, w_in_bf, tables, q_dtype):
    rows = x2d.shape[0]
    n_tab = tables[0].shape[0] // BM_IN
    row_spec = lambda w: pl.BlockSpec((BM_IN, w), lambda i: (i, 0))
    tab_spec = pl.BlockSpec((BM_IN, LANES), lambda i: (i % n_tab, 0))
    full = lambda a: pl.BlockSpec(a.shape, lambda i: (0,) * a.ndim)
    return pl.pallas_call(
        _inproj_kernel,
        grid=(rows // BM_IN,),
        in_specs=[row_spec(D_MODEL), full(g_mix), full(w_in_bf), tab_spec, tab_spec, tab_spec],
        out_specs=[row_spec(ATT_WIDTH), row_spec(KV_WIDTH), row_spec(KV_WIDTH), row_spec(POOL_WIDTH)],
        out_shape=[jax.ShapeDtypeStruct((rows, ATT_WIDTH), q_dtype),
                   jax.ShapeDtypeStruct((rows, KV_WIDTH), F32),
                   jax.ShapeDtypeStruct((rows, KV_WIDTH), F32),
                   jax.ShapeDtypeStruct((rows, POOL_WIDTH), F32)],
        compiler_params=_cparams(("parallel",)),
        name="inproj",
    )(x2d, g_mix, w_in_bf, *tables)


def _sink_column(sinks_ref, kv_head, rows_per_head):
    n = GQA_GROUP * rows_per_head
    grp = lax.broadcasted_iota(I32, (n, 1), 0) // rows_per_head
    col = jnp.full((n, 1), sinks_ref[kv_head * GQA_GROUP], F32)
    for g in range(1, GQA_GROUP):
        col = jnp.where(grp == g, sinks_ref[kv_head * GQA_GROUP + g], col)
    return col


def _band_mask(n_rows, rows_per_head, n_keys):
    i = lax.broadcasted_iota(I32, (n_rows, n_keys), 0) % rows_per_head
    c = lax.broadcasted_iota(I32, (n_rows, n_keys), 1)
    return (c >= i) & (c <= i + WINDOW), c


def _stack_heads(q, kv_head):
    return jnp.concatenate(
        [q[:, (kv_head * GQA_GROUP + g) * HEAD_DIM:(kv_head * GQA_GROUP + g + 1) * HEAD_DIM]
         for g in range(GQA_GROUP)], axis=0)


def _nt_dot(a, b):
    return lax.dot_general(a, b, (((1,), (1,)), ((), ())), preferred_element_type=F32)


def _pool_delta(u, uext_ref, base, n, cnt_fn):
    parts = []
    for g, w in enumerate(POOL_WINDOWS):
        sl = slice(g * POOL_GROUP_DIM, (g + 1) * POOL_GROUP_DIM)
        acc = u[:, sl]
        for m in range(1, w):
            acc = acc + uext_ref[base - m:base - m + n, sl]
        parts.append(acc / cnt_fn(w) - u[:, sl])
    return parts


def _mixer_tail(o_att, d, h, wpool_ref, pscale_ref, gatt_ref, gpool_ref, wout_ref):
    parts = [jnp.dot(d[:, g * POOL_GROUP_DIM:(g + 1) * POOL_GROUP_DIM].astype(BF16), wpool_ref[g],
                     preferred_element_type=F32) for g in range(POOL_GROUPS)]
    o_pool = jnp.concatenate(parts, axis=-1) * pscale_ref[...]
    mixed = jnp.concatenate([_rms(o_att, gatt_ref[...]), _rms(o_pool, gpool_ref[...])], axis=-1)
    return h + jnp.dot(mixed.astype(BF16), wout_ref[...], preferred_element_type=F32)


def _mixer_prompt_kernel(sinks_ref, h_ref, q_ref, kc_ref, kp_ref, vc_ref, vp_ref, uc_ref, up_ref,
                         wpool_ref, pscale_ref, gatt_ref, gpool_ref, wout_ref, h1_ref, uext_ref):
    j = pl.program_id(1)
    q = q_ref[...]
    kprev, kcur = kp_ref[...].astype(BF16), kc_ref[...].astype(BF16)
    vprev, vcur = vp_ref[...].astype(BF16), vc_ref[...].astype(BF16)
    band, col = _band_mask(GQA_GROUP * BQ, BQ, 2 * WINDOW)
    mask = band & ((col >= WINDOW) | (j > 0))
    heads = []
    for hk in range(N_KV_HEADS):
        sl = slice(hk * HEAD_DIM, (hk + 1) * HEAD_DIM)
        kk = jnp.concatenate([kprev[:, sl], kcur[:, sl]], axis=0)
        vv = jnp.concatenate([vprev[:, sl], vcur[:, sl]], axis=0)
        sink = _sink_column(sinks_ref, hk, BQ)
        s = jnp.where(mask, _nt_dot(_stack_heads(q, hk), kk), -jnp.inf)
        m = jnp.maximum(jnp.max(s, axis=-1, keepdims=True), sink)
        e = jnp.exp(s - m)
        den = jnp.sum(e, axis=-1, keepdims=True) + jnp.exp(sink - m)
        o = jnp.dot(e.astype(BF16), vv, preferred_element_type=F32) / den
        heads += [o[g * BQ:(g + 1) * BQ] for g in range(GQA_GROUP)]
    o_att = jnp.concatenate(heads, axis=-1)

    u = uc_ref[...]
    uext_ref[0:16, :] = jnp.where(j > 0, up_ref[...], 0.0)
    uext_ref[16:16 + BQ, :] = u
    pos = j * BQ + lax.broadcasted_iota(I32, (BQ, 1), 0)
    d = jnp.concatenate(
        _pool_delta(u, uext_ref, 16, BQ, lambda w: jnp.minimum(pos + 1, w).astype(F32)), axis=-1)
    h1_ref[...] = _mixer_tail(o_att, d, h_ref[...], wpool_ref, pscale_ref, gatt_ref, gpool_ref, wout_ref)


def _mixer_sample_kernel(sinks_ref, h_ref, q_ref, kn_ref, vn_ref, u_ref, ck_ref, cv_ref, st_ref,
                         wpool_ref, pscale_ref, gatt_ref, gpool_ref, wout_ref, h1_in_ref,
                         h1_ref, ko_ref, vo_ref, po_ref, oatt_ref, d_ref, uext_ref):
    del h1_in_ref
    n_q = GQA_GROUP * DEC_SEQ
    mask_c, _ = _band_mask(n_q, DEC_SEQ, WINDOW)
    qi = lax.broadcasted_iota(I32, (n_q, DEC_SEQ), 0) % DEC_SEQ
    mask_n = lax.broadcasted_iota(I32, (n_q, DEC_SEQ), 1) <= qi
    sinks = [_sink_column(sinks_ref, hk, DEC_SEQ) for hk in range(N_KV_HEADS)]

    def one_batch(b, carry):
        r0 = pl.multiple_of(b * DEC_SEQ, DEC_SEQ)
        rows = pl.ds(r0, DEC_SEQ)
        qb = q_ref[rows, :]
        ck, cv = ck_ref[b], cv_ref[b]
        kn, vn = kn_ref[rows, :], vn_ref[rows, :]
        ko_ref[b, 0:WINDOW - DEC_SEQ, :] = ck[DEC_SEQ:, :]
        ko_ref[b, WINDOW - DEC_SEQ:WINDOW, :] = kn
        vo_ref[b, 0:WINDOW - DEC_SEQ, :] = cv[DEC_SEQ:, :]
        vo_ref[b, WINDOW - DEC_SEQ:WINDOW, :] = vn
        for hk in range(N_KV_HEADS):
            sl = slice(hk * HEAD_DIM, (hk + 1) * HEAD_DIM)
            qs = _stack_heads(qb, hk)
            s_c = jnp.where(mask_c, _nt_dot(qs.astype(BF16), ck[:, sl].astype(BF16)), -jnp.inf)
            s_n = jnp.where(mask_n, _nt_dot(qs, kn[:, sl]), -jnp.inf)
            m = jnp.maximum(jnp.maximum(jnp.max(s_c, axis=-1, keepdims=True),
                                        jnp.max(s_n, axis=-1, keepdims=True)), sinks[hk])
            e_c, e_n = jnp.exp(s_c - m), jnp.exp(s_n - m)
            den = (jnp.sum(e_c, axis=-1, keepdims=True) + jnp.sum(e_n, axis=-1, keepdims=True)
                   + jnp.exp(sinks[hk] - m))
            o = (jnp.dot(e_c.astype(BF16), cv[:, sl].astype(BF16), preferred_element_type=F32)
                 + jnp.dot(e_n, vn[:, sl], preferred_element_type=F32)) / den
            for g in range(GQA_GROUP):
                h0 = (hk * GQA_GROUP + g) * HEAD_DIM
                oatt_ref[rows, h0:h0 + HEAD_DIM] = o[g * DEC_SEQ:(g + 1) * DEC_SEQ]
        ub = u_ref[rows, :]
        uext_ref[1:16, :] = st_ref[b]
        uext_ref[16:16 + DEC_SEQ, :] = ub
        parts = _pool_delta(ub, uext_ref, 16, DEC_SEQ, lambda w: float(w))
        for g in range(POOL_GROUPS):
            d_ref[rows, g * POOL_GROUP_DIM:(g + 1) * POOL_GROUP_DIM] = parts[g]
        po_ref[b] = uext_ref[16 + DEC_SEQ - POOL_STATE:16 + DEC_SEQ, :]
        return carry

    lax.fori_loop(0, SB, one_batch, 0)
    h1_ref[...] = _mixer_tail(oatt_ref[...], d_ref[...], h_ref[...], wpool_ref, pscale_ref, gatt_ref,
                              gpool_ref, wout_ref)


def _full_spec(a, n_grid):
    nd = a.ndim
    return pl.BlockSpec(a.shape, lambda *_: (0,) * nd)


def _mixer_prompt(sinks, x2d, q, k, v, u, wts):
    nb = SEQ // BQ
    row = lambda w: pl.BlockSpec((BQ, w), lambda b, j: (b * nb + j, 0))
    prev = lambda w: pl.BlockSpec((BQ, w), lambda b, j: (b * nb + jnp.maximum(j - 1, 0), 0))
    uprev = pl.BlockSpec((16, POOL_WIDTH), lambda b, j: (jnp.maximum((b * nb + j) * (BQ // 16) - 1, 0), 0))
    smem = pl.BlockSpec(memory_space=pltpu.SMEM)
    return pl.pallas_call(
        _mixer_prompt_kernel,
        grid=(BATCH, nb),
        in_specs=[smem, row(D_MODEL), row(ATT_WIDTH), row(KV_WIDTH), prev(KV_WIDTH), row(KV_WIDTH),
                  prev(KV_WIDTH), row(POOL_WIDTH), uprev] + [_full_spec(w, 2) for w in wts],
        out_specs=row(D_MODEL),
        out_shape=jax.ShapeDtypeStruct((T_ALL, D_MODEL), F32),
        scratch_shapes=[pltpu.VMEM((16 + BQ, POOL_WIDTH), F32)],
        compiler_params=_cparams(("parallel", "parallel")),
        name="mixer_prompt",
    )(sinks, x2d, q, k, k, v, v, u, u, *wts)


def _mixer_sample(sinks, x2d, q, k, v, u, cache_k, cache_v, state, wts, h1_buf):
    rows = SB * DEC_SEQ
    row = lambda w: pl.BlockSpec((rows, w), lambda i: (i, 0))
    bat = lambda a: pl.BlockSpec((SB,) + a.shape[1:], lambda i: (i, 0, 0))
    smem = pl.BlockSpec(memory_space=pltpu.SMEM)
    h1_blocks_before = T_P // rows
    n_in = 9 + len(wts)
    return pl.pallas_call(
        _mixer_sample_kernel,
        grid=(DEC_BATCH // SB,),
        in_specs=[smem, row(D_MODEL), row(ATT_WIDTH), row(KV_WIDTH), row(KV_WIDTH), row(POOL_WIDTH),
                  bat(cache_k), bat(cache_v), bat(state)] + [_full_spec(w, 1) for w in wts]
                 + [pl.BlockSpec(memory_space=pl.ANY)],
        out_specs=[pl.BlockSpec((rows, D_MODEL), lambda i: (h1_blocks_before + i, 0)),
                   bat(cache_k), bat(cache_v), bat(state)],
        out_shape=[jax.ShapeDtypeStruct((T_ALL, D_MODEL), F32),
                   jax.ShapeDtypeStruct(cache_k.shape, F32),
                   jax.ShapeDtypeStruct(cache_v.shape, F32),
                   jax.ShapeDtypeStruct(state.shape, F32)],
        scratch_shapes=[pltpu.VMEM((rows, ATT_WIDTH), F32), pltpu.VMEM((rows, POOL_WIDTH), F32),
                        pltpu.VMEM((16 + DEC_SEQ, POOL_WIDTH), F32)],
        input_output_aliases={n_in: 0},
        compiler_params=_cparams(("parallel",)),
        name="mixer_sample",
    )(sinks, x2d, q, k, v, u, cache_k, cache_v, state, *wts, h1_buf)


def _first_max(vals, iota, n):
    m = jnp.max(vals, axis=0, keepdims=True)
    idx = jnp.min(jnp.where(vals == m, iota, n), axis=0, keepdims=True)
    return m, idx


def _router_kernel(h1_ref, gffn_ref, wrt_ref, bias_ref, wsg_ref, wsu_ref, wsd_ref,
                   xp_ref, hsh_ref, idx_ref, wts_ref):
    h1 = h1_ref[...]
    xn = _rms(h1, gffn_ref[...])
    logits = lax.dot_general(wrt_ref[...], xn, (((1,), (1,)), ((), ())),
                             precision=lax.Precision.HIGHEST, preferred_element_type=F32)
    scores = _sigmoid(logits)
    biased = scores + bias_ref[...]
    n_tok = biased.shape[1]
    neg = -jnp.inf

    iota_g = lax.broadcasted_iota(I32, (GROUP_SIZE, n_tok), 0)
    grp_rows = []
    for g in range(N_EXPERT_GROUPS):
        blk = biased[g * GROUP_SIZE:(g + 1) * GROUP_SIZE, :]
        top1, i1 = _first_max(blk, iota_g, GROUP_SIZE)
        top2 = jnp.max(jnp.where(iota_g == i1, neg, blk), axis=0, keepdims=True)
        grp_rows.append(top1 + top2)
    gs = jnp.concatenate(grp_rows, axis=0)

    iota_n = lax.broadcasted_iota(I32, (N_EXPERT_GROUPS, n_tok), 0)
    gsel = jnp.zeros((N_EXPERT_GROUPS, n_tok), jnp.bool_)
    for _ in range(TOPK_GROUPS):
        _, gi = _first_max(gs, iota_n, N_EXPERT_GROUPS)
        hit = iota_n == gi
        gsel = gsel | hit
        gs = jnp.where(hit, neg, gs)
    emask = jnp.concatenate(
        [jnp.broadcast_to(gsel[g:g + 1, :], (GROUP_SIZE, n_tok)) for g in range(N_EXPERT_GROUPS)], axis=0)
    masked = jnp.where(emask, biased, neg)

    iota_e = lax.broadcasted_iota(I32, (N_EXPERTS, n_tok), 0)
    idx_rows, sel_rows = [], []
    for _ in range(TOP_K):
        _, ei = _first_max(masked, iota_e, N_EXPERTS)
        hit = iota_e == ei
        idx_rows.append(ei)
        sel_rows.append(jnp.sum(jnp.where(hit, scores, 0.0), axis=0, keepdims=True))
        masked = jnp.where(hit, neg, masked)
    sel = jnp.concatenate(sel_rows, axis=0)
    idx_ref[...] = jnp.concatenate(idx_rows, axis=0)
    wts_ref[...] = sel / jnp.sum(sel, axis=0, keepdims=True) * ROUTED_SCALE

    xb = xn.astype(BF16)
    gate = jnp.dot(xb, wsg_ref[...], preferred_element_type=F32)
    up = jnp.dot(xb, wsu_ref[...], preferred_element_type=F32)
    hmid = (gate * _sigmoid(gate) * up).astype(BF16)
    hsh_ref[...] = h1 + jnp.dot(hmid, wsd_ref[...], preferred_element_type=F32)
    xp_ref[...] = _pack_bf16_pairs(xn)


def _router(h1, g_ffn, w_router_t, bias_col, wsg, wsu, wsd):
    row = lambda w: pl.BlockSpec((BM_R, w), lambda i: (i, 0))
    colblk = pl.BlockSpec((TOP_K, BM_R), lambda i: (0, i))
    ws = [g_ffn, w_router_t, bias_col, wsg, wsu, wsd]
    return pl.pallas_call(
        _router_kernel,
        grid=(T_ALL // BM_R,),
        in_specs=[row(D_MODEL)] + [_full_spec(w, 1) for w in ws],
        out_specs=[row(HALF), row(D_MODEL), colblk, colblk],
        out_shape=[jax.ShapeDtypeStruct((T_ALL, HALF), U32),
                   jax.ShapeDtypeStruct((T_ALL, D_MODEL), F32),
                   jax.ShapeDtypeStruct((TOP_K, T_ALL), I32),
                   jax.ShapeDtypeStruct((TOP_K, T_ALL), F32)],
        compiler_params=_cparams(("parallel",)),
        name="router",
    )(h1, *ws)


def _rank_kernel(idx_ref, tri_ref, rank_ref, cnt_ref, carry_ref):
    @pl.when(pl.program_id(0) == 0)
    def _():
        carry_ref[...] = jnp.zeros_like(carry_ref)

    idx = idx_ref[...]
    n_tok = idx.shape[1]
    iota_e = lax.broadcasted_iota(I32, (N_EXPERTS, n_tok), 0)
    member = jnp.zeros((N_EXPERTS, n_tok), F32)
    for k in range(TOP_K):
        member = member + jnp.where(iota_e == idx[k:k + 1, :], 1.0, 0.0)
    before = jnp.dot(member.astype(BF16), tri_ref[...], preferred_element_type=F32) + carry_ref[...]
    rows = [jnp.sum(jnp.where(iota_e == idx[k:k + 1, :], before, 0.0), axis=0, keepdims=True)
            for k in range(TOP_K)]
    rank_ref[...] = jnp.concatenate(rows, axis=0).astype(I32)
    carry_ref[...] = carry_ref[...] + jnp.sum(member, axis=1, keepdims=True)
    cnt_ref[...] = carry_ref[...].astype(I32)


def _rank(idx_t, tri):
    blk = pl.BlockSpec((TOP_K, BT_RANK), lambda i: (0, i))
    return pl.pallas_call(
        _rank_kernel,
        grid=(T_ALL // BT_RANK,),
        in_specs=[blk, _full_spec(tri, 1)],
        out_specs=[blk, pl.BlockSpec((N_EXPERTS, 1), lambda i: (0, 0))],
        out_shape=[jax.ShapeDtypeStruct((TOP_K, T_ALL), I32),
                   jax.ShapeDtypeStruct((N_EXPERTS, 1), I32)],
        scratch_shapes=[pltpu.VMEM((N_EXPERTS, 1), F32)],
        compiler_params=_cparams(("arbitrary",)),
        name="rank",
    )(idx_t, tri)


def _dest_kernel(idx_ref, rank_ref, cnt_ref, dest_ref, blk_e_ref, n_used_ref):
    counts = cnt_ref[...]
    padded = (counts + (BM_E - 1)) // BM_E * BM_E
    r = lax.broadcasted_iota(I32, (N_EXPERTS, N_EXPERTS), 0)
    c = lax.broadcasted_iota(I32, (N_EXPERTS, N_EXPERTS), 1)
    padded_row = jnp.sum(jnp.where(r == c, padded, 0), axis=0, keepdims=True)
    pad_start = jnp.sum(jnp.where(c < r, padded_row, 0), axis=1, keepdims=True)
    pad_end_row = jnp.sum(jnp.where(r <= c, padded, 0), axis=0, keepdims=True)

    idx = idx_ref[...]
    n_tok = idx.shape[1]
    iota_e = lax.broadcasted_iota(I32, (N_EXPERTS, n_tok), 0)
    rows = [jnp.sum(jnp.where(iota_e == idx[k:k + 1, :], pad_start, 0), axis=0, keepdims=True)
            for k in range(TOP_K)]
    dest_ref[...] = jnp.concatenate(rows, axis=0) + rank_ref[...]

    b0 = lax.broadcasted_iota(I32, (N_BLOCKS_PAD, N_EXPERTS), 0) * BM_E
    be = jnp.sum(jnp.where(pad_end_row <= b0, 1, 0), axis=1, keepdims=True)
    blk_e_ref[...] = jnp.minimum(be, N_EXPERTS - 1)
    n_used_ref[...] = pad_end_row[:, N_EXPERTS - 1:N_EXPERTS] // BM_E


N_BLOCKS_PAD = (N_BLOCKS + 7) // 8 * 8


def _dest(idx_t, rank_t, counts):
    blk = pl.BlockSpec((TOP_K, BT_RANK), lambda i: (0, i))
    one = lambda s: pl.BlockSpec(s, lambda i: (0, 0))
    return pl.pallas_call(
        _dest_kernel,
        grid=(T_ALL // BT_RANK,),
        in_specs=[blk, blk, one((N_EXPERTS, 1))],
        out_specs=[blk, one((N_BLOCKS_PAD, 1)), one((1, 1))],
        out_shape=[jax.ShapeDtypeStruct((TOP_K, T_ALL), I32),
                   jax.ShapeDtypeStruct((N_BLOCKS_PAD, 1), I32),
                   jax.ShapeDtypeStruct((1, 1), I32)],
        compiler_params=_cparams(("arbitrary",)),
        name="dest",
    )(idx_t, rank_t, counts)


def _dispatch_kernel(dest_ref, xp_ref, xs_ref, sem):
    def row_copy(t, k):
        return pltpu.make_async_copy(xp_ref.at[pl.ds(t, 1)], xs_ref.at[pl.ds(dest_ref[k, t], 1)], sem)

    def issue(t, c):
        for k in range(TOP_K):
            row_copy(t, k).start()
        return c

    def drain(t, c):
        for k in range(TOP_K):
            row_copy(t, k).wait()
        return c

    lax.fori_loop(0, BT_MOVE, issue, 0)
    lax.fori_loop(0, BT_MOVE, drain, 0)


def _dispatch(dest_t, xp):
    return pl.pallas_call(
        _dispatch_kernel,
        grid=(T_ALL // BT_MOVE,),
        in_specs=[pl.BlockSpec((TOP_K, BT_MOVE), lambda i: (0, i), memory_space=pltpu.SMEM),
                  pl.BlockSpec((BT_MOVE, HALF), lambda i: (i, 0))],
        out_specs=pl.BlockSpec(memory_space=pl.ANY),
        out_shape=jax.ShapeDtypeStruct((N_SLOTS, HALF), U32),
        scratch_shapes=[pltpu.SemaphoreType.DMA],
        compiler_params=pltpu.CompilerParams(dimension_semantics=("arbitrary",), vmem_limit_bytes=VMEM_LIMIT,
                                             has_side_effects=True),
        name="dispatch",
    )(dest_t, xp)


def _experts_kernel(blk_e_ref, n_used_ref, xs_ref, wg_ref, wu_ref, wd_ref, ys_ref, wgu_s, wd_s):
    b = pl.program_id(0)

    @pl.when(b < n_used_ref[0])
    def _():
        prev = blk_e_ref[jnp.maximum(b - 1, 0)]

        @pl.when((b == 0) | (blk_e_ref[b] != prev))
        def _():
            wgu_s[:, :D_EXPERT] = wg_ref[0].astype(BF16)
            wgu_s[:, D_EXPERT:] = wu_ref[0].astype(BF16)
            wd_s[...] = wd_ref[0].astype(BF16)

        p = xs_ref[...]
        gu = (jnp.dot(_unpack_lo(p).astype(BF16), wgu_s[:HALF, :], preferred_element_type=F32)
              + jnp.dot(_unpack_hi(p).astype(BF16), wgu_s[HALF:, :], preferred_element_type=F32))
        gate, up = gu[:, :D_EXPERT], gu[:, D_EXPERT:]
        hmid = (gate * _sigmoid(gate) * up).astype(BF16)
        ys_ref[...] = _pack_bf16_pairs(jnp.dot(hmid, wd_s[...], preferred_element_type=F32))


def _experts(blk_e, n_used, xs, wg, wu, wd):
    def blk(b, be, nu):
        return jnp.minimum(b, nu[0] - 1)

    grid_spec = pltpu.PrefetchScalarGridSpec(
        num_scalar_prefetch=2,
        grid=(N_BLOCKS,),
        in_specs=[pl.BlockSpec((BM_E, HALF), lambda b, be, nu: (blk(b, be, nu), 0)),
                  pl.BlockSpec((1, D_MODEL, D_EXPERT), lambda b, be, nu: (be[blk(b, be, nu)], 0, 0)),
                  pl.BlockSpec((1, D_MODEL, D_EXPERT), lambda b, be, nu: (be[blk(b, be, nu)], 0, 0)),
                  pl.BlockSpec((1, D_EXPERT, D_MODEL), lambda b, be, nu: (be[blk(b, be, nu)], 0, 0))],
        out_specs=pl.BlockSpec((BM_E, HALF), lambda b, be, nu: (blk(b, be, nu), 0)),
        scratch_shapes=[pltpu.VMEM((D_MODEL, 2 * D_EXPERT), BF16), pltpu.VMEM((D_EXPERT, D_MODEL), BF16)],
    )
    return pl.pallas_call(
        _experts_kernel,
        grid_spec=grid_spec,
        out_shape=jax.ShapeDtypeStruct((N_SLOTS, HALF), U32),
        compiler_params=_cparams(("arbitrary",)),
        name="experts",
    )(blk_e, n_used, xs, wg, wu, wd)


def _combine_kernel(dest_ref, wts_ref, hsh_ref, p_ref, gple_ref, wpg_ref, wpp_ref, gfin_ref, ys_ref,
                    y_ref, yg_ref, sem):
    def row_copy(t, k):
        return pltpu.make_async_copy(ys_ref.at[pl.ds(dest_ref[k, t], 1)], yg_ref.at[k, pl.ds(t, 1)], sem)

    def issue(t, c):
        for k in range(TOP_K):
            row_copy(t, k).start()
        return c

    def drain(t, c):
        for k in range(TOP_K):
            row_copy(t, k).wait()
        return c

    lax.fori_loop(0, BT_MOVE, issue, 0)
    lax.fori_loop(0, BT_MOVE, drain, 0)

    wts = wts_ref[...]
    lo = jnp.zeros((BT_MOVE, HALF), F32)
    hi = jnp.zeros((BT_MOVE, HALF), F32)
    for k in range(TOP_K):
        p = yg_ref[k]
        w = wts[:, k:k + 1]
        lo = lo + w * _unpack_lo(p)
        hi = hi + w * _unpack_hi(p)
    h2 = hsh_ref[...] + jnp.concatenate([lo, hi], axis=-1)
    gate = _sigmoid(jnp.dot(_rms(h2, gple_ref[...]).astype(BF16), wpg_ref[...], preferred_element_type=F32))
    proj = jnp.dot(p_ref[...].astype(BF16), wpp_ref[...], preferred_element_type=F32)
    y_ref[...] = _rms(h2 + proj * gate, gfin_ref[...])


def _combine(dest_t, wts_tok, hsh, p2d, g_ple, wpg, wpp, g_final, ys, row0):
    rows = p2d.shape[0]
    blk0 = row0 // BT_MOVE
    ws = [g_ple, wpg, wpp, g_final]
    return pl.pallas_call(
        _combine_kernel,
        grid=(rows // BT_MOVE,),
        in_specs=[pl.BlockSpec((TOP_K, BT_MOVE), lambda i: (0, blk0 + i), memory_space=pltpu.SMEM),
                  pl.BlockSpec((BT_MOVE, TOP_K), lambda i: (blk0 + i, 0)),
                  pl.BlockSpec((BT_MOVE, D_MODEL), lambda i: (blk0 + i, 0)),
                  pl.BlockSpec((BT_MOVE, PLE_DIM), lambda i: (i, 0))]
                 + [_full_spec(w, 1) for w in ws] + [pl.BlockSpec(memory_space=pl.ANY)],
        out_specs=pl.BlockSpec((BT_MOVE, D_MODEL), lambda i: (i, 0)),
        out_shape=jax.ShapeDtypeStruct((rows, D_MODEL), F32),
        scratch_shapes=[pltpu.VMEM((TOP_K, BT_MOVE, HALF), U32), pltpu.SemaphoreType.DMA],
        compiler_params=_cparams(("arbitrary",)),
        name="combine",
    )(dest_t, wts_tok, hsh, p2d, *ws, ys)


def kernel(x_prompt, x_sample, cache_k, cache_v, state_pool, p_prompt, p_sample, g_mix, w_in, attn_sinks,
           w_pool, pool_scale, g_att_out, g_pool_out, w_out, g_ffn, w_router, router_bias, w_exp_gate,
           w_exp_up, w_exp_down, w_sh_gate, w_sh_up, w_sh_down, g_ple, w_ple_gate, w_ple_proj, g_final):
    row = lambda a: a.reshape(1, -1)
    xp2d = x_prompt.reshape(T_P, D_MODEL)
    xs2d = x_sample.reshape(T_S, D_MODEL)
    w_in_bf = w_in[0].astype(BF16)
    mixer_wts = [w_pool[0].astype(BF16), row(pool_scale[0]), row(g_att_out[0]), row(g_pool_out[0]),
                 w_out[0].astype(BF16)]

    tab_p = _rope_tables(jnp.arange(SEQ, dtype=I32))
    pos_s = PAST_LEN + jnp.arange(DEC_SEQ, dtype=I32)
    tab_s = tuple(jnp.tile(t, (BM_IN // DEC_SEQ, 1)) for t in _rope_tables(pos_s))

    q_p, k_p, v_p, u_p = _inproj(xp2d, row(g_mix[0]), w_in_bf, tab_p, BF16)
    q_s, k_s, v_s, u_s = _inproj(xs2d, row(g_mix[0]), w_in_bf, tab_s, F32)

    h1 = _mixer_prompt(attn_sinks[0], xp2d, q_p, k_p, v_p, u_p, mixer_wts)
    h1, k_sample, v_sample, pool_sample = _mixer_sample(
        attn_sinks[0], xs2d, q_s, k_s, v_s, u_s,
        cache_k[0].reshape(DEC_BATCH, WINDOW, KV_WIDTH), cache_v[0].reshape(DEC_BATCH, WINDOW, KV_WIDTH),
        state_pool[0], mixer_wts, h1)

    xp, hsh, idx_t, wts_t = _router(
        h1, row(g_ffn[0]), w_router[0].T, router_bias[0].reshape(N_EXPERTS, 1),
        w_sh_gate[0].astype(BF16), w_sh_up[0].astype(BF16), w_sh_down[0].astype(BF16))

    tri = (lax.broadcasted_iota(I32, (BT_RANK, BT_RANK), 0)
           < lax.broadcasted_iota(I32, (BT_RANK, BT_RANK), 1)).astype(BF16)
    rank_t, counts = _rank(idx_t, tri)
    dest_t, blk_e, n_used = _dest(idx_t, rank_t, counts)

    xs = _dispatch(dest_t, xp)
    ys = _experts(blk_e.reshape(N_BLOCKS_PAD), n_used.reshape(1), xs, w_exp_gate[0], w_exp_up[0],
                  w_exp_down[0])

    wts_tok = wts_t.T
    ple_wts = (row(g_ple[0]), w_ple_gate[0].astype(BF16), w_ple_proj[0].astype(BF16), row(g_final))
    y_p = _combine(dest_t, wts_tok, hsh, p_prompt[0].reshape(T_P, PLE_DIM), *ple_wts, ys, 0)
    y_s = _combine(dest_t, wts_tok, hsh, p_sample[0].reshape(T_S, PLE_DIM), *ple_wts, ys, T_P)

    kv5 = lambda a, b: a.reshape(1, b, WINDOW, N_KV_HEADS, HEAD_DIM)
    k_prompt = kv5(k_p.reshape(BATCH, SEQ, KV_WIDTH)[:, SEQ - WINDOW:], BATCH)
    v_prompt = kv5(v_p.reshape(BATCH, SEQ, KV_WIDTH)[:, SEQ - WINDOW:], BATCH)
    pool_prompt = u_p.reshape(BATCH, SEQ, POOL_WIDTH)[:, SEQ - POOL_STATE:][None]
    return (y_p.reshape(BATCH, SEQ, D_MODEL), y_s.reshape(DEC_BATCH, DEC_SEQ, D_MODEL),
            k_prompt, v_prompt, pool_prompt,
            kv5(k_sample, DEC_BATCH), kv5(v_sample, DEC_BATCH), pool_sample[None])
```

```python
import functools

import jax
import jax.numpy as jnp
from jax import lax
from jax.experimental import pallas as pl
from jax.experimental.pallas import tpu as pltpu
from jax.experimental.pallas import tpu_sc as plsc

F32 = jnp.float32
BF16 = jnp.bfloat16
U32 = jnp.uint32
I32 = jnp.int32

D_MODEL = 1024
BATCH = 8
SEQ = 2048
DEC_BATCH = 128
DEC_SEQ = 8
PAST_LEN = 16384
N_Q_HEADS = 8
N_KV_HEADS = 2
HEAD_DIM = 64
GQA_GROUP = N_Q_HEADS // N_KV_HEADS
ATT_WIDTH = N_Q_HEADS * HEAD_DIM
KV_WIDTH = N_KV_HEADS * HEAD_DIM
WINDOW = 128
ROPE_THETA = 500000.0
ROT_DIM = HEAD_DIM // 4
POOL_WINDOWS = (2, 4, 8, 16)
POOL_GROUPS = 4
POOL_WIDTH = D_MODEL - ATT_WIDTH
POOL_GROUP_DIM = POOL_WIDTH // POOL_GROUPS
POOL_STATE = 15
IN_WIDTH = ATT_WIDTH + 2 * KV_WIDTH + POOL_WIDTH
N_EXPERTS = 64
TOP_K = 8
N_EXPERT_GROUPS = 8
GROUP_SIZE = N_EXPERTS // N_EXPERT_GROUPS
TOPK_GROUPS = 4
D_EXPERT = 256
D_SHARED = 256
ROUTED_SCALE = 2.5
PLE_DIM = 256
EPS = 1e-6

T_P = BATCH * SEQ
T_S = DEC_BATCH * DEC_SEQ
T_ALL = T_P + T_S
HALF = D_MODEL // 2
LANES = 128
VMEM_LIMIT = 48 * 1024 * 1024

BM_IN = 256
BQ = WINDOW
SB = 16
BM_R = 256
BT_RANK = 512
BT_MOVE = 128
BM_E = 512
N_ASSIGN = T_ALL * TOP_K
N_BLOCKS = N_ASSIGN // BM_E + N_EXPERTS
N_SLOTS = N_BLOCKS * BM_E

ROW_CHUNKS = HALF // LANES
SC_CORES = 2
SC_SUBCORES = 16
SC_WORKERS = SC_CORES * SC_SUBCORES
SC_CHUNK = 32
SC_CHUNKS_PER_WORKER = T_ALL // (SC_WORKERS * SC_CHUNK)
SC_RING = 4
assert SC_CHUNKS_PER_WORKER * SC_WORKERS * SC_CHUNK == T_ALL


def _load_chunks(ref, n_rows, lead=()):
    return [ref[lead + (pl.ds(c, n_rows, stride=ROW_CHUNKS), slice(None))] for c in range(ROW_CHUNKS)]


def _store_chunks(ref, packed):
    n_rows = packed.shape[0]
    for c in range(ROW_CHUNKS):
        ref[pl.ds(c, n_rows, stride=ROW_CHUNKS), :] = packed[:, c * LANES:(c + 1) * LANES]


def _cparams(sem):
    return pltpu.CompilerParams(dimension_semantics=sem, vmem_limit_bytes=VMEM_LIMIT)


def _rms(x, g):
    return x * lax.rsqrt(jnp.mean(x * x, axis=-1, keepdims=True) + EPS) * g


def _sigmoid(x):
    return 1.0 / (1.0 + jnp.exp(-x))


def _pack_bf16_pairs(x):
    h = x.shape[-1] // 2
    lo = pltpu.bitcast(x[:, :h].astype(BF16).astype(F32), U32) >> 16
    hi = pltpu.bitcast(x[:, h:].astype(BF16).astype(F32), U32) & jnp.uint32(0xFFFF0000)
    return lo | hi


def _unpack_lo(p):
    return pltpu.bitcast(p << 16, F32)


def _unpack_hi(p):
    return pltpu.bitcast(p & jnp.uint32(0xFFFF0000), F32)


def _inproj_kernel(x_ref, g_ref, w_ref, c_ref, s1_ref, s2_ref, q_ref, k_ref, v_ref, u_ref):
    xn = _rms(x_ref[...], g_ref[...]).astype(BF16)
    z = jnp.dot(xn, w_ref[...], preferred_element_type=F32)
    c, s1, s2 = c_ref[...], s1_ref[...], s2_ref[...]

    def rope(t):
        return t * c + pltpu.roll(t, LANES - ROT_DIM // 2, 1) * s1 + pltpu.roll(t, ROT_DIM // 2, 1) * s2

    for i in range(ATT_WIDTH // LANES):
        sl = slice(i * LANES, (i + 1) * LANES)
        q_ref[:, sl] = (rope(z[:, sl]) * (HEAD_DIM ** -0.5)).astype(q_ref.dtype)
    k_ref[...] = rope(z[:, ATT_WIDTH:ATT_WIDTH + KV_WIDTH])
    v_ref[...] = z[:, ATT_WIDTH + KV_WIDTH:ATT_WIDTH + 2 * KV_WIDTH]
    u_ref[...] = z[:, ATT_WIDTH + 2 * KV_WIDTH:]


def _rope_tables(pos):
    half = ROT_DIM // 2
    inv = ROPE_THETA ** (-jnp.arange(half, dtype=F32) * 2.0 / ROT_DIM)
    ang = pos.astype(F32)[:, None] * inv[None, :]
    cos, sin = jnp.cos(ang), jnp.sin(ang)
    n = pos.shape[0]
    ones = jnp.ones((n, HEAD_DIM - ROT_DIM), F32)
    zeros = jnp.zeros((n, HEAD_DIM - ROT_DIM), F32)
    zh = jnp.zeros((n, half), F32)
    c = jnp.concatenate([cos, cos, ones], axis=1)
    s1 = jnp.concatenate([-sin, zh, zeros], axis=1)
    s2 = jnp.concatenate([zh, sin, zeros], axis=1)
    tile = lambda a: jnp.concatenate([a] * (LANES // HEAD_DIM), axis=1)
    return tile(c), tile(s1), tile(s2)


def _inproj(x2d, g_mix, w_in_bf, tables, q_dtype):
    rows = x2d.shape[0]
    n_tab = tables[0].shape[0] // BM_IN
    row_spec = lambda w: pl.BlockSpec((BM_IN, w), lambda i: (i, 0))
    tab_spec = pl.BlockSpec((BM_IN, LANES), lambda i: (i % n_tab, 0))
    full = lambda a: pl.BlockSpec(a.shape, lambda i: (0,) * a.ndim)
    return pl.pallas_call(
        _inproj_kernel,
        grid=(rows // BM_IN,),
        in_specs=[row_spec(D_MODEL), full(g_mix), full(w_in_bf), tab_spec, tab_spec, tab_spec],
        out_specs=[row_spec(ATT_WIDTH), row_spec(KV_WIDTH), row_spec(KV_WIDTH), row_spec(POOL_WIDTH)],
        out_shape=[jax.ShapeDtypeStruct((rows, ATT_WIDTH), q_dtype),
                   jax.ShapeDtypeStruct((rows, KV_WIDTH), F32),
                   jax.ShapeDtypeStruct((rows, KV_WIDTH), F32),
                   jax.ShapeDtypeStruct((rows, POOL_WIDTH), F32)],
        compiler_params=_cparams(("parallel",)),
        name="inproj",
    )(x2d, g_mix, w_in_bf, *tables)


def _sink_column(sinks_ref, kv_head, rows_per_head):
    n = GQA_GROUP * rows_per_head
    grp = lax.broadcasted_iota(I32, (n, 1), 0) // rows_per_head
    col = jnp.full((n, 1), sinks_ref[kv_head * GQA_GROUP], F32)
    for g in range(1, GQA_GROUP):
        col = jnp.where(grp == g, sinks_ref[kv_head * GQA_GROUP + g], col)
    return col


def _band_mask(n_rows, rows_per_head, n_keys):
    i = lax.broadcasted_iota(I32, (n_rows, n_keys), 0) % rows_per_head
    c = lax.broadcasted_iota(I32, (n_rows, n_keys), 1)
    return (c >= i) & (c <= i + WINDOW), c


def _stack_heads(q, kv_head):
    return jnp.concatenate(
        [q[:, (kv_head * GQA_GROUP + g) * HEAD_DIM:(kv_head * GQA_GROUP + g + 1) * HEAD_DIM]
         for g in range(GQA_GROUP)], axis=0)


def _nt_dot(a, b):
    return lax.dot_general(a, b, (((1,), (1,)), ((), ())), preferred_element_type=F32)


def _pool_delta(u, uext_ref, base, n, cnt_fn):
    parts = []
    for g, w in enumerate(POOL_WINDOWS):
        sl = slice(g * POOL_GROUP_DIM, (g + 1) * POOL_GROUP_DIM)
        acc = u[:, sl]
        for m in range(1, w):
            acc = acc + uext_ref[base - m:base - m + n, sl]
        parts.append(acc / cnt_fn(w) - u[:, sl])
    return parts


def _mixer_tail(o_att, d, h, wpool_ref, pscale_ref, gatt_ref, gpool_ref, wout_ref):
    parts = [jnp.dot(d[:, g * POOL_GROUP_DIM:(g + 1) * POOL_GROUP_DIM].astype(BF16), wpool_ref[g],
                     preferred_element_type=F32) for g in range(POOL_GROUPS)]
    o_pool = jnp.concatenate(parts, axis=-1) * pscale_ref[...]
    mixed = jnp.concatenate([_rms(o_att, gatt_ref[...]), _rms(o_pool, gpool_ref[...])], axis=-1)
    return h + jnp.dot(mixed.astype(BF16), wout_ref[...], preferred_element_type=F32)


def _mixer_prompt_kernel(sinks_ref, h_ref, q_ref, kc_ref, kp_ref, vc_ref, vp_ref, uc_ref, up_ref,
                         wpool_ref, pscale_ref, gatt_ref, gpool_ref, wout_ref, h1_ref, uext_ref):
    j = pl.program_id(1)
    q = q_ref[...]
    kprev, kcur = kp_ref[...].astype(BF16), kc_ref[...].astype(BF16)
    vprev, vcur = vp_ref[...].astype(BF16), vc_ref[...].astype(BF16)
    band, col = _band_mask(GQA_GROUP * BQ, BQ, 2 * WINDOW)
    mask = band & ((col >= WINDOW) | (j > 0))
    heads = []
    for hk in range(N_KV_HEADS):
        sl = slice(hk * HEAD_DIM, (hk + 1) * HEAD_DIM)
        kk = jnp.concatenate([kprev[:, sl], kcur[:, sl]], axis=0)
        vv = jnp.concatenate([vprev[:, sl], vcur[:, sl]], axis=0)
        sink = _sink_column(sinks_ref, hk, BQ)
        s = jnp.where(mask, _nt_dot(_stack_heads(q, hk), kk), -jnp.inf)
        m = jnp.maximum(jnp.max(s, axis=-1, keepdims=True), sink)
        e = jnp.exp(s - m)
        den = jnp.sum(e, axis=-1, keepdims=True) + jnp.exp(sink - m)
        o = jnp.dot(e.astype(BF16), vv, preferred_element_type=F32) / den
        heads += [o[g * BQ:(g + 1) * BQ] for g in range(GQA_GROUP)]
    o_att = jnp.concatenate(heads, axis=-1)

    u = uc_ref[...]
    uext_ref[0:16, :] = jnp.where(j > 0, up_ref[...], 0.0)
    uext_ref[16:16 + BQ, :] = u
    pos = j * BQ + lax.broadcasted_iota(I32, (BQ, 1), 0)
    d = jnp.concatenate(
        _pool_delta(u, uext_ref, 16, BQ, lambda w: jnp.minimum(pos + 1, w).astype(F32)), axis=-1)
    h1_ref[...] = _mixer_tail(o_att, d, h_ref[...], wpool_ref, pscale_ref, gatt_ref, gpool_ref, wout_ref)


def _mixer_sample_kernel(sinks_ref, h_ref, q_ref, kn_ref, vn_ref, u_ref, ck_ref, cv_ref, st_ref,
                         wpool_ref, pscale_ref, gatt_ref, gpool_ref, wout_ref, h1_in_ref,
                         h1_ref, ko_ref, vo_ref, po_ref, oatt_ref, d_ref, uext_ref):
    del h1_in_ref
    n_q = GQA_GROUP * DEC_SEQ
    mask_c, _ = _band_mask(n_q, DEC_SEQ, WINDOW)
    qi = lax.broadcasted_iota(I32, (n_q, DEC_SEQ), 0) % DEC_SEQ
    mask_n = lax.broadcasted_iota(I32, (n_q, DEC_SEQ), 1) <= qi
    sinks = [_sink_column(sinks_ref, hk, DEC_SEQ) for hk in range(N_KV_HEADS)]

    def one_batch(b, carry):
        r0 = pl.multiple_of(b * DEC_SEQ, DEC_SEQ)
        rows = pl.ds(r0, DEC_SEQ)
        qb = q_ref[rows, :]
        ck, cv = ck_ref[b], cv_ref[b]
        kn, vn = kn_ref[rows, :], vn_ref[rows, :]
        ko_ref[b, 0:WINDOW - DEC_SEQ, :] = ck[DEC_SEQ:, :]
        ko_ref[b, WINDOW - DEC_SEQ:WINDOW, :] = kn
        vo_ref[b, 0:WINDOW - DEC_SEQ, :] = cv[DEC_SEQ:, :]
        vo_ref[b, WINDOW - DEC_SEQ:WINDOW, :] = vn
        for hk in range(N_KV_HEADS):
            sl = slice(hk * HEAD_DIM, (hk + 1) * HEAD_DIM)
            qs = _stack_heads(qb, hk)
            s_c = jnp.where(mask_c, _nt_dot(qs.astype(BF16), ck[:, sl].astype(BF16)), -jnp.inf)
            s_n = jnp.where(mask_n, _nt_dot(qs, kn[:, sl]), -jnp.inf)
            m = jnp.maximum(jnp.maximum(jnp.max(s_c, axis=-1, keepdims=True),
                                        jnp.max(s_n, axis=-1, keepdims=True)), sinks[hk])
            e_c, e_n = jnp.exp(s_c - m), jnp.exp(s_n - m)
            den = (jnp.sum(e_c, axis=-1, keepdims=True) + jnp.sum(e_n, axis=-1, keepdims=True)
                   + jnp.exp(sinks[hk] - m))
            o = (jnp.dot(e_c.astype(BF16), cv[:, sl].astype(BF16), preferred_element_type=F32)
                 + jnp.dot(e_n, vn[:, sl], preferred_element_type=F32)) / den
            for g in range(GQA_GROUP):
                h0 = (hk * GQA_GROUP + g) * HEAD_DIM
                oatt_ref[rows, h0:h0 + HEAD_DIM] = o[g * DEC_SEQ:(g + 1) * DEC_SEQ]
        ub = u_ref[rows, :]
        uext_ref[1:16, :] = st_ref[b]
        uext_ref[16:16 + DEC_SEQ, :] = ub
        parts = _pool_delta(ub, uext_ref, 16, DEC_SEQ, lambda w: float(w))
        for g in range(POOL_GROUPS):
            d_ref[rows, g * POOL_GROUP_DIM:(g + 1) * POOL_GROUP_DIM] = parts[g]
        po_ref[b] = uext_ref[16 + DEC_SEQ - POOL_STATE:16 + DEC_SEQ, :]
        return carry

    lax.fori_loop(0, SB, one_batch, 0)
    h1_ref[...] = _mixer_tail(oatt_ref[...], d_ref[...], h_ref[...], wpool_ref, pscale_ref, gatt_ref,
                              gpool_ref, wout_ref)


def _full_spec(a, n_grid):
    nd = a.ndim
    return pl.BlockSpec(a.shape, lambda *_: (0,) * nd)


def _mixer_prompt(sinks, x2d, q, k, v, u, wts):
    nb = SEQ // BQ
    row = lambda w: pl.BlockSpec((BQ, w), lambda b, j: (b * nb + j, 0))
    prev = lambda w: pl.BlockSpec((BQ, w), lambda b, j: (b * nb + jnp.maximum(j - 1, 0), 0))
    uprev = pl.BlockSpec((16, POOL_WIDTH), lambda b, j: (jnp.maximum((b * nb + j) * (BQ // 16) - 1, 0), 0))
    smem = pl.BlockSpec(memory_space=pltpu.SMEM)
    return pl.pallas_call(
        _mixer_prompt_kernel,
        grid=(BATCH, nb),
        in_specs=[smem, row(D_MODEL), row(ATT_WIDTH), row(KV_WIDTH), prev(KV_WIDTH), row(KV_WIDTH),
                  prev(KV_WIDTH), row(POOL_WIDTH), uprev] + [_full_spec(w, 2) for w in wts],
        out_specs=row(D_MODEL),
        out_shape=jax.ShapeDtypeStruct((T_ALL, D_MODEL), F32),
        scratch_shapes=[pltpu.VMEM((16 + BQ, POOL_WIDTH), F32)],
        compiler_params=_cparams(("parallel", "parallel")),
        name="mixer_prompt",
    )(sinks, x2d, q, k, k, v, v, u, u, *wts)


def _mixer_sample(sinks, x2d, q, k, v, u, cache_k, cache_v, state, wts, h1_buf):
    rows = SB * DEC_SEQ
    row = lambda w: pl.BlockSpec((rows, w), lambda i: (i, 0))
    bat = lambda a: pl.BlockSpec((SB,) + a.shape[1:], lambda i: (i, 0, 0))
    smem = pl.BlockSpec(memory_space=pltpu.SMEM)
    h1_blocks_before = T_P // rows
    n_in = 9 + len(wts)
    return pl.pallas_call(
        _mixer_sample_kernel,
        grid=(DEC_BATCH // SB,),
        in_specs=[smem, row(D_MODEL), row(ATT_WIDTH), row(KV_WIDTH), row(KV_WIDTH), row(POOL_WIDTH),
                  bat(cache_k), bat(cache_v), bat(state)] + [_full_spec(w, 1) for w in wts]
                 + [pl.BlockSpec(memory_space=pl.ANY)],
        out_specs=[pl.BlockSpec((rows, D_MODEL), lambda i: (h1_blocks_before + i, 0)),
                   bat(cache_k), bat(cache_v), bat(state)],
        out_shape=[jax.ShapeDtypeStruct((T_ALL, D_MODEL), F32),
                   jax.ShapeDtypeStruct(cache_k.shape, F32),
                   jax.ShapeDtypeStruct(cache_v.shape, F32),
                   jax.ShapeDtypeStruct(state.shape, F32)],
        scratch_shapes=[pltpu.VMEM((rows, ATT_WIDTH), F32), pltpu.VMEM((rows, POOL_WIDTH), F32),
                        pltpu.VMEM((16 + DEC_SEQ, POOL_WIDTH), F32)],
        input_output_aliases={n_in: 0},
        compiler_params=_cparams(("parallel",)),
        name="mixer_sample",
    )(sinks, x2d, q, k, v, u, cache_k, cache_v, state, *wts, h1_buf)


def _first_max(vals, iota, n):
    m = jnp.max(vals, axis=0, keepdims=True)
    idx = jnp.min(jnp.where(vals == m, iota, n), axis=0, keepdims=True)
    return m, idx


def _router_kernel(h1_ref, gffn_ref, wrt_ref, bias_ref, wsg_ref, wsu_ref, wsd_ref,
                   xp_ref, hsh_ref, idx_ref, wts_ref):
    h1 = h1_ref[...]
    xn = _rms(h1, gffn_ref[...])
    logits = lax.dot_general(wrt_ref[...], xn, (((1,), (1,)), ((), ())),
                             precision=lax.Precision.HIGHEST, preferred_element_type=F32)
    scores = _sigmoid(logits)
    biased = scores + bias_ref[...]
    n_tok = biased.shape[1]
    neg = -jnp.inf

    iota_g = lax.broadcasted_iota(I32, (GROUP_SIZE, n_tok), 0)
    grp_rows = []
    for g in range(N_EXPERT_GROUPS):
        blk = biased[g * GROUP_SIZE:(g + 1) * GROUP_SIZE, :]
        top1, i1 = _first_max(blk, iota_g, GROUP_SIZE)
        top2 = jnp.max(jnp.where(iota_g == i1, neg, blk), axis=0, keepdims=True)
        grp_rows.append(top1 + top2)
    gs = jnp.concatenate(grp_rows, axis=0)

    iota_n = lax.broadcasted_iota(I32, (N_EXPERT_GROUPS, n_tok), 0)
    gsel = jnp.zeros((N_EXPERT_GROUPS, n_tok), jnp.bool_)
    for _ in range(TOPK_GROUPS):
        _, gi = _first_max(gs, iota_n, N_EXPERT_GROUPS)
        hit = iota_n == gi
        gsel = gsel | hit
        gs = jnp.where(hit, neg, gs)
    emask = jnp.concatenate(
        [jnp.broadcast_to(gsel[g:g + 1, :], (GROUP_SIZE, n_tok)) for g in range(N_EXPERT_GROUPS)], axis=0)
    masked = jnp.where(emask, biased, neg)

    iota_e = lax.broadcasted_iota(I32, (N_EXPERTS, n_tok), 0)
    idx_rows, sel_rows = [], []
    for _ in range(TOP_K):
        _, ei = _first_max(masked, iota_e, N_EXPERTS)
        hit = iota_e == ei
        idx_rows.append(ei)
        sel_rows.append(jnp.sum(jnp.where(hit, scores, 0.0), axis=0, keepdims=True))
        masked = jnp.where(hit, neg, masked)
    sel = jnp.concatenate(sel_rows, axis=0)
    idx_ref[...] = jnp.concatenate(idx_rows, axis=0)
    wts_ref[...] = sel / jnp.sum(sel, axis=0, keepdims=True) * ROUTED_SCALE

    xb = xn.astype(BF16)
    gate = jnp.dot(xb, wsg_ref[...], preferred_element_type=F32)
    up = jnp.dot(xb, wsu_ref[...], preferred_element_type=F32)
    hmid = (gate * _sigmoid(gate) * up).astype(BF16)
    hsh_ref[...] = h1 + jnp.dot(hmid, wsd_ref[...], preferred_element_type=F32)
    _store_chunks(xp_ref, _pack_bf16_pairs(xn))


def _router(h1, g_ffn, w_router_t, bias_col, wsg, wsu, wsd):
    row = lambda w: pl.BlockSpec((BM_R, w), lambda i: (i, 0))
    colblk = pl.BlockSpec((TOP_K, BM_R), lambda i: (0, i))
    ws = [g_ffn, w_router_t, bias_col, wsg, wsu, wsd]
    return pl.pallas_call(
        _router_kernel,
        grid=(T_ALL // BM_R,),
        in_specs=[row(D_MODEL)] + [_full_spec(w, 1) for w in ws],
        out_specs=[pl.BlockSpec((BM_R * ROW_CHUNKS, LANES), lambda i: (i, 0)), row(D_MODEL), colblk, colblk],
        out_shape=[jax.ShapeDtypeStruct((T_ALL * ROW_CHUNKS, LANES), U32),
                   jax.ShapeDtypeStruct((T_ALL, D_MODEL), F32),
                   jax.ShapeDtypeStruct((TOP_K, T_ALL), I32),
                   jax.ShapeDtypeStruct((TOP_K, T_ALL), F32)],
        compiler_params=_cparams(("parallel",)),
        name="router",
    )(h1, *ws)


def _rank_kernel(idx_ref, tri_ref, rank_ref, cnt_ref, carry_ref):
    @pl.when(pl.program_id(0) == 0)
    def _():
        carry_ref[...] = jnp.zeros_like(carry_ref)

    idx = idx_ref[...]
    n_tok = idx.shape[1]
    iota_e = lax.broadcasted_iota(I32, (N_EXPERTS, n_tok), 0)
    member = jnp.zeros((N_EXPERTS, n_tok), F32)
    for k in range(TOP_K):
        member = member + jnp.where(iota_e == idx[k:k + 1, :], 1.0, 0.0)
    before = jnp.dot(member.astype(BF16), tri_ref[...], preferred_element_type=F32) + carry_ref[...]
    rows = [jnp.sum(jnp.where(iota_e == idx[k:k + 1, :], before, 0.0), axis=0, keepdims=True)
            for k in range(TOP_K)]
    rank_ref[...] = jnp.concatenate(rows, axis=0).astype(I32)
    carry_ref[...] = carry_ref[...] + jnp.sum(member, axis=1, keepdims=True)
    cnt_ref[...] = carry_ref[...].astype(I32)


def _rank(idx_t, tri):
    blk = pl.BlockSpec((TOP_K, BT_RANK), lambda i: (0, i))
    return pl.pallas_call(
        _rank_kernel,
        grid=(T_ALL // BT_RANK,),
        in_specs=[blk, _full_spec(tri, 1)],
        out_specs=[blk, pl.BlockSpec((N_EXPERTS, 1), lambda i: (0, 0))],
        out_shape=[jax.ShapeDtypeStruct((TOP_K, T_ALL), I32),
                   jax.ShapeDtypeStruct((N_EXPERTS, 1), I32)],
        scratch_shapes=[pltpu.VMEM((N_EXPERTS, 1), F32)],
        compiler_params=_cparams(("arbitrary",)),
        name="rank",
    )(idx_t, tri)


def _dest_kernel(idx_ref, rank_ref, cnt_ref, dest_ref, blk_e_ref, n_used_ref):
    counts = cnt_ref[...]
    padded = (counts + (BM_E - 1)) // BM_E * BM_E
    r = lax.broadcasted_iota(I32, (N_EXPERTS, N_EXPERTS), 0)
    c = lax.broadcasted_iota(I32, (N_EXPERTS, N_EXPERTS), 1)
    padded_row = jnp.sum(jnp.where(r == c, padded, 0), axis=0, keepdims=True)
    pad_start = jnp.sum(jnp.where(c < r, padded_row, 0), axis=1, keepdims=True)
    pad_end_row = jnp.sum(jnp.where(r <= c, padded, 0), axis=0, keepdims=True)

    idx = idx_ref[...]
    n_tok = idx.shape[1]
    iota_e = lax.broadcasted_iota(I32, (N_EXPERTS, n_tok), 0)
    rows = [jnp.sum(jnp.where(iota_e == idx[k:k + 1, :], pad_start, 0), axis=0, keepdims=True)
            for k in range(TOP_K)]
    dest_ref[...] = jnp.concatenate(rows, axis=0) + rank_ref[...]

    b0 = lax.broadcasted_iota(I32, (N_BLOCKS_PAD, N_EXPERTS), 0) * BM_E
    be = jnp.sum(jnp.where(pad_end_row <= b0, 1, 0), axis=1, keepdims=True)
    blk_e_ref[...] = jnp.minimum(be, N_EXPERTS - 1)
    n_used_ref[...] = pad_end_row[:, N_EXPERTS - 1:N_EXPERTS] // BM_E


N_BLOCKS_PAD = (N_BLOCKS + 7) // 8 * 8


def _dest(idx_t, rank_t, counts):
    blk = pl.BlockSpec((TOP_K, BT_RANK), lambda i: (0, i))
    one = lambda s: pl.BlockSpec(s, lambda i: (0, 0))
    return pl.pallas_call(
        _dest_kernel,
        grid=(T_ALL // BT_RANK,),
        in_specs=[blk, blk, one((N_EXPERTS, 1))],
        out_specs=[blk, one((N_BLOCKS_PAD, 1)), one((1, 1))],
        out_shape=[jax.ShapeDtypeStruct((TOP_K, T_ALL), I32),
                   jax.ShapeDtypeStruct((N_BLOCKS_PAD, 1), I32),
                   jax.ShapeDtypeStruct((1, 1), I32)],
        compiler_params=_cparams(("arbitrary",)),
        name="dest",
    )(idx_t, rank_t, counts)


def _sc_mesh():
    return plsc.VectorSubcoreMesh(core_axis_name="c", subcore_axis_name="s")


def _sc_worker_id():
    return lax.axis_index("s") * SC_CORES + lax.axis_index("c")


def _dispatch_body(dest_hbm, xp_hbm, xs_hbm, idx_v, rows_v, sem_in, sem_out):
    chunk0 = _sc_worker_id() * SC_CHUNKS_PER_WORKER

    def loads(i):
        chunk = chunk0 + i
        t0 = pl.multiple_of(chunk * SC_CHUNK, SC_CHUNK)
        return (pltpu.make_async_copy(dest_hbm.at[chunk], idx_v.at[i % 2], sem_in.at[i % 2]),
                pltpu.make_async_copy(xp_hbm.at[pl.ds(t0, SC_CHUNK)], rows_v.at[i % 2], sem_in.at[i % 2]))

    def scatters(i):
        return [pltpu.make_async_copy(rows_v.at[i % 2], xs_hbm.at[idx_v.at[i % 2, k]], sem_out.at[i % 2])
                for k in range(TOP_K)]

    for cp in loads(0):
        cp.start()
    for i in range(SC_CHUNKS_PER_WORKER):
        for cp in loads(i):
            cp.wait()
        if i >= 1:
            for cp in scatters(i - 1):
                cp.wait()
        if i + 1 < SC_CHUNKS_PER_WORKER:
            for cp in loads(i + 1):
                cp.start()
        for cp in scatters(i):
            cp.start()
    for cp in scatters(SC_CHUNKS_PER_WORKER - 1):
        cp.wait()


def _dispatch(dest_chunks, xp3):
    return pl.kernel(
        _dispatch_body,
        out_type=jax.ShapeDtypeStruct((N_SLOTS, ROW_CHUNKS, LANES), U32),
        mesh=_sc_mesh(),
        scratch_types=[pltpu.VMEM((2, TOP_K, SC_CHUNK), I32),
                       pltpu.VMEM((2, SC_CHUNK, ROW_CHUNKS, LANES), U32),
                       pltpu.SemaphoreType.DMA((2,)), pltpu.SemaphoreType.DMA((2,))],
        name="dispatch",
    )(dest_chunks, xp3)


def _gather_body(dest_hbm, ys_hbm, yt_hbm, idx_v, rows_v, sem_in, sem_out):
    chunk0 = _sc_worker_id() * SC_CHUNKS_PER_WORKER

    @pl.loop(0, SC_CHUNKS_PER_WORKER)
    def _(i):
        chunk = chunk0 + i
        t0 = pl.multiple_of(chunk * SC_CHUNK, SC_CHUNK)
        pltpu.sync_copy(dest_hbm.at[chunk], idx_v)

        def gather(k):
            return pltpu.make_async_copy(ys_hbm.at[idx_v.at[k]], rows_v.at[k % SC_RING], sem_in.at[k % SC_RING])

        def store(k):
            return pltpu.make_async_copy(rows_v.at[k % SC_RING], yt_hbm.at[k, pl.ds(t0, SC_CHUNK)],
                                         sem_out.at[k % SC_RING])

        for k in range(SC_RING):
            gather(k).start()
        for k in range(TOP_K):
            gather(k).wait()
            store(k).start()
            if k + SC_RING < TOP_K:
                store(k).wait()
                gather(k + SC_RING).start()
        for k in range(TOP_K - SC_RING, TOP_K):
            store(k).wait()


def _gather(dest_chunks, ys3):
    return pl.kernel(
        _gather_body,
        out_type=jax.ShapeDtypeStruct((TOP_K, T_ALL, ROW_CHUNKS, LANES), U32),
        mesh=_sc_mesh(),
        scratch_types=[pltpu.VMEM((TOP_K, SC_CHUNK), I32),
                       pltpu.VMEM((SC_RING, SC_CHUNK, ROW_CHUNKS, LANES), U32),
                       pltpu.SemaphoreType.DMA((SC_RING,)), pltpu.SemaphoreType.DMA((SC_RING,))],
        name="gather",
    )(dest_chunks, ys3)


def _experts_kernel(blk_e_ref, n_used_ref, xs_ref, wg_ref, wu_ref, wd_ref, ys_ref, wgu_s, wd_s):
    b = pl.program_id(0)

    @pl.when(b < n_used_ref[0])
    def _():
        prev = blk_e_ref[jnp.maximum(b - 1, 0)]

        @pl.when((b == 0) | (blk_e_ref[b] != prev))
        def _():
            wgu_s[:, :D_EXPERT] = wg_ref[0].astype(BF16)
            wgu_s[:, D_EXPERT:] = wu_ref[0].astype(BF16)
            wd_s[...] = wd_ref[0].astype(BF16)

        chunks = _load_chunks(xs_ref, BM_E)
        x_lo = jnp.concatenate([_unpack_lo(p) for p in chunks], axis=-1).astype(BF16)
        x_hi = jnp.concatenate([_unpack_hi(p) for p in chunks], axis=-1).astype(BF16)
        gu = (jnp.dot(x_lo, wgu_s[:HALF, :], preferred_element_type=F32)
              + jnp.dot(x_hi, wgu_s[HALF:, :], preferred_element_type=F32))
        gate, up = gu[:, :D_EXPERT], gu[:, D_EXPERT:]
        hmid = (gate * _sigmoid(gate) * up).astype(BF16)
        _store_chunks(ys_ref, _pack_bf16_pairs(jnp.dot(hmid, wd_s[...], preferred_element_type=F32)))


def _experts(blk_e, n_used, xs, wg, wu, wd):
    def blk(b, be, nu):
        return jnp.minimum(b, nu[0] - 1)

    grid_spec = pltpu.PrefetchScalarGridSpec(
        num_scalar_prefetch=2,
        grid=(N_BLOCKS,),
        in_specs=[pl.BlockSpec((BM_E * ROW_CHUNKS, LANES), lambda b, be, nu: (blk(b, be, nu), 0)),
                  pl.BlockSpec((1, D_MODEL, D_EXPERT), lambda b, be, nu: (be[blk(b, be, nu)], 0, 0)),
                  pl.BlockSpec((1, D_MODEL, D_EXPERT), lambda b, be, nu: (be[blk(b, be, nu)], 0, 0)),
                  pl.BlockSpec((1, D_EXPERT, D_MODEL), lambda b, be, nu: (be[blk(b, be, nu)], 0, 0))],
        out_specs=pl.BlockSpec((BM_E * ROW_CHUNKS, LANES), lambda b, be, nu: (blk(b, be, nu), 0)),
        scratch_shapes=[pltpu.VMEM((D_MODEL, 2 * D_EXPERT), BF16), pltpu.VMEM((D_EXPERT, D_MODEL), BF16)],
    )
    return pl.pallas_call(
        _experts_kernel,
        grid_spec=grid_spec,
        out_shape=jax.ShapeDtypeStruct((N_SLOTS * ROW_CHUNKS, LANES), U32),
        compiler_params=_cparams(("arbitrary",)),
        name="experts",
    )(blk_e, n_used, xs, wg, wu, wd)


def _combine_kernel(yt_ref, wts_ref, hsh_ref, p_ref, gple_ref, wpg_ref, wpp_ref, gfin_ref, y_ref):
    wts = wts_ref[...]
    lo = [jnp.zeros((BT_MOVE, LANES), F32) for _ in range(ROW_CHUNKS)]
    hi = [jnp.zeros((BT_MOVE, LANES), F32) for _ in range(ROW_CHUNKS)]
    for k in range(TOP_K):
        w = wts[:, k:k + 1]
        for c, p in enumerate(_load_chunks(yt_ref, BT_MOVE, lead=(k,))):
            lo[c] = lo[c] + w * _unpack_lo(p)
            hi[c] = hi[c] + w * _unpack_hi(p)
    h2 = hsh_ref[...] + jnp.concatenate(lo + hi, axis=-1)
    gate = _sigmoid(jnp.dot(_rms(h2, gple_ref[...]).astype(BF16), wpg_ref[...], preferred_element_type=F32))
    proj = jnp.dot(p_ref[...].astype(BF16), wpp_ref[...], preferred_element_type=F32)
    y_ref[...] = _rms(h2 + proj * gate, gfin_ref[...])


def _combine(yt, wts_tok, hsh, p2d, g_ple, wpg, wpp, g_final, row0):
    rows = p2d.shape[0]
    blk0 = row0 // BT_MOVE
    ws = [g_ple, wpg, wpp, g_final]
    return pl.pallas_call(
        _combine_kernel,
        grid=(rows // BT_MOVE,),
        in_specs=[pl.BlockSpec((TOP_K, BT_MOVE * ROW_CHUNKS, LANES), lambda i: (0, blk0 + i, 0)),
                  pl.BlockSpec((BT_MOVE, TOP_K), lambda i: (blk0 + i, 0)),
                  pl.BlockSpec((BT_MOVE, D_MODEL), lambda i: (blk0 + i, 0)),
                  pl.BlockSpec((BT_MOVE, PLE_DIM), lambda i: (i, 0))]
                 + [_full_spec(w, 1) for w in ws],
        out_specs=pl.BlockSpec((BT_MOVE, D_MODEL), lambda i: (i, 0)),
        out_shape=jax.ShapeDtypeStruct((rows, D_MODEL), F32),
        compiler_params=_cparams(("parallel",)),
        name="combine",
    )(yt, wts_tok, hsh, p2d, *ws)


def kernel(x_prompt, x_sample, cache_k, cache_v, state_pool, p_prompt, p_sample, g_mix, w_in, attn_sinks,
           w_pool, pool_scale, g_att_out, g_pool_out, w_out, g_ffn, w_router, router_bias, w_exp_gate,
           w_exp_up, w_exp_down, w_sh_gate, w_sh_up, w_sh_down, g_ple, w_ple_gate, w_ple_proj, g_final):
    row = lambda a: a.reshape(1, -1)
    xp2d = x_prompt.reshape(T_P, D_MODEL)
    xs2d = x_sample.reshape(T_S, D_MODEL)
    w_in_bf = w_in[0].astype(BF16)
    mixer_wts = [w_pool[0].astype(BF16), row(pool_scale[0]), row(g_att_out[0]), row(g_pool_out[0]),
                 w_out[0].astype(BF16)]

    tab_p = _rope_tables(jnp.arange(SEQ, dtype=I32))
    pos_s = PAST_LEN + jnp.arange(DEC_SEQ, dtype=I32)
    tab_s = tuple(jnp.tile(t, (BM_IN // DEC_SEQ, 1)) for t in _rope_tables(pos_s))

    q_p, k_p, v_p, u_p = _inproj(xp2d, row(g_mix[0]), w_in_bf, tab_p, BF16)
    q_s, k_s, v_s, u_s = _inproj(xs2d, row(g_mix[0]), w_in_bf, tab_s, F32)

    h1 = _mixer_prompt(attn_sinks[0], xp2d, q_p, k_p, v_p, u_p, mixer_wts)
    h1, k_sample, v_sample, pool_sample = _mixer_sample(
        attn_sinks[0], xs2d, q_s, k_s, v_s, u_s,
        cache_k[0].reshape(DEC_BATCH, WINDOW, KV_WIDTH), cache_v[0].reshape(DEC_BATCH, WINDOW, KV_WIDTH),
        state_pool[0], mixer_wts, h1)

    xp, hsh, idx_t, wts_t = _router(
        h1, row(g_ffn[0]), w_router[0].T, router_bias[0].reshape(N_EXPERTS, 1),
        w_sh_gate[0].astype(BF16), w_sh_up[0].astype(BF16), w_sh_down[0].astype(BF16))

    tri = (lax.broadcasted_iota(I32, (BT_RANK, BT_RANK), 0)
           < lax.broadcasted_iota(I32, (BT_RANK, BT_RANK), 1)).astype(BF16)
    rank_t, counts = _rank(idx_t, tri)
    dest_t, blk_e, n_used = _dest(idx_t, rank_t, counts)

    dest_chunks = dest_t.reshape(TOP_K, T_ALL // SC_CHUNK, SC_CHUNK).transpose(1, 0, 2)
    xs = _dispatch(dest_chunks, xp.reshape(T_ALL, ROW_CHUNKS, LANES))
    ys = _experts(blk_e.reshape(N_BLOCKS_PAD), n_used.reshape(1), xs.reshape(N_SLOTS * ROW_CHUNKS, LANES),
                  w_exp_gate[0], w_exp_up[0], w_exp_down[0])
    yt = _gather(dest_chunks, ys.reshape(N_SLOTS, ROW_CHUNKS, LANES))
    yt = yt.reshape(TOP_K, T_ALL * ROW_CHUNKS, LANES)

    wts_tok = wts_t.T
    ple_wts = (row(g_ple[0]), w_ple_gate[0].astype(BF16), w_ple_proj[0].astype(BF16), row(g_final))
    y_p = _combine(yt, wts_tok, hsh, p_prompt[0].reshape(T_P, PLE_DIM), *ple_wts, 0)
    y_s = _combine(yt, wts_tok, hsh, p_sample[0].reshape(T_S, PLE_DIM), *ple_wts, T_P)

    kv5 = lambda a, b: a.reshape(1, b, WINDOW, N_KV_HEADS, HEAD_DIM)
    k_prompt = kv5(k_p.reshape(BATCH, SEQ, KV_WIDTH)[:, SEQ - WINDOW:], BATCH)
    v_prompt = kv5(v_p.reshape(BATCH, SEQ, KV_WIDTH)[:, SEQ - WINDOW:], BATCH)
    pool_prompt = u_p.reshape(BATCH, SEQ, POOL_WIDTH)[:, SEQ - POOL_STATE:][None]
    return (y_p.reshape(BATCH, SEQ, D_MODEL), y_s.reshape(DEC_BATCH, DEC_SEQ, D_MODEL),
            k_prompt, v_prompt, pool_prompt,
            kv5(k_sample, DEC_BATCH), kv5(v_sample, DEC_BATCH), pool_sample[None])
```

```python
import functools

import jax
import jax.numpy as jnp
from jax import lax
from jax.experimental import pallas as pl
from jax.experimental.pallas import tpu as pltpu
from jax.experimental.pallas import tpu_sc as plsc

F32 = jnp.float32
BF16 = jnp.bfloat16
U32 = jnp.uint32
I32 = jnp.int32

D_MODEL = 1024
BATCH = 8
SEQ = 2048
DEC_BATCH = 128
DEC_SEQ = 8
PAST_LEN = 16384
N_Q_HEADS = 8
N_KV_HEADS = 2
HEAD_DIM = 64
GQA_GROUP = N_Q_HEADS // N_KV_HEADS
ATT_WIDTH = N_Q_HEADS * HEAD_DIM
KV_WIDTH = N_KV_HEADS * HEAD_DIM
WINDOW = 128
ROPE_THETA = 500000.0
ROT_DIM = HEAD_DIM // 4
POOL_WINDOWS = (2, 4, 8, 16)
POOL_GROUPS = 4
POOL_WIDTH = D_MODEL - ATT_WIDTH
POOL_GROUP_DIM = POOL_WIDTH // POOL_GROUPS
POOL_STATE = 15
IN_WIDTH = ATT_WIDTH + 2 * KV_WIDTH + POOL_WIDTH
N_EXPERTS = 64
TOP_K = 8
N_EXPERT_GROUPS = 8
GROUP_SIZE = N_EXPERTS // N_EXPERT_GROUPS
TOPK_GROUPS = 4
D_EXPERT = 256
D_SHARED = 256
ROUTED_SCALE = 2.5
PLE_DIM = 256
EPS = 1e-6

T_P = BATCH * SEQ
T_S = DEC_BATCH * DEC_SEQ
T_ALL = T_P + T_S
HALF = D_MODEL // 2
LANES = 128
VMEM_LIMIT = 48 * 1024 * 1024

BM_IN = 256
BQ = 2 * WINDOW
SB = 16
BM_R = 256
BT_RANK = 512
BT_COMB = 256
BM_E = 512
N_ASSIGN = T_ALL * TOP_K
N_BLOCKS = N_ASSIGN // BM_E + N_EXPERTS
N_SLOTS = N_BLOCKS * BM_E

ROW_CHUNKS = HALF // LANES
SC_CORES = 2
SC_SUBCORES = 16
SC_WORKERS = SC_CORES * SC_SUBCORES
SC_CHUNK = 32
SC_CHUNKS_PER_WORKER = T_ALL // (SC_WORKERS * SC_CHUNK)
SC_RING = 4
assert SC_CHUNKS_PER_WORKER * SC_WORKERS * SC_CHUNK == T_ALL


def _load_chunks(ref, n_rows, lead=()):
    return [ref[lead + (pl.ds(c, n_rows, stride=ROW_CHUNKS), slice(None))] for c in range(ROW_CHUNKS)]


def _store_chunks(ref, packed):
    n_rows = packed.shape[0]
    for c in range(ROW_CHUNKS):
        ref[pl.ds(c, n_rows, stride=ROW_CHUNKS), :] = packed[:, c * LANES:(c + 1) * LANES]


def _cparams(sem):
    return pltpu.CompilerParams(dimension_semantics=sem, vmem_limit_bytes=VMEM_LIMIT)


def _rms(x, g):
    return x * lax.rsqrt(jnp.mean(x * x, axis=-1, keepdims=True) + EPS) * g


def _sigmoid(x):
    return 1.0 / (1.0 + jnp.exp(-x))


def _pack_bf16_pairs(x):
    h = x.shape[-1] // 2
    return pltpu.pack_elementwise([x[:, :h], x[:, h:]], packed_dtype=BF16)


def _unpack_lo(p):
    return pltpu.bitcast(p << 16, F32)


def _unpack_hi(p):
    return pltpu.bitcast(p & jnp.uint32(0xFFFF0000), F32)


def _inproj_kernel(x_ref, g_ref, w_ref, c_ref, s1_ref, s2_ref, q_ref, k_ref, v_ref, u_ref):
    xn = _rms(x_ref[...], g_ref[...]).astype(BF16)
    z = jnp.dot(xn, w_ref[...], preferred_element_type=F32)
    c, s1, s2 = c_ref[...], s1_ref[...], s2_ref[...]

    def rope(t):
        return t * c + pltpu.roll(t, LANES - ROT_DIM // 2, 1) * s1 + pltpu.roll(t, ROT_DIM // 2, 1) * s2

    for i in range(ATT_WIDTH // LANES):
        sl = slice(i * LANES, (i + 1) * LANES)
        q_ref[:, sl] = (rope(z[:, sl]) * (HEAD_DIM ** -0.5)).astype(q_ref.dtype)
    k_ref[...] = rope(z[:, ATT_WIDTH:ATT_WIDTH + KV_WIDTH])
    v_ref[...] = z[:, ATT_WIDTH + KV_WIDTH:ATT_WIDTH + 2 * KV_WIDTH]
    u_ref[...] = z[:, ATT_WIDTH + 2 * KV_WIDTH:]


def _rope_tables(pos):
    half = ROT_DIM // 2
    inv = ROPE_THETA ** (-jnp.arange(half, dtype=F32) * 2.0 / ROT_DIM)
    ang = pos.astype(F32)[:, None] * inv[None, :]
    cos, sin = jnp.cos(ang), jnp.sin(ang)
    n = pos.shape[0]
    ones = jnp.ones((n, HEAD_DIM - ROT_DIM), F32)
    zeros = jnp.zeros((n, HEAD_DIM - ROT_DIM), F32)
    zh = jnp.zeros((n, half), F32)
    c = jnp.concatenate([cos, cos, ones], axis=1)
    s1 = jnp.concatenate([-sin, zh, zeros], axis=1)
    s2 = jnp.concatenate([zh, sin, zeros], axis=1)
    tile = lambda a: jnp.concatenate([a] * (LANES // HEAD_DIM), axis=1)
    return tile(c), tile(s1), tile(s2)


def _inproj(x2d, g_mix, w_in_bf, tables, q_dtype):
    rows = x2d.shape[0]
    n_tab = tables[0].shape[0] // BM_IN
    row_spec = lambda w: pl.BlockSpec((BM_IN, w), lambda i: (i, 0))
    tab_spec = pl.BlockSpec((BM_IN, LANES), lambda i: (i % n_tab, 0))
    full = lambda a: pl.BlockSpec(a.shape, lambda i: (0,) * a.ndim)
    return pl.pallas_call(
        _inproj_kernel,
        grid=(rows // BM_IN,),
        in_specs=[row_spec(D_MODEL), full(g_mix), full(w_in_bf), tab_spec, tab_spec, tab_spec],
        out_specs=[row_spec(ATT_WIDTH), row_spec(KV_WIDTH), row_spec(KV_WIDTH), row_spec(POOL_WIDTH)],
        out_shape=[jax.ShapeDtypeStruct((rows, ATT_WIDTH), q_dtype),
                   jax.ShapeDtypeStruct((rows, KV_WIDTH), F32),
                   jax.ShapeDtypeStruct((rows, KV_WIDTH), F32),
                   jax.ShapeDtypeStruct((rows, POOL_WIDTH), F32)],
        compiler_params=_cparams(("parallel",)),
        name="inproj",
    )(x2d, g_mix, w_in_bf, *tables)


def _sink_column(sinks_ref, kv_head, rows_per_head):
    n = GQA_GROUP * rows_per_head
    grp = lax.broadcasted_iota(I32, (n, 1), 0) // rows_per_head
    col = jnp.full((n, 1), sinks_ref[kv_head * GQA_GROUP], F32)
    for g in range(1, GQA_GROUP):
        col = jnp.where(grp == g, sinks_ref[kv_head * GQA_GROUP + g], col)
    return col


def _band_mask(n_rows, rows_per_head, n_keys):
    i = lax.broadcasted_iota(I32, (n_rows, n_keys), 0) % rows_per_head
    c = lax.broadcasted_iota(I32, (n_rows, n_keys), 1)
    return (c >= i) & (c <= i + WINDOW), c


def _stack_heads(q, kv_head):
    return jnp.concatenate(
        [q[:, (kv_head * GQA_GROUP + g) * HEAD_DIM:(kv_head * GQA_GROUP + g + 1) * HEAD_DIM]
         for g in range(GQA_GROUP)], axis=0)


def _nt_dot(a, b):
    return lax.dot_general(a, b, (((1,), (1,)), ((), ())), preferred_element_type=F32)


def _pool_delta(u, uext_ref, base, n, cnt_fn):
    parts = []
    for g, w in enumerate(POOL_WINDOWS):
        sl = slice(g * POOL_GROUP_DIM, (g + 1) * POOL_GROUP_DIM)
        acc = u[:, sl]
        for m in range(1, w):
            acc = acc + uext_ref[base - m:base - m + n, sl]
        parts.append(acc / cnt_fn(w) - u[:, sl])
    return parts


def _mixer_tail(o_att, d, h, wpool_ref, pscale_ref, gatt_ref, gpool_ref, wout_ref):
    parts = [jnp.dot(d[:, g * POOL_GROUP_DIM:(g + 1) * POOL_GROUP_DIM].astype(BF16), wpool_ref[g],
                     preferred_element_type=F32) for g in range(POOL_GROUPS)]
    o_pool = jnp.concatenate(parts, axis=-1) * pscale_ref[...]
    mixed = jnp.concatenate([_rms(o_att, gatt_ref[...]), _rms(o_pool, gpool_ref[...])], axis=-1)
    return h + jnp.dot(mixed.astype(BF16), wout_ref[...], preferred_element_type=F32)


def _mixer_prompt_kernel(sinks_ref, h_ref, q_ref, kc_ref, kp_ref, vc_ref, vp_ref, uc_ref, up_ref,
                         wpool_ref, pscale_ref, gatt_ref, gpool_ref, wout_ref, h1_ref, uext_ref):
    j = pl.program_id(1)
    q = q_ref[...]
    k_all = jnp.concatenate([kp_ref[...], kc_ref[...]], axis=0).astype(BF16)
    v_all = jnp.concatenate([vp_ref[...], vc_ref[...]], axis=0).astype(BF16)
    band, col = _band_mask(GQA_GROUP * WINDOW, WINDOW, 2 * WINDOW)
    sinks = [_sink_column(sinks_ref, hk, WINDOW) for hk in range(N_KV_HEADS)]
    bands = []
    for b in range(BQ // WINDOW):
        rows = slice(b * WINDOW, (b + 1) * WINDOW)
        keys = slice(b * WINDOW, (b + 2) * WINDOW)
        mask = band & ((col >= WINDOW) | (j > 0)) if b == 0 else band
        heads = []
        for hk in range(N_KV_HEADS):
            sl = slice(hk * HEAD_DIM, (hk + 1) * HEAD_DIM)
            s = jnp.where(mask, _nt_dot(_stack_heads(q[rows], hk), k_all[keys, sl]), -jnp.inf)
            m = jnp.maximum(jnp.max(s, axis=-1, keepdims=True), sinks[hk])
            e = jnp.exp(s - m)
            den = jnp.sum(e, axis=-1, keepdims=True) + jnp.exp(sinks[hk] - m)
            o = jnp.dot(e.astype(BF16), v_all[keys, sl], preferred_element_type=F32) / den
            heads += [o[g * WINDOW:(g + 1) * WINDOW] for g in range(GQA_GROUP)]
        bands.append(jnp.concatenate(heads, axis=-1))
    o_att = jnp.concatenate(bands, axis=0)

    u = uc_ref[...]
    uext_ref[0:16, :] = jnp.where(j > 0, up_ref[...], 0.0)
    uext_ref[16:16 + BQ, :] = u
    pos = j * BQ + lax.broadcasted_iota(I32, (BQ, 1), 0)
    d = jnp.concatenate(
        _pool_delta(u, uext_ref, 16, BQ, lambda w: jnp.minimum(pos + 1, w).astype(F32)), axis=-1)
    h1_ref[...] = _mixer_tail(o_att, d, h_ref[...], wpool_ref, pscale_ref, gatt_ref, gpool_ref, wout_ref)


def _mixer_sample_kernel(sinks_ref, h_ref, q_ref, kn_ref, vn_ref, u_ref, ck_ref, cv_ref, st_ref,
                         wpool_ref, pscale_ref, gatt_ref, gpool_ref, wout_ref, h1_in_ref,
                         h1_ref, ko_ref, vo_ref, po_ref, oatt_ref, d_ref, uext_ref):
    del h1_in_ref
    n_q = GQA_GROUP * DEC_SEQ
    mask_c, _ = _band_mask(n_q, DEC_SEQ, WINDOW)
    qi = lax.broadcasted_iota(I32, (n_q, DEC_SEQ), 0) % DEC_SEQ
    mask_n = lax.broadcasted_iota(I32, (n_q, DEC_SEQ), 1) <= qi
    sinks = [_sink_column(sinks_ref, hk, DEC_SEQ) for hk in range(N_KV_HEADS)]

    def one_batch(b, carry):
        r0 = pl.multiple_of(b * DEC_SEQ, DEC_SEQ)
        rows = pl.ds(r0, DEC_SEQ)
        qb = q_ref[rows, :]
        ck, cv = ck_ref[b], cv_ref[b]
        kn, vn = kn_ref[rows, :], vn_ref[rows, :]
        ko_ref[b, 0:WINDOW - DEC_SEQ, :] = ck[DEC_SEQ:, :]
        ko_ref[b, WINDOW - DEC_SEQ:WINDOW, :] = kn
        vo_ref[b, 0:WINDOW - DEC_SEQ, :] = cv[DEC_SEQ:, :]
        vo_ref[b, WINDOW - DEC_SEQ:WINDOW, :] = vn
        for hk in range(N_KV_HEADS):
            sl = slice(hk * HEAD_DIM, (hk + 1) * HEAD_DIM)
            qs = _stack_heads(qb, hk)
            s_c = jnp.where(mask_c, _nt_dot(qs.astype(BF16), ck[:, sl].astype(BF16)), -jnp.inf)
            s_n = jnp.where(mask_n, _nt_dot(qs, kn[:, sl]), -jnp.inf)
            m = jnp.maximum(jnp.maximum(jnp.max(s_c, axis=-1, keepdims=True),
                                        jnp.max(s_n, axis=-1, keepdims=True)), sinks[hk])
            e_c, e_n = jnp.exp(s_c - m), jnp.exp(s_n - m)
            den = (jnp.sum(e_c, axis=-1, keepdims=True) + jnp.sum(e_n, axis=-1, keepdims=True)
                   + jnp.exp(sinks[hk] - m))
            o = (jnp.dot(e_c.astype(BF16), cv[:, sl].astype(BF16), preferred_element_type=F32)
                 + jnp.dot(e_n, vn[:, sl], preferred_element_type=F32)) / den
            for g in range(GQA_GROUP):
                h0 = (hk * GQA_GROUP + g) * HEAD_DIM
                oatt_ref[rows, h0:h0 + HEAD_DIM] = o[g * DEC_SEQ:(g + 1) * DEC_SEQ]
        ub = u_ref[rows, :]
        uext_ref[1:16, :] = st_ref[b]
        uext_ref[16:16 + DEC_SEQ, :] = ub
        parts = _pool_delta(ub, uext_ref, 16, DEC_SEQ, lambda w: float(w))
        for g in range(POOL_GROUPS):
            d_ref[rows, g * POOL_GROUP_DIM:(g + 1) * POOL_GROUP_DIM] = parts[g]
        po_ref[b] = uext_ref[16 + DEC_SEQ - POOL_STATE:16 + DEC_SEQ, :]
        return carry

    lax.fori_loop(0, SB, one_batch, 0)
    h1_ref[...] = _mixer_tail(oatt_ref[...], d_ref[...], h_ref[...], wpool_ref, pscale_ref, gatt_ref,
                              gpool_ref, wout_ref)


def _full_spec(a, n_grid):
    nd = a.ndim
    return pl.BlockSpec(a.shape, lambda *_: (0,) * nd)


def _mixer_prompt(sinks, x2d, q, k, v, u, wts):
    nb = SEQ // BQ
    row = lambda w: pl.BlockSpec((BQ, w), lambda b, j: (b * nb + j, 0))
    prev = lambda w: pl.BlockSpec(
        (WINDOW, w), lambda b, j: (jnp.maximum((b * nb + j) * (BQ // WINDOW) - 1, 0), 0))
    uprev = pl.BlockSpec((16, POOL_WIDTH), lambda b, j: (jnp.maximum((b * nb + j) * (BQ // 16) - 1, 0), 0))
    smem = pl.BlockSpec(memory_space=pltpu.SMEM)
    return pl.pallas_call(
        _mixer_prompt_kernel,
        grid=(BATCH, nb),
        in_specs=[smem, row(D_MODEL), row(ATT_WIDTH), row(KV_WIDTH), prev(KV_WIDTH), row(KV_WIDTH),
                  prev(KV_WIDTH), row(POOL_WIDTH), uprev] + [_full_spec(w, 2) for w in wts],
        out_specs=row(D_MODEL),
        out_shape=jax.ShapeDtypeStruct((T_ALL, D_MODEL), F32),
        scratch_shapes=[pltpu.VMEM((16 + BQ, POOL_WIDTH), F32)],
        compiler_params=_cparams(("parallel", "parallel")),
        name="mixer_prompt",
    )(sinks, x2d, q, k, k, v, v, u, u, *wts)


def _mixer_sample(sinks, x2d, q, k, v, u, cache_k, cache_v, state, wts, h1_buf):
    rows = SB * DEC_SEQ
    row = lambda w: pl.BlockSpec((rows, w), lambda i: (i, 0))
    bat = lambda a: pl.BlockSpec((SB,) + a.shape[1:], lambda i: (i, 0, 0))
    smem = pl.BlockSpec(memory_space=pltpu.SMEM)
    h1_blocks_before = T_P // rows
    n_in = 9 + len(wts)
    return pl.pallas_call(
        _mixer_sample_kernel,
        grid=(DEC_BATCH // SB,),
        in_specs=[smem, row(D_MODEL), row(ATT_WIDTH), row(KV_WIDTH), row(KV_WIDTH), row(POOL_WIDTH),
                  bat(cache_k), bat(cache_v), bat(state)] + [_full_spec(w, 1) for w in wts]
                 + [pl.BlockSpec(memory_space=pl.ANY)],
        out_specs=[pl.BlockSpec((rows, D_MODEL), lambda i: (h1_blocks_before + i, 0)),
                   bat(cache_k), bat(cache_v), bat(state)],
        out_shape=[jax.ShapeDtypeStruct((T_ALL, D_MODEL), F32),
                   jax.ShapeDtypeStruct(cache_k.shape, F32),
                   jax.ShapeDtypeStruct(cache_v.shape, F32),
                   jax.ShapeDtypeStruct(state.shape, F32)],
        scratch_shapes=[pltpu.VMEM((rows, ATT_WIDTH), F32), pltpu.VMEM((rows, POOL_WIDTH), F32),
                        pltpu.VMEM((16 + DEC_SEQ, POOL_WIDTH), F32)],
        input_output_aliases={n_in: 0},
        compiler_params=_cparams(("parallel",)),
        name="mixer_sample",
    )(sinks, x2d, q, k, v, u, cache_k, cache_v, state, *wts, h1_buf)


def _first_max(vals, iota, n):
    m = jnp.max(vals, axis=0, keepdims=True)
    idx = jnp.min(jnp.where(vals == m, iota, n), axis=0, keepdims=True)
    return m, idx


def _router_kernel(h1_ref, gffn_ref, wrt_ref, bias_ref, wsg_ref, wsu_ref, wsd_ref,
                   xp_ref, hsh_ref, idx_ref, wts_ref):
    h1 = h1_ref[...]
    xn = _rms(h1, gffn_ref[...])
    logits = lax.dot_general(wrt_ref[...], xn, (((1,), (1,)), ((), ())),
                             precision=lax.Precision.HIGHEST, preferred_element_type=F32)
    scores = _sigmoid(logits)
    biased = scores + bias_ref[...]
    n_tok = biased.shape[1]
    neg = -jnp.inf

    iota_g = lax.broadcasted_iota(I32, (GROUP_SIZE, n_tok), 0)
    grp_rows = []
    for g in range(N_EXPERT_GROUPS):
        blk = biased[g * GROUP_SIZE:(g + 1) * GROUP_SIZE, :]
        top1, i1 = _first_max(blk, iota_g, GROUP_SIZE)
        top2 = jnp.max(jnp.where(iota_g == i1, neg, blk), axis=0, keepdims=True)
        grp_rows.append(top1 + top2)
    gs = jnp.concatenate(grp_rows, axis=0)

    iota_n = lax.broadcasted_iota(I32, (N_EXPERT_GROUPS, n_tok), 0)
    gsel = jnp.zeros((N_EXPERT_GROUPS, n_tok), jnp.bool_)
    for _ in range(TOPK_GROUPS):
        _, gi = _first_max(gs, iota_n, N_EXPERT_GROUPS)
        hit = iota_n == gi
        gsel = gsel | hit
        gs = jnp.where(hit, neg, gs)
    emask = jnp.concatenate(
        [jnp.broadcast_to(gsel[g:g + 1, :], (GROUP_SIZE, n_tok)) for g in range(N_EXPERT_GROUPS)], axis=0)
    masked = jnp.where(emask, biased, neg)

    iota_e = lax.broadcasted_iota(I32, (N_EXPERTS, n_tok), 0)
    idx_rows, sel_rows = [], []
    for _ in range(TOP_K):
        _, ei = _first_max(masked, iota_e, N_EXPERTS)
        hit = iota_e == ei
        idx_rows.append(ei)
        sel_rows.append(jnp.sum(jnp.where(hit, scores, 0.0), axis=0, keepdims=True))
        masked = jnp.where(hit, neg, masked)
    sel = jnp.concatenate(sel_rows, axis=0)
    idx_ref[...] = jnp.concatenate(idx_rows, axis=0)
    wts_ref[...] = sel / jnp.sum(sel, axis=0, keepdims=True) * ROUTED_SCALE

    xb = xn.astype(BF16)
    gate = jnp.dot(xb, wsg_ref[...], preferred_element_type=F32)
    up = jnp.dot(xb, wsu_ref[...], preferred_element_type=F32)
    hmid = (gate * _sigmoid(gate) * up).astype(BF16)
    hsh_ref[...] = h1 + jnp.dot(hmid, wsd_ref[...], preferred_element_type=F32)
    _store_chunks(xp_ref, _pack_bf16_pairs(xn))


def _router(h1, g_ffn, w_router_t, bias_col, wsg, wsu, wsd):
    row = lambda w: pl.BlockSpec((BM_R, w), lambda i: (i, 0))
    colblk = pl.BlockSpec((TOP_K, BM_R), lambda i: (0, i))
    ws = [g_ffn, w_router_t, bias_col, wsg, wsu, wsd]
    return pl.pallas_call(
        _router_kernel,
        grid=(T_ALL // BM_R,),
        in_specs=[row(D_MODEL)] + [_full_spec(w, 1) for w in ws],
        out_specs=[pl.BlockSpec((BM_R * ROW_CHUNKS, LANES), lambda i: (i, 0)), row(D_MODEL), colblk, colblk],
        out_shape=[jax.ShapeDtypeStruct((T_ALL * ROW_CHUNKS, LANES), U32),
                   jax.ShapeDtypeStruct((T_ALL, D_MODEL), F32),
                   jax.ShapeDtypeStruct((TOP_K, T_ALL), I32),
                   jax.ShapeDtypeStruct((TOP_K, T_ALL), F32)],
        compiler_params=_cparams(("parallel",)),
        name="router",
    )(h1, *ws)


def _rank_kernel(idx_ref, tri_ref, rank_ref, cnt_ref, carry_ref):
    @pl.when(pl.program_id(0) == 0)
    def _():
        carry_ref[...] = jnp.zeros_like(carry_ref)

    idx = idx_ref[...]
    n_tok = idx.shape[1]
    iota_e = lax.broadcasted_iota(I32, (N_EXPERTS, n_tok), 0)
    member = jnp.zeros((N_EXPERTS, n_tok), F32)
    for k in range(TOP_K):
        member = member + jnp.where(iota_e == idx[k:k + 1, :], 1.0, 0.0)
    before = jnp.dot(member.astype(BF16), tri_ref[...], preferred_element_type=F32) + carry_ref[...]
    rows = [jnp.sum(jnp.where(iota_e == idx[k:k + 1, :], before, 0.0), axis=0, keepdims=True)
            for k in range(TOP_K)]
    rank_ref[...] = jnp.concatenate(rows, axis=0).astype(I32)
    carry_ref[...] = carry_ref[...] + jnp.sum(member, axis=1, keepdims=True)
    cnt_ref[...] = carry_ref[...].astype(I32)


def _rank(idx_t, tri):
    blk = pl.BlockSpec((TOP_K, BT_RANK), lambda i: (0, i))
    return pl.pallas_call(
        _rank_kernel,
        grid=(T_ALL // BT_RANK,),
        in_specs=[blk, _full_spec(tri, 1)],
        out_specs=[blk, pl.BlockSpec((N_EXPERTS, 1), lambda i: (0, 0))],
        out_shape=[jax.ShapeDtypeStruct((TOP_K, T_ALL), I32),
                   jax.ShapeDtypeStruct((N_EXPERTS, 1), I32)],
        scratch_shapes=[pltpu.VMEM((N_EXPERTS, 1), F32)],
        compiler_params=_cparams(("arbitrary",)),
        name="rank",
    )(idx_t, tri)


def _dest_kernel(idx_ref, rank_ref, cnt_ref, dest_ref, blk_e_ref, n_used_ref):
    counts = cnt_ref[...]
    padded = (counts + (BM_E - 1)) // BM_E * BM_E
    r = lax.broadcasted_iota(I32, (N_EXPERTS, N_EXPERTS), 0)
    c = lax.broadcasted_iota(I32, (N_EXPERTS, N_EXPERTS), 1)
    padded_row = jnp.sum(jnp.where(r == c, padded, 0), axis=0, keepdims=True)
    pad_start = jnp.sum(jnp.where(c < r, padded_row, 0), axis=1, keepdims=True)
    pad_end_row = jnp.sum(jnp.where(r <= c, padded, 0), axis=0, keepdims=True)

    idx = idx_ref[...]
    n_tok = idx.shape[1]
    iota_e = lax.broadcasted_iota(I32, (N_EXPERTS, n_tok), 0)
    rows = [jnp.sum(jnp.where(iota_e == idx[k:k + 1, :], pad_start, 0), axis=0, keepdims=True)
            for k in range(TOP_K)]
    dest_ref[...] = jnp.concatenate(rows, axis=0) + rank_ref[...]

    b0 = lax.broadcasted_iota(I32, (N_BLOCKS_PAD, N_EXPERTS), 0) * BM_E
    be = jnp.sum(jnp.where(pad_end_row <= b0, 1, 0), axis=1, keepdims=True)
    blk_e_ref[...] = jnp.minimum(be, N_EXPERTS - 1)
    n_used_ref[...] = pad_end_row[:, N_EXPERTS - 1:N_EXPERTS] // BM_E


N_BLOCKS_PAD = (N_BLOCKS + 7) // 8 * 8


def _dest(idx_t, rank_t, counts):
    blk = pl.BlockSpec((TOP_K, BT_RANK), lambda i: (0, i))
    one = lambda s: pl.BlockSpec(s, lambda i: (0, 0))
    return pl.pallas_call(
        _dest_kernel,
        grid=(T_ALL // BT_RANK,),
        in_specs=[blk, blk, one((N_EXPERTS, 1))],
        out_specs=[blk, one((N_BLOCKS_PAD, 1)), one((1, 1))],
        out_shape=[jax.ShapeDtypeStruct((TOP_K, T_ALL), I32),
                   jax.ShapeDtypeStruct((N_BLOCKS_PAD, 1), I32),
                   jax.ShapeDtypeStruct((1, 1), I32)],
        compiler_params=_cparams(("arbitrary",)),
        name="dest",
    )(idx_t, rank_t, counts)


def _sc_mesh():
    return plsc.VectorSubcoreMesh(core_axis_name="c", subcore_axis_name="s")


def _sc_worker_id():
    return lax.axis_index("s") * SC_CORES + lax.axis_index("c")


def _dispatch_body(dest_hbm, xp_hbm, xs_hbm, idx_v, rows_v, sem_in, sem_out):
    chunk0 = _sc_worker_id() * SC_CHUNKS_PER_WORKER

    def loads(i):
        chunk = chunk0 + i
        t0 = pl.multiple_of(chunk * SC_CHUNK, SC_CHUNK)
        return (pltpu.make_async_copy(dest_hbm.at[chunk], idx_v.at[i % 2], sem_in.at[i % 2]),
                pltpu.make_async_copy(xp_hbm.at[pl.ds(t0, SC_CHUNK)], rows_v.at[i % 2], sem_in.at[i % 2]))

    def scatters(i):
        return [pltpu.make_async_copy(rows_v.at[i % 2], xs_hbm.at[idx_v.at[i % 2, k]], sem_out.at[i % 2])
                for k in range(TOP_K)]

    for cp in loads(0):
        cp.start()
    for i in range(SC_CHUNKS_PER_WORKER):
        for cp in loads(i):
            cp.wait()
        if i >= 1:
            for cp in scatters(i - 1):
                cp.wait()
        if i + 1 < SC_CHUNKS_PER_WORKER:
            for cp in loads(i + 1):
                cp.start()
        for cp in scatters(i):
            cp.start()
    for cp in scatters(SC_CHUNKS_PER_WORKER - 1):
        cp.wait()


def _dispatch(dest_chunks, xp3):
    return pl.kernel(
        _dispatch_body,
        out_type=jax.ShapeDtypeStruct((N_SLOTS, ROW_CHUNKS, LANES), U32),
        mesh=_sc_mesh(),
        scratch_types=[pltpu.VMEM((2, TOP_K, SC_CHUNK), I32),
                       pltpu.VMEM((2, SC_CHUNK, ROW_CHUNKS, LANES), U32),
                       pltpu.SemaphoreType.DMA((2,)), pltpu.SemaphoreType.DMA((2,))],
        name="dispatch",
    )(dest_chunks, xp3)


def _gather_body(dest_hbm, ys_hbm, yt_hbm, idx_v, rows_v, sem_in, sem_out):
    chunk0 = _sc_worker_id() * SC_CHUNKS_PER_WORKER

    @pl.loop(0, SC_CHUNKS_PER_WORKER)
    def _(i):
        chunk = chunk0 + i
        t0 = pl.multiple_of(chunk * SC_CHUNK, SC_CHUNK)
        pltpu.sync_copy(dest_hbm.at[chunk], idx_v)

        def gather(k):
            return pltpu.make_async_copy(ys_hbm.at[idx_v.at[k]], rows_v.at[k % SC_RING], sem_in.at[k % SC_RING])

        def store(k):
            return pltpu.make_async_copy(rows_v.at[k % SC_RING], yt_hbm.at[k, pl.ds(t0, SC_CHUNK)],
                                         sem_out.at[k % SC_RING])

        for k in range(SC_RING):
            gather(k).start()
        for k in range(TOP_K):
            gather(k).wait()
            store(k).start()
            if k + SC_RING < TOP_K:
                store(k).wait()
                gather(k + SC_RING).start()
        for k in range(TOP_K - SC_RING, TOP_K):
            store(k).wait()


def _gather(dest_chunks, ys3):
    return pl.kernel(
        _gather_body,
        out_type=jax.ShapeDtypeStruct((TOP_K, T_ALL, ROW_CHUNKS, LANES), U32),
        mesh=_sc_mesh(),
        scratch_types=[pltpu.VMEM((TOP_K, SC_CHUNK), I32),
                       pltpu.VMEM((SC_RING, SC_CHUNK, ROW_CHUNKS, LANES), U32),
                       pltpu.SemaphoreType.DMA((SC_RING,)), pltpu.SemaphoreType.DMA((SC_RING,))],
        name="gather",
    )(dest_chunks, ys3)


def _experts_kernel(blk_e_ref, n_used_ref, xs_ref, wg_ref, wu_ref, wd_ref, ys_ref, wgu_s, wd_s):
    b = pl.program_id(0)

    @pl.when(b < n_used_ref[0])
    def _():
        prev = blk_e_ref[jnp.maximum(b - 1, 0)]

        @pl.when((b == 0) | (blk_e_ref[b] != prev))
        def _():
            wgu_s[:, :D_EXPERT] = wg_ref[0].astype(BF16)
            wgu_s[:, D_EXPERT:] = wu_ref[0].astype(BF16)
            wd_s[...] = wd_ref[0].astype(BF16)

        chunks = _load_chunks(xs_ref, BM_E)
        x_lo = jnp.concatenate([_unpack_lo(p) for p in chunks], axis=-1).astype(BF16)
        x_hi = jnp.concatenate([_unpack_hi(p) for p in chunks], axis=-1).astype(BF16)
        gu = (jnp.dot(x_lo, wgu_s[:HALF, :], preferred_element_type=F32)
              + jnp.dot(x_hi, wgu_s[HALF:, :], preferred_element_type=F32))
        gate, up = gu[:, :D_EXPERT], gu[:, D_EXPERT:]
        hmid = (gate * _sigmoid(gate) * up).astype(BF16)
        _store_chunks(ys_ref, _pack_bf16_pairs(jnp.dot(hmid, wd_s[...], preferred_element_type=F32)))


def _experts(blk_e, n_used, xs, wg, wu, wd):
    def blk(b, be, nu):
        return jnp.minimum(b, nu[0] - 1)

    grid_spec = pltpu.PrefetchScalarGridSpec(
        num_scalar_prefetch=2,
        grid=(N_BLOCKS,),
        in_specs=[pl.BlockSpec((BM_E * ROW_CHUNKS, LANES), lambda b, be, nu: (blk(b, be, nu), 0)),
                  pl.BlockSpec((1, D_MODEL, D_EXPERT), lambda b, be, nu: (be[blk(b, be, nu)], 0, 0)),
                  pl.BlockSpec((1, D_MODEL, D_EXPERT), lambda b, be, nu: (be[blk(b, be, nu)], 0, 0)),
                  pl.BlockSpec((1, D_EXPERT, D_MODEL), lambda b, be, nu: (be[blk(b, be, nu)], 0, 0))],
        out_specs=pl.BlockSpec((BM_E * ROW_CHUNKS, LANES), lambda b, be, nu: (blk(b, be, nu), 0)),
        scratch_shapes=[pltpu.VMEM((D_MODEL, 2 * D_EXPERT), BF16), pltpu.VMEM((D_EXPERT, D_MODEL), BF16)],
    )
    return pl.pallas_call(
        _experts_kernel,
        grid_spec=grid_spec,
        out_shape=jax.ShapeDtypeStruct((N_SLOTS * ROW_CHUNKS, LANES), U32),
        compiler_params=_cparams(("arbitrary",)),
        name="experts",
    )(blk_e, n_used, xs, wg, wu, wd)


def _combine_kernel(yt_ref, wts_ref, hsh_ref, p_ref, gple_ref, wpg_ref, wpp_ref, gfin_ref, y_ref):
    wts = wts_ref[...]
    lo = [jnp.zeros((BT_COMB, LANES), F32) for _ in range(ROW_CHUNKS)]
    hi = [jnp.zeros((BT_COMB, LANES), F32) for _ in range(ROW_CHUNKS)]
    for k in range(TOP_K):
        w = wts[:, k:k + 1]
        for c, p in enumerate(_load_chunks(yt_ref, BT_COMB, lead=(k,))):
            lo[c] = lo[c] + w * _unpack_lo(p)
            hi[c] = hi[c] + w * _unpack_hi(p)
    h2 = hsh_ref[...] + jnp.concatenate(lo + hi, axis=-1)
    gate = _sigmoid(jnp.dot(_rms(h2, gple_ref[...]).astype(BF16), wpg_ref[...], preferred_element_type=F32))
    proj = jnp.dot(p_ref[...].astype(BF16), wpp_ref[...], preferred_element_type=F32)
    y_ref[...] = _rms(h2 + proj * gate, gfin_ref[...])


def _combine(yt, wts_tok, hsh, p2d, g_ple, wpg, wpp, g_final, row0):
    rows = p2d.shape[0]
    blk0 = row0 // BT_COMB
    ws = [g_ple, wpg, wpp, g_final]
    return pl.pallas_call(
        _combine_kernel,
        grid=(rows // BT_COMB,),
        in_specs=[pl.BlockSpec((TOP_K, BT_COMB * ROW_CHUNKS, LANES), lambda i: (0, blk0 + i, 0)),
                  pl.BlockSpec((BT_COMB, TOP_K), lambda i: (blk0 + i, 0)),
                  pl.BlockSpec((BT_COMB, D_MODEL), lambda i: (blk0 + i, 0)),
                  pl.BlockSpec((BT_COMB, PLE_DIM), lambda i: (i, 0))]
                 + [_full_spec(w, 1) for w in ws],
        out_specs=pl.BlockSpec((BT_COMB, D_MODEL), lambda i: (i, 0)),
        out_shape=jax.ShapeDtypeStruct((rows, D_MODEL), F32),
        compiler_params=_cparams(("parallel",)),
        name="combine",
    )(yt, wts_tok, hsh, p2d, *ws)


def kernel(x_prompt, x_sample, cache_k, cache_v, state_pool, p_prompt, p_sample, g_mix, w_in, attn_sinks,
           w_pool, pool_scale, g_att_out, g_pool_out, w_out, g_ffn, w_router, router_bias, w_exp_gate,
           w_exp_up, w_exp_down, w_sh_gate, w_sh_up, w_sh_down, g_ple, w_ple_gate, w_ple_proj, g_final):
    row = lambda a: a.reshape(1, -1)
    xp2d = x_prompt.reshape(T_P, D_MODEL)
    xs2d = x_sample.reshape(T_S, D_MODEL)
    w_in_bf = w_in[0].astype(BF16)
    mixer_wts = [w_pool[0].astype(BF16), row(pool_scale[0]), row(g_att_out[0]), row(g_pool_out[0]),
                 w_out[0].astype(BF16)]

    tab_p = _rope_tables(jnp.arange(SEQ, dtype=I32))
    pos_s = PAST_LEN + jnp.arange(DEC_SEQ, dtype=I32)
    tab_s = tuple(jnp.tile(t, (BM_IN // DEC_SEQ, 1)) for t in _rope_tables(pos_s))

    q_p, k_p, v_p, u_p = _inproj(xp2d, row(g_mix[0]), w_in_bf, tab_p, BF16)
    q_s, k_s, v_s, u_s = _inproj(xs2d, row(g_mix[0]), w_in_bf, tab_s, F32)

    h1 = _mixer_prompt(attn_sinks[0], xp2d, q_p, k_p, v_p, u_p, mixer_wts)
    h1, k_sample, v_sample, pool_sample = _mixer_sample(
        attn_sinks[0], xs2d, q_s, k_s, v_s, u_s,
        cache_k[0].reshape(DEC_BATCH, WINDOW, KV_WIDTH), cache_v[0].reshape(DEC_BATCH, WINDOW, KV_WIDTH),
        state_pool[0], mixer_wts, h1)

    xp, hsh, idx_t, wts_t = _router(
        h1, row(g_ffn[0]), w_router[0].T, router_bias[0].reshape(N_EXPERTS, 1),
        w_sh_gate[0].astype(BF16), w_sh_up[0].astype(BF16), w_sh_down[0].astype(BF16))

    tri = (lax.broadcasted_iota(I32, (BT_RANK, BT_RANK), 0)
           < lax.broadcasted_iota(I32, (BT_RANK, BT_RANK), 1)).astype(BF16)
    rank_t, counts = _rank(idx_t, tri)
    dest_t, blk_e, n_used = _dest(idx_t, rank_t, counts)

    dest_chunks = dest_t.reshape(TOP_K, T_ALL // SC_CHUNK, SC_CHUNK).transpose(1, 0, 2)
    xs = _dispatch(dest_chunks, xp.reshape(T_ALL, ROW_CHUNKS, LANES))
    ys = _experts(blk_e.reshape(N_BLOCKS_PAD), n_used.reshape(1), xs.reshape(N_SLOTS * ROW_CHUNKS, LANES),
                  w_exp_gate[0], w_exp_up[0], w_exp_down[0])
    yt = _gather(dest_chunks, ys.reshape(N_SLOTS, ROW_CHUNKS, LANES))
    yt = yt.reshape(TOP_K, T_ALL * ROW_CHUNKS, LANES)

    wts_tok = wts_t.T
    ple_wts = (row(g_ple[0]), w_ple_gate[0].astype(BF16), w_ple_proj[0].astype(BF16), row(g_final))
    y_p = _combine(yt, wts_tok, hsh, p_prompt[0].reshape(T_P, PLE_DIM), *ple_wts, 0)
    y_s = _combine(yt, wts_tok, hsh, p_sample[0].reshape(T_S, PLE_DIM), *ple_wts, T_P)

    kv5 = lambda a, b: a.reshape(1, b, WINDOW, N_KV_HEADS, HEAD_DIM)
    k_prompt = kv5(k_p.reshape(BATCH, SEQ, KV_WIDTH)[:, SEQ - WINDOW:], BATCH)
    v_prompt = kv5(v_p.reshape(BATCH, SEQ, KV_WIDTH)[:, SEQ - WINDOW:], BATCH)
    pool_prompt = u_p.reshape(BATCH, SEQ, POOL_WIDTH)[:, SEQ - POOL_STATE:][None]
    return (y_p.reshape(BATCH, SEQ, D_MODEL), y_s.reshape(DEC_BATCH, DEC_SEQ, D_MODEL),
            k_prompt, v_prompt, pool_prompt,
            kv5(k_sample, DEC_BATCH), kv5(v_sample, DEC_BATCH), pool_sample[None])
```

```python
import functools

import jax
import jax.numpy as jnp
from jax import lax
from jax.experimental import pallas as pl
from jax.experimental.pallas import tpu as pltpu
from jax.experimental.pallas import tpu_sc as plsc

F32 = jnp.float32
BF16 = jnp.bfloat16
U32 = jnp.uint32
I32 = jnp.int32

D_MODEL = 1024
BATCH = 8
SEQ = 2048
DEC_BATCH = 128
DEC_SEQ = 8
PAST_LEN = 16384
N_Q_HEADS = 8
N_KV_HEADS = 2
HEAD_DIM = 64
GQA_GROUP = N_Q_HEADS // N_KV_HEADS
ATT_WIDTH = N_Q_HEADS * HEAD_DIM
KV_WIDTH = N_KV_HEADS * HEAD_DIM
WINDOW = 128
ROPE_THETA = 500000.0
ROT_DIM = HEAD_DIM // 4
POOL_WINDOWS = (2, 4, 8, 16)
POOL_GROUPS = 4
POOL_WIDTH = D_MODEL - ATT_WIDTH
POOL_GROUP_DIM = POOL_WIDTH // POOL_GROUPS
POOL_STATE = 15
IN_WIDTH = ATT_WIDTH + 2 * KV_WIDTH + POOL_WIDTH
N_EXPERTS = 64
TOP_K = 8
N_EXPERT_GROUPS = 8
GROUP_SIZE = N_EXPERTS // N_EXPERT_GROUPS
TOPK_GROUPS = 4
D_EXPERT = 256
D_SHARED = 256
ROUTED_SCALE = 2.5
PLE_DIM = 256
EPS = 1e-6

T_P = BATCH * SEQ
T_S = DEC_BATCH * DEC_SEQ
T_ALL = T_P + T_S
HALF = D_MODEL // 2
LANES = 128
VMEM_LIMIT = 48 * 1024 * 1024

BM_IN = 256
BQ = 2 * WINDOW
SB = 16
BM_R = 256
BT_RANK = 512
BT_COMB = 256
BM_E = 512
N_ASSIGN = T_ALL * TOP_K
N_BLOCKS = N_ASSIGN // BM_E + N_EXPERTS
N_SLOTS = N_BLOCKS * BM_E

ROW_CHUNKS = HALF // LANES
SC_CORES = 2
SC_SUBCORES = 16
SC_WORKERS = SC_CORES * SC_SUBCORES
SC_CHUNK = 32
SC_CHUNKS_PER_WORKER = T_ALL // (SC_WORKERS * SC_CHUNK)
SC_RING = 4
assert SC_CHUNKS_PER_WORKER * SC_WORKERS * SC_CHUNK == T_ALL


def _load_chunks(ref, n_rows, lead=()):
    return [ref[lead + (pl.ds(c, n_rows, stride=ROW_CHUNKS), slice(None))] for c in range(ROW_CHUNKS)]


def _store_chunks(ref, packed):
    n_rows = packed.shape[0]
    for c in range(ROW_CHUNKS):
        ref[pl.ds(c, n_rows, stride=ROW_CHUNKS), :] = packed[:, c * LANES:(c + 1) * LANES]


def _cparams(sem):
    return pltpu.CompilerParams(dimension_semantics=sem, vmem_limit_bytes=VMEM_LIMIT)


def _rms(x, g):
    return x * lax.rsqrt(jnp.mean(x * x, axis=-1, keepdims=True) + EPS) * g


def _sigmoid(x):
    return 1.0 / (1.0 + jnp.exp(-x))


def _pack_bf16_pairs(x):
    h = x.shape[-1] // 2
    return pltpu.pack_elementwise([x[:, :h], x[:, h:]], packed_dtype=BF16)


def _unpack_lo(p):
    return pltpu.bitcast(p << 16, F32)


def _unpack_hi(p):
    return pltpu.bitcast(p & jnp.uint32(0xFFFF0000), F32)


def _inproj_kernel(x_ref, g_ref, w_ref, c_ref, s1_ref, s2_ref, q_ref, k_ref, v_ref, u_ref):
    xn = _rms(x_ref[...], g_ref[...]).astype(BF16)
    z = jnp.dot(xn, w_ref[...], preferred_element_type=F32)
    c, s1, s2 = c_ref[...], s1_ref[...], s2_ref[...]

    def rope(t):
        return t * c + pltpu.roll(t, LANES - ROT_DIM // 2, 1) * s1 + pltpu.roll(t, ROT_DIM // 2, 1) * s2

    for i in range(ATT_WIDTH // LANES):
        sl = slice(i * LANES, (i + 1) * LANES)
        q_ref[:, sl] = (rope(z[:, sl]) * (HEAD_DIM ** -0.5)).astype(q_ref.dtype)
    k_ref[...] = rope(z[:, ATT_WIDTH:ATT_WIDTH + KV_WIDTH])
    v_ref[...] = z[:, ATT_WIDTH + KV_WIDTH:ATT_WIDTH + 2 * KV_WIDTH]
    u_ref[...] = z[:, ATT_WIDTH + 2 * KV_WIDTH:]


def _rope_tables(pos):
    half = ROT_DIM // 2
    inv = ROPE_THETA ** (-jnp.arange(half, dtype=F32) * 2.0 / ROT_DIM)
    ang = pos.astype(F32)[:, None] * inv[None, :]
    cos, sin = jnp.cos(ang), jnp.sin(ang)
    n = pos.shape[0]
    ones = jnp.ones((n, HEAD_DIM - ROT_DIM), F32)
    zeros = jnp.zeros((n, HEAD_DIM - ROT_DIM), F32)
    zh = jnp.zeros((n, half), F32)
    c = jnp.concatenate([cos, cos, ones], axis=1)
    s1 = jnp.concatenate([-sin, zh, zeros], axis=1)
    s2 = jnp.concatenate([zh, sin, zeros], axis=1)
    tile = lambda a: jnp.concatenate([a] * (LANES // HEAD_DIM), axis=1)
    return tile(c), tile(s1), tile(s2)


def _inproj(x2d, g_mix, w_in_bf, tables, q_dtype):
    rows = x2d.shape[0]
    n_tab = tables[0].shape[0] // BM_IN
    row_spec = lambda w: pl.BlockSpec((BM_IN, w), lambda i: (i, 0))
    tab_spec = pl.BlockSpec((BM_IN, LANES), lambda i: (i % n_tab, 0))
    full = lambda a: pl.BlockSpec(a.shape, lambda i: (0,) * a.ndim)
    return pl.pallas_call(
        _inproj_kernel,
        grid=(rows // BM_IN,),
        in_specs=[row_spec(D_MODEL), full(g_mix), full(w_in_bf), tab_spec, tab_spec, tab_spec],
        out_specs=[row_spec(ATT_WIDTH), row_spec(KV_WIDTH), row_spec(KV_WIDTH), row_spec(POOL_WIDTH)],
        out_shape=[jax.ShapeDtypeStruct((rows, ATT_WIDTH), q_dtype),
                   jax.ShapeDtypeStruct((rows, KV_WIDTH), F32),
                   jax.ShapeDtypeStruct((rows, KV_WIDTH), F32),
                   jax.ShapeDtypeStruct((rows, POOL_WIDTH), F32)],
        compiler_params=_cparams(("parallel",)),
        name="inproj",
    )(x2d, g_mix, w_in_bf, *tables)


def _sink_column(sinks_ref, kv_head, rows_per_head):
    n = GQA_GROUP * rows_per_head
    grp = lax.broadcasted_iota(I32, (n, 1), 0) // rows_per_head
    col = jnp.full((n, 1), sinks_ref[kv_head * GQA_GROUP], F32)
    for g in range(1, GQA_GROUP):
        col = jnp.where(grp == g, sinks_ref[kv_head * GQA_GROUP + g], col)
    return col


def _band_mask(n_rows, rows_per_head, n_keys):
    i = lax.broadcasted_iota(I32, (n_rows, n_keys), 0) % rows_per_head
    c = lax.broadcasted_iota(I32, (n_rows, n_keys), 1)
    return (c >= i) & (c <= i + WINDOW), c


def _stack_heads(q, kv_head):
    return jnp.concatenate(
        [q[:, (kv_head * GQA_GROUP + g) * HEAD_DIM:(kv_head * GQA_GROUP + g + 1) * HEAD_DIM]
         for g in range(GQA_GROUP)], axis=0)


def _nt_dot(a, b):
    return lax.dot_general(a, b, (((1,), (1,)), ((), ())), preferred_element_type=F32)


def _pool_delta(u, uext_ref, base, n, cnt_fn):
    parts = []
    for g, w in enumerate(POOL_WINDOWS):
        sl = slice(g * POOL_GROUP_DIM, (g + 1) * POOL_GROUP_DIM)
        acc = u[:, sl]
        for m in range(1, w):
            acc = acc + uext_ref[base - m:base - m + n, sl]
        parts.append(acc / cnt_fn(w) - u[:, sl])
    return parts


def _mixer_tail(o_att, d, h, wpool_ref, pscale_ref, gatt_ref, gpool_ref, wout_ref):
    parts = [jnp.dot(d[:, g * POOL_GROUP_DIM:(g + 1) * POOL_GROUP_DIM].astype(BF16), wpool_ref[g],
                     preferred_element_type=F32) for g in range(POOL_GROUPS)]
    o_pool = jnp.concatenate(parts, axis=-1) * pscale_ref[...]
    mixed = jnp.concatenate([_rms(o_att, gatt_ref[...]), _rms(o_pool, gpool_ref[...])], axis=-1)
    return h + jnp.dot(mixed.astype(BF16), wout_ref[...], preferred_element_type=F32)


def _mixer_prompt_kernel(sinks_ref, h_ref, q_ref, kc_ref, kp_ref, vc_ref, vp_ref, uc_ref, up_ref,
                         wpool_ref, pscale_ref, gatt_ref, gpool_ref, wout_ref, h1_ref, uext_ref):
    j = pl.program_id(1)
    q = q_ref[...]
    k_all = jnp.concatenate([kp_ref[...], kc_ref[...]], axis=0).astype(BF16)
    v_all = jnp.concatenate([vp_ref[...], vc_ref[...]], axis=0).astype(BF16)
    band, col = _band_mask(GQA_GROUP * WINDOW, WINDOW, 2 * WINDOW)
    sinks = [_sink_column(sinks_ref, hk, WINDOW) for hk in range(N_KV_HEADS)]
    bands = []
    for b in range(BQ // WINDOW):
        rows = slice(b * WINDOW, (b + 1) * WINDOW)
        keys = slice(b * WINDOW, (b + 2) * WINDOW)
        mask = band & ((col >= WINDOW) | (j > 0)) if b == 0 else band
        heads = []
        for hk in range(N_KV_HEADS):
            sl = slice(hk * HEAD_DIM, (hk + 1) * HEAD_DIM)
            s = jnp.where(mask, _nt_dot(_stack_heads(q[rows], hk), k_all[keys, sl]), -jnp.inf)
            m = jnp.maximum(jnp.max(s, axis=-1, keepdims=True), sinks[hk])
            e = jnp.exp(s - m)
            den = jnp.sum(e, axis=-1, keepdims=True) + jnp.exp(sinks[hk] - m)
            o = jnp.dot(e.astype(BF16), v_all[keys, sl], preferred_element_type=F32) / den
            heads += [o[g * WINDOW:(g + 1) * WINDOW] for g in range(GQA_GROUP)]
        bands.append(jnp.concatenate(heads, axis=-1))
    o_att = jnp.concatenate(bands, axis=0)

    u = uc_ref[...]
    uext_ref[0:16, :] = jnp.where(j > 0, up_ref[...], 0.0)
    uext_ref[16:16 + BQ, :] = u
    pos = j * BQ + lax.broadcasted_iota(I32, (BQ, 1), 0)
    d = jnp.concatenate(
        _pool_delta(u, uext_ref, 16, BQ, lambda w: jnp.minimum(pos + 1, w).astype(F32)), axis=-1)
    h1_ref[...] = _mixer_tail(o_att, d, h_ref[...], wpool_ref, pscale_ref, gatt_ref, gpool_ref, wout_ref)


def _mixer_sample_kernel(sinks_ref, h_ref, q_ref, kn_ref, vn_ref, u_ref, ck_ref, cv_ref, st_ref,
                         wpool_ref, pscale_ref, gatt_ref, gpool_ref, wout_ref, h1_in_ref,
                         h1_ref, ko_ref, vo_ref, po_ref, oatt_ref, d_ref, uext_ref):
    del h1_in_ref
    n_q = GQA_GROUP * DEC_SEQ
    mask_c, _ = _band_mask(n_q, DEC_SEQ, WINDOW)
    qi = lax.broadcasted_iota(I32, (n_q, DEC_SEQ), 0) % DEC_SEQ
    mask_n = lax.broadcasted_iota(I32, (n_q, DEC_SEQ), 1) <= qi
    sinks = [_sink_column(sinks_ref, hk, DEC_SEQ) for hk in range(N_KV_HEADS)]

    def one_batch(b, carry):
        r0 = pl.multiple_of(b * DEC_SEQ, DEC_SEQ)
        rows = pl.ds(r0, DEC_SEQ)
        qb = q_ref[rows, :]
        ck, cv = ck_ref[b], cv_ref[b]
        kn, vn = kn_ref[rows, :], vn_ref[rows, :]
        ko_ref[b, 0:WINDOW - DEC_SEQ, :] = ck[DEC_SEQ:, :]
        ko_ref[b, WINDOW - DEC_SEQ:WINDOW, :] = kn
        vo_ref[b, 0:WINDOW - DEC_SEQ, :] = cv[DEC_SEQ:, :]
        vo_ref[b, WINDOW - DEC_SEQ:WINDOW, :] = vn
        for hk in range(N_KV_HEADS):
            sl = slice(hk * HEAD_DIM, (hk + 1) * HEAD_DIM)
            qs = _stack_heads(qb, hk)
            s_c = jnp.where(mask_c, _nt_dot(qs.astype(BF16), ck[:, sl].astype(BF16)), -jnp.inf)
            s_n = jnp.where(mask_n, _nt_dot(qs, kn[:, sl]), -jnp.inf)
            m = jnp.maximum(jnp.maximum(jnp.max(s_c, axis=-1, keepdims=True),
                                        jnp.max(s_n, axis=-1, keepdims=True)), sinks[hk])
            e_c, e_n = jnp.exp(s_c - m), jnp.exp(s_n - m)
            den = (jnp.sum(e_c, axis=-1, keepdims=True) + jnp.sum(e_n, axis=-1, keepdims=True)
                   + jnp.exp(sinks[hk] - m))
            o = (jnp.dot(e_c.astype(BF16), cv[:, sl].astype(BF16), preferred_element_type=F32)
                 + jnp.dot(e_n, vn[:, sl], preferred_element_type=F32)) / den
            for g in range(GQA_GROUP):
                h0 = (hk * GQA_GROUP + g) * HEAD_DIM
                oatt_ref[rows, h0:h0 + HEAD_DIM] = o[g * DEC_SEQ:(g + 1) * DEC_SEQ]
        ub = u_ref[rows, :]
        uext_ref[1:16, :] = st_ref[b]
        uext_ref[16:16 + DEC_SEQ, :] = ub
        parts = _pool_delta(ub, uext_ref, 16, DEC_SEQ, lambda w: float(w))
        for g in range(POOL_GROUPS):
            d_ref[rows, g * POOL_GROUP_DIM:(g + 1) * POOL_GROUP_DIM] = parts[g]
        po_ref[b] = uext_ref[16 + DEC_SEQ - POOL_STATE:16 + DEC_SEQ, :]
        return carry

    lax.fori_loop(0, SB, one_batch, 0)
    h1_ref[...] = _mixer_tail(oatt_ref[...], d_ref[...], h_ref[...], wpool_ref, pscale_ref, gatt_ref,
                              gpool_ref, wout_ref)


def _full_spec(a, n_grid):
    nd = a.ndim
    return pl.BlockSpec(a.shape, lambda *_: (0,) * nd)


def _mixer_prompt(sinks, x2d, q, k, v, u, wts):
    nb = SEQ // BQ
    row = lambda w: pl.BlockSpec((BQ, w), lambda b, j: (b * nb + j, 0))
    prev = lambda w: pl.BlockSpec(
        (WINDOW, w), lambda b, j: (jnp.maximum((b * nb + j) * (BQ // WINDOW) - 1, 0), 0))
    uprev = pl.BlockSpec((16, POOL_WIDTH), lambda b, j: (jnp.maximum((b * nb + j) * (BQ // 16) - 1, 0), 0))
    smem = pl.BlockSpec(memory_space=pltpu.SMEM)
    return pl.pallas_call(
        _mixer_prompt_kernel,
        grid=(BATCH, nb),
        in_specs=[smem, row(D_MODEL), row(ATT_WIDTH), row(KV_WIDTH), prev(KV_WIDTH), row(KV_WIDTH),
                  prev(KV_WIDTH), row(POOL_WIDTH), uprev] + [_full_spec(w, 2) for w in wts],
        out_specs=row(D_MODEL),
        out_shape=jax.ShapeDtypeStruct((T_ALL, D_MODEL), F32),
        scratch_shapes=[pltpu.VMEM((16 + BQ, POOL_WIDTH), F32)],
        compiler_params=_cparams(("parallel", "parallel")),
        name="mixer_prompt",
    )(sinks, x2d, q, k, k, v, v, u, u, *wts)


def _mixer_sample(sinks, x2d, q, k, v, u, cache_k, cache_v, state, wts, h1_buf):
    rows = SB * DEC_SEQ
    row = lambda w: pl.BlockSpec((rows, w), lambda i: (i, 0))
    bat = lambda a: pl.BlockSpec((SB,) + a.shape[1:], lambda i: (i, 0, 0))
    smem = pl.BlockSpec(memory_space=pltpu.SMEM)
    h1_blocks_before = T_P // rows
    n_in = 9 + len(wts)
    return pl.pallas_call(
        _mixer_sample_kernel,
        grid=(DEC_BATCH // SB,),
        in_specs=[smem, row(D_MODEL), row(ATT_WIDTH), row(KV_WIDTH), row(KV_WIDTH), row(POOL_WIDTH),
                  bat(cache_k), bat(cache_v), bat(state)] + [_full_spec(w, 1) for w in wts]
                 + [pl.BlockSpec(memory_space=pl.ANY)],
        out_specs=[pl.BlockSpec((rows, D_MODEL), lambda i: (h1_blocks_before + i, 0)),
                   bat(cache_k), bat(cache_v), bat(state)],
        out_shape=[jax.ShapeDtypeStruct((T_ALL, D_MODEL), F32),
                   jax.ShapeDtypeStruct(cache_k.shape, F32),
                   jax.ShapeDtypeStruct(cache_v.shape, F32),
                   jax.ShapeDtypeStruct(state.shape, F32)],
        scratch_shapes=[pltpu.VMEM((rows, ATT_WIDTH), F32), pltpu.VMEM((rows, POOL_WIDTH), F32),
                        pltpu.VMEM((16 + DEC_SEQ, POOL_WIDTH), F32)],
        input_output_aliases={n_in: 0},
        compiler_params=_cparams(("parallel",)),
        name="mixer_sample",
    )(sinks, x2d, q, k, v, u, cache_k, cache_v, state, *wts, h1_buf)


def _first_max(vals, iota, n):
    m = jnp.max(vals, axis=0, keepdims=True)
    idx = jnp.min(jnp.where(vals == m, iota, n), axis=0, keepdims=True)
    return m, idx


def _router_kernel(h1_ref, gffn_ref, wrt_ref, bias_ref, wsg_ref, wsu_ref, wsd_ref,
                   xp_ref, hsh_ref, idx_ref, wts_ref):
    h1 = h1_ref[...]
    xn = _rms(h1, gffn_ref[...])
    logits = lax.dot_general(wrt_ref[...], xn, (((1,), (1,)), ((), ())),
                             precision=lax.Precision.HIGHEST, preferred_element_type=F32)
    scores = _sigmoid(logits)
    biased = scores + bias_ref[...]
    n_tok = biased.shape[1]
    neg = -jnp.inf

    iota_g = lax.broadcasted_iota(I32, (GROUP_SIZE, n_tok), 0)
    grp_rows = []
    for g in range(N_EXPERT_GROUPS):
        blk = biased[g * GROUP_SIZE:(g + 1) * GROUP_SIZE, :]
        top1, i1 = _first_max(blk, iota_g, GROUP_SIZE)
        top2 = jnp.max(jnp.where(iota_g == i1, neg, blk), axis=0, keepdims=True)
        grp_rows.append(top1 + top2)
    gs = jnp.concatenate(grp_rows, axis=0)

    iota_n = lax.broadcasted_iota(I32, (N_EXPERT_GROUPS, n_tok), 0)
    gsel = jnp.zeros((N_EXPERT_GROUPS, n_tok), jnp.bool_)
    for _ in range(TOPK_GROUPS):
        _, gi = _first_max(gs, iota_n, N_EXPERT_GROUPS)
        hit = iota_n == gi
        gsel = gsel | hit
        gs = jnp.where(hit, neg, gs)
    emask = jnp.concatenate(
        [jnp.broadcast_to(gsel[g:g + 1, :], (GROUP_SIZE, n_tok)) for g in range(N_EXPERT_GROUPS)], axis=0)
    masked = jnp.where(emask, biased, neg)

    iota_e = lax.broadcasted_iota(I32, (N_EXPERTS, n_tok), 0)
    idx_rows, sel_rows = [], []
    for _ in range(TOP_K):
        _, ei = _first_max(masked, iota_e, N_EXPERTS)
        hit = iota_e == ei
        idx_rows.append(ei)
        sel_rows.append(jnp.sum(jnp.where(hit, scores, 0.0), axis=0, keepdims=True))
        masked = jnp.where(hit, neg, masked)
    sel = jnp.concatenate(sel_rows, axis=0)
    idx_ref[...] = jnp.concatenate(idx_rows, axis=0)
    wts_ref[...] = sel / jnp.sum(sel, axis=0, keepdims=True) * ROUTED_SCALE

    xb = xn.astype(BF16)
    gate = jnp.dot(xb, wsg_ref[...], preferred_element_type=F32)
    up = jnp.dot(xb, wsu_ref[...], preferred_element_type=F32)
    hmid = (gate * _sigmoid(gate) * up).astype(BF16)
    hsh_ref[...] = h1 + jnp.dot(hmid, wsd_ref[...], preferred_element_type=F32)
    _store_chunks(xp_ref, _pack_bf16_pairs(xn))


def _router(h1, g_ffn, w_router_t, bias_col, wsg, wsu, wsd):
    row = lambda w: pl.BlockSpec((BM_R, w), lambda i: (i, 0))
    colblk = pl.BlockSpec((TOP_K, BM_R), lambda i: (0, i))
    ws = [g_ffn, w_router_t, bias_col, wsg, wsu, wsd]
    return pl.pallas_call(
        _router_kernel,
        grid=(T_ALL // BM_R,),
        in_specs=[row(D_MODEL)] + [_full_spec(w, 1) for w in ws],
        out_specs=[pl.BlockSpec((BM_R * ROW_CHUNKS, LANES), lambda i: (i, 0)), row(D_MODEL), colblk, colblk],
        out_shape=[jax.ShapeDtypeStruct((T_ALL * ROW_CHUNKS, LANES), U32),
                   jax.ShapeDtypeStruct((T_ALL, D_MODEL), F32),
                   jax.ShapeDtypeStruct((TOP_K, T_ALL), I32),
                   jax.ShapeDtypeStruct((TOP_K, T_ALL), F32)],
        compiler_params=_cparams(("parallel",)),
        name="router",
    )(h1, *ws)


def _rank_kernel(idx_ref, tri_ref, rank_ref, cnt_ref, carry_ref):
    @pl.when(pl.program_id(0) == 0)
    def _():
        carry_ref[...] = jnp.zeros_like(carry_ref)

    idx = idx_ref[...]
    n_tok = idx.shape[1]
    iota_e = lax.broadcasted_iota(I32, (N_EXPERTS, n_tok), 0)
    member = jnp.zeros((N_EXPERTS, n_tok), F32)
    for k in range(TOP_K):
        member = member + jnp.where(iota_e == idx[k:k + 1, :], 1.0, 0.0)
    before = jnp.dot(member.astype(BF16), tri_ref[...], preferred_element_type=F32) + carry_ref[...]
    rows = [jnp.sum(jnp.where(iota_e == idx[k:k + 1, :], before, 0.0), axis=0, keepdims=True)
            for k in range(TOP_K)]
    rank_ref[...] = jnp.concatenate(rows, axis=0).astype(I32)
    carry_ref[...] = carry_ref[...] + jnp.sum(member, axis=1, keepdims=True)
    cnt_ref[...] = carry_ref[...].astype(I32)


def _rank(idx_t, tri):
    blk = pl.BlockSpec((TOP_K, BT_RANK), lambda i: (0, i))
    return pl.pallas_call(
        _rank_kernel,
        grid=(T_ALL // BT_RANK,),
        in_specs=[blk, _full_spec(tri, 1)],
        out_specs=[blk, pl.BlockSpec((N_EXPERTS, 1), lambda i: (0, 0))],
        out_shape=[jax.ShapeDtypeStruct((TOP_K, T_ALL), I32),
                   jax.ShapeDtypeStruct((N_EXPERTS, 1), I32)],
        scratch_shapes=[pltpu.VMEM((N_EXPERTS, 1), F32)],
        compiler_params=_cparams(("arbitrary",)),
        name="rank",
    )(idx_t, tri)


def _dest_kernel(idx_ref, rank_ref, cnt_ref, dest_ref, tile_start_ref):
    counts = cnt_ref[...]
    padded = (counts + (BM_E - 1)) // BM_E * BM_E
    r = lax.broadcasted_iota(I32, (N_EXPERTS, N_EXPERTS), 0)
    c = lax.broadcasted_iota(I32, (N_EXPERTS, N_EXPERTS), 1)
    padded_row = jnp.sum(jnp.where(r == c, padded, 0), axis=0, keepdims=True)
    pad_start = jnp.sum(jnp.where(c < r, padded_row, 0), axis=1, keepdims=True)

    idx = idx_ref[...]
    n_tok = idx.shape[1]
    iota_e = lax.broadcasted_iota(I32, (N_EXPERTS, n_tok), 0)
    rows = [jnp.sum(jnp.where(iota_e == idx[k:k + 1, :], pad_start, 0), axis=0, keepdims=True)
            for k in range(TOP_K)]
    dest_ref[...] = jnp.concatenate(rows, axis=0) + rank_ref[...]

    rt = lax.broadcasted_iota(I32, (TILE_START_ROWS, N_EXPERTS), 0)
    ct = lax.broadcasted_iota(I32, (TILE_START_ROWS, N_EXPERTS), 1)
    tile_start_ref[...] = jnp.sum(jnp.where(ct < rt, padded_row, 0), axis=1, keepdims=True) // BM_E


TILE_START_ROWS = N_EXPERTS + 8


def _dest(idx_t, rank_t, counts):
    blk = pl.BlockSpec((TOP_K, BT_RANK), lambda i: (0, i))
    one = lambda s: pl.BlockSpec(s, lambda i: (0, 0))
    return pl.pallas_call(
        _dest_kernel,
        grid=(T_ALL // BT_RANK,),
        in_specs=[blk, blk, one((N_EXPERTS, 1))],
        out_specs=[blk, one((TILE_START_ROWS, 1))],
        out_shape=[jax.ShapeDtypeStruct((TOP_K, T_ALL), I32),
                   jax.ShapeDtypeStruct((TILE_START_ROWS, 1), I32)],
        compiler_params=_cparams(("arbitrary",)),
        name="dest",
    )(idx_t, rank_t, counts)


def _sc_mesh():
    return plsc.VectorSubcoreMesh(core_axis_name="c", subcore_axis_name="s")


def _sc_worker_id():
    return lax.axis_index("s") * SC_CORES + lax.axis_index("c")


def _dispatch_body(dest_hbm, xp_hbm, xs_hbm, idx_v, rows_v, sem_in, sem_out):
    chunk0 = _sc_worker_id() * SC_CHUNKS_PER_WORKER

    def loads(i):
        chunk = chunk0 + i
        t0 = pl.multiple_of(chunk * SC_CHUNK, SC_CHUNK)
        return (pltpu.make_async_copy(dest_hbm.at[chunk], idx_v.at[i % 2], sem_in.at[i % 2]),
                pltpu.make_async_copy(xp_hbm.at[pl.ds(t0, SC_CHUNK)], rows_v.at[i % 2], sem_in.at[i % 2]))

    def scatters(i):
        return [pltpu.make_async_copy(rows_v.at[i % 2], xs_hbm.at[idx_v.at[i % 2, k]], sem_out.at[i % 2])
                for k in range(TOP_K)]

    for cp in loads(0):
        cp.start()
    for i in range(SC_CHUNKS_PER_WORKER):
        for cp in loads(i):
            cp.wait()
        if i >= 1:
            for cp in scatters(i - 1):
                cp.wait()
        if i + 1 < SC_CHUNKS_PER_WORKER:
            for cp in loads(i + 1):
                cp.start()
        for cp in scatters(i):
            cp.start()
    for cp in scatters(SC_CHUNKS_PER_WORKER - 1):
        cp.wait()


def _dispatch(dest_chunks, xp3):
    return pl.kernel(
        _dispatch_body,
        out_type=jax.ShapeDtypeStruct((N_SLOTS, ROW_CHUNKS, LANES), U32),
        mesh=_sc_mesh(),
        scratch_types=[pltpu.VMEM((2, TOP_K, SC_CHUNK), I32),
                       pltpu.VMEM((2, SC_CHUNK, ROW_CHUNKS, LANES), U32),
                       pltpu.SemaphoreType.DMA((2,)), pltpu.SemaphoreType.DMA((2,))],
        name="dispatch",
    )(dest_chunks, xp3)


def _gather_body(dest_hbm, ys_hbm, yt_hbm, idx_v, rows_v, sem_in, sem_out):
    chunk0 = _sc_worker_id() * SC_CHUNKS_PER_WORKER

    @pl.loop(0, SC_CHUNKS_PER_WORKER)
    def _(i):
        chunk = chunk0 + i
        t0 = pl.multiple_of(chunk * SC_CHUNK, SC_CHUNK)
        pltpu.sync_copy(dest_hbm.at[chunk], idx_v)

        def gather(k):
            return pltpu.make_async_copy(ys_hbm.at[idx_v.at[k]], rows_v.at[k % SC_RING], sem_in.at[k % SC_RING])

        def store(k):
            return pltpu.make_async_copy(rows_v.at[k % SC_RING], yt_hbm.at[k, pl.ds(t0, SC_CHUNK)],
                                         sem_out.at[k % SC_RING])

        for k in range(SC_RING):
            gather(k).start()
        for k in range(TOP_K):
            gather(k).wait()
            store(k).start()
            if k + SC_RING < TOP_K:
                store(k).wait()
                gather(k + SC_RING).start()
        for k in range(TOP_K - SC_RING, TOP_K):
            store(k).wait()


def _gather(dest_chunks, ys3):
    return pl.kernel(
        _gather_body,
        out_type=jax.ShapeDtypeStruct((TOP_K, T_ALL, ROW_CHUNKS, LANES), U32),
        mesh=_sc_mesh(),
        scratch_types=[pltpu.VMEM((TOP_K, SC_CHUNK), I32),
                       pltpu.VMEM((SC_RING, SC_CHUNK, ROW_CHUNKS, LANES), U32),
                       pltpu.SemaphoreType.DMA((SC_RING,)), pltpu.SemaphoreType.DMA((SC_RING,))],
        name="gather",
    )(dest_chunks, ys3)


TILE_ROWS = BM_E * ROW_CHUNKS


def _experts_kernel(ts_ref, xs_hbm, wg_ref, wu_ref, wd_ref, ys_hbm, xbuf, ybuf, wgu_s, wd_s, sem_in, sem_out):
    e = pl.program_id(0)
    g0, g1, n_used = ts_ref[e], ts_ref[e + 1], ts_ref[N_EXPERTS]

    def x_copy(g):
        rows = pl.ds(pl.multiple_of(g * TILE_ROWS, TILE_ROWS), TILE_ROWS)
        return pltpu.make_async_copy(xs_hbm.at[rows], xbuf.at[g % 2], sem_in.at[g % 2])

    def y_copy(g):
        rows = pl.ds(pl.multiple_of(g * TILE_ROWS, TILE_ROWS), TILE_ROWS)
        return pltpu.make_async_copy(ybuf.at[g % 2], ys_hbm.at[rows], sem_out.at[g % 2])

    @pl.when((e == 0) & (n_used > 0))
    def _():
        x_copy(0).start()

    @pl.when(g1 > g0)
    def _():
        wgu_s[:, :D_EXPERT] = wg_ref[0].astype(BF16)
        wgu_s[:, D_EXPERT:] = wu_ref[0].astype(BF16)
        wd_s[...] = wd_ref[0].astype(BF16)

    def tile(g, carry):
        slot = g % 2
        x_copy(g).wait()

        @pl.when(g + 1 < n_used)
        def _():
            x_copy(g + 1).start()

        chunks = _load_chunks(xbuf, BM_E, lead=(slot,))
        x_lo = jnp.concatenate([_unpack_lo(p) for p in chunks], axis=-1).astype(BF16)
        x_hi = jnp.concatenate([_unpack_hi(p) for p in chunks], axis=-1).astype(BF16)
        gu = (jnp.dot(x_lo, wgu_s[:HALF, :], preferred_element_type=F32)
              + jnp.dot(x_hi, wgu_s[HALF:, :], preferred_element_type=F32))
        gate, up = gu[:, :D_EXPERT], gu[:, D_EXPERT:]
        hmid = (gate * _sigmoid(gate) * up).astype(BF16)
        packed = _pack_bf16_pairs(jnp.dot(hmid, wd_s[...], preferred_element_type=F32))

        @pl.when(g >= 2)
        def _():
            y_copy(g - 2).wait()

        for c in range(ROW_CHUNKS):
            ybuf[slot, pl.ds(c, BM_E, stride=ROW_CHUNKS), :] = packed[:, c * LANES:(c + 1) * LANES]
        y_copy(g).start()
        return carry

    lax.fori_loop(g0, g1, tile, 0)

    @pl.when(e == N_EXPERTS - 1)
    def _():
        @pl.when(n_used >= 2)
        def _():
            y_copy(n_used - 2).wait()

        @pl.when(n_used >= 1)
        def _():
            y_copy(n_used - 1).wait()


def _experts(tile_start, xs, wg, wu, wd):
    by_expert = lambda shape: pl.BlockSpec((1,) + shape, lambda e, ts: (e, 0, 0))
    grid_spec = pltpu.PrefetchScalarGridSpec(
        num_scalar_prefetch=1,
        grid=(N_EXPERTS,),
        in_specs=[pl.BlockSpec(memory_space=pl.ANY), by_expert((D_MODEL, D_EXPERT)),
                  by_expert((D_MODEL, D_EXPERT)), by_expert((D_EXPERT, D_MODEL))],
        out_specs=pl.BlockSpec(memory_space=pl.ANY),
        scratch_shapes=[pltpu.VMEM((2, TILE_ROWS, LANES), U32), pltpu.VMEM((2, TILE_ROWS, LANES), U32),
                        pltpu.VMEM((D_MODEL, 2 * D_EXPERT), BF16), pltpu.VMEM((D_EXPERT, D_MODEL), BF16),
                        pltpu.SemaphoreType.DMA((2,)), pltpu.SemaphoreType.DMA((2,))],
    )
    return pl.pallas_call(
        _experts_kernel,
        grid_spec=grid_spec,
        out_shape=jax.ShapeDtypeStruct((N_SLOTS * ROW_CHUNKS, LANES), U32),
        compiler_params=_cparams(("arbitrary",)),
        name="experts",
    )(tile_start, xs, wg, wu, wd)


def _combine_kernel(yt_ref, wts_ref, hsh_ref, p_ref, gple_ref, wpg_ref, wpp_ref, gfin_ref, y_ref):
    wts = wts_ref[...]
    lo = [jnp.zeros((BT_COMB, LANES), F32) for _ in range(ROW_CHUNKS)]
    hi = [jnp.zeros((BT_COMB, LANES), F32) for _ in range(ROW_CHUNKS)]
    for k in range(TOP_K):
        w = wts[:, k:k + 1]
        for c, p in enumerate(_load_chunks(yt_ref, BT_COMB, lead=(k,))):
            lo[c] = lo[c] + w * _unpack_lo(p)
            hi[c] = hi[c] + w * _unpack_hi(p)
    h2 = hsh_ref[...] + jnp.concatenate(lo + hi, axis=-1)
    gate = _sigmoid(jnp.dot(_rms(h2, gple_ref[...]).astype(BF16), wpg_ref[...], preferred_element_type=F32))
    proj = jnp.dot(p_ref[...].astype(BF16), wpp_ref[...], preferred_element_type=F32)
    y_ref[...] = _rms(h2 + proj * gate, gfin_ref[...])


def _combine(yt, wts_tok, hsh, p2d, g_ple, wpg, wpp, g_final, row0):
    rows = p2d.shape[0]
    blk0 = row0 // BT_COMB
    ws = [g_ple, wpg, wpp, g_final]
    return pl.pallas_call(
        _combine_kernel,
        grid=(rows // BT_COMB,),
        in_specs=[pl.BlockSpec((TOP_K, BT_COMB * ROW_CHUNKS, LANES), lambda i: (0, blk0 + i, 0)),
                  pl.BlockSpec((BT_COMB, TOP_K), lambda i: (blk0 + i, 0)),
                  pl.BlockSpec((BT_COMB, D_MODEL), lambda i: (blk0 + i, 0)),
                  pl.BlockSpec((BT_COMB, PLE_DIM), lambda i: (i, 0))]
                 + [_full_spec(w, 1) for w in ws],
        out_specs=pl.BlockSpec((BT_COMB, D_MODEL), lambda i: (i, 0)),
        out_shape=jax.ShapeDtypeStruct((rows, D_MODEL), F32),
        compiler_params=_cparams(("parallel",)),
        name="combine",
    )(yt, wts_tok, hsh, p2d, *ws)


def kernel(x_prompt, x_sample, cache_k, cache_v, state_pool, p_prompt, p_sample, g_mix, w_in, attn_sinks,
           w_pool, pool_scale, g_att_out, g_pool_out, w_out, g_ffn, w_router, router_bias, w_exp_gate,
           w_exp_up, w_exp_down, w_sh_gate, w_sh_up, w_sh_down, g_ple, w_ple_gate, w_ple_proj, g_final):
    row = lambda a: a.reshape(1, -1)
    xp2d = x_prompt.reshape(T_P, D_MODEL)
    xs2d = x_sample.reshape(T_S, D_MODEL)
    w_in_bf = w_in[0].astype(BF16)
    mixer_wts = [w_pool[0].astype(BF16), row(pool_scale[0]), row(g_att_out[0]), row(g_pool_out[0]),
                 w_out[0].astype(BF16)]

    tab_p = _rope_tables(jnp.arange(SEQ, dtype=I32))
    pos_s = PAST_LEN + jnp.arange(DEC_SEQ, dtype=I32)
    tab_s = tuple(jnp.tile(t, (BM_IN // DEC_SEQ, 1)) for t in _rope_tables(pos_s))

    q_p, k_p, v_p, u_p = _inproj(xp2d, row(g_mix[0]), w_in_bf, tab_p, BF16)
    q_s, k_s, v_s, u_s = _inproj(xs2d, row(g_mix[0]), w_in_bf, tab_s, F32)

    h1 = _mixer_prompt(attn_sinks[0], xp2d, q_p, k_p, v_p, u_p, mixer_wts)
    h1, k_sample, v_sample, pool_sample = _mixer_sample(
        attn_sinks[0], xs2d, q_s, k_s, v_s, u_s,
        cache_k[0].reshape(DEC_BATCH, WINDOW, KV_WIDTH), cache_v[0].reshape(DEC_BATCH, WINDOW, KV_WIDTH),
        state_pool[0], mixer_wts, h1)

    xp, hsh, idx_t, wts_t = _router(
        h1, row(g_ffn[0]), w_router[0].T, router_bias[0].reshape(N_EXPERTS, 1),
        w_sh_gate[0].astype(BF16), w_sh_up[0].astype(BF16), w_sh_down[0].astype(BF16))

    tri = (lax.broadcasted_iota(I32, (BT_RANK, BT_RANK), 0)
           < lax.broadcasted_iota(I32, (BT_RANK, BT_RANK), 1)).astype(BF16)
    rank_t, counts = _rank(idx_t, tri)
    dest_t, tile_start = _dest(idx_t, rank_t, counts)

    dest_chunks = dest_t.reshape(TOP_K, T_ALL // SC_CHUNK, SC_CHUNK).transpose(1, 0, 2)
    xs = _dispatch(dest_chunks, xp.reshape(T_ALL, ROW_CHUNKS, LANES))
    ys = _experts(tile_start.reshape(TILE_START_ROWS), xs.reshape(N_SLOTS * ROW_CHUNKS, LANES),
                  w_exp_gate[0], w_exp_up[0], w_exp_down[0])
    yt = _gather(dest_chunks, ys.reshape(N_SLOTS, ROW_CHUNKS, LANES))
    yt = yt.reshape(TOP_K, T_ALL * ROW_CHUNKS, LANES)

    wts_tok = wts_t.T
    ple_wts = (row(g_ple[0]), w_ple_gate[0].astype(BF16), w_ple_proj[0].astype(BF16), row(g_final))
    y_p = _combine(yt, wts_tok, hsh, p_prompt[0].reshape(T_P, PLE_DIM), *ple_wts, 0)
    y_s = _combine(yt, wts_tok, hsh, p_sample[0].reshape(T_S, PLE_DIM), *ple_wts, T_P)

    kv5 = lambda a, b: a.reshape(1, b, WINDOW, N_KV_HEADS, HEAD_DIM)
    k_prompt = kv5(k_p.reshape(BATCH, SEQ, KV_WIDTH)[:, SEQ - WINDOW:], BATCH)
    v_prompt = kv5(v_p.reshape(BATCH, SEQ, KV_WIDTH)[:, SEQ - WINDOW:], BATCH)
    pool_prompt = u_p.reshape(BATCH, SEQ, POOL_WIDTH)[:, SEQ - POOL_STATE:][None]
    return (y_p.reshape(BATCH, SEQ, D_MODEL), y_s.reshape(DEC_BATCH, DEC_SEQ, D_MODEL),
            k_prompt, v_prompt, pool_prompt,
            kv5(k_sample, DEC_BATCH), kv5(v_sample, DEC_BATCH), pool_sample[None])
```

```python
import functools

import jax
import jax.numpy as jnp
from jax import lax
from jax.experimental import pallas as pl
from jax.experimental.pallas import tpu as pltpu
from jax.experimental.pallas import tpu_sc as plsc

F32 = jnp.float32
BF16 = jnp.bfloat16
U32 = jnp.uint32
I32 = jnp.int32

D_MODEL = 1024
BATCH = 8
SEQ = 2048
DEC_BATCH = 128
DEC_SEQ = 8
PAST_LEN = 16384
N_Q_HEADS = 8
N_KV_HEADS = 2
HEAD_DIM = 64
GQA_GROUP = N_Q_HEADS // N_KV_HEADS
ATT_WIDTH = N_Q_HEADS * HEAD_DIM
KV_WIDTH = N_KV_HEADS * HEAD_DIM
WINDOW = 128
ROPE_THETA = 500000.0
ROT_DIM = HEAD_DIM // 4
POOL_WINDOWS = (2, 4, 8, 16)
POOL_GROUPS = 4
POOL_WIDTH = D_MODEL - ATT_WIDTH
POOL_GROUP_DIM = POOL_WIDTH // POOL_GROUPS
POOL_STATE = 15
IN_WIDTH = ATT_WIDTH + 2 * KV_WIDTH + POOL_WIDTH
N_EXPERTS = 64
TOP_K = 8
N_EXPERT_GROUPS = 8
GROUP_SIZE = N_EXPERTS // N_EXPERT_GROUPS
TOPK_GROUPS = 4
D_EXPERT = 256
D_SHARED = 256
ROUTED_SCALE = 2.5
PLE_DIM = 256
EPS = 1e-6

T_P = BATCH * SEQ
T_S = DEC_BATCH * DEC_SEQ
T_ALL = T_P + T_S
HALF = D_MODEL // 2
LANES = 128
VMEM_LIMIT = 48 * 1024 * 1024

BM_IN = 256
BQ = 2 * WINDOW
SB = 16
BM_R = 256
BT_RANK = 512
BT_COMB = 256
BM_E = 256
N_ASSIGN = T_ALL * TOP_K
N_BLOCKS = N_ASSIGN // BM_E + N_EXPERTS
N_SLOTS = N_BLOCKS * BM_E

ROW_CHUNKS = HALF // LANES
SC_CORES = 2
SC_SUBCORES = 16
SC_WORKERS = SC_CORES * SC_SUBCORES
SC_CHUNK = 32
SC_CHUNKS_PER_WORKER = T_ALL // (SC_WORKERS * SC_CHUNK)
SC_RING = 4
assert SC_CHUNKS_PER_WORKER * SC_WORKERS * SC_CHUNK == T_ALL


def _load_chunks(ref, n_rows, lead=()):
    return [ref[lead + (pl.ds(c, n_rows, stride=ROW_CHUNKS), slice(None))] for c in range(ROW_CHUNKS)]


def _store_chunks(ref, packed):
    n_rows = packed.shape[0]
    for c in range(ROW_CHUNKS):
        ref[pl.ds(c, n_rows, stride=ROW_CHUNKS), :] = packed[:, c * LANES:(c + 1) * LANES]


def _cparams(sem):
    return pltpu.CompilerParams(dimension_semantics=sem, vmem_limit_bytes=VMEM_LIMIT)


def _rms(x, g):
    return x * lax.rsqrt(jnp.mean(x * x, axis=-1, keepdims=True) + EPS) * g


def _sigmoid(x):
    return 1.0 / (1.0 + jnp.exp(-x))


def _pack_bf16_pairs(x):
    h = x.shape[-1] // 2
    return pltpu.pack_elementwise([x[:, :h], x[:, h:]], packed_dtype=BF16)


def _unpack_lo(p):
    return pltpu.bitcast(p << 16, F32)


def _unpack_hi(p):
    return pltpu.bitcast(p & jnp.uint32(0xFFFF0000), F32)


def _inproj_kernel(x_ref, g_ref, w_ref, c_ref, s1_ref, s2_ref, q_ref, k_ref, v_ref, u_ref):
    xn = _rms(x_ref[...], g_ref[...]).astype(BF16)
    z = jnp.dot(xn, w_ref[...], preferred_element_type=F32)
    c, s1, s2 = c_ref[...], s1_ref[...], s2_ref[...]

    def rope(t):
        return t * c + pltpu.roll(t, LANES - ROT_DIM // 2, 1) * s1 + pltpu.roll(t, ROT_DIM // 2, 1) * s2

    for i in range(ATT_WIDTH // LANES):
        sl = slice(i * LANES, (i + 1) * LANES)
        q_ref[:, sl] = (rope(z[:, sl]) * (HEAD_DIM ** -0.5)).astype(q_ref.dtype)
    k_ref[...] = rope(z[:, ATT_WIDTH:ATT_WIDTH + KV_WIDTH])
    v_ref[...] = z[:, ATT_WIDTH + KV_WIDTH:ATT_WIDTH + 2 * KV_WIDTH]
    u_ref[...] = z[:, ATT_WIDTH + 2 * KV_WIDTH:]


def _rope_tables(pos):
    half = ROT_DIM // 2
    inv = ROPE_THETA ** (-jnp.arange(half, dtype=F32) * 2.0 / ROT_DIM)
    ang = pos.astype(F32)[:, None] * inv[None, :]
    cos, sin = jnp.cos(ang), jnp.sin(ang)
    n = pos.shape[0]
    ones = jnp.ones((n, HEAD_DIM - ROT_DIM), F32)
    zeros = jnp.zeros((n, HEAD_DIM - ROT_DIM), F32)
    zh = jnp.zeros((n, half), F32)
    c = jnp.concatenate([cos, cos, ones], axis=1)
    s1 = jnp.concatenate([-sin, zh, zeros], axis=1)
    s2 = jnp.concatenate([zh, sin, zeros], axis=1)
    tile = lambda a: jnp.concatenate([a] * (LANES // HEAD_DIM), axis=1)
    return tile(c), tile(s1), tile(s2)


def _inproj(x2d, g_mix, w_in_bf, tables, q_dtype):
    rows = x2d.shape[0]
    n_tab = tables[0].shape[0] // BM_IN
    row_spec = lambda w: pl.BlockSpec((BM_IN, w), lambda i: (i, 0))
    tab_spec = pl.BlockSpec((BM_IN, LANES), lambda i: (i % n_tab, 0))
    full = lambda a: pl.BlockSpec(a.shape, lambda i: (0,) * a.ndim)
    return pl.pallas_call(
        _inproj_kernel,
        grid=(rows // BM_IN,),
        in_specs=[row_spec(D_MODEL), full(g_mix), full(w_in_bf), tab_spec, tab_spec, tab_spec],
        out_specs=[row_spec(ATT_WIDTH), row_spec(KV_WIDTH), row_spec(KV_WIDTH), row_spec(POOL_WIDTH)],
        out_shape=[jax.ShapeDtypeStruct((rows, ATT_WIDTH), q_dtype),
                   jax.ShapeDtypeStruct((rows, KV_WIDTH), F32),
                   jax.ShapeDtypeStruct((rows, KV_WIDTH), F32),
                   jax.ShapeDtypeStruct((rows, POOL_WIDTH), F32)],
        compiler_params=_cparams(("parallel",)),
        name="inproj",
    )(x2d, g_mix, w_in_bf, *tables)


def _sink_column(sinks_ref, kv_head, rows_per_head):
    n = GQA_GROUP * rows_per_head
    grp = lax.broadcasted_iota(I32, (n, 1), 0) // rows_per_head
    col = jnp.full((n, 1), sinks_ref[kv_head * GQA_GROUP], F32)
    for g in range(1, GQA_GROUP):
        col = jnp.where(grp == g, sinks_ref[kv_head * GQA_GROUP + g], col)
    return col


def _band_mask(n_rows, rows_per_head, n_keys):
    i = lax.broadcasted_iota(I32, (n_rows, n_keys), 0) % rows_per_head
    c = lax.broadcasted_iota(I32, (n_rows, n_keys), 1)
    return (c >= i) & (c <= i + WINDOW), c


def _stack_heads(q, kv_head):
    return jnp.concatenate(
        [q[:, (kv_head * GQA_GROUP + g) * HEAD_DIM:(kv_head * GQA_GROUP + g + 1) * HEAD_DIM]
         for g in range(GQA_GROUP)], axis=0)


def _nt_dot(a, b):
    return lax.dot_general(a, b, (((1,), (1,)), ((), ())), preferred_element_type=F32)


def _pool_delta(u, uext_ref, base, n, cnt_fn):
    parts = []
    for g, w in enumerate(POOL_WINDOWS):
        sl = slice(g * POOL_GROUP_DIM, (g + 1) * POOL_GROUP_DIM)
        acc = u[:, sl]
        for m in range(1, w):
            acc = acc + uext_ref[base - m:base - m + n, sl]
        parts.append(acc / cnt_fn(w) - u[:, sl])
    return parts


def _mixer_tail(o_att, d, h, wpool_ref, pscale_ref, gatt_ref, gpool_ref, wout_ref):
    parts = [jnp.dot(d[:, g * POOL_GROUP_DIM:(g + 1) * POOL_GROUP_DIM].astype(BF16), wpool_ref[g],
                     preferred_element_type=F32) for g in range(POOL_GROUPS)]
    o_pool = jnp.concatenate(parts, axis=-1) * pscale_ref[...]
    mixed = jnp.concatenate([_rms(o_att, gatt_ref[...]), _rms(o_pool, gpool_ref[...])], axis=-1)
    return h + jnp.dot(mixed.astype(BF16), wout_ref[...], preferred_element_type=F32)


def _mixer_prompt_kernel(sinks_ref, h_ref, q_ref, kc_ref, kp_ref, vc_ref, vp_ref, uc_ref, up_ref,
                         wpool_ref, pscale_ref, gatt_ref, gpool_ref, wout_ref, h1_ref, uext_ref):
    j = pl.program_id(1)
    q = q_ref[...]
    k_all = jnp.concatenate([kp_ref[...], kc_ref[...]], axis=0).astype(BF16)
    v_all = jnp.concatenate([vp_ref[...], vc_ref[...]], axis=0).astype(BF16)
    band, col = _band_mask(GQA_GROUP * WINDOW, WINDOW, 2 * WINDOW)
    sinks = [_sink_column(sinks_ref, hk, WINDOW) for hk in range(N_KV_HEADS)]
    bands = []
    for b in range(BQ // WINDOW):
        rows = slice(b * WINDOW, (b + 1) * WINDOW)
        keys = slice(b * WINDOW, (b + 2) * WINDOW)
        mask = band & ((col >= WINDOW) | (j > 0)) if b == 0 else band
        heads = []
        for hk in range(N_KV_HEADS):
            sl = slice(hk * HEAD_DIM, (hk + 1) * HEAD_DIM)
            s = jnp.where(mask, _nt_dot(_stack_heads(q[rows], hk), k_all[keys, sl]), -jnp.inf)
            m = jnp.maximum(jnp.max(s, axis=-1, keepdims=True), sinks[hk])
            e = jnp.exp(s - m)
            den = jnp.sum(e, axis=-1, keepdims=True) + jnp.exp(sinks[hk] - m)
            o = jnp.dot(e.astype(BF16), v_all[keys, sl], preferred_element_type=F32) / den
            heads += [o[g * WINDOW:(g + 1) * WINDOW] for g in range(GQA_GROUP)]
        bands.append(jnp.concatenate(heads, axis=-1))
    o_att = jnp.concatenate(bands, axis=0)

    u = uc_ref[...]
    uext_ref[0:16, :] = jnp.where(j > 0, up_ref[...], 0.0)
    uext_ref[16:16 + BQ, :] = u
    pos = j * BQ + lax.broadcasted_iota(I32, (BQ, 1), 0)
    d = jnp.concatenate(
        _pool_delta(u, uext_ref, 16, BQ, lambda w: jnp.minimum(pos + 1, w).astype(F32)), axis=-1)
    h1_ref[...] = _mixer_tail(o_att, d, h_ref[...], wpool_ref, pscale_ref, gatt_ref, gpool_ref, wout_ref)


def _mixer_sample_kernel(sinks_ref, h_ref, q_ref, kn_ref, vn_ref, u_ref, ck_ref, cv_ref, st_ref,
                         wpool_ref, pscale_ref, gatt_ref, gpool_ref, wout_ref, h1_in_ref,
                         h1_ref, ko_ref, vo_ref, po_ref, oatt_ref, d_ref, uext_ref):
    del h1_in_ref
    n_q = GQA_GROUP * DEC_SEQ
    mask_c, _ = _band_mask(n_q, DEC_SEQ, WINDOW)
    qi = lax.broadcasted_iota(I32, (n_q, DEC_SEQ), 0) % DEC_SEQ
    mask_n = lax.broadcasted_iota(I32, (n_q, DEC_SEQ), 1) <= qi
    sinks = [_sink_column(sinks_ref, hk, DEC_SEQ) for hk in range(N_KV_HEADS)]

    def one_batch(b, carry):
        r0 = pl.multiple_of(b * DEC_SEQ, DEC_SEQ)
        rows = pl.ds(r0, DEC_SEQ)
        qb = q_ref[rows, :]
        ck, cv = ck_ref[b], cv_ref[b]
        kn, vn = kn_ref[rows, :], vn_ref[rows, :]
        ko_ref[b, 0:WINDOW - DEC_SEQ, :] = ck[DEC_SEQ:, :]
        ko_ref[b, WINDOW - DEC_SEQ:WINDOW, :] = kn
        vo_ref[b, 0:WINDOW - DEC_SEQ, :] = cv[DEC_SEQ:, :]
        vo_ref[b, WINDOW - DEC_SEQ:WINDOW, :] = vn
        for hk in range(N_KV_HEADS):
            sl = slice(hk * HEAD_DIM, (hk + 1) * HEAD_DIM)
            qs = _stack_heads(qb, hk)
            s_c = jnp.where(mask_c, _nt_dot(qs.astype(BF16), ck[:, sl].astype(BF16)), -jnp.inf)
            s_n = jnp.where(mask_n, _nt_dot(qs, kn[:, sl]), -jnp.inf)
            m = jnp.maximum(jnp.maximum(jnp.max(s_c, axis=-1, keepdims=True),
                                        jnp.max(s_n, axis=-1, keepdims=True)), sinks[hk])
            e_c, e_n = jnp.exp(s_c - m), jnp.exp(s_n - m)
            den = (jnp.sum(e_c, axis=-1, keepdims=True) + jnp.sum(e_n, axis=-1, keepdims=True)
                   + jnp.exp(sinks[hk] - m))
            o = (jnp.dot(e_c.astype(BF16), cv[:, sl].astype(BF16), preferred_element_type=F32)
                 + jnp.dot(e_n, vn[:, sl], preferred_element_type=F32)) / den
            for g in range(GQA_GROUP):
                h0 = (hk * GQA_GROUP + g) * HEAD_DIM
                oatt_ref[rows, h0:h0 + HEAD_DIM] = o[g * DEC_SEQ:(g + 1) * DEC_SEQ]
        ub = u_ref[rows, :]
        uext_ref[1:16, :] = st_ref[b]
        uext_ref[16:16 + DEC_SEQ, :] = ub
        parts = _pool_delta(ub, uext_ref, 16, DEC_SEQ, lambda w: float(w))
        for g in range(POOL_GROUPS):
            d_ref[rows, g * POOL_GROUP_DIM:(g + 1) * POOL_GROUP_DIM] = parts[g]
        po_ref[b] = uext_ref[16 + DEC_SEQ - POOL_STATE:16 + DEC_SEQ, :]
        return carry

    lax.fori_loop(0, SB, one_batch, 0)
    h1_ref[...] = _mixer_tail(oatt_ref[...], d_ref[...], h_ref[...], wpool_ref, pscale_ref, gatt_ref,
                              gpool_ref, wout_ref)


def _full_spec(a, n_grid):
    nd = a.ndim
    return pl.BlockSpec(a.shape, lambda *_: (0,) * nd)


def _mixer_prompt(sinks, x2d, q, k, v, u, wts):
    nb = SEQ // BQ
    row = lambda w: pl.BlockSpec((BQ, w), lambda b, j: (b * nb + j, 0))
    prev = lambda w: pl.BlockSpec(
        (WINDOW, w), lambda b, j: (jnp.maximum((b * nb + j) * (BQ // WINDOW) - 1, 0), 0))
    uprev = pl.BlockSpec((16, POOL_WIDTH), lambda b, j: (jnp.maximum((b * nb + j) * (BQ // 16) - 1, 0), 0))
    smem = pl.BlockSpec(memory_space=pltpu.SMEM)
    return pl.pallas_call(
        _mixer_prompt_kernel,
        grid=(BATCH, nb),
        in_specs=[smem, row(D_MODEL), row(ATT_WIDTH), row(KV_WIDTH), prev(KV_WIDTH), row(KV_WIDTH),
                  prev(KV_WIDTH), row(POOL_WIDTH), uprev] + [_full_spec(w, 2) for w in wts],
        out_specs=row(D_MODEL),
        out_shape=jax.ShapeDtypeStruct((T_ALL, D_MODEL), F32),
        scratch_shapes=[pltpu.VMEM((16 + BQ, POOL_WIDTH), F32)],
        compiler_params=_cparams(("parallel", "parallel")),
        name="mixer_prompt",
    )(sinks, x2d, q, k, k, v, v, u, u, *wts)


def _mixer_sample(sinks, x2d, q, k, v, u, cache_k, cache_v, state, wts, h1_buf):
    rows = SB * DEC_SEQ
    row = lambda w: pl.BlockSpec((rows, w), lambda i: (i, 0))
    bat = lambda a: pl.BlockSpec((SB,) + a.shape[1:], lambda i: (i, 0, 0))
    smem = pl.BlockSpec(memory_space=pltpu.SMEM)
    h1_blocks_before = T_P // rows
    n_in = 9 + len(wts)
    return pl.pallas_call(
        _mixer_sample_kernel,
        grid=(DEC_BATCH // SB,),
        in_specs=[smem, row(D_MODEL), row(ATT_WIDTH), row(KV_WIDTH), row(KV_WIDTH), row(POOL_WIDTH),
                  bat(cache_k), bat(cache_v), bat(state)] + [_full_spec(w, 1) for w in wts]
                 + [pl.BlockSpec(memory_space=pl.ANY)],
        out_specs=[pl.BlockSpec((rows, D_MODEL), lambda i: (h1_blocks_before + i, 0)),
                   bat(cache_k), bat(cache_v), bat(state)],
        out_shape=[jax.ShapeDtypeStruct((T_ALL, D_MODEL), F32),
                   jax.ShapeDtypeStruct(cache_k.shape, F32),
                   jax.ShapeDtypeStruct(cache_v.shape, F32),
                   jax.ShapeDtypeStruct(state.shape, F32)],
        scratch_shapes=[pltpu.VMEM((rows, ATT_WIDTH), F32), pltpu.VMEM((rows, POOL_WIDTH), F32),
                        pltpu.VMEM((16 + DEC_SEQ, POOL_WIDTH), F32)],
        input_output_aliases={n_in: 0},
        compiler_params=_cparams(("parallel",)),
        name="mixer_sample",
    )(sinks, x2d, q, k, v, u, cache_k, cache_v, state, *wts, h1_buf)


def _first_max(vals, iota, n):
    m = jnp.max(vals, axis=0, keepdims=True)
    idx = jnp.min(jnp.where(vals == m, iota, n), axis=0, keepdims=True)
    return m, idx


def _router_kernel(h1_ref, gffn_ref, wrt_ref, bias_ref, wsg_ref, wsu_ref, wsd_ref,
                   xp_ref, hsh_ref, idx_ref, wts_ref):
    h1 = h1_ref[...]
    xn = _rms(h1, gffn_ref[...])
    logits = lax.dot_general(wrt_ref[...], xn, (((1,), (1,)), ((), ())),
                             precision=lax.Precision.HIGHEST, preferred_element_type=F32)
    scores = _sigmoid(logits)
    biased = scores + bias_ref[...]
    n_tok = biased.shape[1]
    neg = -jnp.inf

    iota_g = lax.broadcasted_iota(I32, (GROUP_SIZE, n_tok), 0)
    grp_rows = []
    for g in range(N_EXPERT_GROUPS):
        blk = biased[g * GROUP_SIZE:(g + 1) * GROUP_SIZE, :]
        top1, i1 = _first_max(blk, iota_g, GROUP_SIZE)
        top2 = jnp.max(jnp.where(iota_g == i1, neg, blk), axis=0, keepdims=True)
        grp_rows.append(top1 + top2)
    gs = jnp.concatenate(grp_rows, axis=0)

    iota_n = lax.broadcasted_iota(I32, (N_EXPERT_GROUPS, n_tok), 0)
    gsel = jnp.zeros((N_EXPERT_GROUPS, n_tok), jnp.bool_)
    for _ in range(TOPK_GROUPS):
        _, gi = _first_max(gs, iota_n, N_EXPERT_GROUPS)
        hit = iota_n == gi
        gsel = gsel | hit
        gs = jnp.where(hit, neg, gs)
    emask = jnp.concatenate(
        [jnp.broadcast_to(gsel[g:g + 1, :], (GROUP_SIZE, n_tok)) for g in range(N_EXPERT_GROUPS)], axis=0)
    masked = jnp.where(emask, biased, neg)

    iota_e = lax.broadcasted_iota(I32, (N_EXPERTS, n_tok), 0)
    idx_rows, sel_rows = [], []
    for _ in range(TOP_K):
        _, ei = _first_max(masked, iota_e, N_EXPERTS)
        hit = iota_e == ei
        idx_rows.append(ei)
        sel_rows.append(jnp.sum(jnp.where(hit, scores, 0.0), axis=0, keepdims=True))
        masked = jnp.where(hit, neg, masked)
    sel = jnp.concatenate(sel_rows, axis=0)
    idx_ref[...] = jnp.concatenate(idx_rows, axis=0)
    wts_ref[...] = sel / jnp.sum(sel, axis=0, keepdims=True) * ROUTED_SCALE

    xb = xn.astype(BF16)
    gate = jnp.dot(xb, wsg_ref[...], preferred_element_type=F32)
    up = jnp.dot(xb, wsu_ref[...], preferred_element_type=F32)
    hmid = (gate * _sigmoid(gate) * up).astype(BF16)
    hsh_ref[...] = h1 + jnp.dot(hmid, wsd_ref[...], preferred_element_type=F32)
    _store_chunks(xp_ref, _pack_bf16_pairs(xn))


def _router(h1, g_ffn, w_router_t, bias_col, wsg, wsu, wsd):
    row = lambda w: pl.BlockSpec((BM_R, w), lambda i: (i, 0))
    colblk = pl.BlockSpec((TOP_K, BM_R), lambda i: (0, i))
    ws = [g_ffn, w_router_t, bias_col, wsg, wsu, wsd]
    return pl.pallas_call(
        _router_kernel,
        grid=(T_ALL // BM_R,),
        in_specs=[row(D_MODEL)] + [_full_spec(w, 1) for w in ws],
        out_specs=[pl.BlockSpec((BM_R * ROW_CHUNKS, LANES), lambda i: (i, 0)), row(D_MODEL), colblk, colblk],
        out_shape=[jax.ShapeDtypeStruct((T_ALL * ROW_CHUNKS, LANES), U32),
                   jax.ShapeDtypeStruct((T_ALL, D_MODEL), F32),
                   jax.ShapeDtypeStruct((TOP_K, T_ALL), I32),
                   jax.ShapeDtypeStruct((TOP_K, T_ALL), F32)],
        compiler_params=_cparams(("parallel",)),
        name="router",
    )(h1, *ws)


def _rank_kernel(idx_ref, tri_ref, rank_ref, cnt_ref, carry_ref):
    @pl.when(pl.program_id(0) == 0)
    def _():
        carry_ref[...] = jnp.zeros_like(carry_ref)

    idx = idx_ref[...]
    n_tok = idx.shape[1]
    iota_e = lax.broadcasted_iota(I32, (N_EXPERTS, n_tok), 0)
    member = jnp.zeros((N_EXPERTS, n_tok), F32)
    for k in range(TOP_K):
        member = member + jnp.where(iota_e == idx[k:k + 1, :], 1.0, 0.0)
    before = jnp.dot(member.astype(BF16), tri_ref[...], preferred_element_type=F32) + carry_ref[...]
    rows = [jnp.sum(jnp.where(iota_e == idx[k:k + 1, :], before, 0.0), axis=0, keepdims=True)
            for k in range(TOP_K)]
    rank_ref[...] = jnp.concatenate(rows, axis=0).astype(I32)
    carry_ref[...] = carry_ref[...] + jnp.sum(member, axis=1, keepdims=True)
    cnt_ref[...] = carry_ref[...].astype(I32)


def _rank(idx_t, tri):
    blk = pl.BlockSpec((TOP_K, BT_RANK), lambda i: (0, i))
    return pl.pallas_call(
        _rank_kernel,
        grid=(T_ALL // BT_RANK,),
        in_specs=[blk, _full_spec(tri, 1)],
        out_specs=[blk, pl.BlockSpec((N_EXPERTS, 1), lambda i: (0, 0))],
        out_shape=[jax.ShapeDtypeStruct((TOP_K, T_ALL), I32),
                   jax.ShapeDtypeStruct((N_EXPERTS, 1), I32)],
        scratch_shapes=[pltpu.VMEM((N_EXPERTS, 1), F32)],
        compiler_params=_cparams(("arbitrary",)),
        name="rank",
    )(idx_t, tri)


def _dest_kernel(idx_ref, rank_ref, cnt_ref, dest_ref, tile_start_ref):
    counts = cnt_ref[...]
    padded = (counts + (BM_E - 1)) // BM_E * BM_E
    r = lax.broadcasted_iota(I32, (N_EXPERTS, N_EXPERTS), 0)
    c = lax.broadcasted_iota(I32, (N_EXPERTS, N_EXPERTS), 1)
    padded_row = jnp.sum(jnp.where(r == c, padded, 0), axis=0, keepdims=True)
    pad_start = jnp.sum(jnp.where(c < r, padded_row, 0), axis=1, keepdims=True)

    idx = idx_ref[...]
    n_tok = idx.shape[1]
    iota_e = lax.broadcasted_iota(I32, (N_EXPERTS, n_tok), 0)
    rows = [jnp.sum(jnp.where(iota_e == idx[k:k + 1, :], pad_start, 0), axis=0, keepdims=True)
            for k in range(TOP_K)]
    dest_ref[...] = jnp.concatenate(rows, axis=0) + rank_ref[...]

    rt = lax.broadcasted_iota(I32, (TILE_START_ROWS, N_EXPERTS), 0)
    ct = lax.broadcasted_iota(I32, (TILE_START_ROWS, N_EXPERTS), 1)
    tile_start_ref[...] = jnp.sum(jnp.where(ct < rt, padded_row, 0), axis=1, keepdims=True) // BM_E


TILE_START_ROWS = N_EXPERTS + 8


def _dest(idx_t, rank_t, counts):
    blk = pl.BlockSpec((TOP_K, BT_RANK), lambda i: (0, i))
    one = lambda s: pl.BlockSpec(s, lambda i: (0, 0))
    return pl.pallas_call(
        _dest_kernel,
        grid=(T_ALL // BT_RANK,),
        in_specs=[blk, blk, one((N_EXPERTS, 1))],
        out_specs=[blk, one((TILE_START_ROWS, 1))],
        out_shape=[jax.ShapeDtypeStruct((TOP_K, T_ALL), I32),
                   jax.ShapeDtypeStruct((TILE_START_ROWS, 1), I32)],
        compiler_params=_cparams(("arbitrary",)),
        name="dest",
    )(idx_t, rank_t, counts)


def _sc_mesh():
    return plsc.VectorSubcoreMesh(core_axis_name="c", subcore_axis_name="s")


def _sc_worker_id():
    return lax.axis_index("s") * SC_CORES + lax.axis_index("c")


def _dispatch_body(dest_hbm, xp_hbm, xs_hbm, idx_v, rows_v, sem_in, sem_out):
    chunk0 = _sc_worker_id() * SC_CHUNKS_PER_WORKER

    def loads(i):
        chunk = chunk0 + i
        t0 = pl.multiple_of(chunk * SC_CHUNK, SC_CHUNK)
        return (pltpu.make_async_copy(dest_hbm.at[chunk], idx_v.at[i % 2], sem_in.at[i % 2]),
                pltpu.make_async_copy(xp_hbm.at[pl.ds(t0, SC_CHUNK)], rows_v.at[i % 2], sem_in.at[i % 2]))

    def scatters(i):
        return [pltpu.make_async_copy(rows_v.at[i % 2], xs_hbm.at[idx_v.at[i % 2, k]], sem_out.at[i % 2])
                for k in range(TOP_K)]

    for cp in loads(0):
        cp.start()
    for i in range(SC_CHUNKS_PER_WORKER):
        for cp in loads(i):
            cp.wait()
        if i >= 1:
            for cp in scatters(i - 1):
                cp.wait()
        if i + 1 < SC_CHUNKS_PER_WORKER:
            for cp in loads(i + 1):
                cp.start()
        for cp in scatters(i):
            cp.start()
    for cp in scatters(SC_CHUNKS_PER_WORKER - 1):
        cp.wait()


def _dispatch(dest_chunks, xp3):
    return pl.kernel(
        _dispatch_body,
        out_type=jax.ShapeDtypeStruct((N_SLOTS, ROW_CHUNKS, LANES), U32),
        mesh=_sc_mesh(),
        scratch_types=[pltpu.VMEM((2, TOP_K, SC_CHUNK), I32),
                       pltpu.VMEM((2, SC_CHUNK, ROW_CHUNKS, LANES), U32),
                       pltpu.SemaphoreType.DMA((2,)), pltpu.SemaphoreType.DMA((2,))],
        name="dispatch",
    )(dest_chunks, xp3)


def _gather_body(dest_hbm, ys_hbm, yt_hbm, idx_v, rows_v, sem_in, sem_out):
    chunk0 = _sc_worker_id() * SC_CHUNKS_PER_WORKER

    @pl.loop(0, SC_CHUNKS_PER_WORKER)
    def _(i):
        chunk = chunk0 + i
        t0 = pl.multiple_of(chunk * SC_CHUNK, SC_CHUNK)
        pltpu.sync_copy(dest_hbm.at[chunk], idx_v)

        def gather(k):
            return pltpu.make_async_copy(ys_hbm.at[idx_v.at[k]], rows_v.at[k % SC_RING], sem_in.at[k % SC_RING])

        def store(k):
            return pltpu.make_async_copy(rows_v.at[k % SC_RING], yt_hbm.at[k, pl.ds(t0, SC_CHUNK)],
                                         sem_out.at[k % SC_RING])

        for k in range(SC_RING):
            gather(k).start()
        for k in range(TOP_K):
            gather(k).wait()
            store(k).start()
            if k + SC_RING < TOP_K:
                store(k).wait()
                gather(k + SC_RING).start()
        for k in range(TOP_K - SC_RING, TOP_K):
            store(k).wait()


def _gather(dest_chunks, ys3):
    return pl.kernel(
        _gather_body,
        out_type=jax.ShapeDtypeStruct((TOP_K, T_ALL, ROW_CHUNKS, LANES), U32),
        mesh=_sc_mesh(),
        scratch_types=[pltpu.VMEM((TOP_K, SC_CHUNK), I32),
                       pltpu.VMEM((SC_RING, SC_CHUNK, ROW_CHUNKS, LANES), U32),
                       pltpu.SemaphoreType.DMA((SC_RING,)), pltpu.SemaphoreType.DMA((SC_RING,))],
        name="gather",
    )(dest_chunks, ys3)


TILE_ROWS = BM_E * ROW_CHUNKS


def _experts_kernel(ts_ref, xs_hbm, wg_ref, wu_ref, wd_ref, ys_hbm, xbuf, ybuf, wgu_s, wd_s, sem_in, sem_out):
    e = pl.program_id(0)
    g0, g1, n_used = ts_ref[e], ts_ref[e + 1], ts_ref[N_EXPERTS]

    def x_copy(g):
        rows = pl.ds(pl.multiple_of(g * TILE_ROWS, TILE_ROWS), TILE_ROWS)
        return pltpu.make_async_copy(xs_hbm.at[rows], xbuf.at[g % 2], sem_in.at[g % 2])

    def y_copy(g):
        rows = pl.ds(pl.multiple_of(g * TILE_ROWS, TILE_ROWS), TILE_ROWS)
        return pltpu.make_async_copy(ybuf.at[g % 2], ys_hbm.at[rows], sem_out.at[g % 2])

    @pl.when((e == 0) & (n_used > 0))
    def _():
        x_copy(0).start()

    @pl.when(g1 > g0)
    def _():
        wgu_s[:, :D_EXPERT] = wg_ref[0].astype(BF16)
        wgu_s[:, D_EXPERT:] = wu_ref[0].astype(BF16)
        wd_s[...] = wd_ref[0].astype(BF16)

    def tile(g, carry):
        slot = g % 2
        x_copy(g).wait()

        @pl.when(g + 1 < n_used)
        def _():
            x_copy(g + 1).start()

        chunks = _load_chunks(xbuf, BM_E, lead=(slot,))
        x_lo = jnp.concatenate([_unpack_lo(p) for p in chunks], axis=-1).astype(BF16)
        x_hi = jnp.concatenate([_unpack_hi(p) for p in chunks], axis=-1).astype(BF16)
        gu = (jnp.dot(x_lo, wgu_s[:HALF, :], preferred_element_type=F32)
              + jnp.dot(x_hi, wgu_s[HALF:, :], preferred_element_type=F32))
        gate, up = gu[:, :D_EXPERT], gu[:, D_EXPERT:]
        hmid = (gate * _sigmoid(gate) * up).astype(BF16)
        packed = _pack_bf16_pairs(jnp.dot(hmid, wd_s[...], preferred_element_type=F32))

        @pl.when(g >= 2)
        def _():
            y_copy(g - 2).wait()

        for c in range(ROW_CHUNKS):
            ybuf[slot, pl.ds(c, BM_E, stride=ROW_CHUNKS), :] = packed[:, c * LANES:(c + 1) * LANES]
        y_copy(g).start()
        return carry

    lax.fori_loop(g0, g1, tile, 0)

    @pl.when(e == N_EXPERTS - 1)
    def _():
        @pl.when(n_used >= 2)
        def _():
            y_copy(n_used - 2).wait()

        @pl.when(n_used >= 1)
        def _():
            y_copy(n_used - 1).wait()


def _experts(tile_start, xs, wg, wu, wd):
    by_expert = lambda shape: pl.BlockSpec((1,) + shape, lambda e, ts: (e, 0, 0))
    grid_spec = pltpu.PrefetchScalarGridSpec(
        num_scalar_prefetch=1,
        grid=(N_EXPERTS,),
        in_specs=[pl.BlockSpec(memory_space=pl.ANY), by_expert((D_MODEL, D_EXPERT)),
                  by_expert((D_MODEL, D_EXPERT)), by_expert((D_EXPERT, D_MODEL))],
        out_specs=pl.BlockSpec(memory_space=pl.ANY),
        scratch_shapes=[pltpu.VMEM((2, TILE_ROWS, LANES), U32), pltpu.VMEM((2, TILE_ROWS, LANES), U32),
                        pltpu.VMEM((D_MODEL, 2 * D_EXPERT), BF16), pltpu.VMEM((D_EXPERT, D_MODEL), BF16),
                        pltpu.SemaphoreType.DMA((2,)), pltpu.SemaphoreType.DMA((2,))],
    )
    return pl.pallas_call(
        _experts_kernel,
        grid_spec=grid_spec,
        out_shape=jax.ShapeDtypeStruct((N_SLOTS * ROW_CHUNKS, LANES), U32),
        compiler_params=_cparams(("arbitrary",)),
        name="experts",
    )(tile_start, xs, wg, wu, wd)


def _combine_kernel(yt_ref, wts_ref, hsh_ref, p_ref, gple_ref, wpg_ref, wpp_ref, gfin_ref, y_ref):
    wts = wts_ref[...]
    lo = [jnp.zeros((BT_COMB, LANES), F32) for _ in range(ROW_CHUNKS)]
    hi = [jnp.zeros((BT_COMB, LANES), F32) for _ in range(ROW_CHUNKS)]
    for k in range(TOP_K):
        w = wts[:, k:k + 1]
        for c, p in enumerate(_load_chunks(yt_ref, BT_COMB, lead=(k,))):
            lo[c] = lo[c] + w * _unpack_lo(p)
            hi[c] = hi[c] + w * _unpack_hi(p)
    h2 = hsh_ref[...] + jnp.concatenate(lo + hi, axis=-1)
    gate = _sigmoid(jnp.dot(_rms(h2, gple_ref[...]).astype(BF16), wpg_ref[...], preferred_element_type=F32))
    proj = jnp.dot(p_ref[...].astype(BF16), wpp_ref[...], preferred_element_type=F32)
    y_ref[...] = _rms(h2 + proj * gate, gfin_ref[...])


def _combine(yt, wts_tok, hsh, p2d, g_ple, wpg, wpp, g_final, row0):
    rows = p2d.shape[0]
    blk0 = row0 // BT_COMB
    ws = [g_ple, wpg, wpp, g_final]
    return pl.pallas_call(
        _combine_kernel,
        grid=(rows // BT_COMB,),
        in_specs=[pl.BlockSpec((TOP_K, BT_COMB * ROW_CHUNKS, LANES), lambda i: (0, blk0 + i, 0)),
                  pl.BlockSpec((BT_COMB, TOP_K), lambda i: (blk0 + i, 0)),
                  pl.BlockSpec((BT_COMB, D_MODEL), lambda i: (blk0 + i, 0)),
                  pl.BlockSpec((BT_COMB, PLE_DIM), lambda i: (i, 0))]
                 + [_full_spec(w, 1) for w in ws],
        out_specs=pl.BlockSpec((BT_COMB, D_MODEL), lambda i: (i, 0)),
        out_shape=jax.ShapeDtypeStruct((rows, D_MODEL), F32),
        compiler_params=_cparams(("parallel",)),
        name="combine",
    )(yt, wts_tok, hsh, p2d, *ws)


def kernel(x_prompt, x_sample, cache_k, cache_v, state_pool, p_prompt, p_sample, g_mix, w_in, attn_sinks,
           w_pool, pool_scale, g_att_out, g_pool_out, w_out, g_ffn, w_router, router_bias, w_exp_gate,
           w_exp_up, w_exp_down, w_sh_gate, w_sh_up, w_sh_down, g_ple, w_ple_gate, w_ple_proj, g_final):
    row = lambda a: a.reshape(1, -1)
    xp2d = x_prompt.reshape(T_P, D_MODEL)
    xs2d = x_sample.reshape(T_S, D_MODEL)
    w_in_bf = w_in[0].astype(BF16)
    mixer_wts = [w_pool[0].astype(BF16), row(pool_scale[0]), row(g_att_out[0]), row(g_pool_out[0]),
                 w_out[0].astype(BF16)]

    tab_p = _rope_tables(jnp.arange(SEQ, dtype=I32))
    pos_s = PAST_LEN + jnp.arange(DEC_SEQ, dtype=I32)
    tab_s = tuple(jnp.tile(t, (BM_IN // DEC_SEQ, 1)) for t in _rope_tables(pos_s))

    q_p, k_p, v_p, u_p = _inproj(xp2d, row(g_mix[0]), w_in_bf, tab_p, BF16)
    q_s, k_s, v_s, u_s = _inproj(xs2d, row(g_mix[0]), w_in_bf, tab_s, F32)

    h1 = _mixer_prompt(attn_sinks[0], xp2d, q_p, k_p, v_p, u_p, mixer_wts)
    h1, k_sample, v_sample, pool_sample = _mixer_sample(
        attn_sinks[0], xs2d, q_s, k_s, v_s, u_s,
        cache_k[0].reshape(DEC_BATCH, WINDOW, KV_WIDTH), cache_v[0].reshape(DEC_BATCH, WINDOW, KV_WIDTH),
        state_pool[0], mixer_wts, h1)

    xp, hsh, idx_t, wts_t = _router(
        h1, row(g_ffn[0]), w_router[0].T, router_bias[0].reshape(N_EXPERTS, 1),
        w_sh_gate[0].astype(BF16), w_sh_up[0].astype(BF16), w_sh_down[0].astype(BF16))

    tri = (lax.broadcasted_iota(I32, (BT_RANK, BT_RANK), 0)
           < lax.broadcasted_iota(I32, (BT_RANK, BT_RANK), 1)).astype(BF16)
    rank_t, counts = _rank(idx_t, tri)
    dest_t, tile_start = _dest(idx_t, rank_t, counts)

    dest_chunks = dest_t.reshape(TOP_K, T_ALL // SC_CHUNK, SC_CHUNK).transpose(1, 0, 2)
    xs = _dispatch(dest_chunks, xp.reshape(T_ALL, ROW_CHUNKS, LANES))
    ys = _experts(tile_start.reshape(TILE_START_ROWS), xs.reshape(N_SLOTS * ROW_CHUNKS, LANES),
                  w_exp_gate[0], w_exp_up[0], w_exp_down[0])
    yt = _gather(dest_chunks, ys.reshape(N_SLOTS, ROW_CHUNKS, LANES))
    yt = yt.reshape(TOP_K, T_ALL * ROW_CHUNKS, LANES)

    wts_tok = wts_t.T
    ple_wts = (row(g_ple[0]), w_ple_gate[0].astype(BF16), w_ple_proj[0].astype(BF16), row(g_final))
    y_p = _combine(yt, wts_tok, hsh, p_prompt[0].reshape(T_P, PLE_DIM), *ple_wts, 0)
    y_s = _combine(yt, wts_tok, hsh, p_sample[0].reshape(T_S, PLE_DIM), *ple_wts, T_P)

    kv5 = lambda a, b: a.reshape(1, b, WINDOW, N_KV_HEADS, HEAD_DIM)
    k_prompt = kv5(k_p.reshape(BATCH, SEQ, KV_WIDTH)[:, SEQ - WINDOW:], BATCH)
    v_prompt = kv5(v_p.reshape(BATCH, SEQ, KV_WIDTH)[:, SEQ - WINDOW:], BATCH)
    pool_prompt = u_p.reshape(BATCH, SEQ, POOL_WIDTH)[:, SEQ - POOL_STATE:][None]
    return (y_p.reshape(BATCH, SEQ, D_MODEL), y_s.reshape(DEC_BATCH, DEC_SEQ, D_MODEL),
            k_prompt, v_prompt, pool_prompt,
            kv5(k_sample, DEC_BATCH), kv5(v_sample, DEC_BATCH), pool_sample[None])
```

```python
import functools

import jax
import jax.numpy as jnp
from jax import lax
from jax.experimental import pallas as pl
from jax.experimental.pallas import tpu as pltpu
from jax.experimental.pallas import tpu_sc as plsc

F32 = jnp.float32
BF16 = jnp.bfloat16
U32 = jnp.uint32
I32 = jnp.int32

D_MODEL = 1024
BATCH = 8
SEQ = 2048
DEC_BATCH = 128
DEC_SEQ = 8
PAST_LEN = 16384
N_Q_HEADS = 8
N_KV_HEADS = 2
HEAD_DIM = 64
GQA_GROUP = N_Q_HEADS // N_KV_HEADS
ATT_WIDTH = N_Q_HEADS * HEAD_DIM
KV_WIDTH = N_KV_HEADS * HEAD_DIM
WINDOW = 128
ROPE_THETA = 500000.0
ROT_DIM = HEAD_DIM // 4
POOL_WINDOWS = (2, 4, 8, 16)
POOL_GROUPS = 4
POOL_WIDTH = D_MODEL - ATT_WIDTH
POOL_GROUP_DIM = POOL_WIDTH // POOL_GROUPS
POOL_STATE = 15
IN_WIDTH = ATT_WIDTH + 2 * KV_WIDTH + POOL_WIDTH
N_EXPERTS = 64
TOP_K = 8
N_EXPERT_GROUPS = 8
GROUP_SIZE = N_EXPERTS // N_EXPERT_GROUPS
TOPK_GROUPS = 4
D_EXPERT = 256
D_SHARED = 256
ROUTED_SCALE = 2.5
PLE_DIM = 256
EPS = 1e-6

T_P = BATCH * SEQ
T_S = DEC_BATCH * DEC_SEQ
T_ALL = T_P + T_S
HALF = D_MODEL // 2
LANES = 128
VMEM_LIMIT = 48 * 1024 * 1024

BM_IN = 256
BQ = 2 * WINDOW
SB = 16
BM_R = 256
BT_RANK = 512
BT_COMB = 256
BM_E = 512
N_GROUPS = 2
T_GRP = T_ALL // N_GROUPS
N_ASSIGN = T_GRP * TOP_K
N_BLOCKS = N_ASSIGN // BM_E + N_EXPERTS
N_SLOTS = N_BLOCKS * BM_E
assert T_GRP * N_GROUPS == T_ALL and N_ASSIGN % BM_E == 0

ROW_CHUNKS = HALF // LANES
SC_CORES = 2
SC_SUBCORES = 16
SC_WORKERS = SC_CORES * SC_SUBCORES
SC_CHUNK = 16
SC_CHUNKS_PER_WORKER = T_GRP // (SC_WORKERS * SC_CHUNK)
SC_RING = 4
assert SC_CHUNKS_PER_WORKER * SC_WORKERS * SC_CHUNK == T_GRP


def _load_chunks(ref, n_rows, lead=()):
    return [ref[lead + (pl.ds(c, n_rows, stride=ROW_CHUNKS), slice(None))] for c in range(ROW_CHUNKS)]


def _store_chunks(ref, packed):
    n_rows = packed.shape[0]
    for c in range(ROW_CHUNKS):
        ref[pl.ds(c, n_rows, stride=ROW_CHUNKS), :] = packed[:, c * LANES:(c + 1) * LANES]


def _cparams(sem):
    return pltpu.CompilerParams(dimension_semantics=sem, vmem_limit_bytes=VMEM_LIMIT)


def _rms(x, g):
    return x * lax.rsqrt(jnp.mean(x * x, axis=-1, keepdims=True) + EPS) * g


def _sigmoid(x):
    return 1.0 / (1.0 + jnp.exp(-x))


def _pack_bf16_pairs(x):
    h = x.shape[-1] // 2
    return pltpu.pack_elementwise([x[:, :h], x[:, h:]], packed_dtype=BF16)


def _unpack_lo(p):
    return pltpu.bitcast(p << 16, F32)


def _unpack_hi(p):
    return pltpu.bitcast(p & jnp.uint32(0xFFFF0000), F32)


def _inproj_kernel(x_ref, g_ref, w_ref, c_ref, s1_ref, s2_ref, q_ref, k_ref, v_ref, u_ref):
    xn = _rms(x_ref[...], g_ref[...]).astype(BF16)
    z = jnp.dot(xn, w_ref[...], preferred_element_type=F32)
    c, s1, s2 = c_ref[...], s1_ref[...], s2_ref[...]

    def rope(t):
        return t * c + pltpu.roll(t, LANES - ROT_DIM // 2, 1) * s1 + pltpu.roll(t, ROT_DIM // 2, 1) * s2

    for i in range(ATT_WIDTH // LANES):
        sl = slice(i * LANES, (i + 1) * LANES)
        q_ref[:, sl] = (rope(z[:, sl]) * (HEAD_DIM ** -0.5)).astype(q_ref.dtype)
    k_ref[...] = rope(z[:, ATT_WIDTH:ATT_WIDTH + KV_WIDTH])
    v_ref[...] = z[:, ATT_WIDTH + KV_WIDTH:ATT_WIDTH + 2 * KV_WIDTH]
    u_ref[...] = z[:, ATT_WIDTH + 2 * KV_WIDTH:]


def _rope_tables(pos):
    half = ROT_DIM // 2
    inv = ROPE_THETA ** (-jnp.arange(half, dtype=F32) * 2.0 / ROT_DIM)
    ang = pos.astype(F32)[:, None] * inv[None, :]
    cos, sin = jnp.cos(ang), jnp.sin(ang)
    n = pos.shape[0]
    ones = jnp.ones((n, HEAD_DIM - ROT_DIM), F32)
    zeros = jnp.zeros((n, HEAD_DIM - ROT_DIM), F32)
    zh = jnp.zeros((n, half), F32)
    c = jnp.concatenate([cos, cos, ones], axis=1)
    s1 = jnp.concatenate([-sin, zh, zeros], axis=1)
    s2 = jnp.concatenate([zh, sin, zeros], axis=1)
    tile = lambda a: jnp.concatenate([a] * (LANES // HEAD_DIM), axis=1)
    return tile(c), tile(s1), tile(s2)


def _inproj(x2d, g_mix, w_in_bf, tables, q_dtype):
    rows = x2d.shape[0]
    n_tab = tables[0].shape[0] // BM_IN
    row_spec = lambda w: pl.BlockSpec((BM_IN, w), lambda i: (i, 0))
    tab_spec = pl.BlockSpec((BM_IN, LANES), lambda i: (i % n_tab, 0))
    full = lambda a: pl.BlockSpec(a.shape, lambda i: (0,) * a.ndim)
    return pl.pallas_call(
        _inproj_kernel,
        grid=(rows // BM_IN,),
        in_specs=[row_spec(D_MODEL), full(g_mix), full(w_in_bf), tab_spec, tab_spec, tab_spec],
        out_specs=[row_spec(ATT_WIDTH), row_spec(KV_WIDTH), row_spec(KV_WIDTH), row_spec(POOL_WIDTH)],
        out_shape=[jax.ShapeDtypeStruct((rows, ATT_WIDTH), q_dtype),
                   jax.ShapeDtypeStruct((rows, KV_WIDTH), F32),
                   jax.ShapeDtypeStruct((rows, KV_WIDTH), F32),
                   jax.ShapeDtypeStruct((rows, POOL_WIDTH), F32)],
        compiler_params=_cparams(("parallel",)),
        name="inproj",
    )(x2d, g_mix, w_in_bf, *tables)


def _sink_column(sinks_ref, kv_head, rows_per_head):
    n = GQA_GROUP * rows_per_head
    grp = lax.broadcasted_iota(I32, (n, 1), 0) // rows_per_head
    col = jnp.full((n, 1), sinks_ref[kv_head * GQA_GROUP], F32)
    for g in range(1, GQA_GROUP):
        col = jnp.where(grp == g, sinks_ref[kv_head * GQA_GROUP + g], col)
    return col


def _band_mask(n_rows, rows_per_head, n_keys):
    i = lax.broadcasted_iota(I32, (n_rows, n_keys), 0) % rows_per_head
    c = lax.broadcasted_iota(I32, (n_rows, n_keys), 1)
    return (c >= i) & (c <= i + WINDOW), c


def _stack_heads(q, kv_head):
    return jnp.concatenate(
        [q[:, (kv_head * GQA_GROUP + g) * HEAD_DIM:(kv_head * GQA_GROUP + g + 1) * HEAD_DIM]
         for g in range(GQA_GROUP)], axis=0)


def _nt_dot(a, b):
    return lax.dot_general(a, b, (((1,), (1,)), ((), ())), preferred_element_type=F32)


def _pool_delta(u, uext_ref, base, n, cnt_fn):
    parts = []
    for g, w in enumerate(POOL_WINDOWS):
        sl = slice(g * POOL_GROUP_DIM, (g + 1) * POOL_GROUP_DIM)
        acc = u[:, sl]
        for m in range(1, w):
            acc = acc + uext_ref[base - m:base - m + n, sl]
        parts.append(acc / cnt_fn(w) - u[:, sl])
    return parts


def _mixer_tail(o_att, d, h, wpool_ref, pscale_ref, gatt_ref, gpool_ref, wout_ref):
    parts = [jnp.dot(d[:, g * POOL_GROUP_DIM:(g + 1) * POOL_GROUP_DIM].astype(BF16), wpool_ref[g],
                     preferred_element_type=F32) for g in range(POOL_GROUPS)]
    o_pool = jnp.concatenate(parts, axis=-1) * pscale_ref[...]
    mixed = jnp.concatenate([_rms(o_att, gatt_ref[...]), _rms(o_pool, gpool_ref[...])], axis=-1)
    return h + jnp.dot(mixed.astype(BF16), wout_ref[...], preferred_element_type=F32)


def _mixer_prompt_kernel(sinks_ref, h_ref, q_ref, kc_ref, kp_ref, vc_ref, vp_ref, uc_ref, up_ref,
                         wpool_ref, pscale_ref, gatt_ref, gpool_ref, wout_ref, h1_ref, uext_ref):
    j = pl.program_id(1)
    q = q_ref[...]
    k_all = jnp.concatenate([kp_ref[...], kc_ref[...]], axis=0).astype(BF16)
    v_all = jnp.concatenate([vp_ref[...], vc_ref[...]], axis=0).astype(BF16)
    band, col = _band_mask(GQA_GROUP * WINDOW, WINDOW, 2 * WINDOW)
    sinks = [_sink_column(sinks_ref, hk, WINDOW) for hk in range(N_KV_HEADS)]
    bands = []
    for b in range(BQ // WINDOW):
        rows = slice(b * WINDOW, (b + 1) * WINDOW)
        keys = slice(b * WINDOW, (b + 2) * WINDOW)
        mask = band & ((col >= WINDOW) | (j > 0)) if b == 0 else band
        heads = []
        for hk in range(N_KV_HEADS):
            sl = slice(hk * HEAD_DIM, (hk + 1) * HEAD_DIM)
            s = jnp.where(mask, _nt_dot(_stack_heads(q[rows], hk), k_all[keys, sl]), -jnp.inf)
            m = jnp.maximum(jnp.max(s, axis=-1, keepdims=True), sinks[hk])
            e = jnp.exp(s - m)
            den = jnp.sum(e, axis=-1, keepdims=True) + jnp.exp(sinks[hk] - m)
            o = jnp.dot(e.astype(BF16), v_all[keys, sl], preferred_element_type=F32) / den
            heads += [o[g * WINDOW:(g + 1) * WINDOW] for g in range(GQA_GROUP)]
        bands.append(jnp.concatenate(heads, axis=-1))
    o_att = jnp.concatenate(bands, axis=0)

    u = uc_ref[...]
    uext_ref[0:16, :] = jnp.where(j > 0, up_ref[...], 0.0)
    uext_ref[16:16 + BQ, :] = u
    pos = j * BQ + lax.broadcasted_iota(I32, (BQ, 1), 0)
    d = jnp.concatenate(
        _pool_delta(u, uext_ref, 16, BQ, lambda w: jnp.minimum(pos + 1, w).astype(F32)), axis=-1)
    h1_ref[...] = _mixer_tail(o_att, d, h_ref[...], wpool_ref, pscale_ref, gatt_ref, gpool_ref, wout_ref)


def _mixer_sample_kernel(sinks_ref, h_ref, q_ref, kn_ref, vn_ref, u_ref, ck_ref, cv_ref, st_ref,
                         wpool_ref, pscale_ref, gatt_ref, gpool_ref, wout_ref, h1_in_ref,
                         h1_ref, ko_ref, vo_ref, po_ref, oatt_ref, d_ref, uext_ref):
    del h1_in_ref
    n_q = GQA_GROUP * DEC_SEQ
    mask_c, _ = _band_mask(n_q, DEC_SEQ, WINDOW)
    qi = lax.broadcasted_iota(I32, (n_q, DEC_SEQ), 0) % DEC_SEQ
    mask_n = lax.broadcasted_iota(I32, (n_q, DEC_SEQ), 1) <= qi
    sinks = [_sink_column(sinks_ref, hk, DEC_SEQ) for hk in range(N_KV_HEADS)]

    def one_batch(b, carry):
        r0 = pl.multiple_of(b * DEC_SEQ, DEC_SEQ)
        rows = pl.ds(r0, DEC_SEQ)
        qb = q_ref[rows, :]
        ck, cv = ck_ref[b], cv_ref[b]
        kn, vn = kn_ref[rows, :], vn_ref[rows, :]
        ko_ref[b, 0:WINDOW - DEC_SEQ, :] = ck[DEC_SEQ:, :]
        ko_ref[b, WINDOW - DEC_SEQ:WINDOW, :] = kn
        vo_ref[b, 0:WINDOW - DEC_SEQ, :] = cv[DEC_SEQ:, :]
        vo_ref[b, WINDOW - DEC_SEQ:WINDOW, :] = vn
        for hk in range(N_KV_HEADS):
            sl = slice(hk * HEAD_DIM, (hk + 1) * HEAD_DIM)
            qs = _stack_heads(qb, hk)
            s_c = jnp.where(mask_c, _nt_dot(qs.astype(BF16), ck[:, sl].astype(BF16)), -jnp.inf)
            s_n = jnp.where(mask_n, _nt_dot(qs, kn[:, sl]), -jnp.inf)
            m = jnp.maximum(jnp.maximum(jnp.max(s_c, axis=-1, keepdims=True),
                                        jnp.max(s_n, axis=-1, keepdims=True)), sinks[hk])
            e_c, e_n = jnp.exp(s_c - m), jnp.exp(s_n - m)
            den = (jnp.sum(e_c, axis=-1, keepdims=True) + jnp.sum(e_n, axis=-1, keepdims=True)
                   + jnp.exp(sinks[hk] - m))
            o = (jnp.dot(e_c.astype(BF16), cv[:, sl].astype(BF16), preferred_element_type=F32)
                 + jnp.dot(e_n, vn[:, sl], preferred_element_type=F32)) / den
            for g in range(GQA_GROUP):
                h0 = (hk * GQA_GROUP + g) * HEAD_DIM
                oatt_ref[rows, h0:h0 + HEAD_DIM] = o[g * DEC_SEQ:(g + 1) * DEC_SEQ]
        ub = u_ref[rows, :]
        uext_ref[1:16, :] = st_ref[b]
        uext_ref[16:16 + DEC_SEQ, :] = ub
        parts = _pool_delta(ub, uext_ref, 16, DEC_SEQ, lambda w: float(w))
        for g in range(POOL_GROUPS):
            d_ref[rows, g * POOL_GROUP_DIM:(g + 1) * POOL_GROUP_DIM] = parts[g]
        po_ref[b] = uext_ref[16 + DEC_SEQ - POOL_STATE:16 + DEC_SEQ, :]
        return carry

    lax.fori_loop(0, SB, one_batch, 0)
    h1_ref[...] = _mixer_tail(oatt_ref[...], d_ref[...], h_ref[...], wpool_ref, pscale_ref, gatt_ref,
                              gpool_ref, wout_ref)


def _full_spec(a, n_grid):
    nd = a.ndim
    return pl.BlockSpec(a.shape, lambda *_: (0,) * nd)


def _mixer_prompt(sinks, x2d, q, k, v, u, wts):
    nb = SEQ // BQ
    row = lambda w: pl.BlockSpec((BQ, w), lambda b, j: (b * nb + j, 0))
    prev = lambda w: pl.BlockSpec(
        (WINDOW, w), lambda b, j: (jnp.maximum((b * nb + j) * (BQ // WINDOW) - 1, 0), 0))
    uprev = pl.BlockSpec((16, POOL_WIDTH), lambda b, j: (jnp.maximum((b * nb + j) * (BQ // 16) - 1, 0), 0))
    smem = pl.BlockSpec(memory_space=pltpu.SMEM)
    return pl.pallas_call(
        _mixer_prompt_kernel,
        grid=(BATCH, nb),
        in_specs=[smem, row(D_MODEL), row(ATT_WIDTH), row(KV_WIDTH), prev(KV_WIDTH), row(KV_WIDTH),
                  prev(KV_WIDTH), row(POOL_WIDTH), uprev] + [_full_spec(w, 2) for w in wts],
        out_specs=row(D_MODEL),
        out_shape=jax.ShapeDtypeStruct((T_ALL, D_MODEL), F32),
        scratch_shapes=[pltpu.VMEM((16 + BQ, POOL_WIDTH), F32)],
        compiler_params=_cparams(("parallel", "parallel")),
        name="mixer_prompt",
    )(sinks, x2d, q, k, k, v, v, u, u, *wts)


def _mixer_sample(sinks, x2d, q, k, v, u, cache_k, cache_v, state, wts, h1_buf):
    rows = SB * DEC_SEQ
    row = lambda w: pl.BlockSpec((rows, w), lambda i: (i, 0))
    bat = lambda a: pl.BlockSpec((SB,) + a.shape[1:], lambda i: (i, 0, 0))
    smem = pl.BlockSpec(memory_space=pltpu.SMEM)
    h1_blocks_before = T_P // rows
    n_in = 9 + len(wts)
    return pl.pallas_call(
        _mixer_sample_kernel,
        grid=(DEC_BATCH // SB,),
        in_specs=[smem, row(D_MODEL), row(ATT_WIDTH), row(KV_WIDTH), row(KV_WIDTH), row(POOL_WIDTH),
                  bat(cache_k), bat(cache_v), bat(state)] + [_full_spec(w, 1) for w in wts]
                 + [pl.BlockSpec(memory_space=pl.ANY)],
        out_specs=[pl.BlockSpec((rows, D_MODEL), lambda i: (h1_blocks_before + i, 0)),
                   bat(cache_k), bat(cache_v), bat(state)],
        out_shape=[jax.ShapeDtypeStruct((T_ALL, D_MODEL), F32),
                   jax.ShapeDtypeStruct(cache_k.shape, F32),
                   jax.ShapeDtypeStruct(cache_v.shape, F32),
                   jax.ShapeDtypeStruct(state.shape, F32)],
        scratch_shapes=[pltpu.VMEM((rows, ATT_WIDTH), F32), pltpu.VMEM((rows, POOL_WIDTH), F32),
                        pltpu.VMEM((16 + DEC_SEQ, POOL_WIDTH), F32)],
        input_output_aliases={n_in: 0},
        compiler_params=_cparams(("parallel",)),
        name="mixer_sample",
    )(sinks, x2d, q, k, v, u, cache_k, cache_v, state, *wts, h1_buf)


def _first_max(vals, iota, n):
    m = jnp.max(vals, axis=0, keepdims=True)
    idx = jnp.min(jnp.where(vals == m, iota, n), axis=0, keepdims=True)
    return m, idx


def _router_kernel(h1_ref, gffn_ref, wrt_ref, bias_ref, wsg_ref, wsu_ref, wsd_ref,
                   xp_ref, hsh_ref, idx_ref, wts_ref):
    h1 = h1_ref[...]
    xn = _rms(h1, gffn_ref[...])
    logits = lax.dot_general(wrt_ref[...], xn, (((1,), (1,)), ((), ())),
                             precision=lax.Precision.HIGHEST, preferred_element_type=F32)
    scores = _sigmoid(logits)
    biased = scores + bias_ref[...]
    n_tok = biased.shape[1]
    neg = -jnp.inf

    iota_g = lax.broadcasted_iota(I32, (GROUP_SIZE, n_tok), 0)
    grp_rows = []
    for g in range(N_EXPERT_GROUPS):
        blk = biased[g * GROUP_SIZE:(g + 1) * GROUP_SIZE, :]
        top1, i1 = _first_max(blk, iota_g, GROUP_SIZE)
        top2 = jnp.max(jnp.where(iota_g == i1, neg, blk), axis=0, keepdims=True)
        grp_rows.append(top1 + top2)
    gs = jnp.concatenate(grp_rows, axis=0)

    iota_n = lax.broadcasted_iota(I32, (N_EXPERT_GROUPS, n_tok), 0)
    gsel = jnp.zeros((N_EXPERT_GROUPS, n_tok), jnp.bool_)
    for _ in range(TOPK_GROUPS):
        _, gi = _first_max(gs, iota_n, N_EXPERT_GROUPS)
        hit = iota_n == gi
        gsel = gsel | hit
        gs = jnp.where(hit, neg, gs)
    emask = jnp.concatenate(
        [jnp.broadcast_to(gsel[g:g + 1, :], (GROUP_SIZE, n_tok)) for g in range(N_EXPERT_GROUPS)], axis=0)
    masked = jnp.where(emask, biased, neg)

    iota_e = lax.broadcasted_iota(I32, (N_EXPERTS, n_tok), 0)
    idx_rows, sel_rows = [], []
    for _ in range(TOP_K):
        _, ei = _first_max(masked, iota_e, N_EXPERTS)
        hit = iota_e == ei
        idx_rows.append(ei)
        sel_rows.append(jnp.sum(jnp.where(hit, scores, 0.0), axis=0, keepdims=True))
        masked = jnp.where(hit, neg, masked)
    sel = jnp.concatenate(sel_rows, axis=0)
    idx_ref[...] = jnp.concatenate(idx_rows, axis=0)
    wts_ref[...] = sel / jnp.sum(sel, axis=0, keepdims=True) * ROUTED_SCALE

    xb = xn.astype(BF16)
    gate = jnp.dot(xb, wsg_ref[...], preferred_element_type=F32)
    up = jnp.dot(xb, wsu_ref[...], preferred_element_type=F32)
    hmid = (gate * _sigmoid(gate) * up).astype(BF16)
    hsh_ref[...] = h1 + jnp.dot(hmid, wsd_ref[...], preferred_element_type=F32)
    _store_chunks(xp_ref, _pack_bf16_pairs(xn))


def _router(h1, group, g_ffn, w_router_t, bias_col, wsg, wsu, wsd):
    blk0 = group * T_GRP // BM_R
    row = lambda w: pl.BlockSpec((BM_R, w), lambda i: (i, 0))
    colblk = pl.BlockSpec((TOP_K, BM_R), lambda i: (0, i))
    ws = [g_ffn, w_router_t, bias_col, wsg, wsu, wsd]
    return pl.pallas_call(
        _router_kernel,
        grid=(T_GRP // BM_R,),
        in_specs=[pl.BlockSpec((BM_R, D_MODEL), lambda i: (blk0 + i, 0))] + [_full_spec(w, 1) for w in ws],
        out_specs=[pl.BlockSpec((BM_R * ROW_CHUNKS, LANES), lambda i: (i, 0)), row(D_MODEL), colblk, colblk],
        out_shape=[jax.ShapeDtypeStruct((T_GRP * ROW_CHUNKS, LANES), U32),
                   jax.ShapeDtypeStruct((T_GRP, D_MODEL), F32),
                   jax.ShapeDtypeStruct((TOP_K, T_GRP), I32),
                   jax.ShapeDtypeStruct((TOP_K, T_GRP), F32)],
        compiler_params=_cparams(("parallel",)),
        name="router",
    )(h1, *ws)


def _rank_kernel(idx_ref, tri_ref, rank_ref, cnt_ref, carry_ref):
    @pl.when(pl.program_id(0) == 0)
    def _():
        carry_ref[...] = jnp.zeros_like(carry_ref)

    idx = idx_ref[...]
    n_tok = idx.shape[1]
    iota_e = lax.broadcasted_iota(I32, (N_EXPERTS, n_tok), 0)
    member = jnp.zeros((N_EXPERTS, n_tok), F32)
    for k in range(TOP_K):
        member = member + jnp.where(iota_e == idx[k:k + 1, :], 1.0, 0.0)
    before = jnp.dot(member.astype(BF16), tri_ref[...], preferred_element_type=F32) + carry_ref[...]
    rows = [jnp.sum(jnp.where(iota_e == idx[k:k + 1, :], before, 0.0), axis=0, keepdims=True)
            for k in range(TOP_K)]
    rank_ref[...] = jnp.concatenate(rows, axis=0).astype(I32)
    carry_ref[...] = carry_ref[...] + jnp.sum(member, axis=1, keepdims=True)
    cnt_ref[...] = carry_ref[...].astype(I32)


def _rank(idx_t, tri):
    blk = pl.BlockSpec((TOP_K, BT_RANK), lambda i: (0, i))
    return pl.pallas_call(
        _rank_kernel,
        grid=(T_GRP // BT_RANK,),
        in_specs=[blk, _full_spec(tri, 1)],
        out_specs=[blk, pl.BlockSpec((N_EXPERTS, 1), lambda i: (0, 0))],
        out_shape=[jax.ShapeDtypeStruct((TOP_K, T_GRP), I32),
                   jax.ShapeDtypeStruct((N_EXPERTS, 1), I32)],
        scratch_shapes=[pltpu.VMEM((N_EXPERTS, 1), F32)],
        compiler_params=_cparams(("arbitrary",)),
        name="rank",
    )(idx_t, tri)


def _dest_kernel(idx_ref, rank_ref, cnt_ref, dest_ref, blk_e_ref, n_used_ref):
    counts = cnt_ref[...]
    padded = (counts + (BM_E - 1)) // BM_E * BM_E
    r = lax.broadcasted_iota(I32, (N_EXPERTS, N_EXPERTS), 0)
    c = lax.broadcasted_iota(I32, (N_EXPERTS, N_EXPERTS), 1)
    padded_row = jnp.sum(jnp.where(r == c, padded, 0), axis=0, keepdims=True)
    pad_start = jnp.sum(jnp.where(c < r, padded_row, 0), axis=1, keepdims=True)

    idx = idx_ref[...]
    n_tok = idx.shape[1]
    iota_e = lax.broadcasted_iota(I32, (N_EXPERTS, n_tok), 0)
    rows = [jnp.sum(jnp.where(iota_e == idx[k:k + 1, :], pad_start, 0), axis=0, keepdims=True)
            for k in range(TOP_K)]
    dest_ref[...] = jnp.concatenate(rows, axis=0) + rank_ref[...]

    pad_end_row = jnp.sum(jnp.where(r <= c, padded, 0), axis=0, keepdims=True)
    b0 = lax.broadcasted_iota(I32, (N_BLOCKS_PAD, N_EXPERTS), 0) * BM_E
    be = jnp.sum(jnp.where(pad_end_row <= b0, 1, 0), axis=1, keepdims=True)
    blk_e_ref[...] = jnp.minimum(be, N_EXPERTS - 1)
    n_used_ref[...] = pad_end_row[:, N_EXPERTS - 1:N_EXPERTS] // BM_E


N_BLOCKS_PAD = (N_BLOCKS + 7) // 8 * 8


def _dest(idx_t, rank_t, counts):
    blk = pl.BlockSpec((TOP_K, BT_RANK), lambda i: (0, i))
    one = lambda s: pl.BlockSpec(s, lambda i: (0, 0))
    return pl.pallas_call(
        _dest_kernel,
        grid=(T_GRP // BT_RANK,),
        in_specs=[blk, blk, one((N_EXPERTS, 1))],
        out_specs=[blk, one((N_BLOCKS_PAD, 1)), one((1, 1))],
        out_shape=[jax.ShapeDtypeStruct((TOP_K, T_GRP), I32),
                   jax.ShapeDtypeStruct((N_BLOCKS_PAD, 1), I32),
                   jax.ShapeDtypeStruct((1, 1), I32)],
        compiler_params=_cparams(("arbitrary",)),
        name="dest",
    )(idx_t, rank_t, counts)


def _sc_mesh():
    return plsc.VectorSubcoreMesh(core_axis_name="c", subcore_axis_name="s")


def _sc_worker_id():
    return lax.axis_index("s") * SC_CORES + lax.axis_index("c")


def _dispatch_body(dest_hbm, xp_hbm, xs_hbm, idx_v, rows_v, sem_in, sem_out):
    chunk0 = _sc_worker_id() * SC_CHUNKS_PER_WORKER

    def loads(i):
        chunk = chunk0 + i
        t0 = pl.multiple_of(chunk * SC_CHUNK, SC_CHUNK)
        return (pltpu.make_async_copy(dest_hbm.at[chunk], idx_v.at[i % 2], sem_in.at[i % 2]),
                pltpu.make_async_copy(xp_hbm.at[pl.ds(t0, SC_CHUNK)], rows_v.at[i % 2], sem_in.at[i % 2]))

    def scatters(i):
        return [pltpu.make_async_copy(rows_v.at[i % 2], xs_hbm.at[idx_v.at[i % 2, k]], sem_out.at[i % 2])
                for k in range(TOP_K)]

    for cp in loads(0):
        cp.start()
    for i in range(SC_CHUNKS_PER_WORKER):
        for cp in loads(i):
            cp.wait()
        if i >= 1:
            for cp in scatters(i - 1):
                cp.wait()
        if i + 1 < SC_CHUNKS_PER_WORKER:
            for cp in loads(i + 1):
                cp.start()
        for cp in scatters(i):
            cp.start()
    for cp in scatters(SC_CHUNKS_PER_WORKER - 1):
        cp.wait()


def _dispatch(dest_chunks, xp3):
    return pl.kernel(
        _dispatch_body,
        out_type=jax.ShapeDtypeStruct((N_SLOTS, ROW_CHUNKS, LANES), U32),
        mesh=_sc_mesh(),
        scratch_types=[pltpu.VMEM((2, TOP_K, SC_CHUNK), I32),
                       pltpu.VMEM((2, SC_CHUNK, ROW_CHUNKS, LANES), U32),
                       pltpu.SemaphoreType.DMA((2,)), pltpu.SemaphoreType.DMA((2,))],
        name="dispatch",
    )(dest_chunks, xp3)


def _gather_body(dest_hbm, ys_hbm, yt_hbm, idx_v, rows_v, sem_in, sem_out):
    chunk0 = _sc_worker_id() * SC_CHUNKS_PER_WORKER

    @pl.loop(0, SC_CHUNKS_PER_WORKER)
    def _(i):
        chunk = chunk0 + i
        t0 = pl.multiple_of(chunk * SC_CHUNK, SC_CHUNK)
        pltpu.sync_copy(dest_hbm.at[chunk], idx_v)

        def gather(k):
            return pltpu.make_async_copy(ys_hbm.at[idx_v.at[k]], rows_v.at[k % SC_RING], sem_in.at[k % SC_RING])

        def store(k):
            return pltpu.make_async_copy(rows_v.at[k % SC_RING], yt_hbm.at[k, pl.ds(t0, SC_CHUNK)],
                                         sem_out.at[k % SC_RING])

        for k in range(SC_RING):
            gather(k).start()
        for k in range(TOP_K):
            gather(k).wait()
            store(k).start()
            if k + SC_RING < TOP_K:
                store(k).wait()
                gather(k + SC_RING).start()
        for k in range(TOP_K - SC_RING, TOP_K):
            store(k).wait()


def _gather(dest_chunks, ys3):
    return pl.kernel(
        _gather_body,
        out_type=jax.ShapeDtypeStruct((TOP_K, T_GRP, ROW_CHUNKS, LANES), U32),
        mesh=_sc_mesh(),
        scratch_types=[pltpu.VMEM((TOP_K, SC_CHUNK), I32),
                       pltpu.VMEM((SC_RING, SC_CHUNK, ROW_CHUNKS, LANES), U32),
                       pltpu.SemaphoreType.DMA((SC_RING,)), pltpu.SemaphoreType.DMA((SC_RING,))],
        name="gather",
    )(dest_chunks, ys3)


def _experts_kernel(blk_e_ref, n_used_ref, xs_ref, wg_ref, wu_ref, wd_ref, ys_ref, wgu_s, wd_s):
    b = pl.program_id(0)

    @pl.when(b < n_used_ref[0])
    def _():
        prev = blk_e_ref[jnp.maximum(b - 1, 0)]

        @pl.when((b == 0) | (blk_e_ref[b] != prev))
        def _():
            wgu_s[:, :D_EXPERT] = wg_ref[0].astype(BF16)
            wgu_s[:, D_EXPERT:] = wu_ref[0].astype(BF16)
            wd_s[...] = wd_ref[0].astype(BF16)

        chunks = _load_chunks(xs_ref, BM_E)
        x_lo = jnp.concatenate([_unpack_lo(p) for p in chunks], axis=-1).astype(BF16)
        x_hi = jnp.concatenate([_unpack_hi(p) for p in chunks], axis=-1).astype(BF16)
        gu = (jnp.dot(x_lo, wgu_s[:HALF, :], preferred_element_type=F32)
              + jnp.dot(x_hi, wgu_s[HALF:, :], preferred_element_type=F32))
        gate, up = gu[:, :D_EXPERT], gu[:, D_EXPERT:]
        hmid = (gate * _sigmoid(gate) * up).astype(BF16)
        _store_chunks(ys_ref, _pack_bf16_pairs(jnp.dot(hmid, wd_s[...], preferred_element_type=F32)))


def _experts(blk_e, n_used, xs, wg, wu, wd):
    def blk(b, be, nu):
        return jnp.minimum(b, nu[0] - 1)

    grid_spec = pltpu.PrefetchScalarGridSpec(
        num_scalar_prefetch=2,
        grid=(N_BLOCKS,),
        in_specs=[pl.BlockSpec((BM_E * ROW_CHUNKS, LANES), lambda b, be, nu: (blk(b, be, nu), 0)),
                  pl.BlockSpec((1, D_MODEL, D_EXPERT), lambda b, be, nu: (be[blk(b, be, nu)], 0, 0)),
                  pl.BlockSpec((1, D_MODEL, D_EXPERT), lambda b, be, nu: (be[blk(b, be, nu)], 0, 0)),
                  pl.BlockSpec((1, D_EXPERT, D_MODEL), lambda b, be, nu: (be[blk(b, be, nu)], 0, 0))],
        out_specs=pl.BlockSpec((BM_E * ROW_CHUNKS, LANES), lambda b, be, nu: (blk(b, be, nu), 0)),
        scratch_shapes=[pltpu.VMEM((D_MODEL, 2 * D_EXPERT), BF16), pltpu.VMEM((D_EXPERT, D_MODEL), BF16)],
    )
    return pl.pallas_call(
        _experts_kernel,
        grid_spec=grid_spec,
        out_shape=jax.ShapeDtypeStruct((N_SLOTS * ROW_CHUNKS, LANES), U32),
        compiler_params=_cparams(("arbitrary",)),
        name="experts",
    )(blk_e, n_used, xs, wg, wu, wd)


def _combine_kernel(yt_ref, wts_ref, hsh_ref, p_ref, gple_ref, wpg_ref, wpp_ref, gfin_ref, *y_refs):
    y_ref = y_refs[-1]
    wts = wts_ref[...]
    lo = [jnp.zeros((BT_COMB, LANES), F32) for _ in range(ROW_CHUNKS)]
    hi = [jnp.zeros((BT_COMB, LANES), F32) for _ in range(ROW_CHUNKS)]
    for k in range(TOP_K):
        w = wts[:, k:k + 1]
        for c, p in enumerate(_load_chunks(yt_ref, BT_COMB, lead=(k,))):
            lo[c] = lo[c] + w * _unpack_lo(p)
            hi[c] = hi[c] + w * _unpack_hi(p)
    h2 = hsh_ref[...] + jnp.concatenate(lo + hi, axis=-1)
    gate = _sigmoid(jnp.dot(_rms(h2, gple_ref[...]).astype(BF16), wpg_ref[...], preferred_element_type=F32))
    proj = jnp.dot(p_ref[...].astype(BF16), wpp_ref[...], preferred_element_type=F32)
    y_ref[...] = _rms(h2 + proj * gate, gfin_ref[...])


def _combine(grp, grp_row0, n_rows, p2d, p_row0, ws, y_prev, out_rows, out_row0):
    yt, wts_tok, hsh = grp
    g0, p0, o0 = grp_row0 // BT_COMB, p_row0 // BT_COMB, out_row0 // BT_COMB
    assert grp_row0 % BT_COMB == 0 and p_row0 % BT_COMB == 0 and out_row0 % BT_COMB == 0 and n_rows % BT_COMB == 0
    in_specs = [pl.BlockSpec((TOP_K, BT_COMB * ROW_CHUNKS, LANES), lambda i: (0, g0 + i, 0)),
                pl.BlockSpec((BT_COMB, TOP_K), lambda i: (g0 + i, 0)),
                pl.BlockSpec((BT_COMB, D_MODEL), lambda i: (g0 + i, 0)),
                pl.BlockSpec((BT_COMB, PLE_DIM), lambda i: (p0 + i, 0))] + [_full_spec(w, 1) for w in ws]
    args = [yt, wts_tok, hsh, p2d, *ws]
    aliases = {}
    if y_prev is not None:
        in_specs.append(pl.BlockSpec(memory_space=pl.ANY))
        aliases = {len(args): 0}
        args.append(y_prev)
    return pl.pallas_call(
        _combine_kernel,
        grid=(n_rows // BT_COMB,),
        in_specs=in_specs,
        out_specs=pl.BlockSpec((BT_COMB, D_MODEL), lambda i: (o0 + i, 0)),
        out_shape=jax.ShapeDtypeStruct((out_rows, D_MODEL), F32),
        input_output_aliases=aliases,
        compiler_params=_cparams(("parallel",)),
        name="combine",
    )(*args)


def kernel(x_prompt, x_sample, cache_k, cache_v, state_pool, p_prompt, p_sample, g_mix, w_in, attn_sinks,
           w_pool, pool_scale, g_att_out, g_pool_out, w_out, g_ffn, w_router, router_bias, w_exp_gate,
           w_exp_up, w_exp_down, w_sh_gate, w_sh_up, w_sh_down, g_ple, w_ple_gate, w_ple_proj, g_final):
    row = lambda a: a.reshape(1, -1)
    xp2d = x_prompt.reshape(T_P, D_MODEL)
    xs2d = x_sample.reshape(T_S, D_MODEL)
    w_in_bf = w_in[0].astype(BF16)
    mixer_wts = [w_pool[0].astype(BF16), row(pool_scale[0]), row(g_att_out[0]), row(g_pool_out[0]),
                 w_out[0].astype(BF16)]

    tab_p = _rope_tables(jnp.arange(SEQ, dtype=I32))
    pos_s = PAST_LEN + jnp.arange(DEC_SEQ, dtype=I32)
    tab_s = tuple(jnp.tile(t, (BM_IN // DEC_SEQ, 1)) for t in _rope_tables(pos_s))

    q_p, k_p, v_p, u_p = _inproj(xp2d, row(g_mix[0]), w_in_bf, tab_p, BF16)
    q_s, k_s, v_s, u_s = _inproj(xs2d, row(g_mix[0]), w_in_bf, tab_s, F32)

    h1 = _mixer_prompt(attn_sinks[0], xp2d, q_p, k_p, v_p, u_p, mixer_wts)
    h1, k_sample, v_sample, pool_sample = _mixer_sample(
        attn_sinks[0], xs2d, q_s, k_s, v_s, u_s,
        cache_k[0].reshape(DEC_BATCH, WINDOW, KV_WIDTH), cache_v[0].reshape(DEC_BATCH, WINDOW, KV_WIDTH),
        state_pool[0], mixer_wts, h1)

    router_wts = (row(g_ffn[0]), w_router[0].T, router_bias[0].reshape(N_EXPERTS, 1),
                  w_sh_gate[0].astype(BF16), w_sh_up[0].astype(BF16), w_sh_down[0].astype(BF16))
    tri = (lax.broadcasted_iota(I32, (BT_RANK, BT_RANK), 0)
           < lax.broadcasted_iota(I32, (BT_RANK, BT_RANK), 1)).astype(BF16)

    groups = []
    for g in range(N_GROUPS):
        xp, hsh, idx_t, wts_t = _router(h1, g, *router_wts)
        rank_t, counts = _rank(idx_t, tri)
        dest_t, blk_e, n_used = _dest(idx_t, rank_t, counts)
        dest_chunks = dest_t.reshape(TOP_K, T_GRP // SC_CHUNK, SC_CHUNK).transpose(1, 0, 2)
        xs = _dispatch(dest_chunks, xp.reshape(T_GRP, ROW_CHUNKS, LANES))
        groups.append((hsh, wts_t.T, dest_chunks, xs, blk_e, n_used))

    ple_wts = [row(g_ple[0]), w_ple_gate[0].astype(BF16), w_ple_proj[0].astype(BF16), row(g_final)]
    pp2d = p_prompt[0].reshape(T_P, PLE_DIM)
    ps2d = p_sample[0].reshape(T_S, PLE_DIM)
    y_p = y_s = None
    for g, (hsh, wts_tok, dest_chunks, xs, blk_e, n_used) in enumerate(groups):
        ys = _experts(blk_e.reshape(N_BLOCKS_PAD), n_used.reshape(1), xs.reshape(N_SLOTS * ROW_CHUNKS, LANES),
                      w_exp_gate[0], w_exp_up[0], w_exp_down[0])
        yt = _gather(dest_chunks, ys.reshape(N_SLOTS, ROW_CHUNKS, LANES))
        grp = (yt.reshape(TOP_K, T_GRP * ROW_CHUNKS, LANES), wts_tok, hsh)
        lo, hi = g * T_GRP, (g + 1) * T_GRP
        if lo < T_P:
            n = min(hi, T_P) - lo
            y_p = _combine(grp, 0, n, pp2d, lo, ple_wts, y_p, T_P, lo)
        if hi > T_P:
            s0 = max(lo, T_P)
            y_s = _combine(grp, s0 - lo, hi - s0, ps2d, s0 - T_P, ple_wts, y_s, T_S, s0 - T_P)

    kv5 = lambda a, b: a.reshape(1, b, WINDOW, N_KV_HEADS, HEAD_DIM)
    k_prompt = kv5(k_p.reshape(BATCH, SEQ, KV_WIDTH)[:, SEQ - WINDOW:], BATCH)
    v_prompt = kv5(v_p.reshape(BATCH, SEQ, KV_WIDTH)[:, SEQ - WINDOW:], BATCH)
    pool_prompt = u_p.reshape(BATCH, SEQ, POOL_WIDTH)[:, SEQ - POOL_STATE:][None]
    return (y_p.reshape(BATCH, SEQ, D_MODEL), y_s.reshape(DEC_BATCH, DEC_SEQ, D_MODEL),
            k_prompt, v_prompt, pool_prompt,
            kv5(k_sample, DEC_BATCH), kv5(v_sample, DEC_BATCH), pool_sample[None])
```

```python
import functools

import jax
import jax.numpy as jnp
from jax import lax
from jax.experimental import pallas as pl
from jax.experimental.pallas import tpu as pltpu
from jax.experimental.pallas import tpu_sc as plsc

F32 = jnp.float32
BF16 = jnp.bfloat16
U32 = jnp.uint32
I32 = jnp.int32

D_MODEL = 1024
BATCH = 8
SEQ = 2048
DEC_BATCH = 128
DEC_SEQ = 8
PAST_LEN = 16384
N_Q_HEADS = 8
N_KV_HEADS = 2
HEAD_DIM = 64
GQA_GROUP = N_Q_HEADS // N_KV_HEADS
ATT_WIDTH = N_Q_HEADS * HEAD_DIM
KV_WIDTH = N_KV_HEADS * HEAD_DIM
WINDOW = 128
ROPE_THETA = 500000.0
ROT_DIM = HEAD_DIM // 4
POOL_WINDOWS = (2, 4, 8, 16)
POOL_GROUPS = 4
POOL_WIDTH = D_MODEL - ATT_WIDTH
POOL_GROUP_DIM = POOL_WIDTH // POOL_GROUPS
POOL_STATE = 15
IN_WIDTH = ATT_WIDTH + 2 * KV_WIDTH + POOL_WIDTH
N_EXPERTS = 64
TOP_K = 8
N_EXPERT_GROUPS = 8
GROUP_SIZE = N_EXPERTS // N_EXPERT_GROUPS
TOPK_GROUPS = 4
D_EXPERT = 256
D_SHARED = 256
ROUTED_SCALE = 2.5
PLE_DIM = 256
EPS = 1e-6

T_P = BATCH * SEQ
T_S = DEC_BATCH * DEC_SEQ
T_ALL = T_P + T_S
HALF = D_MODEL // 2
LANES = 128
VMEM_LIMIT = 48 * 1024 * 1024

BM_IN = 256
BQ = 2 * WINDOW
SB = 16
BM_R = 256
BM_SH = 512
BT_RANK = 512
BT_COMB = 256
BM_E = 512
N_ASSIGN = T_ALL * TOP_K
N_BLOCKS = N_ASSIGN // BM_E + N_EXPERTS
N_SLOTS = N_BLOCKS * BM_E
N_GROUPS = 2
T_GRP = T_ALL // N_GROUPS
assert T_GRP * N_GROUPS == T_ALL and N_ASSIGN % BM_E == 0

ROW_CHUNKS = HALF // LANES
SC_CORES = 2
SC_SUBCORES = 16
SC_WORKERS = SC_CORES * SC_SUBCORES
SC_CHUNK_D = 32
SC_CHUNK_G = 16
SC_RING = 4
assert T_ALL % (SC_WORKERS * SC_CHUNK_D) == 0 and T_GRP % (SC_WORKERS * SC_CHUNK_G) == 0


def _load_chunks(ref, n_rows, lead=()):
    return [ref[lead + (pl.ds(c, n_rows, stride=ROW_CHUNKS), slice(None))] for c in range(ROW_CHUNKS)]


def _store_chunks(ref, packed):
    n_rows = packed.shape[0]
    for c in range(ROW_CHUNKS):
        ref[pl.ds(c, n_rows, stride=ROW_CHUNKS), :] = packed[:, c * LANES:(c + 1) * LANES]


def _cparams(sem):
    return pltpu.CompilerParams(dimension_semantics=sem, vmem_limit_bytes=VMEM_LIMIT)


def _rms(x, g):
    return x * lax.rsqrt(jnp.mean(x * x, axis=-1, keepdims=True) + EPS) * g


def _sigmoid(x):
    return 1.0 / (1.0 + jnp.exp(-x))


def _pack_bf16_pairs(x):
    h = x.shape[-1] // 2
    return pltpu.pack_elementwise([x[:, :h], x[:, h:]], packed_dtype=BF16)


def _unpack_lo(p):
    return pltpu.bitcast(p << 16, F32)


def _unpack_hi(p):
    return pltpu.bitcast(p & jnp.uint32(0xFFFF0000), F32)


def _inproj_kernel(x_ref, g_ref, w_ref, c_ref, s1_ref, s2_ref, q_ref, k_ref, v_ref, u_ref):
    xn = _rms(x_ref[...], g_ref[...]).astype(BF16)
    z = jnp.dot(xn, w_ref[...], preferred_element_type=F32)
    c, s1, s2 = c_ref[...], s1_ref[...], s2_ref[...]

    def rope(t):
        return t * c + pltpu.roll(t, LANES - ROT_DIM // 2, 1) * s1 + pltpu.roll(t, ROT_DIM // 2, 1) * s2

    for i in range(ATT_WIDTH // LANES):
        sl = slice(i * LANES, (i + 1) * LANES)
        q_ref[:, sl] = (rope(z[:, sl]) * (HEAD_DIM ** -0.5)).astype(q_ref.dtype)
    k_ref[...] = rope(z[:, ATT_WIDTH:ATT_WIDTH + KV_WIDTH])
    v_ref[...] = z[:, ATT_WIDTH + KV_WIDTH:ATT_WIDTH + 2 * KV_WIDTH]
    u_ref[...] = z[:, ATT_WIDTH + 2 * KV_WIDTH:]


def _rope_tables(pos):
    half = ROT_DIM // 2
    inv = ROPE_THETA ** (-jnp.arange(half, dtype=F32) * 2.0 / ROT_DIM)
    ang = pos.astype(F32)[:, None] * inv[None, :]
    cos, sin = jnp.cos(ang), jnp.sin(ang)
    n = pos.shape[0]
    ones = jnp.ones((n, HEAD_DIM - ROT_DIM), F32)
    zeros = jnp.zeros((n, HEAD_DIM - ROT_DIM), F32)
    zh = jnp.zeros((n, half), F32)
    c = jnp.concatenate([cos, cos, ones], axis=1)
    s1 = jnp.concatenate([-sin, zh, zeros], axis=1)
    s2 = jnp.concatenate([zh, sin, zeros], axis=1)
    tile = lambda a: jnp.concatenate([a] * (LANES // HEAD_DIM), axis=1)
    return tile(c), tile(s1), tile(s2)


def _inproj(x2d, g_mix, w_in_bf, tables, q_dtype):
    rows = x2d.shape[0]
    n_tab = tables[0].shape[0] // BM_IN
    row_spec = lambda w: pl.BlockSpec((BM_IN, w), lambda i: (i, 0))
    tab_spec = pl.BlockSpec((BM_IN, LANES), lambda i: (i % n_tab, 0))
    full = lambda a: pl.BlockSpec(a.shape, lambda i: (0,) * a.ndim)
    return pl.pallas_call(
        _inproj_kernel,
        grid=(rows // BM_IN,),
        in_specs=[row_spec(D_MODEL), full(g_mix), full(w_in_bf), tab_spec, tab_spec, tab_spec],
        out_specs=[row_spec(ATT_WIDTH), row_spec(KV_WIDTH), row_spec(KV_WIDTH), row_spec(POOL_WIDTH)],
        out_shape=[jax.ShapeDtypeStruct((rows, ATT_WIDTH), q_dtype),
                   jax.ShapeDtypeStruct((rows, KV_WIDTH), F32),
                   jax.ShapeDtypeStruct((rows, KV_WIDTH), F32),
                   jax.ShapeDtypeStruct((rows, POOL_WIDTH), F32)],
        compiler_params=_cparams(("parallel",)),
        name="inproj",
    )(x2d, g_mix, w_in_bf, *tables)


def _sink_column(sinks_ref, kv_head, rows_per_head):
    n = GQA_GROUP * rows_per_head
    grp = lax.broadcasted_iota(I32, (n, 1), 0) // rows_per_head
    col = jnp.full((n, 1), sinks_ref[kv_head * GQA_GROUP], F32)
    for g in range(1, GQA_GROUP):
        col = jnp.where(grp == g, sinks_ref[kv_head * GQA_GROUP + g], col)
    return col


def _band_mask(n_rows, rows_per_head, n_keys):
    i = lax.broadcasted_iota(I32, (n_rows, n_keys), 0) % rows_per_head
    c = lax.broadcasted_iota(I32, (n_rows, n_keys), 1)
    return (c >= i) & (c <= i + WINDOW), c


def _stack_heads(q, kv_head):
    return jnp.concatenate(
        [q[:, (kv_head * GQA_GROUP + g) * HEAD_DIM:(kv_head * GQA_GROUP + g + 1) * HEAD_DIM]
         for g in range(GQA_GROUP)], axis=0)


def _nt_dot(a, b):
    return lax.dot_general(a, b, (((1,), (1,)), ((), ())), preferred_element_type=F32)


def _pool_delta(u, uext_ref, base, n, cnt_fn):
    parts = []
    for g, w in enumerate(POOL_WINDOWS):
        sl = slice(g * POOL_GROUP_DIM, (g + 1) * POOL_GROUP_DIM)
        acc = u[:, sl]
        for m in range(1, w):
            acc = acc + uext_ref[base - m:base - m + n, sl]
        parts.append(acc / cnt_fn(w) - u[:, sl])
    return parts


def _mixer_tail(o_att, d, h, wpool_ref, pscale_ref, gatt_ref, gpool_ref, wout_ref):
    parts = [jnp.dot(d[:, g * POOL_GROUP_DIM:(g + 1) * POOL_GROUP_DIM].astype(BF16), wpool_ref[g],
                     preferred_element_type=F32) for g in range(POOL_GROUPS)]
    o_pool = jnp.concatenate(parts, axis=-1) * pscale_ref[...]
    mixed = jnp.concatenate([_rms(o_att, gatt_ref[...]), _rms(o_pool, gpool_ref[...])], axis=-1)
    return h + jnp.dot(mixed.astype(BF16), wout_ref[...], preferred_element_type=F32)


def _mixer_prompt_kernel(sinks_ref, h_ref, q_ref, kc_ref, kp_ref, vc_ref, vp_ref, uc_ref, up_ref,
                         wpool_ref, pscale_ref, gatt_ref, gpool_ref, wout_ref, h1_ref, uext_ref):
    j = pl.program_id(1)
    q = q_ref[...]
    k_all = jnp.concatenate([kp_ref[...], kc_ref[...]], axis=0).astype(BF16)
    v_all = jnp.concatenate([vp_ref[...], vc_ref[...]], axis=0).astype(BF16)
    band, col = _band_mask(GQA_GROUP * WINDOW, WINDOW, 2 * WINDOW)
    sinks = [_sink_column(sinks_ref, hk, WINDOW) for hk in range(N_KV_HEADS)]
    bands = []
    for b in range(BQ // WINDOW):
        rows = slice(b * WINDOW, (b + 1) * WINDOW)
        keys = slice(b * WINDOW, (b + 2) * WINDOW)
        mask = band & ((col >= WINDOW) | (j > 0)) if b == 0 else band
        heads = []
        for hk in range(N_KV_HEADS):
            sl = slice(hk * HEAD_DIM, (hk + 1) * HEAD_DIM)
            s = jnp.where(mask, _nt_dot(_stack_heads(q[rows], hk), k_all[keys, sl]), -jnp.inf)
            m = jnp.maximum(jnp.max(s, axis=-1, keepdims=True), sinks[hk])
            e = jnp.exp(s - m)
            den = jnp.sum(e, axis=-1, keepdims=True) + jnp.exp(sinks[hk] - m)
            o = jnp.dot(e.astype(BF16), v_all[keys, sl], preferred_element_type=F32) / den
            heads += [o[g * WINDOW:(g + 1) * WINDOW] for g in range(GQA_GROUP)]
        bands.append(jnp.concatenate(heads, axis=-1))
    o_att = jnp.concatenate(bands, axis=0)

    u = uc_ref[...]
    uext_ref[0:16, :] = jnp.where(j > 0, up_ref[...], 0.0)
    uext_ref[16:16 + BQ, :] = u
    pos = j * BQ + lax.broadcasted_iota(I32, (BQ, 1), 0)
    d = jnp.concatenate(
        _pool_delta(u, uext_ref, 16, BQ, lambda w: jnp.minimum(pos + 1, w).astype(F32)), axis=-1)
    h1_ref[...] = _mixer_tail(o_att, d, h_ref[...], wpool_ref, pscale_ref, gatt_ref, gpool_ref, wout_ref)


def _mixer_sample_kernel(sinks_ref, h_ref, q_ref, kn_ref, vn_ref, u_ref, ck_ref, cv_ref, st_ref,
                         wpool_ref, pscale_ref, gatt_ref, gpool_ref, wout_ref, h1_in_ref,
                         h1_ref, ko_ref, vo_ref, po_ref, oatt_ref, d_ref, uext_ref):
    del h1_in_ref
    n_q = GQA_GROUP * DEC_SEQ
    mask_c, _ = _band_mask(n_q, DEC_SEQ, WINDOW)
    qi = lax.broadcasted_iota(I32, (n_q, DEC_SEQ), 0) % DEC_SEQ
    mask_n = lax.broadcasted_iota(I32, (n_q, DEC_SEQ), 1) <= qi
    sinks = [_sink_column(sinks_ref, hk, DEC_SEQ) for hk in range(N_KV_HEADS)]

    def one_batch(b, carry):
        r0 = pl.multiple_of(b * DEC_SEQ, DEC_SEQ)
        rows = pl.ds(r0, DEC_SEQ)
        qb = q_ref[rows, :]
        ck, cv = ck_ref[b], cv_ref[b]
        kn, vn = kn_ref[rows, :], vn_ref[rows, :]
        ko_ref[b, 0:WINDOW - DEC_SEQ, :] = ck[DEC_SEQ:, :]
        ko_ref[b, WINDOW - DEC_SEQ:WINDOW, :] = kn
        vo_ref[b, 0:WINDOW - DEC_SEQ, :] = cv[DEC_SEQ:, :]
        vo_ref[b, WINDOW - DEC_SEQ:WINDOW, :] = vn
        for hk in range(N_KV_HEADS):
            sl = slice(hk * HEAD_DIM, (hk + 1) * HEAD_DIM)
            qs = _stack_heads(qb, hk)
            s_c = jnp.where(mask_c, _nt_dot(qs.astype(BF16), ck[:, sl].astype(BF16)), -jnp.inf)
            s_n = jnp.where(mask_n, _nt_dot(qs, kn[:, sl]), -jnp.inf)
            m = jnp.maximum(jnp.maximum(jnp.max(s_c, axis=-1, keepdims=True),
                                        jnp.max(s_n, axis=-1, keepdims=True)), sinks[hk])
            e_c, e_n = jnp.exp(s_c - m), jnp.exp(s_n - m)
            den = (jnp.sum(e_c, axis=-1, keepdims=True) + jnp.sum(e_n, axis=-1, keepdims=True)
                   + jnp.exp(sinks[hk] - m))
            o = (jnp.dot(e_c.astype(BF16), cv[:, sl].astype(BF16), preferred_element_type=F32)
                 + jnp.dot(e_n, vn[:, sl], preferred_element_type=F32)) / den
            for g in range(GQA_GROUP):
                h0 = (hk * GQA_GROUP + g) * HEAD_DIM
                oatt_ref[rows, h0:h0 + HEAD_DIM] = o[g * DEC_SEQ:(g + 1) * DEC_SEQ]
        ub = u_ref[rows, :]
        uext_ref[1:16, :] = st_ref[b]
        uext_ref[16:16 + DEC_SEQ, :] = ub
        parts = _pool_delta(ub, uext_ref, 16, DEC_SEQ, lambda w: float(w))
        for g in range(POOL_GROUPS):
            d_ref[rows, g * POOL_GROUP_DIM:(g + 1) * POOL_GROUP_DIM] = parts[g]
        po_ref[b] = uext_ref[16 + DEC_SEQ - POOL_STATE:16 + DEC_SEQ, :]
        return carry

    lax.fori_loop(0, SB, one_batch, 0)
    h1_ref[...] = _mixer_tail(oatt_ref[...], d_ref[...], h_ref[...], wpool_ref, pscale_ref, gatt_ref,
                              gpool_ref, wout_ref)


def _full_spec(a, n_grid):
    nd = a.ndim
    return pl.BlockSpec(a.shape, lambda *_: (0,) * nd)


def _mixer_prompt(sinks, x2d, q, k, v, u, wts):
    nb = SEQ // BQ
    row = lambda w: pl.BlockSpec((BQ, w), lambda b, j: (b * nb + j, 0))
    prev = lambda w: pl.BlockSpec(
        (WINDOW, w), lambda b, j: (jnp.maximum((b * nb + j) * (BQ // WINDOW) - 1, 0), 0))
    uprev = pl.BlockSpec((16, POOL_WIDTH), lambda b, j: (jnp.maximum((b * nb + j) * (BQ // 16) - 1, 0), 0))
    smem = pl.BlockSpec(memory_space=pltpu.SMEM)
    return pl.pallas_call(
        _mixer_prompt_kernel,
        grid=(BATCH, nb),
        in_specs=[smem, row(D_MODEL), row(ATT_WIDTH), row(KV_WIDTH), prev(KV_WIDTH), row(KV_WIDTH),
                  prev(KV_WIDTH), row(POOL_WIDTH), uprev] + [_full_spec(w, 2) for w in wts],
        out_specs=row(D_MODEL),
        out_shape=jax.ShapeDtypeStruct((T_ALL, D_MODEL), F32),
        scratch_shapes=[pltpu.VMEM((16 + BQ, POOL_WIDTH), F32)],
        compiler_params=_cparams(("parallel", "parallel")),
        name="mixer_prompt",
    )(sinks, x2d, q, k, k, v, v, u, u, *wts)


def _mixer_sample(sinks, x2d, q, k, v, u, cache_k, cache_v, state, wts, h1_buf):
    rows = SB * DEC_SEQ
    row = lambda w: pl.BlockSpec((rows, w), lambda i: (i, 0))
    bat = lambda a: pl.BlockSpec((SB,) + a.shape[1:], lambda i: (i, 0, 0))
    smem = pl.BlockSpec(memory_space=pltpu.SMEM)
    h1_blocks_before = T_P // rows
    n_in = 9 + len(wts)
    return pl.pallas_call(
        _mixer_sample_kernel,
        grid=(DEC_BATCH // SB,),
        in_specs=[smem, row(D_MODEL), row(ATT_WIDTH), row(KV_WIDTH), row(KV_WIDTH), row(POOL_WIDTH),
                  bat(cache_k), bat(cache_v), bat(state)] + [_full_spec(w, 1) for w in wts]
                 + [pl.BlockSpec(memory_space=pl.ANY)],
        out_specs=[pl.BlockSpec((rows, D_MODEL), lambda i: (h1_blocks_before + i, 0)),
                   bat(cache_k), bat(cache_v), bat(state)],
        out_shape=[jax.ShapeDtypeStruct((T_ALL, D_MODEL), F32),
                   jax.ShapeDtypeStruct(cache_k.shape, F32),
                   jax.ShapeDtypeStruct(cache_v.shape, F32),
                   jax.ShapeDtypeStruct(state.shape, F32)],
        scratch_shapes=[pltpu.VMEM((rows, ATT_WIDTH), F32), pltpu.VMEM((rows, POOL_WIDTH), F32),
                        pltpu.VMEM((16 + DEC_SEQ, POOL_WIDTH), F32)],
        input_output_aliases={n_in: 0},
        compiler_params=_cparams(("parallel",)),
        name="mixer_sample",
    )(sinks, x2d, q, k, v, u, cache_k, cache_v, state, *wts, h1_buf)


def _first_max(vals, iota, n):
    m = jnp.max(vals, axis=0, keepdims=True)
    idx = jnp.min(jnp.where(vals == m, iota, n), axis=0, keepdims=True)
    return m, idx


def _router_kernel(h1_ref, gffn_ref, wrt_ref, bias_ref, xp_ref, idx_ref, wts_ref):
    xn = _rms(h1_ref[...], gffn_ref[...])
    logits = lax.dot_general(wrt_ref[...], xn, (((1,), (1,)), ((), ())),
                             precision=lax.Precision.HIGHEST, preferred_element_type=F32)
    scores = _sigmoid(logits)
    biased = scores + bias_ref[...]
    n_tok = biased.shape[1]
    neg = -jnp.inf

    iota_g = lax.broadcasted_iota(I32, (GROUP_SIZE, n_tok), 0)
    grp_rows = []
    for g in range(N_EXPERT_GROUPS):
        blk = biased[g * GROUP_SIZE:(g + 1) * GROUP_SIZE, :]
        top1, i1 = _first_max(blk, iota_g, GROUP_SIZE)
        top2 = jnp.max(jnp.where(iota_g == i1, neg, blk), axis=0, keepdims=True)
        grp_rows.append(top1 + top2)
    gs = jnp.concatenate(grp_rows, axis=0)

    iota_n = lax.broadcasted_iota(I32, (N_EXPERT_GROUPS, n_tok), 0)
    gsel = jnp.zeros((N_EXPERT_GROUPS, n_tok), jnp.bool_)
    for _ in range(TOPK_GROUPS):
        _, gi = _first_max(gs, iota_n, N_EXPERT_GROUPS)
        hit = iota_n == gi
        gsel = gsel | hit
        gs = jnp.where(hit, neg, gs)
    emask = jnp.concatenate(
        [jnp.broadcast_to(gsel[g:g + 1, :], (GROUP_SIZE, n_tok)) for g in range(N_EXPERT_GROUPS)], axis=0)
    masked = jnp.where(emask, biased, neg)

    iota_e = lax.broadcasted_iota(I32, (N_EXPERTS, n_tok), 0)
    idx_rows, sel_rows = [], []
    for _ in range(TOP_K):
        _, ei = _first_max(masked, iota_e, N_EXPERTS)
        hit = iota_e == ei
        idx_rows.append(ei)
        sel_rows.append(jnp.sum(jnp.where(hit, scores, 0.0), axis=0, keepdims=True))
        masked = jnp.where(hit, neg, masked)
    sel = jnp.concatenate(sel_rows, axis=0)
    idx_ref[...] = jnp.concatenate(idx_rows, axis=0)
    wts_ref[...] = sel / jnp.sum(sel, axis=0, keepdims=True) * ROUTED_SCALE
    _store_chunks(xp_ref, _pack_bf16_pairs(xn))


def _router(h1, g_ffn, w_router_t, bias_col):
    colblk = pl.BlockSpec((TOP_K, BM_R), lambda i: (0, i))
    ws = [g_ffn, w_router_t, bias_col]
    return pl.pallas_call(
        _router_kernel,
        grid=(T_ALL // BM_R,),
        in_specs=[pl.BlockSpec((BM_R, D_MODEL), lambda i: (i, 0))] + [_full_spec(w, 1) for w in ws],
        out_specs=[pl.BlockSpec((BM_R * ROW_CHUNKS, LANES), lambda i: (i, 0)), colblk, colblk],
        out_shape=[jax.ShapeDtypeStruct((T_ALL * ROW_CHUNKS, LANES), U32),
                   jax.ShapeDtypeStruct((TOP_K, T_ALL), I32),
                   jax.ShapeDtypeStruct((TOP_K, T_ALL), F32)],
        compiler_params=_cparams(("parallel",)),
        name="router",
    )(h1, *ws)


def _shared_kernel(h1_ref, gffn_ref, wgu_ref, wd_ref, hsh_ref):
    h1 = h1_ref[...]
    gu = jnp.dot(_rms(h1, gffn_ref[...]).astype(BF16), wgu_ref[...], preferred_element_type=F32)
    gate, up = gu[:, :D_SHARED], gu[:, D_SHARED:]
    hmid = (gate * _sigmoid(gate) * up).astype(BF16)
    hsh_ref[...] = h1 + jnp.dot(hmid, wd_ref[...], preferred_element_type=F32)


def _shared(h1, g_ffn, wgu, wd):
    row = pl.BlockSpec((BM_SH, D_MODEL), lambda i: (i, 0))
    ws = [g_ffn, wgu, wd]
    return pl.pallas_call(
        _shared_kernel,
        grid=(T_ALL // BM_SH,),
        in_specs=[row] + [_full_spec(w, 1) for w in ws],
        out_specs=row,
        out_shape=jax.ShapeDtypeStruct((T_ALL, D_MODEL), F32),
        compiler_params=_cparams(("parallel",)),
        name="shared",
    )(h1, *ws)


def _rank_kernel(idx_ref, tri_ref, rank_ref, cnt_ref, carry_ref):
    @pl.when(pl.program_id(0) == 0)
    def _():
        carry_ref[...] = jnp.zeros_like(carry_ref)

    idx = idx_ref[...]
    n_tok = idx.shape[1]
    iota_e = lax.broadcasted_iota(I32, (N_EXPERTS, n_tok), 0)
    member = jnp.zeros((N_EXPERTS, n_tok), F32)
    for k in range(TOP_K):
        member = member + jnp.where(iota_e == idx[k:k + 1, :], 1.0, 0.0)
    before = jnp.dot(member.astype(BF16), tri_ref[...], preferred_element_type=F32) + carry_ref[...]
    rows = [jnp.sum(jnp.where(iota_e == idx[k:k + 1, :], before, 0.0), axis=0, keepdims=True)
            for k in range(TOP_K)]
    rank_ref[...] = jnp.concatenate(rows, axis=0).astype(I32)
    carry_ref[...] = carry_ref[...] + jnp.sum(member, axis=1, keepdims=True)
    cnt_ref[...] = carry_ref[...].astype(I32)


def _rank(idx_t, tri):
    blk = pl.BlockSpec((TOP_K, BT_RANK), lambda i: (0, i))
    return pl.pallas_call(
        _rank_kernel,
        grid=(T_ALL // BT_RANK,),
        in_specs=[blk, _full_spec(tri, 1)],
        out_specs=[blk, pl.BlockSpec((N_EXPERTS, 1), lambda i: (0, 0))],
        out_shape=[jax.ShapeDtypeStruct((TOP_K, T_ALL), I32),
                   jax.ShapeDtypeStruct((N_EXPERTS, 1), I32)],
        scratch_shapes=[pltpu.VMEM((N_EXPERTS, 1), F32)],
        compiler_params=_cparams(("arbitrary",)),
        name="rank",
    )(idx_t, tri)


def _dest_kernel(idx_ref, rank_ref, cnt_ref, dest_ref, blk_e_ref, n_used_ref):
    counts = cnt_ref[...]
    padded = (counts + (BM_E - 1)) // BM_E * BM_E
    r = lax.broadcasted_iota(I32, (N_EXPERTS, N_EXPERTS), 0)
    c = lax.broadcasted_iota(I32, (N_EXPERTS, N_EXPERTS), 1)
    padded_row = jnp.sum(jnp.where(r == c, padded, 0), axis=0, keepdims=True)
    pad_start = jnp.sum(jnp.where(c < r, padded_row, 0), axis=1, keepdims=True)

    idx = idx_ref[...]
    n_tok = idx.shape[1]
    iota_e = lax.broadcasted_iota(I32, (N_EXPERTS, n_tok), 0)
    rows = [jnp.sum(jnp.where(iota_e == idx[k:k + 1, :], pad_start, 0), axis=0, keepdims=True)
            for k in range(TOP_K)]
    dest_ref[...] = jnp.concatenate(rows, axis=0) + rank_ref[...]

    pad_end_row = jnp.sum(jnp.where(r <= c, padded, 0), axis=0, keepdims=True)
    b0 = lax.broadcasted_iota(I32, (N_BLOCKS_PAD, N_EXPERTS), 0) * BM_E
    be = jnp.sum(jnp.where(pad_end_row <= b0, 1, 0), axis=1, keepdims=True)
    blk_e_ref[...] = jnp.minimum(be, N_EXPERTS - 1)
    n_used_ref[...] = pad_end_row[:, N_EXPERTS - 1:N_EXPERTS] // BM_E


N_BLOCKS_PAD = (N_BLOCKS + 7) // 8 * 8


def _dest(idx_t, rank_t, counts):
    blk = pl.BlockSpec((TOP_K, BT_RANK), lambda i: (0, i))
    one = lambda s: pl.BlockSpec(s, lambda i: (0, 0))
    return pl.pallas_call(
        _dest_kernel,
        grid=(T_ALL // BT_RANK,),
        in_specs=[blk, blk, one((N_EXPERTS, 1))],
        out_specs=[blk, one((N_BLOCKS_PAD, 1)), one((1, 1))],
        out_shape=[jax.ShapeDtypeStruct((TOP_K, T_ALL), I32),
                   jax.ShapeDtypeStruct((N_BLOCKS_PAD, 1), I32),
                   jax.ShapeDtypeStruct((1, 1), I32)],
        compiler_params=_cparams(("arbitrary",)),
        name="dest",
    )(idx_t, rank_t, counts)


def _sc_mesh():
    return plsc.VectorSubcoreMesh(core_axis_name="c", subcore_axis_name="s")


def _sc_worker_id():
    return lax.axis_index("s") * SC_CORES + lax.axis_index("c")


def _dispatch_body(dest_hbm, xp_hbm, xs_hbm, idx_v, rows_v, sem_in, sem_out):
    n_chunks, _, n_tok = dest_hbm.shape
    per_worker = n_chunks // SC_WORKERS
    chunk0 = _sc_worker_id() * per_worker

    def loads(i):
        chunk = chunk0 + i
        t0 = pl.multiple_of(chunk * n_tok, n_tok)
        return (pltpu.make_async_copy(dest_hbm.at[chunk], idx_v.at[i % 2], sem_in.at[i % 2]),
                pltpu.make_async_copy(xp_hbm.at[pl.ds(t0, n_tok)], rows_v.at[i % 2], sem_in.at[i % 2]))

    def scatters(i):
        return [pltpu.make_async_copy(rows_v.at[i % 2], xs_hbm.at[idx_v.at[i % 2, k]], sem_out.at[i % 2])
                for k in range(TOP_K)]

    for cp in loads(0):
        cp.start()
    for i in range(per_worker):
        for cp in loads(i):
            cp.wait()
        if i >= 1:
            for cp in scatters(i - 1):
                cp.wait()
        if i + 1 < per_worker:
            for cp in loads(i + 1):
                cp.start()
        for cp in scatters(i):
            cp.start()
    for cp in scatters(per_worker - 1):
        cp.wait()


def _dispatch(dest_chunks, xp3):
    return pl.kernel(
        _dispatch_body,
        out_type=jax.ShapeDtypeStruct((N_SLOTS, ROW_CHUNKS, LANES), U32),
        mesh=_sc_mesh(),
        scratch_types=[pltpu.VMEM((2, TOP_K, SC_CHUNK_D), I32),
                       pltpu.VMEM((2, SC_CHUNK_D, ROW_CHUNKS, LANES), U32),
                       pltpu.SemaphoreType.DMA((2,)), pltpu.SemaphoreType.DMA((2,))],
        name="dispatch",
    )(dest_chunks, xp3)


def _gather_body(dest_hbm, ys_hbm, yt_hbm, idx_v, rows_v, sem_in, sem_out):
    n_chunks, _, n_tok = dest_hbm.shape
    per_worker = n_chunks // SC_WORKERS
    chunk0 = _sc_worker_id() * per_worker

    @pl.loop(0, per_worker)
    def _(i):
        chunk = chunk0 + i
        t0 = pl.multiple_of(chunk * n_tok, n_tok)
        pltpu.sync_copy(dest_hbm.at[chunk], idx_v)

        def gather(k):
            return pltpu.make_async_copy(ys_hbm.at[idx_v.at[k]], rows_v.at[k % SC_RING], sem_in.at[k % SC_RING])

        def store(k):
            return pltpu.make_async_copy(rows_v.at[k % SC_RING], yt_hbm.at[k, pl.ds(t0, n_tok)],
                                         sem_out.at[k % SC_RING])

        for k in range(SC_RING):
            gather(k).start()
        for k in range(TOP_K):
            gather(k).wait()
            store(k).start()
            if k + SC_RING < TOP_K:
                store(k).wait()
                gather(k + SC_RING).start()
        for k in range(TOP_K - SC_RING, TOP_K):
            store(k).wait()


def _gather(dest_chunks, ys3):
    return pl.kernel(
        _gather_body,
        out_type=jax.ShapeDtypeStruct((TOP_K, T_GRP, ROW_CHUNKS, LANES), U32),
        mesh=_sc_mesh(),
        scratch_types=[pltpu.VMEM((TOP_K, SC_CHUNK_G), I32),
                       pltpu.VMEM((SC_RING, SC_CHUNK_G, ROW_CHUNKS, LANES), U32),
                       pltpu.SemaphoreType.DMA((SC_RING,)), pltpu.SemaphoreType.DMA((SC_RING,))],
        name="gather",
    )(dest_chunks, ys3)


def _experts_kernel(blk_e_ref, n_used_ref, xs_ref, wg_ref, wu_ref, wd_ref, ys_ref, wgu_s, wd_s):
    b = pl.program_id(0)

    @pl.when(b < n_used_ref[0])
    def _():
        prev = blk_e_ref[jnp.maximum(b - 1, 0)]

        @pl.when((b == 0) | (blk_e_ref[b] != prev))
        def _():
            wgu_s[:, :D_EXPERT] = wg_ref[0].astype(BF16)
            wgu_s[:, D_EXPERT:] = wu_ref[0].astype(BF16)
            wd_s[...] = wd_ref[0].astype(BF16)

        chunks = _load_chunks(xs_ref, BM_E)
        x_lo = jnp.concatenate([_unpack_lo(p) for p in chunks], axis=-1).astype(BF16)
        x_hi = jnp.concatenate([_unpack_hi(p) for p in chunks], axis=-1).astype(BF16)
        gu = (jnp.dot(x_lo, wgu_s[:HALF, :], preferred_element_type=F32)
              + jnp.dot(x_hi, wgu_s[HALF:, :], preferred_element_type=F32))
        gate, up = gu[:, :D_EXPERT], gu[:, D_EXPERT:]
        hmid = (gate * _sigmoid(gate) * up).astype(BF16)
        _store_chunks(ys_ref, _pack_bf16_pairs(jnp.dot(hmid, wd_s[...], preferred_element_type=F32)))


def _experts(blk_e, n_used, xs, wg, wu, wd):
    def blk(b, be, nu):
        return jnp.minimum(b, nu[0] - 1)

    grid_spec = pltpu.PrefetchScalarGridSpec(
        num_scalar_prefetch=2,
        grid=(N_BLOCKS,),
        in_specs=[pl.BlockSpec((BM_E * ROW_CHUNKS, LANES), lambda b, be, nu: (blk(b, be, nu), 0)),
                  pl.BlockSpec((1, D_MODEL, D_EXPERT), lambda b, be, nu: (be[blk(b, be, nu)], 0, 0)),
                  pl.BlockSpec((1, D_MODEL, D_EXPERT), lambda b, be, nu: (be[blk(b, be, nu)], 0, 0)),
                  pl.BlockSpec((1, D_EXPERT, D_MODEL), lambda b, be, nu: (be[blk(b, be, nu)], 0, 0))],
        out_specs=pl.BlockSpec((BM_E * ROW_CHUNKS, LANES), lambda b, be, nu: (blk(b, be, nu), 0)),
        scratch_shapes=[pltpu.VMEM((D_MODEL, 2 * D_EXPERT), BF16), pltpu.VMEM((D_EXPERT, D_MODEL), BF16)],
    )
    return pl.pallas_call(
        _experts_kernel,
        grid_spec=grid_spec,
        out_shape=jax.ShapeDtypeStruct((N_SLOTS * ROW_CHUNKS, LANES), U32),
        compiler_params=_cparams(("arbitrary",)),
        name="experts",
    )(blk_e, n_used, xs, wg, wu, wd)


def _combine_kernel(yt_ref, wts_ref, hsh_ref, p_ref, gple_ref, wpg_ref, wpp_ref, gfin_ref, *y_refs):
    y_ref = y_refs[-1]
    wts = wts_ref[...]
    lo = [jnp.zeros((BT_COMB, LANES), F32) for _ in range(ROW_CHUNKS)]
    hi = [jnp.zeros((BT_COMB, LANES), F32) for _ in range(ROW_CHUNKS)]
    for k in range(TOP_K):
        w = wts[:, k:k + 1]
        for c, p in enumerate(_load_chunks(yt_ref, BT_COMB, lead=(k,))):
            lo[c] = lo[c] + w * _unpack_lo(p)
            hi[c] = hi[c] + w * _unpack_hi(p)
    h2 = hsh_ref[...] + jnp.concatenate(lo + hi, axis=-1)
    gate = _sigmoid(jnp.dot(_rms(h2, gple_ref[...]).astype(BF16), wpg_ref[...], preferred_element_type=F32))
    proj = jnp.dot(p_ref[...].astype(BF16), wpp_ref[...], preferred_element_type=F32)
    y_ref[...] = _rms(h2 + proj * gate, gfin_ref[...])


def _combine(yt, yt_row0, wts_tok, hsh, tok_row0, n_rows, p2d, p_row0, ws, y_prev, out_rows, out_row0):
    assert all(r % BT_COMB == 0 for r in (yt_row0, tok_row0, n_rows, p_row0, out_row0))
    g0, t0, p0, o0 = yt_row0 // BT_COMB, tok_row0 // BT_COMB, p_row0 // BT_COMB, out_row0 // BT_COMB
    in_specs = [pl.BlockSpec((TOP_K, BT_COMB * ROW_CHUNKS, LANES), lambda i: (0, g0 + i, 0)),
                pl.BlockSpec((BT_COMB, TOP_K), lambda i: (t0 + i, 0)),
                pl.BlockSpec((BT_COMB, D_MODEL), lambda i: (t0 + i, 0)),
                pl.BlockSpec((BT_COMB, PLE_DIM), lambda i: (p0 + i, 0))] + [_full_spec(w, 1) for w in ws]
    args = [yt, wts_tok, hsh, p2d, *ws]
    aliases = {}
    if y_prev is not None:
        in_specs.append(pl.BlockSpec(memory_space=pl.ANY))
        aliases = {len(args): 0}
        args.append(y_prev)
    return pl.pallas_call(
        _combine_kernel,
        grid=(n_rows // BT_COMB,),
        in_specs=in_specs,
        out_specs=pl.BlockSpec((BT_COMB, D_MODEL), lambda i: (o0 + i, 0)),
        out_shape=jax.ShapeDtypeStruct((out_rows, D_MODEL), F32),
        input_output_aliases=aliases,
        compiler_params=_cparams(("parallel",)),
        name="combine",
    )(*args)


def kernel(x_prompt, x_sample, cache_k, cache_v, state_pool, p_prompt, p_sample, g_mix, w_in, attn_sinks,
           w_pool, pool_scale, g_att_out, g_pool_out, w_out, g_ffn, w_router, router_bias, w_exp_gate,
           w_exp_up, w_exp_down, w_sh_gate, w_sh_up, w_sh_down, g_ple, w_ple_gate, w_ple_proj, g_final):
    row = lambda a: a.reshape(1, -1)
    xp2d = x_prompt.reshape(T_P, D_MODEL)
    xs2d = x_sample.reshape(T_S, D_MODEL)
    w_in_bf = w_in[0].astype(BF16)
    mixer_wts = [w_pool[0].astype(BF16), row(pool_scale[0]), row(g_att_out[0]), row(g_pool_out[0]),
                 w_out[0].astype(BF16)]

    tab_p = _rope_tables(jnp.arange(SEQ, dtype=I32))
    pos_s = PAST_LEN + jnp.arange(DEC_SEQ, dtype=I32)
    tab_s = tuple(jnp.tile(t, (BM_IN // DEC_SEQ, 1)) for t in _rope_tables(pos_s))

    q_p, k_p, v_p, u_p = _inproj(xp2d, row(g_mix[0]), w_in_bf, tab_p, BF16)
    q_s, k_s, v_s, u_s = _inproj(xs2d, row(g_mix[0]), w_in_bf, tab_s, F32)

    h1 = _mixer_prompt(attn_sinks[0], xp2d, q_p, k_p, v_p, u_p, mixer_wts)
    h1, k_sample, v_sample, pool_sample = _mixer_sample(
        attn_sinks[0], xs2d, q_s, k_s, v_s, u_s,
        cache_k[0].reshape(DEC_BATCH, WINDOW, KV_WIDTH), cache_v[0].reshape(DEC_BATCH, WINDOW, KV_WIDTH),
        state_pool[0], mixer_wts, h1)

    g_ffn_row = row(g_ffn[0])
    xp, idx_t, wts_t = _router(h1, g_ffn_row, w_router[0].T, router_bias[0].reshape(N_EXPERTS, 1))
    tri = (lax.broadcasted_iota(I32, (BT_RANK, BT_RANK), 0)
           < lax.broadcasted_iota(I32, (BT_RANK, BT_RANK), 1)).astype(BF16)
    rank_t, counts = _rank(idx_t, tri)
    dest_t, blk_e, n_used = _dest(idx_t, rank_t, counts)

    def index_chunks(d, n_tok):
        return d.reshape(TOP_K, d.shape[1] // n_tok, n_tok).transpose(1, 0, 2)

    xs = _dispatch(index_chunks(dest_t, SC_CHUNK_D), xp.reshape(T_ALL, ROW_CHUNKS, LANES))
    hsh = _shared(h1, g_ffn_row, jnp.concatenate([w_sh_gate[0], w_sh_up[0]], axis=1).astype(BF16),
                  w_sh_down[0].astype(BF16))
    ys = _experts(blk_e.reshape(N_BLOCKS_PAD), n_used.reshape(1), xs.reshape(N_SLOTS * ROW_CHUNKS, LANES),
                  w_exp_gate[0], w_exp_up[0], w_exp_down[0])
    ys3 = ys.reshape(N_SLOTS, ROW_CHUNKS, LANES)

    wts_tok = wts_t.T
    ple_wts = [row(g_ple[0]), w_ple_gate[0].astype(BF16), w_ple_proj[0].astype(BF16), row(g_final)]
    pp2d = p_prompt[0].reshape(T_P, PLE_DIM)
    ps2d = p_sample[0].reshape(T_S, PLE_DIM)
    y_p = y_s = None
    for g in range(N_GROUPS):
        lo, hi = g * T_GRP, (g + 1) * T_GRP
        yt = _gather(index_chunks(dest_t[:, lo:hi], SC_CHUNK_G), ys3)
        yt = yt.reshape(TOP_K, T_GRP * ROW_CHUNKS, LANES)
        if lo < T_P:
            n = min(hi, T_P) - lo
            y_p = _combine(yt, 0, wts_tok, hsh, lo, n, pp2d, lo, ple_wts, y_p, T_P, lo)
        if hi > T_P:
            s0 = max(lo, T_P)
            y_s = _combine(yt, s0 - lo, wts_tok, hsh, s0, hi - s0, ps2d, s0 - T_P, ple_wts, y_s, T_S, s0 - T_P)

    kv5 = lambda a, b: a.reshape(1, b, WINDOW, N_KV_HEADS, HEAD_DIM)
    k_prompt = kv5(k_p.reshape(BATCH, SEQ, KV_WIDTH)[:, SEQ - WINDOW:], BATCH)
    v_prompt = kv5(v_p.reshape(BATCH, SEQ, KV_WIDTH)[:, SEQ - WINDOW:], BATCH)
    pool_prompt = u_p.reshape(BATCH, SEQ, POOL_WIDTH)[:, SEQ - POOL_STATE:][None]
    return (y_p.reshape(BATCH, SEQ, D_MODEL), y_s.reshape(DEC_BATCH, DEC_SEQ, D_MODEL),
            k_prompt, v_prompt, pool_prompt,
            kv5(k_sample, DEC_BATCH), kv5(v_sample, DEC_BATCH), pool_sample[None])
```

```python
import functools

import numpy as np
import jax
import jax.numpy as jnp
from jax import lax
from jax.experimental import pallas as pl
from jax.experimental.pallas import tpu as pltpu
from jax.experimental.pallas import tpu_sc as plsc

F32 = jnp.float32
BF16 = jnp.bfloat16
U32 = jnp.uint32
I32 = jnp.int32

D_MODEL = 1024
BATCH = 8
SEQ = 2048
DEC_BATCH = 128
DEC_SEQ = 8
PAST_LEN = 16384
N_Q_HEADS = 8
N_KV_HEADS = 2
HEAD_DIM = 64
GQA_GROUP = N_Q_HEADS // N_KV_HEADS
ATT_WIDTH = N_Q_HEADS * HEAD_DIM
KV_WIDTH = N_KV_HEADS * HEAD_DIM
WINDOW = 128
ROPE_THETA = 500000.0
ROT_DIM = HEAD_DIM // 4
POOL_WINDOWS = (2, 4, 8, 16)
POOL_GROUPS = 4
POOL_WIDTH = D_MODEL - ATT_WIDTH
POOL_GROUP_DIM = POOL_WIDTH // POOL_GROUPS
POOL_STATE = 15
IN_WIDTH = ATT_WIDTH + 2 * KV_WIDTH + POOL_WIDTH
N_EXPERTS = 64
TOP_K = 8
N_EXPERT_GROUPS = 8
GROUP_SIZE = N_EXPERTS // N_EXPERT_GROUPS
TOPK_GROUPS = 4
D_EXPERT = 256
D_SHARED = 256
ROUTED_SCALE = 2.5
PLE_DIM = 256
EPS = 1e-6

T_P = BATCH * SEQ
T_S = DEC_BATCH * DEC_SEQ
T_ALL = T_P + T_S
HALF = D_MODEL // 2
LANES = 128
VMEM_LIMIT = 48 * 1024 * 1024

BM_IN = 256
BQ = 2 * WINDOW
SB = 16
BM_R = 256
BM_SH = 512
BT_RANK = 1024
BT_COMB = 512
BM_E = 512
N_ASSIGN = T_ALL * TOP_K
N_BLOCKS = N_ASSIGN // BM_E + N_EXPERTS
N_SLOTS = N_BLOCKS * BM_E
N_GROUPS = 2
T_GRP = T_ALL // N_GROUPS
assert T_GRP * N_GROUPS == T_ALL and N_ASSIGN % BM_E == 0

ROW_CHUNKS = HALF // LANES
SC_CORES = 2
SC_SUBCORES = 16
SC_WORKERS = SC_CORES * SC_SUBCORES
SC_CHUNK_D = 32
SC_CHUNK_G = 16
SC_RING = 4
assert T_ALL % (SC_WORKERS * SC_CHUNK_D) == 0 and T_GRP % (SC_WORKERS * SC_CHUNK_G) == 0


def _load_chunks(ref, n_rows, lead=()):
    return [ref[lead + (pl.ds(c, n_rows, stride=ROW_CHUNKS), slice(None))] for c in range(ROW_CHUNKS)]


def _store_chunks(ref, packed):
    n_rows = packed.shape[0]
    for c in range(ROW_CHUNKS):
        ref[pl.ds(c, n_rows, stride=ROW_CHUNKS), :] = packed[:, c * LANES:(c + 1) * LANES]


def _cparams(sem):
    return pltpu.CompilerParams(dimension_semantics=sem, vmem_limit_bytes=VMEM_LIMIT)


def _rms(x, g):
    return x * lax.rsqrt(jnp.mean(x * x, axis=-1, keepdims=True) + EPS) * g


def _sigmoid(x):
    return 1.0 / (1.0 + jnp.exp(-x))


def _pack_bf16_pairs(x):
    h = x.shape[-1] // 2
    return pltpu.pack_elementwise([x[:, :h], x[:, h:]], packed_dtype=BF16)


def _unpack_lo(p):
    return pltpu.bitcast(p << 16, F32)


def _unpack_hi(p):
    return pltpu.bitcast(p & jnp.uint32(0xFFFF0000), F32)


def _inproj_kernel(x_ref, g_ref, w_ref, c_ref, s1_ref, s2_ref, q_ref, k_ref, v_ref, u_ref):
    xn = _rms(x_ref[...], g_ref[...]).astype(BF16)
    z = jnp.dot(xn, w_ref[...], preferred_element_type=F32)
    c, s1, s2 = c_ref[...], s1_ref[...], s2_ref[...]

    def rope(t):
        return t * c + pltpu.roll(t, LANES - ROT_DIM // 2, 1) * s1 + pltpu.roll(t, ROT_DIM // 2, 1) * s2

    for i in range(ATT_WIDTH // LANES):
        sl = slice(i * LANES, (i + 1) * LANES)
        q_ref[:, sl] = (rope(z[:, sl]) * (HEAD_DIM ** -0.5)).astype(q_ref.dtype)
    k_ref[...] = rope(z[:, ATT_WIDTH:ATT_WIDTH + KV_WIDTH])
    v_ref[...] = z[:, ATT_WIDTH + KV_WIDTH:ATT_WIDTH + 2 * KV_WIDTH]
    u_ref[...] = z[:, ATT_WIDTH + 2 * KV_WIDTH:]


def _rope_tables(pos, reps=1):
    f32 = np.float32
    half = ROT_DIM // 2
    inv = np.power(f32(ROPE_THETA), -np.arange(half, dtype=f32) * f32(2.0) / f32(ROT_DIM)).astype(f32)
    ang = np.asarray(pos, f32)[:, None] * inv[None, :]
    cos, sin = np.cos(ang).astype(f32), np.sin(ang).astype(f32)
    n = len(pos)
    ones = np.ones((n, HEAD_DIM - ROT_DIM), f32)
    zeros = np.zeros((n, HEAD_DIM - ROT_DIM), f32)
    zh = np.zeros((n, half), f32)
    c = np.concatenate([cos, cos, ones], axis=1)
    s1 = np.concatenate([-sin, zh, zeros], axis=1)
    s2 = np.concatenate([zh, sin, zeros], axis=1)
    tile = lambda a: np.tile(a, (reps, LANES // HEAD_DIM))
    return tile(c), tile(s1), tile(s2)


def _inproj(x2d, g_mix, w_in_bf, tables, q_dtype):
    rows = x2d.shape[0]
    n_tab = tables[0].shape[0] // BM_IN
    row_spec = lambda w: pl.BlockSpec((BM_IN, w), lambda i: (i, 0))
    tab_spec = pl.BlockSpec((BM_IN, LANES), lambda i: (i % n_tab, 0))
    full = lambda a: pl.BlockSpec(a.shape, lambda i: (0,) * a.ndim)
    return pl.pallas_call(
        _inproj_kernel,
        grid=(rows // BM_IN,),
        in_specs=[row_spec(D_MODEL), full(g_mix), full(w_in_bf), tab_spec, tab_spec, tab_spec],
        out_specs=[row_spec(ATT_WIDTH), row_spec(KV_WIDTH), row_spec(KV_WIDTH), row_spec(POOL_WIDTH)],
        out_shape=[jax.ShapeDtypeStruct((rows, ATT_WIDTH), q_dtype),
                   jax.ShapeDtypeStruct((rows, KV_WIDTH), F32),
                   jax.ShapeDtypeStruct((rows, KV_WIDTH), F32),
                   jax.ShapeDtypeStruct((rows, POOL_WIDTH), F32)],
        compiler_params=_cparams(("parallel",)),
        name="inproj",
    )(x2d, g_mix, w_in_bf, *tables)


def _sink_column(sinks_ref, kv_head, rows_per_head):
    n = GQA_GROUP * rows_per_head
    grp = lax.broadcasted_iota(I32, (n, 1), 0) // rows_per_head
    col = jnp.full((n, 1), sinks_ref[kv_head * GQA_GROUP], F32)
    for g in range(1, GQA_GROUP):
        col = jnp.where(grp == g, sinks_ref[kv_head * GQA_GROUP + g], col)
    return col


def _band_mask(n_rows, rows_per_head, n_keys):
    i = lax.broadcasted_iota(I32, (n_rows, n_keys), 0) % rows_per_head
    c = lax.broadcasted_iota(I32, (n_rows, n_keys), 1)
    return (c >= i) & (c <= i + WINDOW), c


def _stack_heads(q, kv_head):
    return jnp.concatenate(
        [q[:, (kv_head * GQA_GROUP + g) * HEAD_DIM:(kv_head * GQA_GROUP + g + 1) * HEAD_DIM]
         for g in range(GQA_GROUP)], axis=0)


def _nt_dot(a, b):
    return lax.dot_general(a, b, (((1,), (1,)), ((), ())), preferred_element_type=F32)


def _pool_delta(u, uext_ref, base, n, cnt_fn):
    parts = []
    for g, w in enumerate(POOL_WINDOWS):
        sl = slice(g * POOL_GROUP_DIM, (g + 1) * POOL_GROUP_DIM)
        acc = u[:, sl]
        for m in range(1, w):
            acc = acc + uext_ref[base - m:base - m + n, sl]
        parts.append(acc / cnt_fn(w) - u[:, sl])
    return parts


def _mixer_tail(o_att, d, h, wpool_ref, pscale_ref, gatt_ref, gpool_ref, wout_ref):
    parts = [jnp.dot(d[:, g * POOL_GROUP_DIM:(g + 1) * POOL_GROUP_DIM].astype(BF16), wpool_ref[g],
                     preferred_element_type=F32) for g in range(POOL_GROUPS)]
    o_pool = jnp.concatenate(parts, axis=-1) * pscale_ref[...]
    mixed = jnp.concatenate([_rms(o_att, gatt_ref[...]), _rms(o_pool, gpool_ref[...])], axis=-1)
    return h + jnp.dot(mixed.astype(BF16), wout_ref[...], preferred_element_type=F32)


def _mixer_prompt_kernel(sinks_ref, h_ref, q_ref, kc_ref, kp_ref, vc_ref, vp_ref, uc_ref, up_ref,
                         wpool_ref, pscale_ref, gatt_ref, gpool_ref, wout_ref, h1_ref, uext_ref):
    j = pl.program_id(1)
    q = q_ref[...]
    k_all = jnp.concatenate([kp_ref[...], kc_ref[...]], axis=0).astype(BF16)
    v_all = jnp.concatenate([vp_ref[...], vc_ref[...]], axis=0).astype(BF16)
    band, col = _band_mask(GQA_GROUP * WINDOW, WINDOW, 2 * WINDOW)
    sinks = [_sink_column(sinks_ref, hk, WINDOW) for hk in range(N_KV_HEADS)]
    bands = []
    for b in range(BQ // WINDOW):
        rows = slice(b * WINDOW, (b + 1) * WINDOW)
        keys = slice(b * WINDOW, (b + 2) * WINDOW)
        mask = band & ((col >= WINDOW) | (j > 0)) if b == 0 else band
        heads = []
        for hk in range(N_KV_HEADS):
            sl = slice(hk * HEAD_DIM, (hk + 1) * HEAD_DIM)
            s = jnp.where(mask, _nt_dot(_stack_heads(q[rows], hk), k_all[keys, sl]), -jnp.inf)
            m = jnp.maximum(jnp.max(s, axis=-1, keepdims=True), sinks[hk])
            e = jnp.exp(s - m)
            den = jnp.sum(e, axis=-1, keepdims=True) + jnp.exp(sinks[hk] - m)
            o = jnp.dot(e.astype(BF16), v_all[keys, sl], preferred_element_type=F32) / den
            heads += [o[g * WINDOW:(g + 1) * WINDOW] for g in range(GQA_GROUP)]
        bands.append(jnp.concatenate(heads, axis=-1))
    o_att = jnp.concatenate(bands, axis=0)

    u = uc_ref[...]
    uext_ref[0:16, :] = jnp.where(j > 0, up_ref[...], 0.0)
    uext_ref[16:16 + BQ, :] = u
    pos = j * BQ + lax.broadcasted_iota(I32, (BQ, 1), 0)
    d = jnp.concatenate(
        _pool_delta(u, uext_ref, 16, BQ, lambda w: jnp.minimum(pos + 1, w).astype(F32)), axis=-1)
    h1_ref[...] = _mixer_tail(o_att, d, h_ref[...], wpool_ref, pscale_ref, gatt_ref, gpool_ref, wout_ref)


def _mixer_sample_kernel(sinks_ref, h_ref, q_ref, kn_ref, vn_ref, u_ref, ck_ref, cv_ref, st_ref,
                         wpool_ref, pscale_ref, gatt_ref, gpool_ref, wout_ref, h1_in_ref,
                         h1_ref, ko_ref, vo_ref, po_ref, uext_ref):
    del h1_in_ref
    n_q = GQA_GROUP * DEC_SEQ
    n_keys = 2 * WINDOW
    band, col = _band_mask(n_q, DEC_SEQ, n_keys)
    mask = (band & (col < WINDOW + DEC_SEQ))[None]
    q3, kn3, vn3, u3 = q_ref[...], kn_ref[...], vn_ref[...], u_ref[...]
    ck, cv = ck_ref[...], cv_ref[...]
    ko_ref[:, 0:WINDOW - DEC_SEQ, :] = ck[:, DEC_SEQ:, :]
    ko_ref[:, WINDOW - DEC_SEQ:WINDOW, :] = kn3
    vo_ref[:, 0:WINDOW - DEC_SEQ, :] = cv[:, DEC_SEQ:, :]
    vo_ref[:, WINDOW - DEC_SEQ:WINDOW, :] = vn3
    pad = jnp.zeros((SB, WINDOW - DEC_SEQ, KV_WIDTH), F32)
    k_all = jnp.concatenate([ck, kn3, pad], axis=1).astype(BF16)
    v_all = jnp.concatenate([cv, vn3, pad], axis=1).astype(BF16)
    heads = []
    for hk in range(N_KV_HEADS):
        sl = slice(hk * HEAD_DIM, (hk + 1) * HEAD_DIM)
        qs = jnp.concatenate(
            [q3[:, :, (hk * GQA_GROUP + g) * HEAD_DIM:(hk * GQA_GROUP + g + 1) * HEAD_DIM]
             for g in range(GQA_GROUP)], axis=1).astype(BF16)
        sink = _sink_column(sinks_ref, hk, DEC_SEQ)[None]
        s = jnp.einsum("bqd,bkd->bqk", qs, k_all[:, :, sl], preferred_element_type=F32)
        s = jnp.where(mask, s, -jnp.inf)
        m = jnp.maximum(jnp.max(s, axis=-1, keepdims=True), sink)
        e = jnp.exp(s - m)
        den = jnp.sum(e, axis=-1, keepdims=True) + jnp.exp(sink - m)
        o = jnp.einsum("bqk,bkd->bqd", e.astype(BF16), v_all[:, :, sl], preferred_element_type=F32) / den
        heads += [o[:, g * DEC_SEQ:(g + 1) * DEC_SEQ, :] for g in range(GQA_GROUP)]
    o_att = jnp.concatenate(heads, axis=-1).reshape(SB * DEC_SEQ, ATT_WIDTH)

    uext_ref[:, 1:16, :] = st_ref[...]
    uext_ref[:, 16:16 + DEC_SEQ, :] = u3
    parts = []
    for g, w in enumerate(POOL_WINDOWS):
        sl = slice(g * POOL_GROUP_DIM, (g + 1) * POOL_GROUP_DIM)
        acc = u3[:, :, sl]
        for back in range(1, w):
            acc = acc + uext_ref[:, 16 - back:16 - back + DEC_SEQ, sl]
        parts.append(acc / float(w) - u3[:, :, sl])
    d = jnp.concatenate(parts, axis=-1).reshape(SB * DEC_SEQ, POOL_WIDTH)
    po_ref[...] = uext_ref[:, 16 + DEC_SEQ - POOL_STATE:16 + DEC_SEQ, :]
    h1_ref[...] = _mixer_tail(o_att, d, h_ref[...], wpool_ref, pscale_ref, gatt_ref, gpool_ref, wout_ref)


def _full_spec(a, n_grid):
    nd = a.ndim
    return pl.BlockSpec(a.shape, lambda *_: (0,) * nd)


def _mixer_prompt(sinks, x2d, q, k, v, u, wts):
    nb = SEQ // BQ
    row = lambda w: pl.BlockSpec((BQ, w), lambda b, j: (b * nb + j, 0))
    prev = lambda w: pl.BlockSpec(
        (WINDOW, w), lambda b, j: (jnp.maximum((b * nb + j) * (BQ // WINDOW) - 1, 0), 0))
    uprev = pl.BlockSpec((16, POOL_WIDTH), lambda b, j: (jnp.maximum((b * nb + j) * (BQ // 16) - 1, 0), 0))
    smem = pl.BlockSpec(memory_space=pltpu.SMEM)
    return pl.pallas_call(
        _mixer_prompt_kernel,
        grid=(BATCH, nb),
        in_specs=[smem, row(D_MODEL), row(ATT_WIDTH), row(KV_WIDTH), prev(KV_WIDTH), row(KV_WIDTH),
                  prev(KV_WIDTH), row(POOL_WIDTH), uprev] + [_full_spec(w, 2) for w in wts],
        out_specs=row(D_MODEL),
        out_shape=jax.ShapeDtypeStruct((T_ALL, D_MODEL), F32),
        scratch_shapes=[pltpu.VMEM((16 + BQ, POOL_WIDTH), F32)],
        compiler_params=_cparams(("parallel", "parallel")),
        name="mixer_prompt",
    )(sinks, x2d, q, k, k, v, v, u, u, *wts)


def _mixer_sample(sinks, x2d, q, k, v, u, cache_k, cache_v, state, wts, h1_buf):
    rows = SB * DEC_SEQ
    row = lambda w: pl.BlockSpec((rows, w), lambda i: (i, 0))
    bat = lambda a: pl.BlockSpec((SB,) + a.shape[1:], lambda i: (i, 0, 0))
    smem = pl.BlockSpec(memory_space=pltpu.SMEM)
    h1_blocks_before = T_P // rows
    n_in = 9 + len(wts)
    q, k, v, u = (a.reshape(DEC_BATCH, DEC_SEQ, a.shape[-1]) for a in (q, k, v, u))
    return pl.pallas_call(
        _mixer_sample_kernel,
        grid=(DEC_BATCH // SB,),
        in_specs=[smem, row(D_MODEL), bat(q), bat(k), bat(v), bat(u),
                  bat(cache_k), bat(cache_v), bat(state)] + [_full_spec(w, 1) for w in wts]
                 + [pl.BlockSpec(memory_space=pl.ANY)],
        out_specs=[pl.BlockSpec((rows, D_MODEL), lambda i: (h1_blocks_before + i, 0)),
                   bat(cache_k), bat(cache_v), bat(state)],
        out_shape=[jax.ShapeDtypeStruct((T_ALL, D_MODEL), F32),
                   jax.ShapeDtypeStruct(cache_k.shape, F32),
                   jax.ShapeDtypeStruct(cache_v.shape, F32),
                   jax.ShapeDtypeStruct(state.shape, F32)],
        scratch_shapes=[pltpu.VMEM((SB, 16 + DEC_SEQ, POOL_WIDTH), F32)],
        input_output_aliases={n_in: 0},
        compiler_params=_cparams(("parallel",)),
        name="mixer_sample",
    )(sinks, x2d, q, k, v, u, cache_k, cache_v, state, *wts, h1_buf)


def _first_max(vals, iota, n):
    m = jnp.max(vals, axis=0, keepdims=True)
    idx = jnp.min(jnp.where(vals == m, iota, n), axis=0, keepdims=True)
    return m, idx


def _router_kernel(h1_ref, gffn_ref, wrt_ref, bias_ref, xp_ref, idx_ref, wts_ref):
    xn = _rms(h1_ref[...], gffn_ref[...])
    logits = lax.dot_general(wrt_ref[...], xn, (((1,), (1,)), ((), ())),
                             precision=lax.Precision.HIGHEST, preferred_element_type=F32)
    scores = _sigmoid(logits)
    biased = scores + bias_ref[...]
    n_tok = biased.shape[1]
    neg = -jnp.inf

    iota_g = lax.broadcasted_iota(I32, (GROUP_SIZE, n_tok), 0)
    grp_rows = []
    for g in range(N_EXPERT_GROUPS):
        blk = biased[g * GROUP_SIZE:(g + 1) * GROUP_SIZE, :]
        top1, i1 = _first_max(blk, iota_g, GROUP_SIZE)
        top2 = jnp.max(jnp.where(iota_g == i1, neg, blk), axis=0, keepdims=True)
        grp_rows.append(top1 + top2)
    gs = jnp.concatenate(grp_rows, axis=0)

    iota_n = lax.broadcasted_iota(I32, (N_EXPERT_GROUPS, n_tok), 0)
    gsel = jnp.zeros((N_EXPERT_GROUPS, n_tok), jnp.bool_)
    for _ in range(TOPK_GROUPS):
        _, gi = _first_max(gs, iota_n, N_EXPERT_GROUPS)
        hit = iota_n == gi
        gsel = gsel | hit
        gs = jnp.where(hit, neg, gs)
    emask = jnp.concatenate(
        [jnp.broadcast_to(gsel[g:g + 1, :], (GROUP_SIZE, n_tok)) for g in range(N_EXPERT_GROUPS)], axis=0)
    masked = jnp.where(emask, biased, neg)

    iota_e = lax.broadcasted_iota(I32, (N_EXPERTS, n_tok), 0)
    idx_rows, sel_rows = [], []
    for _ in range(TOP_K):
        _, ei = _first_max(masked, iota_e, N_EXPERTS)
        hit = iota_e == ei
        idx_rows.append(ei)
        sel_rows.append(jnp.sum(jnp.where(hit, scores, 0.0), axis=0, keepdims=True))
        masked = jnp.where(hit, neg, masked)
    sel = jnp.concatenate(sel_rows, axis=0)
    idx_ref[...] = jnp.concatenate(idx_rows, axis=0)
    wts_ref[...] = sel / jnp.sum(sel, axis=0, keepdims=True) * ROUTED_SCALE
    _store_chunks(xp_ref, _pack_bf16_pairs(xn))


def _router(h1, g_ffn, w_router_t, bias_col):
    colblk = pl.BlockSpec((TOP_K, BM_R), lambda i: (0, i))
    ws = [g_ffn, w_router_t, bias_col]
    return pl.pallas_call(
        _router_kernel,
        grid=(T_ALL // BM_R,),
        in_specs=[pl.BlockSpec((BM_R, D_MODEL), lambda i: (i, 0))] + [_full_spec(w, 1) for w in ws],
        out_specs=[pl.BlockSpec((BM_R * ROW_CHUNKS, LANES), lambda i: (i, 0)), colblk, colblk],
        out_shape=[jax.ShapeDtypeStruct((T_ALL * ROW_CHUNKS, LANES), U32),
                   jax.ShapeDtypeStruct((TOP_K, T_ALL), I32),
                   jax.ShapeDtypeStruct((TOP_K, T_ALL), F32)],
        compiler_params=_cparams(("parallel",)),
        name="router",
    )(h1, *ws)


def _shared_kernel(h1_ref, gffn_ref, wgu_ref, wd_ref, hsh_ref):
    h1 = h1_ref[...]
    gu = jnp.dot(_rms(h1, gffn_ref[...]).astype(BF16), wgu_ref[...], preferred_element_type=F32)
    gate, up = gu[:, :D_SHARED], gu[:, D_SHARED:]
    hmid = (gate * _sigmoid(gate) * up).astype(BF16)
    hsh_ref[...] = h1 + jnp.dot(hmid, wd_ref[...], preferred_element_type=F32)


def _shared(h1, g_ffn, wgu, wd):
    row = pl.BlockSpec((BM_SH, D_MODEL), lambda i: (i, 0))
    ws = [g_ffn, wgu, wd]
    return pl.pallas_call(
        _shared_kernel,
        grid=(T_ALL // BM_SH,),
        in_specs=[row] + [_full_spec(w, 1) for w in ws],
        out_specs=row,
        out_shape=jax.ShapeDtypeStruct((T_ALL, D_MODEL), F32),
        compiler_params=_cparams(("parallel",)),
        name="shared",
    )(h1, *ws)


def _rank_kernel(idx_ref, tri_ref, rank_ref, cnt_ref, carry_ref):
    @pl.when(pl.program_id(0) == 0)
    def _():
        carry_ref[...] = jnp.zeros_like(carry_ref)

    idx = idx_ref[...]
    n_tok = idx.shape[1]
    iota_e = lax.broadcasted_iota(I32, (N_EXPERTS, n_tok), 0)
    member = jnp.zeros((N_EXPERTS, n_tok), F32)
    for k in range(TOP_K):
        member = member + jnp.where(iota_e == idx[k:k + 1, :], 1.0, 0.0)
    before = jnp.dot(member.astype(BF16), tri_ref[...], preferred_element_type=F32) + carry_ref[...]
    rows = [jnp.sum(jnp.where(iota_e == idx[k:k + 1, :], before, 0.0), axis=0, keepdims=True)
            for k in range(TOP_K)]
    rank_ref[...] = jnp.concatenate(rows, axis=0).astype(I32)
    carry_ref[...] = carry_ref[...] + jnp.sum(member, axis=1, keepdims=True)
    cnt_ref[...] = carry_ref[...].astype(I32)


def _rank(idx_t, tri):
    blk = pl.BlockSpec((TOP_K, BT_RANK), lambda i: (0, i))
    return pl.pallas_call(
        _rank_kernel,
        grid=(T_ALL // BT_RANK,),
        in_specs=[blk, _full_spec(tri, 1)],
        out_specs=[blk, pl.BlockSpec((N_EXPERTS, 1), lambda i: (0, 0))],
        out_shape=[jax.ShapeDtypeStruct((TOP_K, T_ALL), I32),
                   jax.ShapeDtypeStruct((N_EXPERTS, 1), I32)],
        scratch_shapes=[pltpu.VMEM((N_EXPERTS, 1), F32)],
        compiler_params=_cparams(("arbitrary",)),
        name="rank",
    )(idx_t, tri)


def _dest_kernel(idx_ref, rank_ref, cnt_ref, dest_ref, blk_e_ref, n_used_ref):
    counts = cnt_ref[...]
    padded = (counts + (BM_E - 1)) // BM_E * BM_E
    r = lax.broadcasted_iota(I32, (N_EXPERTS, N_EXPERTS), 0)
    c = lax.broadcasted_iota(I32, (N_EXPERTS, N_EXPERTS), 1)
    padded_row = jnp.sum(jnp.where(r == c, padded, 0), axis=0, keepdims=True)
    pad_start = jnp.sum(jnp.where(c < r, padded_row, 0), axis=1, keepdims=True)

    idx = idx_ref[...]
    n_tok = idx.shape[1]
    iota_e = lax.broadcasted_iota(I32, (N_EXPERTS, n_tok), 0)
    rows = [jnp.sum(jnp.where(iota_e == idx[k:k + 1, :], pad_start, 0), axis=0, keepdims=True)
            for k in range(TOP_K)]
    dest_ref[...] = jnp.concatenate(rows, axis=0) + rank_ref[...]

    pad_end_row = jnp.sum(jnp.where(r <= c, padded, 0), axis=0, keepdims=True)
    b0 = lax.broadcasted_iota(I32, (N_BLOCKS_PAD, N_EXPERTS), 0) * BM_E
    be = jnp.sum(jnp.where(pad_end_row <= b0, 1, 0), axis=1, keepdims=True)
    blk_e_ref[...] = jnp.minimum(be, N_EXPERTS - 1)
    n_used_ref[...] = pad_end_row[:, N_EXPERTS - 1:N_EXPERTS] // BM_E


N_BLOCKS_PAD = (N_BLOCKS + 7) // 8 * 8


def _dest(idx_t, rank_t, counts):
    blk = pl.BlockSpec((TOP_K, BT_RANK), lambda i: (0, i))
    one = lambda s: pl.BlockSpec(s, lambda i: (0, 0))
    return pl.pallas_call(
        _dest_kernel,
        grid=(T_ALL // BT_RANK,),
        in_specs=[blk, blk, one((N_EXPERTS, 1))],
        out_specs=[blk, one((N_BLOCKS_PAD, 1)), one((1, 1))],
        out_shape=[jax.ShapeDtypeStruct((TOP_K, T_ALL), I32),
                   jax.ShapeDtypeStruct((N_BLOCKS_PAD, 1), I32),
                   jax.ShapeDtypeStruct((1, 1), I32)],
        compiler_params=_cparams(("arbitrary",)),
        name="dest",
    )(idx_t, rank_t, counts)


def _sc_mesh():
    return plsc.VectorSubcoreMesh(core_axis_name="c", subcore_axis_name="s")


def _sc_worker_id():
    return lax.axis_index("s") * SC_CORES + lax.axis_index("c")


def _dispatch_body(dest_hbm, xp_hbm, xs_hbm, idx_v, rows_v, sem_in, sem_out):
    n_chunks, _, n_tok = dest_hbm.shape
    per_worker = n_chunks // SC_WORKERS
    chunk0 = _sc_worker_id() * per_worker

    def loads(i):
        chunk = chunk0 + i
        t0 = pl.multiple_of(chunk * n_tok, n_tok)
        return (pltpu.make_async_copy(dest_hbm.at[chunk], idx_v.at[i % 2], sem_in.at[i % 2]),
                pltpu.make_async_copy(xp_hbm.at[pl.ds(t0, n_tok)], rows_v.at[i % 2], sem_in.at[i % 2]))

    def scatters(i):
        return [pltpu.make_async_copy(rows_v.at[i % 2], xs_hbm.at[idx_v.at[i % 2, k]], sem_out.at[i % 2])
                for k in range(TOP_K)]

    for cp in loads(0):
        cp.start()
    for i in range(per_worker):
        for cp in loads(i):
            cp.wait()
        if i >= 1:
            for cp in scatters(i - 1):
                cp.wait()
        if i + 1 < per_worker:
            for cp in loads(i + 1):
                cp.start()
        for cp in scatters(i):
            cp.start()
    for cp in scatters(per_worker - 1):
        cp.wait()


def _dispatch(dest_chunks, xp3):
    return pl.kernel(
        _dispatch_body,
        out_type=jax.ShapeDtypeStruct((N_SLOTS, ROW_CHUNKS, LANES), U32),
        mesh=_sc_mesh(),
        scratch_types=[pltpu.VMEM((2, TOP_K, SC_CHUNK_D), I32),
                       pltpu.VMEM((2, SC_CHUNK_D, ROW_CHUNKS, LANES), U32),
                       pltpu.SemaphoreType.DMA((2,)), pltpu.SemaphoreType.DMA((2,))],
        name="dispatch",
    )(dest_chunks, xp3)


def _gather_body(dest_hbm, ys_hbm, yt_hbm, idx_v, rows_v, sem_in, sem_out):
    n_chunks, _, n_tok = dest_hbm.shape
    per_worker = n_chunks // SC_WORKERS
    chunk0 = _sc_worker_id() * per_worker

    @pl.loop(0, per_worker)
    def _(i):
        chunk = chunk0 + i
        t0 = pl.multiple_of(chunk * n_tok, n_tok)
        pltpu.sync_copy(dest_hbm.at[chunk], idx_v)

        def gather(k):
            return pltpu.make_async_copy(ys_hbm.at[idx_v.at[k]], rows_v.at[k % SC_RING], sem_in.at[k % SC_RING])

        def store(k):
            return pltpu.make_async_copy(rows_v.at[k % SC_RING], yt_hbm.at[k, pl.ds(t0, n_tok)],
                                         sem_out.at[k % SC_RING])

        for k in range(SC_RING):
            gather(k).start()
        for k in range(TOP_K):
            gather(k).wait()
            store(k).start()
            if k + SC_RING < TOP_K:
                store(k).wait()
                gather(k + SC_RING).start()
        for k in range(TOP_K - SC_RING, TOP_K):
            store(k).wait()


def _gather(dest_chunks, ys3):
    return pl.kernel(
        _gather_body,
        out_type=jax.ShapeDtypeStruct((TOP_K, T_GRP, ROW_CHUNKS, LANES), U32),
        mesh=_sc_mesh(),
        scratch_types=[pltpu.VMEM((TOP_K, SC_CHUNK_G), I32),
                       pltpu.VMEM((SC_RING, SC_CHUNK_G, ROW_CHUNKS, LANES), U32),
                       pltpu.SemaphoreType.DMA((SC_RING,)), pltpu.SemaphoreType.DMA((SC_RING,))],
        name="gather",
    )(dest_chunks, ys3)


def _experts_kernel(blk_e_ref, n_used_ref, xs_ref, wg_ref, wu_ref, wd_ref, ys_ref, wgu_s, wd_s):
    b = pl.program_id(0)

    @pl.when(b < n_used_ref[0])
    def _():
        prev = blk_e_ref[jnp.maximum(b - 1, 0)]

        @pl.when((b == 0) | (blk_e_ref[b] != prev))
        def _():
            wgu_s[:, :D_EXPERT] = wg_ref[0].astype(BF16)
            wgu_s[:, D_EXPERT:] = wu_ref[0].astype(BF16)
            wd_s[...] = wd_ref[0].astype(BF16)

        chunks = _load_chunks(xs_ref, BM_E)
        x_lo = jnp.concatenate([_unpack_lo(p) for p in chunks], axis=-1).astype(BF16)
        x_hi = jnp.concatenate([_unpack_hi(p) for p in chunks], axis=-1).astype(BF16)
        gu = (jnp.dot(x_lo, wgu_s[:HALF, :], preferred_element_type=F32)
              + jnp.dot(x_hi, wgu_s[HALF:, :], preferred_element_type=F32))
        gate, up = gu[:, :D_EXPERT], gu[:, D_EXPERT:]
        hmid = (gate * _sigmoid(gate) * up).astype(BF16)
        _store_chunks(ys_ref, _pack_bf16_pairs(jnp.dot(hmid, wd_s[...], preferred_element_type=F32)))


def _experts(blk_e, n_used, xs, wg, wu, wd):
    def blk(b, be, nu):
        return jnp.minimum(b, nu[0] - 1)

    grid_spec = pltpu.PrefetchScalarGridSpec(
        num_scalar_prefetch=2,
        grid=(N_BLOCKS,),
        in_specs=[pl.BlockSpec((BM_E * ROW_CHUNKS, LANES), lambda b, be, nu: (blk(b, be, nu), 0)),
                  pl.BlockSpec((1, D_MODEL, D_EXPERT), lambda b, be, nu: (be[blk(b, be, nu)], 0, 0)),
                  pl.BlockSpec((1, D_MODEL, D_EXPERT), lambda b, be, nu: (be[blk(b, be, nu)], 0, 0)),
                  pl.BlockSpec((1, D_EXPERT, D_MODEL), lambda b, be, nu: (be[blk(b, be, nu)], 0, 0))],
        out_specs=pl.BlockSpec((BM_E * ROW_CHUNKS, LANES), lambda b, be, nu: (blk(b, be, nu), 0)),
        scratch_shapes=[pltpu.VMEM((D_MODEL, 2 * D_EXPERT), BF16), pltpu.VMEM((D_EXPERT, D_MODEL), BF16)],
    )
    return pl.pallas_call(
        _experts_kernel,
        grid_spec=grid_spec,
        out_shape=jax.ShapeDtypeStruct((N_SLOTS * ROW_CHUNKS, LANES), U32),
        compiler_params=_cparams(("arbitrary",)),
        name="experts",
    )(blk_e, n_used, xs, wg, wu, wd)


def _combine_kernel(yt_ref, wts_ref, hsh_ref, p_ref, gple_ref, wpg_ref, wpp_ref, gfin_ref, *y_refs):
    y_ref = y_refs[-1]
    wts = jnp.transpose(wts_ref[...])
    lo = [jnp.zeros((BT_COMB, LANES), F32) for _ in range(ROW_CHUNKS)]
    hi = [jnp.zeros((BT_COMB, LANES), F32) for _ in range(ROW_CHUNKS)]
    for k in range(TOP_K):
        w = wts[:, k:k + 1]
        for c, p in enumerate(_load_chunks(yt_ref, BT_COMB, lead=(k,))):
            lo[c] = lo[c] + w * _unpack_lo(p)
            hi[c] = hi[c] + w * _unpack_hi(p)
    h2 = hsh_ref[...] + jnp.concatenate(lo + hi, axis=-1)
    gate = _sigmoid(jnp.dot(_rms(h2, gple_ref[...]).astype(BF16), wpg_ref[...], preferred_element_type=F32))
    proj = jnp.dot(p_ref[...].astype(BF16), wpp_ref[...], preferred_element_type=F32)
    y_ref[...] = _rms(h2 + proj * gate, gfin_ref[...])


def _combine(yt, yt_row0, wts_t, hsh, tok_row0, n_rows, p2d, p_row0, ws, y_prev, out_rows, out_row0):
    assert all(r % BT_COMB == 0 for r in (yt_row0, tok_row0, n_rows, p_row0, out_row0))
    g0, t0, p0, o0 = yt_row0 // BT_COMB, tok_row0 // BT_COMB, p_row0 // BT_COMB, out_row0 // BT_COMB
    in_specs = [pl.BlockSpec((TOP_K, BT_COMB * ROW_CHUNKS, LANES), lambda i: (0, g0 + i, 0)),
                pl.BlockSpec((TOP_K, BT_COMB), lambda i: (0, t0 + i)),
                pl.BlockSpec((BT_COMB, D_MODEL), lambda i: (t0 + i, 0)),
                pl.BlockSpec((BT_COMB, PLE_DIM), lambda i: (p0 + i, 0))] + [_full_spec(w, 1) for w in ws]
    args = [yt, wts_t, hsh, p2d, *ws]
    aliases = {}
    if y_prev is not None:
        in_specs.append(pl.BlockSpec(memory_space=pl.ANY))
        aliases = {len(args): 0}
        args.append(y_prev)
    return pl.pallas_call(
        _combine_kernel,
        grid=(n_rows // BT_COMB,),
        in_specs=in_specs,
        out_specs=pl.BlockSpec((BT_COMB, D_MODEL), lambda i: (o0 + i, 0)),
        out_shape=jax.ShapeDtypeStruct((out_rows, D_MODEL), F32),
        input_output_aliases=aliases,
        compiler_params=_cparams(("parallel",)),
        name="combine",
    )(*args)


def kernel(x_prompt, x_sample, cache_k, cache_v, state_pool, p_prompt, p_sample, g_mix, w_in, attn_sinks,
           w_pool, pool_scale, g_att_out, g_pool_out, w_out, g_ffn, w_router, router_bias, w_exp_gate,
           w_exp_up, w_exp_down, w_sh_gate, w_sh_up, w_sh_down, g_ple, w_ple_gate, w_ple_proj, g_final):
    row = lambda a: a.reshape(1, -1)
    xp2d = x_prompt.reshape(T_P, D_MODEL)
    xs2d = x_sample.reshape(T_S, D_MODEL)
    w_in_bf = w_in[0].astype(BF16)
    mixer_wts = [w_pool[0].astype(BF16), row(pool_scale[0]), row(g_att_out[0]), row(g_pool_out[0]),
                 w_out[0].astype(BF16)]

    tab_p = _rope_tables(np.arange(SEQ))
    tab_s = _rope_tables(PAST_LEN + np.arange(DEC_SEQ), reps=BM_IN // DEC_SEQ)

    q_p, k_p, v_p, u_p = _inproj(xp2d, row(g_mix[0]), w_in_bf, tab_p, BF16)
    q_s, k_s, v_s, u_s = _inproj(xs2d, row(g_mix[0]), w_in_bf, tab_s, F32)

    h1 = _mixer_prompt(attn_sinks[0], xp2d, q_p, k_p, v_p, u_p, mixer_wts)
    h1, k_sample, v_sample, pool_sample = _mixer_sample(
        attn_sinks[0], xs2d, q_s, k_s, v_s, u_s,
        cache_k[0].reshape(DEC_BATCH, WINDOW, KV_WIDTH), cache_v[0].reshape(DEC_BATCH, WINDOW, KV_WIDTH),
        state_pool[0], mixer_wts, h1)

    g_ffn_row = row(g_ffn[0])
    xp, idx_t, wts_t = _router(h1, g_ffn_row, w_router[0].T, router_bias[0].reshape(N_EXPERTS, 1))
    tri = (lax.broadcasted_iota(I32, (BT_RANK, BT_RANK), 0)
           < lax.broadcasted_iota(I32, (BT_RANK, BT_RANK), 1)).astype(BF16)
    rank_t, counts = _rank(idx_t, tri)
    dest_t, blk_e, n_used = _dest(idx_t, rank_t, counts)

    def index_chunks(d, n_tok):
        return d.reshape(TOP_K, d.shape[1] // n_tok, n_tok).transpose(1, 0, 2)

    xs = _dispatch(index_chunks(dest_t, SC_CHUNK_D), xp.reshape(T_ALL, ROW_CHUNKS, LANES))
    hsh = _shared(h1, g_ffn_row, jnp.concatenate([w_sh_gate[0], w_sh_up[0]], axis=1).astype(BF16),
                  w_sh_down[0].astype(BF16))
    ys = _experts(blk_e.reshape(N_BLOCKS_PAD), n_used.reshape(1), xs.reshape(N_SLOTS * ROW_CHUNKS, LANES),
                  w_exp_gate[0], w_exp_up[0], w_exp_down[0])
    ys3 = ys.reshape(N_SLOTS, ROW_CHUNKS, LANES)

    ple_wts = [row(g_ple[0]), w_ple_gate[0].astype(BF16), w_ple_proj[0].astype(BF16), row(g_final)]
    pp2d = p_prompt[0].reshape(T_P, PLE_DIM)
    ps2d = p_sample[0].reshape(T_S, PLE_DIM)
    y_p = y_s = None
    for g in range(N_GROUPS):
        lo, hi = g * T_GRP, (g + 1) * T_GRP
        yt = _gather(index_chunks(dest_t[:, lo:hi], SC_CHUNK_G), ys3)
        yt = yt.reshape(TOP_K, T_GRP * ROW_CHUNKS, LANES)
        if lo < T_P:
            n = min(hi, T_P) - lo
            y_p = _combine(yt, 0, wts_t, hsh, lo, n, pp2d, lo, ple_wts, y_p, T_P, lo)
        if hi > T_P:
            s0 = max(lo, T_P)
            y_s = _combine(yt, s0 - lo, wts_t, hsh, s0, hi - s0, ps2d, s0 - T_P, ple_wts, y_s, T_S, s0 - T_P)

    kv5 = lambda a, b: a.reshape(1, b, WINDOW, N_KV_HEADS, HEAD_DIM)
    k_prompt = kv5(k_p.reshape(BATCH, SEQ, KV_WIDTH)[:, SEQ - WINDOW:], BATCH)
    v_prompt = kv5(v_p.reshape(BATCH, SEQ, KV_WIDTH)[:, SEQ - WINDOW:], BATCH)
    pool_prompt = u_p.reshape(BATCH, SEQ, POOL_WIDTH)[:, SEQ - POOL_STATE:][None]
    return (y_p.reshape(BATCH, SEQ, D_MODEL), y_s.reshape(DEC_BATCH, DEC_SEQ, D_MODEL),
            k_prompt, v_prompt, pool_prompt,
            kv5(k_sample, DEC_BATCH), kv5(v_sample, DEC_BATCH), pool_sample[None])
```

```python
import functools

import numpy as np
import jax
import jax.numpy as jnp
from jax import lax
from jax.experimental import pallas as pl
from jax.experimental.pallas import tpu as pltpu
from jax.experimental.pallas import tpu_sc as plsc

F32 = jnp.float32
BF16 = jnp.bfloat16
U32 = jnp.uint32
I32 = jnp.int32

D_MODEL = 1024
BATCH = 8
SEQ = 2048
DEC_BATCH = 128
DEC_SEQ = 8
PAST_LEN = 16384
N_Q_HEADS = 8
N_KV_HEADS = 2
HEAD_DIM = 64
GQA_GROUP = N_Q_HEADS // N_KV_HEADS
ATT_WIDTH = N_Q_HEADS * HEAD_DIM
KV_WIDTH = N_KV_HEADS * HEAD_DIM
WINDOW = 128
ROPE_THETA = 500000.0
ROT_DIM = HEAD_DIM // 4
POOL_WINDOWS = (2, 4, 8, 16)
POOL_GROUPS = 4
POOL_WIDTH = D_MODEL - ATT_WIDTH
POOL_GROUP_DIM = POOL_WIDTH // POOL_GROUPS
POOL_STATE = 15
IN_WIDTH = ATT_WIDTH + 2 * KV_WIDTH + POOL_WIDTH
N_EXPERTS = 64
TOP_K = 8
N_EXPERT_GROUPS = 8
GROUP_SIZE = N_EXPERTS // N_EXPERT_GROUPS
TOPK_GROUPS = 4
D_EXPERT = 256
D_SHARED = 256
ROUTED_SCALE = 2.5
PLE_DIM = 256
EPS = 1e-6

T_P = BATCH * SEQ
T_S = DEC_BATCH * DEC_SEQ
T_ALL = T_P + T_S
HALF = D_MODEL // 2
LANES = 128
VMEM_LIMIT = 48 * 1024 * 1024

BM_IN = 256
BQ = 2 * WINDOW
SB = 16
BM_R = 256
BT_RANK = 1024
BT_COMB = 256
BM_E = 512
N_ASSIGN = T_ALL * TOP_K
N_BLOCKS = N_ASSIGN // BM_E + N_EXPERTS
N_SLOTS = N_BLOCKS * BM_E
N_GROUPS = 4
T_GRP = T_ALL // N_GROUPS
assert T_GRP * N_GROUPS == T_ALL and N_ASSIGN % BM_E == 0

ROW_CHUNKS = HALF // LANES
SC_CORES = 2
SC_SUBCORES = 16
SC_WORKERS = SC_CORES * SC_SUBCORES
SC_CHUNK_D = 32
SC_CHUNK_G = 8
SC_RING = 4
assert T_ALL % (SC_WORKERS * SC_CHUNK_D) == 0 and T_GRP % (SC_WORKERS * SC_CHUNK_G) == 0


def _load_chunks(ref, n_rows, lead=()):
    return [ref[lead + (pl.ds(c, n_rows, stride=ROW_CHUNKS), slice(None))] for c in range(ROW_CHUNKS)]


def _store_chunks(ref, packed):
    n_rows = packed.shape[0]
    for c in range(ROW_CHUNKS):
        ref[pl.ds(c, n_rows, stride=ROW_CHUNKS), :] = packed[:, c * LANES:(c + 1) * LANES]


def _cparams(sem):
    return pltpu.CompilerParams(dimension_semantics=sem, vmem_limit_bytes=VMEM_LIMIT)


def _rms(x, g):
    return x * lax.rsqrt(jnp.mean(x * x, axis=-1, keepdims=True) + EPS) * g


def _sigmoid(x):
    return 1.0 / (1.0 + jnp.exp(-x))


def _pack_bf16_pairs(x):
    h = x.shape[-1] // 2
    return pltpu.pack_elementwise([x[:, :h], x[:, h:]], packed_dtype=BF16)


def _unpack_lo(p):
    return pltpu.bitcast(p << 16, F32)


def _unpack_hi(p):
    return pltpu.bitcast(p & jnp.uint32(0xFFFF0000), F32)


def _inproj_kernel(x_ref, g_ref, w_ref, c_ref, s1_ref, s2_ref, q_ref, k_ref, v_ref, u_ref):
    xn = _rms(x_ref[...], g_ref[...]).astype(BF16)
    z = jnp.dot(xn, w_ref[...], preferred_element_type=F32)
    c, s1, s2 = c_ref[...], s1_ref[...], s2_ref[...]

    def rope(t):
        return t * c + pltpu.roll(t, LANES - ROT_DIM // 2, 1) * s1 + pltpu.roll(t, ROT_DIM // 2, 1) * s2

    for i in range(ATT_WIDTH // LANES):
        sl = slice(i * LANES, (i + 1) * LANES)
        q_ref[:, sl] = (rope(z[:, sl]) * (HEAD_DIM ** -0.5)).astype(q_ref.dtype)
    k_ref[...] = rope(z[:, ATT_WIDTH:ATT_WIDTH + KV_WIDTH])
    v_ref[...] = z[:, ATT_WIDTH + KV_WIDTH:ATT_WIDTH + 2 * KV_WIDTH]
    u_ref[...] = z[:, ATT_WIDTH + 2 * KV_WIDTH:]


def _rope_tables(pos, reps=1):
    f32 = np.float32
    half = ROT_DIM // 2
    inv = np.power(f32(ROPE_THETA), -np.arange(half, dtype=f32) * f32(2.0) / f32(ROT_DIM)).astype(f32)
    ang = np.asarray(pos, f32)[:, None] * inv[None, :]
    cos, sin = np.cos(ang).astype(f32), np.sin(ang).astype(f32)
    n = len(pos)
    ones = np.ones((n, HEAD_DIM - ROT_DIM), f32)
    zeros = np.zeros((n, HEAD_DIM - ROT_DIM), f32)
    zh = np.zeros((n, half), f32)
    c = np.concatenate([cos, cos, ones], axis=1)
    s1 = np.concatenate([-sin, zh, zeros], axis=1)
    s2 = np.concatenate([zh, sin, zeros], axis=1)
    tile = lambda a: np.tile(a, (reps, LANES // HEAD_DIM))
    return tile(c), tile(s1), tile(s2)


def _inproj(x2d, g_mix, w_in_bf, tables, q_dtype):
    rows = x2d.shape[0]
    n_tab = tables[0].shape[0] // BM_IN
    row_spec = lambda w: pl.BlockSpec((BM_IN, w), lambda i: (i, 0))
    tab_spec = pl.BlockSpec((BM_IN, LANES), lambda i: (i % n_tab, 0))
    full = lambda a: pl.BlockSpec(a.shape, lambda i: (0,) * a.ndim)
    return pl.pallas_call(
        _inproj_kernel,
        grid=(rows // BM_IN,),
        in_specs=[row_spec(D_MODEL), full(g_mix), full(w_in_bf), tab_spec, tab_spec, tab_spec],
        out_specs=[row_spec(ATT_WIDTH), row_spec(KV_WIDTH), row_spec(KV_WIDTH), row_spec(POOL_WIDTH)],
        out_shape=[jax.ShapeDtypeStruct((rows, ATT_WIDTH), q_dtype),
                   jax.ShapeDtypeStruct((rows, KV_WIDTH), F32),
                   jax.ShapeDtypeStruct((rows, KV_WIDTH), F32),
                   jax.ShapeDtypeStruct((rows, POOL_WIDTH), F32)],
        compiler_params=_cparams(("parallel",)),
        name="inproj",
    )(x2d, g_mix, w_in_bf, *tables)


def _sink_column(sinks_ref, kv_head, rows_per_head):
    n = GQA_GROUP * rows_per_head
    grp = lax.broadcasted_iota(I32, (n, 1), 0) // rows_per_head
    col = jnp.full((n, 1), sinks_ref[kv_head * GQA_GROUP], F32)
    for g in range(1, GQA_GROUP):
        col = jnp.where(grp == g, sinks_ref[kv_head * GQA_GROUP + g], col)
    return col


def _band_mask(n_rows, rows_per_head, n_keys):
    i = lax.broadcasted_iota(I32, (n_rows, n_keys), 0) % rows_per_head
    c = lax.broadcasted_iota(I32, (n_rows, n_keys), 1)
    return (c >= i) & (c <= i + WINDOW), c


def _stack_heads(q, kv_head):
    return jnp.concatenate(
        [q[:, (kv_head * GQA_GROUP + g) * HEAD_DIM:(kv_head * GQA_GROUP + g + 1) * HEAD_DIM]
         for g in range(GQA_GROUP)], axis=0)


def _nt_dot(a, b):
    return lax.dot_general(a, b, (((1,), (1,)), ((), ())), preferred_element_type=F32)


def _pool_delta(u, uext_ref, base, n, cnt_fn):
    parts = []
    for g, w in enumerate(POOL_WINDOWS):
        sl = slice(g * POOL_GROUP_DIM, (g + 1) * POOL_GROUP_DIM)
        acc = u[:, sl]
        for m in range(1, w):
            acc = acc + uext_ref[base - m:base - m + n, sl]
        parts.append(acc / cnt_fn(w) - u[:, sl])
    return parts


def _mixer_tail(o_att, d, h, wpool_ref, pscale_ref, gatt_ref, gpool_ref, wout_ref):
    parts = [jnp.dot(d[:, g * POOL_GROUP_DIM:(g + 1) * POOL_GROUP_DIM].astype(BF16), wpool_ref[g],
                     preferred_element_type=F32) for g in range(POOL_GROUPS)]
    o_pool = jnp.concatenate(parts, axis=-1) * pscale_ref[...]
    mixed = jnp.concatenate([_rms(o_att, gatt_ref[...]), _rms(o_pool, gpool_ref[...])], axis=-1)
    return h + jnp.dot(mixed.astype(BF16), wout_ref[...], preferred_element_type=F32)


def _mixer_prompt_kernel(sinks_ref, h_ref, q_ref, kc_ref, kp_ref, vc_ref, vp_ref, uc_ref, up_ref,
                         wpool_ref, pscale_ref, gatt_ref, gpool_ref, wout_ref, h1_ref, uext_ref):
    j = pl.program_id(1)
    q = q_ref[...]
    k_all = jnp.concatenate([kp_ref[...], kc_ref[...]], axis=0).astype(BF16)
    v_all = jnp.concatenate([vp_ref[...], vc_ref[...]], axis=0).astype(BF16)
    band, col = _band_mask(GQA_GROUP * WINDOW, WINDOW, 2 * WINDOW)
    sinks = [_sink_column(sinks_ref, hk, WINDOW) for hk in range(N_KV_HEADS)]
    bands = []
    for b in range(BQ // WINDOW):
        rows = slice(b * WINDOW, (b + 1) * WINDOW)
        keys = slice(b * WINDOW, (b + 2) * WINDOW)
        mask = band & ((col >= WINDOW) | (j > 0)) if b == 0 else band
        heads = []
        for hk in range(N_KV_HEADS):
            sl = slice(hk * HEAD_DIM, (hk + 1) * HEAD_DIM)
            s = jnp.where(mask, _nt_dot(_stack_heads(q[rows], hk), k_all[keys, sl]), -jnp.inf)
            m = jnp.maximum(jnp.max(s, axis=-1, keepdims=True), sinks[hk])
            e = jnp.exp(s - m)
            den = jnp.sum(e, axis=-1, keepdims=True) + jnp.exp(sinks[hk] - m)
            o = jnp.dot(e.astype(BF16), v_all[keys, sl], preferred_element_type=F32) / den
            heads += [o[g * WINDOW:(g + 1) * WINDOW] for g in range(GQA_GROUP)]
        bands.append(jnp.concatenate(heads, axis=-1))
    o_att = jnp.concatenate(bands, axis=0)

    u = uc_ref[...]
    uext_ref[0:16, :] = jnp.where(j > 0, up_ref[...], 0.0)
    uext_ref[16:16 + BQ, :] = u
    pos = j * BQ + lax.broadcasted_iota(I32, (BQ, 1), 0)
    d = jnp.concatenate(
        _pool_delta(u, uext_ref, 16, BQ, lambda w: jnp.minimum(pos + 1, w).astype(F32)), axis=-1)
    h1_ref[...] = _mixer_tail(o_att, d, h_ref[...], wpool_ref, pscale_ref, gatt_ref, gpool_ref, wout_ref)


def _mixer_sample_kernel(sinks_ref, h_ref, q_ref, kn_ref, vn_ref, u_ref, ck_ref, cv_ref, st_ref,
                         wpool_ref, pscale_ref, gatt_ref, gpool_ref, wout_ref, h1_in_ref,
                         h1_ref, ko_ref, vo_ref, po_ref, uext_ref):
    del h1_in_ref
    n_q = GQA_GROUP * DEC_SEQ
    n_keys = 2 * WINDOW
    band, col = _band_mask(n_q, DEC_SEQ, n_keys)
    mask = (band & (col < WINDOW + DEC_SEQ))[None]
    q3, kn3, vn3, u3 = q_ref[...], kn_ref[...], vn_ref[...], u_ref[...]
    ck, cv = ck_ref[...], cv_ref[...]
    ko_ref[:, 0:WINDOW - DEC_SEQ, :] = ck[:, DEC_SEQ:, :]
    ko_ref[:, WINDOW - DEC_SEQ:WINDOW, :] = kn3
    vo_ref[:, 0:WINDOW - DEC_SEQ, :] = cv[:, DEC_SEQ:, :]
    vo_ref[:, WINDOW - DEC_SEQ:WINDOW, :] = vn3
    pad = jnp.zeros((SB, WINDOW - DEC_SEQ, KV_WIDTH), F32)
    k_all = jnp.concatenate([ck, kn3, pad], axis=1).astype(BF16)
    v_all = jnp.concatenate([cv, vn3, pad], axis=1).astype(BF16)
    heads = []
    for hk in range(N_KV_HEADS):
        sl = slice(hk * HEAD_DIM, (hk + 1) * HEAD_DIM)
        qs = jnp.concatenate(
            [q3[:, :, (hk * GQA_GROUP + g) * HEAD_DIM:(hk * GQA_GROUP + g + 1) * HEAD_DIM]
             for g in range(GQA_GROUP)], axis=1).astype(BF16)
        sink = _sink_column(sinks_ref, hk, DEC_SEQ)[None]
        s = jnp.einsum("bqd,bkd->bqk", qs, k_all[:, :, sl], preferred_element_type=F32)
        s = jnp.where(mask, s, -jnp.inf)
        m = jnp.maximum(jnp.max(s, axis=-1, keepdims=True), sink)
        e = jnp.exp(s - m)
        den = jnp.sum(e, axis=-1, keepdims=True) + jnp.exp(sink - m)
        o = jnp.einsum("bqk,bkd->bqd", e.astype(BF16), v_all[:, :, sl], preferred_element_type=F32) / den
        heads += [o[:, g * DEC_SEQ:(g + 1) * DEC_SEQ, :] for g in range(GQA_GROUP)]
    o_att = jnp.concatenate(heads, axis=-1).reshape(SB * DEC_SEQ, ATT_WIDTH)

    uext_ref[:, 1:16, :] = st_ref[...]
    uext_ref[:, 16:16 + DEC_SEQ, :] = u3
    parts = []
    for g, w in enumerate(POOL_WINDOWS):
        sl = slice(g * POOL_GROUP_DIM, (g + 1) * POOL_GROUP_DIM)
        acc = u3[:, :, sl]
        for back in range(1, w):
            acc = acc + uext_ref[:, 16 - back:16 - back + DEC_SEQ, sl]
        parts.append(acc / float(w) - u3[:, :, sl])
    d = jnp.concatenate(parts, axis=-1).reshape(SB * DEC_SEQ, POOL_WIDTH)
    po_ref[...] = uext_ref[:, 16 + DEC_SEQ - POOL_STATE:16 + DEC_SEQ, :]
    h1_ref[...] = _mixer_tail(o_att, d, h_ref[...], wpool_ref, pscale_ref, gatt_ref, gpool_ref, wout_ref)


def _full_spec(a, n_grid):
    nd = a.ndim
    return pl.BlockSpec(a.shape, lambda *_: (0,) * nd)


def _mixer_prompt(sinks, x2d, q, k, v, u, wts):
    nb = SEQ // BQ
    row = lambda w: pl.BlockSpec((BQ, w), lambda b, j: (b * nb + j, 0))
    prev = lambda w: pl.BlockSpec(
        (WINDOW, w), lambda b, j: (jnp.maximum((b * nb + j) * (BQ // WINDOW) - 1, 0), 0))
    uprev = pl.BlockSpec((16, POOL_WIDTH), lambda b, j: (jnp.maximum((b * nb + j) * (BQ // 16) - 1, 0), 0))
    smem = pl.BlockSpec(memory_space=pltpu.SMEM)
    return pl.pallas_call(
        _mixer_prompt_kernel,
        grid=(BATCH, nb),
        in_specs=[smem, row(D_MODEL), row(ATT_WIDTH), row(KV_WIDTH), prev(KV_WIDTH), row(KV_WIDTH),
                  prev(KV_WIDTH), row(POOL_WIDTH), uprev] + [_full_spec(w, 2) for w in wts],
        out_specs=row(D_MODEL),
        out_shape=jax.ShapeDtypeStruct((T_ALL, D_MODEL), F32),
        scratch_shapes=[pltpu.VMEM((16 + BQ, POOL_WIDTH), F32)],
        compiler_params=_cparams(("parallel", "parallel")),
        name="mixer_prompt",
    )(sinks, x2d, q, k, k, v, v, u, u, *wts)


def _mixer_sample(sinks, x2d, q, k, v, u, cache_k, cache_v, state, wts, h1_buf):
    rows = SB * DEC_SEQ
    row = lambda w: pl.BlockSpec((rows, w), lambda i: (i, 0))
    bat = lambda a: pl.BlockSpec((SB,) + a.shape[1:], lambda i: (i, 0, 0))
    smem = pl.BlockSpec(memory_space=pltpu.SMEM)
    h1_blocks_before = T_P // rows
    n_in = 9 + len(wts)
    q, k, v, u = (a.reshape(DEC_BATCH, DEC_SEQ, a.shape[-1]) for a in (q, k, v, u))
    return pl.pallas_call(
        _mixer_sample_kernel,
        grid=(DEC_BATCH // SB,),
        in_specs=[smem, row(D_MODEL), bat(q), bat(k), bat(v), bat(u),
                  bat(cache_k), bat(cache_v), bat(state)] + [_full_spec(w, 1) for w in wts]
                 + [pl.BlockSpec(memory_space=pl.ANY)],
        out_specs=[pl.BlockSpec((rows, D_MODEL), lambda i: (h1_blocks_before + i, 0)),
                   bat(cache_k), bat(cache_v), bat(state)],
        out_shape=[jax.ShapeDtypeStruct((T_ALL, D_MODEL), F32),
                   jax.ShapeDtypeStruct(cache_k.shape, F32),
                   jax.ShapeDtypeStruct(cache_v.shape, F32),
                   jax.ShapeDtypeStruct(state.shape, F32)],
        scratch_shapes=[pltpu.VMEM((SB, 16 + DEC_SEQ, POOL_WIDTH), F32)],
        input_output_aliases={n_in: 0},
        compiler_params=_cparams(("parallel",)),
        name="mixer_sample",
    )(sinks, x2d, q, k, v, u, cache_k, cache_v, state, *wts, h1_buf)


def _first_max(vals, iota, n):
    m = jnp.max(vals, axis=0, keepdims=True)
    idx = jnp.min(jnp.where(vals == m, iota, n), axis=0, keepdims=True)
    return m, idx


def _router_kernel(h1_ref, gffn_ref, wrt_ref, bias_ref, xp_ref, idx_ref, wts_ref):
    xn = _rms(h1_ref[...], gffn_ref[...])
    logits = lax.dot_general(wrt_ref[...], xn, (((1,), (1,)), ((), ())),
                             precision=lax.Precision.HIGHEST, preferred_element_type=F32)
    scores = _sigmoid(logits)
    biased = scores + bias_ref[...]
    n_tok = biased.shape[1]
    neg = -jnp.inf

    iota_g = lax.broadcasted_iota(I32, (GROUP_SIZE, n_tok), 0)
    grp_rows = []
    for g in range(N_EXPERT_GROUPS):
        blk = biased[g * GROUP_SIZE:(g + 1) * GROUP_SIZE, :]
        top1, i1 = _first_max(blk, iota_g, GROUP_SIZE)
        top2 = jnp.max(jnp.where(iota_g == i1, neg, blk), axis=0, keepdims=True)
        grp_rows.append(top1 + top2)
    gs = jnp.concatenate(grp_rows, axis=0)

    iota_n = lax.broadcasted_iota(I32, (N_EXPERT_GROUPS, n_tok), 0)
    gsel = jnp.zeros((N_EXPERT_GROUPS, n_tok), jnp.bool_)
    for _ in range(TOPK_GROUPS):
        _, gi = _first_max(gs, iota_n, N_EXPERT_GROUPS)
        hit = iota_n == gi
        gsel = gsel | hit
        gs = jnp.where(hit, neg, gs)
    emask = jnp.concatenate(
        [jnp.broadcast_to(gsel[g:g + 1, :], (GROUP_SIZE, n_tok)) for g in range(N_EXPERT_GROUPS)], axis=0)
    masked = jnp.where(emask, biased, neg)

    iota_e = lax.broadcasted_iota(I32, (N_EXPERTS, n_tok), 0)
    idx_rows, sel_rows = [], []
    for _ in range(TOP_K):
        _, ei = _first_max(masked, iota_e, N_EXPERTS)
        hit = iota_e == ei
        idx_rows.append(ei)
        sel_rows.append(jnp.sum(jnp.where(hit, scores, 0.0), axis=0, keepdims=True))
        masked = jnp.where(hit, neg, masked)
    sel = jnp.concatenate(sel_rows, axis=0)
    idx_ref[...] = jnp.concatenate(idx_rows, axis=0)
    wts_ref[...] = sel / jnp.sum(sel, axis=0, keepdims=True) * ROUTED_SCALE
    _store_chunks(xp_ref, _pack_bf16_pairs(xn))


def _router(h1, g_ffn, w_router_t, bias_col):
    colblk = pl.BlockSpec((TOP_K, BM_R), lambda i: (0, i))
    ws = [g_ffn, w_router_t, bias_col]
    return pl.pallas_call(
        _router_kernel,
        grid=(T_ALL // BM_R,),
        in_specs=[pl.BlockSpec((BM_R, D_MODEL), lambda i: (i, 0))] + [_full_spec(w, 1) for w in ws],
        out_specs=[pl.BlockSpec((BM_R * ROW_CHUNKS, LANES), lambda i: (i, 0)), colblk, colblk],
        out_shape=[jax.ShapeDtypeStruct((T_ALL * ROW_CHUNKS, LANES), U32),
                   jax.ShapeDtypeStruct((TOP_K, T_ALL), I32),
                   jax.ShapeDtypeStruct((TOP_K, T_ALL), F32)],
        compiler_params=_cparams(("parallel",)),
        name="router",
    )(h1, *ws)


def _rank_kernel(idx_ref, tri_ref, rank_ref, cnt_ref, carry_ref):
    @pl.when(pl.program_id(0) == 0)
    def _():
        carry_ref[...] = jnp.zeros_like(carry_ref)

    idx = idx_ref[...]
    n_tok = idx.shape[1]
    iota_e = lax.broadcasted_iota(I32, (N_EXPERTS, n_tok), 0)
    member = jnp.zeros((N_EXPERTS, n_tok), F32)
    for k in range(TOP_K):
        member = member + jnp.where(iota_e == idx[k:k + 1, :], 1.0, 0.0)
    before = jnp.dot(member.astype(BF16), tri_ref[...], preferred_element_type=F32) + carry_ref[...]
    rows = [jnp.sum(jnp.where(iota_e == idx[k:k + 1, :], before, 0.0), axis=0, keepdims=True)
            for k in range(TOP_K)]
    rank_ref[...] = jnp.concatenate(rows, axis=0).astype(I32)
    carry_ref[...] = carry_ref[...] + jnp.sum(member, axis=1, keepdims=True)
    cnt_ref[...] = carry_ref[...].astype(I32)


def _rank(idx_t, tri):
    blk = pl.BlockSpec((TOP_K, BT_RANK), lambda i: (0, i))
    return pl.pallas_call(
        _rank_kernel,
        grid=(T_ALL // BT_RANK,),
        in_specs=[blk, _full_spec(tri, 1)],
        out_specs=[blk, pl.BlockSpec((N_EXPERTS, 1), lambda i: (0, 0))],
        out_shape=[jax.ShapeDtypeStruct((TOP_K, T_ALL), I32),
                   jax.ShapeDtypeStruct((N_EXPERTS, 1), I32)],
        scratch_shapes=[pltpu.VMEM((N_EXPERTS, 1), F32)],
        compiler_params=_cparams(("arbitrary",)),
        name="rank",
    )(idx_t, tri)


def _dest_kernel(idx_ref, rank_ref, cnt_ref, dest_ref, blk_e_ref, n_used_ref):
    counts = cnt_ref[...]
    padded = (counts + (BM_E - 1)) // BM_E * BM_E
    r = lax.broadcasted_iota(I32, (N_EXPERTS, N_EXPERTS), 0)
    c = lax.broadcasted_iota(I32, (N_EXPERTS, N_EXPERTS), 1)
    padded_row = jnp.sum(jnp.where(r == c, padded, 0), axis=0, keepdims=True)
    pad_start = jnp.sum(jnp.where(c < r, padded_row, 0), axis=1, keepdims=True)

    idx = idx_ref[...]
    n_tok = idx.shape[1]
    iota_e = lax.broadcasted_iota(I32, (N_EXPERTS, n_tok), 0)
    rows = [jnp.sum(jnp.where(iota_e == idx[k:k + 1, :], pad_start, 0), axis=0, keepdims=True)
            for k in range(TOP_K)]
    dest_ref[...] = jnp.concatenate(rows, axis=0) + rank_ref[...]

    pad_end_row = jnp.sum(jnp.where(r <= c, padded, 0), axis=0, keepdims=True)
    b0 = lax.broadcasted_iota(I32, (N_BLOCKS_PAD, N_EXPERTS), 0) * BM_E
    be = jnp.sum(jnp.where(pad_end_row <= b0, 1, 0), axis=1, keepdims=True)
    blk_e_ref[...] = jnp.minimum(be, N_EXPERTS - 1)
    n_used_ref[...] = pad_end_row[:, N_EXPERTS - 1:N_EXPERTS] // BM_E


N_BLOCKS_PAD = (N_BLOCKS + 7) // 8 * 8


def _dest(idx_t, rank_t, counts):
    blk = pl.BlockSpec((TOP_K, BT_RANK), lambda i: (0, i))
    one = lambda s: pl.BlockSpec(s, lambda i: (0, 0))
    return pl.pallas_call(
        _dest_kernel,
        grid=(T_ALL // BT_RANK,),
        in_specs=[blk, blk, one((N_EXPERTS, 1))],
        out_specs=[blk, one((N_BLOCKS_PAD, 1)), one((1, 1))],
        out_shape=[jax.ShapeDtypeStruct((TOP_K, T_ALL), I32),
                   jax.ShapeDtypeStruct((N_BLOCKS_PAD, 1), I32),
                   jax.ShapeDtypeStruct((1, 1), I32)],
        compiler_params=_cparams(("arbitrary",)),
        name="dest",
    )(idx_t, rank_t, counts)


def _sc_mesh():
    return plsc.VectorSubcoreMesh(core_axis_name="c", subcore_axis_name="s")


def _sc_worker_id():
    return lax.axis_index("s") * SC_CORES + lax.axis_index("c")


def _dispatch_body(dest_hbm, xp_hbm, xs_hbm, idx_v, rows_v, sem_in, sem_out):
    n_chunks, _, n_tok = dest_hbm.shape
    per_worker = n_chunks // SC_WORKERS
    chunk0 = _sc_worker_id() * per_worker

    def loads(i):
        chunk = chunk0 + i
        t0 = pl.multiple_of(chunk * n_tok, n_tok)
        return (pltpu.make_async_copy(dest_hbm.at[chunk], idx_v.at[i % 2], sem_in.at[i % 2]),
                pltpu.make_async_copy(xp_hbm.at[pl.ds(t0, n_tok)], rows_v.at[i % 2], sem_in.at[i % 2]))

    def scatters(i):
        return [pltpu.make_async_copy(rows_v.at[i % 2], xs_hbm.at[idx_v.at[i % 2, k]], sem_out.at[i % 2])
                for k in range(TOP_K)]

    for cp in loads(0):
        cp.start()
    for i in range(per_worker):
        for cp in loads(i):
            cp.wait()
        if i >= 1:
            for cp in scatters(i - 1):
                cp.wait()
        if i + 1 < per_worker:
            for cp in loads(i + 1):
                cp.start()
        for cp in scatters(i):
            cp.start()
    for cp in scatters(per_worker - 1):
        cp.wait()


def _dispatch(dest_chunks, xp3):
    return pl.kernel(
        _dispatch_body,
        out_type=jax.ShapeDtypeStruct((N_SLOTS, ROW_CHUNKS, LANES), U32),
        mesh=_sc_mesh(),
        scratch_types=[pltpu.VMEM((2, TOP_K, SC_CHUNK_D), I32),
                       pltpu.VMEM((2, SC_CHUNK_D, ROW_CHUNKS, LANES), U32),
                       pltpu.SemaphoreType.DMA((2,)), pltpu.SemaphoreType.DMA((2,))],
        name="dispatch",
    )(dest_chunks, xp3)


def _gather_body(dest_hbm, ys_hbm, yt_hbm, idx_v, rows_v, sem_in, sem_out):
    n_chunks, _, n_tok = dest_hbm.shape
    per_worker = n_chunks // SC_WORKERS
    chunk0 = _sc_worker_id() * per_worker

    @pl.loop(0, per_worker)
    def _(i):
        chunk = chunk0 + i
        t0 = pl.multiple_of(chunk * n_tok, n_tok)
        pltpu.sync_copy(dest_hbm.at[chunk], idx_v)

        def gather(k):
            return pltpu.make_async_copy(ys_hbm.at[idx_v.at[k]], rows_v.at[k % SC_RING], sem_in.at[k % SC_RING])

        def store(k):
            return pltpu.make_async_copy(rows_v.at[k % SC_RING], yt_hbm.at[k, pl.ds(t0, n_tok)],
                                         sem_out.at[k % SC_RING])

        for k in range(SC_RING):
            gather(k).start()
        for k in range(TOP_K):
            gather(k).wait()
            store(k).start()
            if k + SC_RING < TOP_K:
                store(k).wait()
                gather(k + SC_RING).start()
        for k in range(TOP_K - SC_RING, TOP_K):
            store(k).wait()


def _gather(dest_chunks, ys3):
    return pl.kernel(
        _gather_body,
        out_type=jax.ShapeDtypeStruct((TOP_K, T_GRP, ROW_CHUNKS, LANES), U32),
        mesh=_sc_mesh(),
        scratch_types=[pltpu.VMEM((TOP_K, SC_CHUNK_G), I32),
                       pltpu.VMEM((SC_RING, SC_CHUNK_G, ROW_CHUNKS, LANES), U32),
                       pltpu.SemaphoreType.DMA((SC_RING,)), pltpu.SemaphoreType.DMA((SC_RING,))],
        name="gather",
    )(dest_chunks, ys3)


def _experts_kernel(blk_e_ref, n_used_ref, xs_ref, wg_ref, wu_ref, wd_ref, ys_ref, wgu_s, wd_s):
    b = pl.program_id(0)

    @pl.when(b < n_used_ref[0])
    def _():
        prev = blk_e_ref[jnp.maximum(b - 1, 0)]

        @pl.when((b == 0) | (blk_e_ref[b] != prev))
        def _():
            wgu_s[:, :D_EXPERT] = wg_ref[0].astype(BF16)
            wgu_s[:, D_EXPERT:] = wu_ref[0].astype(BF16)
            wd_s[...] = wd_ref[0].astype(BF16)

        chunks = _load_chunks(xs_ref, BM_E)
        x_lo = jnp.concatenate([_unpack_lo(p) for p in chunks], axis=-1).astype(BF16)
        x_hi = jnp.concatenate([_unpack_hi(p) for p in chunks], axis=-1).astype(BF16)
        gu = (jnp.dot(x_lo, wgu_s[:HALF, :], preferred_element_type=F32)
              + jnp.dot(x_hi, wgu_s[HALF:, :], preferred_element_type=F32))
        gate, up = gu[:, :D_EXPERT], gu[:, D_EXPERT:]
        hmid = (gate * _sigmoid(gate) * up).astype(BF16)
        _store_chunks(ys_ref, _pack_bf16_pairs(jnp.dot(hmid, wd_s[...], preferred_element_type=F32)))


def _experts(blk_e, n_used, xs, wg, wu, wd):
    def blk(b, be, nu):
        return jnp.minimum(b, nu[0] - 1)

    grid_spec = pltpu.PrefetchScalarGridSpec(
        num_scalar_prefetch=2,
        grid=(N_BLOCKS,),
        in_specs=[pl.BlockSpec((BM_E * ROW_CHUNKS, LANES), lambda b, be, nu: (blk(b, be, nu), 0)),
                  pl.BlockSpec((1, D_MODEL, D_EXPERT), lambda b, be, nu: (be[blk(b, be, nu)], 0, 0)),
                  pl.BlockSpec((1, D_MODEL, D_EXPERT), lambda b, be, nu: (be[blk(b, be, nu)], 0, 0)),
                  pl.BlockSpec((1, D_EXPERT, D_MODEL), lambda b, be, nu: (be[blk(b, be, nu)], 0, 0))],
        out_specs=pl.BlockSpec((BM_E * ROW_CHUNKS, LANES), lambda b, be, nu: (blk(b, be, nu), 0)),
        scratch_shapes=[pltpu.VMEM((D_MODEL, 2 * D_EXPERT), BF16), pltpu.VMEM((D_EXPERT, D_MODEL), BF16)],
    )
    return pl.pallas_call(
        _experts_kernel,
        grid_spec=grid_spec,
        out_shape=jax.ShapeDtypeStruct((N_SLOTS * ROW_CHUNKS, LANES), U32),
        compiler_params=_cparams(("arbitrary",)),
        name="experts",
    )(blk_e, n_used, xs, wg, wu, wd)


def _combine_kernel(yt_ref, wts_ref, h1_ref, p_ref, gffn_ref, wsgu_ref, wsd_ref, gple_ref, wpg_ref, wpp_ref,
                    gfin_ref, *y_refs):
    y_ref = y_refs[-1]
    h1 = h1_ref[...]
    gu = jnp.dot(_rms(h1, gffn_ref[...]).astype(BF16), wsgu_ref[...], preferred_element_type=F32)
    sgate, sup = gu[:, :D_SHARED], gu[:, D_SHARED:]
    hsh = h1 + jnp.dot((sgate * _sigmoid(sgate) * sup).astype(BF16), wsd_ref[...], preferred_element_type=F32)
    wts = jnp.transpose(wts_ref[...])
    lo = [jnp.zeros((BT_COMB, LANES), F32) for _ in range(ROW_CHUNKS)]
    hi = [jnp.zeros((BT_COMB, LANES), F32) for _ in range(ROW_CHUNKS)]
    for k in range(TOP_K):
        w = wts[:, k:k + 1]
        for c, p in enumerate(_load_chunks(yt_ref, BT_COMB, lead=(k,))):
            lo[c] = lo[c] + w * _unpack_lo(p)
            hi[c] = hi[c] + w * _unpack_hi(p)
    h2 = hsh + jnp.concatenate(lo + hi, axis=-1)
    gate = _sigmoid(jnp.dot(_rms(h2, gple_ref[...]).astype(BF16), wpg_ref[...], preferred_element_type=F32))
    proj = jnp.dot(p_ref[...].astype(BF16), wpp_ref[...], preferred_element_type=F32)
    y_ref[...] = _rms(h2 + proj * gate, gfin_ref[...])


def _combine(yt, yt_row0, wts_t, h1, tok_row0, n_rows, p2d, p_row0, ws, y_prev, out_rows, out_row0):
    assert all(r % BT_COMB == 0 for r in (yt_row0, tok_row0, n_rows, p_row0, out_row0))
    g0, t0, p0, o0 = yt_row0 // BT_COMB, tok_row0 // BT_COMB, p_row0 // BT_COMB, out_row0 // BT_COMB
    in_specs = [pl.BlockSpec((TOP_K, BT_COMB * ROW_CHUNKS, LANES), lambda i: (0, g0 + i, 0)),
                pl.BlockSpec((TOP_K, BT_COMB), lambda i: (0, t0 + i)),
                pl.BlockSpec((BT_COMB, D_MODEL), lambda i: (t0 + i, 0)),
                pl.BlockSpec((BT_COMB, PLE_DIM), lambda i: (p0 + i, 0))] + [_full_spec(w, 1) for w in ws]
    args = [yt, wts_t, h1, p2d, *ws]
    aliases = {}
    if y_prev is not None:
        in_specs.append(pl.BlockSpec(memory_space=pl.ANY))
        aliases = {len(args): 0}
        args.append(y_prev)
    return pl.pallas_call(
        _combine_kernel,
        grid=(n_rows // BT_COMB,),
        in_specs=in_specs,
        out_specs=pl.BlockSpec((BT_COMB, D_MODEL), lambda i: (o0 + i, 0)),
        out_shape=jax.ShapeDtypeStruct((out_rows, D_MODEL), F32),
        input_output_aliases=aliases,
        compiler_params=_cparams(("parallel",)),
        name="combine",
    )(*args)


def kernel(x_prompt, x_sample, cache_k, cache_v, state_pool, p_prompt, p_sample, g_mix, w_in, attn_sinks,
           w_pool, pool_scale, g_att_out, g_pool_out, w_out, g_ffn, w_router, router_bias, w_exp_gate,
           w_exp_up, w_exp_down, w_sh_gate, w_sh_up, w_sh_down, g_ple, w_ple_gate, w_ple_proj, g_final):
    row = lambda a: a.reshape(1, -1)
    xp2d = x_prompt.reshape(T_P, D_MODEL)
    xs2d = x_sample.reshape(T_S, D_MODEL)
    w_in_bf = w_in[0].astype(BF16)
    mixer_wts = [w_pool[0].astype(BF16), row(pool_scale[0]), row(g_att_out[0]), row(g_pool_out[0]),
                 w_out[0].astype(BF16)]

    tab_p = _rope_tables(np.arange(SEQ))
    tab_s = _rope_tables(PAST_LEN + np.arange(DEC_SEQ), reps=BM_IN // DEC_SEQ)

    q_p, k_p, v_p, u_p = _inproj(xp2d, row(g_mix[0]), w_in_bf, tab_p, BF16)
    q_s, k_s, v_s, u_s = _inproj(xs2d, row(g_mix[0]), w_in_bf, tab_s, F32)

    h1 = _mixer_prompt(attn_sinks[0], xp2d, q_p, k_p, v_p, u_p, mixer_wts)
    h1, k_sample, v_sample, pool_sample = _mixer_sample(
        attn_sinks[0], xs2d, q_s, k_s, v_s, u_s,
        cache_k[0].reshape(DEC_BATCH, WINDOW, KV_WIDTH), cache_v[0].reshape(DEC_BATCH, WINDOW, KV_WIDTH),
        state_pool[0], mixer_wts, h1)

    g_ffn_row = row(g_ffn[0])
    xp, idx_t, wts_t = _router(h1, g_ffn_row, w_router[0].T, router_bias[0].reshape(N_EXPERTS, 1))
    tri = (lax.broadcasted_iota(I32, (BT_RANK, BT_RANK), 0)
           < lax.broadcasted_iota(I32, (BT_RANK, BT_RANK), 1)).astype(BF16)
    rank_t, counts = _rank(idx_t, tri)
    dest_t, blk_e, n_used = _dest(idx_t, rank_t, counts)

    def index_chunks(d, n_tok):
        return d.reshape(TOP_K, d.shape[1] // n_tok, n_tok).transpose(1, 0, 2)

    xs = _dispatch(index_chunks(dest_t, SC_CHUNK_D), xp.reshape(T_ALL, ROW_CHUNKS, LANES))
    ys = _experts(blk_e.reshape(N_BLOCKS_PAD), n_used.reshape(1), xs.reshape(N_SLOTS * ROW_CHUNKS, LANES),
                  w_exp_gate[0], w_exp_up[0], w_exp_down[0])
    ys3 = ys.reshape(N_SLOTS, ROW_CHUNKS, LANES)

    ple_wts = [g_ffn_row, jnp.concatenate([w_sh_gate[0], w_sh_up[0]], axis=1).astype(BF16),
               w_sh_down[0].astype(BF16),
               row(g_ple[0]), w_ple_gate[0].astype(BF16), w_ple_proj[0].astype(BF16), row(g_final)]
    pp2d = p_prompt[0].reshape(T_P, PLE_DIM)
    ps2d = p_sample[0].reshape(T_S, PLE_DIM)
    y_p = y_s = None
    for g in range(N_GROUPS):
        lo, hi = g * T_GRP, (g + 1) * T_GRP
        yt = _gather(index_chunks(dest_t[:, lo:hi], SC_CHUNK_G), ys3)
        yt = yt.reshape(TOP_K, T_GRP * ROW_CHUNKS, LANES)
        if lo < T_P:
            n = min(hi, T_P) - lo
            y_p = _combine(yt, 0, wts_t, h1, lo, n, pp2d, lo, ple_wts, y_p, T_P, lo)
        if hi > T_P:
            s0 = max(lo, T_P)
            y_s = _combine(yt, s0 - lo, wts_t, h1, s0, hi - s0, ps2d, s0 - T_P, ple_wts, y_s, T_S, s0 - T_P)

    kv5 = lambda a, b: a.reshape(1, b, WINDOW, N_KV_HEADS, HEAD_DIM)
    k_prompt = kv5(k_p.reshape(BATCH, SEQ, KV_WIDTH)[:, SEQ - WINDOW:], BATCH)
    v_prompt = kv5(v_p.reshape(BATCH, SEQ, KV_WIDTH)[:, SEQ - WINDOW:], BATCH)
    pool_prompt = u_p.reshape(BATCH, SEQ, POOL_WIDTH)[:, SEQ - POOL_STATE:][None]
    return (y_p.reshape(BATCH, SEQ, D_MODEL), y_s.reshape(DEC_BATCH, DEC_SEQ, D_MODEL),
            k_prompt, v_prompt, pool_prompt,
            kv5(k_sample, DEC_BATCH), kv5(v_sample, DEC_BATCH), pool_sample[None])
```

```python
import functools

import numpy as np
import jax
import jax.numpy as jnp
from jax import lax
from jax.experimental import pallas as pl
from jax.experimental.pallas import tpu as pltpu
from jax.experimental.pallas import tpu_sc as plsc

F32 = jnp.float32
BF16 = jnp.bfloat16
U32 = jnp.uint32
I32 = jnp.int32

D_MODEL = 1024
BATCH = 8
SEQ = 2048
DEC_BATCH = 128
DEC_SEQ = 8
PAST_LEN = 16384
N_Q_HEADS = 8
N_KV_HEADS = 2
HEAD_DIM = 64
GQA_GROUP = N_Q_HEADS // N_KV_HEADS
ATT_WIDTH = N_Q_HEADS * HEAD_DIM
KV_WIDTH = N_KV_HEADS * HEAD_DIM
WINDOW = 128
ROPE_THETA = 500000.0
ROT_DIM = HEAD_DIM // 4
POOL_WINDOWS = (2, 4, 8, 16)
POOL_GROUPS = 4
POOL_WIDTH = D_MODEL - ATT_WIDTH
POOL_GROUP_DIM = POOL_WIDTH // POOL_GROUPS
POOL_STATE = 15
IN_WIDTH = ATT_WIDTH + 2 * KV_WIDTH + POOL_WIDTH
N_EXPERTS = 64
TOP_K = 8
N_EXPERT_GROUPS = 8
GROUP_SIZE = N_EXPERTS // N_EXPERT_GROUPS
TOPK_GROUPS = 4
D_EXPERT = 256
D_SHARED = 256
ROUTED_SCALE = 2.5
PLE_DIM = 256
EPS = 1e-6

T_P = BATCH * SEQ
T_S = DEC_BATCH * DEC_SEQ
T_ALL = T_P + T_S
HALF = D_MODEL // 2
LANES = 128
VMEM_LIMIT = 48 * 1024 * 1024

BM_IN = 256
BQ = 2 * WINDOW
SB = 16
BM_R = 256
BT_RANK = 1024
BT_COMB = 512
BM_E = 1280
N_ASSIGN = T_ALL * TOP_K
N_BLOCKS = -(-N_ASSIGN // BM_E) + N_EXPERTS
N_SLOTS = N_BLOCKS * BM_E
N_GROUPS = 2
T_GRP = T_ALL // N_GROUPS
assert T_GRP * N_GROUPS == T_ALL

ROW_CHUNKS = HALF // LANES
SC_CORES = 2
SC_SUBCORES = 16
SC_WORKERS = SC_CORES * SC_SUBCORES
SC_CHUNK_D = 32
SC_CHUNK_G = 16
SC_RING = 4
assert T_ALL % (SC_WORKERS * SC_CHUNK_D) == 0 and T_GRP % (SC_WORKERS * SC_CHUNK_G) == 0


def _load_chunks(ref, n_rows, lead=()):
    return [ref[lead + (pl.ds(c, n_rows, stride=ROW_CHUNKS), slice(None))] for c in range(ROW_CHUNKS)]


def _store_chunks(ref, packed):
    n_rows = packed.shape[0]
    for c in range(ROW_CHUNKS):
        ref[pl.ds(c, n_rows, stride=ROW_CHUNKS), :] = packed[:, c * LANES:(c + 1) * LANES]


def _cparams(sem):
    return pltpu.CompilerParams(dimension_semantics=sem, vmem_limit_bytes=VMEM_LIMIT)


def _rms(x, g):
    return x * lax.rsqrt(jnp.mean(x * x, axis=-1, keepdims=True) + EPS) * g


def _sigmoid(x):
    return 1.0 / (1.0 + jnp.exp(-x))


def _pack_bf16_pairs(x):
    h = x.shape[-1] // 2
    return pltpu.pack_elementwise([x[:, :h], x[:, h:]], packed_dtype=BF16)


def _unpack_lo(p):
    return pltpu.bitcast(p << 16, F32)


def _unpack_hi(p):
    return pltpu.bitcast(p & jnp.uint32(0xFFFF0000), F32)


def _inproj_kernel(x_ref, g_ref, w_ref, c_ref, s1_ref, s2_ref, q_ref, k_ref, v_ref, u_ref):
    xn = _rms(x_ref[...], g_ref[...]).astype(BF16)
    z = jnp.dot(xn, w_ref[...], preferred_element_type=F32)
    c, s1, s2 = c_ref[...], s1_ref[...], s2_ref[...]

    def rope(t):
        return t * c + pltpu.roll(t, LANES - ROT_DIM // 2, 1) * s1 + pltpu.roll(t, ROT_DIM // 2, 1) * s2

    for i in range(ATT_WIDTH // LANES):
        sl = slice(i * LANES, (i + 1) * LANES)
        q_ref[:, sl] = (rope(z[:, sl]) * (HEAD_DIM ** -0.5)).astype(q_ref.dtype)
    k_ref[...] = rope(z[:, ATT_WIDTH:ATT_WIDTH + KV_WIDTH])
    v_ref[...] = z[:, ATT_WIDTH + KV_WIDTH:ATT_WIDTH + 2 * KV_WIDTH]
    u_ref[...] = z[:, ATT_WIDTH + 2 * KV_WIDTH:]


def _rope_tables(pos, reps=1):
    f32 = np.float32
    half = ROT_DIM // 2
    inv = np.power(f32(ROPE_THETA), -np.arange(half, dtype=f32) * f32(2.0) / f32(ROT_DIM)).astype(f32)
    ang = np.asarray(pos, f32)[:, None] * inv[None, :]
    cos, sin = np.cos(ang).astype(f32), np.sin(ang).astype(f32)
    n = len(pos)
    ones = np.ones((n, HEAD_DIM - ROT_DIM), f32)
    zeros = np.zeros((n, HEAD_DIM - ROT_DIM), f32)
    zh = np.zeros((n, half), f32)
    c = np.concatenate([cos, cos, ones], axis=1)
    s1 = np.concatenate([-sin, zh, zeros], axis=1)
    s2 = np.concatenate([zh, sin, zeros], axis=1)
    tile = lambda a: np.tile(a, (reps, LANES // HEAD_DIM))
    return tile(c), tile(s1), tile(s2)


def _inproj(x2d, g_mix, w_in_bf, tables, q_dtype):
    rows = x2d.shape[0]
    n_tab = tables[0].shape[0] // BM_IN
    row_spec = lambda w: pl.BlockSpec((BM_IN, w), lambda i: (i, 0))
    tab_spec = pl.BlockSpec((BM_IN, LANES), lambda i: (i % n_tab, 0))
    full = lambda a: pl.BlockSpec(a.shape, lambda i: (0,) * a.ndim)
    return pl.pallas_call(
        _inproj_kernel,
        grid=(rows // BM_IN,),
        in_specs=[row_spec(D_MODEL), full(g_mix), full(w_in_bf), tab_spec, tab_spec, tab_spec],
        out_specs=[row_spec(ATT_WIDTH), row_spec(KV_WIDTH), row_spec(KV_WIDTH), row_spec(POOL_WIDTH)],
        out_shape=[jax.ShapeDtypeStruct((rows, ATT_WIDTH), q_dtype),
                   jax.ShapeDtypeStruct((rows, KV_WIDTH), F32),
                   jax.ShapeDtypeStruct((rows, KV_WIDTH), F32),
                   jax.ShapeDtypeStruct((rows, POOL_WIDTH), F32)],
        compiler_params=_cparams(("parallel",)),
        name="inproj",
    )(x2d, g_mix, w_in_bf, *tables)


def _sink_column(sinks_ref, kv_head, rows_per_head):
    n = GQA_GROUP * rows_per_head
    grp = lax.broadcasted_iota(I32, (n, 1), 0) // rows_per_head
    col = jnp.full((n, 1), sinks_ref[kv_head * GQA_GROUP], F32)
    for g in range(1, GQA_GROUP):
        col = jnp.where(grp == g, sinks_ref[kv_head * GQA_GROUP + g], col)
    return col


def _band_mask(n_rows, rows_per_head, n_keys):
    i = lax.broadcasted_iota(I32, (n_rows, n_keys), 0) % rows_per_head
    c = lax.broadcasted_iota(I32, (n_rows, n_keys), 1)
    return (c >= i) & (c <= i + WINDOW), c


def _stack_heads(q, kv_head):
    return jnp.concatenate(
        [q[:, (kv_head * GQA_GROUP + g) * HEAD_DIM:(kv_head * GQA_GROUP + g + 1) * HEAD_DIM]
         for g in range(GQA_GROUP)], axis=0)


def _nt_dot(a, b):
    return lax.dot_general(a, b, (((1,), (1,)), ((), ())), preferred_element_type=F32)


def _pool_delta(u, uext_ref, base, n, cnt_fn):
    parts = []
    for g, w in enumerate(POOL_WINDOWS):
        sl = slice(g * POOL_GROUP_DIM, (g + 1) * POOL_GROUP_DIM)
        acc = u[:, sl]
        for m in range(1, w):
            acc = acc + uext_ref[base - m:base - m + n, sl]
        parts.append(acc / cnt_fn(w) - u[:, sl])
    return parts


def _mixer_tail(o_att, d, h, wpool_ref, pscale_ref, gatt_ref, gpool_ref, wout_ref):
    parts = [jnp.dot(d[:, g * POOL_GROUP_DIM:(g + 1) * POOL_GROUP_DIM].astype(BF16), wpool_ref[g],
                     preferred_element_type=F32) for g in range(POOL_GROUPS)]
    o_pool = jnp.concatenate(parts, axis=-1) * pscale_ref[...]
    mixed = jnp.concatenate([_rms(o_att, gatt_ref[...]), _rms(o_pool, gpool_ref[...])], axis=-1)
    return h + jnp.dot(mixed.astype(BF16), wout_ref[...], preferred_element_type=F32)


def _mixer_prompt_kernel(sinks_ref, h_ref, q_ref, kc_ref, kp_ref, vc_ref, vp_ref, uc_ref, up_ref,
                         wpool_ref, pscale_ref, gatt_ref, gpool_ref, wout_ref, h1_ref, uext_ref):
    j = pl.program_id(1)
    q = q_ref[...]
    k_all = jnp.concatenate([kp_ref[...], kc_ref[...]], axis=0).astype(BF16)
    v_all = jnp.concatenate([vp_ref[...], vc_ref[...]], axis=0).astype(BF16)
    band, col = _band_mask(GQA_GROUP * WINDOW, WINDOW, 2 * WINDOW)
    sinks = [_sink_column(sinks_ref, hk, WINDOW) for hk in range(N_KV_HEADS)]
    bands = []
    for b in range(BQ // WINDOW):
        rows = slice(b * WINDOW, (b + 1) * WINDOW)
        keys = slice(b * WINDOW, (b + 2) * WINDOW)
        mask = band & ((col >= WINDOW) | (j > 0)) if b == 0 else band
        heads = []
        for hk in range(N_KV_HEADS):
            sl = slice(hk * HEAD_DIM, (hk + 1) * HEAD_DIM)
            s = jnp.where(mask, _nt_dot(_stack_heads(q[rows], hk), k_all[keys, sl]), -jnp.inf)
            m = jnp.maximum(jnp.max(s, axis=-1, keepdims=True), sinks[hk])
            e = jnp.exp(s - m)
            den = jnp.sum(e, axis=-1, keepdims=True) + jnp.exp(sinks[hk] - m)
            o = jnp.dot(e.astype(BF16), v_all[keys, sl], preferred_element_type=F32) / den
            heads += [o[g * WINDOW:(g + 1) * WINDOW] for g in range(GQA_GROUP)]
        bands.append(jnp.concatenate(heads, axis=-1))
    o_att = jnp.concatenate(bands, axis=0)

    u = uc_ref[...]
    uext_ref[0:16, :] = jnp.where(j > 0, up_ref[...], 0.0)
    uext_ref[16:16 + BQ, :] = u
    pos = j * BQ + lax.broadcasted_iota(I32, (BQ, 1), 0)
    d = jnp.concatenate(
        _pool_delta(u, uext_ref, 16, BQ, lambda w: jnp.minimum(pos + 1, w).astype(F32)), axis=-1)
    h1_ref[...] = _mixer_tail(o_att, d, h_ref[...], wpool_ref, pscale_ref, gatt_ref, gpool_ref, wout_ref)


def _mixer_sample_kernel(sinks_ref, h_ref, q_ref, kn_ref, vn_ref, u_ref, ck_ref, cv_ref, st_ref,
                         wpool_ref, pscale_ref, gatt_ref, gpool_ref, wout_ref, h1_in_ref,
                         h1_ref, ko_ref, vo_ref, po_ref, uext_ref):
    del h1_in_ref
    n_q = GQA_GROUP * DEC_SEQ
    n_keys = 2 * WINDOW
    band, col = _band_mask(n_q, DEC_SEQ, n_keys)
    mask = (band & (col < WINDOW + DEC_SEQ))[None]
    q3, kn3, vn3, u3 = q_ref[...], kn_ref[...], vn_ref[...], u_ref[...]
    ck, cv = ck_ref[...], cv_ref[...]
    ko_ref[:, 0:WINDOW - DEC_SEQ, :] = ck[:, DEC_SEQ:, :]
    ko_ref[:, WINDOW - DEC_SEQ:WINDOW, :] = kn3
    vo_ref[:, 0:WINDOW - DEC_SEQ, :] = cv[:, DEC_SEQ:, :]
    vo_ref[:, WINDOW - DEC_SEQ:WINDOW, :] = vn3
    pad = jnp.zeros((SB, WINDOW - DEC_SEQ, KV_WIDTH), F32)
    k_all = jnp.concatenate([ck, kn3, pad], axis=1).astype(BF16)
    v_all = jnp.concatenate([cv, vn3, pad], axis=1).astype(BF16)
    heads = []
    for hk in range(N_KV_HEADS):
        sl = slice(hk * HEAD_DIM, (hk + 1) * HEAD_DIM)
        qs = jnp.concatenate(
            [q3[:, :, (hk * GQA_GROUP + g) * HEAD_DIM:(hk * GQA_GROUP + g + 1) * HEAD_DIM]
             for g in range(GQA_GROUP)], axis=1).astype(BF16)
        sink = _sink_column(sinks_ref, hk, DEC_SEQ)[None]
        s = jnp.einsum("bqd,bkd->bqk", qs, k_all[:, :, sl], preferred_element_type=F32)
        s = jnp.where(mask, s, -jnp.inf)
        m = jnp.maximum(jnp.max(s, axis=-1, keepdims=True), sink)
        e = jnp.exp(s - m)
        den = jnp.sum(e, axis=-1, keepdims=True) + jnp.exp(sink - m)
        o = jnp.einsum("bqk,bkd->bqd", e.astype(BF16), v_all[:, :, sl], preferred_element_type=F32) / den
        heads += [o[:, g * DEC_SEQ:(g + 1) * DEC_SEQ, :] for g in range(GQA_GROUP)]
    o_att = jnp.concatenate(heads, axis=-1).reshape(SB * DEC_SEQ, ATT_WIDTH)

    uext_ref[:, 1:16, :] = st_ref[...]
    uext_ref[:, 16:16 + DEC_SEQ, :] = u3
    parts = []
    for g, w in enumerate(POOL_WINDOWS):
        sl = slice(g * POOL_GROUP_DIM, (g + 1) * POOL_GROUP_DIM)
        acc = u3[:, :, sl]
        for back in range(1, w):
            acc = acc + uext_ref[:, 16 - back:16 - back + DEC_SEQ, sl]
        parts.append(acc / float(w) - u3[:, :, sl])
    d = jnp.concatenate(parts, axis=-1).reshape(SB * DEC_SEQ, POOL_WIDTH)
    po_ref[...] = uext_ref[:, 16 + DEC_SEQ - POOL_STATE:16 + DEC_SEQ, :]
    h1_ref[...] = _mixer_tail(o_att, d, h_ref[...], wpool_ref, pscale_ref, gatt_ref, gpool_ref, wout_ref)


def _full_spec(a, n_grid):
    nd = a.ndim
    return pl.BlockSpec(a.shape, lambda *_: (0,) * nd)


def _mixer_prompt(sinks, x2d, q, k, v, u, wts):
    nb = SEQ // BQ
    row = lambda w: pl.BlockSpec((BQ, w), lambda b, j: (b * nb + j, 0))
    prev = lambda w: pl.BlockSpec(
        (WINDOW, w), lambda b, j: (jnp.maximum((b * nb + j) * (BQ // WINDOW) - 1, 0), 0))
    uprev = pl.BlockSpec((16, POOL_WIDTH), lambda b, j: (jnp.maximum((b * nb + j) * (BQ // 16) - 1, 0), 0))
    smem = pl.BlockSpec(memory_space=pltpu.SMEM)
    return pl.pallas_call(
        _mixer_prompt_kernel,
        grid=(BATCH, nb),
        in_specs=[smem, row(D_MODEL), row(ATT_WIDTH), row(KV_WIDTH), prev(KV_WIDTH), row(KV_WIDTH),
                  prev(KV_WIDTH), row(POOL_WIDTH), uprev] + [_full_spec(w, 2) for w in wts],
        out_specs=row(D_MODEL),
        out_shape=jax.ShapeDtypeStruct((T_ALL, D_MODEL), F32),
        scratch_shapes=[pltpu.VMEM((16 + BQ, POOL_WIDTH), F32)],
        compiler_params=_cparams(("parallel", "parallel")),
        name="mixer_prompt",
    )(sinks, x2d, q, k, k, v, v, u, u, *wts)


def _mixer_sample(sinks, x2d, q, k, v, u, cache_k, cache_v, state, wts, h1_buf):
    rows = SB * DEC_SEQ
    row = lambda w: pl.BlockSpec((rows, w), lambda i: (i, 0))
    bat = lambda a: pl.BlockSpec((SB,) + a.shape[1:], lambda i: (i, 0, 0))
    smem = pl.BlockSpec(memory_space=pltpu.SMEM)
    h1_blocks_before = T_P // rows
    n_in = 9 + len(wts)
    q, k, v, u = (a.reshape(DEC_BATCH, DEC_SEQ, a.shape[-1]) for a in (q, k, v, u))
    return pl.pallas_call(
        _mixer_sample_kernel,
        grid=(DEC_BATCH // SB,),
        in_specs=[smem, row(D_MODEL), bat(q), bat(k), bat(v), bat(u),
                  bat(cache_k), bat(cache_v), bat(state)] + [_full_spec(w, 1) for w in wts]
                 + [pl.BlockSpec(memory_space=pl.ANY)],
        out_specs=[pl.BlockSpec((rows, D_MODEL), lambda i: (h1_blocks_before + i, 0)),
                   bat(cache_k), bat(cache_v), bat(state)],
        out_shape=[jax.ShapeDtypeStruct((T_ALL, D_MODEL), F32),
                   jax.ShapeDtypeStruct(cache_k.shape, F32),
                   jax.ShapeDtypeStruct(cache_v.shape, F32),
                   jax.ShapeDtypeStruct(state.shape, F32)],
        scratch_shapes=[pltpu.VMEM((SB, 16 + DEC_SEQ, POOL_WIDTH), F32)],
        input_output_aliases={n_in: 0},
        compiler_params=_cparams(("parallel",)),
        name="mixer_sample",
    )(sinks, x2d, q, k, v, u, cache_k, cache_v, state, *wts, h1_buf)


def _first_max(vals, iota, n):
    m = jnp.max(vals, axis=0, keepdims=True)
    idx = jnp.min(jnp.where(vals == m, iota, n), axis=0, keepdims=True)
    return m, idx


def _router_kernel(h1_ref, gffn_ref, wrt_ref, bias_ref, xp_ref, idx_ref, wts_ref):
    xn = _rms(h1_ref[...], gffn_ref[...])
    logits = lax.dot_general(wrt_ref[...], xn, (((1,), (1,)), ((), ())),
                             precision=lax.Precision.HIGHEST, preferred_element_type=F32)
    scores = _sigmoid(logits)
    biased = scores + bias_ref[...]
    n_tok = biased.shape[1]
    neg = -jnp.inf

    iota_g = lax.broadcasted_iota(I32, (GROUP_SIZE, n_tok), 0)
    grp_rows = []
    for g in range(N_EXPERT_GROUPS):
        blk = biased[g * GROUP_SIZE:(g + 1) * GROUP_SIZE, :]
        top1, i1 = _first_max(blk, iota_g, GROUP_SIZE)
        top2 = jnp.max(jnp.where(iota_g == i1, neg, blk), axis=0, keepdims=True)
        grp_rows.append(top1 + top2)
    gs = jnp.concatenate(grp_rows, axis=0)

    iota_n = lax.broadcasted_iota(I32, (N_EXPERT_GROUPS, n_tok), 0)
    gsel = jnp.zeros((N_EXPERT_GROUPS, n_tok), jnp.bool_)
    for _ in range(TOPK_GROUPS):
        _, gi = _first_max(gs, iota_n, N_EXPERT_GROUPS)
        hit = iota_n == gi
        gsel = gsel | hit
        gs = jnp.where(hit, neg, gs)
    emask = jnp.concatenate(
        [jnp.broadcast_to(gsel[g:g + 1, :], (GROUP_SIZE, n_tok)) for g in range(N_EXPERT_GROUPS)], axis=0)
    masked = jnp.where(emask, biased, neg)

    iota_e = lax.broadcasted_iota(I32, (N_EXPERTS, n_tok), 0)
    idx_rows, sel_rows = [], []
    for _ in range(TOP_K):
        _, ei = _first_max(masked, iota_e, N_EXPERTS)
        hit = iota_e == ei
        idx_rows.append(ei)
        sel_rows.append(jnp.sum(jnp.where(hit, scores, 0.0), axis=0, keepdims=True))
        masked = jnp.where(hit, neg, masked)
    sel = jnp.concatenate(sel_rows, axis=0)
    idx_ref[...] = jnp.concatenate(idx_rows, axis=0)
    wts_ref[...] = sel / jnp.sum(sel, axis=0, keepdims=True) * ROUTED_SCALE
    _store_chunks(xp_ref, _pack_bf16_pairs(xn))


def _router(h1, g_ffn, w_router_t, bias_col):
    colblk = pl.BlockSpec((TOP_K, BM_R), lambda i: (0, i))
    ws = [g_ffn, w_router_t, bias_col]
    return pl.pallas_call(
        _router_kernel,
        grid=(T_ALL // BM_R,),
        in_specs=[pl.BlockSpec((BM_R, D_MODEL), lambda i: (i, 0))] + [_full_spec(w, 1) for w in ws],
        out_specs=[pl.BlockSpec((BM_R * ROW_CHUNKS, LANES), lambda i: (i, 0)), colblk, colblk],
        out_shape=[jax.ShapeDtypeStruct((T_ALL * ROW_CHUNKS, LANES), U32),
                   jax.ShapeDtypeStruct((TOP_K, T_ALL), I32),
                   jax.ShapeDtypeStruct((TOP_K, T_ALL), F32)],
        compiler_params=_cparams(("parallel",)),
        name="router",
    )(h1, *ws)


def _rank_kernel(idx_ref, tri_ref, rank_ref, cnt_ref, carry_ref):
    @pl.when(pl.program_id(0) == 0)
    def _():
        carry_ref[...] = jnp.zeros_like(carry_ref)

    idx = idx_ref[...]
    n_tok = idx.shape[1]
    iota_e = lax.broadcasted_iota(I32, (N_EXPERTS, n_tok), 0)
    member = jnp.zeros((N_EXPERTS, n_tok), F32)
    for k in range(TOP_K):
        member = member + jnp.where(iota_e == idx[k:k + 1, :], 1.0, 0.0)
    before = jnp.dot(member.astype(BF16), tri_ref[...], preferred_element_type=F32) + carry_ref[...]
    rows = [jnp.sum(jnp.where(iota_e == idx[k:k + 1, :], before, 0.0), axis=0, keepdims=True)
            for k in range(TOP_K)]
    rank_ref[...] = jnp.concatenate(rows, axis=0).astype(I32)
    carry_ref[...] = carry_ref[...] + jnp.sum(member, axis=1, keepdims=True)
    cnt_ref[...] = carry_ref[...].astype(I32)


def _rank(idx_t, tri):
    blk = pl.BlockSpec((TOP_K, BT_RANK), lambda i: (0, i))
    return pl.pallas_call(
        _rank_kernel,
        grid=(T_ALL // BT_RANK,),
        in_specs=[blk, _full_spec(tri, 1)],
        out_specs=[blk, pl.BlockSpec((N_EXPERTS, 1), lambda i: (0, 0))],
        out_shape=[jax.ShapeDtypeStruct((TOP_K, T_ALL), I32),
                   jax.ShapeDtypeStruct((N_EXPERTS, 1), I32)],
        scratch_shapes=[pltpu.VMEM((N_EXPERTS, 1), F32)],
        compiler_params=_cparams(("arbitrary",)),
        name="rank",
    )(idx_t, tri)


def _dest_kernel(idx_ref, rank_ref, cnt_ref, dest_ref, blk_e_ref, n_used_ref):
    counts = cnt_ref[...]
    padded = (counts + (BM_E - 1)) // BM_E * BM_E
    r = lax.broadcasted_iota(I32, (N_EXPERTS, N_EXPERTS), 0)
    c = lax.broadcasted_iota(I32, (N_EXPERTS, N_EXPERTS), 1)
    padded_row = jnp.sum(jnp.where(r == c, padded, 0), axis=0, keepdims=True)
    pad_start = jnp.sum(jnp.where(c < r, padded_row, 0), axis=1, keepdims=True)

    idx = idx_ref[...]
    n_tok = idx.shape[1]
    iota_e = lax.broadcasted_iota(I32, (N_EXPERTS, n_tok), 0)
    rows = [jnp.sum(jnp.where(iota_e == idx[k:k + 1, :], pad_start, 0), axis=0, keepdims=True)
            for k in range(TOP_K)]
    dest_ref[...] = jnp.concatenate(rows, axis=0) + rank_ref[...]

    pad_end_row = jnp.sum(jnp.where(r <= c, padded, 0), axis=0, keepdims=True)
    b0 = lax.broadcasted_iota(I32, (N_BLOCKS_PAD, N_EXPERTS), 0) * BM_E
    be = jnp.sum(jnp.where(pad_end_row <= b0, 1, 0), axis=1, keepdims=True)
    blk_e_ref[...] = jnp.minimum(be, N_EXPERTS - 1)
    n_used_ref[...] = pad_end_row[:, N_EXPERTS - 1:N_EXPERTS] // BM_E


N_BLOCKS_PAD = (N_BLOCKS + 7) // 8 * 8


def _dest(idx_t, rank_t, counts):
    blk = pl.BlockSpec((TOP_K, BT_RANK), lambda i: (0, i))
    one = lambda s: pl.BlockSpec(s, lambda i: (0, 0))
    return pl.pallas_call(
        _dest_kernel,
        grid=(T_ALL // BT_RANK,),
        in_specs=[blk, blk, one((N_EXPERTS, 1))],
        out_specs=[blk, one((N_BLOCKS_PAD, 1)), one((1, 1))],
        out_shape=[jax.ShapeDtypeStruct((TOP_K, T_ALL), I32),
                   jax.ShapeDtypeStruct((N_BLOCKS_PAD, 1), I32),
                   jax.ShapeDtypeStruct((1, 1), I32)],
        compiler_params=_cparams(("arbitrary",)),
        name="dest",
    )(idx_t, rank_t, counts)


def _sc_mesh():
    return plsc.VectorSubcoreMesh(core_axis_name="c", subcore_axis_name="s")


def _sc_worker_id():
    return lax.axis_index("s") * SC_CORES + lax.axis_index("c")


def _dispatch_body(dest_hbm, xp_hbm, xs_hbm, idx_v, rows_v, sem_in, sem_out):
    n_chunks, _, n_tok = dest_hbm.shape
    per_worker = n_chunks // SC_WORKERS
    chunk0 = _sc_worker_id() * per_worker

    def loads(i):
        chunk = chunk0 + i
        t0 = pl.multiple_of(chunk * n_tok, n_tok)
        return (pltpu.make_async_copy(dest_hbm.at[chunk], idx_v.at[i % 2], sem_in.at[i % 2]),
                pltpu.make_async_copy(xp_hbm.at[pl.ds(t0, n_tok)], rows_v.at[i % 2], sem_in.at[i % 2]))

    def scatters(i):
        return [pltpu.make_async_copy(rows_v.at[i % 2], xs_hbm.at[idx_v.at[i % 2, k]], sem_out.at[i % 2])
                for k in range(TOP_K)]

    for cp in loads(0):
        cp.start()
    for i in range(per_worker):
        for cp in loads(i):
            cp.wait()
        if i >= 1:
            for cp in scatters(i - 1):
                cp.wait()
        if i + 1 < per_worker:
            for cp in loads(i + 1):
                cp.start()
        for cp in scatters(i):
            cp.start()
    for cp in scatters(per_worker - 1):
        cp.wait()


def _dispatch(dest_chunks, xp3):
    return pl.kernel(
        _dispatch_body,
        out_type=jax.ShapeDtypeStruct((N_SLOTS, ROW_CHUNKS, LANES), U32),
        mesh=_sc_mesh(),
        scratch_types=[pltpu.VMEM((2, TOP_K, SC_CHUNK_D), I32),
                       pltpu.VMEM((2, SC_CHUNK_D, ROW_CHUNKS, LANES), U32),
                       pltpu.SemaphoreType.DMA((2,)), pltpu.SemaphoreType.DMA((2,))],
        name="dispatch",
    )(dest_chunks, xp3)


def _gather_body(dest_hbm, ys_hbm, yt_hbm, idx_v, rows_v, sem_in, sem_out):
    n_chunks, _, n_tok = dest_hbm.shape
    per_worker = n_chunks // SC_WORKERS
    chunk0 = _sc_worker_id() * per_worker

    @pl.loop(0, per_worker)
    def _(i):
        chunk = chunk0 + i
        t0 = pl.multiple_of(chunk * n_tok, n_tok)
        pltpu.sync_copy(dest_hbm.at[chunk], idx_v)

        def gather(k):
            return pltpu.make_async_copy(ys_hbm.at[idx_v.at[k]], rows_v.at[k % SC_RING], sem_in.at[k % SC_RING])

        def store(k):
            return pltpu.make_async_copy(rows_v.at[k % SC_RING], yt_hbm.at[k, pl.ds(t0, n_tok)],
                                         sem_out.at[k % SC_RING])

        for k in range(SC_RING):
            gather(k).start()
        for k in range(TOP_K):
            gather(k).wait()
            store(k).start()
            if k + SC_RING < TOP_K:
                store(k).wait()
                gather(k + SC_RING).start()
        for k in range(TOP_K - SC_RING, TOP_K):
            store(k).wait()


def _gather(dest_chunks, ys3):
    return pl.kernel(
        _gather_body,
        out_type=jax.ShapeDtypeStruct((TOP_K, T_GRP, ROW_CHUNKS, LANES), U32),
        mesh=_sc_mesh(),
        scratch_types=[pltpu.VMEM((TOP_K, SC_CHUNK_G), I32),
                       pltpu.VMEM((SC_RING, SC_CHUNK_G, ROW_CHUNKS, LANES), U32),
                       pltpu.SemaphoreType.DMA((SC_RING,)), pltpu.SemaphoreType.DMA((SC_RING,))],
        name="gather",
    )(dest_chunks, ys3)


def _experts_kernel(blk_e_ref, n_used_ref, xs_ref, wg_ref, wu_ref, wd_ref, ys_ref, wgu_s, wd_s):
    b = pl.program_id(0)

    @pl.when(b < n_used_ref[0])
    def _():
        prev = blk_e_ref[jnp.maximum(b - 1, 0)]

        @pl.when((b == 0) | (blk_e_ref[b] != prev))
        def _():
            wgu_s[:, :D_EXPERT] = wg_ref[0].astype(BF16)
            wgu_s[:, D_EXPERT:] = wu_ref[0].astype(BF16)
            wd_s[...] = wd_ref[0].astype(BF16)

        chunks = _load_chunks(xs_ref, BM_E)
        x_lo = jnp.concatenate([_unpack_lo(p) for p in chunks], axis=-1).astype(BF16)
        x_hi = jnp.concatenate([_unpack_hi(p) for p in chunks], axis=-1).astype(BF16)
        gu = (jnp.dot(x_lo, wgu_s[:HALF, :], preferred_element_type=F32)
              + jnp.dot(x_hi, wgu_s[HALF:, :], preferred_element_type=F32))
        gate, up = gu[:, :D_EXPERT], gu[:, D_EXPERT:]
        hmid = (gate * _sigmoid(gate) * up).astype(BF16)
        _store_chunks(ys_ref, _pack_bf16_pairs(jnp.dot(hmid, wd_s[...], preferred_element_type=F32)))


def _experts(blk_e, n_used, xs, wg, wu, wd):
    def blk(b, be, nu):
        return jnp.minimum(b, nu[0] - 1)

    grid_spec = pltpu.PrefetchScalarGridSpec(
        num_scalar_prefetch=2,
        grid=(N_BLOCKS,),
        in_specs=[pl.BlockSpec((BM_E * ROW_CHUNKS, LANES), lambda b, be, nu: (blk(b, be, nu), 0)),
                  pl.BlockSpec((1, D_MODEL, D_EXPERT), lambda b, be, nu: (be[blk(b, be, nu)], 0, 0)),
                  pl.BlockSpec((1, D_MODEL, D_EXPERT), lambda b, be, nu: (be[blk(b, be, nu)], 0, 0)),
                  pl.BlockSpec((1, D_EXPERT, D_MODEL), lambda b, be, nu: (be[blk(b, be, nu)], 0, 0))],
        out_specs=pl.BlockSpec((BM_E * ROW_CHUNKS, LANES), lambda b, be, nu: (blk(b, be, nu), 0)),
        scratch_shapes=[pltpu.VMEM((D_MODEL, 2 * D_EXPERT), BF16), pltpu.VMEM((D_EXPERT, D_MODEL), BF16)],
    )
    return pl.pallas_call(
        _experts_kernel,
        grid_spec=grid_spec,
        out_shape=jax.ShapeDtypeStruct((N_SLOTS * ROW_CHUNKS, LANES), U32),
        compiler_params=_cparams(("arbitrary",)),
        name="experts",
    )(blk_e, n_used, xs, wg, wu, wd)


def _combine_kernel(yt_ref, wts_ref, h1_ref, p_ref, gffn_ref, wsgu_ref, wsd_ref, gple_ref, wpg_ref, wpp_ref,
                    gfin_ref, *y_refs):
    y_ref = y_refs[-1]
    h1 = h1_ref[...]
    gu = jnp.dot(_rms(h1, gffn_ref[...]).astype(BF16), wsgu_ref[...], preferred_element_type=F32)
    sgate, sup = gu[:, :D_SHARED], gu[:, D_SHARED:]
    hsh = h1 + jnp.dot((sgate * _sigmoid(sgate) * sup).astype(BF16), wsd_ref[...], preferred_element_type=F32)
    wts = jnp.transpose(wts_ref[...])
    lo = [jnp.zeros((BT_COMB, LANES), F32) for _ in range(ROW_CHUNKS)]
    hi = [jnp.zeros((BT_COMB, LANES), F32) for _ in range(ROW_CHUNKS)]
    for k in range(TOP_K):
        w = wts[:, k:k + 1]
        for c, p in enumerate(_load_chunks(yt_ref, BT_COMB, lead=(k,))):
            lo[c] = lo[c] + w * _unpack_lo(p)
            hi[c] = hi[c] + w * _unpack_hi(p)
    h2 = hsh + jnp.concatenate(lo + hi, axis=-1)
    gate = _sigmoid(jnp.dot(_rms(h2, gple_ref[...]).astype(BF16), wpg_ref[...], preferred_element_type=F32))
    proj = jnp.dot(p_ref[...].astype(BF16), wpp_ref[...], preferred_element_type=F32)
    y_ref[...] = _rms(h2 + proj * gate, gfin_ref[...])


def _combine(yt, yt_row0, wts_t, h1, tok_row0, n_rows, p2d, p_row0, ws, y_prev, out_rows, out_row0):
    assert all(r % BT_COMB == 0 for r in (yt_row0, tok_row0, n_rows, p_row0, out_row0))
    g0, t0, p0, o0 = yt_row0 // BT_COMB, tok_row0 // BT_COMB, p_row0 // BT_COMB, out_row0 // BT_COMB
    in_specs = [pl.BlockSpec((TOP_K, BT_COMB * ROW_CHUNKS, LANES), lambda i: (0, g0 + i, 0)),
                pl.BlockSpec((TOP_K, BT_COMB), lambda i: (0, t0 + i)),
                pl.BlockSpec((BT_COMB, D_MODEL), lambda i: (t0 + i, 0)),
                pl.BlockSpec((BT_COMB, PLE_DIM), lambda i: (p0 + i, 0))] + [_full_spec(w, 1) for w in ws]
    args = [yt, wts_t, h1, p2d, *ws]
    aliases = {}
    if y_prev is not None:
        in_specs.append(pl.BlockSpec(memory_space=pl.ANY))
        aliases = {len(args): 0}
        args.append(y_prev)
    return pl.pallas_call(
        _combine_kernel,
        grid=(n_rows // BT_COMB,),
        in_specs=in_specs,
        out_specs=pl.BlockSpec((BT_COMB, D_MODEL), lambda i: (o0 + i, 0)),
        out_shape=jax.ShapeDtypeStruct((out_rows, D_MODEL), F32),
        input_output_aliases=aliases,
        compiler_params=_cparams(("parallel",)),
        name="combine",
    )(*args)


def kernel(x_prompt, x_sample, cache_k, cache_v, state_pool, p_prompt, p_sample, g_mix, w_in, attn_sinks,
           w_pool, pool_scale, g_att_out, g_pool_out, w_out, g_ffn, w_router, router_bias, w_exp_gate,
           w_exp_up, w_exp_down, w_sh_gate, w_sh_up, w_sh_down, g_ple, w_ple_gate, w_ple_proj, g_final):
    row = lambda a: a.reshape(1, -1)
    xp2d = x_prompt.reshape(T_P, D_MODEL)
    xs2d = x_sample.reshape(T_S, D_MODEL)
    w_in_bf = w_in[0].astype(BF16)
    mixer_wts = [w_pool[0].astype(BF16), row(pool_scale[0]), row(g_att_out[0]), row(g_pool_out[0]),
                 w_out[0].astype(BF16)]

    tab_p = _rope_tables(np.arange(SEQ))
    tab_s = _rope_tables(PAST_LEN + np.arange(DEC_SEQ), reps=BM_IN // DEC_SEQ)

    q_p, k_p, v_p, u_p = _inproj(xp2d, row(g_mix[0]), w_in_bf, tab_p, BF16)
    q_s, k_s, v_s, u_s = _inproj(xs2d, row(g_mix[0]), w_in_bf, tab_s, F32)

    h1 = _mixer_prompt(attn_sinks[0], xp2d, q_p, k_p, v_p, u_p, mixer_wts)
    h1, k_sample, v_sample, pool_sample = _mixer_sample(
        attn_sinks[0], xs2d, q_s, k_s, v_s, u_s,
        cache_k[0].reshape(DEC_BATCH, WINDOW, KV_WIDTH), cache_v[0].reshape(DEC_BATCH, WINDOW, KV_WIDTH),
        state_pool[0], mixer_wts, h1)

    g_ffn_row = row(g_ffn[0])
    xp, idx_t, wts_t = _router(h1, g_ffn_row, w_router[0].T, router_bias[0].reshape(N_EXPERTS, 1))
    tri = (lax.broadcasted_iota(I32, (BT_RANK, BT_RANK), 0)
           < lax.broadcasted_iota(I32, (BT_RANK, BT_RANK), 1)).astype(BF16)
    rank_t, counts = _rank(idx_t, tri)
    dest_t, blk_e, n_used = _dest(idx_t, rank_t, counts)

    def index_chunks(d, n_tok):
        return d.reshape(TOP_K, d.shape[1] // n_tok, n_tok).transpose(1, 0, 2)

    xs = _dispatch(index_chunks(dest_t, SC_CHUNK_D), xp.reshape(T_ALL, ROW_CHUNKS, LANES))
    ys = _experts(blk_e.reshape(N_BLOCKS_PAD), n_used.reshape(1), xs.reshape(N_SLOTS * ROW_CHUNKS, LANES),
                  w_exp_gate[0], w_exp_up[0], w_exp_down[0])
    ys3 = ys.reshape(N_SLOTS, ROW_CHUNKS, LANES)

    ple_wts = [g_ffn_row, jnp.concatenate([w_sh_gate[0], w_sh_up[0]], axis=1).astype(BF16),
               w_sh_down[0].astype(BF16),
               row(g_ple[0]), w_ple_gate[0].astype(BF16), w_ple_proj[0].astype(BF16), row(g_final)]
    pp2d = p_prompt[0].reshape(T_P, PLE_DIM)
    ps2d = p_sample[0].reshape(T_S, PLE_DIM)
    y_p = y_s = None
    for g in range(N_GROUPS):
        lo, hi = g * T_GRP, (g + 1) * T_GRP
        yt = _gather(index_chunks(dest_t[:, lo:hi], SC_CHUNK_G), ys3)
        yt = yt.reshape(TOP_K, T_GRP * ROW_CHUNKS, LANES)
        if lo < T_P:
            n = min(hi, T_P) - lo
            y_p = _combine(yt, 0, wts_t, h1, lo, n, pp2d, lo, ple_wts, y_p, T_P, lo)
        if hi > T_P:
            s0 = max(lo, T_P)
            y_s = _combine(yt, s0 - lo, wts_t, h1, s0, hi - s0, ps2d, s0 - T_P, ple_wts, y_s, T_S, s0 - T_P)

    kv5 = lambda a, b: a.reshape(1, b, WINDOW, N_KV_HEADS, HEAD_DIM)
    k_prompt = kv5(k_p.reshape(BATCH, SEQ, KV_WIDTH)[:, SEQ - WINDOW:], BATCH)
    v_prompt = kv5(v_p.reshape(BATCH, SEQ, KV_WIDTH)[:, SEQ - WINDOW:], BATCH)
    pool_prompt = u_p.reshape(BATCH, SEQ, POOL_WIDTH)[:, SEQ - POOL_STATE:][None]
    return (y_p.reshape(BATCH, SEQ, D_MODEL), y_s.reshape(DEC_BATCH, DEC_SEQ, D_MODEL),
            k_prompt, v_prompt, pool_prompt,
            kv5(k_sample, DEC_BATCH), kv5(v_sample, DEC_BATCH), pool_sample[None])
```

```python
import functools

import numpy as np
import jax
import jax.numpy as jnp
from jax import lax
from jax.experimental import pallas as pl
from jax.experimental.pallas import tpu as pltpu
from jax.experimental.pallas import tpu_sc as plsc

F32 = jnp.float32
BF16 = jnp.bfloat16
U32 = jnp.uint32
I32 = jnp.int32

D_MODEL = 1024
BATCH = 8
SEQ = 2048
DEC_BATCH = 128
DEC_SEQ = 8
PAST_LEN = 16384
N_Q_HEADS = 8
N_KV_HEADS = 2
HEAD_DIM = 64
GQA_GROUP = N_Q_HEADS // N_KV_HEADS
ATT_WIDTH = N_Q_HEADS * HEAD_DIM
KV_WIDTH = N_KV_HEADS * HEAD_DIM
WINDOW = 128
ROPE_THETA = 500000.0
ROT_DIM = HEAD_DIM // 4
POOL_WINDOWS = (2, 4, 8, 16)
POOL_GROUPS = 4
POOL_WIDTH = D_MODEL - ATT_WIDTH
POOL_GROUP_DIM = POOL_WIDTH // POOL_GROUPS
POOL_STATE = 15
IN_WIDTH = ATT_WIDTH + 2 * KV_WIDTH + POOL_WIDTH
N_EXPERTS = 64
TOP_K = 8
N_EXPERT_GROUPS = 8
GROUP_SIZE = N_EXPERTS // N_EXPERT_GROUPS
TOPK_GROUPS = 4
D_EXPERT = 256
D_SHARED = 256
ROUTED_SCALE = 2.5
PLE_DIM = 256
EPS = 1e-6

T_P = BATCH * SEQ
T_S = DEC_BATCH * DEC_SEQ
T_ALL = T_P + T_S
HALF = D_MODEL // 2
LANES = 128
VMEM_LIMIT = 48 * 1024 * 1024

BM_IN = 256
BQ = 2 * WINDOW
SB = 16
BM_R = 256
BT_RANK = 512
BT_COMB = 512
N_GROUPS = 2
T_GRP = T_ALL // N_GROUPS
assert T_GRP * N_GROUPS == T_ALL
BM_E = 1280
N_ASSIGN = T_GRP * TOP_K
N_BLOCKS = -(-N_ASSIGN // BM_E) + N_EXPERTS
N_SLOTS = N_BLOCKS * BM_E

ROW_CHUNKS = HALF // LANES
SC_CORES = 2
SC_SUBCORES = 16
SC_WORKERS = SC_CORES * SC_SUBCORES
SC_CHUNK = 16
SC_RING = 4
assert T_GRP % (SC_WORKERS * SC_CHUNK) == 0


def _load_chunks(ref, n_rows, lead=()):
    return [ref[lead + (pl.ds(c, n_rows, stride=ROW_CHUNKS), slice(None))] for c in range(ROW_CHUNKS)]


def _store_chunks(ref, packed):
    n_rows = packed.shape[0]
    for c in range(ROW_CHUNKS):
        ref[pl.ds(c, n_rows, stride=ROW_CHUNKS), :] = packed[:, c * LANES:(c + 1) * LANES]


def _cparams(sem):
    return pltpu.CompilerParams(dimension_semantics=sem, vmem_limit_bytes=VMEM_LIMIT)


def _rms(x, g):
    return x * lax.rsqrt(jnp.mean(x * x, axis=-1, keepdims=True) + EPS) * g


def _sigmoid(x):
    return 1.0 / (1.0 + jnp.exp(-x))


def _pack_bf16_pairs(x):
    h = x.shape[-1] // 2
    return pltpu.pack_elementwise([x[:, :h], x[:, h:]], packed_dtype=BF16)


def _unpack_lo(p):
    return pltpu.bitcast(p << 16, F32)


def _unpack_hi(p):
    return pltpu.bitcast(p & jnp.uint32(0xFFFF0000), F32)


def _inproj_kernel(x_ref, g_ref, w_ref, c_ref, s1_ref, s2_ref, q_ref, k_ref, v_ref, u_ref):
    xn = _rms(x_ref[...], g_ref[...]).astype(BF16)
    z = jnp.dot(xn, w_ref[...], preferred_element_type=F32)
    c, s1, s2 = c_ref[...], s1_ref[...], s2_ref[...]

    def rope(t):
        return t * c + pltpu.roll(t, LANES - ROT_DIM // 2, 1) * s1 + pltpu.roll(t, ROT_DIM // 2, 1) * s2

    for i in range(ATT_WIDTH // LANES):
        sl = slice(i * LANES, (i + 1) * LANES)
        q_ref[:, sl] = (rope(z[:, sl]) * (HEAD_DIM ** -0.5)).astype(q_ref.dtype)
    k_ref[...] = rope(z[:, ATT_WIDTH:ATT_WIDTH + KV_WIDTH])
    v_ref[...] = z[:, ATT_WIDTH + KV_WIDTH:ATT_WIDTH + 2 * KV_WIDTH]
    u_ref[...] = z[:, ATT_WIDTH + 2 * KV_WIDTH:]


def _rope_tables(pos, reps=1):
    f32 = np.float32
    half = ROT_DIM // 2
    inv = np.power(f32(ROPE_THETA), -np.arange(half, dtype=f32) * f32(2.0) / f32(ROT_DIM)).astype(f32)
    ang = np.asarray(pos, f32)[:, None] * inv[None, :]
    cos, sin = np.cos(ang).astype(f32), np.sin(ang).astype(f32)
    n = len(pos)
    ones = np.ones((n, HEAD_DIM - ROT_DIM), f32)
    zeros = np.zeros((n, HEAD_DIM - ROT_DIM), f32)
    zh = np.zeros((n, half), f32)
    c = np.concatenate([cos, cos, ones], axis=1)
    s1 = np.concatenate([-sin, zh, zeros], axis=1)
    s2 = np.concatenate([zh, sin, zeros], axis=1)
    tile = lambda a: np.tile(a, (reps, LANES // HEAD_DIM))
    return tile(c), tile(s1), tile(s2)


def _inproj(x2d, g_mix, w_in_bf, tables, q_dtype):
    rows = x2d.shape[0]
    n_tab = tables[0].shape[0] // BM_IN
    row_spec = lambda w: pl.BlockSpec((BM_IN, w), lambda i: (i, 0))
    tab_spec = pl.BlockSpec((BM_IN, LANES), lambda i: (i % n_tab, 0))
    full = lambda a: pl.BlockSpec(a.shape, lambda i: (0,) * a.ndim)
    return pl.pallas_call(
        _inproj_kernel,
        grid=(rows // BM_IN,),
        in_specs=[row_spec(D_MODEL), full(g_mix), full(w_in_bf), tab_spec, tab_spec, tab_spec],
        out_specs=[row_spec(ATT_WIDTH), row_spec(KV_WIDTH), row_spec(KV_WIDTH), row_spec(POOL_WIDTH)],
        out_shape=[jax.ShapeDtypeStruct((rows, ATT_WIDTH), q_dtype),
                   jax.ShapeDtypeStruct((rows, KV_WIDTH), F32),
                   jax.ShapeDtypeStruct((rows, KV_WIDTH), F32),
                   jax.ShapeDtypeStruct((rows, POOL_WIDTH), F32)],
        compiler_params=_cparams(("parallel",)),
        name="inproj",
    )(x2d, g_mix, w_in_bf, *tables)


def _sink_column(sinks_ref, kv_head, rows_per_head):
    n = GQA_GROUP * rows_per_head
    grp = lax.broadcasted_iota(I32, (n, 1), 0) // rows_per_head
    col = jnp.full((n, 1), sinks_ref[kv_head * GQA_GROUP], F32)
    for g in range(1, GQA_GROUP):
        col = jnp.where(grp == g, sinks_ref[kv_head * GQA_GROUP + g], col)
    return col


def _band_mask(n_rows, rows_per_head, n_keys):
    i = lax.broadcasted_iota(I32, (n_rows, n_keys), 0) % rows_per_head
    c = lax.broadcasted_iota(I32, (n_rows, n_keys), 1)
    return (c >= i) & (c <= i + WINDOW), c


def _stack_heads(q, kv_head):
    return jnp.concatenate(
        [q[:, (kv_head * GQA_GROUP + g) * HEAD_DIM:(kv_head * GQA_GROUP + g + 1) * HEAD_DIM]
         for g in range(GQA_GROUP)], axis=0)


def _nt_dot(a, b):
    return lax.dot_general(a, b, (((1,), (1,)), ((), ())), preferred_element_type=F32)


def _pool_delta(u, uext_ref, base, n, cnt_fn):
    parts = []
    for g, w in enumerate(POOL_WINDOWS):
        sl = slice(g * POOL_GROUP_DIM, (g + 1) * POOL_GROUP_DIM)
        acc = u[:, sl]
        for m in range(1, w):
            acc = acc + uext_ref[base - m:base - m + n, sl]
        parts.append(acc / cnt_fn(w) - u[:, sl])
    return parts


def _mixer_tail(o_att, d, h, wpool_ref, pscale_ref, gatt_ref, gpool_ref, wout_ref):
    parts = [jnp.dot(d[:, g * POOL_GROUP_DIM:(g + 1) * POOL_GROUP_DIM].astype(BF16), wpool_ref[g],
                     preferred_element_type=F32) for g in range(POOL_GROUPS)]
    o_pool = jnp.concatenate(parts, axis=-1) * pscale_ref[...]
    mixed = jnp.concatenate([_rms(o_att, gatt_ref[...]), _rms(o_pool, gpool_ref[...])], axis=-1)
    return h + jnp.dot(mixed.astype(BF16), wout_ref[...], preferred_element_type=F32)


def _mixer_prompt_kernel(sinks_ref, h_ref, q_ref, kc_ref, kp_ref, vc_ref, vp_ref, uc_ref, up_ref,
                         wpool_ref, pscale_ref, gatt_ref, gpool_ref, wout_ref, h1_ref, uext_ref):
    j = pl.program_id(1)
    q = q_ref[...]
    k_all = jnp.concatenate([kp_ref[...], kc_ref[...]], axis=0).astype(BF16)
    v_all = jnp.concatenate([vp_ref[...], vc_ref[...]], axis=0).astype(BF16)
    band, col = _band_mask(GQA_GROUP * WINDOW, WINDOW, 2 * WINDOW)
    sinks = [_sink_column(sinks_ref, hk, WINDOW) for hk in range(N_KV_HEADS)]
    bands = []
    for b in range(BQ // WINDOW):
        rows = slice(b * WINDOW, (b + 1) * WINDOW)
        keys = slice(b * WINDOW, (b + 2) * WINDOW)
        mask = band & ((col >= WINDOW) | (j > 0)) if b == 0 else band
        heads = []
        for hk in range(N_KV_HEADS):
            sl = slice(hk * HEAD_DIM, (hk + 1) * HEAD_DIM)
            s = jnp.where(mask, _nt_dot(_stack_heads(q[rows], hk), k_all[keys, sl]), -jnp.inf)
            m = jnp.maximum(jnp.max(s, axis=-1, keepdims=True), sinks[hk])
            e = jnp.exp(s - m)
            den = jnp.sum(e, axis=-1, keepdims=True) + jnp.exp(sinks[hk] - m)
            o = jnp.dot(e.astype(BF16), v_all[keys, sl], preferred_element_type=F32) / den
            heads += [o[g * WINDOW:(g + 1) * WINDOW] for g in range(GQA_GROUP)]
        bands.append(jnp.concatenate(heads, axis=-1))
    o_att = jnp.concatenate(bands, axis=0)

    u = uc_ref[...]
    uext_ref[0:16, :] = jnp.where(j > 0, up_ref[...], 0.0)
    uext_ref[16:16 + BQ, :] = u
    pos = j * BQ + lax.broadcasted_iota(I32, (BQ, 1), 0)
    d = jnp.concatenate(
        _pool_delta(u, uext_ref, 16, BQ, lambda w: jnp.minimum(pos + 1, w).astype(F32)), axis=-1)
    h1_ref[...] = _mixer_tail(o_att, d, h_ref[...], wpool_ref, pscale_ref, gatt_ref, gpool_ref, wout_ref)


def _mixer_sample_kernel(sinks_ref, h_ref, q_ref, kn_ref, vn_ref, u_ref, ck_ref, cv_ref, st_ref,
                         wpool_ref, pscale_ref, gatt_ref, gpool_ref, wout_ref, h1_in_ref,
                         h1_ref, ko_ref, vo_ref, po_ref, uext_ref):
    del h1_in_ref
    n_q = GQA_GROUP * DEC_SEQ
    n_keys = 2 * WINDOW
    band, col = _band_mask(n_q, DEC_SEQ, n_keys)
    mask = (band & (col < WINDOW + DEC_SEQ))[None]
    q3, kn3, vn3, u3 = q_ref[...], kn_ref[...], vn_ref[...], u_ref[...]
    ck, cv = ck_ref[...], cv_ref[...]
    ko_ref[:, 0:WINDOW - DEC_SEQ, :] = ck[:, DEC_SEQ:, :]
    ko_ref[:, WINDOW - DEC_SEQ:WINDOW, :] = kn3
    vo_ref[:, 0:WINDOW - DEC_SEQ, :] = cv[:, DEC_SEQ:, :]
    vo_ref[:, WINDOW - DEC_SEQ:WINDOW, :] = vn3
    pad = jnp.zeros((SB, WINDOW - DEC_SEQ, KV_WIDTH), F32)
    k_all = jnp.concatenate([ck, kn3, pad], axis=1).astype(BF16)
    v_all = jnp.concatenate([cv, vn3, pad], axis=1).astype(BF16)
    heads = []
    for hk in range(N_KV_HEADS):
        sl = slice(hk * HEAD_DIM, (hk + 1) * HEAD_DIM)
        qs = jnp.concatenate(
            [q3[:, :, (hk * GQA_GROUP + g) * HEAD_DIM:(hk * GQA_GROUP + g + 1) * HEAD_DIM]
             for g in range(GQA_GROUP)], axis=1).astype(BF16)
        sink = _sink_column(sinks_ref, hk, DEC_SEQ)[None]
        s = jnp.einsum("bqd,bkd->bqk", qs, k_all[:, :, sl], preferred_element_type=F32)
        s = jnp.where(mask, s, -jnp.inf)
        m = jnp.maximum(jnp.max(s, axis=-1, keepdims=True), sink)
        e = jnp.exp(s - m)
        den = jnp.sum(e, axis=-1, keepdims=True) + jnp.exp(sink - m)
        o = jnp.einsum("bqk,bkd->bqd", e.astype(BF16), v_all[:, :, sl], preferred_element_type=F32) / den
        heads += [o[:, g * DEC_SEQ:(g + 1) * DEC_SEQ, :] for g in range(GQA_GROUP)]
    o_att = jnp.concatenate(heads, axis=-1).reshape(SB * DEC_SEQ, ATT_WIDTH)

    uext_ref[:, 1:16, :] = st_ref[...]
    uext_ref[:, 16:16 + DEC_SEQ, :] = u3
    parts = []
    for g, w in enumerate(POOL_WINDOWS):
        sl = slice(g * POOL_GROUP_DIM, (g + 1) * POOL_GROUP_DIM)
        acc = u3[:, :, sl]
        for back in range(1, w):
            acc = acc + uext_ref[:, 16 - back:16 - back + DEC_SEQ, sl]
        parts.append(acc / float(w) - u3[:, :, sl])
    d = jnp.concatenate(parts, axis=-1).reshape(SB * DEC_SEQ, POOL_WIDTH)
    po_ref[...] = uext_ref[:, 16 + DEC_SEQ - POOL_STATE:16 + DEC_SEQ, :]
    h1_ref[...] = _mixer_tail(o_att, d, h_ref[...], wpool_ref, pscale_ref, gatt_ref, gpool_ref, wout_ref)


def _full_spec(a, n_grid):
    nd = a.ndim
    return pl.BlockSpec(a.shape, lambda *_: (0,) * nd)


def _mixer_prompt(sinks, x2d, q, k, v, u, wts):
    nb = SEQ // BQ
    row = lambda w: pl.BlockSpec((BQ, w), lambda b, j: (b * nb + j, 0))
    prev = lambda w: pl.BlockSpec(
        (WINDOW, w), lambda b, j: (jnp.maximum((b * nb + j) * (BQ // WINDOW) - 1, 0), 0))
    uprev = pl.BlockSpec((16, POOL_WIDTH), lambda b, j: (jnp.maximum((b * nb + j) * (BQ // 16) - 1, 0), 0))
    smem = pl.BlockSpec(memory_space=pltpu.SMEM)
    return pl.pallas_call(
        _mixer_prompt_kernel,
        grid=(BATCH, nb),
        in_specs=[smem, row(D_MODEL), row(ATT_WIDTH), row(KV_WIDTH), prev(KV_WIDTH), row(KV_WIDTH),
                  prev(KV_WIDTH), row(POOL_WIDTH), uprev] + [_full_spec(w, 2) for w in wts],
        out_specs=row(D_MODEL),
        out_shape=jax.ShapeDtypeStruct((T_ALL, D_MODEL), F32),
        scratch_shapes=[pltpu.VMEM((16 + BQ, POOL_WIDTH), F32)],
        compiler_params=_cparams(("parallel", "parallel")),
        name="mixer_prompt",
    )(sinks, x2d, q, k, k, v, v, u, u, *wts)


def _mixer_sample(sinks, x2d, q, k, v, u, cache_k, cache_v, state, wts, h1_buf):
    rows = SB * DEC_SEQ
    row = lambda w: pl.BlockSpec((rows, w), lambda i: (i, 0))
    bat = lambda a: pl.BlockSpec((SB,) + a.shape[1:], lambda i: (i, 0, 0))
    smem = pl.BlockSpec(memory_space=pltpu.SMEM)
    h1_blocks_before = T_P // rows
    n_in = 9 + len(wts)
    q, k, v, u = (a.reshape(DEC_BATCH, DEC_SEQ, a.shape[-1]) for a in (q, k, v, u))
    return pl.pallas_call(
        _mixer_sample_kernel,
        grid=(DEC_BATCH // SB,),
        in_specs=[smem, row(D_MODEL), bat(q), bat(k), bat(v), bat(u),
                  bat(cache_k), bat(cache_v), bat(state)] + [_full_spec(w, 1) for w in wts]
                 + [pl.BlockSpec(memory_space=pl.ANY)],
        out_specs=[pl.BlockSpec((rows, D_MODEL), lambda i: (h1_blocks_before + i, 0)),
                   bat(cache_k), bat(cache_v), bat(state)],
        out_shape=[jax.ShapeDtypeStruct((T_ALL, D_MODEL), F32),
                   jax.ShapeDtypeStruct(cache_k.shape, F32),
                   jax.ShapeDtypeStruct(cache_v.shape, F32),
                   jax.ShapeDtypeStruct(state.shape, F32)],
        scratch_shapes=[pltpu.VMEM((SB, 16 + DEC_SEQ, POOL_WIDTH), F32)],
        input_output_aliases={n_in: 0},
        compiler_params=_cparams(("parallel",)),
        name="mixer_sample",
    )(sinks, x2d, q, k, v, u, cache_k, cache_v, state, *wts, h1_buf)


def _first_max(vals, iota, n):
    m = jnp.max(vals, axis=0, keepdims=True)
    idx = jnp.min(jnp.where(vals == m, iota, n), axis=0, keepdims=True)
    return m, idx


def _router_kernel(h1_ref, gffn_ref, wrt_ref, bias_ref, xp_ref, idx_ref, wts_ref):
    xn = _rms(h1_ref[...], gffn_ref[...])
    logits = lax.dot_general(wrt_ref[...], xn, (((1,), (1,)), ((), ())),
                             precision=lax.Precision.HIGHEST, preferred_element_type=F32)
    scores = _sigmoid(logits)
    biased = scores + bias_ref[...]
    n_tok = biased.shape[1]
    neg = -jnp.inf

    iota_g = lax.broadcasted_iota(I32, (GROUP_SIZE, n_tok), 0)
    grp_rows = []
    for g in range(N_EXPERT_GROUPS):
        blk = biased[g * GROUP_SIZE:(g + 1) * GROUP_SIZE, :]
        top1, i1 = _first_max(blk, iota_g, GROUP_SIZE)
        top2 = jnp.max(jnp.where(iota_g == i1, neg, blk), axis=0, keepdims=True)
        grp_rows.append(top1 + top2)
    gs = jnp.concatenate(grp_rows, axis=0)

    iota_n = lax.broadcasted_iota(I32, (N_EXPERT_GROUPS, n_tok), 0)
    gsel = jnp.zeros((N_EXPERT_GROUPS, n_tok), jnp.bool_)
    for _ in range(TOPK_GROUPS):
        _, gi = _first_max(gs, iota_n, N_EXPERT_GROUPS)
        hit = iota_n == gi
        gsel = gsel | hit
        gs = jnp.where(hit, neg, gs)
    emask = jnp.concatenate(
        [jnp.broadcast_to(gsel[g:g + 1, :], (GROUP_SIZE, n_tok)) for g in range(N_EXPERT_GROUPS)], axis=0)
    masked = jnp.where(emask, biased, neg)

    iota_e = lax.broadcasted_iota(I32, (N_EXPERTS, n_tok), 0)
    idx_rows, sel_rows = [], []
    for _ in range(TOP_K):
        _, ei = _first_max(masked, iota_e, N_EXPERTS)
        hit = iota_e == ei
        idx_rows.append(ei)
        sel_rows.append(jnp.sum(jnp.where(hit, scores, 0.0), axis=0, keepdims=True))
        masked = jnp.where(hit, neg, masked)
    sel = jnp.concatenate(sel_rows, axis=0)
    idx_ref[...] = jnp.concatenate(idx_rows, axis=0)
    wts_ref[...] = sel / jnp.sum(sel, axis=0, keepdims=True) * ROUTED_SCALE
    _store_chunks(xp_ref, _pack_bf16_pairs(xn))


def _router(h1, group, g_ffn, w_router_t, bias_col):
    blk0 = group * T_GRP // BM_R
    colblk = pl.BlockSpec((TOP_K, BM_R), lambda i: (0, i))
    ws = [g_ffn, w_router_t, bias_col]
    return pl.pallas_call(
        _router_kernel,
        grid=(T_GRP // BM_R,),
        in_specs=[pl.BlockSpec((BM_R, D_MODEL), lambda i: (blk0 + i, 0))] + [_full_spec(w, 1) for w in ws],
        out_specs=[pl.BlockSpec((BM_R * ROW_CHUNKS, LANES), lambda i: (i, 0)), colblk, colblk],
        out_shape=[jax.ShapeDtypeStruct((T_GRP * ROW_CHUNKS, LANES), U32),
                   jax.ShapeDtypeStruct((TOP_K, T_GRP), I32),
                   jax.ShapeDtypeStruct((TOP_K, T_GRP), F32)],
        compiler_params=_cparams(("parallel",)),
        name="router",
    )(h1, *ws)


def _rank_kernel(idx_ref, tri_ref, rank_ref, cnt_ref, carry_ref):
    @pl.when(pl.program_id(0) == 0)
    def _():
        carry_ref[...] = jnp.zeros_like(carry_ref)

    idx = idx_ref[...]
    n_tok = idx.shape[1]
    iota_e = lax.broadcasted_iota(I32, (N_EXPERTS, n_tok), 0)
    member = jnp.zeros((N_EXPERTS, n_tok), F32)
    for k in range(TOP_K):
        member = member + jnp.where(iota_e == idx[k:k + 1, :], 1.0, 0.0)
    before = jnp.dot(member.astype(BF16), tri_ref[...], preferred_element_type=F32) + carry_ref[...]
    rows = [jnp.sum(jnp.where(iota_e == idx[k:k + 1, :], before, 0.0), axis=0, keepdims=True)
            for k in range(TOP_K)]
    rank_ref[...] = jnp.concatenate(rows, axis=0).astype(I32)
    carry_ref[...] = carry_ref[...] + jnp.sum(member, axis=1, keepdims=True)
    cnt_ref[...] = carry_ref[...].astype(I32)


def _rank(idx_t, tri):
    blk = pl.BlockSpec((TOP_K, BT_RANK), lambda i: (0, i))
    return pl.pallas_call(
        _rank_kernel,
        grid=(T_GRP // BT_RANK,),
        in_specs=[blk, _full_spec(tri, 1)],
        out_specs=[blk, pl.BlockSpec((N_EXPERTS, 1), lambda i: (0, 0))],
        out_shape=[jax.ShapeDtypeStruct((TOP_K, T_GRP), I32),
                   jax.ShapeDtypeStruct((N_EXPERTS, 1), I32)],
        scratch_shapes=[pltpu.VMEM((N_EXPERTS, 1), F32)],
        compiler_params=_cparams(("arbitrary",)),
        name="rank",
    )(idx_t, tri)


def _dest_kernel(idx_ref, rank_ref, cnt_ref, dest_ref, blk_e_ref, n_used_ref):
    counts = cnt_ref[...]
    padded = (counts + (BM_E - 1)) // BM_E * BM_E
    r = lax.broadcasted_iota(I32, (N_EXPERTS, N_EXPERTS), 0)
    c = lax.broadcasted_iota(I32, (N_EXPERTS, N_EXPERTS), 1)
    padded_row = jnp.sum(jnp.where(r == c, padded, 0), axis=0, keepdims=True)
    pad_start = jnp.sum(jnp.where(c < r, padded_row, 0), axis=1, keepdims=True)

    idx = idx_ref[...]
    n_tok = idx.shape[1]
    iota_e = lax.broadcasted_iota(I32, (N_EXPERTS, n_tok), 0)
    rows = [jnp.sum(jnp.where(iota_e == idx[k:k + 1, :], pad_start, 0), axis=0, keepdims=True)
            for k in range(TOP_K)]
    dest_ref[...] = jnp.concatenate(rows, axis=0) + rank_ref[...]

    pad_end_row = jnp.sum(jnp.where(r <= c, padded, 0), axis=0, keepdims=True)
    b0 = lax.broadcasted_iota(I32, (N_BLOCKS_PAD, N_EXPERTS), 0) * BM_E
    be = jnp.sum(jnp.where(pad_end_row <= b0, 1, 0), axis=1, keepdims=True)
    blk_e_ref[...] = jnp.minimum(be, N_EXPERTS - 1)
    n_used_ref[...] = pad_end_row[:, N_EXPERTS - 1:N_EXPERTS] // BM_E


N_BLOCKS_PAD = (N_BLOCKS + 7) // 8 * 8


def _dest(idx_t, rank_t, counts):
    blk = pl.BlockSpec((TOP_K, BT_RANK), lambda i: (0, i))
    one = lambda s: pl.BlockSpec(s, lambda i: (0, 0))
    return pl.pallas_call(
        _dest_kernel,
        grid=(T_GRP // BT_RANK,),
        in_specs=[blk, blk, one((N_EXPERTS, 1))],
        out_specs=[blk, one((N_BLOCKS_PAD, 1)), one((1, 1))],
        out_shape=[jax.ShapeDtypeStruct((TOP_K, T_GRP), I32),
                   jax.ShapeDtypeStruct((N_BLOCKS_PAD, 1), I32),
                   jax.ShapeDtypeStruct((1, 1), I32)],
        compiler_params=_cparams(("arbitrary",)),
        name="dest",
    )(idx_t, rank_t, counts)


def _sc_mesh():
    return plsc.VectorSubcoreMesh(core_axis_name="c", subcore_axis_name="s")


def _sc_worker_id():
    return lax.axis_index("s") * SC_CORES + lax.axis_index("c")


def _dispatch_body(dest_hbm, xp_hbm, xs_hbm, idx_v, rows_v, sem_in, sem_out):
    n_chunks, _, n_tok = dest_hbm.shape
    per_worker = n_chunks // SC_WORKERS
    chunk0 = _sc_worker_id() * per_worker

    def loads(i):
        chunk = chunk0 + i
        t0 = pl.multiple_of(chunk * n_tok, n_tok)
        return (pltpu.make_async_copy(dest_hbm.at[chunk], idx_v.at[i % 2], sem_in.at[i % 2]),
                pltpu.make_async_copy(xp_hbm.at[pl.ds(t0, n_tok)], rows_v.at[i % 2], sem_in.at[i % 2]))

    def scatters(i):
        return [pltpu.make_async_copy(rows_v.at[i % 2], xs_hbm.at[idx_v.at[i % 2, k]], sem_out.at[i % 2])
                for k in range(TOP_K)]

    for cp in loads(0):
        cp.start()
    for i in range(per_worker):
        for cp in loads(i):
            cp.wait()
        if i >= 1:
            for cp in scatters(i - 1):
                cp.wait()
        if i + 1 < per_worker:
            for cp in loads(i + 1):
                cp.start()
        for cp in scatters(i):
            cp.start()
    for cp in scatters(per_worker - 1):
        cp.wait()


def _dispatch(dest_chunks, xp3):
    return pl.kernel(
        _dispatch_body,
        out_type=jax.ShapeDtypeStruct((N_SLOTS, ROW_CHUNKS, LANES), U32),
        mesh=_sc_mesh(),
        scratch_types=[pltpu.VMEM((2, TOP_K, SC_CHUNK), I32),
                       pltpu.VMEM((2, SC_CHUNK, ROW_CHUNKS, LANES), U32),
                       pltpu.SemaphoreType.DMA((2,)), pltpu.SemaphoreType.DMA((2,))],
        name="dispatch",
    )(dest_chunks, xp3)


def _gather_body(dest_hbm, ys_hbm, yt_hbm, idx_v, rows_v, sem_in, sem_out):
    n_chunks, _, n_tok = dest_hbm.shape
    per_worker = n_chunks // SC_WORKERS
    chunk0 = _sc_worker_id() * per_worker

    @pl.loop(0, per_worker)
    def _(i):
        chunk = chunk0 + i
        t0 = pl.multiple_of(chunk * n_tok, n_tok)
        pltpu.sync_copy(dest_hbm.at[chunk], idx_v)

        def gather(k):
            return pltpu.make_async_copy(ys_hbm.at[idx_v.at[k]], rows_v.at[k % SC_RING], sem_in.at[k % SC_RING])

        def store(k):
            return pltpu.make_async_copy(rows_v.at[k % SC_RING], yt_hbm.at[k, pl.ds(t0, n_tok)],
                                         sem_out.at[k % SC_RING])

        for k in range(SC_RING):
            gather(k).start()
        for k in range(TOP_K):
            gather(k).wait()
            store(k).start()
            if k + SC_RING < TOP_K:
                store(k).wait()
                gather(k + SC_RING).start()
        for k in range(TOP_K - SC_RING, TOP_K):
            store(k).wait()


def _gather(dest_chunks, ys3):
    return pl.kernel(
        _gather_body,
        out_type=jax.ShapeDtypeStruct((TOP_K, T_GRP, ROW_CHUNKS, LANES), U32),
        mesh=_sc_mesh(),
        scratch_types=[pltpu.VMEM((TOP_K, SC_CHUNK), I32),
                       pltpu.VMEM((SC_RING, SC_CHUNK, ROW_CHUNKS, LANES), U32),
                       pltpu.SemaphoreType.DMA((SC_RING,)), pltpu.SemaphoreType.DMA((SC_RING,))],
        name="gather",
    )(dest_chunks, ys3)


def _experts_kernel(blk_e_ref, n_used_ref, xs_ref, wg_ref, wu_ref, wd_ref, ys_ref, wgu_s, wd_s):
    b = pl.program_id(0)

    @pl.when(b < n_used_ref[0])
    def _():
        prev = blk_e_ref[jnp.maximum(b - 1, 0)]

        @pl.when((b == 0) | (blk_e_ref[b] != prev))
        def _():
            wgu_s[:, :D_EXPERT] = wg_ref[0].astype(BF16)
            wgu_s[:, D_EXPERT:] = wu_ref[0].astype(BF16)
            wd_s[...] = wd_ref[0].astype(BF16)

        chunks = _load_chunks(xs_ref, BM_E)
        x_lo = jnp.concatenate([_unpack_lo(p) for p in chunks], axis=-1).astype(BF16)
        x_hi = jnp.concatenate([_unpack_hi(p) for p in chunks], axis=-1).astype(BF16)
        gu = (jnp.dot(x_lo, wgu_s[:HALF, :], preferred_element_type=F32)
              + jnp.dot(x_hi, wgu_s[HALF:, :], preferred_element_type=F32))
        gate, up = gu[:, :D_EXPERT], gu[:, D_EXPERT:]
        hmid = (gate * _sigmoid(gate) * up).astype(BF16)
        _store_chunks(ys_ref, _pack_bf16_pairs(jnp.dot(hmid, wd_s[...], preferred_element_type=F32)))


def _experts(blk_e, n_used, xs, wg, wu, wd):
    def blk(b, be, nu):
        return jnp.minimum(b, nu[0] - 1)

    grid_spec = pltpu.PrefetchScalarGridSpec(
        num_scalar_prefetch=2,
        grid=(N_BLOCKS,),
        in_specs=[pl.BlockSpec((BM_E * ROW_CHUNKS, LANES), lambda b, be, nu: (blk(b, be, nu), 0)),
                  pl.BlockSpec((1, D_MODEL, D_EXPERT), lambda b, be, nu: (be[blk(b, be, nu)], 0, 0)),
                  pl.BlockSpec((1, D_MODEL, D_EXPERT), lambda b, be, nu: (be[blk(b, be, nu)], 0, 0)),
                  pl.BlockSpec((1, D_EXPERT, D_MODEL), lambda b, be, nu: (be[blk(b, be, nu)], 0, 0))],
        out_specs=pl.BlockSpec((BM_E * ROW_CHUNKS, LANES), lambda b, be, nu: (blk(b, be, nu), 0)),
        scratch_shapes=[pltpu.VMEM((D_MODEL, 2 * D_EXPERT), BF16), pltpu.VMEM((D_EXPERT, D_MODEL), BF16)],
    )
    return pl.pallas_call(
        _experts_kernel,
        grid_spec=grid_spec,
        out_shape=jax.ShapeDtypeStruct((N_SLOTS * ROW_CHUNKS, LANES), U32),
        compiler_params=_cparams(("arbitrary",)),
        name="experts",
    )(blk_e, n_used, xs, wg, wu, wd)


def _combine_kernel(yt_ref, wts_ref, h1_ref, p_ref, gffn_ref, wsgu_ref, wsd_ref, gple_ref, wpg_ref, wpp_ref,
                    gfin_ref, *y_refs):
    y_ref = y_refs[-1]
    h1 = h1_ref[...]
    gu = jnp.dot(_rms(h1, gffn_ref[...]).astype(BF16), wsgu_ref[...], preferred_element_type=F32)
    sgate, sup = gu[:, :D_SHARED], gu[:, D_SHARED:]
    hsh = h1 + jnp.dot((sgate * _sigmoid(sgate) * sup).astype(BF16), wsd_ref[...], preferred_element_type=F32)
    wts = jnp.transpose(wts_ref[...])
    lo = [jnp.zeros((BT_COMB, LANES), F32) for _ in range(ROW_CHUNKS)]
    hi = [jnp.zeros((BT_COMB, LANES), F32) for _ in range(ROW_CHUNKS)]
    for k in range(TOP_K):
        w = wts[:, k:k + 1]
        for c, p in enumerate(_load_chunks(yt_ref, BT_COMB, lead=(k,))):
            lo[c] = lo[c] + w * _unpack_lo(p)
            hi[c] = hi[c] + w * _unpack_hi(p)
    h2 = hsh + jnp.concatenate(lo + hi, axis=-1)
    gate = _sigmoid(jnp.dot(_rms(h2, gple_ref[...]).astype(BF16), wpg_ref[...], preferred_element_type=F32))
    proj = jnp.dot(p_ref[...].astype(BF16), wpp_ref[...], preferred_element_type=F32)
    y_ref[...] = _rms(h2 + proj * gate, gfin_ref[...])


def _combine(yt, yt_row0, wts_t, h1, tok_row0, n_rows, p2d, p_row0, ws, y_prev, out_rows, out_row0):
    assert all(r % BT_COMB == 0 for r in (yt_row0, tok_row0, n_rows, p_row0, out_row0))
    g0, t0, p0, o0 = yt_row0 // BT_COMB, tok_row0 // BT_COMB, p_row0 // BT_COMB, out_row0 // BT_COMB
    in_specs = [pl.BlockSpec((TOP_K, BT_COMB * ROW_CHUNKS, LANES), lambda i: (0, g0 + i, 0)),
                pl.BlockSpec((TOP_K, BT_COMB), lambda i: (0, g0 + i)),
                pl.BlockSpec((BT_COMB, D_MODEL), lambda i: (t0 + i, 0)),
                pl.BlockSpec((BT_COMB, PLE_DIM), lambda i: (p0 + i, 0))] + [_full_spec(w, 1) for w in ws]
    args = [yt, wts_t, h1, p2d, *ws]
    aliases = {}
    if y_prev is not None:
        in_specs.append(pl.BlockSpec(memory_space=pl.ANY))
        aliases = {len(args): 0}
        args.append(y_prev)
    return pl.pallas_call(
        _combine_kernel,
        grid=(n_rows // BT_COMB,),
        in_specs=in_specs,
        out_specs=pl.BlockSpec((BT_COMB, D_MODEL), lambda i: (o0 + i, 0)),
        out_shape=jax.ShapeDtypeStruct((out_rows, D_MODEL), F32),
        input_output_aliases=aliases,
        compiler_params=_cparams(("parallel",)),
        name="combine",
    )(*args)


def kernel(x_prompt, x_sample, cache_k, cache_v, state_pool, p_prompt, p_sample, g_mix, w_in, attn_sinks,
           w_pool, pool_scale, g_att_out, g_pool_out, w_out, g_ffn, w_router, router_bias, w_exp_gate,
           w_exp_up, w_exp_down, w_sh_gate, w_sh_up, w_sh_down, g_ple, w_ple_gate, w_ple_proj, g_final):
    row = lambda a: a.reshape(1, -1)
    xp2d = x_prompt.reshape(T_P, D_MODEL)
    xs2d = x_sample.reshape(T_S, D_MODEL)
    w_in_bf = w_in[0].astype(BF16)
    mixer_wts = [w_pool[0].astype(BF16), row(pool_scale[0]), row(g_att_out[0]), row(g_pool_out[0]),
                 w_out[0].astype(BF16)]

    tab_p = _rope_tables(np.arange(SEQ))
    tab_s = _rope_tables(PAST_LEN + np.arange(DEC_SEQ), reps=BM_IN // DEC_SEQ)

    q_p, k_p, v_p, u_p = _inproj(xp2d, row(g_mix[0]), w_in_bf, tab_p, BF16)
    q_s, k_s, v_s, u_s = _inproj(xs2d, row(g_mix[0]), w_in_bf, tab_s, F32)

    h1 = _mixer_prompt(attn_sinks[0], xp2d, q_p, k_p, v_p, u_p, mixer_wts)
    h1, k_sample, v_sample, pool_sample = _mixer_sample(
        attn_sinks[0], xs2d, q_s, k_s, v_s, u_s,
        cache_k[0].reshape(DEC_BATCH, WINDOW, KV_WIDTH), cache_v[0].reshape(DEC_BATCH, WINDOW, KV_WIDTH),
        state_pool[0], mixer_wts, h1)

    g_ffn_row = row(g_ffn[0])
    router_wts = (g_ffn_row, w_router[0].T, router_bias[0].reshape(N_EXPERTS, 1))
    tri = (lax.broadcasted_iota(I32, (BT_RANK, BT_RANK), 0)
           < lax.broadcasted_iota(I32, (BT_RANK, BT_RANK), 1)).astype(BF16)

    groups = []
    for g in range(N_GROUPS):
        xp, idx_t, wts_t = _router(h1, g, *router_wts)
        rank_t, counts = _rank(idx_t, tri)
        dest_t, blk_e, n_used = _dest(idx_t, rank_t, counts)
        dest_chunks = dest_t.reshape(TOP_K, T_GRP // SC_CHUNK, SC_CHUNK).transpose(1, 0, 2)
        xs = _dispatch(dest_chunks, xp.reshape(T_GRP, ROW_CHUNKS, LANES))
        groups.append((wts_t, dest_chunks, xs, blk_e, n_used))

    ple_wts = [g_ffn_row, jnp.concatenate([w_sh_gate[0], w_sh_up[0]], axis=1).astype(BF16),
               w_sh_down[0].astype(BF16),
               row(g_ple[0]), w_ple_gate[0].astype(BF16), w_ple_proj[0].astype(BF16), row(g_final)]
    pp2d = p_prompt[0].reshape(T_P, PLE_DIM)
    ps2d = p_sample[0].reshape(T_S, PLE_DIM)
    y_p = y_s = None
    for g, (wts_t, dest_chunks, xs, blk_e, n_used) in enumerate(groups):
        lo, hi = g * T_GRP, (g + 1) * T_GRP
        ys = _experts(blk_e.reshape(N_BLOCKS_PAD), n_used.reshape(1), xs.reshape(N_SLOTS * ROW_CHUNKS, LANES),
                      w_exp_gate[0], w_exp_up[0], w_exp_down[0])
        yt = _gather(dest_chunks, ys.reshape(N_SLOTS, ROW_CHUNKS, LANES))
        yt = yt.reshape(TOP_K, T_GRP * ROW_CHUNKS, LANES)
        if lo < T_P:
            n = min(hi, T_P) - lo
            y_p = _combine(yt, 0, wts_t, h1, lo, n, pp2d, lo, ple_wts, y_p, T_P, lo)
        if hi > T_P:
            s0 = max(lo, T_P)
            y_s = _combine(yt, s0 - lo, wts_t, h1, s0, hi - s0, ps2d, s0 - T_P, ple_wts, y_s, T_S, s0 - T_P)

    kv5 = lambda a, b: a.reshape(1, b, WINDOW, N_KV_HEADS, HEAD_DIM)
    k_prompt = kv5(k_p.reshape(BATCH, SEQ, KV_WIDTH)[:, SEQ - WINDOW:], BATCH)
    v_prompt = kv5(v_p.reshape(BATCH, SEQ, KV_WIDTH)[:, SEQ - WINDOW:], BATCH)
    pool_prompt = u_p.reshape(BATCH, SEQ, POOL_WIDTH)[:, SEQ - POOL_STATE:][None]
    return (y_p.reshape(BATCH, SEQ, D_MODEL), y_s.reshape(DEC_BATCH, DEC_SEQ, D_MODEL),
            k_prompt, v_prompt, pool_prompt,
            kv5(k_sample, DEC_BATCH), kv5(v_sample, DEC_BATCH), pool_sample[None])
```

```python
import functools

import numpy as np
import jax
import jax.numpy as jnp
from jax import lax
from jax.experimental import pallas as pl
from jax.experimental.pallas import tpu as pltpu
from jax.experimental.pallas import tpu_sc as plsc

F32 = jnp.float32
BF16 = jnp.bfloat16
U32 = jnp.uint32
I32 = jnp.int32

D_MODEL = 1024
BATCH = 8
SEQ = 2048
DEC_BATCH = 128
DEC_SEQ = 8
PAST_LEN = 16384
N_Q_HEADS = 8
N_KV_HEADS = 2
HEAD_DIM = 64
GQA_GROUP = N_Q_HEADS // N_KV_HEADS
ATT_WIDTH = N_Q_HEADS * HEAD_DIM
KV_WIDTH = N_KV_HEADS * HEAD_DIM
WINDOW = 128
ROPE_THETA = 500000.0
ROT_DIM = HEAD_DIM // 4
POOL_WINDOWS = (2, 4, 8, 16)
POOL_GROUPS = 4
POOL_WIDTH = D_MODEL - ATT_WIDTH
POOL_GROUP_DIM = POOL_WIDTH // POOL_GROUPS
POOL_STATE = 15
IN_WIDTH = ATT_WIDTH + 2 * KV_WIDTH + POOL_WIDTH
N_EXPERTS = 64
TOP_K = 8
N_EXPERT_GROUPS = 8
GROUP_SIZE = N_EXPERTS // N_EXPERT_GROUPS
TOPK_GROUPS = 4
D_EXPERT = 256
D_SHARED = 256
ROUTED_SCALE = 2.5
PLE_DIM = 256
EPS = 1e-6

T_P = BATCH * SEQ
T_S = DEC_BATCH * DEC_SEQ
T_ALL = T_P + T_S
HALF = D_MODEL // 2
LANES = 128
VMEM_LIMIT = 48 * 1024 * 1024

BM_IN = 512
BQ = 2 * WINDOW
SB = 16
BM_R = 512
BT_RANK = 512
BT_COMB = 512
N_GROUPS = 2
T_GRP = T_ALL // N_GROUPS
assert T_GRP * N_GROUPS == T_ALL
BM_E = 1280
N_ASSIGN = T_GRP * TOP_K
N_BLOCKS = -(-N_ASSIGN // BM_E) + N_EXPERTS
N_SLOTS = N_BLOCKS * BM_E

ROW_CHUNKS = HALF // LANES
SC_CORES = 2
SC_SUBCORES = 16
SC_WORKERS = SC_CORES * SC_SUBCORES
SC_CHUNK = 16
SC_RING = 4
assert T_GRP % (SC_WORKERS * SC_CHUNK) == 0


def _load_chunks(ref, n_rows, lead=()):
    return [ref[lead + (pl.ds(c, n_rows, stride=ROW_CHUNKS), slice(None))] for c in range(ROW_CHUNKS)]


def _store_chunks(ref, packed):
    n_rows = packed.shape[0]
    for c in range(ROW_CHUNKS):
        ref[pl.ds(c, n_rows, stride=ROW_CHUNKS), :] = packed[:, c * LANES:(c + 1) * LANES]


def _cparams(sem):
    return pltpu.CompilerParams(dimension_semantics=sem, vmem_limit_bytes=VMEM_LIMIT)


def _rms(x, g):
    return x * lax.rsqrt(jnp.mean(x * x, axis=-1, keepdims=True) + EPS) * g


def _sigmoid(x):
    return 1.0 / (1.0 + jnp.exp(-x))


def _pack_bf16_pairs(x):
    h = x.shape[-1] // 2
    return pltpu.pack_elementwise([x[:, :h], x[:, h:]], packed_dtype=BF16)


def _unpack_lo(p):
    return pltpu.bitcast(p << 16, F32)


def _unpack_hi(p):
    return pltpu.bitcast(p & jnp.uint32(0xFFFF0000), F32)


def _inproj_kernel(x_ref, g_ref, w_ref, c_ref, s1_ref, s2_ref, q_ref, k_ref, v_ref, u_ref):
    xn = _rms(x_ref[...], g_ref[...]).astype(BF16)
    z = jnp.dot(xn, w_ref[...], preferred_element_type=F32)
    c, s1, s2 = c_ref[...], s1_ref[...], s2_ref[...]

    def rope(t):
        return t * c + pltpu.roll(t, LANES - ROT_DIM // 2, 1) * s1 + pltpu.roll(t, ROT_DIM // 2, 1) * s2

    for i in range(ATT_WIDTH // LANES):
        sl = slice(i * LANES, (i + 1) * LANES)
        q_ref[:, sl] = (rope(z[:, sl]) * (HEAD_DIM ** -0.5)).astype(q_ref.dtype)
    k_ref[...] = rope(z[:, ATT_WIDTH:ATT_WIDTH + KV_WIDTH])
    v_ref[...] = z[:, ATT_WIDTH + KV_WIDTH:ATT_WIDTH + 2 * KV_WIDTH]
    u_ref[...] = z[:, ATT_WIDTH + 2 * KV_WIDTH:]


def _rope_tables(pos, reps=1):
    f32 = np.float32
    half = ROT_DIM // 2
    inv = np.power(f32(ROPE_THETA), -np.arange(half, dtype=f32) * f32(2.0) / f32(ROT_DIM)).astype(f32)
    ang = np.asarray(pos, f32)[:, None] * inv[None, :]
    cos, sin = np.cos(ang).astype(f32), np.sin(ang).astype(f32)
    n = len(pos)
    ones = np.ones((n, HEAD_DIM - ROT_DIM), f32)
    zeros = np.zeros((n, HEAD_DIM - ROT_DIM), f32)
    zh = np.zeros((n, half), f32)
    c = np.concatenate([cos, cos, ones], axis=1)
    s1 = np.concatenate([-sin, zh, zeros], axis=1)
    s2 = np.concatenate([zh, sin, zeros], axis=1)
    tile = lambda a: np.tile(a, (reps, LANES // HEAD_DIM))
    return tile(c), tile(s1), tile(s2)


def _inproj(x2d, g_mix, w_in_bf, tables, q_dtype):
    rows = x2d.shape[0]
    n_tab = tables[0].shape[0] // BM_IN
    row_spec = lambda w: pl.BlockSpec((BM_IN, w), lambda i: (i, 0))
    tab_spec = pl.BlockSpec((BM_IN, LANES), lambda i: (i % n_tab, 0))
    full = lambda a: pl.BlockSpec(a.shape, lambda i: (0,) * a.ndim)
    return pl.pallas_call(
        _inproj_kernel,
        grid=(rows // BM_IN,),
        in_specs=[row_spec(D_MODEL), full(g_mix), full(w_in_bf), tab_spec, tab_spec, tab_spec],
        out_specs=[row_spec(ATT_WIDTH), row_spec(KV_WIDTH), row_spec(KV_WIDTH), row_spec(POOL_WIDTH)],
        out_shape=[jax.ShapeDtypeStruct((rows, ATT_WIDTH), q_dtype),
                   jax.ShapeDtypeStruct((rows, KV_WIDTH), F32),
                   jax.ShapeDtypeStruct((rows, KV_WIDTH), F32),
                   jax.ShapeDtypeStruct((rows, POOL_WIDTH), F32)],
        compiler_params=_cparams(("parallel",)),
        name="inproj",
    )(x2d, g_mix, w_in_bf, *tables)


def _sink_column(sinks_ref, kv_head, rows_per_head):
    n = GQA_GROUP * rows_per_head
    grp = lax.broadcasted_iota(I32, (n, 1), 0) // rows_per_head
    col = jnp.full((n, 1), sinks_ref[kv_head * GQA_GROUP], F32)
    for g in range(1, GQA_GROUP):
        col = jnp.where(grp == g, sinks_ref[kv_head * GQA_GROUP + g], col)
    return col


def _band_mask(n_rows, rows_per_head, n_keys):
    i = lax.broadcasted_iota(I32, (n_rows, n_keys), 0) % rows_per_head
    c = lax.broadcasted_iota(I32, (n_rows, n_keys), 1)
    return (c >= i) & (c <= i + WINDOW), c


def _stack_heads(q, kv_head):
    return jnp.concatenate(
        [q[:, (kv_head * GQA_GROUP + g) * HEAD_DIM:(kv_head * GQA_GROUP + g + 1) * HEAD_DIM]
         for g in range(GQA_GROUP)], axis=0)


def _nt_dot(a, b):
    return lax.dot_general(a, b, (((1,), (1,)), ((), ())), preferred_element_type=F32)


POOL_HALO = 16
POOL_HEAD = 8
assert all(w == 2 << g for g, w in enumerate(POOL_WINDOWS)) and POOL_WINDOWS[-1] - 1 <= POOL_HALO


def _window_sums(ext_ref, n):
    lo, hi = POOL_HEAD, POOL_HEAD + POOL_HALO + n
    for p in range(POOL_GROUPS):
        lanes = slice(p * POOL_GROUP_DIM, POOL_WIDTH)
        ext_ref[lo:hi, lanes] = ext_ref[lo:hi, lanes] + ext_ref[lo - (1 << p):hi - (1 << p), lanes]


def _mixer_tail(o_att, d, h, wpool_ref, pscale_ref, gatt_ref, gpool_ref, wout_ref):
    parts = [jnp.dot(d[:, g * POOL_GROUP_DIM:(g + 1) * POOL_GROUP_DIM].astype(BF16), wpool_ref[g],
                     preferred_element_type=F32) for g in range(POOL_GROUPS)]
    o_pool = jnp.concatenate(parts, axis=-1) * pscale_ref[...]
    mixed = jnp.concatenate([_rms(o_att, gatt_ref[...]), _rms(o_pool, gpool_ref[...])], axis=-1)
    return h + jnp.dot(mixed.astype(BF16), wout_ref[...], preferred_element_type=F32)


def _mixer_prompt_kernel(sinks_ref, h_ref, q_ref, kc_ref, kp_ref, vc_ref, vp_ref, uc_ref, up_ref,
                         wpool_ref, pscale_ref, gatt_ref, gpool_ref, wout_ref, h1_ref, uext_ref):
    j = pl.program_id(1)
    q = q_ref[...]
    k_all = jnp.concatenate([kp_ref[...], kc_ref[...]], axis=0).astype(BF16)
    v_all = jnp.concatenate([vp_ref[...], vc_ref[...]], axis=0).astype(BF16)
    ones = jnp.ones((WINDOW + BQ, HEAD_DIM), BF16)
    v_ones = [jnp.concatenate([v_all[:, hk * HEAD_DIM:(hk + 1) * HEAD_DIM], ones], axis=1)
              for hk in range(N_KV_HEADS)]
    band, col = _band_mask(GQA_GROUP * WINDOW, WINDOW, 2 * WINDOW)
    sinks = [_sink_column(sinks_ref, hk, WINDOW) for hk in range(N_KV_HEADS)]
    bands = []
    for b in range(BQ // WINDOW):
        rows = slice(b * WINDOW, (b + 1) * WINDOW)
        keys = slice(b * WINDOW, (b + 2) * WINDOW)
        mask = band & ((col >= WINDOW) | (j > 0)) if b == 0 else band
        heads = []
        for hk in range(N_KV_HEADS):
            sl = slice(hk * HEAD_DIM, (hk + 1) * HEAD_DIM)
            s = jnp.where(mask, _nt_dot(_stack_heads(q[rows], hk), k_all[keys, sl]), -jnp.inf)
            m = jnp.maximum(jnp.max(s, axis=-1, keepdims=True), sinks[hk])
            e = jnp.exp(s - m).astype(BF16)
            ov = jnp.dot(e, v_ones[hk][keys], preferred_element_type=F32)
            den = ov[:, HEAD_DIM:HEAD_DIM + 1] + jnp.exp(sinks[hk] - m)
            o = ov[:, :HEAD_DIM] / den
            heads += [o[g * WINDOW:(g + 1) * WINDOW] for g in range(GQA_GROUP)]
        bands.append(jnp.concatenate(heads, axis=-1))
    o_att = jnp.concatenate(bands, axis=0)

    u = uc_ref[...]
    base = POOL_HEAD + POOL_HALO
    uext_ref[0:POOL_HEAD, :] = jnp.zeros((POOL_HEAD, POOL_WIDTH), F32)
    uext_ref[POOL_HEAD:base, :] = jnp.where(j > 0, up_ref[...], 0.0)
    uext_ref[base:base + BQ, :] = u
    _window_sums(uext_ref, BQ)
    pos = j * BQ + lax.broadcasted_iota(I32, (BQ, 1), 0)
    parts = []
    for g, w in enumerate(POOL_WINDOWS):
        sl = slice(g * POOL_GROUP_DIM, (g + 1) * POOL_GROUP_DIM)
        parts.append(uext_ref[base:base + BQ, sl] / jnp.minimum(pos + 1, w).astype(F32) - u[:, sl])
    d = jnp.concatenate(parts, axis=-1)
    h1_ref[...] = _mixer_tail(o_att, d, h_ref[...], wpool_ref, pscale_ref, gatt_ref, gpool_ref, wout_ref)


def _mixer_sample_kernel(sinks_ref, h_ref, q_ref, kn_ref, vn_ref, u_ref, ck_ref, cv_ref, st_ref,
                         wpool_ref, pscale_ref, gatt_ref, gpool_ref, wout_ref, h1_in_ref,
                         h1_ref, ko_ref, vo_ref, po_ref, uext_ref):
    del h1_in_ref
    n_q = GQA_GROUP * DEC_SEQ
    n_keys = 2 * WINDOW
    band, col = _band_mask(n_q, DEC_SEQ, n_keys)
    mask = (band & (col < WINDOW + DEC_SEQ))[None]
    q3, kn3, vn3, u3 = q_ref[...], kn_ref[...], vn_ref[...], u_ref[...]
    ck, cv = ck_ref[...], cv_ref[...]
    ko_ref[:, 0:WINDOW - DEC_SEQ, :] = ck[:, DEC_SEQ:, :]
    ko_ref[:, WINDOW - DEC_SEQ:WINDOW, :] = kn3
    vo_ref[:, 0:WINDOW - DEC_SEQ, :] = cv[:, DEC_SEQ:, :]
    vo_ref[:, WINDOW - DEC_SEQ:WINDOW, :] = vn3
    pad = jnp.zeros((SB, WINDOW - DEC_SEQ, KV_WIDTH), F32)
    k_all = jnp.concatenate([ck, kn3, pad], axis=1).astype(BF16)
    v_all = jnp.concatenate([cv, vn3, pad], axis=1).astype(BF16)
    heads = []
    for hk in range(N_KV_HEADS):
        sl = slice(hk * HEAD_DIM, (hk + 1) * HEAD_DIM)
        qs = jnp.concatenate(
            [q3[:, :, (hk * GQA_GROUP + g) * HEAD_DIM:(hk * GQA_GROUP + g + 1) * HEAD_DIM]
             for g in range(GQA_GROUP)], axis=1).astype(BF16)
        sink = _sink_column(sinks_ref, hk, DEC_SEQ)[None]
        s = jnp.einsum("bqd,bkd->bqk", qs, k_all[:, :, sl], preferred_element_type=F32)
        s = jnp.where(mask, s, -jnp.inf)
        m = jnp.maximum(jnp.max(s, axis=-1, keepdims=True), sink)
        e = jnp.exp(s - m)
        den = jnp.sum(e, axis=-1, keepdims=True) + jnp.exp(sink - m)
        o = jnp.einsum("bqk,bkd->bqd", e.astype(BF16), v_all[:, :, sl], preferred_element_type=F32) / den
        heads += [o[:, g * DEC_SEQ:(g + 1) * DEC_SEQ, :] for g in range(GQA_GROUP)]
    o_att = jnp.concatenate(heads, axis=-1).reshape(SB * DEC_SEQ, ATT_WIDTH)

    uext_ref[:, 1:16, :] = st_ref[...]
    uext_ref[:, 16:16 + DEC_SEQ, :] = u3
    parts = []
    for g, w in enumerate(POOL_WINDOWS):
        sl = slice(g * POOL_GROUP_DIM, (g + 1) * POOL_GROUP_DIM)
        acc = u3[:, :, sl]
        for back in range(1, w):
            acc = acc + uext_ref[:, 16 - back:16 - back + DEC_SEQ, sl]
        parts.append(acc / float(w) - u3[:, :, sl])
    d = jnp.concatenate(parts, axis=-1).reshape(SB * DEC_SEQ, POOL_WIDTH)
    po_ref[...] = uext_ref[:, 16 + DEC_SEQ - POOL_STATE:16 + DEC_SEQ, :]
    h1_ref[...] = _mixer_tail(o_att, d, h_ref[...], wpool_ref, pscale_ref, gatt_ref, gpool_ref, wout_ref)


def _full_spec(a, n_grid):
    nd = a.ndim
    return pl.BlockSpec(a.shape, lambda *_: (0,) * nd)


def _mixer_prompt(sinks, x2d, q, k, v, u, wts):
    nb = SEQ // BQ
    row = lambda w: pl.BlockSpec((BQ, w), lambda b, j: (b * nb + j, 0))
    prev = lambda w: pl.BlockSpec(
        (WINDOW, w), lambda b, j: (jnp.maximum((b * nb + j) * (BQ // WINDOW) - 1, 0), 0))
    uprev = pl.BlockSpec((POOL_HALO, POOL_WIDTH),
                         lambda b, j: (jnp.maximum((b * nb + j) * (BQ // POOL_HALO) - 1, 0), 0))
    smem = pl.BlockSpec(memory_space=pltpu.SMEM)
    return pl.pallas_call(
        _mixer_prompt_kernel,
        grid=(BATCH, nb),
        in_specs=[smem, row(D_MODEL), row(ATT_WIDTH), row(KV_WIDTH), prev(KV_WIDTH), row(KV_WIDTH),
                  prev(KV_WIDTH), row(POOL_WIDTH), uprev] + [_full_spec(w, 2) for w in wts],
        out_specs=row(D_MODEL),
        out_shape=jax.ShapeDtypeStruct((T_ALL, D_MODEL), F32),
        scratch_shapes=[pltpu.VMEM((POOL_HEAD + POOL_HALO + BQ, POOL_WIDTH), F32)],
        compiler_params=_cparams(("parallel", "parallel")),
        name="mixer_prompt",
    )(sinks, x2d, q, k, k, v, v, u, u, *wts)


def _mixer_sample(sinks, x2d, q, k, v, u, cache_k, cache_v, state, wts, h1_buf):
    rows = SB * DEC_SEQ
    row = lambda w: pl.BlockSpec((rows, w), lambda i: (i, 0))
    bat = lambda a: pl.BlockSpec((SB,) + a.shape[1:], lambda i: (i, 0, 0))
    smem = pl.BlockSpec(memory_space=pltpu.SMEM)
    h1_blocks_before = T_P // rows
    n_in = 9 + len(wts)
    q, k, v, u = (a.reshape(DEC_BATCH, DEC_SEQ, a.shape[-1]) for a in (q, k, v, u))
    return pl.pallas_call(
        _mixer_sample_kernel,
        grid=(DEC_BATCH // SB,),
        in_specs=[smem, row(D_MODEL), bat(q), bat(k), bat(v), bat(u),
                  bat(cache_k), bat(cache_v), bat(state)] + [_full_spec(w, 1) for w in wts]
                 + [pl.BlockSpec(memory_space=pl.ANY)],
        out_specs=[pl.BlockSpec((rows, D_MODEL), lambda i: (h1_blocks_before + i, 0)),
                   bat(cache_k), bat(cache_v), bat(state)],
        out_shape=[jax.ShapeDtypeStruct((T_ALL, D_MODEL), F32),
                   jax.ShapeDtypeStruct(cache_k.shape, F32),
                   jax.ShapeDtypeStruct(cache_v.shape, F32),
                   jax.ShapeDtypeStruct(state.shape, F32)],
        scratch_shapes=[pltpu.VMEM((SB, 16 + DEC_SEQ, POOL_WIDTH), F32)],
        input_output_aliases={n_in: 0},
        compiler_params=_cparams(("parallel",)),
        name="mixer_sample",
    )(sinks, x2d, q, k, v, u, cache_k, cache_v, state, *wts, h1_buf)


def _first_max(vals, iota, n):
    m = jnp.max(vals, axis=0, keepdims=True)
    idx = jnp.min(jnp.where(vals == m, iota, n), axis=0, keepdims=True)
    return m, idx


def _router_kernel(h1_ref, gffn_ref, wrt_ref, bias_ref, xp_ref, idx_ref, wts_ref):
    xn = _rms(h1_ref[...], gffn_ref[...])
    logits = lax.dot_general(wrt_ref[...], xn, (((1,), (1,)), ((), ())),
                             precision=lax.Precision.HIGHEST, preferred_element_type=F32)
    scores = _sigmoid(logits)
    biased = scores + bias_ref[...]
    n_tok = biased.shape[1]
    neg = -jnp.inf

    iota_g = lax.broadcasted_iota(I32, (GROUP_SIZE, n_tok), 0)
    grp_rows = []
    for g in range(N_EXPERT_GROUPS):
        blk = biased[g * GROUP_SIZE:(g + 1) * GROUP_SIZE, :]
        top1, i1 = _first_max(blk, iota_g, GROUP_SIZE)
        top2 = jnp.max(jnp.where(iota_g == i1, neg, blk), axis=0, keepdims=True)
        grp_rows.append(top1 + top2)
    gs = jnp.concatenate(grp_rows, axis=0)

    iota_n = lax.broadcasted_iota(I32, (N_EXPERT_GROUPS, n_tok), 0)
    gsel = jnp.zeros((N_EXPERT_GROUPS, n_tok), jnp.bool_)
    for _ in range(TOPK_GROUPS):
        _, gi = _first_max(gs, iota_n, N_EXPERT_GROUPS)
        hit = iota_n == gi
        gsel = gsel | hit
        gs = jnp.where(hit, neg, gs)
    emask = jnp.concatenate(
        [jnp.broadcast_to(gsel[g:g + 1, :], (GROUP_SIZE, n_tok)) for g in range(N_EXPERT_GROUPS)], axis=0)
    masked = jnp.where(emask, biased, neg)

    iota_e = lax.broadcasted_iota(I32, (N_EXPERTS, n_tok), 0)
    idx_rows, sel_rows = [], []
    for _ in range(TOP_K):
        _, ei = _first_max(masked, iota_e, N_EXPERTS)
        hit = iota_e == ei
        idx_rows.append(ei)
        sel_rows.append(jnp.sum(jnp.where(hit, scores, 0.0), axis=0, keepdims=True))
        masked = jnp.where(hit, neg, masked)
    sel = jnp.concatenate(sel_rows, axis=0)
    idx_ref[...] = jnp.concatenate(idx_rows, axis=0)
    wts_ref[...] = sel / jnp.sum(sel, axis=0, keepdims=True) * ROUTED_SCALE
    _store_chunks(xp_ref, _pack_bf16_pairs(xn))


def _router(h1, group, g_ffn, w_router_t, bias_col):
    blk0 = group * T_GRP // BM_R
    colblk = pl.BlockSpec((TOP_K, BM_R), lambda i: (0, i))
    ws = [g_ffn, w_router_t, bias_col]
    return pl.pallas_call(
        _router_kernel,
        grid=(T_GRP // BM_R,),
        in_specs=[pl.BlockSpec((BM_R, D_MODEL), lambda i: (blk0 + i, 0))] + [_full_spec(w, 1) for w in ws],
        out_specs=[pl.BlockSpec((BM_R * ROW_CHUNKS, LANES), lambda i: (i, 0)), colblk, colblk],
        out_shape=[jax.ShapeDtypeStruct((T_GRP * ROW_CHUNKS, LANES), U32),
                   jax.ShapeDtypeStruct((TOP_K, T_GRP), I32),
                   jax.ShapeDtypeStruct((TOP_K, T_GRP), F32)],
        compiler_params=_cparams(("parallel",)),
        name="router",
    )(h1, *ws)


def _rank_kernel(idx_ref, tri_ref, rank_ref, cnt_ref, carry_ref):
    @pl.when(pl.program_id(0) == 0)
    def _():
        carry_ref[...] = jnp.zeros_like(carry_ref)

    idx = idx_ref[...]
    n_tok = idx.shape[1]
    iota_e = lax.broadcasted_iota(I32, (N_EXPERTS, n_tok), 0)
    member = jnp.zeros((N_EXPERTS, n_tok), F32)
    for k in range(TOP_K):
        member = member + jnp.where(iota_e == idx[k:k + 1, :], 1.0, 0.0)
    before = jnp.dot(member.astype(BF16), tri_ref[...], preferred_element_type=F32) + carry_ref[...]
    rows = [jnp.sum(jnp.where(iota_e == idx[k:k + 1, :], before, 0.0), axis=0, keepdims=True)
            for k in range(TOP_K)]
    rank_ref[...] = jnp.concatenate(rows, axis=0).astype(I32)
    carry_ref[...] = carry_ref[...] + jnp.sum(member, axis=1, keepdims=True)
    cnt_ref[...] = carry_ref[...].astype(I32)


def _rank(idx_t, tri):
    blk = pl.BlockSpec((TOP_K, BT_RANK), lambda i: (0, i))
    return pl.pallas_call(
        _rank_kernel,
        grid=(T_GRP // BT_RANK,),
        in_specs=[blk, _full_spec(tri, 1)],
        out_specs=[blk, pl.BlockSpec((N_EXPERTS, 1), lambda i: (0, 0))],
        out_shape=[jax.ShapeDtypeStruct((TOP_K, T_GRP), I32),
                   jax.ShapeDtypeStruct((N_EXPERTS, 1), I32)],
        scratch_shapes=[pltpu.VMEM((N_EXPERTS, 1), F32)],
        compiler_params=_cparams(("arbitrary",)),
        name="rank",
    )(idx_t, tri)


def _dest_kernel(idx_ref, rank_ref, cnt_ref, dest_ref, blk_e_ref, n_used_ref):
    counts = cnt_ref[...]
    padded = (counts + (BM_E - 1)) // BM_E * BM_E
    r = lax.broadcasted_iota(I32, (N_EXPERTS, N_EXPERTS), 0)
    c = lax.broadcasted_iota(I32, (N_EXPERTS, N_EXPERTS), 1)
    padded_row = jnp.sum(jnp.where(r == c, padded, 0), axis=0, keepdims=True)
    pad_start = jnp.sum(jnp.where(c < r, padded_row, 0), axis=1, keepdims=True)

    idx = idx_ref[...]
    n_tok = idx.shape[1]
    iota_e = lax.broadcasted_iota(I32, (N_EXPERTS, n_tok), 0)
    rows = [jnp.sum(jnp.where(iota_e == idx[k:k + 1, :], pad_start, 0), axis=0, keepdims=True)
            for k in range(TOP_K)]
    dest_ref[...] = jnp.concatenate(rows, axis=0) + rank_ref[...]

    pad_end_row = jnp.sum(jnp.where(r <= c, padded, 0), axis=0, keepdims=True)
    b0 = lax.broadcasted_iota(I32, (N_BLOCKS_PAD, N_EXPERTS), 0) * BM_E
    be = jnp.sum(jnp.where(pad_end_row <= b0, 1, 0), axis=1, keepdims=True)
    blk_e_ref[...] = jnp.minimum(be, N_EXPERTS - 1)
    n_used_ref[...] = pad_end_row[:, N_EXPERTS - 1:N_EXPERTS] // BM_E


N_BLOCKS_PAD = (N_BLOCKS + 7) // 8 * 8


def _dest(idx_t, rank_t, counts):
    blk = pl.BlockSpec((TOP_K, BT_RANK), lambda i: (0, i))
    one = lambda s: pl.BlockSpec(s, lambda i: (0, 0))
    return pl.pallas_call(
        _dest_kernel,
        grid=(T_GRP // BT_RANK,),
        in_specs=[blk, blk, one((N_EXPERTS, 1))],
        out_specs=[blk, one((N_BLOCKS_PAD, 1)), one((1, 1))],
        out_shape=[jax.ShapeDtypeStruct((TOP_K, T_GRP), I32),
                   jax.ShapeDtypeStruct((N_BLOCKS_PAD, 1), I32),
                   jax.ShapeDtypeStruct((1, 1), I32)],
        compiler_params=_cparams(("arbitrary",)),
        name="dest",
    )(idx_t, rank_t, counts)


def _sc_mesh():
    return plsc.VectorSubcoreMesh(core_axis_name="c", subcore_axis_name="s")


def _sc_worker_id():
    return lax.axis_index("s") * SC_CORES + lax.axis_index("c")


def _dispatch_body(dest_hbm, xp_hbm, xs_hbm, idx_v, rows_v, sem_in, sem_out):
    n_chunks, _, n_tok = dest_hbm.shape
    per_worker = n_chunks // SC_WORKERS
    chunk0 = _sc_worker_id() * per_worker

    def loads(i):
        chunk = chunk0 + i
        t0 = pl.multiple_of(chunk * n_tok, n_tok)
        return (pltpu.make_async_copy(dest_hbm.at[chunk], idx_v.at[i % 2], sem_in.at[i % 2]),
                pltpu.make_async_copy(xp_hbm.at[pl.ds(t0, n_tok)], rows_v.at[i % 2], sem_in.at[i % 2]))

    def scatters(i):
        return [pltpu.make_async_copy(rows_v.at[i % 2], xs_hbm.at[idx_v.at[i % 2, k]], sem_out.at[i % 2])
                for k in range(TOP_K)]

    for cp in loads(0):
        cp.start()
    for i in range(per_worker):
        for cp in loads(i):
            cp.wait()
        if i >= 1:
            for cp in scatters(i - 1):
                cp.wait()
        if i + 1 < per_worker:
            for cp in loads(i + 1):
                cp.start()
        for cp in scatters(i):
            cp.start()
    for cp in scatters(per_worker - 1):
        cp.wait()


def _dispatch(dest_chunks, xp3):
    return pl.kernel(
        _dispatch_body,
        out_type=jax.ShapeDtypeStruct((N_SLOTS, ROW_CHUNKS, LANES), U32),
        mesh=_sc_mesh(),
        scratch_types=[pltpu.VMEM((2, TOP_K, SC_CHUNK), I32),
                       pltpu.VMEM((2, SC_CHUNK, ROW_CHUNKS, LANES), U32),
                       pltpu.SemaphoreType.DMA((2,)), pltpu.SemaphoreType.DMA((2,))],
        name="dispatch",
    )(dest_chunks, xp3)


def _gather_body(dest_hbm, ys_hbm, yt_hbm, idx_v, rows_v, sem_in, sem_out):
    n_chunks, _, n_tok = dest_hbm.shape
    per_worker = n_chunks // SC_WORKERS
    chunk0 = _sc_worker_id() * per_worker

    @pl.loop(0, per_worker)
    def _(i):
        chunk = chunk0 + i
        t0 = pl.multiple_of(chunk * n_tok, n_tok)
        pltpu.sync_copy(dest_hbm.at[chunk], idx_v)

        def gather(k):
            return pltpu.make_async_copy(ys_hbm.at[idx_v.at[k]], rows_v.at[k % SC_RING], sem_in.at[k % SC_RING])

        def store(k):
            return pltpu.make_async_copy(rows_v.at[k % SC_RING], yt_hbm.at[k, pl.ds(t0, n_tok)],
                                         sem_out.at[k % SC_RING])

        for k in range(SC_RING):
            gather(k).start()
        for k in range(TOP_K):
            gather(k).wait()
            store(k).start()
            if k + SC_RING < TOP_K:
                store(k).wait()
                gather(k + SC_RING).start()
        for k in range(TOP_K - SC_RING, TOP_K):
            store(k).wait()


def _gather(dest_chunks, ys3):
    return pl.kernel(
        _gather_body,
        out_type=jax.ShapeDtypeStruct((TOP_K, T_GRP, ROW_CHUNKS, LANES), U32),
        mesh=_sc_mesh(),
        scratch_types=[pltpu.VMEM((TOP_K, SC_CHUNK), I32),
                       pltpu.VMEM((SC_RING, SC_CHUNK, ROW_CHUNKS, LANES), U32),
                       pltpu.SemaphoreType.DMA((SC_RING,)), pltpu.SemaphoreType.DMA((SC_RING,))],
        name="gather",
    )(dest_chunks, ys3)


def _experts_kernel(blk_e_ref, n_used_ref, xs_ref, wg_ref, wu_ref, wd_ref, ys_ref, wgu_s, wd_s):
    b = pl.program_id(0)

    @pl.when(b < n_used_ref[0])
    def _():
        prev = blk_e_ref[jnp.maximum(b - 1, 0)]

        @pl.when((b == 0) | (blk_e_ref[b] != prev))
        def _():
            wgu_s[:, :D_EXPERT] = wg_ref[0].astype(BF16)
            wgu_s[:, D_EXPERT:] = wu_ref[0].astype(BF16)
            wd_s[...] = wd_ref[0].astype(BF16)

        chunks = _load_chunks(xs_ref, BM_E)
        x_lo = jnp.concatenate([_unpack_lo(p) for p in chunks], axis=-1).astype(BF16)
        x_hi = jnp.concatenate([_unpack_hi(p) for p in chunks], axis=-1).astype(BF16)
        gu = (jnp.dot(x_lo, wgu_s[:HALF, :], preferred_element_type=F32)
              + jnp.dot(x_hi, wgu_s[HALF:, :], preferred_element_type=F32))
        gate, up = gu[:, :D_EXPERT], gu[:, D_EXPERT:]
        hmid = (gate * _sigmoid(gate) * up).astype(BF16)
        _store_chunks(ys_ref, _pack_bf16_pairs(jnp.dot(hmid, wd_s[...], preferred_element_type=F32)))


def _experts(blk_e, n_used, xs, wg, wu, wd):
    def blk(b, be, nu):
        return jnp.minimum(b, nu[0] - 1)

    grid_spec = pltpu.PrefetchScalarGridSpec(
        num_scalar_prefetch=2,
        grid=(N_BLOCKS,),
        in_specs=[pl.BlockSpec((BM_E * ROW_CHUNKS, LANES), lambda b, be, nu: (blk(b, be, nu), 0)),
                  pl.BlockSpec((1, D_MODEL, D_EXPERT), lambda b, be, nu: (be[blk(b, be, nu)], 0, 0)),
                  pl.BlockSpec((1, D_MODEL, D_EXPERT), lambda b, be, nu: (be[blk(b, be, nu)], 0, 0)),
                  pl.BlockSpec((1, D_EXPERT, D_MODEL), lambda b, be, nu: (be[blk(b, be, nu)], 0, 0))],
        out_specs=pl.BlockSpec((BM_E * ROW_CHUNKS, LANES), lambda b, be, nu: (blk(b, be, nu), 0)),
        scratch_shapes=[pltpu.VMEM((D_MODEL, 2 * D_EXPERT), BF16), pltpu.VMEM((D_EXPERT, D_MODEL), BF16)],
    )
    return pl.pallas_call(
        _experts_kernel,
        grid_spec=grid_spec,
        out_shape=jax.ShapeDtypeStruct((N_SLOTS * ROW_CHUNKS, LANES), U32),
        compiler_params=_cparams(("arbitrary",)),
        name="experts",
    )(blk_e, n_used, xs, wg, wu, wd)


def _combine_kernel(yt_ref, wts_ref, h1_ref, p_ref, gffn_ref, wsgu_ref, wsd_ref, gple_ref, wpg_ref, wpp_ref,
                    gfin_ref, *y_refs):
    y_ref = y_refs[-1]
    h1 = h1_ref[...]
    gu = jnp.dot(_rms(h1, gffn_ref[...]).astype(BF16), wsgu_ref[...], preferred_element_type=F32)
    sgate, sup = gu[:, :D_SHARED], gu[:, D_SHARED:]
    hsh = h1 + jnp.dot((sgate * _sigmoid(sgate) * sup).astype(BF16), wsd_ref[...], preferred_element_type=F32)
    wts = jnp.transpose(wts_ref[...])
    lo = [jnp.zeros((BT_COMB, LANES), F32) for _ in range(ROW_CHUNKS)]
    hi = [jnp.zeros((BT_COMB, LANES), F32) for _ in range(ROW_CHUNKS)]
    for k in range(TOP_K):
        w = wts[:, k:k + 1]
        for c, p in enumerate(_load_chunks(yt_ref, BT_COMB, lead=(k,))):
            lo[c] = lo[c] + w * _unpack_lo(p)
            hi[c] = hi[c] + w * _unpack_hi(p)
    h2 = hsh + jnp.concatenate(lo + hi, axis=-1)
    gate = _sigmoid(jnp.dot(_rms(h2, gple_ref[...]).astype(BF16), wpg_ref[...], preferred_element_type=F32))
    proj = jnp.dot(p_ref[...].astype(BF16), wpp_ref[...], preferred_element_type=F32)
    y_ref[...] = _rms(h2 + proj * gate, gfin_ref[...])


def _combine(yt, yt_row0, wts_t, h1, tok_row0, n_rows, p2d, p_row0, ws, y_prev, out_rows, out_row0):
    assert all(r % BT_COMB == 0 for r in (yt_row0, tok_row0, n_rows, p_row0, out_row0))
    g0, t0, p0, o0 = yt_row0 // BT_COMB, tok_row0 // BT_COMB, p_row0 // BT_COMB, out_row0 // BT_COMB
    in_specs = [pl.BlockSpec((TOP_K, BT_COMB * ROW_CHUNKS, LANES), lambda i: (0, g0 + i, 0)),
                pl.BlockSpec((TOP_K, BT_COMB), lambda i: (0, g0 + i)),
                pl.BlockSpec((BT_COMB, D_MODEL), lambda i: (t0 + i, 0)),
                pl.BlockSpec((BT_COMB, PLE_DIM), lambda i: (p0 + i, 0))] + [_full_spec(w, 1) for w in ws]
    args = [yt, wts_t, h1, p2d, *ws]
    aliases = {}
    if y_prev is not None:
        in_specs.append(pl.BlockSpec(memory_space=pl.ANY))
        aliases = {len(args): 0}
        args.append(y_prev)
    return pl.pallas_call(
        _combine_kernel,
        grid=(n_rows // BT_COMB,),
        in_specs=in_specs,
        out_specs=pl.BlockSpec((BT_COMB, D_MODEL), lambda i: (o0 + i, 0)),
        out_shape=jax.ShapeDtypeStruct((out_rows, D_MODEL), F32),
        input_output_aliases=aliases,
        compiler_params=_cparams(("parallel",)),
        name="combine",
    )(*args)


def kernel(x_prompt, x_sample, cache_k, cache_v, state_pool, p_prompt, p_sample, g_mix, w_in, attn_sinks,
           w_pool, pool_scale, g_att_out, g_pool_out, w_out, g_ffn, w_router, router_bias, w_exp_gate,
           w_exp_up, w_exp_down, w_sh_gate, w_sh_up, w_sh_down, g_ple, w_ple_gate, w_ple_proj, g_final):
    row = lambda a: a.reshape(1, -1)
    xp2d = x_prompt.reshape(T_P, D_MODEL)
    xs2d = x_sample.reshape(T_S, D_MODEL)
    w_in_bf = w_in[0].astype(BF16)
    mixer_wts = [w_pool[0].astype(BF16), row(pool_scale[0]), row(g_att_out[0]), row(g_pool_out[0]),
                 w_out[0].astype(BF16)]

    tab_p = _rope_tables(np.arange(SEQ))
    tab_s = _rope_tables(PAST_LEN + np.arange(DEC_SEQ), reps=BM_IN // DEC_SEQ)

    q_p, k_p, v_p, u_p = _inproj(xp2d, row(g_mix[0]), w_in_bf, tab_p, BF16)
    q_s, k_s, v_s, u_s = _inproj(xs2d, row(g_mix[0]), w_in_bf, tab_s, F32)

    h1 = _mixer_prompt(attn_sinks[0], xp2d, q_p, k_p, v_p, u_p, mixer_wts)
    h1, k_sample, v_sample, pool_sample = _mixer_sample(
        attn_sinks[0], xs2d, q_s, k_s, v_s, u_s,
        cache_k[0].reshape(DEC_BATCH, WINDOW, KV_WIDTH), cache_v[0].reshape(DEC_BATCH, WINDOW, KV_WIDTH),
        state_pool[0], mixer_wts, h1)

    g_ffn_row = row(g_ffn[0])
    router_wts = (g_ffn_row, w_router[0].T, router_bias[0].reshape(N_EXPERTS, 1))
    tri = (lax.broadcasted_iota(I32, (BT_RANK, BT_RANK), 0)
           < lax.broadcasted_iota(I32, (BT_RANK, BT_RANK), 1)).astype(BF16)

    groups = []
    for g in range(N_GROUPS):
        xp, idx_t, wts_t = _router(h1, g, *router_wts)
        rank_t, counts = _rank(idx_t, tri)
        dest_t, blk_e, n_used = _dest(idx_t, rank_t, counts)
        dest_chunks = dest_t.reshape(TOP_K, T_GRP // SC_CHUNK, SC_CHUNK).transpose(1, 0, 2)
        xs = _dispatch(dest_chunks, xp.reshape(T_GRP, ROW_CHUNKS, LANES))
        groups.append((wts_t, dest_chunks, xs, blk_e, n_used))

    ple_wts = [g_ffn_row, jnp.concatenate([w_sh_gate[0], w_sh_up[0]], axis=1).astype(BF16),
               w_sh_down[0].astype(BF16),
               row(g_ple[0]), w_ple_gate[0].astype(BF16), w_ple_proj[0].astype(BF16), row(g_final)]
    pp2d = p_prompt[0].reshape(T_P, PLE_DIM)
    ps2d = p_sample[0].reshape(T_S, PLE_DIM)
    y_p = y_s = None
    for g, (wts_t, dest_chunks, xs, blk_e, n_used) in enumerate(groups):
        lo, hi = g * T_GRP, (g + 1) * T_GRP
        ys = _experts(blk_e.reshape(N_BLOCKS_PAD), n_used.reshape(1), xs.reshape(N_SLOTS * ROW_CHUNKS, LANES),
                      w_exp_gate[0], w_exp_up[0], w_exp_down[0])
        yt = _gather(dest_chunks, ys.reshape(N_SLOTS, ROW_CHUNKS, LANES))
        yt = yt.reshape(TOP_K, T_GRP * ROW_CHUNKS, LANES)
        if lo < T_P:
            n = min(hi, T_P) - lo
            y_p = _combine(yt, 0, wts_t, h1, lo, n, pp2d, lo, ple_wts, y_p, T_P, lo)
        if hi > T_P:
            s0 = max(lo, T_P)
            y_s = _combine(yt, s0 - lo, wts_t, h1, s0, hi - s0, ps2d, s0 - T_P, ple_wts, y_s, T_S, s0 - T_P)

    kv5 = lambda a, b: a.reshape(1, b, WINDOW, N_KV_HEADS, HEAD_DIM)
    k_prompt = kv5(k_p.reshape(BATCH, SEQ, KV_WIDTH)[:, SEQ - WINDOW:], BATCH)
    v_prompt = kv5(v_p.reshape(BATCH, SEQ, KV_WIDTH)[:, SEQ - WINDOW:], BATCH)
    pool_prompt = u_p.reshape(BATCH, SEQ, POOL_WIDTH)[:, SEQ - POOL_STATE:][None]
    return (y_p.reshape(BATCH, SEQ, D_MODEL), y_s.reshape(DEC_BATCH, DEC_SEQ, D_MODEL),
            k_prompt, v_prompt, pool_prompt,
            kv5(k_sample, DEC_BATCH), kv5(v_sample, DEC_BATCH), pool_sample[None])
```

```python
import functools

import numpy as np
import jax
import jax.numpy as jnp
from jax import lax
from jax.experimental import pallas as pl
from jax.experimental.pallas import tpu as pltpu
from jax.experimental.pallas import tpu_sc as plsc

F32 = jnp.float32
BF16 = jnp.bfloat16
U32 = jnp.uint32
I32 = jnp.int32

D_MODEL = 1024
BATCH = 8
SEQ = 2048
DEC_BATCH = 128
DEC_SEQ = 8
PAST_LEN = 16384
N_Q_HEADS = 8
N_KV_HEADS = 2
HEAD_DIM = 64
GQA_GROUP = N_Q_HEADS // N_KV_HEADS
ATT_WIDTH = N_Q_HEADS * HEAD_DIM
KV_WIDTH = N_KV_HEADS * HEAD_DIM
WINDOW = 128
ROPE_THETA = 500000.0
ROT_DIM = HEAD_DIM // 4
POOL_WINDOWS = (2, 4, 8, 16)
POOL_GROUPS = 4
POOL_WIDTH = D_MODEL - ATT_WIDTH
POOL_GROUP_DIM = POOL_WIDTH // POOL_GROUPS
POOL_STATE = 15
IN_WIDTH = ATT_WIDTH + 2 * KV_WIDTH + POOL_WIDTH
N_EXPERTS = 64
TOP_K = 8
N_EXPERT_GROUPS = 8
GROUP_SIZE = N_EXPERTS // N_EXPERT_GROUPS
TOPK_GROUPS = 4
D_EXPERT = 256
D_SHARED = 256
ROUTED_SCALE = 2.5
PLE_DIM = 256
EPS = 1e-6

T_P = BATCH * SEQ
T_S = DEC_BATCH * DEC_SEQ
T_ALL = T_P + T_S
HALF = D_MODEL // 2
LANES = 128
VMEM_LIMIT = 48 * 1024 * 1024

BM_IN = 1024
BQ = 2 * WINDOW
SB = 16
BM_R = 512
BT_RANK = 512
BT_COMB = 512
N_GROUPS = 2
T_GRP = T_ALL // N_GROUPS
assert T_GRP * N_GROUPS == T_ALL
BM_E = 1280
E_STRIP = 256
assert BM_E % E_STRIP == 0
N_ASSIGN = T_GRP * TOP_K
N_BLOCKS = -(-N_ASSIGN // BM_E) + N_EXPERTS
N_SLOTS = N_BLOCKS * BM_E

ROW_CHUNKS = HALF // LANES
SC_CORES = 2
SC_SUBCORES = 16
SC_WORKERS = SC_CORES * SC_SUBCORES
SC_CHUNK = 16
SC_RING = 4
assert T_GRP % (SC_WORKERS * SC_CHUNK) == 0


def _load_chunks(ref, n_rows, lead=()):
    return [ref[lead + (pl.ds(c, n_rows, stride=ROW_CHUNKS), slice(None))] for c in range(ROW_CHUNKS)]


def _store_chunks(ref, packed):
    n_rows = packed.shape[0]
    for c in range(ROW_CHUNKS):
        ref[pl.ds(c, n_rows, stride=ROW_CHUNKS), :] = packed[:, c * LANES:(c + 1) * LANES]


def _cparams(sem):
    return pltpu.CompilerParams(dimension_semantics=sem, vmem_limit_bytes=VMEM_LIMIT)


def _rms(x, g):
    return x * lax.rsqrt(jnp.mean(x * x, axis=-1, keepdims=True) + EPS) * g


def _sigmoid(x):
    return 1.0 / (1.0 + jnp.exp(-x))


def _pack_bf16_pairs(x):
    h = x.shape[-1] // 2
    return pltpu.pack_elementwise([x[:, :h], x[:, h:]], packed_dtype=BF16)


def _unpack_lo(p):
    return pltpu.bitcast(p << 16, F32)


def _unpack_hi(p):
    return pltpu.bitcast(p & jnp.uint32(0xFFFF0000), F32)


def _inproj_kernel(x_ref, g_ref, w_ref, c_ref, s1_ref, s2_ref, q_ref, k_ref, v_ref, u_ref):
    xn = _rms(x_ref[...], g_ref[...]).astype(BF16)
    z = jnp.dot(xn, w_ref[...], preferred_element_type=F32)
    c, s1, s2 = c_ref[...], s1_ref[...], s2_ref[...]

    def rope(t):
        return t * c + pltpu.roll(t, LANES - ROT_DIM // 2, 1) * s1 + pltpu.roll(t, ROT_DIM // 2, 1) * s2

    for i in range(ATT_WIDTH // LANES):
        sl = slice(i * LANES, (i + 1) * LANES)
        q_ref[:, sl] = (rope(z[:, sl]) * (HEAD_DIM ** -0.5)).astype(q_ref.dtype)
    k_ref[...] = rope(z[:, ATT_WIDTH:ATT_WIDTH + KV_WIDTH])
    v_ref[...] = z[:, ATT_WIDTH + KV_WIDTH:ATT_WIDTH + 2 * KV_WIDTH]
    u_ref[...] = z[:, ATT_WIDTH + 2 * KV_WIDTH:]


def _rope_tables(pos, reps=1):
    f32 = np.float32
    half = ROT_DIM // 2
    inv = np.power(f32(ROPE_THETA), -np.arange(half, dtype=f32) * f32(2.0) / f32(ROT_DIM)).astype(f32)
    ang = np.asarray(pos, f32)[:, None] * inv[None, :]
    cos, sin = np.cos(ang).astype(f32), np.sin(ang).astype(f32)
    n = len(pos)
    ones = np.ones((n, HEAD_DIM - ROT_DIM), f32)
    zeros = np.zeros((n, HEAD_DIM - ROT_DIM), f32)
    zh = np.zeros((n, half), f32)
    c = np.concatenate([cos, cos, ones], axis=1)
    s1 = np.concatenate([-sin, zh, zeros], axis=1)
    s2 = np.concatenate([zh, sin, zeros], axis=1)
    tile = lambda a: np.tile(a, (reps, LANES // HEAD_DIM))
    return tile(c), tile(s1), tile(s2)


def _inproj(x2d, g_mix, w_in_bf, tables, q_dtype):
    rows = x2d.shape[0]
    n_tab = tables[0].shape[0] // BM_IN
    row_spec = lambda w: pl.BlockSpec((BM_IN, w), lambda i: (i, 0))
    tab_spec = pl.BlockSpec((BM_IN, LANES), lambda i: (i % n_tab, 0))
    full = lambda a: pl.BlockSpec(a.shape, lambda i: (0,) * a.ndim)
    return pl.pallas_call(
        _inproj_kernel,
        grid=(rows // BM_IN,),
        in_specs=[row_spec(D_MODEL), full(g_mix), full(w_in_bf), tab_spec, tab_spec, tab_spec],
        out_specs=[row_spec(ATT_WIDTH), row_spec(KV_WIDTH), row_spec(KV_WIDTH), row_spec(POOL_WIDTH)],
        out_shape=[jax.ShapeDtypeStruct((rows, ATT_WIDTH), q_dtype),
                   jax.ShapeDtypeStruct((rows, KV_WIDTH), F32),
                   jax.ShapeDtypeStruct((rows, KV_WIDTH), F32),
                   jax.ShapeDtypeStruct((rows, POOL_WIDTH), F32)],
        compiler_params=_cparams(("parallel",)),
        name="inproj",
    )(x2d, g_mix, w_in_bf, *tables)


def _sink_column(sinks_ref, kv_head, rows_per_head):
    n = GQA_GROUP * rows_per_head
    grp = lax.broadcasted_iota(I32, (n, 1), 0) // rows_per_head
    col = jnp.full((n, 1), sinks_ref[kv_head * GQA_GROUP], F32)
    for g in range(1, GQA_GROUP):
        col = jnp.where(grp == g, sinks_ref[kv_head * GQA_GROUP + g], col)
    return col


def _band_mask(n_rows, rows_per_head, n_keys):
    i = lax.broadcasted_iota(I32, (n_rows, n_keys), 0) % rows_per_head
    c = lax.broadcasted_iota(I32, (n_rows, n_keys), 1)
    return (c >= i) & (c <= i + WINDOW), c


def _stack_heads(q, kv_head):
    return jnp.concatenate(
        [q[:, (kv_head * GQA_GROUP + g) * HEAD_DIM:(kv_head * GQA_GROUP + g + 1) * HEAD_DIM]
         for g in range(GQA_GROUP)], axis=0)


def _nt_dot(a, b):
    return lax.dot_general(a, b, (((1,), (1,)), ((), ())), preferred_element_type=F32)


POOL_HALO = 16
POOL_HEAD = 8
assert all(w == 2 << g for g, w in enumerate(POOL_WINDOWS)) and POOL_WINDOWS[-1] - 1 <= POOL_HALO


def _window_sums(ext_ref, n):
    lo, hi = POOL_HEAD, POOL_HEAD + POOL_HALO + n
    for p in range(POOL_GROUPS):
        lanes = slice(p * POOL_GROUP_DIM, POOL_WIDTH)
        ext_ref[lo:hi, lanes] = ext_ref[lo:hi, lanes] + ext_ref[lo - (1 << p):hi - (1 << p), lanes]


def _mixer_tail(o_att, d, h, wpool_ref, pscale_ref, gatt_ref, gpool_ref, wout_ref):
    parts = [jnp.dot(d[:, g * POOL_GROUP_DIM:(g + 1) * POOL_GROUP_DIM].astype(BF16), wpool_ref[g],
                     preferred_element_type=F32) for g in range(POOL_GROUPS)]
    o_pool = jnp.concatenate(parts, axis=-1) * pscale_ref[...]
    mixed = jnp.concatenate([_rms(o_att, gatt_ref[...]), _rms(o_pool, gpool_ref[...])], axis=-1)
    return h + jnp.dot(mixed.astype(BF16), wout_ref[...], preferred_element_type=F32)


def _mixer_prompt_kernel(sinks_ref, h_ref, q_ref, kc_ref, kp_ref, vc_ref, vp_ref, uc_ref, up_ref,
                         wpool_ref, pscale_ref, gatt_ref, gpool_ref, wout_ref, h1_ref, uext_ref):
    j = pl.program_id(1)
    q = q_ref[...]
    k_all = jnp.concatenate([kp_ref[...], kc_ref[...]], axis=0).astype(BF16)
    v_all = jnp.concatenate([vp_ref[...], vc_ref[...]], axis=0).astype(BF16)
    ones = jnp.ones((WINDOW + BQ, HEAD_DIM), BF16)
    v_ones = [jnp.concatenate([v_all[:, hk * HEAD_DIM:(hk + 1) * HEAD_DIM], ones], axis=1)
              for hk in range(N_KV_HEADS)]
    band, col = _band_mask(GQA_GROUP * WINDOW, WINDOW, 2 * WINDOW)
    sinks = [_sink_column(sinks_ref, hk, WINDOW) for hk in range(N_KV_HEADS)]
    bands = []
    for b in range(BQ // WINDOW):
        rows = slice(b * WINDOW, (b + 1) * WINDOW)
        keys = slice(b * WINDOW, (b + 2) * WINDOW)
        mask = band & ((col >= WINDOW) | (j > 0)) if b == 0 else band
        heads = []
        for hk in range(N_KV_HEADS):
            sl = slice(hk * HEAD_DIM, (hk + 1) * HEAD_DIM)
            s = jnp.where(mask, _nt_dot(_stack_heads(q[rows], hk), k_all[keys, sl]), -jnp.inf)
            m = jnp.maximum(jnp.max(s, axis=-1, keepdims=True), sinks[hk])
            e = jnp.exp(s - m).astype(BF16)
            ov = jnp.dot(e, v_ones[hk][keys], preferred_element_type=F32)
            den = ov[:, HEAD_DIM:HEAD_DIM + 1] + jnp.exp(sinks[hk] - m)
            o = ov[:, :HEAD_DIM] / den
            heads += [o[g * WINDOW:(g + 1) * WINDOW] for g in range(GQA_GROUP)]
        bands.append(jnp.concatenate(heads, axis=-1))
    o_att = jnp.concatenate(bands, axis=0)

    u = uc_ref[...]
    base = POOL_HEAD + POOL_HALO
    uext_ref[0:POOL_HEAD, :] = jnp.zeros((POOL_HEAD, POOL_WIDTH), F32)
    uext_ref[POOL_HEAD:base, :] = jnp.where(j > 0, up_ref[...], 0.0)
    uext_ref[base:base + BQ, :] = u
    _window_sums(uext_ref, BQ)
    pos = j * BQ + lax.broadcasted_iota(I32, (BQ, 1), 0)
    parts = []
    for g, w in enumerate(POOL_WINDOWS):
        sl = slice(g * POOL_GROUP_DIM, (g + 1) * POOL_GROUP_DIM)
        parts.append(uext_ref[base:base + BQ, sl] / jnp.minimum(pos + 1, w).astype(F32) - u[:, sl])
    d = jnp.concatenate(parts, axis=-1)
    h1_ref[...] = _mixer_tail(o_att, d, h_ref[...], wpool_ref, pscale_ref, gatt_ref, gpool_ref, wout_ref)


def _mixer_sample_kernel(sinks_ref, h_ref, q_ref, kn_ref, vn_ref, u_ref, ck_ref, cv_ref, st_ref,
                         wpool_ref, pscale_ref, gatt_ref, gpool_ref, wout_ref, h1_in_ref,
                         h1_ref, ko_ref, vo_ref, po_ref, uext_ref):
    del h1_in_ref
    n_q = GQA_GROUP * DEC_SEQ
    n_keys = 2 * WINDOW
    band, col = _band_mask(n_q, DEC_SEQ, n_keys)
    mask = (band & (col < WINDOW + DEC_SEQ))[None]
    q3, kn3, vn3, u3 = q_ref[...], kn_ref[...], vn_ref[...], u_ref[...]
    ck, cv = ck_ref[...], cv_ref[...]
    ko_ref[:, 0:WINDOW - DEC_SEQ, :] = ck[:, DEC_SEQ:, :]
    ko_ref[:, WINDOW - DEC_SEQ:WINDOW, :] = kn3
    vo_ref[:, 0:WINDOW - DEC_SEQ, :] = cv[:, DEC_SEQ:, :]
    vo_ref[:, WINDOW - DEC_SEQ:WINDOW, :] = vn3
    pad = jnp.zeros((SB, WINDOW - DEC_SEQ, KV_WIDTH), F32)
    k_all = jnp.concatenate([ck, kn3, pad], axis=1).astype(BF16)
    v_all = jnp.concatenate([cv, vn3, pad], axis=1).astype(BF16)
    heads = []
    for hk in range(N_KV_HEADS):
        sl = slice(hk * HEAD_DIM, (hk + 1) * HEAD_DIM)
        qs = jnp.concatenate(
            [q3[:, :, (hk * GQA_GROUP + g) * HEAD_DIM:(hk * GQA_GROUP + g + 1) * HEAD_DIM]
             for g in range(GQA_GROUP)], axis=1).astype(BF16)
        sink = _sink_column(sinks_ref, hk, DEC_SEQ)[None]
        s = jnp.einsum("bqd,bkd->bqk", qs, k_all[:, :, sl], preferred_element_type=F32)
        s = jnp.where(mask, s, -jnp.inf)
        m = jnp.maximum(jnp.max(s, axis=-1, keepdims=True), sink)
        e = jnp.exp(s - m)
        den = jnp.sum(e, axis=-1, keepdims=True) + jnp.exp(sink - m)
        o = jnp.einsum("bqk,bkd->bqd", e.astype(BF16), v_all[:, :, sl], preferred_element_type=F32) / den
        heads += [o[:, g * DEC_SEQ:(g + 1) * DEC_SEQ, :] for g in range(GQA_GROUP)]
    o_att = jnp.concatenate(heads, axis=-1).reshape(SB * DEC_SEQ, ATT_WIDTH)

    uext_ref[:, 1:16, :] = st_ref[...]
    uext_ref[:, 16:16 + DEC_SEQ, :] = u3
    parts = []
    for g, w in enumerate(POOL_WINDOWS):
        sl = slice(g * POOL_GROUP_DIM, (g + 1) * POOL_GROUP_DIM)
        acc = u3[:, :, sl]
        for back in range(1, w):
            acc = acc + uext_ref[:, 16 - back:16 - back + DEC_SEQ, sl]
        parts.append(acc / float(w) - u3[:, :, sl])
    d = jnp.concatenate(parts, axis=-1).reshape(SB * DEC_SEQ, POOL_WIDTH)
    po_ref[...] = uext_ref[:, 16 + DEC_SEQ - POOL_STATE:16 + DEC_SEQ, :]
    h1_ref[...] = _mixer_tail(o_att, d, h_ref[...], wpool_ref, pscale_ref, gatt_ref, gpool_ref, wout_ref)


def _full_spec(a, n_grid):
    nd = a.ndim
    return pl.BlockSpec(a.shape, lambda *_: (0,) * nd)


def _mixer_prompt(sinks, x2d, q, k, v, u, wts):
    nb = SEQ // BQ
    row = lambda w: pl.BlockSpec((BQ, w), lambda b, j: (b * nb + j, 0))
    prev = lambda w: pl.BlockSpec(
        (WINDOW, w), lambda b, j: (jnp.maximum((b * nb + j) * (BQ // WINDOW) - 1, 0), 0))
    uprev = pl.BlockSpec((POOL_HALO, POOL_WIDTH),
                         lambda b, j: (jnp.maximum((b * nb + j) * (BQ // POOL_HALO) - 1, 0), 0))
    smem = pl.BlockSpec(memory_space=pltpu.SMEM)
    return pl.pallas_call(
        _mixer_prompt_kernel,
        grid=(BATCH, nb),
        in_specs=[smem, row(D_MODEL), row(ATT_WIDTH), row(KV_WIDTH), prev(KV_WIDTH), row(KV_WIDTH),
                  prev(KV_WIDTH), row(POOL_WIDTH), uprev] + [_full_spec(w, 2) for w in wts],
        out_specs=row(D_MODEL),
        out_shape=jax.ShapeDtypeStruct((T_ALL, D_MODEL), F32),
        scratch_shapes=[pltpu.VMEM((POOL_HEAD + POOL_HALO + BQ, POOL_WIDTH), F32)],
        compiler_params=_cparams(("parallel", "parallel")),
        name="mixer_prompt",
    )(sinks, x2d, q, k, k, v, v, u, u, *wts)


def _mixer_sample(sinks, x2d, q, k, v, u, cache_k, cache_v, state, wts, h1_buf):
    rows = SB * DEC_SEQ
    row = lambda w: pl.BlockSpec((rows, w), lambda i: (i, 0))
    bat = lambda a: pl.BlockSpec((SB,) + a.shape[1:], lambda i: (i, 0, 0))
    smem = pl.BlockSpec(memory_space=pltpu.SMEM)
    h1_blocks_before = T_P // rows
    n_in = 9 + len(wts)
    q, k, v, u = (a.reshape(DEC_BATCH, DEC_SEQ, a.shape[-1]) for a in (q, k, v, u))
    return pl.pallas_call(
        _mixer_sample_kernel,
        grid=(DEC_BATCH // SB,),
        in_specs=[smem, row(D_MODEL), bat(q), bat(k), bat(v), bat(u),
                  bat(cache_k), bat(cache_v), bat(state)] + [_full_spec(w, 1) for w in wts]
                 + [pl.BlockSpec(memory_space=pl.ANY)],
        out_specs=[pl.BlockSpec((rows, D_MODEL), lambda i: (h1_blocks_before + i, 0)),
                   bat(cache_k), bat(cache_v), bat(state)],
        out_shape=[jax.ShapeDtypeStruct((T_ALL, D_MODEL), F32),
                   jax.ShapeDtypeStruct(cache_k.shape, F32),
                   jax.ShapeDtypeStruct(cache_v.shape, F32),
                   jax.ShapeDtypeStruct(state.shape, F32)],
        scratch_shapes=[pltpu.VMEM((SB, 16 + DEC_SEQ, POOL_WIDTH), F32)],
        input_output_aliases={n_in: 0},
        compiler_params=_cparams(("parallel",)),
        name="mixer_sample",
    )(sinks, x2d, q, k, v, u, cache_k, cache_v, state, *wts, h1_buf)


def _first_max(vals, iota, n):
    m = jnp.max(vals, axis=0, keepdims=True)
    idx = jnp.min(jnp.where(vals == m, iota, n), axis=0, keepdims=True)
    return m, idx


def _router_kernel(h1_ref, gffn_ref, wrt_ref, bias_ref, xp_ref, idx_ref, wts_ref):
    xn = _rms(h1_ref[...], gffn_ref[...])
    logits = lax.dot_general(wrt_ref[...], xn, (((1,), (1,)), ((), ())),
                             precision=lax.Precision.HIGHEST, preferred_element_type=F32)
    scores = _sigmoid(logits)
    biased = scores + bias_ref[...]
    n_tok = biased.shape[1]
    neg = -jnp.inf

    iota_g = lax.broadcasted_iota(I32, (GROUP_SIZE, n_tok), 0)
    grp_rows = []
    for g in range(N_EXPERT_GROUPS):
        blk = biased[g * GROUP_SIZE:(g + 1) * GROUP_SIZE, :]
        top1, i1 = _first_max(blk, iota_g, GROUP_SIZE)
        top2 = jnp.max(jnp.where(iota_g == i1, neg, blk), axis=0, keepdims=True)
        grp_rows.append(top1 + top2)
    gs = jnp.concatenate(grp_rows, axis=0)

    iota_n = lax.broadcasted_iota(I32, (N_EXPERT_GROUPS, n_tok), 0)
    gsel = jnp.zeros((N_EXPERT_GROUPS, n_tok), jnp.bool_)
    for _ in range(TOPK_GROUPS):
        _, gi = _first_max(gs, iota_n, N_EXPERT_GROUPS)
        hit = iota_n == gi
        gsel = gsel | hit
        gs = jnp.where(hit, neg, gs)
    emask = jnp.concatenate(
        [jnp.broadcast_to(gsel[g:g + 1, :], (GROUP_SIZE, n_tok)) for g in range(N_EXPERT_GROUPS)], axis=0)
    masked = jnp.where(emask, biased, neg)

    iota_e = lax.broadcasted_iota(I32, (N_EXPERTS, n_tok), 0)
    idx_rows, sel_rows = [], []
    for _ in range(TOP_K):
        _, ei = _first_max(masked, iota_e, N_EXPERTS)
        hit = iota_e == ei
        idx_rows.append(ei)
        sel_rows.append(jnp.sum(jnp.where(hit, scores, 0.0), axis=0, keepdims=True))
        masked = jnp.where(hit, neg, masked)
    sel = jnp.concatenate(sel_rows, axis=0)
    idx_ref[...] = jnp.concatenate(idx_rows, axis=0)
    wts_ref[...] = sel / jnp.sum(sel, axis=0, keepdims=True) * ROUTED_SCALE
    _store_chunks(xp_ref, _pack_bf16_pairs(xn))


def _router(h1, group, g_ffn, w_router_t, bias_col):
    blk0 = group * T_GRP // BM_R
    colblk = pl.BlockSpec((TOP_K, BM_R), lambda i: (0, i))
    ws = [g_ffn, w_router_t, bias_col]
    return pl.pallas_call(
        _router_kernel,
        grid=(T_GRP // BM_R,),
        in_specs=[pl.BlockSpec((BM_R, D_MODEL), lambda i: (blk0 + i, 0))] + [_full_spec(w, 1) for w in ws],
        out_specs=[pl.BlockSpec((BM_R * ROW_CHUNKS, LANES), lambda i: (i, 0)), colblk, colblk],
        out_shape=[jax.ShapeDtypeStruct((T_GRP * ROW_CHUNKS, LANES), U32),
                   jax.ShapeDtypeStruct((TOP_K, T_GRP), I32),
                   jax.ShapeDtypeStruct((TOP_K, T_GRP), F32)],
        compiler_params=_cparams(("parallel",)),
        name="router",
    )(h1, *ws)


def _rank_kernel(idx_ref, tri_ref, rank_ref, cnt_ref, carry_ref):
    @pl.when(pl.program_id(0) == 0)
    def _():
        carry_ref[...] = jnp.zeros_like(carry_ref)

    idx = idx_ref[...]
    n_tok = idx.shape[1]
    iota_e = lax.broadcasted_iota(I32, (N_EXPERTS, n_tok), 0)
    member = jnp.zeros((N_EXPERTS, n_tok), F32)
    for k in range(TOP_K):
        member = member + jnp.where(iota_e == idx[k:k + 1, :], 1.0, 0.0)
    before = jnp.dot(member.astype(BF16), tri_ref[...], preferred_element_type=F32) + carry_ref[...]
    rows = [jnp.sum(jnp.where(iota_e == idx[k:k + 1, :], before, 0.0), axis=0, keepdims=True)
            for k in range(TOP_K)]
    rank_ref[...] = jnp.concatenate(rows, axis=0).astype(I32)
    carry_ref[...] = carry_ref[...] + jnp.sum(member, axis=1, keepdims=True)
    cnt_ref[...] = carry_ref[...].astype(I32)


def _rank(idx_t, tri):
    blk = pl.BlockSpec((TOP_K, BT_RANK), lambda i: (0, i))
    return pl.pallas_call(
        _rank_kernel,
        grid=(T_GRP // BT_RANK,),
        in_specs=[blk, _full_spec(tri, 1)],
        out_specs=[blk, pl.BlockSpec((N_EXPERTS, 1), lambda i: (0, 0))],
        out_shape=[jax.ShapeDtypeStruct((TOP_K, T_GRP), I32),
                   jax.ShapeDtypeStruct((N_EXPERTS, 1), I32)],
        scratch_shapes=[pltpu.VMEM((N_EXPERTS, 1), F32)],
        compiler_params=_cparams(("arbitrary",)),
        name="rank",
    )(idx_t, tri)


def _dest_kernel(idx_ref, rank_ref, cnt_ref, dest_ref, blk_e_ref, n_used_ref, blk_rows_ref):
    counts = cnt_ref[...]
    padded = (counts + (BM_E - 1)) // BM_E * BM_E
    r = lax.broadcasted_iota(I32, (N_EXPERTS, N_EXPERTS), 0)
    c = lax.broadcasted_iota(I32, (N_EXPERTS, N_EXPERTS), 1)
    padded_row = jnp.sum(jnp.where(r == c, padded, 0), axis=0, keepdims=True)
    pad_start = jnp.sum(jnp.where(c < r, padded_row, 0), axis=1, keepdims=True)

    idx = idx_ref[...]
    n_tok = idx.shape[1]
    iota_e = lax.broadcasted_iota(I32, (N_EXPERTS, n_tok), 0)
    rows = [jnp.sum(jnp.where(iota_e == idx[k:k + 1, :], pad_start, 0), axis=0, keepdims=True)
            for k in range(TOP_K)]
    dest_ref[...] = jnp.concatenate(rows, axis=0) + rank_ref[...]

    pad_end_row = jnp.sum(jnp.where(r <= c, padded, 0), axis=0, keepdims=True)
    b0 = lax.broadcasted_iota(I32, (N_BLOCKS_PAD, N_EXPERTS), 0) * BM_E
    be = jnp.minimum(jnp.sum(jnp.where(pad_end_row <= b0, 1, 0), axis=1, keepdims=True), N_EXPERTS - 1)
    blk_e_ref[...] = be
    n_used_ref[...] = pad_end_row[:, N_EXPERTS - 1:N_EXPERTS] // BM_E
    counts_row = jnp.sum(jnp.where(r == c, counts, 0), axis=0, keepdims=True)
    mine = lax.broadcasted_iota(I32, (N_BLOCKS_PAD, N_EXPERTS), 1) == be
    end_valid = jnp.sum(jnp.where(mine, pad_end_row - padded_row + counts_row, 0), axis=1, keepdims=True)
    blk_rows_ref[...] = jnp.clip(end_valid - b0[:, :1], 0, BM_E)


N_BLOCKS_PAD = (N_BLOCKS + 7) // 8 * 8


def _dest(idx_t, rank_t, counts):
    blk = pl.BlockSpec((TOP_K, BT_RANK), lambda i: (0, i))
    one = lambda s: pl.BlockSpec(s, lambda i: (0, 0))
    return pl.pallas_call(
        _dest_kernel,
        grid=(T_GRP // BT_RANK,),
        in_specs=[blk, blk, one((N_EXPERTS, 1))],
        out_specs=[blk, one((N_BLOCKS_PAD, 1)), one((1, 1)), one((N_BLOCKS_PAD, 1))],
        out_shape=[jax.ShapeDtypeStruct((TOP_K, T_GRP), I32),
                   jax.ShapeDtypeStruct((N_BLOCKS_PAD, 1), I32),
                   jax.ShapeDtypeStruct((1, 1), I32),
                   jax.ShapeDtypeStruct((N_BLOCKS_PAD, 1), I32)],
        compiler_params=_cparams(("arbitrary",)),
        name="dest",
    )(idx_t, rank_t, counts)


def _sc_mesh():
    return plsc.VectorSubcoreMesh(core_axis_name="c", subcore_axis_name="s")


def _sc_worker_id():
    return lax.axis_index("s") * SC_CORES + lax.axis_index("c")


def _dispatch_body(dest_hbm, xp_hbm, xs_hbm, idx_v, rows_v, sem_in, sem_out):
    n_chunks, _, n_tok = dest_hbm.shape
    per_worker = n_chunks // SC_WORKERS
    chunk0 = _sc_worker_id() * per_worker

    def loads(i):
        chunk = chunk0 + i
        t0 = pl.multiple_of(chunk * n_tok, n_tok)
        return (pltpu.make_async_copy(dest_hbm.at[chunk], idx_v.at[i % 2], sem_in.at[i % 2]),
                pltpu.make_async_copy(xp_hbm.at[pl.ds(t0, n_tok)], rows_v.at[i % 2], sem_in.at[i % 2]))

    def scatters(i):
        return [pltpu.make_async_copy(rows_v.at[i % 2], xs_hbm.at[idx_v.at[i % 2, k]], sem_out.at[i % 2])
                for k in range(TOP_K)]

    for cp in loads(0):
        cp.start()
    for i in range(per_worker):
        for cp in loads(i):
            cp.wait()
        if i >= 1:
            for cp in scatters(i - 1):
                cp.wait()
        if i + 1 < per_worker:
            for cp in loads(i + 1):
                cp.start()
        for cp in scatters(i):
            cp.start()
    for cp in scatters(per_worker - 1):
        cp.wait()


def _dispatch(dest_chunks, xp3):
    return pl.kernel(
        _dispatch_body,
        out_type=jax.ShapeDtypeStruct((N_SLOTS, ROW_CHUNKS, LANES), U32),
        mesh=_sc_mesh(),
        scratch_types=[pltpu.VMEM((2, TOP_K, SC_CHUNK), I32),
                       pltpu.VMEM((2, SC_CHUNK, ROW_CHUNKS, LANES), U32),
                       pltpu.SemaphoreType.DMA((2,)), pltpu.SemaphoreType.DMA((2,))],
        name="dispatch",
    )(dest_chunks, xp3)


def _gather_body(dest_hbm, ys_hbm, yt_hbm, idx_v, rows_v, sem_in, sem_out):
    n_chunks, _, n_tok = dest_hbm.shape
    per_worker = n_chunks // SC_WORKERS
    chunk0 = _sc_worker_id() * per_worker

    @pl.loop(0, per_worker)
    def _(i):
        chunk = chunk0 + i
        t0 = pl.multiple_of(chunk * n_tok, n_tok)
        pltpu.sync_copy(dest_hbm.at[chunk], idx_v)

        def gather(k):
            return pltpu.make_async_copy(ys_hbm.at[idx_v.at[k]], rows_v.at[k % SC_RING], sem_in.at[k % SC_RING])

        def store(k):
            return pltpu.make_async_copy(rows_v.at[k % SC_RING], yt_hbm.at[k, pl.ds(t0, n_tok)],
                                         sem_out.at[k % SC_RING])

        for k in range(SC_RING):
            gather(k).start()
        for k in range(TOP_K):
            gather(k).wait()
            store(k).start()
            if k + SC_RING < TOP_K:
                store(k).wait()
                gather(k + SC_RING).start()
        for k in range(TOP_K - SC_RING, TOP_K):
            store(k).wait()


def _gather(dest_chunks, ys3):
    return pl.kernel(
        _gather_body,
        out_type=jax.ShapeDtypeStruct((TOP_K, T_GRP, ROW_CHUNKS, LANES), U32),
        mesh=_sc_mesh(),
        scratch_types=[pltpu.VMEM((TOP_K, SC_CHUNK), I32),
                       pltpu.VMEM((SC_RING, SC_CHUNK, ROW_CHUNKS, LANES), U32),
                       pltpu.SemaphoreType.DMA((SC_RING,)), pltpu.SemaphoreType.DMA((SC_RING,))],
        name="gather",
    )(dest_chunks, ys3)


def _experts_kernel(blk_e_ref, n_used_ref, blk_rows_ref, xs_ref, wg_ref, wu_ref, wd_ref, ys_ref, wgu_s, wd_s):
    b = pl.program_id(0)

    def swiglu_rows(n_rows):
        chunks = _load_chunks(xs_ref, n_rows)
        x_lo = jnp.concatenate([_unpack_lo(p) for p in chunks], axis=-1).astype(BF16)
        x_hi = jnp.concatenate([_unpack_hi(p) for p in chunks], axis=-1).astype(BF16)
        gu = (jnp.dot(x_lo, wgu_s[:HALF, :], preferred_element_type=F32)
              + jnp.dot(x_hi, wgu_s[HALF:, :], preferred_element_type=F32))
        gate, up = gu[:, :D_EXPERT], gu[:, D_EXPERT:]
        hmid = (gate * _sigmoid(gate) * up).astype(BF16)
        _store_chunks(ys_ref, _pack_bf16_pairs(jnp.dot(hmid, wd_s[...], preferred_element_type=F32)))

    @pl.when(b < n_used_ref[0])
    def _():
        prev = blk_e_ref[jnp.maximum(b - 1, 0)]

        @pl.when((b == 0) | (blk_e_ref[b] != prev))
        def _():
            wgu_s[:, :D_EXPERT] = wg_ref[0].astype(BF16)
            wgu_s[:, D_EXPERT:] = wu_ref[0].astype(BF16)
            wd_s[...] = wd_ref[0].astype(BF16)

        valid = blk_rows_ref[b]
        for n_rows in range(E_STRIP, BM_E + 1, E_STRIP):
            @pl.when((valid > n_rows - E_STRIP) & (valid <= n_rows))
            def _(n_rows=n_rows):
                swiglu_rows(n_rows)


def _experts(blk_e, n_used, blk_rows, xs, wg, wu, wd):
    def blk(b, be, nu, nr):
        return jnp.minimum(b, nu[0] - 1)

    def by_expert(shape):
        return pl.BlockSpec((1,) + shape, lambda b, be, nu, nr: (be[blk(b, be, nu, nr)], 0, 0))

    tile = pl.BlockSpec((BM_E * ROW_CHUNKS, LANES), lambda b, be, nu, nr: (blk(b, be, nu, nr), 0))
    grid_spec = pltpu.PrefetchScalarGridSpec(
        num_scalar_prefetch=3,
        grid=(N_BLOCKS,),
        in_specs=[tile, by_expert((D_MODEL, D_EXPERT)), by_expert((D_MODEL, D_EXPERT)),
                  by_expert((D_EXPERT, D_MODEL))],
        out_specs=tile,
        scratch_shapes=[pltpu.VMEM((D_MODEL, 2 * D_EXPERT), BF16), pltpu.VMEM((D_EXPERT, D_MODEL), BF16)],
    )
    return pl.pallas_call(
        _experts_kernel,
        grid_spec=grid_spec,
        out_shape=jax.ShapeDtypeStruct((N_SLOTS * ROW_CHUNKS, LANES), U32),
        compiler_params=_cparams(("arbitrary",)),
        name="experts",
    )(blk_e, n_used, blk_rows, xs, wg, wu, wd)


def _combine_kernel(yt_ref, wts_ref, h1_ref, p_ref, gffn_ref, wsgu_ref, wsd_ref, gple_ref, wpg_ref, wpp_ref,
                    gfin_ref, *y_refs):
    y_ref = y_refs[-1]
    h1 = h1_ref[...]
    gu = jnp.dot(_rms(h1, gffn_ref[...]).astype(BF16), wsgu_ref[...], preferred_element_type=F32)
    sgate, sup = gu[:, :D_SHARED], gu[:, D_SHARED:]
    hsh = h1 + jnp.dot((sgate * _sigmoid(sgate) * sup).astype(BF16), wsd_ref[...], preferred_element_type=F32)
    wts = jnp.transpose(wts_ref[...])
    lo = [jnp.zeros((BT_COMB, LANES), F32) for _ in range(ROW_CHUNKS)]
    hi = [jnp.zeros((BT_COMB, LANES), F32) for _ in range(ROW_CHUNKS)]
    for k in range(TOP_K):
        w = wts[:, k:k + 1]
        for c, p in enumerate(_load_chunks(yt_ref, BT_COMB, lead=(k,))):
            lo[c] = lo[c] + w * _unpack_lo(p)
            hi[c] = hi[c] + w * _unpack_hi(p)
    h2 = hsh + jnp.concatenate(lo + hi, axis=-1)
    gate = _sigmoid(jnp.dot(_rms(h2, gple_ref[...]).astype(BF16), wpg_ref[...], preferred_element_type=F32))
    proj = jnp.dot(p_ref[...].astype(BF16), wpp_ref[...], preferred_element_type=F32)
    y_ref[...] = _rms(h2 + proj * gate, gfin_ref[...])


def _combine(yt, yt_row0, wts_t, h1, tok_row0, n_rows, p2d, p_row0, ws, y_prev, out_rows, out_row0):
    assert all(r % BT_COMB == 0 for r in (yt_row0, tok_row0, n_rows, p_row0, out_row0))
    g0, t0, p0, o0 = yt_row0 // BT_COMB, tok_row0 // BT_COMB, p_row0 // BT_COMB, out_row0 // BT_COMB
    in_specs = [pl.BlockSpec((TOP_K, BT_COMB * ROW_CHUNKS, LANES), lambda i: (0, g0 + i, 0)),
                pl.BlockSpec((TOP_K, BT_COMB), lambda i: (0, g0 + i)),
                pl.BlockSpec((BT_COMB, D_MODEL), lambda i: (t0 + i, 0)),
                pl.BlockSpec((BT_COMB, PLE_DIM), lambda i: (p0 + i, 0))] + [_full_spec(w, 1) for w in ws]
    args = [yt, wts_t, h1, p2d, *ws]
    aliases = {}
    if y_prev is not None:
        in_specs.append(pl.BlockSpec(memory_space=pl.ANY))
        aliases = {len(args): 0}
        args.append(y_prev)
    return pl.pallas_call(
        _combine_kernel,
        grid=(n_rows // BT_COMB,),
        in_specs=in_specs,
        out_specs=pl.BlockSpec((BT_COMB, D_MODEL), lambda i: (o0 + i, 0)),
        out_shape=jax.ShapeDtypeStruct((out_rows, D_MODEL), F32),
        input_output_aliases=aliases,
        compiler_params=_cparams(("parallel",)),
        name="combine",
    )(*args)


def kernel(x_prompt, x_sample, cache_k, cache_v, state_pool, p_prompt, p_sample, g_mix, w_in, attn_sinks,
           w_pool, pool_scale, g_att_out, g_pool_out, w_out, g_ffn, w_router, router_bias, w_exp_gate,
           w_exp_up, w_exp_down, w_sh_gate, w_sh_up, w_sh_down, g_ple, w_ple_gate, w_ple_proj, g_final):
    row = lambda a: a.reshape(1, -1)
    xp2d = x_prompt.reshape(T_P, D_MODEL)
    xs2d = x_sample.reshape(T_S, D_MODEL)
    w_in_bf = w_in[0].astype(BF16)
    mixer_wts = [w_pool[0].astype(BF16), row(pool_scale[0]), row(g_att_out[0]), row(g_pool_out[0]),
                 w_out[0].astype(BF16)]

    tab_p = _rope_tables(np.arange(SEQ))
    tab_s = _rope_tables(PAST_LEN + np.arange(DEC_SEQ), reps=BM_IN // DEC_SEQ)

    q_p, k_p, v_p, u_p = _inproj(xp2d, row(g_mix[0]), w_in_bf, tab_p, BF16)
    q_s, k_s, v_s, u_s = _inproj(xs2d, row(g_mix[0]), w_in_bf, tab_s, F32)

    h1 = _mixer_prompt(attn_sinks[0], xp2d, q_p, k_p, v_p, u_p, mixer_wts)
    h1, k_sample, v_sample, pool_sample = _mixer_sample(
        attn_sinks[0], xs2d, q_s, k_s, v_s, u_s,
        cache_k[0].reshape(DEC_BATCH, WINDOW, KV_WIDTH), cache_v[0].reshape(DEC_BATCH, WINDOW, KV_WIDTH),
        state_pool[0], mixer_wts, h1)

    g_ffn_row = row(g_ffn[0])
    router_wts = (g_ffn_row, w_router[0].T, router_bias[0].reshape(N_EXPERTS, 1))
    tri = (lax.broadcasted_iota(I32, (BT_RANK, BT_RANK), 0)
           < lax.broadcasted_iota(I32, (BT_RANK, BT_RANK), 1)).astype(BF16)

    groups = []
    for g in range(N_GROUPS):
        xp, idx_t, wts_t = _router(h1, g, *router_wts)
        rank_t, counts = _rank(idx_t, tri)
        dest_t, *plan = _dest(idx_t, rank_t, counts)
        dest_chunks = dest_t.reshape(TOP_K, T_GRP // SC_CHUNK, SC_CHUNK).transpose(1, 0, 2)
        xs = _dispatch(dest_chunks, xp.reshape(T_GRP, ROW_CHUNKS, LANES))
        groups.append((wts_t, dest_chunks, xs, [a.reshape(-1) for a in plan]))

    ple_wts = [g_ffn_row, jnp.concatenate([w_sh_gate[0], w_sh_up[0]], axis=1).astype(BF16),
               w_sh_down[0].astype(BF16),
               row(g_ple[0]), w_ple_gate[0].astype(BF16), w_ple_proj[0].astype(BF16), row(g_final)]
    pp2d = p_prompt[0].reshape(T_P, PLE_DIM)
    ps2d = p_sample[0].reshape(T_S, PLE_DIM)
    y_p = y_s = None
    for g, (wts_t, dest_chunks, xs, plan) in enumerate(groups):
        lo, hi = g * T_GRP, (g + 1) * T_GRP
        ys = _experts(*plan, xs.reshape(N_SLOTS * ROW_CHUNKS, LANES), w_exp_gate[0], w_exp_up[0], w_exp_down[0])
        yt = _gather(dest_chunks, ys.reshape(N_SLOTS, ROW_CHUNKS, LANES))
        yt = yt.reshape(TOP_K, T_GRP * ROW_CHUNKS, LANES)
        if lo < T_P:
            n = min(hi, T_P) - lo
            y_p = _combine(yt, 0, wts_t, h1, lo, n, pp2d, lo, ple_wts, y_p, T_P, lo)
        if hi > T_P:
            s0 = max(lo, T_P)
            y_s = _combine(yt, s0 - lo, wts_t, h1, s0, hi - s0, ps2d, s0 - T_P, ple_wts, y_s, T_S, s0 - T_P)

    kv5 = lambda a, b: a.reshape(1, b, WINDOW, N_KV_HEADS, HEAD_DIM)
    k_prompt = kv5(k_p.reshape(BATCH, SEQ, KV_WIDTH)[:, SEQ - WINDOW:], BATCH)
    v_prompt = kv5(v_p.reshape(BATCH, SEQ, KV_WIDTH)[:, SEQ - WINDOW:], BATCH)
    pool_prompt = u_p.reshape(BATCH, SEQ, POOL_WIDTH)[:, SEQ - POOL_STATE:][None]
    return (y_p.reshape(BATCH, SEQ, D_MODEL), y_s.reshape(DEC_BATCH, DEC_SEQ, D_MODEL),
            k_prompt, v_prompt, pool_prompt,
            kv5(k_sample, DEC_BATCH), kv5(v_sample, DEC_BATCH), pool_sample[None])
```

```python
import functools

import numpy as np
import jax
import jax.numpy as jnp
from jax import lax
from jax.experimental import pallas as pl
from jax.experimental.pallas import tpu as pltpu
from jax.experimental.pallas import tpu_sc as plsc

F32 = jnp.float32
BF16 = jnp.bfloat16
U32 = jnp.uint32
I32 = jnp.int32

D_MODEL = 1024
BATCH = 8
SEQ = 2048
DEC_BATCH = 128
DEC_SEQ = 8
PAST_LEN = 16384
N_Q_HEADS = 8
N_KV_HEADS = 2
HEAD_DIM = 64
GQA_GROUP = N_Q_HEADS // N_KV_HEADS
ATT_WIDTH = N_Q_HEADS * HEAD_DIM
KV_WIDTH = N_KV_HEADS * HEAD_DIM
WINDOW = 128
ROPE_THETA = 500000.0
ROT_DIM = HEAD_DIM // 4
POOL_WINDOWS = (2, 4, 8, 16)
POOL_GROUPS = 4
POOL_WIDTH = D_MODEL - ATT_WIDTH
POOL_GROUP_DIM = POOL_WIDTH // POOL_GROUPS
POOL_STATE = 15
IN_WIDTH = ATT_WIDTH + 2 * KV_WIDTH + POOL_WIDTH
N_EXPERTS = 64
TOP_K = 8
N_EXPERT_GROUPS = 8
GROUP_SIZE = N_EXPERTS // N_EXPERT_GROUPS
TOPK_GROUPS = 4
D_EXPERT = 256
D_SHARED = 256
ROUTED_SCALE = 2.5
PLE_DIM = 256
EPS = 1e-6

T_P = BATCH * SEQ
T_S = DEC_BATCH * DEC_SEQ
T_ALL = T_P + T_S
HALF = D_MODEL // 2
LANES = 128
VMEM_LIMIT = 48 * 1024 * 1024

BM_IN = 1024
BQ = 2 * WINDOW
SB = 16
BM_R = 512
BT_RANK = 512
BT_COMB = 256
N_GROUPS = 2
T_GRP = T_ALL // N_GROUPS
N_SUB = 2
T_SUB = T_GRP // N_SUB
assert T_GRP * N_GROUPS == T_ALL and T_SUB * N_SUB == T_GRP
BM_E = 1280
E_STRIP = 256
assert BM_E % E_STRIP == 0
N_ASSIGN = T_GRP * TOP_K
N_BLOCKS = -(-N_ASSIGN // BM_E) + N_EXPERTS
N_SLOTS = N_BLOCKS * BM_E

ROW_CHUNKS = HALF // LANES
SC_CORES = 2
SC_SUBCORES = 16
SC_WORKERS = SC_CORES * SC_SUBCORES
SC_CHUNK = 16
SC_CHUNK_G = 8
SC_RING = 4
assert T_GRP % (SC_WORKERS * SC_CHUNK) == 0 and T_SUB % (SC_WORKERS * SC_CHUNK_G) == 0


def _load_chunks(ref, n_rows, lead=()):
    return [ref[lead + (pl.ds(c, n_rows, stride=ROW_CHUNKS), slice(None))] for c in range(ROW_CHUNKS)]


def _store_chunks(ref, packed):
    n_rows = packed.shape[0]
    for c in range(ROW_CHUNKS):
        ref[pl.ds(c, n_rows, stride=ROW_CHUNKS), :] = packed[:, c * LANES:(c + 1) * LANES]


def _cparams(sem):
    return pltpu.CompilerParams(dimension_semantics=sem, vmem_limit_bytes=VMEM_LIMIT)


def _rms(x, g):
    return x * lax.rsqrt(jnp.mean(x * x, axis=-1, keepdims=True) + EPS) * g


def _sigmoid(x):
    return 1.0 / (1.0 + jnp.exp(-x))


def _pack_bf16_pairs(x):
    h = x.shape[-1] // 2
    return pltpu.pack_elementwise([x[:, :h], x[:, h:]], packed_dtype=BF16)


def _unpack_lo(p):
    return pltpu.bitcast(p << 16, F32)


def _unpack_hi(p):
    return pltpu.bitcast(p & jnp.uint32(0xFFFF0000), F32)


def _inproj_kernel(x_ref, g_ref, w_ref, c_ref, s1_ref, s2_ref, q_ref, k_ref, v_ref, u_ref):
    xn = _rms(x_ref[...], g_ref[...]).astype(BF16)
    z = jnp.dot(xn, w_ref[...], preferred_element_type=F32)
    c, s1, s2 = c_ref[...], s1_ref[...], s2_ref[...]

    def rope(t):
        return t * c + pltpu.roll(t, LANES - ROT_DIM // 2, 1) * s1 + pltpu.roll(t, ROT_DIM // 2, 1) * s2

    for i in range(ATT_WIDTH // LANES):
        sl = slice(i * LANES, (i + 1) * LANES)
        q_ref[:, sl] = (rope(z[:, sl]) * (HEAD_DIM ** -0.5)).astype(q_ref.dtype)
    k_ref[...] = rope(z[:, ATT_WIDTH:ATT_WIDTH + KV_WIDTH])
    v_ref[...] = z[:, ATT_WIDTH + KV_WIDTH:ATT_WIDTH + 2 * KV_WIDTH]
    u_ref[...] = z[:, ATT_WIDTH + 2 * KV_WIDTH:]


def _rope_tables(pos, reps=1):
    f32 = np.float32
    half = ROT_DIM // 2
    inv = np.power(f32(ROPE_THETA), -np.arange(half, dtype=f32) * f32(2.0) / f32(ROT_DIM)).astype(f32)
    ang = np.asarray(pos, f32)[:, None] * inv[None, :]
    cos, sin = np.cos(ang).astype(f32), np.sin(ang).astype(f32)
    n = len(pos)
    ones = np.ones((n, HEAD_DIM - ROT_DIM), f32)
    zeros = np.zeros((n, HEAD_DIM - ROT_DIM), f32)
    zh = np.zeros((n, half), f32)
    c = np.concatenate([cos, cos, ones], axis=1)
    s1 = np.concatenate([-sin, zh, zeros], axis=1)
    s2 = np.concatenate([zh, sin, zeros], axis=1)
    tile = lambda a: np.tile(a, (reps, LANES // HEAD_DIM))
    return tile(c), tile(s1), tile(s2)


def _inproj(x2d, g_mix, w_in_bf, tables, q_dtype):
    rows = x2d.shape[0]
    n_tab = tables[0].shape[0] // BM_IN
    row_spec = lambda w: pl.BlockSpec((BM_IN, w), lambda i: (i, 0))
    tab_spec = pl.BlockSpec((BM_IN, LANES), lambda i: (i % n_tab, 0))
    full = lambda a: pl.BlockSpec(a.shape, lambda i: (0,) * a.ndim)
    return pl.pallas_call(
        _inproj_kernel,
        grid=(rows // BM_IN,),
        in_specs=[row_spec(D_MODEL), full(g_mix), full(w_in_bf), tab_spec, tab_spec, tab_spec],
        out_specs=[row_spec(ATT_WIDTH), row_spec(KV_WIDTH), row_spec(KV_WIDTH), row_spec(POOL_WIDTH)],
        out_shape=[jax.ShapeDtypeStruct((rows, ATT_WIDTH), q_dtype),
                   jax.ShapeDtypeStruct((rows, KV_WIDTH), F32),
                   jax.ShapeDtypeStruct((rows, KV_WIDTH), F32),
                   jax.ShapeDtypeStruct((rows, POOL_WIDTH), F32)],
        compiler_params=_cparams(("parallel",)),
        name="inproj",
    )(x2d, g_mix, w_in_bf, *tables)


def _sink_column(sinks_ref, kv_head, rows_per_head):
    n = GQA_GROUP * rows_per_head
    grp = lax.broadcasted_iota(I32, (n, 1), 0) // rows_per_head
    col = jnp.full((n, 1), sinks_ref[kv_head * GQA_GROUP], F32)
    for g in range(1, GQA_GROUP):
        col = jnp.where(grp == g, sinks_ref[kv_head * GQA_GROUP + g], col)
    return col


def _band_mask(n_rows, rows_per_head, n_keys):
    i = lax.broadcasted_iota(I32, (n_rows, n_keys), 0) % rows_per_head
    c = lax.broadcasted_iota(I32, (n_rows, n_keys), 1)
    return (c >= i) & (c <= i + WINDOW), c


def _stack_heads(q, kv_head):
    return jnp.concatenate(
        [q[:, (kv_head * GQA_GROUP + g) * HEAD_DIM:(kv_head * GQA_GROUP + g + 1) * HEAD_DIM]
         for g in range(GQA_GROUP)], axis=0)


def _nt_dot(a, b):
    return lax.dot_general(a, b, (((1,), (1,)), ((), ())), preferred_element_type=F32)


POOL_HALO = 16
POOL_HEAD = 8
assert all(w == 2 << g for g, w in enumerate(POOL_WINDOWS)) and POOL_WINDOWS[-1] - 1 <= POOL_HALO


def _window_sums(ext_ref, n):
    lo, hi = POOL_HEAD, POOL_HEAD + POOL_HALO + n
    for p in range(POOL_GROUPS):
        lanes = slice(p * POOL_GROUP_DIM, POOL_WIDTH)
        ext_ref[lo:hi, lanes] = ext_ref[lo:hi, lanes] + ext_ref[lo - (1 << p):hi - (1 << p), lanes]


def _mixer_tail(o_att, d, h, wpool_ref, pscale_ref, gatt_ref, gpool_ref, wout_ref):
    parts = [jnp.dot(d[:, g * POOL_GROUP_DIM:(g + 1) * POOL_GROUP_DIM].astype(BF16), wpool_ref[g],
                     preferred_element_type=F32) for g in range(POOL_GROUPS)]
    o_pool = jnp.concatenate(parts, axis=-1) * pscale_ref[...]
    mixed = jnp.concatenate([_rms(o_att, gatt_ref[...]), _rms(o_pool, gpool_ref[...])], axis=-1)
    return h + jnp.dot(mixed.astype(BF16), wout_ref[...], preferred_element_type=F32)


def _mixer_prompt_kernel(sinks_ref, h_ref, q_ref, kc_ref, kp_ref, vc_ref, vp_ref, uc_ref, up_ref,
                         wpool_ref, pscale_ref, gatt_ref, gpool_ref, wout_ref, h1_ref, uext_ref):
    j = pl.program_id(1)
    q = q_ref[...]
    k_all = jnp.concatenate([kp_ref[...], kc_ref[...]], axis=0).astype(BF16)
    v_all = jnp.concatenate([vp_ref[...], vc_ref[...]], axis=0).astype(BF16)
    ones = jnp.ones((WINDOW + BQ, HEAD_DIM), BF16)
    v_ones = [jnp.concatenate([v_all[:, hk * HEAD_DIM:(hk + 1) * HEAD_DIM], ones], axis=1)
              for hk in range(N_KV_HEADS)]
    band, col = _band_mask(GQA_GROUP * WINDOW, WINDOW, 2 * WINDOW)
    sinks = [_sink_column(sinks_ref, hk, WINDOW) for hk in range(N_KV_HEADS)]
    bands = []
    for b in range(BQ // WINDOW):
        rows = slice(b * WINDOW, (b + 1) * WINDOW)
        keys = slice(b * WINDOW, (b + 2) * WINDOW)
        mask = band & ((col >= WINDOW) | (j > 0)) if b == 0 else band
        heads = []
        for hk in range(N_KV_HEADS):
            sl = slice(hk * HEAD_DIM, (hk + 1) * HEAD_DIM)
            s = jnp.where(mask, _nt_dot(_stack_heads(q[rows], hk), k_all[keys, sl]), -jnp.inf)
            m = jnp.maximum(jnp.max(s, axis=-1, keepdims=True), sinks[hk])
            e = jnp.exp(s - m).astype(BF16)
            ov = jnp.dot(e, v_ones[hk][keys], preferred_element_type=F32)
            den = ov[:, HEAD_DIM:HEAD_DIM + 1] + jnp.exp(sinks[hk] - m)
            o = ov[:, :HEAD_DIM] / den
            heads += [o[g * WINDOW:(g + 1) * WINDOW] for g in range(GQA_GROUP)]
        bands.append(jnp.concatenate(heads, axis=-1))
    o_att = jnp.concatenate(bands, axis=0)

    u = uc_ref[...]
    base = POOL_HEAD + POOL_HALO
    uext_ref[0:POOL_HEAD, :] = jnp.zeros((POOL_HEAD, POOL_WIDTH), F32)
    uext_ref[POOL_HEAD:base, :] = jnp.where(j > 0, up_ref[...], 0.0)
    uext_ref[base:base + BQ, :] = u
    _window_sums(uext_ref, BQ)
    pos = j * BQ + lax.broadcasted_iota(I32, (BQ, 1), 0)
    parts = []
    for g, w in enumerate(POOL_WINDOWS):
        sl = slice(g * POOL_GROUP_DIM, (g + 1) * POOL_GROUP_DIM)
        parts.append(uext_ref[base:base + BQ, sl] / jnp.minimum(pos + 1, w).astype(F32) - u[:, sl])
    d = jnp.concatenate(parts, axis=-1)
    h1_ref[...] = _mixer_tail(o_att, d, h_ref[...], wpool_ref, pscale_ref, gatt_ref, gpool_ref, wout_ref)


def _mixer_sample_kernel(sinks_ref, h_ref, q_ref, kn_ref, vn_ref, u_ref, ck_ref, cv_ref, st_ref,
                         wpool_ref, pscale_ref, gatt_ref, gpool_ref, wout_ref, h1_in_ref,
                         h1_ref, ko_ref, vo_ref, po_ref, uext_ref):
    del h1_in_ref
    n_q = GQA_GROUP * DEC_SEQ
    n_keys = 2 * WINDOW
    band, col = _band_mask(n_q, DEC_SEQ, n_keys)
    mask = (band & (col < WINDOW + DEC_SEQ))[None]
    q3, kn3, vn3, u3 = q_ref[...], kn_ref[...], vn_ref[...], u_ref[...]
    ck, cv = ck_ref[...], cv_ref[...]
    ko_ref[:, 0:WINDOW - DEC_SEQ, :] = ck[:, DEC_SEQ:, :]
    ko_ref[:, WINDOW - DEC_SEQ:WINDOW, :] = kn3
    vo_ref[:, 0:WINDOW - DEC_SEQ, :] = cv[:, DEC_SEQ:, :]
    vo_ref[:, WINDOW - DEC_SEQ:WINDOW, :] = vn3
    pad = jnp.zeros((SB, WINDOW - DEC_SEQ, KV_WIDTH), F32)
    k_all = jnp.concatenate([ck, kn3, pad], axis=1).astype(BF16)
    v_all = jnp.concatenate([cv, vn3, pad], axis=1).astype(BF16)
    heads = []
    for hk in range(N_KV_HEADS):
        sl = slice(hk * HEAD_DIM, (hk + 1) * HEAD_DIM)
        qs = jnp.concatenate(
            [q3[:, :, (hk * GQA_GROUP + g) * HEAD_DIM:(hk * GQA_GROUP + g + 1) * HEAD_DIM]
             for g in range(GQA_GROUP)], axis=1).astype(BF16)
        sink = _sink_column(sinks_ref, hk, DEC_SEQ)[None]
        s = jnp.einsum("bqd,bkd->bqk", qs, k_all[:, :, sl], preferred_element_type=F32)
        s = jnp.where(mask, s, -jnp.inf)
        m = jnp.maximum(jnp.max(s, axis=-1, keepdims=True), sink)
        e = jnp.exp(s - m)
        den = jnp.sum(e, axis=-1, keepdims=True) + jnp.exp(sink - m)
        o = jnp.einsum("bqk,bkd->bqd", e.astype(BF16), v_all[:, :, sl], preferred_element_type=F32) / den
        heads += [o[:, g * DEC_SEQ:(g + 1) * DEC_SEQ, :] for g in range(GQA_GROUP)]
    o_att = jnp.concatenate(heads, axis=-1).reshape(SB * DEC_SEQ, ATT_WIDTH)

    uext_ref[:, 1:16, :] = st_ref[...]
    uext_ref[:, 16:16 + DEC_SEQ, :] = u3
    parts = []
    for g, w in enumerate(POOL_WINDOWS):
        sl = slice(g * POOL_GROUP_DIM, (g + 1) * POOL_GROUP_DIM)
        acc = u3[:, :, sl]
        for back in range(1, w):
            acc = acc + uext_ref[:, 16 - back:16 - back + DEC_SEQ, sl]
        parts.append(acc / float(w) - u3[:, :, sl])
    d = jnp.concatenate(parts, axis=-1).reshape(SB * DEC_SEQ, POOL_WIDTH)
    po_ref[...] = uext_ref[:, 16 + DEC_SEQ - POOL_STATE:16 + DEC_SEQ, :]
    h1_ref[...] = _mixer_tail(o_att, d, h_ref[...], wpool_ref, pscale_ref, gatt_ref, gpool_ref, wout_ref)


def _full_spec(a, n_grid):
    nd = a.ndim
    return pl.BlockSpec(a.shape, lambda *_: (0,) * nd)


def _mixer_prompt(sinks, x2d, q, k, v, u, wts):
    nb = SEQ // BQ
    row = lambda w: pl.BlockSpec((BQ, w), lambda b, j: (b * nb + j, 0))
    prev = lambda w: pl.BlockSpec(
        (WINDOW, w), lambda b, j: (jnp.maximum((b * nb + j) * (BQ // WINDOW) - 1, 0), 0))
    uprev = pl.BlockSpec((POOL_HALO, POOL_WIDTH),
                         lambda b, j: (jnp.maximum((b * nb + j) * (BQ // POOL_HALO) - 1, 0), 0))
    smem = pl.BlockSpec(memory_space=pltpu.SMEM)
    return pl.pallas_call(
        _mixer_prompt_kernel,
        grid=(BATCH, nb),
        in_specs=[smem, row(D_MODEL), row(ATT_WIDTH), row(KV_WIDTH), prev(KV_WIDTH), row(KV_WIDTH),
                  prev(KV_WIDTH), row(POOL_WIDTH), uprev] + [_full_spec(w, 2) for w in wts],
        out_specs=row(D_MODEL),
        out_shape=jax.ShapeDtypeStruct((T_ALL, D_MODEL), F32),
        scratch_shapes=[pltpu.VMEM((POOL_HEAD + POOL_HALO + BQ, POOL_WIDTH), F32)],
        compiler_params=_cparams(("parallel", "parallel")),
        name="mixer_prompt",
    )(sinks, x2d, q, k, k, v, v, u, u, *wts)


def _mixer_sample(sinks, x2d, q, k, v, u, cache_k, cache_v, state, wts, h1_buf):
    rows = SB * DEC_SEQ
    row = lambda w: pl.BlockSpec((rows, w), lambda i: (i, 0))
    bat = lambda a: pl.BlockSpec((SB,) + a.shape[1:], lambda i: (i, 0, 0))
    smem = pl.BlockSpec(memory_space=pltpu.SMEM)
    h1_blocks_before = T_P // rows
    n_in = 9 + len(wts)
    q, k, v, u = (a.reshape(DEC_BATCH, DEC_SEQ, a.shape[-1]) for a in (q, k, v, u))
    return pl.pallas_call(
        _mixer_sample_kernel,
        grid=(DEC_BATCH // SB,),
        in_specs=[smem, row(D_MODEL), bat(q), bat(k), bat(v), bat(u),
                  bat(cache_k), bat(cache_v), bat(state)] + [_full_spec(w, 1) for w in wts]
                 + [pl.BlockSpec(memory_space=pl.ANY)],
        out_specs=[pl.BlockSpec((rows, D_MODEL), lambda i: (h1_blocks_before + i, 0)),
                   bat(cache_k), bat(cache_v), bat(state)],
        out_shape=[jax.ShapeDtypeStruct((T_ALL, D_MODEL), F32),
                   jax.ShapeDtypeStruct(cache_k.shape, F32),
                   jax.ShapeDtypeStruct(cache_v.shape, F32),
                   jax.ShapeDtypeStruct(state.shape, F32)],
        scratch_shapes=[pltpu.VMEM((SB, 16 + DEC_SEQ, POOL_WIDTH), F32)],
        input_output_aliases={n_in: 0},
        compiler_params=_cparams(("parallel",)),
        name="mixer_sample",
    )(sinks, x2d, q, k, v, u, cache_k, cache_v, state, *wts, h1_buf)


def _first_max(vals, iota, n):
    m = jnp.max(vals, axis=0, keepdims=True)
    idx = jnp.min(jnp.where(vals == m, iota, n), axis=0, keepdims=True)
    return m, idx


def _router_kernel(h1_ref, gffn_ref, wrt_ref, bias_ref, xp_ref, idx_ref, wts_ref):
    xn = _rms(h1_ref[...], gffn_ref[...])
    logits = lax.dot_general(wrt_ref[...], xn, (((1,), (1,)), ((), ())),
                             precision=lax.Precision.HIGHEST, preferred_element_type=F32)
    scores = _sigmoid(logits)
    biased = scores + bias_ref[...]
    n_tok = biased.shape[1]
    neg = -jnp.inf

    iota_g = lax.broadcasted_iota(I32, (GROUP_SIZE, n_tok), 0)
    grp_rows = []
    for g in range(N_EXPERT_GROUPS):
        blk = biased[g * GROUP_SIZE:(g + 1) * GROUP_SIZE, :]
        top1, i1 = _first_max(blk, iota_g, GROUP_SIZE)
        top2 = jnp.max(jnp.where(iota_g == i1, neg, blk), axis=0, keepdims=True)
        grp_rows.append(top1 + top2)
    gs = jnp.concatenate(grp_rows, axis=0)

    iota_n = lax.broadcasted_iota(I32, (N_EXPERT_GROUPS, n_tok), 0)
    gsel = jnp.zeros((N_EXPERT_GROUPS, n_tok), jnp.bool_)
    for _ in range(TOPK_GROUPS):
        _, gi = _first_max(gs, iota_n, N_EXPERT_GROUPS)
        hit = iota_n == gi
        gsel = gsel | hit
        gs = jnp.where(hit, neg, gs)
    emask = jnp.concatenate(
        [jnp.broadcast_to(gsel[g:g + 1, :], (GROUP_SIZE, n_tok)) for g in range(N_EXPERT_GROUPS)], axis=0)
    masked = jnp.where(emask, biased, neg)

    iota_e = lax.broadcasted_iota(I32, (N_EXPERTS, n_tok), 0)
    idx_rows, sel_rows = [], []
    for _ in range(TOP_K):
        _, ei = _first_max(masked, iota_e, N_EXPERTS)
        hit = iota_e == ei
        idx_rows.append(ei)
        sel_rows.append(jnp.sum(jnp.where(hit, scores, 0.0), axis=0, keepdims=True))
        masked = jnp.where(hit, neg, masked)
    sel = jnp.concatenate(sel_rows, axis=0)
    idx_ref[...] = jnp.concatenate(idx_rows, axis=0)
    wts_ref[...] = sel / jnp.sum(sel, axis=0, keepdims=True) * ROUTED_SCALE
    _store_chunks(xp_ref, _pack_bf16_pairs(xn))


def _router(h1, group, g_ffn, w_router_t, bias_col):
    blk0 = group * T_GRP // BM_R
    colblk = pl.BlockSpec((TOP_K, BM_R), lambda i: (0, i))
    ws = [g_ffn, w_router_t, bias_col]
    return pl.pallas_call(
        _router_kernel,
        grid=(T_GRP // BM_R,),
        in_specs=[pl.BlockSpec((BM_R, D_MODEL), lambda i: (blk0 + i, 0))] + [_full_spec(w, 1) for w in ws],
        out_specs=[pl.BlockSpec((BM_R * ROW_CHUNKS, LANES), lambda i: (i, 0)), colblk, colblk],
        out_shape=[jax.ShapeDtypeStruct((T_GRP * ROW_CHUNKS, LANES), U32),
                   jax.ShapeDtypeStruct((TOP_K, T_GRP), I32),
                   jax.ShapeDtypeStruct((TOP_K, T_GRP), F32)],
        compiler_params=_cparams(("parallel",)),
        name="router",
    )(h1, *ws)


def _rank_kernel(idx_ref, tri_ref, rank_ref, cnt_ref, carry_ref):
    @pl.when(pl.program_id(0) == 0)
    def _():
        carry_ref[...] = jnp.zeros_like(carry_ref)

    idx = idx_ref[...]
    n_tok = idx.shape[1]
    iota_e = lax.broadcasted_iota(I32, (N_EXPERTS, n_tok), 0)
    member = jnp.zeros((N_EXPERTS, n_tok), F32)
    for k in range(TOP_K):
        member = member + jnp.where(iota_e == idx[k:k + 1, :], 1.0, 0.0)
    before = jnp.dot(member.astype(BF16), tri_ref[...], preferred_element_type=F32) + carry_ref[...]
    rows = [jnp.sum(jnp.where(iota_e == idx[k:k + 1, :], before, 0.0), axis=0, keepdims=True)
            for k in range(TOP_K)]
    rank_ref[...] = jnp.concatenate(rows, axis=0).astype(I32)
    carry_ref[...] = carry_ref[...] + jnp.sum(member, axis=1, keepdims=True)
    cnt_ref[...] = carry_ref[...].astype(I32)


def _rank(idx_t, tri):
    blk = pl.BlockSpec((TOP_K, BT_RANK), lambda i: (0, i))
    return pl.pallas_call(
        _rank_kernel,
        grid=(T_GRP // BT_RANK,),
        in_specs=[blk, _full_spec(tri, 1)],
        out_specs=[blk, pl.BlockSpec((N_EXPERTS, 1), lambda i: (0, 0))],
        out_shape=[jax.ShapeDtypeStruct((TOP_K, T_GRP), I32),
                   jax.ShapeDtypeStruct((N_EXPERTS, 1), I32)],
        scratch_shapes=[pltpu.VMEM((N_EXPERTS, 1), F32)],
        compiler_params=_cparams(("arbitrary",)),
        name="rank",
    )(idx_t, tri)


def _dest_kernel(idx_ref, rank_ref, cnt_ref, dest_ref, blk_e_ref, n_used_ref, blk_rows_ref):
    counts = cnt_ref[...]
    padded = (counts + (BM_E - 1)) // BM_E * BM_E
    r = lax.broadcasted_iota(I32, (N_EXPERTS, N_EXPERTS), 0)
    c = lax.broadcasted_iota(I32, (N_EXPERTS, N_EXPERTS), 1)
    padded_row = jnp.sum(jnp.where(r == c, padded, 0), axis=0, keepdims=True)
    pad_start = jnp.sum(jnp.where(c < r, padded_row, 0), axis=1, keepdims=True)

    idx = idx_ref[...]
    n_tok = idx.shape[1]
    iota_e = lax.broadcasted_iota(I32, (N_EXPERTS, n_tok), 0)
    rows = [jnp.sum(jnp.where(iota_e == idx[k:k + 1, :], pad_start, 0), axis=0, keepdims=True)
            for k in range(TOP_K)]
    dest_ref[...] = jnp.concatenate(rows, axis=0) + rank_ref[...]

    pad_end_row = jnp.sum(jnp.where(r <= c, padded, 0), axis=0, keepdims=True)
    b0 = lax.broadcasted_iota(I32, (N_BLOCKS_PAD, N_EXPERTS), 0) * BM_E
    be = jnp.minimum(jnp.sum(jnp.where(pad_end_row <= b0, 1, 0), axis=1, keepdims=True), N_EXPERTS - 1)
    blk_e_ref[...] = be
    n_used_ref[...] = pad_end_row[:, N_EXPERTS - 1:N_EXPERTS] // BM_E
    counts_row = jnp.sum(jnp.where(r == c, counts, 0), axis=0, keepdims=True)
    mine = lax.broadcasted_iota(I32, (N_BLOCKS_PAD, N_EXPERTS), 1) == be
    end_valid = jnp.sum(jnp.where(mine, pad_end_row - padded_row + counts_row, 0), axis=1, keepdims=True)
    blk_rows_ref[...] = jnp.clip(end_valid - b0[:, :1], 0, BM_E)


N_BLOCKS_PAD = (N_BLOCKS + 7) // 8 * 8


def _dest(idx_t, rank_t, counts):
    blk = pl.BlockSpec((TOP_K, BT_RANK), lambda i: (0, i))
    one = lambda s: pl.BlockSpec(s, lambda i: (0, 0))
    return pl.pallas_call(
        _dest_kernel,
        grid=(T_GRP // BT_RANK,),
        in_specs=[blk, blk, one((N_EXPERTS, 1))],
        out_specs=[blk, one((N_BLOCKS_PAD, 1)), one((1, 1)), one((N_BLOCKS_PAD, 1))],
        out_shape=[jax.ShapeDtypeStruct((TOP_K, T_GRP), I32),
                   jax.ShapeDtypeStruct((N_BLOCKS_PAD, 1), I32),
                   jax.ShapeDtypeStruct((1, 1), I32),
                   jax.ShapeDtypeStruct((N_BLOCKS_PAD, 1), I32)],
        compiler_params=_cparams(("arbitrary",)),
        name="dest",
    )(idx_t, rank_t, counts)


def _sc_mesh():
    return plsc.VectorSubcoreMesh(core_axis_name="c", subcore_axis_name="s")


def _sc_worker_id():
    return lax.axis_index("s") * SC_CORES + lax.axis_index("c")


def _dispatch_body(dest_hbm, xp_hbm, xs_hbm, idx_v, rows_v, sem_in, sem_out):
    n_chunks, _, n_tok = dest_hbm.shape
    per_worker = n_chunks // SC_WORKERS
    chunk0 = _sc_worker_id() * per_worker

    def loads(i):
        chunk = chunk0 + i
        t0 = pl.multiple_of(chunk * n_tok, n_tok)
        return (pltpu.make_async_copy(dest_hbm.at[chunk], idx_v.at[i % 2], sem_in.at[i % 2]),
                pltpu.make_async_copy(xp_hbm.at[pl.ds(t0, n_tok)], rows_v.at[i % 2], sem_in.at[i % 2]))

    def scatters(i):
        return [pltpu.make_async_copy(rows_v.at[i % 2], xs_hbm.at[idx_v.at[i % 2, k]], sem_out.at[i % 2])
                for k in range(TOP_K)]

    for cp in loads(0):
        cp.start()
    for i in range(per_worker):
        for cp in loads(i):
            cp.wait()
        if i >= 1:
            for cp in scatters(i - 1):
                cp.wait()
        if i + 1 < per_worker:
            for cp in loads(i + 1):
                cp.start()
        for cp in scatters(i):
            cp.start()
    for cp in scatters(per_worker - 1):
        cp.wait()


def _dispatch(dest_chunks, xp3):
    return pl.kernel(
        _dispatch_body,
        out_type=jax.ShapeDtypeStruct((N_SLOTS, ROW_CHUNKS, LANES), U32),
        mesh=_sc_mesh(),
        scratch_types=[pltpu.VMEM((2, TOP_K, SC_CHUNK), I32),
                       pltpu.VMEM((2, SC_CHUNK, ROW_CHUNKS, LANES), U32),
                       pltpu.SemaphoreType.DMA((2,)), pltpu.SemaphoreType.DMA((2,))],
        name="dispatch",
    )(dest_chunks, xp3)


def _gather_body(dest_hbm, ys_hbm, yt_hbm, idx_v, rows_v, sem_in, sem_out):
    n_chunks, _, n_tok = dest_hbm.shape
    per_worker = n_chunks // SC_WORKERS
    chunk0 = _sc_worker_id() * per_worker

    @pl.loop(0, per_worker)
    def _(i):
        chunk = chunk0 + i
        t0 = pl.multiple_of(chunk * n_tok, n_tok)
        pltpu.sync_copy(dest_hbm.at[chunk], idx_v)

        def gather(k):
            return pltpu.make_async_copy(ys_hbm.at[idx_v.at[k]], rows_v.at[k % SC_RING], sem_in.at[k % SC_RING])

        def store(k):
            return pltpu.make_async_copy(rows_v.at[k % SC_RING], yt_hbm.at[k, pl.ds(t0, n_tok)],
                                         sem_out.at[k % SC_RING])

        for k in range(SC_RING):
            gather(k).start()
        for k in range(TOP_K):
            gather(k).wait()
            store(k).start()
            if k + SC_RING < TOP_K:
                store(k).wait()
                gather(k + SC_RING).start()
        for k in range(TOP_K - SC_RING, TOP_K):
            store(k).wait()


def _gather(dest_chunks, ys3):
    n_chunks, _, n_tok = dest_chunks.shape
    assert n_chunks % SC_WORKERS == 0
    return pl.kernel(
        _gather_body,
        out_type=jax.ShapeDtypeStruct((TOP_K, n_chunks * n_tok, ROW_CHUNKS, LANES), U32),
        mesh=_sc_mesh(),
        scratch_types=[pltpu.VMEM((TOP_K, n_tok), I32),
                       pltpu.VMEM((SC_RING, n_tok, ROW_CHUNKS, LANES), U32),
                       pltpu.SemaphoreType.DMA((SC_RING,)), pltpu.SemaphoreType.DMA((SC_RING,))],
        name="gather",
    )(dest_chunks, ys3)


def _experts_kernel(blk_e_ref, n_used_ref, blk_rows_ref, xs_ref, wg_ref, wu_ref, wd_ref, ys_ref, wgu_s, wd_s):
    b = pl.program_id(0)

    def swiglu_rows(n_rows):
        chunks = _load_chunks(xs_ref, n_rows)
        x_lo = jnp.concatenate([_unpack_lo(p) for p in chunks], axis=-1).astype(BF16)
        x_hi = jnp.concatenate([_unpack_hi(p) for p in chunks], axis=-1).astype(BF16)
        gu = (jnp.dot(x_lo, wgu_s[:HALF, :], preferred_element_type=F32)
              + jnp.dot(x_hi, wgu_s[HALF:, :], preferred_element_type=F32))
        gate, up = gu[:, :D_EXPERT], gu[:, D_EXPERT:]
        hmid = (gate * _sigmoid(gate) * up).astype(BF16)
        _store_chunks(ys_ref, _pack_bf16_pairs(jnp.dot(hmid, wd_s[...], preferred_element_type=F32)))

    @pl.when(b < n_used_ref[0])
    def _():
        prev = blk_e_ref[jnp.maximum(b - 1, 0)]

        @pl.when((b == 0) | (blk_e_ref[b] != prev))
        def _():
            wgu_s[:, :D_EXPERT] = wg_ref[0].astype(BF16)
            wgu_s[:, D_EXPERT:] = wu_ref[0].astype(BF16)
            wd_s[...] = wd_ref[0].astype(BF16)

        valid = blk_rows_ref[b]
        for n_rows in range(E_STRIP, BM_E + 1, E_STRIP):
            @pl.when((valid > n_rows - E_STRIP) & (valid <= n_rows))
            def _(n_rows=n_rows):
                swiglu_rows(n_rows)


def _experts(blk_e, n_used, blk_rows, xs, wg, wu, wd):
    def blk(b, be, nu, nr):
        return jnp.minimum(b, nu[0] - 1)

    def by_expert(shape):
        return pl.BlockSpec((1,) + shape, lambda b, be, nu, nr: (be[blk(b, be, nu, nr)], 0, 0))

    tile = pl.BlockSpec((BM_E * ROW_CHUNKS, LANES), lambda b, be, nu, nr: (blk(b, be, nu, nr), 0))
    grid_spec = pltpu.PrefetchScalarGridSpec(
        num_scalar_prefetch=3,
        grid=(N_BLOCKS,),
        in_specs=[tile, by_expert((D_MODEL, D_EXPERT)), by_expert((D_MODEL, D_EXPERT)),
                  by_expert((D_EXPERT, D_MODEL))],
        out_specs=tile,
        scratch_shapes=[pltpu.VMEM((D_MODEL, 2 * D_EXPERT), BF16), pltpu.VMEM((D_EXPERT, D_MODEL), BF16)],
    )
    return pl.pallas_call(
        _experts_kernel,
        grid_spec=grid_spec,
        out_shape=jax.ShapeDtypeStruct((N_SLOTS * ROW_CHUNKS, LANES), U32),
        compiler_params=_cparams(("arbitrary",)),
        name="experts",
    )(blk_e, n_used, blk_rows, xs, wg, wu, wd)


def _combine_kernel(yt_ref, wts_ref, h1_ref, p_ref, gffn_ref, wsgu_ref, wsd_ref, gple_ref, wpg_ref, wpp_ref,
                    gfin_ref, *y_refs):
    y_ref = y_refs[-1]
    h1 = h1_ref[...]
    gu = jnp.dot(_rms(h1, gffn_ref[...]).astype(BF16), wsgu_ref[...], preferred_element_type=F32)
    sgate, sup = gu[:, :D_SHARED], gu[:, D_SHARED:]
    hsh = h1 + jnp.dot((sgate * _sigmoid(sgate) * sup).astype(BF16), wsd_ref[...], preferred_element_type=F32)
    wts = jnp.transpose(wts_ref[...])
    lo = [jnp.zeros((BT_COMB, LANES), F32) for _ in range(ROW_CHUNKS)]
    hi = [jnp.zeros((BT_COMB, LANES), F32) for _ in range(ROW_CHUNKS)]
    for k in range(TOP_K):
        w = wts[:, k:k + 1]
        for c, p in enumerate(_load_chunks(yt_ref, BT_COMB, lead=(k,))):
            lo[c] = lo[c] + w * _unpack_lo(p)
            hi[c] = hi[c] + w * _unpack_hi(p)
    h2 = hsh + jnp.concatenate(lo + hi, axis=-1)
    gate = _sigmoid(jnp.dot(_rms(h2, gple_ref[...]).astype(BF16), wpg_ref[...], preferred_element_type=F32))
    proj = jnp.dot(p_ref[...].astype(BF16), wpp_ref[...], preferred_element_type=F32)
    y_ref[...] = _rms(h2 + proj * gate, gfin_ref[...])


def _combine(yt, yt_row0, wts_t, wts_row0, h1, tok_row0, n_rows, p2d, p_row0, ws, y_prev, out_rows, out_row0):
    assert all(r % BT_COMB == 0 for r in (yt_row0, wts_row0, tok_row0, n_rows, p_row0, out_row0))
    g0, w0, t0, p0, o0 = (r // BT_COMB for r in (yt_row0, wts_row0, tok_row0, p_row0, out_row0))
    in_specs = [pl.BlockSpec((TOP_K, BT_COMB * ROW_CHUNKS, LANES), lambda i: (0, g0 + i, 0)),
                pl.BlockSpec((TOP_K, BT_COMB), lambda i: (0, w0 + i)),
                pl.BlockSpec((BT_COMB, D_MODEL), lambda i: (t0 + i, 0)),
                pl.BlockSpec((BT_COMB, PLE_DIM), lambda i: (p0 + i, 0))] + [_full_spec(w, 1) for w in ws]
    args = [yt, wts_t, h1, p2d, *ws]
    aliases = {}
    if y_prev is not None:
        in_specs.append(pl.BlockSpec(memory_space=pl.ANY))
        aliases = {len(args): 0}
        args.append(y_prev)
    return pl.pallas_call(
        _combine_kernel,
        grid=(n_rows // BT_COMB,),
        in_specs=in_specs,
        out_specs=pl.BlockSpec((BT_COMB, D_MODEL), lambda i: (o0 + i, 0)),
        out_shape=jax.ShapeDtypeStruct((out_rows, D_MODEL), F32),
        input_output_aliases=aliases,
        compiler_params=_cparams(("parallel",)),
        name="combine",
    )(*args)


def kernel(x_prompt, x_sample, cache_k, cache_v, state_pool, p_prompt, p_sample, g_mix, w_in, attn_sinks,
           w_pool, pool_scale, g_att_out, g_pool_out, w_out, g_ffn, w_router, router_bias, w_exp_gate,
           w_exp_up, w_exp_down, w_sh_gate, w_sh_up, w_sh_down, g_ple, w_ple_gate, w_ple_proj, g_final):
    row = lambda a: a.reshape(1, -1)
    xp2d = x_prompt.reshape(T_P, D_MODEL)
    xs2d = x_sample.reshape(T_S, D_MODEL)
    w_in_bf = w_in[0].astype(BF16)
    mixer_wts = [w_pool[0].astype(BF16), row(pool_scale[0]), row(g_att_out[0]), row(g_pool_out[0]),
                 w_out[0].astype(BF16)]

    tab_p = _rope_tables(np.arange(SEQ))
    tab_s = _rope_tables(PAST_LEN + np.arange(DEC_SEQ), reps=BM_IN // DEC_SEQ)

    q_p, k_p, v_p, u_p = _inproj(xp2d, row(g_mix[0]), w_in_bf, tab_p, BF16)
    q_s, k_s, v_s, u_s = _inproj(xs2d, row(g_mix[0]), w_in_bf, tab_s, F32)

    h1 = _mixer_prompt(attn_sinks[0], xp2d, q_p, k_p, v_p, u_p, mixer_wts)
    h1, k_sample, v_sample, pool_sample = _mixer_sample(
        attn_sinks[0], xs2d, q_s, k_s, v_s, u_s,
        cache_k[0].reshape(DEC_BATCH, WINDOW, KV_WIDTH), cache_v[0].reshape(DEC_BATCH, WINDOW, KV_WIDTH),
        state_pool[0], mixer_wts, h1)

    g_ffn_row = row(g_ffn[0])
    router_wts = (g_ffn_row, w_router[0].T, router_bias[0].reshape(N_EXPERTS, 1))
    tri = (lax.broadcasted_iota(I32, (BT_RANK, BT_RANK), 0)
           < lax.broadcasted_iota(I32, (BT_RANK, BT_RANK), 1)).astype(BF16)

    def index_chunks(d, n_tok):
        return d.reshape(TOP_K, d.shape[1] // n_tok, n_tok).transpose(1, 0, 2)

    groups = []
    for g in range(N_GROUPS):
        xp, idx_t, wts_t = _router(h1, g, *router_wts)
        rank_t, counts = _rank(idx_t, tri)
        dest_t, *plan = _dest(idx_t, rank_t, counts)
        xs = _dispatch(index_chunks(dest_t, SC_CHUNK), xp.reshape(T_GRP, ROW_CHUNKS, LANES))
        groups.append((wts_t, dest_t, xs, [a.reshape(-1) for a in plan]))

    ple_wts = [g_ffn_row, jnp.concatenate([w_sh_gate[0], w_sh_up[0]], axis=1).astype(BF16),
               w_sh_down[0].astype(BF16),
               row(g_ple[0]), w_ple_gate[0].astype(BF16), w_ple_proj[0].astype(BF16), row(g_final)]
    pp2d = p_prompt[0].reshape(T_P, PLE_DIM)
    ps2d = p_sample[0].reshape(T_S, PLE_DIM)
    y_p = y_s = None
    for g, (wts_t, dest_t, xs, plan) in enumerate(groups):
        lo, hi = g * T_GRP, (g + 1) * T_GRP
        ys = _experts(*plan, xs.reshape(N_SLOTS * ROW_CHUNKS, LANES), w_exp_gate[0], w_exp_up[0], w_exp_down[0])
        ys3 = ys.reshape(N_SLOTS, ROW_CHUNKS, LANES)
        for s in range(N_SUB):
            a, b = lo + s * T_SUB, lo + (s + 1) * T_SUB
            yt = _gather(index_chunks(dest_t[:, a - lo:b - lo], SC_CHUNK_G), ys3)
            yt = yt.reshape(TOP_K, T_SUB * ROW_CHUNKS, LANES)
            if a < T_P:
                n = min(b, T_P) - a
                y_p = _combine(yt, 0, wts_t, a - lo, h1, a, n, pp2d, a, ple_wts, y_p, T_P, a)
            if b > T_P:
                s0 = max(a, T_P)
                y_s = _combine(yt, s0 - a, wts_t, s0 - lo, h1, s0, b - s0, ps2d, s0 - T_P, ple_wts, y_s, T_S,
                               s0 - T_P)

    kv5 = lambda a, b: a.reshape(1, b, WINDOW, N_KV_HEADS, HEAD_DIM)
    k_prompt = kv5(k_p.reshape(BATCH, SEQ, KV_WIDTH)[:, SEQ - WINDOW:], BATCH)
    v_prompt = kv5(v_p.reshape(BATCH, SEQ, KV_WIDTH)[:, SEQ - WINDOW:], BATCH)
    pool_prompt = u_p.reshape(BATCH, SEQ, POOL_WIDTH)[:, SEQ - POOL_STATE:][None]
    return (y_p.reshape(BATCH, SEQ, D_MODEL), y_s.reshape(DEC_BATCH, DEC_SEQ, D_MODEL),
            k_prompt, v_prompt, pool_prompt,
            kv5(k_sample, DEC_BATCH), kv5(v_sample, DEC_BATCH), pool_sample[None])
```

```python
import functools

import numpy as np
import jax
import jax.numpy as jnp
from jax import lax
from jax.experimental import pallas as pl
from jax.experimental.pallas import tpu as pltpu
from jax.experimental.pallas import tpu_sc as plsc

F32 = jnp.float32
BF16 = jnp.bfloat16
U32 = jnp.uint32
I32 = jnp.int32

D_MODEL = 1024
BATCH = 8
SEQ = 2048
DEC_BATCH = 128
DEC_SEQ = 8
PAST_LEN = 16384
N_Q_HEADS = 8
N_KV_HEADS = 2
HEAD_DIM = 64
GQA_GROUP = N_Q_HEADS // N_KV_HEADS
ATT_WIDTH = N_Q_HEADS * HEAD_DIM
KV_WIDTH = N_KV_HEADS * HEAD_DIM
WINDOW = 128
ROPE_THETA = 500000.0
ROT_DIM = HEAD_DIM // 4
POOL_WINDOWS = (2, 4, 8, 16)
POOL_GROUPS = 4
POOL_WIDTH = D_MODEL - ATT_WIDTH
POOL_GROUP_DIM = POOL_WIDTH // POOL_GROUPS
POOL_STATE = 15
IN_WIDTH = ATT_WIDTH + 2 * KV_WIDTH + POOL_WIDTH
N_EXPERTS = 64
TOP_K = 8
N_EXPERT_GROUPS = 8
GROUP_SIZE = N_EXPERTS // N_EXPERT_GROUPS
TOPK_GROUPS = 4
D_EXPERT = 256
D_SHARED = 256
ROUTED_SCALE = 2.5
PLE_DIM = 256
EPS = 1e-6

T_P = BATCH * SEQ
T_S = DEC_BATCH * DEC_SEQ
T_ALL = T_P + T_S
HALF = D_MODEL // 2
LANES = 128
VMEM_LIMIT = 48 * 1024 * 1024

BM_IN = 1024
BQ = 2 * WINDOW
EXPERTS_PER_MIX_STEP = N_EXPERTS * BQ // T_P
assert EXPERTS_PER_MIX_STEP * T_P == N_EXPERTS * BQ
SB = 16
BM_R = 512
BT_RANK = 512
BT_COMB = 256
N_GROUPS = 2
T_GRP = T_ALL // N_GROUPS
N_SUB = 2
T_SUB = T_GRP // N_SUB
assert T_GRP * N_GROUPS == T_ALL and T_SUB * N_SUB == T_GRP
BM_E = 1280
E_STRIP = 256
assert BM_E % E_STRIP == 0
N_ASSIGN = T_GRP * TOP_K
N_BLOCKS = -(-N_ASSIGN // BM_E) + N_EXPERTS
N_SLOTS = N_BLOCKS * BM_E

ROW_CHUNKS = HALF // LANES
SC_CORES = 2
SC_SUBCORES = 16
SC_WORKERS = SC_CORES * SC_SUBCORES
SC_CHUNK = 16
SC_CHUNK_G = 8
SC_RING = 4
assert T_GRP % (SC_WORKERS * SC_CHUNK) == 0 and T_SUB % (SC_WORKERS * SC_CHUNK_G) == 0


def _load_chunks(ref, n_rows, lead=()):
    return [ref[lead + (pl.ds(c, n_rows, stride=ROW_CHUNKS), slice(None))] for c in range(ROW_CHUNKS)]


def _store_chunks(ref, packed):
    n_rows = packed.shape[0]
    for c in range(ROW_CHUNKS):
        ref[pl.ds(c, n_rows, stride=ROW_CHUNKS), :] = packed[:, c * LANES:(c + 1) * LANES]


def _cparams(sem):
    return pltpu.CompilerParams(dimension_semantics=sem, vmem_limit_bytes=VMEM_LIMIT)


def _rms(x, g):
    return x * lax.rsqrt(jnp.mean(x * x, axis=-1, keepdims=True) + EPS) * g


def _sigmoid(x):
    return 1.0 / (1.0 + jnp.exp(-x))


def _pack_bf16_pairs(x):
    h = x.shape[-1] // 2
    return pltpu.pack_elementwise([x[:, :h], x[:, h:]], packed_dtype=BF16)


def _unpack_lo(p):
    return pltpu.bitcast(p << 16, F32)


def _unpack_hi(p):
    return pltpu.bitcast(p & jnp.uint32(0xFFFF0000), F32)


def _inproj_kernel(x_ref, g_ref, w_ref, c_ref, s1_ref, s2_ref, q_ref, k_ref, v_ref, u_ref):
    xn = _rms(x_ref[...], g_ref[...]).astype(BF16)
    z = jnp.dot(xn, w_ref[...], preferred_element_type=F32)
    c, s1, s2 = c_ref[...], s1_ref[...], s2_ref[...]

    def rope(t):
        return t * c + pltpu.roll(t, LANES - ROT_DIM // 2, 1) * s1 + pltpu.roll(t, ROT_DIM // 2, 1) * s2

    for i in range(ATT_WIDTH // LANES):
        sl = slice(i * LANES, (i + 1) * LANES)
        q_ref[:, sl] = (rope(z[:, sl]) * (HEAD_DIM ** -0.5)).astype(q_ref.dtype)
    k_ref[...] = rope(z[:, ATT_WIDTH:ATT_WIDTH + KV_WIDTH])
    v_ref[...] = z[:, ATT_WIDTH + KV_WIDTH:ATT_WIDTH + 2 * KV_WIDTH]
    u_ref[...] = z[:, ATT_WIDTH + 2 * KV_WIDTH:]


def _rope_tables(pos, reps=1):
    f32 = np.float32
    half = ROT_DIM // 2
    inv = np.power(f32(ROPE_THETA), -np.arange(half, dtype=f32) * f32(2.0) / f32(ROT_DIM)).astype(f32)
    ang = np.asarray(pos, f32)[:, None] * inv[None, :]
    cos, sin = np.cos(ang).astype(f32), np.sin(ang).astype(f32)
    n = len(pos)
    ones = np.ones((n, HEAD_DIM - ROT_DIM), f32)
    zeros = np.zeros((n, HEAD_DIM - ROT_DIM), f32)
    zh = np.zeros((n, half), f32)
    c = np.concatenate([cos, cos, ones], axis=1)
    s1 = np.concatenate([-sin, zh, zeros], axis=1)
    s2 = np.concatenate([zh, sin, zeros], axis=1)
    tile = lambda a: np.tile(a, (reps, LANES // HEAD_DIM))
    return tile(c), tile(s1), tile(s2)


def _inproj(x2d, g_mix, w_in_bf, tables, q_dtype):
    rows = x2d.shape[0]
    n_tab = tables[0].shape[0] // BM_IN
    row_spec = lambda w: pl.BlockSpec((BM_IN, w), lambda i: (i, 0))
    tab_spec = pl.BlockSpec((BM_IN, LANES), lambda i: (i % n_tab, 0))
    full = lambda a: pl.BlockSpec(a.shape, lambda i: (0,) * a.ndim)
    return pl.pallas_call(
        _inproj_kernel,
        grid=(rows // BM_IN,),
        in_specs=[row_spec(D_MODEL), full(g_mix), full(w_in_bf), tab_spec, tab_spec, tab_spec],
        out_specs=[row_spec(ATT_WIDTH), row_spec(KV_WIDTH), row_spec(KV_WIDTH), row_spec(POOL_WIDTH)],
        out_shape=[jax.ShapeDtypeStruct((rows, ATT_WIDTH), q_dtype),
                   jax.ShapeDtypeStruct((rows, KV_WIDTH), F32),
                   jax.ShapeDtypeStruct((rows, KV_WIDTH), F32),
                   jax.ShapeDtypeStruct((rows, POOL_WIDTH), F32)],
        compiler_params=_cparams(("parallel",)),
        name="inproj",
    )(x2d, g_mix, w_in_bf, *tables)


def _sink_column(sinks_ref, kv_head, rows_per_head):
    n = GQA_GROUP * rows_per_head
    grp = lax.broadcasted_iota(I32, (n, 1), 0) // rows_per_head
    col = jnp.full((n, 1), sinks_ref[kv_head * GQA_GROUP], F32)
    for g in range(1, GQA_GROUP):
        col = jnp.where(grp == g, sinks_ref[kv_head * GQA_GROUP + g], col)
    return col


def _band_mask(n_rows, rows_per_head, n_keys):
    i = lax.broadcasted_iota(I32, (n_rows, n_keys), 0) % rows_per_head
    c = lax.broadcasted_iota(I32, (n_rows, n_keys), 1)
    return (c >= i) & (c <= i + WINDOW), c


def _stack_heads(q, kv_head):
    return jnp.concatenate(
        [q[:, (kv_head * GQA_GROUP + g) * HEAD_DIM:(kv_head * GQA_GROUP + g + 1) * HEAD_DIM]
         for g in range(GQA_GROUP)], axis=0)


def _nt_dot(a, b):
    return lax.dot_general(a, b, (((1,), (1,)), ((), ())), preferred_element_type=F32)


POOL_HALO = 16
POOL_HEAD = 8
assert all(w == 2 << g for g, w in enumerate(POOL_WINDOWS)) and POOL_WINDOWS[-1] - 1 <= POOL_HALO


def _window_sums(ext_ref, n):
    lo, hi = POOL_HEAD, POOL_HEAD + POOL_HALO + n
    for p in range(POOL_GROUPS):
        lanes = slice(p * POOL_GROUP_DIM, POOL_WIDTH)
        ext_ref[lo:hi, lanes] = ext_ref[lo:hi, lanes] + ext_ref[lo - (1 << p):hi - (1 << p), lanes]


def _pool_out(d, wpool_ref, pscale_ref, gpool_ref):
    parts = [jnp.dot(d[:, g * POOL_GROUP_DIM:(g + 1) * POOL_GROUP_DIM].astype(BF16), wpool_ref[g],
                     preferred_element_type=F32) for g in range(POOL_GROUPS)]
    return _rms(jnp.concatenate(parts, axis=-1) * pscale_ref[...], gpool_ref[...])


def _mixer_tail(o_att, pooled, h, gatt_ref, wout_ref):
    mixed = jnp.concatenate([_rms(o_att, gatt_ref[...]), pooled], axis=-1)
    return h + jnp.dot(mixed.astype(BF16), wout_ref[...], preferred_element_type=F32)


def _mixer_prompt_kernel(sinks_ref, h_ref, q_ref, kc_ref, kp_ref, vc_ref, vp_ref, uc_ref, up_ref,
                         wpool_ref, pscale_ref, gatt_ref, gpool_ref, wout_ref, weg_ref, weu_ref, wed_ref,
                         h1_ref, wgu_bf_ref, wd_bf_ref, uext_ref):
    j = pl.program_id(1)
    wgu_bf_ref[:, :, :D_EXPERT] = weg_ref[...].astype(BF16)
    wgu_bf_ref[:, :, D_EXPERT:] = weu_ref[...].astype(BF16)
    wd_bf_ref[...] = wed_ref[...].astype(BF16)

    u = uc_ref[...]
    base = POOL_HEAD + POOL_HALO
    uext_ref[0:POOL_HEAD, :] = jnp.zeros((POOL_HEAD, POOL_WIDTH), F32)
    uext_ref[POOL_HEAD:base, :] = jnp.where(j > 0, up_ref[...], 0.0)
    uext_ref[base:base + BQ, :] = u
    _window_sums(uext_ref, BQ)
    pos = j * BQ + lax.broadcasted_iota(I32, (BQ, 1), 0)
    parts = []
    for g, w in enumerate(POOL_WINDOWS):
        sl = slice(g * POOL_GROUP_DIM, (g + 1) * POOL_GROUP_DIM)
        parts.append(uext_ref[base:base + BQ, sl] / jnp.minimum(pos + 1, w).astype(F32) - u[:, sl])
    pooled = _pool_out(jnp.concatenate(parts, axis=-1), wpool_ref, pscale_ref, gpool_ref)

    q = q_ref[...]
    k_all = jnp.concatenate([kp_ref[...], kc_ref[...]], axis=0).astype(BF16)
    v_all = jnp.concatenate([vp_ref[...], vc_ref[...]], axis=0).astype(BF16)
    ones = jnp.ones((WINDOW + BQ, HEAD_DIM), BF16)
    v_ones = [jnp.concatenate([v_all[:, hk * HEAD_DIM:(hk + 1) * HEAD_DIM], ones], axis=1)
              for hk in range(N_KV_HEADS)]
    band, col = _band_mask(GQA_GROUP * WINDOW, WINDOW, 2 * WINDOW)
    sinks = [_sink_column(sinks_ref, hk, WINDOW) for hk in range(N_KV_HEADS)]
    bands = []
    for b in range(BQ // WINDOW):
        rows = slice(b * WINDOW, (b + 1) * WINDOW)
        keys = slice(b * WINDOW, (b + 2) * WINDOW)
        mask = band & ((col >= WINDOW) | (j > 0)) if b == 0 else band
        heads = []
        for hk in range(N_KV_HEADS):
            sl = slice(hk * HEAD_DIM, (hk + 1) * HEAD_DIM)
            s = jnp.where(mask, _nt_dot(_stack_heads(q[rows], hk), k_all[keys, sl]), -jnp.inf)
            m = jnp.maximum(jnp.max(s, axis=-1, keepdims=True), sinks[hk])
            e = jnp.exp(s - m).astype(BF16)
            ov = jnp.dot(e, v_ones[hk][keys], preferred_element_type=F32)
            den = ov[:, HEAD_DIM:HEAD_DIM + 1] + jnp.exp(sinks[hk] - m)
            o = ov[:, :HEAD_DIM] / den
            heads += [o[g * WINDOW:(g + 1) * WINDOW] for g in range(GQA_GROUP)]
        bands.append(jnp.concatenate(heads, axis=-1))
    o_att = jnp.concatenate(bands, axis=0)
    h1_ref[...] = _mixer_tail(o_att, pooled, h_ref[...], gatt_ref, wout_ref)


def _mixer_sample_kernel(sinks_ref, h_ref, q_ref, kn_ref, vn_ref, u_ref, ck_ref, cv_ref, st_ref,
                         wpool_ref, pscale_ref, gatt_ref, gpool_ref, wout_ref, h1_in_ref,
                         h1_ref, ko_ref, vo_ref, po_ref, uext_ref):
    del h1_in_ref
    n_q = GQA_GROUP * DEC_SEQ
    n_keys = 2 * WINDOW
    band, col = _band_mask(n_q, DEC_SEQ, n_keys)
    mask = (band & (col < WINDOW + DEC_SEQ))[None]
    q3, kn3, vn3, u3 = q_ref[...], kn_ref[...], vn_ref[...], u_ref[...]
    ck, cv = ck_ref[...], cv_ref[...]
    ko_ref[:, 0:WINDOW - DEC_SEQ, :] = ck[:, DEC_SEQ:, :]
    ko_ref[:, WINDOW - DEC_SEQ:WINDOW, :] = kn3
    vo_ref[:, 0:WINDOW - DEC_SEQ, :] = cv[:, DEC_SEQ:, :]
    vo_ref[:, WINDOW - DEC_SEQ:WINDOW, :] = vn3
    pad = jnp.zeros((SB, WINDOW - DEC_SEQ, KV_WIDTH), F32)
    k_all = jnp.concatenate([ck, kn3, pad], axis=1).astype(BF16)
    v_all = jnp.concatenate([cv, vn3, pad], axis=1).astype(BF16)
    heads = []
    for hk in range(N_KV_HEADS):
        sl = slice(hk * HEAD_DIM, (hk + 1) * HEAD_DIM)
        qs = jnp.concatenate(
            [q3[:, :, (hk * GQA_GROUP + g) * HEAD_DIM:(hk * GQA_GROUP + g + 1) * HEAD_DIM]
             for g in range(GQA_GROUP)], axis=1).astype(BF16)
        sink = _sink_column(sinks_ref, hk, DEC_SEQ)[None]
        s = jnp.einsum("bqd,bkd->bqk", qs, k_all[:, :, sl], preferred_element_type=F32)
        s = jnp.where(mask, s, -jnp.inf)
        m = jnp.maximum(jnp.max(s, axis=-1, keepdims=True), sink)
        e = jnp.exp(s - m)
        den = jnp.sum(e, axis=-1, keepdims=True) + jnp.exp(sink - m)
        o = jnp.einsum("bqk,bkd->bqd", e.astype(BF16), v_all[:, :, sl], preferred_element_type=F32) / den
        heads += [o[:, g * DEC_SEQ:(g + 1) * DEC_SEQ, :] for g in range(GQA_GROUP)]
    o_att = jnp.concatenate(heads, axis=-1).reshape(SB * DEC_SEQ, ATT_WIDTH)

    uext_ref[:, 1:16, :] = st_ref[...]
    uext_ref[:, 16:16 + DEC_SEQ, :] = u3
    parts = []
    for g, w in enumerate(POOL_WINDOWS):
        sl = slice(g * POOL_GROUP_DIM, (g + 1) * POOL_GROUP_DIM)
        acc = u3[:, :, sl]
        for back in range(1, w):
            acc = acc + uext_ref[:, 16 - back:16 - back + DEC_SEQ, sl]
        parts.append(acc / float(w) - u3[:, :, sl])
    d = jnp.concatenate(parts, axis=-1).reshape(SB * DEC_SEQ, POOL_WIDTH)
    po_ref[...] = uext_ref[:, 16 + DEC_SEQ - POOL_STATE:16 + DEC_SEQ, :]
    pooled = _pool_out(d, wpool_ref, pscale_ref, gpool_ref)
    h1_ref[...] = _mixer_tail(o_att, pooled, h_ref[...], gatt_ref, wout_ref)


def _full_spec(a, n_grid):
    nd = a.ndim
    return pl.BlockSpec(a.shape, lambda *_: (0,) * nd)


def _mixer_prompt(sinks, x2d, q, k, v, u, wts, w_exp):
    nb = SEQ // BQ
    row = lambda w: pl.BlockSpec((BQ, w), lambda b, j: (b * nb + j, 0))
    prev = lambda w: pl.BlockSpec(
        (WINDOW, w), lambda b, j: (jnp.maximum((b * nb + j) * (BQ // WINDOW) - 1, 0), 0))
    uprev = pl.BlockSpec((POOL_HALO, POOL_WIDTH),
                         lambda b, j: (jnp.maximum((b * nb + j) * (BQ // POOL_HALO) - 1, 0), 0))
    smem = pl.BlockSpec(memory_space=pltpu.SMEM)
    per_step = lambda r, c: pl.BlockSpec((EXPERTS_PER_MIX_STEP, r, c), lambda b, j: (b * nb + j, 0, 0))
    return pl.pallas_call(
        _mixer_prompt_kernel,
        grid=(BATCH, nb),
        in_specs=[smem, row(D_MODEL), row(ATT_WIDTH), row(KV_WIDTH), prev(KV_WIDTH), row(KV_WIDTH),
                  prev(KV_WIDTH), row(POOL_WIDTH), uprev] + [_full_spec(w, 2) for w in wts]
                 + [per_step(D_MODEL, D_EXPERT), per_step(D_MODEL, D_EXPERT), per_step(D_EXPERT, D_MODEL)],
        out_specs=[row(D_MODEL), per_step(D_MODEL, 2 * D_EXPERT), per_step(D_EXPERT, D_MODEL)],
        out_shape=[jax.ShapeDtypeStruct((T_ALL, D_MODEL), F32),
                   jax.ShapeDtypeStruct((N_EXPERTS, D_MODEL, 2 * D_EXPERT), BF16),
                   jax.ShapeDtypeStruct((N_EXPERTS, D_EXPERT, D_MODEL), BF16)],
        scratch_shapes=[pltpu.VMEM((POOL_HEAD + POOL_HALO + BQ, POOL_WIDTH), F32)],
        compiler_params=_cparams(("parallel", "parallel")),
        name="mixer_prompt",
    )(sinks, x2d, q, k, k, v, v, u, u, *wts, *w_exp)


def _mixer_sample(sinks, x2d, q, k, v, u, cache_k, cache_v, state, wts, h1_buf):
    rows = SB * DEC_SEQ
    row = lambda w: pl.BlockSpec((rows, w), lambda i: (i, 0))
    bat = lambda a: pl.BlockSpec((SB,) + a.shape[1:], lambda i: (i, 0, 0))
    smem = pl.BlockSpec(memory_space=pltpu.SMEM)
    h1_blocks_before = T_P // rows
    n_in = 9 + len(wts)
    q, k, v, u = (a.reshape(DEC_BATCH, DEC_SEQ, a.shape[-1]) for a in (q, k, v, u))
    return pl.pallas_call(
        _mixer_sample_kernel,
        grid=(DEC_BATCH // SB,),
        in_specs=[smem, row(D_MODEL), bat(q), bat(k), bat(v), bat(u),
                  bat(cache_k), bat(cache_v), bat(state)] + [_full_spec(w, 1) for w in wts]
                 + [pl.BlockSpec(memory_space=pl.ANY)],
        out_specs=[pl.BlockSpec((rows, D_MODEL), lambda i: (h1_blocks_before + i, 0)),
                   bat(cache_k), bat(cache_v), bat(state)],
        out_shape=[jax.ShapeDtypeStruct((T_ALL, D_MODEL), F32),
                   jax.ShapeDtypeStruct(cache_k.shape, F32),
                   jax.ShapeDtypeStruct(cache_v.shape, F32),
                   jax.ShapeDtypeStruct(state.shape, F32)],
        scratch_shapes=[pltpu.VMEM((SB, 16 + DEC_SEQ, POOL_WIDTH), F32)],
        input_output_aliases={n_in: 0},
        compiler_params=_cparams(("parallel",)),
        name="mixer_sample",
    )(sinks, x2d, q, k, v, u, cache_k, cache_v, state, *wts, h1_buf)


def _first_max(vals, iota, n):
    m = jnp.max(vals, axis=0, keepdims=True)
    idx = jnp.min(jnp.where(vals == m, iota, n), axis=0, keepdims=True)
    return m, idx


def _router_kernel(h1_ref, gffn_ref, wrt_ref, bias_ref, xp_ref, idx_ref, wts_ref):
    xn = _rms(h1_ref[...], gffn_ref[...])
    logits = lax.dot_general(wrt_ref[...], xn, (((1,), (1,)), ((), ())),
                             precision=lax.Precision.HIGHEST, preferred_element_type=F32)
    scores = _sigmoid(logits)
    biased = scores + bias_ref[...]
    n_tok = biased.shape[1]
    neg = -jnp.inf

    iota_g = lax.broadcasted_iota(I32, (GROUP_SIZE, n_tok), 0)
    grp_rows = []
    for g in range(N_EXPERT_GROUPS):
        blk = biased[g * GROUP_SIZE:(g + 1) * GROUP_SIZE, :]
        top1, i1 = _first_max(blk, iota_g, GROUP_SIZE)
        top2 = jnp.max(jnp.where(iota_g == i1, neg, blk), axis=0, keepdims=True)
        grp_rows.append(top1 + top2)
    gs = jnp.concatenate(grp_rows, axis=0)

    iota_n = lax.broadcasted_iota(I32, (N_EXPERT_GROUPS, n_tok), 0)
    gsel = jnp.zeros((N_EXPERT_GROUPS, n_tok), jnp.bool_)
    for _ in range(TOPK_GROUPS):
        _, gi = _first_max(gs, iota_n, N_EXPERT_GROUPS)
        hit = iota_n == gi
        gsel = gsel | hit
        gs = jnp.where(hit, neg, gs)
    emask = jnp.concatenate(
        [jnp.broadcast_to(gsel[g:g + 1, :], (GROUP_SIZE, n_tok)) for g in range(N_EXPERT_GROUPS)], axis=0)
    masked = jnp.where(emask, biased, neg)

    iota_e = lax.broadcasted_iota(I32, (N_EXPERTS, n_tok), 0)
    idx_rows, sel_rows = [], []
    for _ in range(TOP_K):
        _, ei = _first_max(masked, iota_e, N_EXPERTS)
        hit = iota_e == ei
        idx_rows.append(ei)
        sel_rows.append(jnp.sum(jnp.where(hit, scores, 0.0), axis=0, keepdims=True))
        masked = jnp.where(hit, neg, masked)
    sel = jnp.concatenate(sel_rows, axis=0)
    idx_ref[...] = jnp.concatenate(idx_rows, axis=0)
    wts_ref[...] = sel / jnp.sum(sel, axis=0, keepdims=True) * ROUTED_SCALE
    _store_chunks(xp_ref, _pack_bf16_pairs(xn))


def _router(h1, group, g_ffn, w_router_t, bias_col):
    blk0 = group * T_GRP // BM_R
    colblk = pl.BlockSpec((TOP_K, BM_R), lambda i: (0, i))
    ws = [g_ffn, w_router_t, bias_col]
    return pl.pallas_call(
        _router_kernel,
        grid=(T_GRP // BM_R,),
        in_specs=[pl.BlockSpec((BM_R, D_MODEL), lambda i: (blk0 + i, 0))] + [_full_spec(w, 1) for w in ws],
        out_specs=[pl.BlockSpec((BM_R * ROW_CHUNKS, LANES), lambda i: (i, 0)), colblk, colblk],
        out_shape=[jax.ShapeDtypeStruct((T_GRP * ROW_CHUNKS, LANES), U32),
                   jax.ShapeDtypeStruct((TOP_K, T_GRP), I32),
                   jax.ShapeDtypeStruct((TOP_K, T_GRP), F32)],
        compiler_params=_cparams(("parallel",)),
        name="router",
    )(h1, *ws)


def _rank_kernel(idx_ref, tri_ref, rank_ref, cnt_ref, carry_ref):
    @pl.when(pl.program_id(0) == 0)
    def _():
        carry_ref[...] = jnp.zeros_like(carry_ref)

    idx = idx_ref[...]
    n_tok = idx.shape[1]
    iota_e = lax.broadcasted_iota(I32, (N_EXPERTS, n_tok), 0)
    member = jnp.zeros((N_EXPERTS, n_tok), F32)
    for k in range(TOP_K):
        member = member + jnp.where(iota_e == idx[k:k + 1, :], 1.0, 0.0)
    before = jnp.dot(member.astype(BF16), tri_ref[...], preferred_element_type=F32) + carry_ref[...]
    rows = [jnp.sum(jnp.where(iota_e == idx[k:k + 1, :], before, 0.0), axis=0, keepdims=True)
            for k in range(TOP_K)]
    rank_ref[...] = jnp.concatenate(rows, axis=0).astype(I32)
    carry_ref[...] = carry_ref[...] + jnp.sum(member, axis=1, keepdims=True)
    cnt_ref[...] = carry_ref[...].astype(I32)


def _rank(idx_t, tri):
    blk = pl.BlockSpec((TOP_K, BT_RANK), lambda i: (0, i))
    return pl.pallas_call(
        _rank_kernel,
        grid=(T_GRP // BT_RANK,),
        in_specs=[blk, _full_spec(tri, 1)],
        out_specs=[blk, pl.BlockSpec((N_EXPERTS, 1), lambda i: (0, 0))],
        out_shape=[jax.ShapeDtypeStruct((TOP_K, T_GRP), I32),
                   jax.ShapeDtypeStruct((N_EXPERTS, 1), I32)],
        scratch_shapes=[pltpu.VMEM((N_EXPERTS, 1), F32)],
        compiler_params=_cparams(("arbitrary",)),
        name="rank",
    )(idx_t, tri)


def _dest_kernel(idx_ref, rank_ref, cnt_ref, dest_ref, blk_e_ref, n_used_ref, blk_rows_ref):
    counts = cnt_ref[...]
    padded = (counts + (BM_E - 1)) // BM_E * BM_E
    r = lax.broadcasted_iota(I32, (N_EXPERTS, N_EXPERTS), 0)
    c = lax.broadcasted_iota(I32, (N_EXPERTS, N_EXPERTS), 1)
    padded_row = jnp.sum(jnp.where(r == c, padded, 0), axis=0, keepdims=True)
    pad_start = jnp.sum(jnp.where(c < r, padded_row, 0), axis=1, keepdims=True)

    idx = idx_ref[...]
    n_tok = idx.shape[1]
    iota_e = lax.broadcasted_iota(I32, (N_EXPERTS, n_tok), 0)
    rows = [jnp.sum(jnp.where(iota_e == idx[k:k + 1, :], pad_start, 0), axis=0, keepdims=True)
            for k in range(TOP_K)]
    dest_ref[...] = jnp.concatenate(rows, axis=0) + rank_ref[...]

    pad_end_row = jnp.sum(jnp.where(r <= c, padded, 0), axis=0, keepdims=True)
    b0 = lax.broadcasted_iota(I32, (N_BLOCKS_PAD, N_EXPERTS), 0) * BM_E
    be = jnp.minimum(jnp.sum(jnp.where(pad_end_row <= b0, 1, 0), axis=1, keepdims=True), N_EXPERTS - 1)
    blk_e_ref[...] = be
    n_used_ref[...] = pad_end_row[:, N_EXPERTS - 1:N_EXPERTS] // BM_E
    counts_row = jnp.sum(jnp.where(r == c, counts, 0), axis=0, keepdims=True)
    mine = lax.broadcasted_iota(I32, (N_BLOCKS_PAD, N_EXPERTS), 1) == be
    end_valid = jnp.sum(jnp.where(mine, pad_end_row - padded_row + counts_row, 0), axis=1, keepdims=True)
    blk_rows_ref[...] = jnp.clip(end_valid - b0[:, :1], 0, BM_E)


N_BLOCKS_PAD = (N_BLOCKS + 7) // 8 * 8


def _dest(idx_t, rank_t, counts):
    blk = pl.BlockSpec((TOP_K, BT_RANK), lambda i: (0, i))
    one = lambda s: pl.BlockSpec(s, lambda i: (0, 0))
    return pl.pallas_call(
        _dest_kernel,
        grid=(T_GRP // BT_RANK,),
        in_specs=[blk, blk, one((N_EXPERTS, 1))],
        out_specs=[blk, one((N_BLOCKS_PAD, 1)), one((1, 1)), one((N_BLOCKS_PAD, 1))],
        out_shape=[jax.ShapeDtypeStruct((TOP_K, T_GRP), I32),
                   jax.ShapeDtypeStruct((N_BLOCKS_PAD, 1), I32),
                   jax.ShapeDtypeStruct((1, 1), I32),
                   jax.ShapeDtypeStruct((N_BLOCKS_PAD, 1), I32)],
        compiler_params=_cparams(("arbitrary",)),
        name="dest",
    )(idx_t, rank_t, counts)


def _sc_mesh():
    return plsc.VectorSubcoreMesh(core_axis_name="c", subcore_axis_name="s")


def _sc_worker_id():
    return lax.axis_index("s") * SC_CORES + lax.axis_index("c")


def _dispatch_body(dest_hbm, xp_hbm, xs_hbm, idx_v, rows_v, sem_in, sem_out):
    n_chunks, _, n_tok = dest_hbm.shape
    per_worker = n_chunks // SC_WORKERS
    chunk0 = _sc_worker_id() * per_worker

    def loads(i):
        chunk = chunk0 + i
        t0 = pl.multiple_of(chunk * n_tok, n_tok)
        return (pltpu.make_async_copy(dest_hbm.at[chunk], idx_v.at[i % 2], sem_in.at[i % 2]),
                pltpu.make_async_copy(xp_hbm.at[pl.ds(t0, n_tok)], rows_v.at[i % 2], sem_in.at[i % 2]))

    def scatters(i):
        return [pltpu.make_async_copy(rows_v.at[i % 2], xs_hbm.at[idx_v.at[i % 2, k]], sem_out.at[i % 2])
                for k in range(TOP_K)]

    for cp in loads(0):
        cp.start()
    for i in range(per_worker):
        for cp in loads(i):
            cp.wait()
        if i >= 1:
            for cp in scatters(i - 1):
                cp.wait()
        if i + 1 < per_worker:
            for cp in loads(i + 1):
                cp.start()
        for cp in scatters(i):
            cp.start()
    for cp in scatters(per_worker - 1):
        cp.wait()


def _dispatch(dest_chunks, xp3):
    return pl.kernel(
        _dispatch_body,
        out_type=jax.ShapeDtypeStruct((N_SLOTS, ROW_CHUNKS, LANES), U32),
        mesh=_sc_mesh(),
        scratch_types=[pltpu.VMEM((2, TOP_K, SC_CHUNK), I32),
                       pltpu.VMEM((2, SC_CHUNK, ROW_CHUNKS, LANES), U32),
                       pltpu.SemaphoreType.DMA((2,)), pltpu.SemaphoreType.DMA((2,))],
        name="dispatch",
    )(dest_chunks, xp3)


def _gather_body(dest_hbm, ys_hbm, yt_hbm, idx_v, rows_v, sem_in, sem_out):
    n_chunks, _, n_tok = dest_hbm.shape
    per_worker = n_chunks // SC_WORKERS
    chunk0 = _sc_worker_id() * per_worker

    @pl.loop(0, per_worker)
    def _(i):
        chunk = chunk0 + i
        t0 = pl.multiple_of(chunk * n_tok, n_tok)
        pltpu.sync_copy(dest_hbm.at[chunk], idx_v)

        def gather(k):
            return pltpu.make_async_copy(ys_hbm.at[idx_v.at[k]], rows_v.at[k % SC_RING], sem_in.at[k % SC_RING])

        def store(k):
            return pltpu.make_async_copy(rows_v.at[k % SC_RING], yt_hbm.at[k, pl.ds(t0, n_tok)],
                                         sem_out.at[k % SC_RING])

        for k in range(SC_RING):
            gather(k).start()
        for k in range(TOP_K):
            gather(k).wait()
            store(k).start()
            if k + SC_RING < TOP_K:
                store(k).wait()
                gather(k + SC_RING).start()
        for k in range(TOP_K - SC_RING, TOP_K):
            store(k).wait()


def _gather(dest_chunks, ys3):
    n_chunks, _, n_tok = dest_chunks.shape
    assert n_chunks % SC_WORKERS == 0
    return pl.kernel(
        _gather_body,
        out_type=jax.ShapeDtypeStruct((TOP_K, n_chunks * n_tok, ROW_CHUNKS, LANES), U32),
        mesh=_sc_mesh(),
        scratch_types=[pltpu.VMEM((TOP_K, n_tok), I32),
                       pltpu.VMEM((SC_RING, n_tok, ROW_CHUNKS, LANES), U32),
                       pltpu.SemaphoreType.DMA((SC_RING,)), pltpu.SemaphoreType.DMA((SC_RING,))],
        name="gather",
    )(dest_chunks, ys3)


def _experts_kernel(blk_e_ref, n_used_ref, blk_rows_ref, xs_ref, wgu_ref, wd_ref, ys_ref):
    del blk_e_ref
    b = pl.program_id(0)

    def swiglu_rows(n_rows):
        chunks = _load_chunks(xs_ref, n_rows)
        x_lo = jnp.concatenate([_unpack_lo(p) for p in chunks], axis=-1).astype(BF16)
        x_hi = jnp.concatenate([_unpack_hi(p) for p in chunks], axis=-1).astype(BF16)
        gu = (jnp.dot(x_lo, wgu_ref[0, :HALF, :], preferred_element_type=F32)
              + jnp.dot(x_hi, wgu_ref[0, HALF:, :], preferred_element_type=F32))
        gate, up = gu[:, :D_EXPERT], gu[:, D_EXPERT:]
        hmid = (gate * _sigmoid(gate) * up).astype(BF16)
        _store_chunks(ys_ref, _pack_bf16_pairs(jnp.dot(hmid, wd_ref[0], preferred_element_type=F32)))

    @pl.when(b < n_used_ref[0])
    def _():
        valid = blk_rows_ref[b]
        for n_rows in range(E_STRIP, BM_E + 1, E_STRIP):
            @pl.when((valid > n_rows - E_STRIP) & (valid <= n_rows))
            def _(n_rows=n_rows):
                swiglu_rows(n_rows)


def _experts(blk_e, n_used, blk_rows, xs, wgu_bf, wd_bf):
    def blk(b, be, nu, nr):
        return jnp.minimum(b, nu[0] - 1)

    def by_expert(shape):
        return pl.BlockSpec((1,) + shape, lambda b, be, nu, nr: (be[blk(b, be, nu, nr)], 0, 0))

    tile = pl.BlockSpec((BM_E * ROW_CHUNKS, LANES), lambda b, be, nu, nr: (blk(b, be, nu, nr), 0))
    grid_spec = pltpu.PrefetchScalarGridSpec(
        num_scalar_prefetch=3,
        grid=(N_BLOCKS,),
        in_specs=[tile, by_expert((D_MODEL, 2 * D_EXPERT)), by_expert((D_EXPERT, D_MODEL))],
        out_specs=tile,
    )
    return pl.pallas_call(
        _experts_kernel,
        grid_spec=grid_spec,
        out_shape=jax.ShapeDtypeStruct((N_SLOTS * ROW_CHUNKS, LANES), U32),
        compiler_params=_cparams(("arbitrary",)),
        name="experts",
    )(blk_e, n_used, blk_rows, xs, wgu_bf, wd_bf)


def _combine_kernel(yt_ref, wts_ref, h1_ref, p_ref, gffn_ref, wsgu_ref, wsd_ref, gple_ref, wpg_ref, wpp_ref,
                    gfin_ref, *y_refs):
    y_ref = y_refs[-1]
    h1 = h1_ref[...]
    gu = jnp.dot(_rms(h1, gffn_ref[...]).astype(BF16), wsgu_ref[...], preferred_element_type=F32)
    sgate, sup = gu[:, :D_SHARED], gu[:, D_SHARED:]
    hsh = h1 + jnp.dot((sgate * _sigmoid(sgate) * sup).astype(BF16), wsd_ref[...], preferred_element_type=F32)
    wts = jnp.transpose(wts_ref[...])
    lo = [jnp.zeros((BT_COMB, LANES), F32) for _ in range(ROW_CHUNKS)]
    hi = [jnp.zeros((BT_COMB, LANES), F32) for _ in range(ROW_CHUNKS)]
    for k in range(TOP_K):
        w = wts[:, k:k + 1]
        for c, p in enumerate(_load_chunks(yt_ref, BT_COMB, lead=(k,))):
            lo[c] = lo[c] + w * _unpack_lo(p)
            hi[c] = hi[c] + w * _unpack_hi(p)
    h2 = hsh + jnp.concatenate(lo + hi, axis=-1)
    gate = _sigmoid(jnp.dot(_rms(h2, gple_ref[...]).astype(BF16), wpg_ref[...], preferred_element_type=F32))
    proj = jnp.dot(p_ref[...].astype(BF16), wpp_ref[...], preferred_element_type=F32)
    y_ref[...] = _rms(h2 + proj * gate, gfin_ref[...])


def _combine(yt, yt_row0, wts_t, wts_row0, h1, tok_row0, n_rows, p2d, p_row0, ws, y_prev, out_rows, out_row0):
    assert all(r % BT_COMB == 0 for r in (yt_row0, wts_row0, tok_row0, n_rows, p_row0, out_row0))
    g0, w0, t0, p0, o0 = (r // BT_COMB for r in (yt_row0, wts_row0, tok_row0, p_row0, out_row0))
    in_specs = [pl.BlockSpec((TOP_K, BT_COMB * ROW_CHUNKS, LANES), lambda i: (0, g0 + i, 0)),
                pl.BlockSpec((TOP_K, BT_COMB), lambda i: (0, w0 + i)),
                pl.BlockSpec((BT_COMB, D_MODEL), lambda i: (t0 + i, 0)),
                pl.BlockSpec((BT_COMB, PLE_DIM), lambda i: (p0 + i, 0))] + [_full_spec(w, 1) for w in ws]
    args = [yt, wts_t, h1, p2d, *ws]
    aliases = {}
    if y_prev is not None:
        in_specs.append(pl.BlockSpec(memory_space=pl.ANY))
        aliases = {len(args): 0}
        args.append(y_prev)
    return pl.pallas_call(
        _combine_kernel,
        grid=(n_rows // BT_COMB,),
        in_specs=in_specs,
        out_specs=pl.BlockSpec((BT_COMB, D_MODEL), lambda i: (o0 + i, 0)),
        out_shape=jax.ShapeDtypeStruct((out_rows, D_MODEL), F32),
        input_output_aliases=aliases,
        compiler_params=_cparams(("parallel",)),
        name="combine",
    )(*args)


def kernel(x_prompt, x_sample, cache_k, cache_v, state_pool, p_prompt, p_sample, g_mix, w_in, attn_sinks,
           w_pool, pool_scale, g_att_out, g_pool_out, w_out, g_ffn, w_router, router_bias, w_exp_gate,
           w_exp_up, w_exp_down, w_sh_gate, w_sh_up, w_sh_down, g_ple, w_ple_gate, w_ple_proj, g_final):
    row = lambda a: a.reshape(1, -1)
    xp2d = x_prompt.reshape(T_P, D_MODEL)
    xs2d = x_sample.reshape(T_S, D_MODEL)
    w_in_bf = w_in[0].astype(BF16)
    mixer_wts = [w_pool[0].astype(BF16), row(pool_scale[0]), row(g_att_out[0]), row(g_pool_out[0]),
                 w_out[0].astype(BF16)]

    tab_p = _rope_tables(np.arange(SEQ))
    tab_s = _rope_tables(PAST_LEN + np.arange(DEC_SEQ), reps=BM_IN // DEC_SEQ)

    q_p, k_p, v_p, u_p = _inproj(xp2d, row(g_mix[0]), w_in_bf, tab_p, BF16)
    q_s, k_s, v_s, u_s = _inproj(xs2d, row(g_mix[0]), w_in_bf, tab_s, F32)

    h1, wgu_bf, wd_bf = _mixer_prompt(attn_sinks[0], xp2d, q_p, k_p, v_p, u_p, mixer_wts,
                                      (w_exp_gate[0], w_exp_up[0], w_exp_down[0]))
    h1, k_sample, v_sample, pool_sample = _mixer_sample(
        attn_sinks[0], xs2d, q_s, k_s, v_s, u_s,
        cache_k[0].reshape(DEC_BATCH, WINDOW, KV_WIDTH), cache_v[0].reshape(DEC_BATCH, WINDOW, KV_WIDTH),
        state_pool[0], mixer_wts, h1)

    g_ffn_row = row(g_ffn[0])
    router_wts = (g_ffn_row, w_router[0].T, router_bias[0].reshape(N_EXPERTS, 1))
    tri = (lax.broadcasted_iota(I32, (BT_RANK, BT_RANK), 0)
           < lax.broadcasted_iota(I32, (BT_RANK, BT_RANK), 1)).astype(BF16)

    def index_chunks(d, n_tok):
        return d.reshape(TOP_K, d.shape[1] // n_tok, n_tok).transpose(1, 0, 2)

    groups = []
    for g in range(N_GROUPS):
        xp, idx_t, wts_t = _router(h1, g, *router_wts)
        rank_t, counts = _rank(idx_t, tri)
        dest_t, *plan = _dest(idx_t, rank_t, counts)
        xs = _dispatch(index_chunks(dest_t, SC_CHUNK), xp.reshape(T_GRP, ROW_CHUNKS, LANES))
        groups.append((wts_t, dest_t, xs, [a.reshape(-1) for a in plan]))

    ple_wts = [g_ffn_row, jnp.concatenate([w_sh_gate[0], w_sh_up[0]], axis=1).astype(BF16),
               w_sh_down[0].astype(BF16),
               row(g_ple[0]), w_ple_gate[0].astype(BF16), w_ple_proj[0].astype(BF16), row(g_final)]
    pp2d = p_prompt[0].reshape(T_P, PLE_DIM)
    ps2d = p_sample[0].reshape(T_S, PLE_DIM)
    y_p = y_s = None
    for g, (wts_t, dest_t, xs, plan) in enumerate(groups):
        lo, hi = g * T_GRP, (g + 1) * T_GRP
        ys = _experts(*plan, xs.reshape(N_SLOTS * ROW_CHUNKS, LANES), wgu_bf, wd_bf)
        ys3 = ys.reshape(N_SLOTS, ROW_CHUNKS, LANES)
        for s in range(N_SUB):
            a, b = lo + s * T_SUB, lo + (s + 1) * T_SUB
            yt = _gather(index_chunks(dest_t[:, a - lo:b - lo], SC_CHUNK_G), ys3)
            yt = yt.reshape(TOP_K, T_SUB * ROW_CHUNKS, LANES)
            if a < T_P:
                n = min(b, T_P) - a
                y_p = _combine(yt, 0, wts_t, a - lo, h1, a, n, pp2d, a, ple_wts, y_p, T_P, a)
            if b > T_P:
                s0 = max(a, T_P)
                y_s = _combine(yt, s0 - a, wts_t, s0 - lo, h1, s0, b - s0, ps2d, s0 - T_P, ple_wts, y_s, T_S,
                               s0 - T_P)

    kv5 = lambda a, b: a.reshape(1, b, WINDOW, N_KV_HEADS, HEAD_DIM)
    k_prompt = kv5(k_p.reshape(BATCH, SEQ, KV_WIDTH)[:, SEQ - WINDOW:], BATCH)
    v_prompt = kv5(v_p.reshape(BATCH, SEQ, KV_WIDTH)[:, SEQ - WINDOW:], BATCH)
    pool_prompt = u_p.reshape(BATCH, SEQ, POOL_WIDTH)[:, SEQ - POOL_STATE:][None]
    return (y_p.reshape(BATCH, SEQ, D_MODEL), y_s.reshape(DEC_BATCH, DEC_SEQ, D_MODEL),
            k_prompt, v_prompt, pool_prompt,
            kv5(k_sample, DEC_BATCH), kv5(v_sample, DEC_BATCH), pool_sample[None])
```

```python
import functools

import numpy as np
import jax
import jax.numpy as jnp
from jax import lax
from jax.experimental import pallas as pl
from jax.experimental.pallas import tpu as pltpu
from jax.experimental.pallas import tpu_sc as plsc

F32 = jnp.float32
BF16 = jnp.bfloat16
U32 = jnp.uint32
I32 = jnp.int32

D_MODEL = 1024
BATCH = 8
SEQ = 2048
DEC_BATCH = 128
DEC_SEQ = 8
PAST_LEN = 16384
N_Q_HEADS = 8
N_KV_HEADS = 2
HEAD_DIM = 64
GQA_GROUP = N_Q_HEADS // N_KV_HEADS
ATT_WIDTH = N_Q_HEADS * HEAD_DIM
KV_WIDTH = N_KV_HEADS * HEAD_DIM
WINDOW = 128
ROPE_THETA = 500000.0
ROT_DIM = HEAD_DIM // 4
POOL_WINDOWS = (2, 4, 8, 16)
POOL_GROUPS = 4
POOL_WIDTH = D_MODEL - ATT_WIDTH
POOL_GROUP_DIM = POOL_WIDTH // POOL_GROUPS
POOL_STATE = 15
IN_WIDTH = ATT_WIDTH + 2 * KV_WIDTH + POOL_WIDTH
N_EXPERTS = 64
TOP_K = 8
N_EXPERT_GROUPS = 8
GROUP_SIZE = N_EXPERTS // N_EXPERT_GROUPS
TOPK_GROUPS = 4
D_EXPERT = 256
D_SHARED = 256
ROUTED_SCALE = 2.5
PLE_DIM = 256
EPS = 1e-6

T_P = BATCH * SEQ
T_S = DEC_BATCH * DEC_SEQ
T_ALL = T_P + T_S
HALF = D_MODEL // 2
LANES = 128
VMEM_LIMIT = 48 * 1024 * 1024

BM_IN = 1024
BQ = 2 * WINDOW
EXPERTS_PER_MIX_STEP = N_EXPERTS * BQ // T_P
assert EXPERTS_PER_MIX_STEP * T_P == N_EXPERTS * BQ
SB = 16
BM_R = 512
BT_RANK = 512
BT_DEST = 2176
BT_COMB = 256
N_GROUPS = 2
T_GRP = T_ALL // N_GROUPS
N_SUB = 2
T_SUB = T_GRP // N_SUB
assert T_GRP * N_GROUPS == T_ALL and T_SUB * N_SUB == T_GRP
BM_E = 1280
E_STRIP = 256
assert BM_E % E_STRIP == 0
N_ASSIGN = T_GRP * TOP_K
N_BLOCKS = -(-N_ASSIGN // BM_E) + N_EXPERTS
N_SLOTS = N_BLOCKS * BM_E

ROW_CHUNKS = HALF // LANES
SC_CORES = 2
SC_SUBCORES = 16
SC_WORKERS = SC_CORES * SC_SUBCORES
SC_CHUNK = 16
SC_CHUNK_G = 8
SC_RING = 4
assert T_GRP % (SC_WORKERS * SC_CHUNK) == 0 and T_SUB % (SC_WORKERS * SC_CHUNK_G) == 0


def _load_chunks(ref, n_rows, lead=()):
    return [ref[lead + (pl.ds(c, n_rows, stride=ROW_CHUNKS), slice(None))] for c in range(ROW_CHUNKS)]


def _store_chunks(ref, packed):
    n_rows = packed.shape[0]
    for c in range(ROW_CHUNKS):
        ref[pl.ds(c, n_rows, stride=ROW_CHUNKS), :] = packed[:, c * LANES:(c + 1) * LANES]


def _cparams(sem):
    return pltpu.CompilerParams(dimension_semantics=sem, vmem_limit_bytes=VMEM_LIMIT)


def _rms(x, g):
    return x * lax.rsqrt(jnp.mean(x * x, axis=-1, keepdims=True) + EPS) * g


def _sigmoid(x):
    return 1.0 / (1.0 + jnp.exp(-x))


def _pack_bf16_pairs(x):
    h = x.shape[-1] // 2
    return pltpu.pack_elementwise([x[:, :h], x[:, h:]], packed_dtype=BF16)


def _unpack_lo(p):
    return pltpu.bitcast(p << 16, F32)


def _unpack_hi(p):
    return pltpu.bitcast(p & jnp.uint32(0xFFFF0000), F32)


def _inproj_kernel(x_ref, g_ref, w_ref, c_ref, s1_ref, s2_ref, q_ref, k_ref, v_ref, u_ref):
    xn = _rms(x_ref[...], g_ref[...]).astype(BF16)
    z = jnp.dot(xn, w_ref[...], preferred_element_type=F32)
    c, s1, s2 = c_ref[...], s1_ref[...], s2_ref[...]

    def rope(t):
        return t * c + pltpu.roll(t, LANES - ROT_DIM // 2, 1) * s1 + pltpu.roll(t, ROT_DIM // 2, 1) * s2

    for i in range(ATT_WIDTH // LANES):
        sl = slice(i * LANES, (i + 1) * LANES)
        q_ref[:, sl] = (rope(z[:, sl]) * (HEAD_DIM ** -0.5)).astype(q_ref.dtype)
    k_ref[...] = rope(z[:, ATT_WIDTH:ATT_WIDTH + KV_WIDTH])
    v_ref[...] = z[:, ATT_WIDTH + KV_WIDTH:ATT_WIDTH + 2 * KV_WIDTH]
    u_ref[...] = z[:, ATT_WIDTH + 2 * KV_WIDTH:]


def _rope_tables(pos, reps=1):
    f32 = np.float32
    half = ROT_DIM // 2
    inv = np.power(f32(ROPE_THETA), -np.arange(half, dtype=f32) * f32(2.0) / f32(ROT_DIM)).astype(f32)
    ang = np.asarray(pos, f32)[:, None] * inv[None, :]
    cos, sin = np.cos(ang).astype(f32), np.sin(ang).astype(f32)
    n = len(pos)
    ones = np.ones((n, HEAD_DIM - ROT_DIM), f32)
    zeros = np.zeros((n, HEAD_DIM - ROT_DIM), f32)
    zh = np.zeros((n, half), f32)
    c = np.concatenate([cos, cos, ones], axis=1)
    s1 = np.concatenate([-sin, zh, zeros], axis=1)
    s2 = np.concatenate([zh, sin, zeros], axis=1)
    tile = lambda a: np.tile(a, (reps, LANES // HEAD_DIM))
    return tile(c), tile(s1), tile(s2)


def _inproj(x2d, g_mix, w_in_bf, tables, q_dtype):
    rows = x2d.shape[0]
    n_tab = tables[0].shape[0] // BM_IN
    row_spec = lambda w: pl.BlockSpec((BM_IN, w), lambda i: (i, 0))
    tab_spec = pl.BlockSpec((BM_IN, LANES), lambda i: (i % n_tab, 0))
    full = lambda a: pl.BlockSpec(a.shape, lambda i: (0,) * a.ndim)
    return pl.pallas_call(
        _inproj_kernel,
        grid=(rows // BM_IN,),
        in_specs=[row_spec(D_MODEL), full(g_mix), full(w_in_bf), tab_spec, tab_spec, tab_spec],
        out_specs=[row_spec(ATT_WIDTH), row_spec(KV_WIDTH), row_spec(KV_WIDTH), row_spec(POOL_WIDTH)],
        out_shape=[jax.ShapeDtypeStruct((rows, ATT_WIDTH), q_dtype),
                   jax.ShapeDtypeStruct((rows, KV_WIDTH), F32),
                   jax.ShapeDtypeStruct((rows, KV_WIDTH), F32),
                   jax.ShapeDtypeStruct((rows, POOL_WIDTH), F32)],
        compiler_params=_cparams(("parallel",)),
        name="inproj",
    )(x2d, g_mix, w_in_bf, *tables)


def _sink_column(sinks_ref, kv_head, rows_per_head):
    n = GQA_GROUP * rows_per_head
    grp = lax.broadcasted_iota(I32, (n, 1), 0) // rows_per_head
    col = jnp.full((n, 1), sinks_ref[kv_head * GQA_GROUP], F32)
    for g in range(1, GQA_GROUP):
        col = jnp.where(grp == g, sinks_ref[kv_head * GQA_GROUP + g], col)
    return col


def _band_mask(n_rows, rows_per_head, n_keys):
    i = lax.broadcasted_iota(I32, (n_rows, n_keys), 0) % rows_per_head
    c = lax.broadcasted_iota(I32, (n_rows, n_keys), 1)
    return (c >= i) & (c <= i + WINDOW), c


def _stack_heads(q, kv_head):
    return jnp.concatenate(
        [q[:, (kv_head * GQA_GROUP + g) * HEAD_DIM:(kv_head * GQA_GROUP + g + 1) * HEAD_DIM]
         for g in range(GQA_GROUP)], axis=0)


def _nt_dot(a, b):
    return lax.dot_general(a, b, (((1,), (1,)), ((), ())), preferred_element_type=F32)


POOL_HALO = 16
POOL_HEAD = 8
assert all(w == 2 << g for g, w in enumerate(POOL_WINDOWS)) and POOL_WINDOWS[-1] - 1 <= POOL_HALO


def _window_sums(ext_ref, n):
    lo, hi = POOL_HEAD, POOL_HEAD + POOL_HALO + n
    for p in range(POOL_GROUPS):
        lanes = slice(p * POOL_GROUP_DIM, POOL_WIDTH)
        ext_ref[lo:hi, lanes] = ext_ref[lo:hi, lanes] + ext_ref[lo - (1 << p):hi - (1 << p), lanes]


def _pool_out(d, wpool_ref, pscale_ref, gpool_ref):
    parts = [jnp.dot(d[:, g * POOL_GROUP_DIM:(g + 1) * POOL_GROUP_DIM].astype(BF16), wpool_ref[g],
                     preferred_element_type=F32) for g in range(POOL_GROUPS)]
    return _rms(jnp.concatenate(parts, axis=-1) * pscale_ref[...], gpool_ref[...])


def _mixer_tail(o_att, pooled, h, gatt_ref, wout_ref):
    mixed = jnp.concatenate([_rms(o_att, gatt_ref[...]), pooled], axis=-1)
    return h + jnp.dot(mixed.astype(BF16), wout_ref[...], preferred_element_type=F32)


def _mixer_prompt_kernel(sinks_ref, h_ref, q_ref, kc_ref, kp_ref, vc_ref, vp_ref, uc_ref, up_ref,
                         wpool_ref, pscale_ref, gatt_ref, gpool_ref, wout_ref, weg_ref, weu_ref, wed_ref,
                         h1_ref, wgu_bf_ref, wd_bf_ref, uext_ref):
    j = pl.program_id(1)
    wgu_bf_ref[:, :, :D_EXPERT] = weg_ref[...].astype(BF16)
    wgu_bf_ref[:, :, D_EXPERT:] = weu_ref[...].astype(BF16)
    wd_bf_ref[...] = wed_ref[...].astype(BF16)

    u = uc_ref[...]
    base = POOL_HEAD + POOL_HALO
    uext_ref[0:POOL_HEAD, :] = jnp.zeros((POOL_HEAD, POOL_WIDTH), F32)
    uext_ref[POOL_HEAD:base, :] = jnp.where(j > 0, up_ref[...], 0.0)
    uext_ref[base:base + BQ, :] = u
    _window_sums(uext_ref, BQ)
    pos = j * BQ + lax.broadcasted_iota(I32, (BQ, 1), 0)
    parts = []
    for g, w in enumerate(POOL_WINDOWS):
        sl = slice(g * POOL_GROUP_DIM, (g + 1) * POOL_GROUP_DIM)
        parts.append(uext_ref[base:base + BQ, sl] / jnp.minimum(pos + 1, w).astype(F32) - u[:, sl])
    pooled = _pool_out(jnp.concatenate(parts, axis=-1), wpool_ref, pscale_ref, gpool_ref)

    q = q_ref[...]
    k_all = jnp.concatenate([kp_ref[...], kc_ref[...]], axis=0).astype(BF16)
    v_all = jnp.concatenate([vp_ref[...], vc_ref[...]], axis=0).astype(BF16)
    ones = jnp.ones((WINDOW + BQ, HEAD_DIM), BF16)
    v_ones = [jnp.concatenate([v_all[:, hk * HEAD_DIM:(hk + 1) * HEAD_DIM], ones], axis=1)
              for hk in range(N_KV_HEADS)]
    band, col = _band_mask(GQA_GROUP * WINDOW, WINDOW, 2 * WINDOW)
    sinks = [_sink_column(sinks_ref, hk, WINDOW) for hk in range(N_KV_HEADS)]
    bands = []
    for b in range(BQ // WINDOW):
        rows = slice(b * WINDOW, (b + 1) * WINDOW)
        keys = slice(b * WINDOW, (b + 2) * WINDOW)
        mask = band & ((col >= WINDOW) | (j > 0)) if b == 0 else band
        heads = []
        for hk in range(N_KV_HEADS):
            sl = slice(hk * HEAD_DIM, (hk + 1) * HEAD_DIM)
            s = jnp.where(mask, _nt_dot(_stack_heads(q[rows], hk), k_all[keys, sl]), -jnp.inf)
            m = jnp.maximum(jnp.max(s, axis=-1, keepdims=True), sinks[hk])
            e = jnp.exp(s - m).astype(BF16)
            ov = jnp.dot(e, v_ones[hk][keys], preferred_element_type=F32)
            den = ov[:, HEAD_DIM:HEAD_DIM + 1] + jnp.exp(sinks[hk] - m)
            o = ov[:, :HEAD_DIM] / den
            heads += [o[g * WINDOW:(g + 1) * WINDOW] for g in range(GQA_GROUP)]
        bands.append(jnp.concatenate(heads, axis=-1))
    o_att = jnp.concatenate(bands, axis=0)
    h1_ref[...] = _mixer_tail(o_att, pooled, h_ref[...], gatt_ref, wout_ref)


def _mixer_sample_kernel(sinks_ref, h_ref, q_ref, kn_ref, vn_ref, u_ref, ck_ref, cv_ref, st_ref,
                         wpool_ref, pscale_ref, gatt_ref, gpool_ref, wout_ref, h1_in_ref,
                         h1_ref, ko_ref, vo_ref, po_ref, uext_ref):
    del h1_in_ref
    n_q = GQA_GROUP * DEC_SEQ
    n_keys = 2 * WINDOW
    band, col = _band_mask(n_q, DEC_SEQ, n_keys)
    mask = (band & (col < WINDOW + DEC_SEQ))[None]
    q3, kn3, vn3, u3 = q_ref[...], kn_ref[...], vn_ref[...], u_ref[...]
    ck, cv = ck_ref[...], cv_ref[...]
    ko_ref[:, 0:WINDOW - DEC_SEQ, :] = ck[:, DEC_SEQ:, :]
    ko_ref[:, WINDOW - DEC_SEQ:WINDOW, :] = kn3
    vo_ref[:, 0:WINDOW - DEC_SEQ, :] = cv[:, DEC_SEQ:, :]
    vo_ref[:, WINDOW - DEC_SEQ:WINDOW, :] = vn3
    pad = jnp.zeros((SB, WINDOW - DEC_SEQ, KV_WIDTH), F32)
    k_all = jnp.concatenate([ck, kn3, pad], axis=1).astype(BF16)
    v_all = jnp.concatenate([cv, vn3, pad], axis=1).astype(BF16)
    heads = []
    for hk in range(N_KV_HEADS):
        sl = slice(hk * HEAD_DIM, (hk + 1) * HEAD_DIM)
        qs = jnp.concatenate(
            [q3[:, :, (hk * GQA_GROUP + g) * HEAD_DIM:(hk * GQA_GROUP + g + 1) * HEAD_DIM]
             for g in range(GQA_GROUP)], axis=1).astype(BF16)
        sink = _sink_column(sinks_ref, hk, DEC_SEQ)[None]
        s = jnp.einsum("bqd,bkd->bqk", qs, k_all[:, :, sl], preferred_element_type=F32)
        s = jnp.where(mask, s, -jnp.inf)
        m = jnp.maximum(jnp.max(s, axis=-1, keepdims=True), sink)
        e = jnp.exp(s - m)
        den = jnp.sum(e, axis=-1, keepdims=True) + jnp.exp(sink - m)
        o = jnp.einsum("bqk,bkd->bqd", e.astype(BF16), v_all[:, :, sl], preferred_element_type=F32) / den
        heads += [o[:, g * DEC_SEQ:(g + 1) * DEC_SEQ, :] for g in range(GQA_GROUP)]
    o_att = jnp.concatenate(heads, axis=-1).reshape(SB * DEC_SEQ, ATT_WIDTH)

    uext_ref[:, 1:16, :] = st_ref[...]
    uext_ref[:, 16:16 + DEC_SEQ, :] = u3
    parts = []
    for g, w in enumerate(POOL_WINDOWS):
        sl = slice(g * POOL_GROUP_DIM, (g + 1) * POOL_GROUP_DIM)
        acc = u3[:, :, sl]
        for back in range(1, w):
            acc = acc + uext_ref[:, 16 - back:16 - back + DEC_SEQ, sl]
        parts.append(acc / float(w) - u3[:, :, sl])
    d = jnp.concatenate(parts, axis=-1).reshape(SB * DEC_SEQ, POOL_WIDTH)
    po_ref[...] = uext_ref[:, 16 + DEC_SEQ - POOL_STATE:16 + DEC_SEQ, :]
    pooled = _pool_out(d, wpool_ref, pscale_ref, gpool_ref)
    h1_ref[...] = _mixer_tail(o_att, pooled, h_ref[...], gatt_ref, wout_ref)


def _full_spec(a, n_grid):
    nd = a.ndim
    return pl.BlockSpec(a.shape, lambda *_: (0,) * nd)


def _mixer_prompt(sinks, x2d, q, k, v, u, wts, w_exp):
    nb = SEQ // BQ
    row = lambda w: pl.BlockSpec((BQ, w), lambda b, j: (b * nb + j, 0))
    prev = lambda w: pl.BlockSpec(
        (WINDOW, w), lambda b, j: (jnp.maximum((b * nb + j) * (BQ // WINDOW) - 1, 0), 0))
    uprev = pl.BlockSpec((POOL_HALO, POOL_WIDTH),
                         lambda b, j: (jnp.maximum((b * nb + j) * (BQ // POOL_HALO) - 1, 0), 0))
    smem = pl.BlockSpec(memory_space=pltpu.SMEM)
    per_step = lambda r, c: pl.BlockSpec((EXPERTS_PER_MIX_STEP, r, c), lambda b, j: (b * nb + j, 0, 0))
    return pl.pallas_call(
        _mixer_prompt_kernel,
        grid=(BATCH, nb),
        in_specs=[smem, row(D_MODEL), row(ATT_WIDTH), row(KV_WIDTH), prev(KV_WIDTH), row(KV_WIDTH),
                  prev(KV_WIDTH), row(POOL_WIDTH), uprev] + [_full_spec(w, 2) for w in wts]
                 + [per_step(D_MODEL, D_EXPERT), per_step(D_MODEL, D_EXPERT), per_step(D_EXPERT, D_MODEL)],
        out_specs=[row(D_MODEL), per_step(D_MODEL, 2 * D_EXPERT), per_step(D_EXPERT, D_MODEL)],
        out_shape=[jax.ShapeDtypeStruct((T_ALL, D_MODEL), F32),
                   jax.ShapeDtypeStruct((N_EXPERTS, D_MODEL, 2 * D_EXPERT), BF16),
                   jax.ShapeDtypeStruct((N_EXPERTS, D_EXPERT, D_MODEL), BF16)],
        scratch_shapes=[pltpu.VMEM((POOL_HEAD + POOL_HALO + BQ, POOL_WIDTH), F32)],
        compiler_params=_cparams(("parallel", "parallel")),
        name="mixer_prompt",
    )(sinks, x2d, q, k, k, v, v, u, u, *wts, *w_exp)


def _mixer_sample(sinks, x2d, q, k, v, u, cache_k, cache_v, state, wts, h1_buf):
    rows = SB * DEC_SEQ
    row = lambda w: pl.BlockSpec((rows, w), lambda i: (i, 0))
    bat = lambda a: pl.BlockSpec((SB,) + a.shape[1:], lambda i: (i, 0, 0))
    smem = pl.BlockSpec(memory_space=pltpu.SMEM)
    h1_blocks_before = T_P // rows
    n_in = 9 + len(wts)
    q, k, v, u = (a.reshape(DEC_BATCH, DEC_SEQ, a.shape[-1]) for a in (q, k, v, u))
    return pl.pallas_call(
        _mixer_sample_kernel,
        grid=(DEC_BATCH // SB,),
        in_specs=[smem, row(D_MODEL), bat(q), bat(k), bat(v), bat(u),
                  bat(cache_k), bat(cache_v), bat(state)] + [_full_spec(w, 1) for w in wts]
                 + [pl.BlockSpec(memory_space=pl.ANY)],
        out_specs=[pl.BlockSpec((rows, D_MODEL), lambda i: (h1_blocks_before + i, 0)),
                   bat(cache_k), bat(cache_v), bat(state)],
        out_shape=[jax.ShapeDtypeStruct((T_ALL, D_MODEL), F32),
                   jax.ShapeDtypeStruct(cache_k.shape, F32),
                   jax.ShapeDtypeStruct(cache_v.shape, F32),
                   jax.ShapeDtypeStruct(state.shape, F32)],
        scratch_shapes=[pltpu.VMEM((SB, 16 + DEC_SEQ, POOL_WIDTH), F32)],
        input_output_aliases={n_in: 0},
        compiler_params=_cparams(("parallel",)),
        name="mixer_sample",
    )(sinks, x2d, q, k, v, u, cache_k, cache_v, state, *wts, h1_buf)


def _first_max(vals, iota, n):
    m = jnp.max(vals, axis=0, keepdims=True)
    idx = jnp.min(jnp.where(vals == m, iota, n), axis=0, keepdims=True)
    return m, idx


def _router_kernel(h1_ref, gffn_ref, wrt_ref, bias_ref, xp_ref, idx_ref, wts_ref):
    xn = _rms(h1_ref[...], gffn_ref[...])
    w = wrt_ref[...]
    w_hi, x_hi = w.astype(BF16), xn.astype(BF16)
    w_lo, x_lo = (w - w_hi.astype(F32)).astype(BF16), (xn - x_hi.astype(F32)).astype(BF16)
    logits = _nt_dot(w_hi, x_hi) + (_nt_dot(w_hi, x_lo) + _nt_dot(w_lo, x_hi))
    scores = _sigmoid(logits)
    biased = scores + bias_ref[...]
    n_tok = biased.shape[1]
    neg = -jnp.inf

    iota_g = lax.broadcasted_iota(I32, (GROUP_SIZE, n_tok), 0)
    grp_rows = []
    for g in range(N_EXPERT_GROUPS):
        blk = biased[g * GROUP_SIZE:(g + 1) * GROUP_SIZE, :]
        top1, i1 = _first_max(blk, iota_g, GROUP_SIZE)
        top2 = jnp.max(jnp.where(iota_g == i1, neg, blk), axis=0, keepdims=True)
        grp_rows.append(top1 + top2)
    gs = jnp.concatenate(grp_rows, axis=0)

    iota_n = lax.broadcasted_iota(I32, (N_EXPERT_GROUPS, n_tok), 0)
    gsel = jnp.zeros((N_EXPERT_GROUPS, n_tok), jnp.bool_)
    for _ in range(TOPK_GROUPS):
        _, gi = _first_max(gs, iota_n, N_EXPERT_GROUPS)
        hit = iota_n == gi
        gsel = gsel | hit
        gs = jnp.where(hit, neg, gs)
    emask = jnp.concatenate(
        [jnp.broadcast_to(gsel[g:g + 1, :], (GROUP_SIZE, n_tok)) for g in range(N_EXPERT_GROUPS)], axis=0)
    masked = jnp.where(emask, biased, neg)

    iota_e = lax.broadcasted_iota(I32, (N_EXPERTS, n_tok), 0)
    idx_rows, sel_rows = [], []
    for _ in range(TOP_K):
        _, ei = _first_max(masked, iota_e, N_EXPERTS)
        hit = iota_e == ei
        idx_rows.append(ei)
        sel_rows.append(jnp.sum(jnp.where(hit, scores, 0.0), axis=0, keepdims=True))
        masked = jnp.where(hit, neg, masked)
    sel = jnp.concatenate(sel_rows, axis=0)
    idx_ref[...] = jnp.concatenate(idx_rows, axis=0)
    wts_ref[...] = sel / jnp.sum(sel, axis=0, keepdims=True) * ROUTED_SCALE
    _store_chunks(xp_ref, _pack_bf16_pairs(xn))


def _router(h1, group, g_ffn, w_router_t, bias_col):
    blk0 = group * T_GRP // BM_R
    colblk = pl.BlockSpec((TOP_K, BM_R), lambda i: (0, i))
    ws = [g_ffn, w_router_t, bias_col]
    return pl.pallas_call(
        _router_kernel,
        grid=(T_GRP // BM_R,),
        in_specs=[pl.BlockSpec((BM_R, D_MODEL), lambda i: (blk0 + i, 0))] + [_full_spec(w, 1) for w in ws],
        out_specs=[pl.BlockSpec((BM_R * ROW_CHUNKS, LANES), lambda i: (i, 0)), colblk, colblk],
        out_shape=[jax.ShapeDtypeStruct((T_GRP * ROW_CHUNKS, LANES), U32),
                   jax.ShapeDtypeStruct((TOP_K, T_GRP), I32),
                   jax.ShapeDtypeStruct((TOP_K, T_GRP), F32)],
        compiler_params=_cparams(("parallel",)),
        name="router",
    )(h1, *ws)


def _rank_kernel(idx_ref, tri_ref, rank_ref, cnt_ref, carry_ref):
    @pl.when(pl.program_id(0) == 0)
    def _():
        carry_ref[...] = jnp.zeros_like(carry_ref)

    idx = idx_ref[...]
    n_tok = idx.shape[1]
    iota_e = lax.broadcasted_iota(I32, (N_EXPERTS, n_tok), 0)
    member = jnp.zeros((N_EXPERTS, n_tok), F32)
    for k in range(TOP_K):
        member = member + jnp.where(iota_e == idx[k:k + 1, :], 1.0, 0.0)
    before = jnp.dot(member.astype(BF16), tri_ref[...], preferred_element_type=F32) + carry_ref[...]
    rows = [jnp.sum(jnp.where(iota_e == idx[k:k + 1, :], before, 0.0), axis=0, keepdims=True)
            for k in range(TOP_K)]
    rank_ref[...] = jnp.concatenate(rows, axis=0).astype(I32)
    carry_ref[...] = carry_ref[...] + jnp.sum(member, axis=1, keepdims=True)
    cnt_ref[...] = carry_ref[...].astype(I32)


def _rank(idx_t, tri):
    blk = pl.BlockSpec((TOP_K, BT_RANK), lambda i: (0, i))
    return pl.pallas_call(
        _rank_kernel,
        grid=(T_GRP // BT_RANK,),
        in_specs=[blk, _full_spec(tri, 1)],
        out_specs=[blk, pl.BlockSpec((N_EXPERTS, 1), lambda i: (0, 0))],
        out_shape=[jax.ShapeDtypeStruct((TOP_K, T_GRP), I32),
                   jax.ShapeDtypeStruct((N_EXPERTS, 1), I32)],
        scratch_shapes=[pltpu.VMEM((N_EXPERTS, 1), F32)],
        compiler_params=_cparams(("arbitrary",)),
        name="rank",
    )(idx_t, tri)


def _dest_kernel(idx_ref, rank_ref, cnt_ref, dest_ref, blk_e_ref, n_used_ref, blk_rows_ref):
    counts = cnt_ref[...]
    padded = (counts + (BM_E - 1)) // BM_E * BM_E
    r = lax.broadcasted_iota(I32, (N_EXPERTS, N_EXPERTS), 0)
    c = lax.broadcasted_iota(I32, (N_EXPERTS, N_EXPERTS), 1)
    padded_row = jnp.sum(jnp.where(r == c, padded, 0), axis=0, keepdims=True)
    pad_start = jnp.sum(jnp.where(c < r, padded_row, 0), axis=1, keepdims=True)

    idx = idx_ref[...]
    n_tok = idx.shape[1]
    iota_e = lax.broadcasted_iota(I32, (N_EXPERTS, n_tok), 0)
    rows = [jnp.sum(jnp.where(iota_e == idx[k:k + 1, :], pad_start, 0), axis=0, keepdims=True)
            for k in range(TOP_K)]
    dest_ref[...] = jnp.concatenate(rows, axis=0) + rank_ref[...]

    @pl.when(pl.program_id(0) == 0)
    def _():
        pad_end_row = jnp.sum(jnp.where(r <= c, padded, 0), axis=0, keepdims=True)
        b0 = lax.broadcasted_iota(I32, (N_BLOCKS_PAD, N_EXPERTS), 0) * BM_E
        be = jnp.minimum(jnp.sum(jnp.where(pad_end_row <= b0, 1, 0), axis=1, keepdims=True), N_EXPERTS - 1)
        blk_e_ref[...] = be
        n_used_ref[...] = pad_end_row[:, N_EXPERTS - 1:N_EXPERTS] // BM_E
        counts_row = jnp.sum(jnp.where(r == c, counts, 0), axis=0, keepdims=True)
        mine = lax.broadcasted_iota(I32, (N_BLOCKS_PAD, N_EXPERTS), 1) == be
        end_valid = jnp.sum(jnp.where(mine, pad_end_row - padded_row + counts_row, 0), axis=1, keepdims=True)
        blk_rows_ref[...] = jnp.clip(end_valid - b0[:, :1], 0, BM_E)


N_BLOCKS_PAD = (N_BLOCKS + 7) // 8 * 8


def _dest(idx_t, rank_t, counts):
    blk = pl.BlockSpec((TOP_K, BT_DEST), lambda i: (0, i))
    one = lambda s: pl.BlockSpec(s, lambda i: (0, 0))
    return pl.pallas_call(
        _dest_kernel,
        grid=(T_GRP // BT_DEST,),
        in_specs=[blk, blk, one((N_EXPERTS, 1))],
        out_specs=[blk, one((N_BLOCKS_PAD, 1)), one((1, 1)), one((N_BLOCKS_PAD, 1))],
        out_shape=[jax.ShapeDtypeStruct((TOP_K, T_GRP), I32),
                   jax.ShapeDtypeStruct((N_BLOCKS_PAD, 1), I32),
                   jax.ShapeDtypeStruct((1, 1), I32),
                   jax.ShapeDtypeStruct((N_BLOCKS_PAD, 1), I32)],
        compiler_params=_cparams(("arbitrary",)),
        name="dest",
    )(idx_t, rank_t, counts)


def _sc_mesh():
    return plsc.VectorSubcoreMesh(core_axis_name="c", subcore_axis_name="s")


def _sc_worker_id():
    return lax.axis_index("s") * SC_CORES + lax.axis_index("c")


def _dispatch_body(dest_hbm, xp_hbm, xs_hbm, idx_v, rows_v, sem_in, sem_out):
    n_chunks, _, n_tok = dest_hbm.shape
    per_worker = n_chunks // SC_WORKERS
    chunk0 = _sc_worker_id() * per_worker

    def loads(i):
        chunk = chunk0 + i
        t0 = pl.multiple_of(chunk * n_tok, n_tok)
        return (pltpu.make_async_copy(dest_hbm.at[chunk], idx_v.at[i % 2], sem_in.at[i % 2]),
                pltpu.make_async_copy(xp_hbm.at[pl.ds(t0, n_tok)], rows_v.at[i % 2], sem_in.at[i % 2]))

    def scatters(i):
        return [pltpu.make_async_copy(rows_v.at[i % 2], xs_hbm.at[idx_v.at[i % 2, k]], sem_out.at[i % 2])
                for k in range(TOP_K)]

    for cp in loads(0):
        cp.start()
    for i in range(per_worker):
        for cp in loads(i):
            cp.wait()
        if i >= 1:
            for cp in scatters(i - 1):
                cp.wait()
        if i + 1 < per_worker:
            for cp in loads(i + 1):
                cp.start()
        for cp in scatters(i):
            cp.start()
    for cp in scatters(per_worker - 1):
        cp.wait()


def _dispatch(dest_chunks, xp3):
    return pl.kernel(
        _dispatch_body,
        out_type=jax.ShapeDtypeStruct((N_SLOTS, ROW_CHUNKS, LANES), U32),
        mesh=_sc_mesh(),
        scratch_types=[pltpu.VMEM((2, TOP_K, SC_CHUNK), I32),
                       pltpu.VMEM((2, SC_CHUNK, ROW_CHUNKS, LANES), U32),
                       pltpu.SemaphoreType.DMA((2,)), pltpu.SemaphoreType.DMA((2,))],
        name="dispatch",
    )(dest_chunks, xp3)


def _gather_body(dest_hbm, ys_hbm, yt_hbm, idx_v, rows_v, sem_in, sem_out):
    n_chunks, _, n_tok = dest_hbm.shape
    per_worker = n_chunks // SC_WORKERS
    chunk0 = _sc_worker_id() * per_worker

    @pl.loop(0, per_worker)
    def _(i):
        chunk = chunk0 + i
        t0 = pl.multiple_of(chunk * n_tok, n_tok)
        pltpu.sync_copy(dest_hbm.at[chunk], idx_v)

        def gather(k):
            return pltpu.make_async_copy(ys_hbm.at[idx_v.at[k]], rows_v.at[k % SC_RING], sem_in.at[k % SC_RING])

        def store(k):
            return pltpu.make_async_copy(rows_v.at[k % SC_RING], yt_hbm.at[k, pl.ds(t0, n_tok)],
                                         sem_out.at[k % SC_RING])

        for k in range(SC_RING):
            gather(k).start()
        for k in range(TOP_K):
            gather(k).wait()
            store(k).start()
            if k + SC_RING < TOP_K:
                store(k).wait()
                gather(k + SC_RING).start()
        for k in range(TOP_K - SC_RING, TOP_K):
            store(k).wait()


def _gather(dest_chunks, ys3):
    n_chunks, _, n_tok = dest_chunks.shape
    assert n_chunks % SC_WORKERS == 0
    return pl.kernel(
        _gather_body,
        out_type=jax.ShapeDtypeStruct((TOP_K, n_chunks * n_tok, ROW_CHUNKS, LANES), U32),
        mesh=_sc_mesh(),
        scratch_types=[pltpu.VMEM((TOP_K, n_tok), I32),
                       pltpu.VMEM((SC_RING, n_tok, ROW_CHUNKS, LANES), U32),
                       pltpu.SemaphoreType.DMA((SC_RING,)), pltpu.SemaphoreType.DMA((SC_RING,))],
        name="gather",
    )(dest_chunks, ys3)


def _experts_kernel(blk_e_ref, n_used_ref, blk_rows_ref, xs_ref, wgu_ref, wd_ref, ys_ref):
    del blk_e_ref
    b = pl.program_id(0)

    def swiglu_rows(n_rows):
        chunks = _load_chunks(xs_ref, n_rows)
        x_lo = jnp.concatenate([_unpack_lo(p) for p in chunks], axis=-1).astype(BF16)
        x_hi = jnp.concatenate([_unpack_hi(p) for p in chunks], axis=-1).astype(BF16)
        gu = (jnp.dot(x_lo, wgu_ref[0, :HALF, :], preferred_element_type=F32)
              + jnp.dot(x_hi, wgu_ref[0, HALF:, :], preferred_element_type=F32))
        gate, up = gu[:, :D_EXPERT], gu[:, D_EXPERT:]
        hmid = (gate * _sigmoid(gate) * up).astype(BF16)
        _store_chunks(ys_ref, _pack_bf16_pairs(jnp.dot(hmid, wd_ref[0], preferred_element_type=F32)))

    @pl.when(b < n_used_ref[0])
    def _():
        valid = blk_rows_ref[b]
        for n_rows in range(E_STRIP, BM_E + 1, E_STRIP):
            @pl.when((valid > n_rows - E_STRIP) & (valid <= n_rows))
            def _(n_rows=n_rows):
                swiglu_rows(n_rows)


def _experts(blk_e, n_used, blk_rows, xs, wgu_bf, wd_bf):
    def blk(b, be, nu, nr):
        return jnp.minimum(b, nu[0] - 1)

    def by_expert(shape):
        return pl.BlockSpec((1,) + shape, lambda b, be, nu, nr: (be[blk(b, be, nu, nr)], 0, 0))

    tile = pl.BlockSpec((BM_E * ROW_CHUNKS, LANES), lambda b, be, nu, nr: (blk(b, be, nu, nr), 0))
    grid_spec = pltpu.PrefetchScalarGridSpec(
        num_scalar_prefetch=3,
        grid=(N_BLOCKS,),
        in_specs=[tile, by_expert((D_MODEL, 2 * D_EXPERT)), by_expert((D_EXPERT, D_MODEL))],
        out_specs=tile,
    )
    return pl.pallas_call(
        _experts_kernel,
        grid_spec=grid_spec,
        out_shape=jax.ShapeDtypeStruct((N_SLOTS * ROW_CHUNKS, LANES), U32),
        compiler_params=_cparams(("arbitrary",)),
        name="experts",
    )(blk_e, n_used, blk_rows, xs, wgu_bf, wd_bf)


def _combine_kernel(yt_ref, wts_ref, h1_ref, p_ref, gffn_ref, wsgu_ref, wsd_ref, gple_ref, wpg_ref, wpp_ref,
                    gfin_ref, *y_refs):
    y_ref = y_refs[-1]
    h1 = h1_ref[...]
    gu = jnp.dot(_rms(h1, gffn_ref[...]).astype(BF16), wsgu_ref[...], preferred_element_type=F32)
    sgate, sup = gu[:, :D_SHARED], gu[:, D_SHARED:]
    hsh = h1 + jnp.dot((sgate * _sigmoid(sgate) * sup).astype(BF16), wsd_ref[...], preferred_element_type=F32)
    wts = jnp.transpose(wts_ref[...])
    lo = [jnp.zeros((BT_COMB, LANES), F32) for _ in range(ROW_CHUNKS)]
    hi = [jnp.zeros((BT_COMB, LANES), F32) for _ in range(ROW_CHUNKS)]
    for k in range(TOP_K):
        w = wts[:, k:k + 1]
        for c, p in enumerate(_load_chunks(yt_ref, BT_COMB, lead=(k,))):
            lo[c] = lo[c] + w * _unpack_lo(p)
            hi[c] = hi[c] + w * _unpack_hi(p)
    h2 = hsh + jnp.concatenate(lo + hi, axis=-1)
    gate = _sigmoid(jnp.dot(_rms(h2, gple_ref[...]).astype(BF16), wpg_ref[...], preferred_element_type=F32))
    proj = jnp.dot(p_ref[...].astype(BF16), wpp_ref[...], preferred_element_type=F32)
    y_ref[...] = _rms(h2 + proj * gate, gfin_ref[...])


def _combine(yt, yt_row0, wts_t, wts_row0, h1, tok_row0, n_rows, p2d, p_row0, ws, y_prev, out_rows, out_row0):
    assert all(r % BT_COMB == 0 for r in (yt_row0, wts_row0, tok_row0, n_rows, p_row0, out_row0))
    g0, w0, t0, p0, o0 = (r // BT_COMB for r in (yt_row0, wts_row0, tok_row0, p_row0, out_row0))
    in_specs = [pl.BlockSpec((TOP_K, BT_COMB * ROW_CHUNKS, LANES), lambda i: (0, g0 + i, 0)),
                pl.BlockSpec((TOP_K, BT_COMB), lambda i: (0, w0 + i)),
                pl.BlockSpec((BT_COMB, D_MODEL), lambda i: (t0 + i, 0)),
                pl.BlockSpec((BT_COMB, PLE_DIM), lambda i: (p0 + i, 0))] + [_full_spec(w, 1) for w in ws]
    args = [yt, wts_t, h1, p2d, *ws]
    aliases = {}
    if y_prev is not None:
        in_specs.append(pl.BlockSpec(memory_space=pl.ANY))
        aliases = {len(args): 0}
        args.append(y_prev)
    return pl.pallas_call(
        _combine_kernel,
        grid=(n_rows // BT_COMB,),
        in_specs=in_specs,
        out_specs=pl.BlockSpec((BT_COMB, D_MODEL), lambda i: (o0 + i, 0)),
        out_shape=jax.ShapeDtypeStruct((out_rows, D_MODEL), F32),
        input_output_aliases=aliases,
        compiler_params=_cparams(("parallel",)),
        name="combine",
    )(*args)


def kernel(x_prompt, x_sample, cache_k, cache_v, state_pool, p_prompt, p_sample, g_mix, w_in, attn_sinks,
           w_pool, pool_scale, g_att_out, g_pool_out, w_out, g_ffn, w_router, router_bias, w_exp_gate,
           w_exp_up, w_exp_down, w_sh_gate, w_sh_up, w_sh_down, g_ple, w_ple_gate, w_ple_proj, g_final):
    row = lambda a: a.reshape(1, -1)
    xp2d = x_prompt.reshape(T_P, D_MODEL)
    xs2d = x_sample.reshape(T_S, D_MODEL)
    w_in_bf = w_in[0].astype(BF16)
    mixer_wts = [w_pool[0].astype(BF16), row(pool_scale[0]), row(g_att_out[0]), row(g_pool_out[0]),
                 w_out[0].astype(BF16)]

    tab_p = _rope_tables(np.arange(SEQ))
    tab_s = _rope_tables(PAST_LEN + np.arange(DEC_SEQ), reps=BM_IN // DEC_SEQ)

    q_p, k_p, v_p, u_p = _inproj(xp2d, row(g_mix[0]), w_in_bf, tab_p, BF16)
    q_s, k_s, v_s, u_s = _inproj(xs2d, row(g_mix[0]), w_in_bf, tab_s, F32)

    h1, wgu_bf, wd_bf = _mixer_prompt(attn_sinks[0], xp2d, q_p, k_p, v_p, u_p, mixer_wts,
                                      (w_exp_gate[0], w_exp_up[0], w_exp_down[0]))
    h1, k_sample, v_sample, pool_sample = _mixer_sample(
        attn_sinks[0], xs2d, q_s, k_s, v_s, u_s,
        cache_k[0].reshape(DEC_BATCH, WINDOW, KV_WIDTH), cache_v[0].reshape(DEC_BATCH, WINDOW, KV_WIDTH),
        state_pool[0], mixer_wts, h1)

    g_ffn_row = row(g_ffn[0])
    router_wts = (g_ffn_row, w_router[0].T, router_bias[0].reshape(N_EXPERTS, 1))
    tri = (lax.broadcasted_iota(I32, (BT_RANK, BT_RANK), 0)
           < lax.broadcasted_iota(I32, (BT_RANK, BT_RANK), 1)).astype(BF16)

    def index_chunks(d, n_tok):
        return d.reshape(TOP_K, d.shape[1] // n_tok, n_tok).transpose(1, 0, 2)

    groups = []
    for g in range(N_GROUPS):
        xp, idx_t, wts_t = _router(h1, g, *router_wts)
        rank_t, counts = _rank(idx_t, tri)
        dest_t, *plan = _dest(idx_t, rank_t, counts)
        xs = _dispatch(index_chunks(dest_t, SC_CHUNK), xp.reshape(T_GRP, ROW_CHUNKS, LANES))
        groups.append((wts_t, index_chunks(dest_t, SC_CHUNK_G), xs, [a.reshape(-1) for a in plan]))

    ple_wts = [g_ffn_row, jnp.concatenate([w_sh_gate[0], w_sh_up[0]], axis=1).astype(BF16),
               w_sh_down[0].astype(BF16),
               row(g_ple[0]), w_ple_gate[0].astype(BF16), w_ple_proj[0].astype(BF16), row(g_final)]
    pp2d = p_prompt[0].reshape(T_P, PLE_DIM)
    ps2d = p_sample[0].reshape(T_S, PLE_DIM)
    y_p = y_s = None
    sub_chunks = T_SUB // SC_CHUNK_G
    for g, (wts_t, gather_chunks, xs, plan) in enumerate(groups):
        lo, hi = g * T_GRP, (g + 1) * T_GRP
        ys = _experts(*plan, xs.reshape(N_SLOTS * ROW_CHUNKS, LANES), wgu_bf, wd_bf)
        ys3 = ys.reshape(N_SLOTS, ROW_CHUNKS, LANES)
        for s in range(N_SUB):
            a, b = lo + s * T_SUB, lo + (s + 1) * T_SUB
            yt = _gather(gather_chunks[s * sub_chunks:(s + 1) * sub_chunks], ys3)
            yt = yt.reshape(TOP_K, T_SUB * ROW_CHUNKS, LANES)
            if a < T_P:
                n = min(b, T_P) - a
                y_p = _combine(yt, 0, wts_t, a - lo, h1, a, n, pp2d, a, ple_wts, y_p, T_P, a)
            if b > T_P:
                s0 = max(a, T_P)
                y_s = _combine(yt, s0 - a, wts_t, s0 - lo, h1, s0, b - s0, ps2d, s0 - T_P, ple_wts, y_s, T_S,
                               s0 - T_P)

    kv5 = lambda a, b: a.reshape(1, b, WINDOW, N_KV_HEADS, HEAD_DIM)
    k_prompt = kv5(k_p.reshape(BATCH, SEQ, KV_WIDTH)[:, SEQ - WINDOW:], BATCH)
    v_prompt = kv5(v_p.reshape(BATCH, SEQ, KV_WIDTH)[:, SEQ - WINDOW:], BATCH)
    pool_prompt = u_p.reshape(BATCH, SEQ, POOL_WIDTH)[:, SEQ - POOL_STATE:][None]
    return (y_p.reshape(BATCH, SEQ, D_MODEL), y_s.reshape(DEC_BATCH, DEC_SEQ, D_MODEL),
            k_prompt, v_prompt, pool_prompt,
            kv5(k_sample, DEC_BATCH), kv5(v_sample, DEC_BATCH), pool_sample[None])
```

```python
import functools

import numpy as np
import jax
import jax.numpy as jnp
from jax import lax
from jax.experimental import pallas as pl
from jax.experimental.pallas import tpu as pltpu
from jax.experimental.pallas import tpu_sc as plsc

F32 = jnp.float32
BF16 = jnp.bfloat16
U32 = jnp.uint32
I32 = jnp.int32

D_MODEL = 1024
BATCH = 8
SEQ = 2048
DEC_BATCH = 128
DEC_SEQ = 8
PAST_LEN = 16384
N_Q_HEADS = 8
N_KV_HEADS = 2
HEAD_DIM = 64
GQA_GROUP = N_Q_HEADS // N_KV_HEADS
ATT_WIDTH = N_Q_HEADS * HEAD_DIM
KV_WIDTH = N_KV_HEADS * HEAD_DIM
WINDOW = 128
ROPE_THETA = 500000.0
ROT_DIM = HEAD_DIM // 4
POOL_WINDOWS = (2, 4, 8, 16)
POOL_GROUPS = 4
POOL_WIDTH = D_MODEL - ATT_WIDTH
POOL_GROUP_DIM = POOL_WIDTH // POOL_GROUPS
POOL_STATE = 15
IN_WIDTH = ATT_WIDTH + 2 * KV_WIDTH + POOL_WIDTH
N_EXPERTS = 64
TOP_K = 8
N_EXPERT_GROUPS = 8
GROUP_SIZE = N_EXPERTS // N_EXPERT_GROUPS
TOPK_GROUPS = 4
D_EXPERT = 256
D_SHARED = 256
ROUTED_SCALE = 2.5
PLE_DIM = 256
EPS = 1e-6

T_P = BATCH * SEQ
T_S = DEC_BATCH * DEC_SEQ
T_ALL = T_P + T_S
HALF = D_MODEL // 2
LANES = 128
VMEM_LIMIT = 48 * 1024 * 1024

BM_IN = 1024
BQ = 2 * WINDOW
QB = WINDOW // 2
EXPERTS_PER_MIX_STEP = N_EXPERTS * BQ // T_P
assert EXPERTS_PER_MIX_STEP * T_P == N_EXPERTS * BQ
SB = 16
BM_R = 512
BT_RANK = 512
BT_DEST = 2176
BT_COMB = 256
N_GROUPS = 2
T_GRP = T_ALL // N_GROUPS
N_SUB = 2
T_SUB = T_GRP // N_SUB
assert T_GRP * N_GROUPS == T_ALL and T_SUB * N_SUB == T_GRP
BM_E = 1280
E_STRIP = 256
assert BM_E % E_STRIP == 0
N_ASSIGN = T_GRP * TOP_K
N_BLOCKS = -(-N_ASSIGN // BM_E) + N_EXPERTS
N_SLOTS = N_BLOCKS * BM_E

ROW_CHUNKS = HALF // LANES
SC_CORES = 2
SC_SUBCORES = 16
SC_WORKERS = SC_CORES * SC_SUBCORES
SC_CHUNK = 16
SC_CHUNK_G = 8
SC_RING = 4
assert T_GRP % (SC_WORKERS * SC_CHUNK) == 0 and T_SUB % (SC_WORKERS * SC_CHUNK_G) == 0


def _load_chunks(ref, n_rows, lead=()):
    return [ref[lead + (pl.ds(c, n_rows, stride=ROW_CHUNKS), slice(None))] for c in range(ROW_CHUNKS)]


def _store_chunks(ref, packed):
    n_rows = packed.shape[0]
    for c in range(ROW_CHUNKS):
        ref[pl.ds(c, n_rows, stride=ROW_CHUNKS), :] = packed[:, c * LANES:(c + 1) * LANES]


def _cparams(sem):
    return pltpu.CompilerParams(dimension_semantics=sem, vmem_limit_bytes=VMEM_LIMIT)


def _rms(x, g):
    return x * lax.rsqrt(jnp.mean(x * x, axis=-1, keepdims=True) + EPS) * g


def _sigmoid(x):
    return 1.0 / (1.0 + jnp.exp(-x))


def _pack_bf16_pairs(x):
    h = x.shape[-1] // 2
    return pltpu.pack_elementwise([x[:, :h], x[:, h:]], packed_dtype=BF16)


def _unpack_lo(p):
    return pltpu.bitcast(p << 16, F32)


def _unpack_hi(p):
    return pltpu.bitcast(p & jnp.uint32(0xFFFF0000), F32)


def _inproj_kernel(x_ref, g_ref, w_ref, c_ref, s1_ref, s2_ref, q_ref, k_ref, v_ref, u_ref):
    xn = _rms(x_ref[...], g_ref[...]).astype(BF16)
    z = jnp.dot(xn, w_ref[...], preferred_element_type=F32)
    c, s1, s2 = c_ref[...], s1_ref[...], s2_ref[...]

    def rope(t):
        return t * c + pltpu.roll(t, LANES - ROT_DIM // 2, 1) * s1 + pltpu.roll(t, ROT_DIM // 2, 1) * s2

    for i in range(ATT_WIDTH // LANES):
        sl = slice(i * LANES, (i + 1) * LANES)
        q_ref[:, sl] = (rope(z[:, sl]) * (HEAD_DIM ** -0.5)).astype(q_ref.dtype)
    k_ref[...] = rope(z[:, ATT_WIDTH:ATT_WIDTH + KV_WIDTH])
    v_ref[...] = z[:, ATT_WIDTH + KV_WIDTH:ATT_WIDTH + 2 * KV_WIDTH]
    u_ref[...] = z[:, ATT_WIDTH + 2 * KV_WIDTH:]


def _rope_tables(pos, reps=1):
    f32 = np.float32
    half = ROT_DIM // 2
    inv = np.power(f32(ROPE_THETA), -np.arange(half, dtype=f32) * f32(2.0) / f32(ROT_DIM)).astype(f32)
    ang = np.asarray(pos, f32)[:, None] * inv[None, :]
    cos, sin = np.cos(ang).astype(f32), np.sin(ang).astype(f32)
    n = len(pos)
    ones = np.ones((n, HEAD_DIM - ROT_DIM), f32)
    zeros = np.zeros((n, HEAD_DIM - ROT_DIM), f32)
    zh = np.zeros((n, half), f32)
    c = np.concatenate([cos, cos, ones], axis=1)
    s1 = np.concatenate([-sin, zh, zeros], axis=1)
    s2 = np.concatenate([zh, sin, zeros], axis=1)
    tile = lambda a: np.tile(a, (reps, LANES // HEAD_DIM))
    return tile(c), tile(s1), tile(s2)


def _inproj(x2d, g_mix, w_in_bf, tables, q_dtype):
    rows = x2d.shape[0]
    n_tab = tables[0].shape[0] // BM_IN
    row_spec = lambda w: pl.BlockSpec((BM_IN, w), lambda i: (i, 0))
    tab_spec = pl.BlockSpec((BM_IN, LANES), lambda i: (i % n_tab, 0))
    full = lambda a: pl.BlockSpec(a.shape, lambda i: (0,) * a.ndim)
    return pl.pallas_call(
        _inproj_kernel,
        grid=(rows // BM_IN,),
        in_specs=[row_spec(D_MODEL), full(g_mix), full(w_in_bf), tab_spec, tab_spec, tab_spec],
        out_specs=[row_spec(ATT_WIDTH), row_spec(KV_WIDTH), row_spec(KV_WIDTH), row_spec(POOL_WIDTH)],
        out_shape=[jax.ShapeDtypeStruct((rows, ATT_WIDTH), q_dtype),
                   jax.ShapeDtypeStruct((rows, KV_WIDTH), F32),
                   jax.ShapeDtypeStruct((rows, KV_WIDTH), F32),
                   jax.ShapeDtypeStruct((rows, POOL_WIDTH), F32)],
        compiler_params=_cparams(("parallel",)),
        name="inproj",
    )(x2d, g_mix, w_in_bf, *tables)


def _sink_column(sinks_ref, kv_head, rows_per_head):
    n = GQA_GROUP * rows_per_head
    grp = lax.broadcasted_iota(I32, (n, 1), 0) // rows_per_head
    col = jnp.full((n, 1), sinks_ref[kv_head * GQA_GROUP], F32)
    for g in range(1, GQA_GROUP):
        col = jnp.where(grp == g, sinks_ref[kv_head * GQA_GROUP + g], col)
    return col


def _band_mask(n_rows, rows_per_head, n_keys):
    i = lax.broadcasted_iota(I32, (n_rows, n_keys), 0) % rows_per_head
    c = lax.broadcasted_iota(I32, (n_rows, n_keys), 1)
    return (c >= i) & (c <= i + WINDOW), c


def _stack_heads(q, kv_head):
    return jnp.concatenate(
        [q[:, (kv_head * GQA_GROUP + g) * HEAD_DIM:(kv_head * GQA_GROUP + g + 1) * HEAD_DIM]
         for g in range(GQA_GROUP)], axis=0)


def _nt_dot(a, b):
    return lax.dot_general(a, b, (((1,), (1,)), ((), ())), preferred_element_type=F32)


POOL_HALO = 16
POOL_HEAD = 8
assert all(w == 2 << g for g, w in enumerate(POOL_WINDOWS)) and POOL_WINDOWS[-1] - 1 <= POOL_HALO


def _window_sums(ext_ref, n):
    lo, hi = POOL_HEAD, POOL_HEAD + POOL_HALO + n
    for p in range(POOL_GROUPS):
        lanes = slice(p * POOL_GROUP_DIM, POOL_WIDTH)
        ext_ref[lo:hi, lanes] = ext_ref[lo:hi, lanes] + ext_ref[lo - (1 << p):hi - (1 << p), lanes]


def _pool_out(d, wpool_ref, pscale_ref, gpool_ref):
    parts = [jnp.dot(d[:, g * POOL_GROUP_DIM:(g + 1) * POOL_GROUP_DIM].astype(BF16), wpool_ref[g],
                     preferred_element_type=F32) for g in range(POOL_GROUPS)]
    return _rms(jnp.concatenate(parts, axis=-1) * pscale_ref[...], gpool_ref[...])


def _mixer_tail(o_att, pooled, h, gatt_ref, wout_ref):
    mixed = jnp.concatenate([_rms(o_att, gatt_ref[...]), pooled], axis=-1)
    return h + jnp.dot(mixed.astype(BF16), wout_ref[...], preferred_element_type=F32)


def _mixer_prompt_kernel(sinks_ref, h_ref, q_ref, kc_ref, kp_ref, vc_ref, vp_ref, uc_ref, up_ref,
                         wpool_ref, pscale_ref, gatt_ref, gpool_ref, wout_ref, weg_ref, weu_ref, wed_ref,
                         h1_ref, wgu_bf_ref, wd_bf_ref, uext_ref):
    j = pl.program_id(1)
    wgu_bf_ref[:, :, :D_EXPERT] = weg_ref[...].astype(BF16)
    wgu_bf_ref[:, :, D_EXPERT:] = weu_ref[...].astype(BF16)
    wd_bf_ref[...] = wed_ref[...].astype(BF16)

    u = uc_ref[...]
    base = POOL_HEAD + POOL_HALO
    uext_ref[0:POOL_HEAD, :] = jnp.zeros((POOL_HEAD, POOL_WIDTH), F32)
    uext_ref[POOL_HEAD:base, :] = jnp.where(j > 0, up_ref[...], 0.0)
    uext_ref[base:base + BQ, :] = u
    _window_sums(uext_ref, BQ)
    pos = j * BQ + lax.broadcasted_iota(I32, (BQ, 1), 0)
    parts = []
    for g, w in enumerate(POOL_WINDOWS):
        sl = slice(g * POOL_GROUP_DIM, (g + 1) * POOL_GROUP_DIM)
        parts.append(uext_ref[base:base + BQ, sl] / jnp.minimum(pos + 1, w).astype(F32) - u[:, sl])
    pooled = _pool_out(jnp.concatenate(parts, axis=-1), wpool_ref, pscale_ref, gpool_ref)

    q = q_ref[...]
    k_all = jnp.concatenate([kp_ref[...], kc_ref[...]], axis=0).astype(BF16)
    v_all = jnp.concatenate([vp_ref[...], vc_ref[...]], axis=0).astype(BF16)
    ones = jnp.ones((WINDOW + BQ, HEAD_DIM), BF16)
    v_ones = [jnp.concatenate([v_all[:, hk * HEAD_DIM:(hk + 1) * HEAD_DIM], ones], axis=1)
              for hk in range(N_KV_HEADS)]
    band, col = _band_mask(GQA_GROUP * QB, QB, QB + WINDOW)
    sinks = [_sink_column(sinks_ref, hk, QB) for hk in range(N_KV_HEADS)]
    bands = []
    for b in range(BQ // QB):
        rows = slice(b * QB, (b + 1) * QB)
        keys = slice(b * QB, (b + 1) * QB + WINDOW)
        mask = band & ((col >= WINDOW - b * QB) | (j > 0)) if b * QB < WINDOW else band
        heads = []
        for hk in range(N_KV_HEADS):
            sl = slice(hk * HEAD_DIM, (hk + 1) * HEAD_DIM)
            s = jnp.where(mask, _nt_dot(_stack_heads(q[rows], hk), k_all[keys, sl]), -jnp.inf)
            m = jnp.maximum(jnp.max(s, axis=-1, keepdims=True), sinks[hk])
            e = jnp.exp(s - m).astype(BF16)
            ov = jnp.dot(e, v_ones[hk][keys], preferred_element_type=F32)
            den = ov[:, HEAD_DIM:HEAD_DIM + 1] + jnp.exp(sinks[hk] - m)
            o = ov[:, :HEAD_DIM] / den
            heads += [o[g * QB:(g + 1) * QB] for g in range(GQA_GROUP)]
        bands.append(jnp.concatenate(heads, axis=-1))
    o_att = jnp.concatenate(bands, axis=0)
    h1_ref[...] = _mixer_tail(o_att, pooled, h_ref[...], gatt_ref, wout_ref)


def _mixer_sample_kernel(sinks_ref, h_ref, q_ref, kn_ref, vn_ref, u_ref, ck_ref, cv_ref, st_ref,
                         wpool_ref, pscale_ref, gatt_ref, gpool_ref, wout_ref, h1_in_ref,
                         h1_ref, ko_ref, vo_ref, po_ref, uext_ref):
    del h1_in_ref
    n_q = GQA_GROUP * DEC_SEQ
    n_keys = 2 * WINDOW
    band, col = _band_mask(n_q, DEC_SEQ, n_keys)
    mask = (band & (col < WINDOW + DEC_SEQ))[None]
    q3, kn3, vn3, u3 = q_ref[...], kn_ref[...], vn_ref[...], u_ref[...]
    ck, cv = ck_ref[...], cv_ref[...]
    ko_ref[:, 0:WINDOW - DEC_SEQ, :] = ck[:, DEC_SEQ:, :]
    ko_ref[:, WINDOW - DEC_SEQ:WINDOW, :] = kn3
    vo_ref[:, 0:WINDOW - DEC_SEQ, :] = cv[:, DEC_SEQ:, :]
    vo_ref[:, WINDOW - DEC_SEQ:WINDOW, :] = vn3
    pad = jnp.zeros((SB, WINDOW - DEC_SEQ, KV_WIDTH), F32)
    k_all = jnp.concatenate([ck, kn3, pad], axis=1).astype(BF16)
    v_all = jnp.concatenate([cv, vn3, pad], axis=1).astype(BF16)
    heads = []
    for hk in range(N_KV_HEADS):
        sl = slice(hk * HEAD_DIM, (hk + 1) * HEAD_DIM)
        qs = jnp.concatenate(
            [q3[:, :, (hk * GQA_GROUP + g) * HEAD_DIM:(hk * GQA_GROUP + g + 1) * HEAD_DIM]
             for g in range(GQA_GROUP)], axis=1).astype(BF16)
        sink = _sink_column(sinks_ref, hk, DEC_SEQ)[None]
        s = jnp.einsum("bqd,bkd->bqk", qs, k_all[:, :, sl], preferred_element_type=F32)
        s = jnp.where(mask, s, -jnp.inf)
        m = jnp.maximum(jnp.max(s, axis=-1, keepdims=True), sink)
        e = jnp.exp(s - m)
        den = jnp.sum(e, axis=-1, keepdims=True) + jnp.exp(sink - m)
        o = jnp.einsum("bqk,bkd->bqd", e.astype(BF16), v_all[:, :, sl], preferred_element_type=F32) / den
        heads += [o[:, g * DEC_SEQ:(g + 1) * DEC_SEQ, :] for g in range(GQA_GROUP)]
    o_att = jnp.concatenate(heads, axis=-1).reshape(SB * DEC_SEQ, ATT_WIDTH)

    uext_ref[:, 1:16, :] = st_ref[...]
    uext_ref[:, 16:16 + DEC_SEQ, :] = u3
    parts = []
    for g, w in enumerate(POOL_WINDOWS):
        sl = slice(g * POOL_GROUP_DIM, (g + 1) * POOL_GROUP_DIM)
        acc = u3[:, :, sl]
        for back in range(1, w):
            acc = acc + uext_ref[:, 16 - back:16 - back + DEC_SEQ, sl]
        parts.append(acc / float(w) - u3[:, :, sl])
    d = jnp.concatenate(parts, axis=-1).reshape(SB * DEC_SEQ, POOL_WIDTH)
    po_ref[...] = uext_ref[:, 16 + DEC_SEQ - POOL_STATE:16 + DEC_SEQ, :]
    pooled = _pool_out(d, wpool_ref, pscale_ref, gpool_ref)
    h1_ref[...] = _mixer_tail(o_att, pooled, h_ref[...], gatt_ref, wout_ref)


def _full_spec(a, n_grid):
    nd = a.ndim
    return pl.BlockSpec(a.shape, lambda *_: (0,) * nd)


def _mixer_prompt(sinks, x2d, q, k, v, u, wts, w_exp):
    nb = SEQ // BQ
    row = lambda w: pl.BlockSpec((BQ, w), lambda b, j: (b * nb + j, 0))
    prev = lambda w: pl.BlockSpec(
        (WINDOW, w), lambda b, j: (jnp.maximum((b * nb + j) * (BQ // WINDOW) - 1, 0), 0))
    uprev = pl.BlockSpec((POOL_HALO, POOL_WIDTH),
                         lambda b, j: (jnp.maximum((b * nb + j) * (BQ // POOL_HALO) - 1, 0), 0))
    smem = pl.BlockSpec(memory_space=pltpu.SMEM)
    per_step = lambda r, c: pl.BlockSpec((EXPERTS_PER_MIX_STEP, r, c), lambda b, j: (b * nb + j, 0, 0))
    return pl.pallas_call(
        _mixer_prompt_kernel,
        grid=(BATCH, nb),
        in_specs=[smem, row(D_MODEL), row(ATT_WIDTH), row(KV_WIDTH), prev(KV_WIDTH), row(KV_WIDTH),
                  prev(KV_WIDTH), row(POOL_WIDTH), uprev] + [_full_spec(w, 2) for w in wts]
                 + [per_step(D_MODEL, D_EXPERT), per_step(D_MODEL, D_EXPERT), per_step(D_EXPERT, D_MODEL)],
        out_specs=[row(D_MODEL), per_step(D_MODEL, 2 * D_EXPERT), per_step(D_EXPERT, D_MODEL)],
        out_shape=[jax.ShapeDtypeStruct((T_ALL, D_MODEL), F32),
                   jax.ShapeDtypeStruct((N_EXPERTS, D_MODEL, 2 * D_EXPERT), BF16),
                   jax.ShapeDtypeStruct((N_EXPERTS, D_EXPERT, D_MODEL), BF16)],
        scratch_shapes=[pltpu.VMEM((POOL_HEAD + POOL_HALO + BQ, POOL_WIDTH), F32)],
        compiler_params=_cparams(("parallel", "parallel")),
        name="mixer_prompt",
    )(sinks, x2d, q, k, k, v, v, u, u, *wts, *w_exp)


def _mixer_sample(sinks, x2d, q, k, v, u, cache_k, cache_v, state, wts, h1_buf):
    rows = SB * DEC_SEQ
    row = lambda w: pl.BlockSpec((rows, w), lambda i: (i, 0))
    bat = lambda a: pl.BlockSpec((SB,) + a.shape[1:], lambda i: (i, 0, 0))
    smem = pl.BlockSpec(memory_space=pltpu.SMEM)
    h1_blocks_before = T_P // rows
    n_in = 9 + len(wts)
    q, k, v, u = (a.reshape(DEC_BATCH, DEC_SEQ, a.shape[-1]) for a in (q, k, v, u))
    return pl.pallas_call(
        _mixer_sample_kernel,
        grid=(DEC_BATCH // SB,),
        in_specs=[smem, row(D_MODEL), bat(q), bat(k), bat(v), bat(u),
                  bat(cache_k), bat(cache_v), bat(state)] + [_full_spec(w, 1) for w in wts]
                 + [pl.BlockSpec(memory_space=pl.ANY)],
        out_specs=[pl.BlockSpec((rows, D_MODEL), lambda i: (h1_blocks_before + i, 0)),
                   bat(cache_k), bat(cache_v), bat(state)],
        out_shape=[jax.ShapeDtypeStruct((T_ALL, D_MODEL), F32),
                   jax.ShapeDtypeStruct(cache_k.shape, F32),
                   jax.ShapeDtypeStruct(cache_v.shape, F32),
                   jax.ShapeDtypeStruct(state.shape, F32)],
        scratch_shapes=[pltpu.VMEM((SB, 16 + DEC_SEQ, POOL_WIDTH), F32)],
        input_output_aliases={n_in: 0},
        compiler_params=_cparams(("parallel",)),
        name="mixer_sample",
    )(sinks, x2d, q, k, v, u, cache_k, cache_v, state, *wts, h1_buf)


def _first_max(vals, iota, n):
    m = jnp.max(vals, axis=0, keepdims=True)
    idx = jnp.min(jnp.where(vals == m, iota, n), axis=0, keepdims=True)
    return m, idx


def _router_kernel(h1_ref, gffn_ref, wrt_ref, bias_ref, xp_ref, idx_ref, wts_ref):
    xn = _rms(h1_ref[...], gffn_ref[...])
    w = wrt_ref[...]
    w_hi, x_hi = w.astype(BF16), xn.astype(BF16)
    w_lo, x_lo = (w - w_hi.astype(F32)).astype(BF16), (xn - x_hi.astype(F32)).astype(BF16)
    logits = _nt_dot(w_hi, x_hi) + (_nt_dot(w_hi, x_lo) + _nt_dot(w_lo, x_hi))
    scores = _sigmoid(logits)
    biased = scores + bias_ref[...]
    n_tok = biased.shape[1]
    neg = -jnp.inf

    iota_g = lax.broadcasted_iota(I32, (GROUP_SIZE, n_tok), 0)
    grp_rows = []
    for g in range(N_EXPERT_GROUPS):
        blk = biased[g * GROUP_SIZE:(g + 1) * GROUP_SIZE, :]
        top1, i1 = _first_max(blk, iota_g, GROUP_SIZE)
        top2 = jnp.max(jnp.where(iota_g == i1, neg, blk), axis=0, keepdims=True)
        grp_rows.append(top1 + top2)
    gs = jnp.concatenate(grp_rows, axis=0)

    iota_n = lax.broadcasted_iota(I32, (N_EXPERT_GROUPS, n_tok), 0)
    gsel = jnp.zeros((N_EXPERT_GROUPS, n_tok), jnp.bool_)
    for _ in range(TOPK_GROUPS):
        _, gi = _first_max(gs, iota_n, N_EXPERT_GROUPS)
        hit = iota_n == gi
        gsel = gsel | hit
        gs = jnp.where(hit, neg, gs)
    emask = jnp.concatenate(
        [jnp.broadcast_to(gsel[g:g + 1, :], (GROUP_SIZE, n_tok)) for g in range(N_EXPERT_GROUPS)], axis=0)
    masked = jnp.where(emask, biased, neg)

    iota_e = lax.broadcasted_iota(I32, (N_EXPERTS, n_tok), 0)
    idx_rows, sel_rows = [], []
    for _ in range(TOP_K):
        _, ei = _first_max(masked, iota_e, N_EXPERTS)
        hit = iota_e == ei
        idx_rows.append(ei)
        sel_rows.append(jnp.sum(jnp.where(hit, scores, 0.0), axis=0, keepdims=True))
        masked = jnp.where(hit, neg, masked)
    sel = jnp.concatenate(sel_rows, axis=0)
    idx_ref[...] = jnp.concatenate(idx_rows, axis=0)
    wts_ref[...] = sel / jnp.sum(sel, axis=0, keepdims=True) * ROUTED_SCALE
    _store_chunks(xp_ref, _pack_bf16_pairs(xn))


def _router(h1, group, g_ffn, w_router_t, bias_col):
    blk0 = group * T_GRP // BM_R
    colblk = pl.BlockSpec((TOP_K, BM_R), lambda i: (0, i))
    ws = [g_ffn, w_router_t, bias_col]
    return pl.pallas_call(
        _router_kernel,
        grid=(T_GRP // BM_R,),
        in_specs=[pl.BlockSpec((BM_R, D_MODEL), lambda i: (blk0 + i, 0))] + [_full_spec(w, 1) for w in ws],
        out_specs=[pl.BlockSpec((BM_R * ROW_CHUNKS, LANES), lambda i: (i, 0)), colblk, colblk],
        out_shape=[jax.ShapeDtypeStruct((T_GRP * ROW_CHUNKS, LANES), U32),
                   jax.ShapeDtypeStruct((TOP_K, T_GRP), I32),
                   jax.ShapeDtypeStruct((TOP_K, T_GRP), F32)],
        compiler_params=_cparams(("parallel",)),
        name="router",
    )(h1, *ws)


def _rank_kernel(idx_ref, tri_ref, rank_ref, cnt_ref, carry_ref):
    @pl.when(pl.program_id(0) == 0)
    def _():
        carry_ref[...] = jnp.zeros_like(carry_ref)

    idx = idx_ref[...]
    n_tok = idx.shape[1]
    iota_e = lax.broadcasted_iota(I32, (N_EXPERTS, n_tok), 0)
    member = jnp.zeros((N_EXPERTS, n_tok), F32)
    for k in range(TOP_K):
        member = member + jnp.where(iota_e == idx[k:k + 1, :], 1.0, 0.0)
    before = jnp.dot(member.astype(BF16), tri_ref[...], preferred_element_type=F32) + carry_ref[...]
    rows = [jnp.sum(jnp.where(iota_e == idx[k:k + 1, :], before, 0.0), axis=0, keepdims=True)
            for k in range(TOP_K)]
    rank_ref[...] = jnp.concatenate(rows, axis=0).astype(I32)
    carry_ref[...] = carry_ref[...] + jnp.sum(member, axis=1, keepdims=True)
    cnt_ref[...] = carry_ref[...].astype(I32)


def _rank(idx_t, tri):
    blk = pl.BlockSpec((TOP_K, BT_RANK), lambda i: (0, i))
    return pl.pallas_call(
        _rank_kernel,
        grid=(T_GRP // BT_RANK,),
        in_specs=[blk, _full_spec(tri, 1)],
        out_specs=[blk, pl.BlockSpec((N_EXPERTS, 1), lambda i: (0, 0))],
        out_shape=[jax.ShapeDtypeStruct((TOP_K, T_GRP), I32),
                   jax.ShapeDtypeStruct((N_EXPERTS, 1), I32)],
        scratch_shapes=[pltpu.VMEM((N_EXPERTS, 1), F32)],
        compiler_params=_cparams(("arbitrary",)),
        name="rank",
    )(idx_t, tri)


def _dest_kernel(idx_ref, rank_ref, cnt_ref, dest_ref, blk_e_ref, n_used_ref, blk_rows_ref):
    counts = cnt_ref[...]
    padded = (counts + (BM_E - 1)) // BM_E * BM_E
    r = lax.broadcasted_iota(I32, (N_EXPERTS, N_EXPERTS), 0)
    c = lax.broadcasted_iota(I32, (N_EXPERTS, N_EXPERTS), 1)
    padded_row = jnp.sum(jnp.where(r == c, padded, 0), axis=0, keepdims=True)
    pad_start = jnp.sum(jnp.where(c < r, padded_row, 0), axis=1, keepdims=True)

    idx = idx_ref[...]
    n_tok = idx.shape[1]
    iota_e = lax.broadcasted_iota(I32, (N_EXPERTS, n_tok), 0)
    rows = [jnp.sum(jnp.where(iota_e == idx[k:k + 1, :], pad_start, 0), axis=0, keepdims=True)
            for k in range(TOP_K)]
    dest_ref[...] = jnp.concatenate(rows, axis=0) + rank_ref[...]

    @pl.when(pl.program_id(0) == 0)
    def _():
        pad_end_row = jnp.sum(jnp.where(r <= c, padded, 0), axis=0, keepdims=True)
        b0 = lax.broadcasted_iota(I32, (N_BLOCKS_PAD, N_EXPERTS), 0) * BM_E
        be = jnp.minimum(jnp.sum(jnp.where(pad_end_row <= b0, 1, 0), axis=1, keepdims=True), N_EXPERTS - 1)
        blk_e_ref[...] = be
        n_used_ref[...] = pad_end_row[:, N_EXPERTS - 1:N_EXPERTS] // BM_E
        counts_row = jnp.sum(jnp.where(r == c, counts, 0), axis=0, keepdims=True)
        mine = lax.broadcasted_iota(I32, (N_BLOCKS_PAD, N_EXPERTS), 1) == be
        end_valid = jnp.sum(jnp.where(mine, pad_end_row - padded_row + counts_row, 0), axis=1, keepdims=True)
        blk_rows_ref[...] = jnp.clip(end_valid - b0[:, :1], 0, BM_E)


N_BLOCKS_PAD = (N_BLOCKS + 7) // 8 * 8


def _dest(idx_t, rank_t, counts):
    blk = pl.BlockSpec((TOP_K, BT_DEST), lambda i: (0, i))
    one = lambda s: pl.BlockSpec(s, lambda i: (0, 0))
    return pl.pallas_call(
        _dest_kernel,
        grid=(T_GRP // BT_DEST,),
        in_specs=[blk, blk, one((N_EXPERTS, 1))],
        out_specs=[blk, one((N_BLOCKS_PAD, 1)), one((1, 1)), one((N_BLOCKS_PAD, 1))],
        out_shape=[jax.ShapeDtypeStruct((TOP_K, T_GRP), I32),
                   jax.ShapeDtypeStruct((N_BLOCKS_PAD, 1), I32),
                   jax.ShapeDtypeStruct((1, 1), I32),
                   jax.ShapeDtypeStruct((N_BLOCKS_PAD, 1), I32)],
        compiler_params=_cparams(("arbitrary",)),
        name="dest",
    )(idx_t, rank_t, counts)


def _sc_mesh():
    return plsc.VectorSubcoreMesh(core_axis_name="c", subcore_axis_name="s")


def _sc_worker_id():
    return lax.axis_index("s") * SC_CORES + lax.axis_index("c")


def _dispatch_body(dest_hbm, xp_hbm, xs_hbm, idx_v, rows_v, sem_in, sem_out):
    n_chunks, _, n_tok = dest_hbm.shape
    per_worker = n_chunks // SC_WORKERS
    chunk0 = _sc_worker_id() * per_worker

    def loads(i):
        chunk = chunk0 + i
        t0 = pl.multiple_of(chunk * n_tok, n_tok)
        return (pltpu.make_async_copy(dest_hbm.at[chunk], idx_v.at[i % 2], sem_in.at[i % 2]),
                pltpu.make_async_copy(xp_hbm.at[pl.ds(t0, n_tok)], rows_v.at[i % 2], sem_in.at[i % 2]))

    def scatters(i):
        return [pltpu.make_async_copy(rows_v.at[i % 2], xs_hbm.at[idx_v.at[i % 2, k]], sem_out.at[i % 2])
                for k in range(TOP_K)]

    for cp in loads(0):
        cp.start()
    for i in range(per_worker):
        for cp in loads(i):
            cp.wait()
        if i >= 1:
            for cp in scatters(i - 1):
                cp.wait()
        if i + 1 < per_worker:
            for cp in loads(i + 1):
                cp.start()
        for cp in scatters(i):
            cp.start()
    for cp in scatters(per_worker - 1):
        cp.wait()


def _dispatch(dest_chunks, xp3):
    return pl.kernel(
        _dispatch_body,
        out_type=jax.ShapeDtypeStruct((N_SLOTS, ROW_CHUNKS, LANES), U32),
        mesh=_sc_mesh(),
        scratch_types=[pltpu.VMEM((2, TOP_K, SC_CHUNK), I32),
                       pltpu.VMEM((2, SC_CHUNK, ROW_CHUNKS, LANES), U32),
                       pltpu.SemaphoreType.DMA((2,)), pltpu.SemaphoreType.DMA((2,))],
        name="dispatch",
    )(dest_chunks, xp3)


def _gather_body(dest_hbm, ys_hbm, yt_hbm, idx_v, rows_v, sem_in, sem_out):
    n_chunks, _, n_tok = dest_hbm.shape
    per_worker = n_chunks // SC_WORKERS
    chunk0 = _sc_worker_id() * per_worker

    @pl.loop(0, per_worker)
    def _(i):
        chunk = chunk0 + i
        t0 = pl.multiple_of(chunk * n_tok, n_tok)
        pltpu.sync_copy(dest_hbm.at[chunk], idx_v)

        def gather(k):
            return pltpu.make_async_copy(ys_hbm.at[idx_v.at[k]], rows_v.at[k % SC_RING], sem_in.at[k % SC_RING])

        def store(k):
            return pltpu.make_async_copy(rows_v.at[k % SC_RING], yt_hbm.at[k, pl.ds(t0, n_tok)],
                                         sem_out.at[k % SC_RING])

        for k in range(SC_RING):
            gather(k).start()
        for k in range(TOP_K):
            gather(k).wait()
            store(k).start()
            if k + SC_RING < TOP_K:
                store(k).wait()
                gather(k + SC_RING).start()
        for k in range(TOP_K - SC_RING, TOP_K):
            store(k).wait()


def _gather(dest_chunks, ys3):
    n_chunks, _, n_tok = dest_chunks.shape
    assert n_chunks % SC_WORKERS == 0
    return pl.kernel(
        _gather_body,
        out_type=jax.ShapeDtypeStruct((TOP_K, n_chunks * n_tok, ROW_CHUNKS, LANES), U32),
        mesh=_sc_mesh(),
        scratch_types=[pltpu.VMEM((TOP_K, n_tok), I32),
                       pltpu.VMEM((SC_RING, n_tok, ROW_CHUNKS, LANES), U32),
                       pltpu.SemaphoreType.DMA((SC_RING,)), pltpu.SemaphoreType.DMA((SC_RING,))],
        name="gather",
    )(dest_chunks, ys3)


def _experts_kernel(blk_e_ref, n_used_ref, blk_rows_ref, xs_ref, wgu_ref, wd_ref, ys_ref):
    del blk_e_ref
    b = pl.program_id(0)

    def swiglu_rows(n_rows):
        chunks = _load_chunks(xs_ref, n_rows)
        x_lo = jnp.concatenate([_unpack_lo(p) for p in chunks], axis=-1).astype(BF16)
        x_hi = jnp.concatenate([_unpack_hi(p) for p in chunks], axis=-1).astype(BF16)
        gu = (jnp.dot(x_lo, wgu_ref[0, :HALF, :], preferred_element_type=F32)
              + jnp.dot(x_hi, wgu_ref[0, HALF:, :], preferred_element_type=F32))
        gate, up = gu[:, :D_EXPERT], gu[:, D_EXPERT:]
        hmid = (gate * _sigmoid(gate) * up).astype(BF16)
        _store_chunks(ys_ref, _pack_bf16_pairs(jnp.dot(hmid, wd_ref[0], preferred_element_type=F32)))

    @pl.when(b < n_used_ref[0])
    def _():
        valid = blk_rows_ref[b]
        for n_rows in range(E_STRIP, BM_E + 1, E_STRIP):
            @pl.when((valid > n_rows - E_STRIP) & (valid <= n_rows))
            def _(n_rows=n_rows):
                swiglu_rows(n_rows)


def _experts(blk_e, n_used, blk_rows, xs, wgu_bf, wd_bf):
    def blk(b, be, nu, nr):
        return jnp.minimum(b, nu[0] - 1)

    def by_expert(shape):
        return pl.BlockSpec((1,) + shape, lambda b, be, nu, nr: (be[blk(b, be, nu, nr)], 0, 0))

    tile = pl.BlockSpec((BM_E * ROW_CHUNKS, LANES), lambda b, be, nu, nr: (blk(b, be, nu, nr), 0))
    grid_spec = pltpu.PrefetchScalarGridSpec(
        num_scalar_prefetch=3,
        grid=(N_BLOCKS,),
        in_specs=[tile, by_expert((D_MODEL, 2 * D_EXPERT)), by_expert((D_EXPERT, D_MODEL))],
        out_specs=tile,
    )
    return pl.pallas_call(
        _experts_kernel,
        grid_spec=grid_spec,
        out_shape=jax.ShapeDtypeStruct((N_SLOTS * ROW_CHUNKS, LANES), U32),
        compiler_params=_cparams(("arbitrary",)),
        name="experts",
    )(blk_e, n_used, blk_rows, xs, wgu_bf, wd_bf)


def _combine_kernel(yt_ref, wts_ref, h1_ref, p_ref, gffn_ref, wsgu_ref, wsd_ref, gple_ref, wpg_ref, wpp_ref,
                    gfin_ref, *y_refs):
    y_ref = y_refs[-1]
    h1 = h1_ref[...]
    gu = jnp.dot(_rms(h1, gffn_ref[...]).astype(BF16), wsgu_ref[...], preferred_element_type=F32)
    sgate, sup = gu[:, :D_SHARED], gu[:, D_SHARED:]
    hsh = h1 + jnp.dot((sgate * _sigmoid(sgate) * sup).astype(BF16), wsd_ref[...], preferred_element_type=F32)
    wts = jnp.transpose(wts_ref[...])
    lo = [jnp.zeros((BT_COMB, LANES), F32) for _ in range(ROW_CHUNKS)]
    hi = [jnp.zeros((BT_COMB, LANES), F32) for _ in range(ROW_CHUNKS)]
    for k in range(TOP_K):
        w = wts[:, k:k + 1]
        for c, p in enumerate(_load_chunks(yt_ref, BT_COMB, lead=(k,))):
            lo[c] = lo[c] + w * _unpack_lo(p)
            hi[c] = hi[c] + w * _unpack_hi(p)
    h2 = hsh + jnp.concatenate(lo + hi, axis=-1)
    gate = _sigmoid(jnp.dot(_rms(h2, gple_ref[...]).astype(BF16), wpg_ref[...], preferred_element_type=F32))
    proj = jnp.dot(p_ref[...].astype(BF16), wpp_ref[...], preferred_element_type=F32)
    y_ref[...] = _rms(h2 + proj * gate, gfin_ref[...])


def _combine(yt, yt_row0, wts_t, wts_row0, h1, tok_row0, n_rows, p2d, p_row0, ws, y_prev, out_rows, out_row0):
    assert all(r % BT_COMB == 0 for r in (yt_row0, wts_row0, tok_row0, n_rows, p_row0, out_row0))
    g0, w0, t0, p0, o0 = (r // BT_COMB for r in (yt_row0, wts_row0, tok_row0, p_row0, out_row0))
    in_specs = [pl.BlockSpec((TOP_K, BT_COMB * ROW_CHUNKS, LANES), lambda i: (0, g0 + i, 0)),
                pl.BlockSpec((TOP_K, BT_COMB), lambda i: (0, w0 + i)),
                pl.BlockSpec((BT_COMB, D_MODEL), lambda i: (t0 + i, 0)),
                pl.BlockSpec((BT_COMB, PLE_DIM), lambda i: (p0 + i, 0))] + [_full_spec(w, 1) for w in ws]
    args = [yt, wts_t, h1, p2d, *ws]
    aliases = {}
    if y_prev is not None:
        in_specs.append(pl.BlockSpec(memory_space=pl.ANY))
        aliases = {len(args): 0}
        args.append(y_prev)
    return pl.pallas_call(
        _combine_kernel,
        grid=(n_rows // BT_COMB,),
        in_specs=in_specs,
        out_specs=pl.BlockSpec((BT_COMB, D_MODEL), lambda i: (o0 + i, 0)),
        out_shape=jax.ShapeDtypeStruct((out_rows, D_MODEL), F32),
        input_output_aliases=aliases,
        compiler_params=_cparams(("parallel",)),
        name="combine",
    )(*args)


def kernel(x_prompt, x_sample, cache_k, cache_v, state_pool, p_prompt, p_sample, g_mix, w_in, attn_sinks,
           w_pool, pool_scale, g_att_out, g_pool_out, w_out, g_ffn, w_router, router_bias, w_exp_gate,
           w_exp_up, w_exp_down, w_sh_gate, w_sh_up, w_sh_down, g_ple, w_ple_gate, w_ple_proj, g_final):
    row = lambda a: a.reshape(1, -1)
    xp2d = x_prompt.reshape(T_P, D_MODEL)
    xs2d = x_sample.reshape(T_S, D_MODEL)
    w_in_bf = w_in[0].astype(BF16)
    mixer_wts = [w_pool[0].astype(BF16), row(pool_scale[0]), row(g_att_out[0]), row(g_pool_out[0]),
                 w_out[0].astype(BF16)]

    tab_p = _rope_tables(np.arange(SEQ))
    tab_s = _rope_tables(PAST_LEN + np.arange(DEC_SEQ), reps=BM_IN // DEC_SEQ)

    q_p, k_p, v_p, u_p = _inproj(xp2d, row(g_mix[0]), w_in_bf, tab_p, BF16)
    q_s, k_s, v_s, u_s = _inproj(xs2d, row(g_mix[0]), w_in_bf, tab_s, F32)

    h1, wgu_bf, wd_bf = _mixer_prompt(attn_sinks[0], xp2d, q_p, k_p, v_p, u_p, mixer_wts,
                                      (w_exp_gate[0], w_exp_up[0], w_exp_down[0]))
    h1, k_sample, v_sample, pool_sample = _mixer_sample(
        attn_sinks[0], xs2d, q_s, k_s, v_s, u_s,
        cache_k[0].reshape(DEC_BATCH, WINDOW, KV_WIDTH), cache_v[0].reshape(DEC_BATCH, WINDOW, KV_WIDTH),
        state_pool[0], mixer_wts, h1)

    g_ffn_row = row(g_ffn[0])
    router_wts = (g_ffn_row, w_router[0].T, router_bias[0].reshape(N_EXPERTS, 1))
    tri = (lax.broadcasted_iota(I32, (BT_RANK, BT_RANK), 0)
           < lax.broadcasted_iota(I32, (BT_RANK, BT_RANK), 1)).astype(BF16)

    def index_chunks(d, n_tok):
        return d.reshape(TOP_K, d.shape[1] // n_tok, n_tok).transpose(1, 0, 2)

    groups = []
    for g in range(N_GROUPS):
        xp, idx_t, wts_t = _router(h1, g, *router_wts)
        rank_t, counts = _rank(idx_t, tri)
        dest_t, *plan = _dest(idx_t, rank_t, counts)
        xs = _dispatch(index_chunks(dest_t, SC_CHUNK), xp.reshape(T_GRP, ROW_CHUNKS, LANES))
        groups.append((wts_t, index_chunks(dest_t, SC_CHUNK_G), xs, [a.reshape(-1) for a in plan]))

    ple_wts = [g_ffn_row, jnp.concatenate([w_sh_gate[0], w_sh_up[0]], axis=1).astype(BF16),
               w_sh_down[0].astype(BF16),
               row(g_ple[0]), w_ple_gate[0].astype(BF16), w_ple_proj[0].astype(BF16), row(g_final)]
    pp2d = p_prompt[0].reshape(T_P, PLE_DIM)
    ps2d = p_sample[0].reshape(T_S, PLE_DIM)
    y_p = y_s = None
    sub_chunks = T_SUB // SC_CHUNK_G
    for g, (wts_t, gather_chunks, xs, plan) in enumerate(groups):
        lo, hi = g * T_GRP, (g + 1) * T_GRP
        ys = _experts(*plan, xs.reshape(N_SLOTS * ROW_CHUNKS, LANES), wgu_bf, wd_bf)
        ys3 = ys.reshape(N_SLOTS, ROW_CHUNKS, LANES)
        for s in range(N_SUB):
            a, b = lo + s * T_SUB, lo + (s + 1) * T_SUB
            yt = _gather(gather_chunks[s * sub_chunks:(s + 1) * sub_chunks], ys3)
            yt = yt.reshape(TOP_K, T_SUB * ROW_CHUNKS, LANES)
            if a < T_P:
                n = min(b, T_P) - a
                y_p = _combine(yt, 0, wts_t, a - lo, h1, a, n, pp2d, a, ple_wts, y_p, T_P, a)
            if b > T_P:
                s0 = max(a, T_P)
                y_s = _combine(yt, s0 - a, wts_t, s0 - lo, h1, s0, b - s0, ps2d, s0 - T_P, ple_wts, y_s, T_S,
                               s0 - T_P)

    kv5 = lambda a, b: a.reshape(1, b, WINDOW, N_KV_HEADS, HEAD_DIM)
    k_prompt = kv5(k_p.reshape(BATCH, SEQ, KV_WIDTH)[:, SEQ - WINDOW:], BATCH)
    v_prompt = kv5(v_p.reshape(BATCH, SEQ, KV_WIDTH)[:, SEQ - WINDOW:], BATCH)
    pool_prompt = u_p.reshape(BATCH, SEQ, POOL_WIDTH)[:, SEQ - POOL_STATE:][None]
    return (y_p.reshape(BATCH, SEQ, D_MODEL), y_s.reshape(DEC_BATCH, DEC_SEQ, D_MODEL),
            k_prompt, v_prompt, pool_prompt,
            kv5(k_sample, DEC_BATCH), kv5(v_sample, DEC_BATCH), pool_sample[None])
```

```python
import functools

import numpy as np
import jax
import jax.numpy as jnp
from jax import lax
from jax.experimental import pallas as pl
from jax.experimental.pallas import tpu as pltpu
from jax.experimental.pallas import tpu_sc as plsc

F32 = jnp.float32
BF16 = jnp.bfloat16
U32 = jnp.uint32
I32 = jnp.int32

D_MODEL = 1024
BATCH = 8
SEQ = 2048
DEC_BATCH = 128
DEC_SEQ = 8
PAST_LEN = 16384
N_Q_HEADS = 8
N_KV_HEADS = 2
HEAD_DIM = 64
GQA_GROUP = N_Q_HEADS // N_KV_HEADS
ATT_WIDTH = N_Q_HEADS * HEAD_DIM
KV_WIDTH = N_KV_HEADS * HEAD_DIM
WINDOW = 128
ROPE_THETA = 500000.0
ROT_DIM = HEAD_DIM // 4
POOL_WINDOWS = (2, 4, 8, 16)
POOL_GROUPS = 4
POOL_WIDTH = D_MODEL - ATT_WIDTH
POOL_GROUP_DIM = POOL_WIDTH // POOL_GROUPS
POOL_STATE = 15
IN_WIDTH = ATT_WIDTH + 2 * KV_WIDTH + POOL_WIDTH
N_EXPERTS = 64
TOP_K = 8
N_EXPERT_GROUPS = 8
GROUP_SIZE = N_EXPERTS // N_EXPERT_GROUPS
TOPK_GROUPS = 4
D_EXPERT = 256
D_SHARED = 256
ROUTED_SCALE = 2.5
PLE_DIM = 256
EPS = 1e-6

T_P = BATCH * SEQ
T_S = DEC_BATCH * DEC_SEQ
T_ALL = T_P + T_S
HALF = D_MODEL // 2
LANES = 128
VMEM_LIMIT = 48 * 1024 * 1024

BM_IN = 1024
BQ = 2 * WINDOW
EXPERTS_PER_MIX_STEP = N_EXPERTS * BQ // T_P
assert EXPERTS_PER_MIX_STEP * T_P == N_EXPERTS * BQ
SB = 16
BM_R = 512
BT_RANK = 512
BT_DEST = 2176
BT_COMB = 256
N_GROUPS = 2
T_GRP = T_ALL // N_GROUPS
N_SUB = 2
T_SUB = T_GRP // N_SUB
assert T_GRP * N_GROUPS == T_ALL and T_SUB * N_SUB == T_GRP
BM_E = 1280
E_STRIP = 256
assert BM_E % E_STRIP == 0
N_ASSIGN = T_GRP * TOP_K
N_BLOCKS = -(-N_ASSIGN // BM_E) + N_EXPERTS
N_SLOTS = N_BLOCKS * BM_E

ROW_CHUNKS = HALF // LANES
SC_CORES = 2
SC_SUBCORES = 16
SC_WORKERS = SC_CORES * SC_SUBCORES
SC_CHUNK = 16
SC_CHUNK_G = 8
SC_RING = 4
assert T_GRP % (SC_WORKERS * SC_CHUNK) == 0 and T_SUB % (SC_WORKERS * SC_CHUNK_G) == 0


def _load_chunks(ref, n_rows, lead=()):
    return [ref[lead + (pl.ds(c, n_rows, stride=ROW_CHUNKS), slice(None))] for c in range(ROW_CHUNKS)]


def _store_chunks(ref, packed):
    n_rows = packed.shape[0]
    for c in range(ROW_CHUNKS):
        ref[pl.ds(c, n_rows, stride=ROW_CHUNKS), :] = packed[:, c * LANES:(c + 1) * LANES]


def _cparams(sem):
    return pltpu.CompilerParams(dimension_semantics=sem, vmem_limit_bytes=VMEM_LIMIT)


def _rms(x, g):
    return x * lax.rsqrt(jnp.mean(x * x, axis=-1, keepdims=True) + EPS) * g


def _sigmoid(x):
    return 1.0 / (1.0 + jnp.exp(-x))


def _pack_bf16_pairs(x):
    h = x.shape[-1] // 2
    return pltpu.pack_elementwise([x[:, :h], x[:, h:]], packed_dtype=BF16)


def _unpack_lo(p):
    return pltpu.bitcast(p << 16, F32)


def _unpack_hi(p):
    return pltpu.bitcast(p & jnp.uint32(0xFFFF0000), F32)


def _inproj_kernel(x_ref, g_ref, w_ref, c_ref, s1_ref, s2_ref, q_ref, k_ref, v_ref, u_ref):
    xn = _rms(x_ref[...], g_ref[...]).astype(BF16)
    z = jnp.dot(xn, w_ref[...], preferred_element_type=F32)
    c, s1, s2 = c_ref[...], s1_ref[...], s2_ref[...]

    def rope(t):
        return t * c + pltpu.roll(t, LANES - ROT_DIM // 2, 1) * s1 + pltpu.roll(t, ROT_DIM // 2, 1) * s2

    for i in range(ATT_WIDTH // LANES):
        sl = slice(i * LANES, (i + 1) * LANES)
        q_ref[:, sl] = (rope(z[:, sl]) * (HEAD_DIM ** -0.5)).astype(q_ref.dtype)
    k_ref[...] = rope(z[:, ATT_WIDTH:ATT_WIDTH + KV_WIDTH])
    v_ref[...] = z[:, ATT_WIDTH + KV_WIDTH:ATT_WIDTH + 2 * KV_WIDTH]
    u_ref[...] = z[:, ATT_WIDTH + 2 * KV_WIDTH:]


def _rope_tables(pos, reps=1):
    f32 = np.float32
    half = ROT_DIM // 2
    inv = np.power(f32(ROPE_THETA), -np.arange(half, dtype=f32) * f32(2.0) / f32(ROT_DIM)).astype(f32)
    ang = np.asarray(pos, f32)[:, None] * inv[None, :]
    cos, sin = np.cos(ang).astype(f32), np.sin(ang).astype(f32)
    n = len(pos)
    ones = np.ones((n, HEAD_DIM - ROT_DIM), f32)
    zeros = np.zeros((n, HEAD_DIM - ROT_DIM), f32)
    zh = np.zeros((n, half), f32)
    c = np.concatenate([cos, cos, ones], axis=1)
    s1 = np.concatenate([-sin, zh, zeros], axis=1)
    s2 = np.concatenate([zh, sin, zeros], axis=1)
    tile = lambda a: np.tile(a, (reps, LANES // HEAD_DIM))
    return tile(c), tile(s1), tile(s2)


def _inproj(x2d, g_mix, w_in_bf, tables, q_dtype):
    rows = x2d.shape[0]
    n_tab = tables[0].shape[0] // BM_IN
    row_spec = lambda w: pl.BlockSpec((BM_IN, w), lambda i: (i, 0))
    tab_spec = pl.BlockSpec((BM_IN, LANES), lambda i: (i % n_tab, 0))
    full = lambda a: pl.BlockSpec(a.shape, lambda i: (0,) * a.ndim)
    return pl.pallas_call(
        _inproj_kernel,
        grid=(rows // BM_IN,),
        in_specs=[row_spec(D_MODEL), full(g_mix), full(w_in_bf), tab_spec, tab_spec, tab_spec],
        out_specs=[row_spec(ATT_WIDTH), row_spec(KV_WIDTH), row_spec(KV_WIDTH), row_spec(POOL_WIDTH)],
        out_shape=[jax.ShapeDtypeStruct((rows, ATT_WIDTH), q_dtype),
                   jax.ShapeDtypeStruct((rows, KV_WIDTH), F32),
                   jax.ShapeDtypeStruct((rows, KV_WIDTH), F32),
                   jax.ShapeDtypeStruct((rows, POOL_WIDTH), F32)],
        compiler_params=_cparams(("parallel",)),
        name="inproj",
    )(x2d, g_mix, w_in_bf, *tables)


def _sink_column(sinks_ref, kv_head, rows_per_head):
    n = GQA_GROUP * rows_per_head
    grp = lax.broadcasted_iota(I32, (n, 1), 0) // rows_per_head
    col = jnp.full((n, 1), sinks_ref[kv_head * GQA_GROUP], F32)
    for g in range(1, GQA_GROUP):
        col = jnp.where(grp == g, sinks_ref[kv_head * GQA_GROUP + g], col)
    return col


def _band_mask(n_rows, rows_per_head, n_keys):
    i = lax.broadcasted_iota(I32, (n_rows, n_keys), 0) % rows_per_head
    c = lax.broadcasted_iota(I32, (n_rows, n_keys), 1)
    return (c >= i) & (c <= i + WINDOW), c


def _stack_heads(q, kv_head):
    return jnp.concatenate(
        [q[:, (kv_head * GQA_GROUP + g) * HEAD_DIM:(kv_head * GQA_GROUP + g + 1) * HEAD_DIM]
         for g in range(GQA_GROUP)], axis=0)


def _nt_dot(a, b):
    return lax.dot_general(a, b, (((1,), (1,)), ((), ())), preferred_element_type=F32)


POOL_HALO = 16
POOL_HEAD = 8
assert all(w == 2 << g for g, w in enumerate(POOL_WINDOWS)) and POOL_WINDOWS[-1] - 1 <= POOL_HALO


def _window_sums(ext_ref, n):
    lo, hi = POOL_HEAD, POOL_HEAD + POOL_HALO + n
    for p in range(POOL_GROUPS):
        lanes = slice(p * POOL_GROUP_DIM, POOL_WIDTH)
        ext_ref[lo:hi, lanes] = ext_ref[lo:hi, lanes] + ext_ref[lo - (1 << p):hi - (1 << p), lanes]


def _pool_out(d, wpool_ref, pscale_ref, gpool_ref):
    parts = [jnp.dot(d[:, g * POOL_GROUP_DIM:(g + 1) * POOL_GROUP_DIM].astype(BF16), wpool_ref[g],
                     preferred_element_type=F32) for g in range(POOL_GROUPS)]
    return _rms(jnp.concatenate(parts, axis=-1) * pscale_ref[...], gpool_ref[...])


def _mixer_tail(o_att, pooled, h, gatt_ref, wout_ref):
    mixed = jnp.concatenate([_rms(o_att, gatt_ref[...]), pooled], axis=-1)
    return h + jnp.dot(mixed.astype(BF16), wout_ref[...], preferred_element_type=F32)


def _mixer_prompt_kernel(sinks_ref, h_ref, q_ref, kc_ref, kp_ref, vc_ref, vp_ref, uc_ref, up_ref,
                         wpool_ref, pscale_ref, gatt_ref, gpool_ref, wout_ref, weg_ref, weu_ref, wed_ref,
                         h1_ref, wgu_bf_ref, wd_bf_ref, uext_ref):
    j = pl.program_id(1)
    wgu_bf_ref[:, :, :D_EXPERT] = weg_ref[...].astype(BF16)
    wgu_bf_ref[:, :, D_EXPERT:] = weu_ref[...].astype(BF16)
    wd_bf_ref[...] = wed_ref[...].astype(BF16)

    u = uc_ref[...]
    base = POOL_HEAD + POOL_HALO
    uext_ref[0:POOL_HEAD, :] = jnp.zeros((POOL_HEAD, POOL_WIDTH), F32)
    uext_ref[POOL_HEAD:base, :] = jnp.where(j > 0, up_ref[...], 0.0)
    uext_ref[base:base + BQ, :] = u
    _window_sums(uext_ref, BQ)
    pos = j * BQ + lax.broadcasted_iota(I32, (BQ, 1), 0)
    parts = []
    for g, w in enumerate(POOL_WINDOWS):
        sl = slice(g * POOL_GROUP_DIM, (g + 1) * POOL_GROUP_DIM)
        parts.append(uext_ref[base:base + BQ, sl] / jnp.minimum(pos + 1, w).astype(F32) - u[:, sl])
    pooled = _pool_out(jnp.concatenate(parts, axis=-1), wpool_ref, pscale_ref, gpool_ref)

    q = q_ref[...]
    k_all = jnp.concatenate([kp_ref[...], kc_ref[...]], axis=0).astype(BF16)
    v_all = jnp.concatenate([vp_ref[...], vc_ref[...]], axis=0).astype(BF16)
    ones = jnp.ones((WINDOW + BQ, HEAD_DIM), BF16)
    v_ones = [jnp.concatenate([v_all[:, hk * HEAD_DIM:(hk + 1) * HEAD_DIM], ones], axis=1)
              for hk in range(N_KV_HEADS)]
    band, col = _band_mask(GQA_GROUP * WINDOW, WINDOW, 2 * WINDOW)
    sinks = [_sink_column(sinks_ref, hk, WINDOW) for hk in range(N_KV_HEADS)]
    bands = []
    for b in range(BQ // WINDOW):
        rows = slice(b * WINDOW, (b + 1) * WINDOW)
        keys = slice(b * WINDOW, (b + 2) * WINDOW)
        mask = band & ((col >= WINDOW) | (j > 0)) if b == 0 else band
        heads = []
        for hk in range(N_KV_HEADS):
            sl = slice(hk * HEAD_DIM, (hk + 1) * HEAD_DIM)
            s = jnp.where(mask, _nt_dot(_stack_heads(q[rows], hk), k_all[keys, sl]), -jnp.inf)
            m = jnp.maximum(jnp.max(s, axis=-1, keepdims=True), sinks[hk])
            e = jnp.exp(s - m).astype(BF16)
            ov = jnp.dot(e, v_ones[hk][keys], preferred_element_type=F32)
            den = ov[:, HEAD_DIM:HEAD_DIM + 1] + jnp.exp(sinks[hk] - m)
            o = ov[:, :HEAD_DIM] / den
            heads += [o[g * WINDOW:(g + 1) * WINDOW] for g in range(GQA_GROUP)]
        bands.append(jnp.concatenate(heads, axis=-1))
    o_att = jnp.concatenate(bands, axis=0)
    h1_ref[...] = _mixer_tail(o_att, pooled, h_ref[...], gatt_ref, wout_ref)


def _mixer_sample_kernel(sinks_ref, h_ref, q_ref, kn_ref, vn_ref, u_ref, ck_ref, cv_ref, st_ref,
                         wpool_ref, pscale_ref, gatt_ref, gpool_ref, wout_ref, h1_in_ref,
                         h1_ref, ko_ref, vo_ref, po_ref, uext_ref):
    del h1_in_ref
    n_q = GQA_GROUP * DEC_SEQ
    n_keys = 2 * WINDOW
    band, col = _band_mask(n_q, DEC_SEQ, n_keys)
    mask = (band & (col < WINDOW + DEC_SEQ))[None]
    q3, kn3, vn3, u3 = q_ref[...], kn_ref[...], vn_ref[...], u_ref[...]
    ck, cv = ck_ref[...], cv_ref[...]
    ko_ref[:, 0:WINDOW - DEC_SEQ, :] = ck[:, DEC_SEQ:, :]
    ko_ref[:, WINDOW - DEC_SEQ:WINDOW, :] = kn3
    vo_ref[:, 0:WINDOW - DEC_SEQ, :] = cv[:, DEC_SEQ:, :]
    vo_ref[:, WINDOW - DEC_SEQ:WINDOW, :] = vn3
    pad = jnp.zeros((SB, WINDOW - DEC_SEQ, KV_WIDTH), F32)
    k_all = jnp.concatenate([ck, kn3, pad], axis=1).astype(BF16)
    v_all = jnp.concatenate([cv, vn3, pad], axis=1).astype(BF16)
    heads = []
    for hk in range(N_KV_HEADS):
        sl = slice(hk * HEAD_DIM, (hk + 1) * HEAD_DIM)
        qs = jnp.concatenate(
            [q3[:, :, (hk * GQA_GROUP + g) * HEAD_DIM:(hk * GQA_GROUP + g + 1) * HEAD_DIM]
             for g in range(GQA_GROUP)], axis=1).astype(BF16)
        sink = _sink_column(sinks_ref, hk, DEC_SEQ)[None]
        s = jnp.einsum("bqd,bkd->bqk", qs, k_all[:, :, sl], preferred_element_type=F32)
        s = jnp.where(mask, s, -jnp.inf)
        m = jnp.maximum(jnp.max(s, axis=-1, keepdims=True), sink)
        e = jnp.exp(s - m)
        den = jnp.sum(e, axis=-1, keepdims=True) + jnp.exp(sink - m)
        o = jnp.einsum("bqk,bkd->bqd", e.astype(BF16), v_all[:, :, sl], preferred_element_type=F32) / den
        heads += [o[:, g * DEC_SEQ:(g + 1) * DEC_SEQ, :] for g in range(GQA_GROUP)]
    o_att = jnp.concatenate(heads, axis=-1).reshape(SB * DEC_SEQ, ATT_WIDTH)

    uext_ref[:, 1:16, :] = st_ref[...]
    uext_ref[:, 16:16 + DEC_SEQ, :] = u3
    parts = []
    for g, w in enumerate(POOL_WINDOWS):
        sl = slice(g * POOL_GROUP_DIM, (g + 1) * POOL_GROUP_DIM)
        acc = u3[:, :, sl]
        for back in range(1, w):
            acc = acc + uext_ref[:, 16 - back:16 - back + DEC_SEQ, sl]
        parts.append(acc / float(w) - u3[:, :, sl])
    d = jnp.concatenate(parts, axis=-1).reshape(SB * DEC_SEQ, POOL_WIDTH)
    po_ref[...] = uext_ref[:, 16 + DEC_SEQ - POOL_STATE:16 + DEC_SEQ, :]
    pooled = _pool_out(d, wpool_ref, pscale_ref, gpool_ref)
    h1_ref[...] = _mixer_tail(o_att, pooled, h_ref[...], gatt_ref, wout_ref)


def _full_spec(a, n_grid):
    nd = a.ndim
    return pl.BlockSpec(a.shape, lambda *_: (0,) * nd)


def _mixer_prompt(sinks, x2d, q, k, v, u, wts, w_exp):
    nb = SEQ // BQ
    row = lambda w: pl.BlockSpec((BQ, w), lambda b, j: (b * nb + j, 0))
    prev = lambda w: pl.BlockSpec(
        (WINDOW, w), lambda b, j: (jnp.maximum((b * nb + j) * (BQ // WINDOW) - 1, 0), 0))
    uprev = pl.BlockSpec((POOL_HALO, POOL_WIDTH),
                         lambda b, j: (jnp.maximum((b * nb + j) * (BQ // POOL_HALO) - 1, 0), 0))
    smem = pl.BlockSpec(memory_space=pltpu.SMEM)
    per_step = lambda r, c: pl.BlockSpec((EXPERTS_PER_MIX_STEP, r, c), lambda b, j: (b * nb + j, 0, 0))
    return pl.pallas_call(
        _mixer_prompt_kernel,
        grid=(BATCH, nb),
        in_specs=[smem, row(D_MODEL), row(ATT_WIDTH), row(KV_WIDTH), prev(KV_WIDTH), row(KV_WIDTH),
                  prev(KV_WIDTH), row(POOL_WIDTH), uprev] + [_full_spec(w, 2) for w in wts]
                 + [per_step(D_MODEL, D_EXPERT), per_step(D_MODEL, D_EXPERT), per_step(D_EXPERT, D_MODEL)],
        out_specs=[row(D_MODEL), per_step(D_MODEL, 2 * D_EXPERT), per_step(D_EXPERT, D_MODEL)],
        out_shape=[jax.ShapeDtypeStruct((T_ALL, D_MODEL), F32),
                   jax.ShapeDtypeStruct((N_EXPERTS, D_MODEL, 2 * D_EXPERT), BF16),
                   jax.ShapeDtypeStruct((N_EXPERTS, D_EXPERT, D_MODEL), BF16)],
        scratch_shapes=[pltpu.VMEM((POOL_HEAD + POOL_HALO + BQ, POOL_WIDTH), F32)],
        compiler_params=_cparams(("parallel", "parallel")),
        name="mixer_prompt",
    )(sinks, x2d, q, k, k, v, v, u, u, *wts, *w_exp)


def _mixer_sample(sinks, x2d, q, k, v, u, cache_k, cache_v, state, wts, h1_buf):
    rows = SB * DEC_SEQ
    row = lambda w: pl.BlockSpec((rows, w), lambda i: (i, 0))
    bat = lambda a: pl.BlockSpec((SB,) + a.shape[1:], lambda i: (i, 0, 0))
    smem = pl.BlockSpec(memory_space=pltpu.SMEM)
    h1_blocks_before = T_P // rows
    n_in = 9 + len(wts)
    q, k, v, u = (a.reshape(DEC_BATCH, DEC_SEQ, a.shape[-1]) for a in (q, k, v, u))
    return pl.pallas_call(
        _mixer_sample_kernel,
        grid=(DEC_BATCH // SB,),
        in_specs=[smem, row(D_MODEL), bat(q), bat(k), bat(v), bat(u),
                  bat(cache_k), bat(cache_v), bat(state)] + [_full_spec(w, 1) for w in wts]
                 + [pl.BlockSpec(memory_space=pl.ANY)],
        out_specs=[pl.BlockSpec((rows, D_MODEL), lambda i: (h1_blocks_before + i, 0)),
                   bat(cache_k), bat(cache_v), bat(state)],
        out_shape=[jax.ShapeDtypeStruct((T_ALL, D_MODEL), F32),
                   jax.ShapeDtypeStruct(cache_k.shape, F32),
                   jax.ShapeDtypeStruct(cache_v.shape, F32),
                   jax.ShapeDtypeStruct(state.shape, F32)],
        scratch_shapes=[pltpu.VMEM((SB, 16 + DEC_SEQ, POOL_WIDTH), F32)],
        input_output_aliases={n_in: 0},
        compiler_params=_cparams(("parallel",)),
        name="mixer_sample",
    )(sinks, x2d, q, k, v, u, cache_k, cache_v, state, *wts, h1_buf)


def _first_max(vals, iota, n):
    m = jnp.max(vals, axis=0, keepdims=True)
    idx = jnp.min(jnp.where(vals == m, iota, n), axis=0, keepdims=True)
    return m, idx


def _router_kernel(h1_ref, gffn_ref, wrt_ref, bias_ref, xp_ref, idx_ref, wts_ref):
    xn = _rms(h1_ref[...], gffn_ref[...])
    w = wrt_ref[...]
    w_hi, x_hi = w.astype(BF16), xn.astype(BF16)
    w_lo, x_lo = (w - w_hi.astype(F32)).astype(BF16), (xn - x_hi.astype(F32)).astype(BF16)
    logits = _nt_dot(w_hi, x_hi) + (_nt_dot(w_hi, x_lo) + _nt_dot(w_lo, x_hi))
    scores = _sigmoid(logits)
    biased = scores + bias_ref[...]
    n_tok = biased.shape[1]
    neg = -jnp.inf

    iota_g = lax.broadcasted_iota(I32, (GROUP_SIZE, n_tok), 0)
    grp_rows = []
    for g in range(N_EXPERT_GROUPS):
        blk = biased[g * GROUP_SIZE:(g + 1) * GROUP_SIZE, :]
        top1, i1 = _first_max(blk, iota_g, GROUP_SIZE)
        top2 = jnp.max(jnp.where(iota_g == i1, neg, blk), axis=0, keepdims=True)
        grp_rows.append(top1 + top2)
    gs = jnp.concatenate(grp_rows, axis=0)

    iota_n = lax.broadcasted_iota(I32, (N_EXPERT_GROUPS, n_tok), 0)
    gsel = jnp.zeros((N_EXPERT_GROUPS, n_tok), jnp.bool_)
    for _ in range(TOPK_GROUPS):
        _, gi = _first_max(gs, iota_n, N_EXPERT_GROUPS)
        hit = iota_n == gi
        gsel = gsel | hit
        gs = jnp.where(hit, neg, gs)
    emask = jnp.concatenate(
        [jnp.broadcast_to(gsel[g:g + 1, :], (GROUP_SIZE, n_tok)) for g in range(N_EXPERT_GROUPS)], axis=0)
    masked = jnp.where(emask, biased, neg)

    iota_e = lax.broadcasted_iota(I32, (N_EXPERTS, n_tok), 0)
    idx_rows, sel_rows = [], []
    for _ in range(TOP_K):
        _, ei = _first_max(masked, iota_e, N_EXPERTS)
        hit = iota_e == ei
        idx_rows.append(ei)
        sel_rows.append(jnp.sum(jnp.where(hit, scores, 0.0), axis=0, keepdims=True))
        masked = jnp.where(hit, neg, masked)
    sel = jnp.concatenate(sel_rows, axis=0)
    idx_ref[...] = jnp.concatenate(idx_rows, axis=0)
    wts_ref[...] = sel / jnp.sum(sel, axis=0, keepdims=True) * ROUTED_SCALE
    _store_chunks(xp_ref, _pack_bf16_pairs(xn))


def _router(h1, group, g_ffn, w_router_t, bias_col):
    blk0 = group * T_GRP // BM_R
    colblk = pl.BlockSpec((TOP_K, BM_R), lambda i: (0, i))
    ws = [g_ffn, w_router_t, bias_col]
    return pl.pallas_call(
        _router_kernel,
        grid=(T_GRP // BM_R,),
        in_specs=[pl.BlockSpec((BM_R, D_MODEL), lambda i: (blk0 + i, 0))] + [_full_spec(w, 1) for w in ws],
        out_specs=[pl.BlockSpec((BM_R * ROW_CHUNKS, LANES), lambda i: (i, 0)), colblk, colblk],
        out_shape=[jax.ShapeDtypeStruct((T_GRP * ROW_CHUNKS, LANES), U32),
                   jax.ShapeDtypeStruct((TOP_K, T_GRP), I32),
                   jax.ShapeDtypeStruct((TOP_K, T_GRP), F32)],
        compiler_params=_cparams(("parallel",)),
        name="router",
    )(h1, *ws)


def _rank_kernel(idx_ref, tri_ref, rank_ref, cnt_ref, carry_ref):
    @pl.when(pl.program_id(0) == 0)
    def _():
        carry_ref[...] = jnp.zeros_like(carry_ref)

    idx = idx_ref[...]
    n_tok = idx.shape[1]
    iota_e = lax.broadcasted_iota(I32, (N_EXPERTS, n_tok), 0)
    member = jnp.zeros((N_EXPERTS, n_tok), F32)
    for k in range(TOP_K):
        member = member + jnp.where(iota_e == idx[k:k + 1, :], 1.0, 0.0)
    before = jnp.dot(member.astype(BF16), tri_ref[...], preferred_element_type=F32) + carry_ref[...]
    rows = [jnp.sum(jnp.where(iota_e == idx[k:k + 1, :], before, 0.0), axis=0, keepdims=True)
            for k in range(TOP_K)]
    rank_ref[...] = jnp.concatenate(rows, axis=0).astype(I32)
    carry_ref[...] = carry_ref[...] + jnp.sum(member, axis=1, keepdims=True)
    cnt_ref[...] = carry_ref[...].astype(I32)


def _rank(idx_t, tri):
    blk = pl.BlockSpec((TOP_K, BT_RANK), lambda i: (0, i))
    return pl.pallas_call(
        _rank_kernel,
        grid=(T_GRP // BT_RANK,),
        in_specs=[blk, _full_spec(tri, 1)],
        out_specs=[blk, pl.BlockSpec((N_EXPERTS, 1), lambda i: (0, 0))],
        out_shape=[jax.ShapeDtypeStruct((TOP_K, T_GRP), I32),
                   jax.ShapeDtypeStruct((N_EXPERTS, 1), I32)],
        scratch_shapes=[pltpu.VMEM((N_EXPERTS, 1), F32)],
        compiler_params=_cparams(("arbitrary",)),
        name="rank",
    )(idx_t, tri)


def _dest_kernel(idx_ref, rank_ref, cnt_ref, dest_ref, blk_e_ref, n_used_ref, blk_rows_ref):
    counts = cnt_ref[...]
    padded = (counts + (BM_E - 1)) // BM_E * BM_E
    r = lax.broadcasted_iota(I32, (N_EXPERTS, N_EXPERTS), 0)
    c = lax.broadcasted_iota(I32, (N_EXPERTS, N_EXPERTS), 1)
    padded_row = jnp.sum(jnp.where(r == c, padded, 0), axis=0, keepdims=True)
    pad_start = jnp.sum(jnp.where(c < r, padded_row, 0), axis=1, keepdims=True)

    idx = idx_ref[...]
    n_tok = idx.shape[1]
    iota_e = lax.broadcasted_iota(I32, (N_EXPERTS, n_tok), 0)
    rows = [jnp.sum(jnp.where(iota_e == idx[k:k + 1, :], pad_start, 0), axis=0, keepdims=True)
            for k in range(TOP_K)]
    dest_ref[...] = jnp.concatenate(rows, axis=0) + rank_ref[...]

    @pl.when(pl.program_id(0) == 0)
    def _():
        pad_end_row = jnp.sum(jnp.where(r <= c, padded, 0), axis=0, keepdims=True)
        b0 = lax.broadcasted_iota(I32, (N_BLOCKS_PAD, N_EXPERTS), 0) * BM_E
        be = jnp.minimum(jnp.sum(jnp.where(pad_end_row <= b0, 1, 0), axis=1, keepdims=True), N_EXPERTS - 1)
        blk_e_ref[...] = be
        n_used_ref[...] = pad_end_row[:, N_EXPERTS - 1:N_EXPERTS] // BM_E
        counts_row = jnp.sum(jnp.where(r == c, counts, 0), axis=0, keepdims=True)
        mine = lax.broadcasted_iota(I32, (N_BLOCKS_PAD, N_EXPERTS), 1) == be
        end_valid = jnp.sum(jnp.where(mine, pad_end_row - padded_row + counts_row, 0), axis=1, keepdims=True)
        blk_rows_ref[...] = jnp.clip(end_valid - b0[:, :1], 0, BM_E)


N_BLOCKS_PAD = (N_BLOCKS + 7) // 8 * 8


def _dest(idx_t, rank_t, counts):
    blk = pl.BlockSpec((TOP_K, BT_DEST), lambda i: (0, i))
    one = lambda s: pl.BlockSpec(s, lambda i: (0, 0))
    return pl.pallas_call(
        _dest_kernel,
        grid=(T_GRP // BT_DEST,),
        in_specs=[blk, blk, one((N_EXPERTS, 1))],
        out_specs=[blk, one((N_BLOCKS_PAD, 1)), one((1, 1)), one((N_BLOCKS_PAD, 1))],
        out_shape=[jax.ShapeDtypeStruct((TOP_K, T_GRP), I32),
                   jax.ShapeDtypeStruct((N_BLOCKS_PAD, 1), I32),
                   jax.ShapeDtypeStruct((1, 1), I32),
                   jax.ShapeDtypeStruct((N_BLOCKS_PAD, 1), I32)],
        compiler_params=_cparams(("arbitrary",)),
        name="dest",
    )(idx_t, rank_t, counts)


def _sc_mesh():
    return plsc.VectorSubcoreMesh(core_axis_name="c", subcore_axis_name="s")


def _sc_worker_id():
    return lax.axis_index("s") * SC_CORES + lax.axis_index("c")


def _dispatch_body(dest_hbm, xp_hbm, xs_hbm, idx_v, rows_v, sem_in, sem_out):
    n_chunks, _, n_tok = dest_hbm.shape
    per_worker = n_chunks // SC_WORKERS
    chunk0 = _sc_worker_id() * per_worker

    def loads(i):
        chunk = chunk0 + i
        t0 = pl.multiple_of(chunk * n_tok, n_tok)
        return (pltpu.make_async_copy(dest_hbm.at[chunk], idx_v.at[i % 2], sem_in.at[i % 2]),
                pltpu.make_async_copy(xp_hbm.at[pl.ds(t0, n_tok)], rows_v.at[i % 2], sem_in.at[i % 2]))

    def scatters(i):
        return [pltpu.make_async_copy(rows_v.at[i % 2], xs_hbm.at[idx_v.at[i % 2, k]], sem_out.at[i % 2])
                for k in range(TOP_K)]

    for cp in loads(0):
        cp.start()
    for i in range(per_worker):
        for cp in loads(i):
            cp.wait()
        if i >= 1:
            for cp in scatters(i - 1):
                cp.wait()
        if i + 1 < per_worker:
            for cp in loads(i + 1):
                cp.start()
        for cp in scatters(i):
            cp.start()
    for cp in scatters(per_worker - 1):
        cp.wait()


def _dispatch(dest_chunks, xp3):
    return pl.kernel(
        _dispatch_body,
        out_type=jax.ShapeDtypeStruct((N_SLOTS, ROW_CHUNKS, LANES), U32),
        mesh=_sc_mesh(),
        scratch_types=[pltpu.VMEM((2, TOP_K, SC_CHUNK), I32),
                       pltpu.VMEM((2, SC_CHUNK, ROW_CHUNKS, LANES), U32),
                       pltpu.SemaphoreType.DMA((2,)), pltpu.SemaphoreType.DMA((2,))],
        name="dispatch",
    )(dest_chunks, xp3)


def _gather_body(dest_hbm, ys_hbm, yt_hbm, idx_v, rows_v, sem_in, sem_out):
    n_chunks, _, n_tok = dest_hbm.shape
    per_worker = n_chunks // SC_WORKERS
    chunk0 = _sc_worker_id() * per_worker

    @pl.loop(0, per_worker)
    def _(i):
        chunk = chunk0 + i
        t0 = pl.multiple_of(chunk * n_tok, n_tok)
        pltpu.sync_copy(dest_hbm.at[chunk], idx_v)

        def gather(k):
            return pltpu.make_async_copy(ys_hbm.at[idx_v.at[k]], rows_v.at[k % SC_RING], sem_in.at[k % SC_RING])

        def store(k):
            return pltpu.make_async_copy(rows_v.at[k % SC_RING], yt_hbm.at[k, pl.ds(t0, n_tok)],
                                         sem_out.at[k % SC_RING])

        for k in range(SC_RING):
            gather(k).start()
        for k in range(TOP_K):
            gather(k).wait()
            store(k).start()
            if k + SC_RING < TOP_K:
                store(k).wait()
                gather(k + SC_RING).start()
        for k in range(TOP_K - SC_RING, TOP_K):
            store(k).wait()


def _gather(dest_chunks, ys3):
    n_chunks, _, n_tok = dest_chunks.shape
    assert n_chunks % SC_WORKERS == 0
    return pl.kernel(
        _gather_body,
        out_type=jax.ShapeDtypeStruct((TOP_K, n_chunks * n_tok, ROW_CHUNKS, LANES), U32),
        mesh=_sc_mesh(),
        scratch_types=[pltpu.VMEM((TOP_K, n_tok), I32),
                       pltpu.VMEM((SC_RING, n_tok, ROW_CHUNKS, LANES), U32),
                       pltpu.SemaphoreType.DMA((SC_RING,)), pltpu.SemaphoreType.DMA((SC_RING,))],
        name="gather",
    )(dest_chunks, ys3)


STRIP_ROWS = E_STRIP * ROW_CHUNKS
N_STRIPS = BM_E // E_STRIP


def _experts_kernel(blk_e_ref, n_used_ref, blk_rows_ref, xs_hbm, wgu_ref, wd_ref, ys_hbm,
                    xbuf, ybuf, sem_in, sem_out):
    del blk_e_ref
    b = pl.program_id(0)
    n_used = n_used_ref[0]

    def strips(blk):
        return (blk_rows_ref[blk] + (E_STRIP - 1)) // E_STRIP

    def x_copy(blk, s):
        rows = pl.ds(pl.multiple_of(blk * (BM_E * ROW_CHUNKS), STRIP_ROWS) + s * STRIP_ROWS, STRIP_ROWS)
        return pltpu.make_async_copy(xs_hbm.at[rows], xbuf.at[blk % 2, pl.ds(s * STRIP_ROWS, STRIP_ROWS)],
                                     sem_in.at[blk % 2])

    def y_copy(blk, s):
        rows = pl.ds(pl.multiple_of(blk * (BM_E * ROW_CHUNKS), STRIP_ROWS) + s * STRIP_ROWS, STRIP_ROWS)
        return pltpu.make_async_copy(ybuf.at[blk % 2, pl.ds(s * STRIP_ROWS, STRIP_ROWS)], ys_hbm.at[rows],
                                     sem_out.at[blk % 2])

    def for_strips(blk, cond, fn):
        n = strips(blk)
        for s in range(N_STRIPS):
            @pl.when(cond & (s < n))
            def _(s=s):
                fn(blk, s)

    def swiglu_rows(n_rows):
        slot = b % 2
        chunks = _load_chunks(xbuf, n_rows, lead=(slot,))
        x_lo = jnp.concatenate([_unpack_lo(p) for p in chunks], axis=-1).astype(BF16)
        x_hi = jnp.concatenate([_unpack_hi(p) for p in chunks], axis=-1).astype(BF16)
        gu = (jnp.dot(x_lo, wgu_ref[0, :HALF, :], preferred_element_type=F32)
              + jnp.dot(x_hi, wgu_ref[0, HALF:, :], preferred_element_type=F32))
        gate, up = gu[:, :D_EXPERT], gu[:, D_EXPERT:]
        hmid = (gate * _sigmoid(gate) * up).astype(BF16)
        packed = _pack_bf16_pairs(jnp.dot(hmid, wd_ref[0], preferred_element_type=F32))
        for c in range(ROW_CHUNKS):
            ybuf[slot, pl.ds(c, n_rows, stride=ROW_CHUNKS), :] = packed[:, c * LANES:(c + 1) * LANES]

    used = b < n_used
    for_strips(0, (b == 0) & (n_used > 0), lambda blk, s: x_copy(blk, s).start())
    for_strips(b, used, lambda blk, s: x_copy(blk, s).wait())
    nxt = jnp.minimum(b + 1, N_BLOCKS - 1)
    for_strips(nxt, b + 1 < n_used, lambda blk, s: x_copy(blk, s).start())
    for_strips(jnp.maximum(b - 2, 0), used & (b >= 2), lambda blk, s: y_copy(blk, s).wait())

    @pl.when(used)
    def _():
        valid = blk_rows_ref[b]
        for n_rows in range(E_STRIP, BM_E + 1, E_STRIP):
            @pl.when((valid > n_rows - E_STRIP) & (valid <= n_rows))
            def _(n_rows=n_rows):
                swiglu_rows(n_rows)

    for_strips(b, used, lambda blk, s: y_copy(blk, s).start())
    last = b == n_used - 1
    for_strips(jnp.maximum(b - 1, 0), last & (b >= 1), lambda blk, s: y_copy(blk, s).wait())
    for_strips(b, last, lambda blk, s: y_copy(blk, s).wait())


def _experts(blk_e, n_used, blk_rows, xs, wgu_bf, wd_bf):
    def blk(b, be, nu, nr):
        return jnp.minimum(b, nu[0] - 1)

    def by_expert(shape):
        return pl.BlockSpec((1,) + shape, lambda b, be, nu, nr: (be[blk(b, be, nu, nr)], 0, 0))

    ring = pltpu.VMEM((2, BM_E * ROW_CHUNKS, LANES), U32)
    grid_spec = pltpu.PrefetchScalarGridSpec(
        num_scalar_prefetch=3,
        grid=(N_BLOCKS,),
        in_specs=[pl.BlockSpec(memory_space=pl.ANY), by_expert((D_MODEL, 2 * D_EXPERT)),
                  by_expert((D_EXPERT, D_MODEL))],
        out_specs=pl.BlockSpec(memory_space=pl.ANY),
        scratch_shapes=[ring, ring, pltpu.SemaphoreType.DMA((2,)), pltpu.SemaphoreType.DMA((2,))],
    )
    return pl.pallas_call(
        _experts_kernel,
        grid_spec=grid_spec,
        out_shape=jax.ShapeDtypeStruct((N_SLOTS * ROW_CHUNKS, LANES), U32),
        compiler_params=_cparams(("arbitrary",)),
        name="experts",
    )(blk_e, n_used, blk_rows, xs, wgu_bf, wd_bf)


def _combine_kernel(yt_ref, wts_ref, h1_ref, p_ref, gffn_ref, wsgu_ref, wsd_ref, gple_ref, wpg_ref, wpp_ref,
                    gfin_ref, *y_refs):
    y_ref = y_refs[-1]
    h1 = h1_ref[...]
    gu = jnp.dot(_rms(h1, gffn_ref[...]).astype(BF16), wsgu_ref[...], preferred_element_type=F32)
    sgate, sup = gu[:, :D_SHARED], gu[:, D_SHARED:]
    hsh = h1 + jnp.dot((sgate * _sigmoid(sgate) * sup).astype(BF16), wsd_ref[...], preferred_element_type=F32)
    wts = jnp.transpose(wts_ref[...])
    lo = [jnp.zeros((BT_COMB, LANES), F32) for _ in range(ROW_CHUNKS)]
    hi = [jnp.zeros((BT_COMB, LANES), F32) for _ in range(ROW_CHUNKS)]
    for k in range(TOP_K):
        w = wts[:, k:k + 1]
        for c, p in enumerate(_load_chunks(yt_ref, BT_COMB, lead=(k,))):
            lo[c] = lo[c] + w * _unpack_lo(p)
            hi[c] = hi[c] + w * _unpack_hi(p)
    h2 = hsh + jnp.concatenate(lo + hi, axis=-1)
    gate = _sigmoid(jnp.dot(_rms(h2, gple_ref[...]).astype(BF16), wpg_ref[...], preferred_element_type=F32))
    proj = jnp.dot(p_ref[...].astype(BF16), wpp_ref[...], preferred_element_type=F32)
    y_ref[...] = _rms(h2 + proj * gate, gfin_ref[...])


def _combine(yt, yt_row0, wts_t, wts_row0, h1, tok_row0, n_rows, p2d, p_row0, ws, y_prev, out_rows, out_row0):
    assert all(r % BT_COMB == 0 for r in (yt_row0, wts_row0, tok_row0, n_rows, p_row0, out_row0))
    g0, w0, t0, p0, o0 = (r // BT_COMB for r in (yt_row0, wts_row0, tok_row0, p_row0, out_row0))
    in_specs = [pl.BlockSpec((TOP_K, BT_COMB * ROW_CHUNKS, LANES), lambda i: (0, g0 + i, 0)),
                pl.BlockSpec((TOP_K, BT_COMB), lambda i: (0, w0 + i)),
                pl.BlockSpec((BT_COMB, D_MODEL), lambda i: (t0 + i, 0)),
                pl.BlockSpec((BT_COMB, PLE_DIM), lambda i: (p0 + i, 0))] + [_full_spec(w, 1) for w in ws]
    args = [yt, wts_t, h1, p2d, *ws]
    aliases = {}
    if y_prev is not None:
        in_specs.append(pl.BlockSpec(memory_space=pl.ANY))
        aliases = {len(args): 0}
        args.append(y_prev)
    return pl.pallas_call(
        _combine_kernel,
        grid=(n_rows // BT_COMB,),
        in_specs=in_specs,
        out_specs=pl.BlockSpec((BT_COMB, D_MODEL), lambda i: (o0 + i, 0)),
        out_shape=jax.ShapeDtypeStruct((out_rows, D_MODEL), F32),
        input_output_aliases=aliases,
        compiler_params=_cparams(("parallel",)),
        name="combine",
    )(*args)


def kernel(x_prompt, x_sample, cache_k, cache_v, state_pool, p_prompt, p_sample, g_mix, w_in, attn_sinks,
           w_pool, pool_scale, g_att_out, g_pool_out, w_out, g_ffn, w_router, router_bias, w_exp_gate,
           w_exp_up, w_exp_down, w_sh_gate, w_sh_up, w_sh_down, g_ple, w_ple_gate, w_ple_proj, g_final):
    row = lambda a: a.reshape(1, -1)
    xp2d = x_prompt.reshape(T_P, D_MODEL)
    xs2d = x_sample.reshape(T_S, D_MODEL)
    w_in_bf = w_in[0].astype(BF16)
    mixer_wts = [w_pool[0].astype(BF16), row(pool_scale[0]), row(g_att_out[0]), row(g_pool_out[0]),
                 w_out[0].astype(BF16)]

    tab_p = _rope_tables(np.arange(SEQ))
    tab_s = _rope_tables(PAST_LEN + np.arange(DEC_SEQ), reps=BM_IN // DEC_SEQ)

    q_p, k_p, v_p, u_p = _inproj(xp2d, row(g_mix[0]), w_in_bf, tab_p, BF16)
    q_s, k_s, v_s, u_s = _inproj(xs2d, row(g_mix[0]), w_in_bf, tab_s, F32)

    h1, wgu_bf, wd_bf = _mixer_prompt(attn_sinks[0], xp2d, q_p, k_p, v_p, u_p, mixer_wts,
                                      (w_exp_gate[0], w_exp_up[0], w_exp_down[0]))
    h1, k_sample, v_sample, pool_sample = _mixer_sample(
        attn_sinks[0], xs2d, q_s, k_s, v_s, u_s,
        cache_k[0].reshape(DEC_BATCH, WINDOW, KV_WIDTH), cache_v[0].reshape(DEC_BATCH, WINDOW, KV_WIDTH),
        state_pool[0], mixer_wts, h1)

    g_ffn_row = row(g_ffn[0])
    router_wts = (g_ffn_row, w_router[0].T, router_bias[0].reshape(N_EXPERTS, 1))
    tri = (lax.broadcasted_iota(I32, (BT_RANK, BT_RANK), 0)
           < lax.broadcasted_iota(I32, (BT_RANK, BT_RANK), 1)).astype(BF16)

    def index_chunks(d, n_tok):
        return d.reshape(TOP_K, d.shape[1] // n_tok, n_tok).transpose(1, 0, 2)

    groups = []
    for g in range(N_GROUPS):
        xp, idx_t, wts_t = _router(h1, g, *router_wts)
        rank_t, counts = _rank(idx_t, tri)
        dest_t, *plan = _dest(idx_t, rank_t, counts)
        xs = _dispatch(index_chunks(dest_t, SC_CHUNK), xp.reshape(T_GRP, ROW_CHUNKS, LANES))
        groups.append((wts_t, index_chunks(dest_t, SC_CHUNK_G), xs, [a.reshape(-1) for a in plan]))

    ple_wts = [g_ffn_row, jnp.concatenate([w_sh_gate[0], w_sh_up[0]], axis=1).astype(BF16),
               w_sh_down[0].astype(BF16),
               row(g_ple[0]), w_ple_gate[0].astype(BF16), w_ple_proj[0].astype(BF16), row(g_final)]
    pp2d = p_prompt[0].reshape(T_P, PLE_DIM)
    ps2d = p_sample[0].reshape(T_S, PLE_DIM)
    y_p = y_s = None
    sub_chunks = T_SUB // SC_CHUNK_G
    for g, (wts_t, gather_chunks, xs, plan) in enumerate(groups):
        lo, hi = g * T_GRP, (g + 1) * T_GRP
        ys = _experts(*plan, xs.reshape(N_SLOTS * ROW_CHUNKS, LANES), wgu_bf, wd_bf)
        ys3 = ys.reshape(N_SLOTS, ROW_CHUNKS, LANES)
        for s in range(N_SUB):
            a, b = lo + s * T_SUB, lo + (s + 1) * T_SUB
            yt = _gather(gather_chunks[s * sub_chunks:(s + 1) * sub_chunks], ys3)
            yt = yt.reshape(TOP_K, T_SUB * ROW_CHUNKS, LANES)
            if a < T_P:
                n = min(b, T_P) - a
                y_p = _combine(yt, 0, wts_t, a - lo, h1, a, n, pp2d, a, ple_wts, y_p, T_P, a)
            if b > T_P:
                s0 = max(a, T_P)
                y_s = _combine(yt, s0 - a, wts_t, s0 - lo, h1, s0, b - s0, ps2d, s0 - T_P, ple_wts, y_s, T_S,
                               s0 - T_P)

    kv5 = lambda a, b: a.reshape(1, b, WINDOW, N_KV_HEADS, HEAD_DIM)
    k_prompt = kv5(k_p.reshape(BATCH, SEQ, KV_WIDTH)[:, SEQ - WINDOW:], BATCH)
    v_prompt = kv5(v_p.reshape(BATCH, SEQ, KV_WIDTH)[:, SEQ - WINDOW:], BATCH)
    pool_prompt = u_p.reshape(BATCH, SEQ, POOL_WIDTH)[:, SEQ - POOL_STATE:][None]
    return (y_p.reshape(BATCH, SEQ, D_MODEL), y_s.reshape(DEC_BATCH, DEC_SEQ, D_MODEL),
            k_prompt, v_prompt, pool_prompt,
            kv5(k_sample, DEC_BATCH), kv5(v_sample, DEC_BATCH), pool_sample[None])
```

```python
import functools

import numpy as np
import jax
import jax.numpy as jnp
from jax import lax
from jax.experimental import pallas as pl
from jax.experimental.pallas import tpu as pltpu
from jax.experimental.pallas import tpu_sc as plsc

F32 = jnp.float32
BF16 = jnp.bfloat16
U32 = jnp.uint32
I32 = jnp.int32

D_MODEL = 1024
BATCH = 8
SEQ = 2048
DEC_BATCH = 128
DEC_SEQ = 8
PAST_LEN = 16384
N_Q_HEADS = 8
N_KV_HEADS = 2
HEAD_DIM = 64
GQA_GROUP = N_Q_HEADS // N_KV_HEADS
ATT_WIDTH = N_Q_HEADS * HEAD_DIM
KV_WIDTH = N_KV_HEADS * HEAD_DIM
WINDOW = 128
ROPE_THETA = 500000.0
ROT_DIM = HEAD_DIM // 4
POOL_WINDOWS = (2, 4, 8, 16)
POOL_GROUPS = 4
POOL_WIDTH = D_MODEL - ATT_WIDTH
POOL_GROUP_DIM = POOL_WIDTH // POOL_GROUPS
POOL_STATE = 15
IN_WIDTH = ATT_WIDTH + 2 * KV_WIDTH + POOL_WIDTH
N_EXPERTS = 64
TOP_K = 8
N_EXPERT_GROUPS = 8
GROUP_SIZE = N_EXPERTS // N_EXPERT_GROUPS
TOPK_GROUPS = 4
D_EXPERT = 256
D_SHARED = 256
ROUTED_SCALE = 2.5
PLE_DIM = 256
EPS = 1e-6

T_P = BATCH * SEQ
T_S = DEC_BATCH * DEC_SEQ
T_ALL = T_P + T_S
HALF = D_MODEL // 2
LANES = 128
VMEM_LIMIT = 48 * 1024 * 1024

BM_IN = 1024
BQ = 2 * WINDOW
EXPERTS_PER_MIX_STEP = N_EXPERTS * BQ // T_P
assert EXPERTS_PER_MIX_STEP * T_P == N_EXPERTS * BQ
SB = 16
BM_R = 512
BT_RANK = 512
BT_DEST = 2176
BT_COMB = 256
N_GROUPS = 2
T_GRP = T_ALL // N_GROUPS
N_SUB = 2
T_SUB = T_GRP // N_SUB
assert T_GRP * N_GROUPS == T_ALL and T_SUB * N_SUB == T_GRP
BM_E = 1280
E_STRIP = 128
assert BM_E % E_STRIP == 0
N_ASSIGN = T_GRP * TOP_K
N_BLOCKS = -(-N_ASSIGN // BM_E) + N_EXPERTS
N_SLOTS = N_BLOCKS * BM_E

ROW_CHUNKS = HALF // LANES
SC_CORES = 2
SC_SUBCORES = 16
SC_WORKERS = SC_CORES * SC_SUBCORES
SC_CHUNK = 16
SC_CHUNK_G = 8
SC_RING = 4
assert T_GRP % (SC_WORKERS * SC_CHUNK) == 0 and T_SUB % (SC_WORKERS * SC_CHUNK_G) == 0


def _load_chunks(ref, n_rows, lead=()):
    return [ref[lead + (pl.ds(c, n_rows, stride=ROW_CHUNKS), slice(None))] for c in range(ROW_CHUNKS)]


def _store_chunks(ref, packed):
    n_rows = packed.shape[0]
    for c in range(ROW_CHUNKS):
        ref[pl.ds(c, n_rows, stride=ROW_CHUNKS), :] = packed[:, c * LANES:(c + 1) * LANES]


def _cparams(sem):
    return pltpu.CompilerParams(dimension_semantics=sem, vmem_limit_bytes=VMEM_LIMIT)


def _rms(x, g):
    return x * lax.rsqrt(jnp.mean(x * x, axis=-1, keepdims=True) + EPS) * g


def _sigmoid(x):
    return 1.0 / (1.0 + jnp.exp(-x))


def _pack_bf16_pairs(x):
    h = x.shape[-1] // 2
    return pltpu.pack_elementwise([x[:, :h], x[:, h:]], packed_dtype=BF16)


def _unpack_lo(p):
    return pltpu.bitcast(p << 16, F32)


def _unpack_hi(p):
    return pltpu.bitcast(p & jnp.uint32(0xFFFF0000), F32)


def _inproj_kernel(x_ref, g_ref, w_ref, c_ref, s1_ref, s2_ref, q_ref, k_ref, v_ref, u_ref):
    xn = _rms(x_ref[...], g_ref[...]).astype(BF16)
    z = jnp.dot(xn, w_ref[...], preferred_element_type=F32)
    c, s1, s2 = c_ref[...], s1_ref[...], s2_ref[...]

    def rope(t):
        return t * c + pltpu.roll(t, LANES - ROT_DIM // 2, 1) * s1 + pltpu.roll(t, ROT_DIM // 2, 1) * s2

    for i in range(ATT_WIDTH // LANES):
        sl = slice(i * LANES, (i + 1) * LANES)
        q_ref[:, sl] = (rope(z[:, sl]) * (HEAD_DIM ** -0.5)).astype(q_ref.dtype)
    k_ref[...] = rope(z[:, ATT_WIDTH:ATT_WIDTH + KV_WIDTH])
    v_ref[...] = z[:, ATT_WIDTH + KV_WIDTH:ATT_WIDTH + 2 * KV_WIDTH]
    u_ref[...] = z[:, ATT_WIDTH + 2 * KV_WIDTH:]


def _rope_tables(pos, reps=1):
    f32 = np.float32
    half = ROT_DIM // 2
    inv = np.power(f32(ROPE_THETA), -np.arange(half, dtype=f32) * f32(2.0) / f32(ROT_DIM)).astype(f32)
    ang = np.asarray(pos, f32)[:, None] * inv[None, :]
    cos, sin = np.cos(ang).astype(f32), np.sin(ang).astype(f32)
    n = len(pos)
    ones = np.ones((n, HEAD_DIM - ROT_DIM), f32)
    zeros = np.zeros((n, HEAD_DIM - ROT_DIM), f32)
    zh = np.zeros((n, half), f32)
    c = np.concatenate([cos, cos, ones], axis=1)
    s1 = np.concatenate([-sin, zh, zeros], axis=1)
    s2 = np.concatenate([zh, sin, zeros], axis=1)
    tile = lambda a: np.tile(a, (reps, LANES // HEAD_DIM))
    return tile(c), tile(s1), tile(s2)


def _inproj(x2d, g_mix, w_in_bf, tables, q_dtype):
    rows = x2d.shape[0]
    n_tab = tables[0].shape[0] // BM_IN
    row_spec = lambda w: pl.BlockSpec((BM_IN, w), lambda i: (i, 0))
    tab_spec = pl.BlockSpec((BM_IN, LANES), lambda i: (i % n_tab, 0))
    full = lambda a: pl.BlockSpec(a.shape, lambda i: (0,) * a.ndim)
    return pl.pallas_call(
        _inproj_kernel,
        grid=(rows // BM_IN,),
        in_specs=[row_spec(D_MODEL), full(g_mix), full(w_in_bf), tab_spec, tab_spec, tab_spec],
        out_specs=[row_spec(ATT_WIDTH), row_spec(KV_WIDTH), row_spec(KV_WIDTH), row_spec(POOL_WIDTH)],
        out_shape=[jax.ShapeDtypeStruct((rows, ATT_WIDTH), q_dtype),
                   jax.ShapeDtypeStruct((rows, KV_WIDTH), F32),
                   jax.ShapeDtypeStruct((rows, KV_WIDTH), F32),
                   jax.ShapeDtypeStruct((rows, POOL_WIDTH), F32)],
        compiler_params=_cparams(("parallel",)),
        name="inproj",
    )(x2d, g_mix, w_in_bf, *tables)


def _sink_column(sinks_ref, kv_head, rows_per_head):
    n = GQA_GROUP * rows_per_head
    grp = lax.broadcasted_iota(I32, (n, 1), 0) // rows_per_head
    col = jnp.full((n, 1), sinks_ref[kv_head * GQA_GROUP], F32)
    for g in range(1, GQA_GROUP):
        col = jnp.where(grp == g, sinks_ref[kv_head * GQA_GROUP + g], col)
    return col


def _band_mask(n_rows, rows_per_head, n_keys):
    i = lax.broadcasted_iota(I32, (n_rows, n_keys), 0) % rows_per_head
    c = lax.broadcasted_iota(I32, (n_rows, n_keys), 1)
    return (c >= i) & (c <= i + WINDOW), c


def _stack_heads(q, kv_head):
    return jnp.concatenate(
        [q[:, (kv_head * GQA_GROUP + g) * HEAD_DIM:(kv_head * GQA_GROUP + g + 1) * HEAD_DIM]
         for g in range(GQA_GROUP)], axis=0)


def _nt_dot(a, b):
    return lax.dot_general(a, b, (((1,), (1,)), ((), ())), preferred_element_type=F32)


POOL_HALO = 16
POOL_HEAD = 8
assert all(w == 2 << g for g, w in enumerate(POOL_WINDOWS)) and POOL_WINDOWS[-1] - 1 <= POOL_HALO


def _window_sums(ext_ref, n):
    lo, hi = POOL_HEAD, POOL_HEAD + POOL_HALO + n
    for p in range(POOL_GROUPS):
        lanes = slice(p * POOL_GROUP_DIM, POOL_WIDTH)
        ext_ref[lo:hi, lanes] = ext_ref[lo:hi, lanes] + ext_ref[lo - (1 << p):hi - (1 << p), lanes]


def _pool_out(d, wpool_ref, pscale_ref, gpool_ref):
    parts = [jnp.dot(d[:, g * POOL_GROUP_DIM:(g + 1) * POOL_GROUP_DIM].astype(BF16), wpool_ref[g],
                     preferred_element_type=F32) for g in range(POOL_GROUPS)]
    return _rms(jnp.concatenate(parts, axis=-1) * pscale_ref[...], gpool_ref[...])


def _mixer_tail(o_att, pooled, h, gatt_ref, wout_ref):
    mixed = jnp.concatenate([_rms(o_att, gatt_ref[...]), pooled], axis=-1)
    return h + jnp.dot(mixed.astype(BF16), wout_ref[...], preferred_element_type=F32)


def _mixer_prompt_kernel(sinks_ref, h_ref, q_ref, kc_ref, kp_ref, vc_ref, vp_ref, uc_ref, up_ref,
                         wpool_ref, pscale_ref, gatt_ref, gpool_ref, wout_ref, weg_ref, weu_ref, wed_ref,
                         h1_ref, wgu_bf_ref, wd_bf_ref, uext_ref):
    j = pl.program_id(1)
    wgu_bf_ref[:, :, :D_EXPERT] = weg_ref[...].astype(BF16)
    wgu_bf_ref[:, :, D_EXPERT:] = weu_ref[...].astype(BF16)
    wd_bf_ref[...] = wed_ref[...].astype(BF16)

    u = uc_ref[...]
    base = POOL_HEAD + POOL_HALO
    uext_ref[0:POOL_HEAD, :] = jnp.zeros((POOL_HEAD, POOL_WIDTH), F32)
    uext_ref[POOL_HEAD:base, :] = jnp.where(j > 0, up_ref[...], 0.0)
    uext_ref[base:base + BQ, :] = u
    _window_sums(uext_ref, BQ)
    pos = j * BQ + lax.broadcasted_iota(I32, (BQ, 1), 0)
    parts = []
    for g, w in enumerate(POOL_WINDOWS):
        sl = slice(g * POOL_GROUP_DIM, (g + 1) * POOL_GROUP_DIM)
        parts.append(uext_ref[base:base + BQ, sl] / jnp.minimum(pos + 1, w).astype(F32) - u[:, sl])
    pooled = _pool_out(jnp.concatenate(parts, axis=-1), wpool_ref, pscale_ref, gpool_ref)

    q = q_ref[...]
    k_all = jnp.concatenate([kp_ref[...], kc_ref[...]], axis=0).astype(BF16)
    v_all = jnp.concatenate([vp_ref[...], vc_ref[...]], axis=0).astype(BF16)
    ones = jnp.ones((WINDOW + BQ, HEAD_DIM), BF16)
    v_ones = [jnp.concatenate([v_all[:, hk * HEAD_DIM:(hk + 1) * HEAD_DIM], ones], axis=1)
              for hk in range(N_KV_HEADS)]
    band, col = _band_mask(GQA_GROUP * WINDOW, WINDOW, 2 * WINDOW)
    sinks = [_sink_column(sinks_ref, hk, WINDOW) for hk in range(N_KV_HEADS)]
    bands = []
    for b in range(BQ // WINDOW):
        rows = slice(b * WINDOW, (b + 1) * WINDOW)
        keys = slice(b * WINDOW, (b + 2) * WINDOW)
        mask = band & ((col >= WINDOW) | (j > 0)) if b == 0 else band
        heads = []
        for hk in range(N_KV_HEADS):
            sl = slice(hk * HEAD_DIM, (hk + 1) * HEAD_DIM)
            s = jnp.where(mask, _nt_dot(_stack_heads(q[rows], hk), k_all[keys, sl]), -jnp.inf)
            m = jnp.maximum(jnp.max(s, axis=-1, keepdims=True), sinks[hk])
            e = jnp.exp(s - m).astype(BF16)
            ov = jnp.dot(e, v_ones[hk][keys], preferred_element_type=F32)
            den = ov[:, HEAD_DIM:HEAD_DIM + 1] + jnp.exp(sinks[hk] - m)
            o = ov[:, :HEAD_DIM] / den
            heads += [o[g * WINDOW:(g + 1) * WINDOW] for g in range(GQA_GROUP)]
        bands.append(jnp.concatenate(heads, axis=-1))
    o_att = jnp.concatenate(bands, axis=0)
    h1_ref[...] = _mixer_tail(o_att, pooled, h_ref[...], gatt_ref, wout_ref)


def _mixer_sample_kernel(sinks_ref, h_ref, q_ref, kn_ref, vn_ref, u_ref, ck_ref, cv_ref, st_ref,
                         wpool_ref, pscale_ref, gatt_ref, gpool_ref, wout_ref, h1_in_ref,
                         h1_ref, ko_ref, vo_ref, po_ref, uext_ref):
    del h1_in_ref
    n_q = GQA_GROUP * DEC_SEQ
    n_keys = 2 * WINDOW
    band, col = _band_mask(n_q, DEC_SEQ, n_keys)
    mask = (band & (col < WINDOW + DEC_SEQ))[None]
    q3, kn3, vn3, u3 = q_ref[...], kn_ref[...], vn_ref[...], u_ref[...]
    ck, cv = ck_ref[...], cv_ref[...]
    ko_ref[:, 0:WINDOW - DEC_SEQ, :] = ck[:, DEC_SEQ:, :]
    ko_ref[:, WINDOW - DEC_SEQ:WINDOW, :] = kn3
    vo_ref[:, 0:WINDOW - DEC_SEQ, :] = cv[:, DEC_SEQ:, :]
    vo_ref[:, WINDOW - DEC_SEQ:WINDOW, :] = vn3
    pad = jnp.zeros((SB, WINDOW - DEC_SEQ, KV_WIDTH), F32)
    k_all = jnp.concatenate([ck, kn3, pad], axis=1).astype(BF16)
    v_all = jnp.concatenate([cv, vn3, pad], axis=1).astype(BF16)
    heads = []
    for hk in range(N_KV_HEADS):
        sl = slice(hk * HEAD_DIM, (hk + 1) * HEAD_DIM)
        qs = jnp.concatenate(
            [q3[:, :, (hk * GQA_GROUP + g) * HEAD_DIM:(hk * GQA_GROUP + g + 1) * HEAD_DIM]
             for g in range(GQA_GROUP)], axis=1).astype(BF16)
        sink = _sink_column(sinks_ref, hk, DEC_SEQ)[None]
        s = jnp.einsum("bqd,bkd->bqk", qs, k_all[:, :, sl], preferred_element_type=F32)
        s = jnp.where(mask, s, -jnp.inf)
        m = jnp.maximum(jnp.max(s, axis=-1, keepdims=True), sink)
        e = jnp.exp(s - m)
        den = jnp.sum(e, axis=-1, keepdims=True) + jnp.exp(sink - m)
        o = jnp.einsum("bqk,bkd->bqd", e.astype(BF16), v_all[:, :, sl], preferred_element_type=F32) / den
        heads += [o[:, g * DEC_SEQ:(g + 1) * DEC_SEQ, :] for g in range(GQA_GROUP)]
    o_att = jnp.concatenate(heads, axis=-1).reshape(SB * DEC_SEQ, ATT_WIDTH)

    uext_ref[:, 1:16, :] = st_ref[...]
    uext_ref[:, 16:16 + DEC_SEQ, :] = u3
    parts = []
    for g, w in enumerate(POOL_WINDOWS):
        sl = slice(g * POOL_GROUP_DIM, (g + 1) * POOL_GROUP_DIM)
        acc = u3[:, :, sl]
        for back in range(1, w):
            acc = acc + uext_ref[:, 16 - back:16 - back + DEC_SEQ, sl]
        parts.append(acc / float(w) - u3[:, :, sl])
    d = jnp.concatenate(parts, axis=-1).reshape(SB * DEC_SEQ, POOL_WIDTH)
    po_ref[...] = uext_ref[:, 16 + DEC_SEQ - POOL_STATE:16 + DEC_SEQ, :]
    pooled = _pool_out(d, wpool_ref, pscale_ref, gpool_ref)
    h1_ref[...] = _mixer_tail(o_att, pooled, h_ref[...], gatt_ref, wout_ref)


def _full_spec(a, n_grid):
    nd = a.ndim
    return pl.BlockSpec(a.shape, lambda *_: (0,) * nd)


def _mixer_prompt(sinks, x2d, q, k, v, u, wts, w_exp):
    nb = SEQ // BQ
    row = lambda w: pl.BlockSpec((BQ, w), lambda b, j: (b * nb + j, 0))
    prev = lambda w: pl.BlockSpec(
        (WINDOW, w), lambda b, j: (jnp.maximum((b * nb + j) * (BQ // WINDOW) - 1, 0), 0))
    uprev = pl.BlockSpec((POOL_HALO, POOL_WIDTH),
                         lambda b, j: (jnp.maximum((b * nb + j) * (BQ // POOL_HALO) - 1, 0), 0))
    smem = pl.BlockSpec(memory_space=pltpu.SMEM)
    per_step = lambda r, c: pl.BlockSpec((EXPERTS_PER_MIX_STEP, r, c), lambda b, j: (b * nb + j, 0, 0))
    return pl.pallas_call(
        _mixer_prompt_kernel,
        grid=(BATCH, nb),
        in_specs=[smem, row(D_MODEL), row(ATT_WIDTH), row(KV_WIDTH), prev(KV_WIDTH), row(KV_WIDTH),
                  prev(KV_WIDTH), row(POOL_WIDTH), uprev] + [_full_spec(w, 2) for w in wts]
                 + [per_step(D_MODEL, D_EXPERT), per_step(D_MODEL, D_EXPERT), per_step(D_EXPERT, D_MODEL)],
        out_specs=[row(D_MODEL), per_step(D_MODEL, 2 * D_EXPERT), per_step(D_EXPERT, D_MODEL)],
        out_shape=[jax.ShapeDtypeStruct((T_ALL, D_MODEL), F32),
                   jax.ShapeDtypeStruct((N_EXPERTS, D_MODEL, 2 * D_EXPERT), BF16),
                   jax.ShapeDtypeStruct((N_EXPERTS, D_EXPERT, D_MODEL), BF16)],
        scratch_shapes=[pltpu.VMEM((POOL_HEAD + POOL_HALO + BQ, POOL_WIDTH), F32)],
        compiler_params=_cparams(("parallel", "parallel")),
        name="mixer_prompt",
    )(sinks, x2d, q, k, k, v, v, u, u, *wts, *w_exp)


def _mixer_sample(sinks, x2d, q, k, v, u, cache_k, cache_v, state, wts, h1_buf):
    rows = SB * DEC_SEQ
    row = lambda w: pl.BlockSpec((rows, w), lambda i: (i, 0))
    bat = lambda a: pl.BlockSpec((SB,) + a.shape[1:], lambda i: (i, 0, 0))
    smem = pl.BlockSpec(memory_space=pltpu.SMEM)
    h1_blocks_before = T_P // rows
    n_in = 9 + len(wts)
    q, k, v, u = (a.reshape(DEC_BATCH, DEC_SEQ, a.shape[-1]) for a in (q, k, v, u))
    st = pl.BlockSpec((None, SB) + state.shape[2:], lambda i: (0, i, 0, 0))
    return pl.pallas_call(
        _mixer_sample_kernel,
        grid=(DEC_BATCH // SB,),
        in_specs=[smem, row(D_MODEL), bat(q), bat(k), bat(v), bat(u),
                  bat(cache_k), bat(cache_v), st] + [_full_spec(w, 1) for w in wts]
                 + [pl.BlockSpec(memory_space=pl.ANY)],
        out_specs=[pl.BlockSpec((rows, D_MODEL), lambda i: (h1_blocks_before + i, 0)),
                   bat(cache_k), bat(cache_v), st],
        out_shape=[jax.ShapeDtypeStruct((T_ALL, D_MODEL), F32),
                   jax.ShapeDtypeStruct(cache_k.shape, F32),
                   jax.ShapeDtypeStruct(cache_v.shape, F32),
                   jax.ShapeDtypeStruct(state.shape, F32)],
        scratch_shapes=[pltpu.VMEM((SB, 16 + DEC_SEQ, POOL_WIDTH), F32)],
        input_output_aliases={n_in: 0},
        compiler_params=_cparams(("parallel",)),
        name="mixer_sample",
    )(sinks, x2d, q, k, v, u, cache_k, cache_v, state, *wts, h1_buf)


def _first_max(vals, iota, n):
    m = jnp.max(vals, axis=0, keepdims=True)
    idx = jnp.min(jnp.where(vals == m, iota, n), axis=0, keepdims=True)
    return m, idx


def _router_kernel(h1_ref, gffn_ref, wrt_ref, bias_ref, xp_ref, idx_ref, wts_ref):
    xn = _rms(h1_ref[...], gffn_ref[...])
    w = wrt_ref[...]
    w_hi, x_hi = w.astype(BF16), xn.astype(BF16)
    w_lo, x_lo = (w - w_hi.astype(F32)).astype(BF16), (xn - x_hi.astype(F32)).astype(BF16)
    logits = _nt_dot(w_hi, x_hi) + (_nt_dot(w_hi, x_lo) + _nt_dot(w_lo, x_hi))
    scores = _sigmoid(logits)
    biased = scores + bias_ref[...]
    n_tok = biased.shape[1]
    neg = -jnp.inf

    iota_g = lax.broadcasted_iota(I32, (GROUP_SIZE, n_tok), 0)
    grp_rows = []
    for g in range(N_EXPERT_GROUPS):
        blk = biased[g * GROUP_SIZE:(g + 1) * GROUP_SIZE, :]
        top1, i1 = _first_max(blk, iota_g, GROUP_SIZE)
        top2 = jnp.max(jnp.where(iota_g == i1, neg, blk), axis=0, keepdims=True)
        grp_rows.append(top1 + top2)
    gs = jnp.concatenate(grp_rows, axis=0)

    iota_n = lax.broadcasted_iota(I32, (N_EXPERT_GROUPS, n_tok), 0)
    gsel = jnp.zeros((N_EXPERT_GROUPS, n_tok), jnp.bool_)
    for _ in range(TOPK_GROUPS):
        _, gi = _first_max(gs, iota_n, N_EXPERT_GROUPS)
        hit = iota_n == gi
        gsel = gsel | hit
        gs = jnp.where(hit, neg, gs)
    emask = jnp.concatenate(
        [jnp.broadcast_to(gsel[g:g + 1, :], (GROUP_SIZE, n_tok)) for g in range(N_EXPERT_GROUPS)], axis=0)
    masked = jnp.where(emask, biased, neg)

    iota_e = lax.broadcasted_iota(I32, (N_EXPERTS, n_tok), 0)
    idx_rows, sel_rows = [], []
    for _ in range(TOP_K):
        _, ei = _first_max(masked, iota_e, N_EXPERTS)
        hit = iota_e == ei
        idx_rows.append(ei)
        sel_rows.append(jnp.sum(jnp.where(hit, scores, 0.0), axis=0, keepdims=True))
        masked = jnp.where(hit, neg, masked)
    sel = jnp.concatenate(sel_rows, axis=0)
    idx_ref[...] = jnp.concatenate(idx_rows, axis=0)
    wts_ref[...] = sel / jnp.sum(sel, axis=0, keepdims=True) * ROUTED_SCALE
    _store_chunks(xp_ref, _pack_bf16_pairs(xn))


def _router(h1, group, g_ffn, w_router_t, bias_col):
    blk0 = group * T_GRP // BM_R
    colblk = pl.BlockSpec((TOP_K, BM_R), lambda i: (0, i))
    ws = [g_ffn, w_router_t, bias_col]
    return pl.pallas_call(
        _router_kernel,
        grid=(T_GRP // BM_R,),
        in_specs=[pl.BlockSpec((BM_R, D_MODEL), lambda i: (blk0 + i, 0))] + [_full_spec(w, 1) for w in ws],
        out_specs=[pl.BlockSpec((BM_R * ROW_CHUNKS, LANES), lambda i: (i, 0)), colblk, colblk],
        out_shape=[jax.ShapeDtypeStruct((T_GRP * ROW_CHUNKS, LANES), U32),
                   jax.ShapeDtypeStruct((TOP_K, T_GRP), I32),
                   jax.ShapeDtypeStruct((TOP_K, T_GRP), F32)],
        compiler_params=_cparams(("parallel",)),
        name="router",
    )(h1, *ws)


def _rank_kernel(idx_ref, tri_ref, rank_ref, cnt_ref, carry_ref):
    @pl.when(pl.program_id(0) == 0)
    def _():
        carry_ref[...] = jnp.zeros_like(carry_ref)

    idx = idx_ref[...]
    n_tok = idx.shape[1]
    iota_e = lax.broadcasted_iota(I32, (N_EXPERTS, n_tok), 0)
    member = jnp.zeros((N_EXPERTS, n_tok), F32)
    for k in range(TOP_K):
        member = member + jnp.where(iota_e == idx[k:k + 1, :], 1.0, 0.0)
    before = jnp.dot(member.astype(BF16), tri_ref[...], preferred_element_type=F32) + carry_ref[...]
    rows = [jnp.sum(jnp.where(iota_e == idx[k:k + 1, :], before, 0.0), axis=0, keepdims=True)
            for k in range(TOP_K)]
    rank_ref[...] = jnp.concatenate(rows, axis=0).astype(I32)
    carry_ref[...] = carry_ref[...] + jnp.sum(member, axis=1, keepdims=True)
    cnt_ref[...] = carry_ref[...].astype(I32)


def _rank(idx_t, tri):
    blk = pl.BlockSpec((TOP_K, BT_RANK), lambda i: (0, i))
    return pl.pallas_call(
        _rank_kernel,
        grid=(T_GRP // BT_RANK,),
        in_specs=[blk, _full_spec(tri, 1)],
        out_specs=[blk, pl.BlockSpec((N_EXPERTS, 1), lambda i: (0, 0))],
        out_shape=[jax.ShapeDtypeStruct((TOP_K, T_GRP), I32),
                   jax.ShapeDtypeStruct((N_EXPERTS, 1), I32)],
        scratch_shapes=[pltpu.VMEM((N_EXPERTS, 1), F32)],
        compiler_params=_cparams(("arbitrary",)),
        name="rank",
    )(idx_t, tri)


def _dest_kernel(idx_ref, rank_ref, cnt_ref, dest_ref, blk_e_ref, n_used_ref, blk_rows_ref):
    counts = cnt_ref[...]
    padded = (counts + (BM_E - 1)) // BM_E * BM_E
    r = lax.broadcasted_iota(I32, (N_EXPERTS, N_EXPERTS), 0)
    c = lax.broadcasted_iota(I32, (N_EXPERTS, N_EXPERTS), 1)
    padded_row = jnp.sum(jnp.where(r == c, padded, 0), axis=0, keepdims=True)
    pad_start = jnp.sum(jnp.where(c < r, padded_row, 0), axis=1, keepdims=True)

    idx = idx_ref[...]
    n_tok = idx.shape[1]
    iota_e = lax.broadcasted_iota(I32, (N_EXPERTS, n_tok), 0)
    rows = [jnp.sum(jnp.where(iota_e == idx[k:k + 1, :], pad_start, 0), axis=0, keepdims=True)
            for k in range(TOP_K)]
    dest_ref[...] = jnp.concatenate(rows, axis=0) + rank_ref[...]

    @pl.when(pl.program_id(0) == 0)
    def _():
        pad_end_row = jnp.sum(jnp.where(r <= c, padded, 0), axis=0, keepdims=True)
        b0 = lax.broadcasted_iota(I32, (N_BLOCKS_PAD, N_EXPERTS), 0) * BM_E
        be = jnp.minimum(jnp.sum(jnp.where(pad_end_row <= b0, 1, 0), axis=1, keepdims=True), N_EXPERTS - 1)
        blk_e_ref[...] = be
        n_used_ref[...] = pad_end_row[:, N_EXPERTS - 1:N_EXPERTS] // BM_E
        counts_row = jnp.sum(jnp.where(r == c, counts, 0), axis=0, keepdims=True)
        mine = lax.broadcasted_iota(I32, (N_BLOCKS_PAD, N_EXPERTS), 1) == be
        end_valid = jnp.sum(jnp.where(mine, pad_end_row - padded_row + counts_row, 0), axis=1, keepdims=True)
        blk_rows_ref[...] = jnp.clip(end_valid - b0[:, :1], 0, BM_E)


N_BLOCKS_PAD = (N_BLOCKS + 7) // 8 * 8


def _dest(idx_t, rank_t, counts):
    blk = pl.BlockSpec((TOP_K, BT_DEST), lambda i: (0, i))
    one = lambda s: pl.BlockSpec(s, lambda i: (0, 0))
    return pl.pallas_call(
        _dest_kernel,
        grid=(T_GRP // BT_DEST,),
        in_specs=[blk, blk, one((N_EXPERTS, 1))],
        out_specs=[blk, one((N_BLOCKS_PAD, 1)), one((1, 1)), one((N_BLOCKS_PAD, 1))],
        out_shape=[jax.ShapeDtypeStruct((TOP_K, T_GRP), I32),
                   jax.ShapeDtypeStruct((N_BLOCKS_PAD, 1), I32),
                   jax.ShapeDtypeStruct((1, 1), I32),
                   jax.ShapeDtypeStruct((N_BLOCKS_PAD, 1), I32)],
        compiler_params=_cparams(("arbitrary",)),
        name="dest",
    )(idx_t, rank_t, counts)


def _sc_mesh():
    return plsc.VectorSubcoreMesh(core_axis_name="c", subcore_axis_name="s")


def _sc_worker_id():
    return lax.axis_index("s") * SC_CORES + lax.axis_index("c")


def _dispatch_body(dest_hbm, xp_hbm, xs_hbm, idx_v, rows_v, sem_in, sem_out):
    n_chunks, _, n_tok = dest_hbm.shape
    per_worker = n_chunks // SC_WORKERS
    chunk0 = _sc_worker_id() * per_worker

    def loads(i):
        chunk = chunk0 + i
        t0 = pl.multiple_of(chunk * n_tok, n_tok)
        return (pltpu.make_async_copy(dest_hbm.at[chunk], idx_v.at[i % 2], sem_in.at[i % 2]),
                pltpu.make_async_copy(xp_hbm.at[pl.ds(t0, n_tok)], rows_v.at[i % 2], sem_in.at[i % 2]))

    def scatters(i):
        return [pltpu.make_async_copy(rows_v.at[i % 2], xs_hbm.at[idx_v.at[i % 2, k]], sem_out.at[i % 2])
                for k in range(TOP_K)]

    for cp in loads(0):
        cp.start()
    for i in range(per_worker):
        for cp in loads(i):
            cp.wait()
        if i >= 1:
            for cp in scatters(i - 1):
                cp.wait()
        if i + 1 < per_worker:
            for cp in loads(i + 1):
                cp.start()
        for cp in scatters(i):
            cp.start()
    for cp in scatters(per_worker - 1):
        cp.wait()


def _dispatch(dest_chunks, xp3):
    return pl.kernel(
        _dispatch_body,
        out_type=jax.ShapeDtypeStruct((N_SLOTS, ROW_CHUNKS, LANES), U32),
        mesh=_sc_mesh(),
        scratch_types=[pltpu.VMEM((2, TOP_K, SC_CHUNK), I32),
                       pltpu.VMEM((2, SC_CHUNK, ROW_CHUNKS, LANES), U32),
                       pltpu.SemaphoreType.DMA((2,)), pltpu.SemaphoreType.DMA((2,))],
        name="dispatch",
    )(dest_chunks, xp3)


def _gather_body(dest_hbm, ys_hbm, yt_hbm, idx_v, rows_v, sem_in, sem_out):
    n_chunks, _, n_tok = dest_hbm.shape
    per_worker = n_chunks // SC_WORKERS
    chunk0 = _sc_worker_id() * per_worker

    @pl.loop(0, per_worker)
    def _(i):
        chunk = chunk0 + i
        t0 = pl.multiple_of(chunk * n_tok, n_tok)
        pltpu.sync_copy(dest_hbm.at[chunk], idx_v)

        def gather(k):
            return pltpu.make_async_copy(ys_hbm.at[idx_v.at[k]], rows_v.at[k % SC_RING], sem_in.at[k % SC_RING])

        def store(k):
            return pltpu.make_async_copy(rows_v.at[k % SC_RING], yt_hbm.at[k, pl.ds(t0, n_tok)],
                                         sem_out.at[k % SC_RING])

        for k in range(SC_RING):
            gather(k).start()
        for k in range(TOP_K):
            gather(k).wait()
            store(k).start()
            if k + SC_RING < TOP_K:
                store(k).wait()
                gather(k + SC_RING).start()
        for k in range(TOP_K - SC_RING, TOP_K):
            store(k).wait()


def _gather(dest_chunks, ys3):
    n_chunks, _, n_tok = dest_chunks.shape
    assert n_chunks % SC_WORKERS == 0
    return pl.kernel(
        _gather_body,
        out_type=jax.ShapeDtypeStruct((TOP_K, n_chunks * n_tok, ROW_CHUNKS, LANES), U32),
        mesh=_sc_mesh(),
        scratch_types=[pltpu.VMEM((TOP_K, n_tok), I32),
                       pltpu.VMEM((SC_RING, n_tok, ROW_CHUNKS, LANES), U32),
                       pltpu.SemaphoreType.DMA((SC_RING,)), pltpu.SemaphoreType.DMA((SC_RING,))],
        name="gather",
    )(dest_chunks, ys3)


def _experts_kernel(blk_e_ref, n_used_ref, blk_rows_ref, xs_ref, wgu_ref, wd_ref, ys_ref):
    del blk_e_ref
    b = pl.program_id(0)

    def swiglu_rows(n_rows):
        chunks = _load_chunks(xs_ref, n_rows)
        x_lo = jnp.concatenate([_unpack_lo(p) for p in chunks], axis=-1).astype(BF16)
        x_hi = jnp.concatenate([_unpack_hi(p) for p in chunks], axis=-1).astype(BF16)
        gu = (jnp.dot(x_lo, wgu_ref[0, :HALF, :], preferred_element_type=F32)
              + jnp.dot(x_hi, wgu_ref[0, HALF:, :], preferred_element_type=F32))
        gate, up = gu[:, :D_EXPERT], gu[:, D_EXPERT:]
        hmid = (gate * _sigmoid(gate) * up).astype(BF16)
        _store_chunks(ys_ref, _pack_bf16_pairs(jnp.dot(hmid, wd_ref[0], preferred_element_type=F32)))

    @pl.when(b < n_used_ref[0])
    def _():
        valid = blk_rows_ref[b]
        for n_rows in range(E_STRIP, BM_E + 1, E_STRIP):
            @pl.when((valid > n_rows - E_STRIP) & (valid <= n_rows))
            def _(n_rows=n_rows):
                swiglu_rows(n_rows)


def _experts(blk_e, n_used, blk_rows, xs, wgu_bf, wd_bf):
    def blk(b, be, nu, nr):
        return jnp.minimum(b, nu[0] - 1)

    def by_expert(shape):
        return pl.BlockSpec((1,) + shape, lambda b, be, nu, nr: (be[blk(b, be, nu, nr)], 0, 0))

    tile = pl.BlockSpec((BM_E * ROW_CHUNKS, LANES), lambda b, be, nu, nr: (blk(b, be, nu, nr), 0))
    grid_spec = pltpu.PrefetchScalarGridSpec(
        num_scalar_prefetch=3,
        grid=(N_BLOCKS,),
        in_specs=[tile, by_expert((D_MODEL, 2 * D_EXPERT)), by_expert((D_EXPERT, D_MODEL))],
        out_specs=tile,
    )
    return pl.pallas_call(
        _experts_kernel,
        grid_spec=grid_spec,
        out_shape=jax.ShapeDtypeStruct((N_SLOTS * ROW_CHUNKS, LANES), U32),
        compiler_params=_cparams(("arbitrary",)),
        name="experts",
    )(blk_e, n_used, blk_rows, xs, wgu_bf, wd_bf)


def _combine_kernel(yt_ref, wts_ref, h1_ref, p_ref, gffn_ref, wsgu_ref, wsd_ref, gple_ref, wpg_ref, wpp_ref,
                    gfin_ref, *y_refs):
    y_ref = y_refs[-1]
    h1 = h1_ref[...]
    gu = jnp.dot(_rms(h1, gffn_ref[...]).astype(BF16), wsgu_ref[...], preferred_element_type=F32)
    sgate, sup = gu[:, :D_SHARED], gu[:, D_SHARED:]
    hsh = h1 + jnp.dot((sgate * _sigmoid(sgate) * sup).astype(BF16), wsd_ref[...], preferred_element_type=F32)
    wts = jnp.transpose(wts_ref[...])
    lo = [jnp.zeros((BT_COMB, LANES), F32) for _ in range(ROW_CHUNKS)]
    hi = [jnp.zeros((BT_COMB, LANES), F32) for _ in range(ROW_CHUNKS)]
    for k in range(TOP_K):
        w = wts[:, k:k + 1]
        for c, p in enumerate(_load_chunks(yt_ref, BT_COMB, lead=(k,))):
            lo[c] = lo[c] + w * _unpack_lo(p)
            hi[c] = hi[c] + w * _unpack_hi(p)
    h2 = hsh + jnp.concatenate(lo + hi, axis=-1)
    gate = _sigmoid(jnp.dot(_rms(h2, gple_ref[...]).astype(BF16), wpg_ref[...], preferred_element_type=F32))
    proj = jnp.dot(p_ref[...].astype(BF16), wpp_ref[...], preferred_element_type=F32)
    y_ref[...] = _rms(h2 + proj * gate, gfin_ref[...])


def _combine(yt, yt_row0, wts_t, wts_row0, h1, tok_row0, n_rows, p2d, p_row0, ws, y_prev, out_rows, out_row0):
    assert all(r % BT_COMB == 0 for r in (yt_row0, wts_row0, tok_row0, n_rows, p_row0, out_row0))
    g0, w0, t0, p0, o0 = (r // BT_COMB for r in (yt_row0, wts_row0, tok_row0, p_row0, out_row0))
    in_specs = [pl.BlockSpec((TOP_K, BT_COMB * ROW_CHUNKS, LANES), lambda i: (0, g0 + i, 0)),
                pl.BlockSpec((TOP_K, BT_COMB), lambda i: (0, w0 + i)),
                pl.BlockSpec((BT_COMB, D_MODEL), lambda i: (t0 + i, 0)),
                pl.BlockSpec((BT_COMB, PLE_DIM), lambda i: (p0 + i, 0))] + [_full_spec(w, 1) for w in ws]
    args = [yt, wts_t, h1, p2d, *ws]
    aliases = {}
    if y_prev is not None:
        in_specs.append(pl.BlockSpec(memory_space=pl.ANY))
        aliases = {len(args): 0}
        args.append(y_prev)
    return pl.pallas_call(
        _combine_kernel,
        grid=(n_rows // BT_COMB,),
        in_specs=in_specs,
        out_specs=pl.BlockSpec((BT_COMB, D_MODEL), lambda i: (o0 + i, 0)),
        out_shape=jax.ShapeDtypeStruct((out_rows, D_MODEL), F32),
        input_output_aliases=aliases,
        compiler_params=_cparams(("parallel",)),
        name="combine",
    )(*args)


def kernel(x_prompt, x_sample, cache_k, cache_v, state_pool, p_prompt, p_sample, g_mix, w_in, attn_sinks,
           w_pool, pool_scale, g_att_out, g_pool_out, w_out, g_ffn, w_router, router_bias, w_exp_gate,
           w_exp_up, w_exp_down, w_sh_gate, w_sh_up, w_sh_down, g_ple, w_ple_gate, w_ple_proj, g_final):
    row = lambda a: a.reshape(1, -1)
    xp2d = x_prompt.reshape(T_P, D_MODEL)
    xs2d = x_sample.reshape(T_S, D_MODEL)
    w_in_bf = w_in[0].astype(BF16)
    mixer_wts = [w_pool[0].astype(BF16), row(pool_scale[0]), row(g_att_out[0]), row(g_pool_out[0]),
                 w_out[0].astype(BF16)]

    tab_p = _rope_tables(np.arange(SEQ))
    tab_s = _rope_tables(PAST_LEN + np.arange(DEC_SEQ), reps=BM_IN // DEC_SEQ)

    q_p, k_p, v_p, u_p = _inproj(xp2d, row(g_mix[0]), w_in_bf, tab_p, BF16)
    q_s, k_s, v_s, u_s = _inproj(xs2d, row(g_mix[0]), w_in_bf, tab_s, F32)

    h1, wgu_bf, wd_bf = _mixer_prompt(attn_sinks[0], xp2d, q_p, k_p, v_p, u_p, mixer_wts,
                                      (w_exp_gate[0], w_exp_up[0], w_exp_down[0]))
    h1, k_sample, v_sample, pool_sample = _mixer_sample(
        attn_sinks[0], xs2d, q_s, k_s, v_s, u_s,
        cache_k[0].reshape(DEC_BATCH, WINDOW, KV_WIDTH), cache_v[0].reshape(DEC_BATCH, WINDOW, KV_WIDTH),
        state_pool, mixer_wts, h1)

    g_ffn_row = row(g_ffn[0])
    router_wts = (g_ffn_row, w_router[0].T, router_bias[0].reshape(N_EXPERTS, 1))
    tri = (lax.broadcasted_iota(I32, (BT_RANK, BT_RANK), 0)
           < lax.broadcasted_iota(I32, (BT_RANK, BT_RANK), 1)).astype(BF16)

    def index_chunks(d, n_tok):
        return d.reshape(TOP_K, d.shape[1] // n_tok, n_tok).transpose(1, 0, 2)

    groups = []
    for g in range(N_GROUPS):
        xp, idx_t, wts_t = _router(h1, g, *router_wts)
        rank_t, counts = _rank(idx_t, tri)
        dest_t, *plan = _dest(idx_t, rank_t, counts)
        xs = _dispatch(index_chunks(dest_t, SC_CHUNK), xp.reshape(T_GRP, ROW_CHUNKS, LANES))
        groups.append((wts_t, index_chunks(dest_t, SC_CHUNK_G), xs, [a.reshape(-1) for a in plan]))

    ple_wts = [g_ffn_row, jnp.concatenate([w_sh_gate[0], w_sh_up[0]], axis=1).astype(BF16),
               w_sh_down[0].astype(BF16),
               row(g_ple[0]), w_ple_gate[0].astype(BF16), w_ple_proj[0].astype(BF16), row(g_final)]
    pp2d = p_prompt[0].reshape(T_P, PLE_DIM)
    ps2d = p_sample[0].reshape(T_S, PLE_DIM)
    y_p = y_s = None
    sub_chunks = T_SUB // SC_CHUNK_G
    for g, (wts_t, gather_chunks, xs, plan) in enumerate(groups):
        lo, hi = g * T_GRP, (g + 1) * T_GRP
        ys = _experts(*plan, xs.reshape(N_SLOTS * ROW_CHUNKS, LANES), wgu_bf, wd_bf)
        ys3 = ys.reshape(N_SLOTS, ROW_CHUNKS, LANES)
        for s in range(N_SUB):
            a, b = lo + s * T_SUB, lo + (s + 1) * T_SUB
            yt = _gather(gather_chunks[s * sub_chunks:(s + 1) * sub_chunks], ys3)
            yt = yt.reshape(TOP_K, T_SUB * ROW_CHUNKS, LANES)
            if a < T_P:
                n = min(b, T_P) - a
                y_p = _combine(yt, 0, wts_t, a - lo, h1, a, n, pp2d, a, ple_wts, y_p, T_P, a)
            if b > T_P:
                s0 = max(a, T_P)
                y_s = _combine(yt, s0 - a, wts_t, s0 - lo, h1, s0, b - s0, ps2d, s0 - T_P, ple_wts, y_s, T_S,
                               s0 - T_P)

    kv5 = lambda a, b: a.reshape(1, b, WINDOW, N_KV_HEADS, HEAD_DIM)
    k_prompt = kv5(k_p.reshape(BATCH, SEQ, KV_WIDTH)[:, SEQ - WINDOW:], BATCH)
    v_prompt = kv5(v_p.reshape(BATCH, SEQ, KV_WIDTH)[:, SEQ - WINDOW:], BATCH)
    pool_prompt = u_p.reshape(BATCH, SEQ, POOL_WIDTH)[:, SEQ - POOL_STATE:][None]
    return (y_p.reshape(BATCH, SEQ, D_MODEL), y_s.reshape(DEC_BATCH, DEC_SEQ, D_MODEL),
            k_prompt, v_prompt, pool_prompt,
            kv5(k_sample, DEC_BATCH), kv5(v_sample, DEC_BATCH), pool_sample)
```

```python
import functools

import numpy as np
import jax
import jax.numpy as jnp
from jax import lax
from jax.experimental import pallas as pl
from jax.experimental.pallas import tpu as pltpu
from jax.experimental.pallas import tpu_sc as plsc

F32 = jnp.float32
BF16 = jnp.bfloat16
U32 = jnp.uint32
I32 = jnp.int32

D_MODEL = 1024
BATCH = 8
SEQ = 2048
DEC_BATCH = 128
DEC_SEQ = 8
PAST_LEN = 16384
N_Q_HEADS = 8
N_KV_HEADS = 2
HEAD_DIM = 64
GQA_GROUP = N_Q_HEADS // N_KV_HEADS
ATT_WIDTH = N_Q_HEADS * HEAD_DIM
KV_WIDTH = N_KV_HEADS * HEAD_DIM
WINDOW = 128
ROPE_THETA = 500000.0
ROT_DIM = HEAD_DIM // 4
POOL_WINDOWS = (2, 4, 8, 16)
POOL_GROUPS = 4
POOL_WIDTH = D_MODEL - ATT_WIDTH
POOL_GROUP_DIM = POOL_WIDTH // POOL_GROUPS
POOL_STATE = 15
IN_WIDTH = ATT_WIDTH + 2 * KV_WIDTH + POOL_WIDTH
N_EXPERTS = 64
TOP_K = 8
N_EXPERT_GROUPS = 8
GROUP_SIZE = N_EXPERTS // N_EXPERT_GROUPS
TOPK_GROUPS = 4
D_EXPERT = 256
D_SHARED = 256
ROUTED_SCALE = 2.5
PLE_DIM = 256
EPS = 1e-6

T_P = BATCH * SEQ
T_S = DEC_BATCH * DEC_SEQ
T_ALL = T_P + T_S
HALF = D_MODEL // 2
LANES = 128
VMEM_LIMIT = 48 * 1024 * 1024

BM_IN = 1024
BQ = 2 * WINDOW
EXPERTS_PER_MIX_STEP = N_EXPERTS * BQ // T_P
assert EXPERTS_PER_MIX_STEP * T_P == N_EXPERTS * BQ
SB = 16
BM_R = 512
BT_DEST = 2176
BT_COMB = 256
N_GROUPS = 2
T_GRP = T_ALL // N_GROUPS
N_SUB = 2
T_SUB = T_GRP // N_SUB
assert T_GRP * N_GROUPS == T_ALL and T_SUB * N_SUB == T_GRP
BM_E = 1280
E_STRIP = 128
assert BM_E % E_STRIP == 0
N_ASSIGN = T_GRP * TOP_K
N_BLOCKS = -(-N_ASSIGN // BM_E) + N_EXPERTS
N_SLOTS = N_BLOCKS * BM_E

ROW_CHUNKS = HALF // LANES
SC_CORES = 2
SC_SUBCORES = 16
SC_WORKERS = SC_CORES * SC_SUBCORES
SC_CHUNK = 16
SC_CHUNK_G = 8
SC_RING = 4
assert T_GRP % (SC_WORKERS * SC_CHUNK) == 0 and T_SUB % (SC_WORKERS * SC_CHUNK_G) == 0


def _load_chunks(ref, n_rows, lead=()):
    return [ref[lead + (pl.ds(c, n_rows, stride=ROW_CHUNKS), slice(None))] for c in range(ROW_CHUNKS)]


def _store_chunks(ref, packed):
    n_rows = packed.shape[0]
    for c in range(ROW_CHUNKS):
        ref[pl.ds(c, n_rows, stride=ROW_CHUNKS), :] = packed[:, c * LANES:(c + 1) * LANES]


def _cparams(sem):
    return pltpu.CompilerParams(dimension_semantics=sem, vmem_limit_bytes=VMEM_LIMIT)


def _rms(x, g):
    return x * lax.rsqrt(jnp.mean(x * x, axis=-1, keepdims=True) + EPS) * g


def _sigmoid(x):
    return 1.0 / (1.0 + jnp.exp(-x))


def _pack_bf16_pairs(x):
    h = x.shape[-1] // 2
    return pltpu.pack_elementwise([x[:, :h], x[:, h:]], packed_dtype=BF16)


def _unpack_lo(p):
    return pltpu.bitcast(p << 16, F32)


def _unpack_hi(p):
    return pltpu.bitcast(p & jnp.uint32(0xFFFF0000), F32)


def _inproj_kernel(x_ref, g_ref, w_ref, c_ref, s1_ref, s2_ref, q_ref, k_ref, v_ref, u_ref):
    xn = _rms(x_ref[...], g_ref[...]).astype(BF16)
    z = jnp.dot(xn, w_ref[...], preferred_element_type=F32)
    c, s1, s2 = c_ref[...], s1_ref[...], s2_ref[...]

    def rope(t):
        return t * c + pltpu.roll(t, LANES - ROT_DIM // 2, 1) * s1 + pltpu.roll(t, ROT_DIM // 2, 1) * s2

    for i in range(ATT_WIDTH // LANES):
        sl = slice(i * LANES, (i + 1) * LANES)
        q_ref[:, sl] = (rope(z[:, sl]) * (HEAD_DIM ** -0.5)).astype(q_ref.dtype)
    k_ref[...] = rope(z[:, ATT_WIDTH:ATT_WIDTH + KV_WIDTH])
    v_ref[...] = z[:, ATT_WIDTH + KV_WIDTH:ATT_WIDTH + 2 * KV_WIDTH]
    u_ref[...] = z[:, ATT_WIDTH + 2 * KV_WIDTH:]


def _rope_tables(pos, reps=1):
    f32 = np.float32
    half = ROT_DIM // 2
    inv = np.power(f32(ROPE_THETA), -np.arange(half, dtype=f32) * f32(2.0) / f32(ROT_DIM)).astype(f32)
    ang = np.asarray(pos, f32)[:, None] * inv[None, :]
    cos, sin = np.cos(ang).astype(f32), np.sin(ang).astype(f32)
    n = len(pos)
    ones = np.ones((n, HEAD_DIM - ROT_DIM), f32)
    zeros = np.zeros((n, HEAD_DIM - ROT_DIM), f32)
    zh = np.zeros((n, half), f32)
    c = np.concatenate([cos, cos, ones], axis=1)
    s1 = np.concatenate([-sin, zh, zeros], axis=1)
    s2 = np.concatenate([zh, sin, zeros], axis=1)
    tile = lambda a: np.tile(a, (reps, LANES // HEAD_DIM))
    return tile(c), tile(s1), tile(s2)


def _inproj(x2d, g_mix, w_in_bf, tables, q_dtype):
    rows = x2d.shape[0]
    n_tab = tables[0].shape[0] // BM_IN
    row_spec = lambda w: pl.BlockSpec((BM_IN, w), lambda i: (i, 0))
    tab_spec = pl.BlockSpec((BM_IN, LANES), lambda i: (i % n_tab, 0))
    full = lambda a: pl.BlockSpec(a.shape, lambda i: (0,) * a.ndim)
    return pl.pallas_call(
        _inproj_kernel,
        grid=(rows // BM_IN,),
        in_specs=[row_spec(D_MODEL), full(g_mix), full(w_in_bf), tab_spec, tab_spec, tab_spec],
        out_specs=[row_spec(ATT_WIDTH), row_spec(KV_WIDTH), row_spec(KV_WIDTH), row_spec(POOL_WIDTH)],
        out_shape=[jax.ShapeDtypeStruct((rows, ATT_WIDTH), q_dtype),
                   jax.ShapeDtypeStruct((rows, KV_WIDTH), F32),
                   jax.ShapeDtypeStruct((rows, KV_WIDTH), F32),
                   jax.ShapeDtypeStruct((rows, POOL_WIDTH), F32)],
        compiler_params=_cparams(("parallel",)),
        name="inproj",
    )(x2d, g_mix, w_in_bf, *tables)


def _sink_column(sinks_ref, kv_head, rows_per_head):
    n = GQA_GROUP * rows_per_head
    grp = lax.broadcasted_iota(I32, (n, 1), 0) // rows_per_head
    col = jnp.full((n, 1), sinks_ref[kv_head * GQA_GROUP], F32)
    for g in range(1, GQA_GROUP):
        col = jnp.where(grp == g, sinks_ref[kv_head * GQA_GROUP + g], col)
    return col


def _band_mask(n_rows, rows_per_head, n_keys):
    i = lax.broadcasted_iota(I32, (n_rows, n_keys), 0) % rows_per_head
    c = lax.broadcasted_iota(I32, (n_rows, n_keys), 1)
    return (c >= i) & (c <= i + WINDOW), c


def _stack_heads(q, kv_head):
    return jnp.concatenate(
        [q[:, (kv_head * GQA_GROUP + g) * HEAD_DIM:(kv_head * GQA_GROUP + g + 1) * HEAD_DIM]
         for g in range(GQA_GROUP)], axis=0)


def _nt_dot(a, b):
    return lax.dot_general(a, b, (((1,), (1,)), ((), ())), preferred_element_type=F32)


POOL_HALO = 16
POOL_HEAD = 8
assert all(w == 2 << g for g, w in enumerate(POOL_WINDOWS)) and POOL_WINDOWS[-1] - 1 <= POOL_HALO


def _window_sums(ext_ref, n):
    lo, hi = POOL_HEAD, POOL_HEAD + POOL_HALO + n
    for p in range(POOL_GROUPS):
        lanes = slice(p * POOL_GROUP_DIM, POOL_WIDTH)
        ext_ref[lo:hi, lanes] = ext_ref[lo:hi, lanes] + ext_ref[lo - (1 << p):hi - (1 << p), lanes]


def _pool_out(d, wpool_ref, pscale_ref, gpool_ref):
    parts = [jnp.dot(d[:, g * POOL_GROUP_DIM:(g + 1) * POOL_GROUP_DIM].astype(BF16), wpool_ref[g],
                     preferred_element_type=F32) for g in range(POOL_GROUPS)]
    return _rms(jnp.concatenate(parts, axis=-1) * pscale_ref[...], gpool_ref[...])


def _mixer_tail(o_att, pooled, h, gatt_ref, wout_ref):
    mixed = jnp.concatenate([_rms(o_att, gatt_ref[...]), pooled], axis=-1)
    return h + jnp.dot(mixed.astype(BF16), wout_ref[...], preferred_element_type=F32)


def _mixer_prompt_kernel(sinks_ref, h_ref, q_ref, kc_ref, kp_ref, vc_ref, vp_ref, uc_ref, up_ref,
                         wpool_ref, pscale_ref, gatt_ref, gpool_ref, wout_ref, weg_ref, weu_ref, wed_ref,
                         h1_ref, wgu_bf_ref, wd_bf_ref, uext_ref):
    j = pl.program_id(1)
    wgu_bf_ref[:, :, :D_EXPERT] = weg_ref[...].astype(BF16)
    wgu_bf_ref[:, :, D_EXPERT:] = weu_ref[...].astype(BF16)
    wd_bf_ref[...] = wed_ref[...].astype(BF16)

    u = uc_ref[...]
    base = POOL_HEAD + POOL_HALO
    uext_ref[0:POOL_HEAD, :] = jnp.zeros((POOL_HEAD, POOL_WIDTH), F32)
    uext_ref[POOL_HEAD:base, :] = jnp.where(j > 0, up_ref[...], 0.0)
    uext_ref[base:base + BQ, :] = u
    _window_sums(uext_ref, BQ)
    pos = j * BQ + lax.broadcasted_iota(I32, (BQ, 1), 0)
    parts = []
    for g, w in enumerate(POOL_WINDOWS):
        sl = slice(g * POOL_GROUP_DIM, (g + 1) * POOL_GROUP_DIM)
        parts.append(uext_ref[base:base + BQ, sl] / jnp.minimum(pos + 1, w).astype(F32) - u[:, sl])
    pooled = _pool_out(jnp.concatenate(parts, axis=-1), wpool_ref, pscale_ref, gpool_ref)

    q = q_ref[...]
    k_all = jnp.concatenate([kp_ref[...], kc_ref[...]], axis=0).astype(BF16)
    v_all = jnp.concatenate([vp_ref[...], vc_ref[...]], axis=0).astype(BF16)
    ones = jnp.ones((WINDOW + BQ, HEAD_DIM), BF16)
    v_ones = [jnp.concatenate([v_all[:, hk * HEAD_DIM:(hk + 1) * HEAD_DIM], ones], axis=1)
              for hk in range(N_KV_HEADS)]
    band, col = _band_mask(GQA_GROUP * WINDOW, WINDOW, 2 * WINDOW)
    sinks = [_sink_column(sinks_ref, hk, WINDOW) for hk in range(N_KV_HEADS)]
    bands = []
    for b in range(BQ // WINDOW):
        rows = slice(b * WINDOW, (b + 1) * WINDOW)
        keys = slice(b * WINDOW, (b + 2) * WINDOW)
        mask = band & ((col >= WINDOW) | (j > 0)) if b == 0 else band
        heads = []
        for hk in range(N_KV_HEADS):
            sl = slice(hk * HEAD_DIM, (hk + 1) * HEAD_DIM)
            s = jnp.where(mask, _nt_dot(_stack_heads(q[rows], hk), k_all[keys, sl]), -jnp.inf)
            m = jnp.maximum(jnp.max(s, axis=-1, keepdims=True), sinks[hk])
            e = jnp.exp(s - m).astype(BF16)
            ov = jnp.dot(e, v_ones[hk][keys], preferred_element_type=F32)
            den = ov[:, HEAD_DIM:HEAD_DIM + 1] + jnp.exp(sinks[hk] - m)
            o = ov[:, :HEAD_DIM] / den
            heads += [o[g * WINDOW:(g + 1) * WINDOW] for g in range(GQA_GROUP)]
        bands.append(jnp.concatenate(heads, axis=-1))
    o_att = jnp.concatenate(bands, axis=0)
    h1_ref[...] = _mixer_tail(o_att, pooled, h_ref[...], gatt_ref, wout_ref)


def _mixer_sample_kernel(sinks_ref, h_ref, q_ref, kn_ref, vn_ref, u_ref, ck_ref, cv_ref, st_ref,
                         wpool_ref, pscale_ref, gatt_ref, gpool_ref, wout_ref, h1_in_ref,
                         h1_ref, ko_ref, vo_ref, po_ref, uext_ref):
    del h1_in_ref
    n_q = GQA_GROUP * DEC_SEQ
    n_keys = 2 * WINDOW
    band, col = _band_mask(n_q, DEC_SEQ, n_keys)
    mask = (band & (col < WINDOW + DEC_SEQ))[None]
    q3, kn3, vn3, u3 = q_ref[...], kn_ref[...], vn_ref[...], u_ref[...]
    ck, cv = ck_ref[...], cv_ref[...]
    ko_ref[:, 0:WINDOW - DEC_SEQ, :] = ck[:, DEC_SEQ:, :]
    ko_ref[:, WINDOW - DEC_SEQ:WINDOW, :] = kn3
    vo_ref[:, 0:WINDOW - DEC_SEQ, :] = cv[:, DEC_SEQ:, :]
    vo_ref[:, WINDOW - DEC_SEQ:WINDOW, :] = vn3
    pad = jnp.zeros((SB, WINDOW - DEC_SEQ, KV_WIDTH), F32)
    k_all = jnp.concatenate([ck, kn3, pad], axis=1).astype(BF16)
    v_all = jnp.concatenate([cv, vn3, pad], axis=1).astype(BF16)
    heads = []
    for hk in range(N_KV_HEADS):
        sl = slice(hk * HEAD_DIM, (hk + 1) * HEAD_DIM)
        qs = jnp.concatenate(
            [q3[:, :, (hk * GQA_GROUP + g) * HEAD_DIM:(hk * GQA_GROUP + g + 1) * HEAD_DIM]
             for g in range(GQA_GROUP)], axis=1).astype(BF16)
        sink = _sink_column(sinks_ref, hk, DEC_SEQ)[None]
        s = jnp.einsum("bqd,bkd->bqk", qs, k_all[:, :, sl], preferred_element_type=F32)
        s = jnp.where(mask, s, -jnp.inf)
        m = jnp.maximum(jnp.max(s, axis=-1, keepdims=True), sink)
        e = jnp.exp(s - m)
        den = jnp.sum(e, axis=-1, keepdims=True) + jnp.exp(sink - m)
        o = jnp.einsum("bqk,bkd->bqd", e.astype(BF16), v_all[:, :, sl], preferred_element_type=F32) / den
        heads += [o[:, g * DEC_SEQ:(g + 1) * DEC_SEQ, :] for g in range(GQA_GROUP)]
    o_att = jnp.concatenate(heads, axis=-1).reshape(SB * DEC_SEQ, ATT_WIDTH)

    uext_ref[:, 1:16, :] = st_ref[...]
    uext_ref[:, 16:16 + DEC_SEQ, :] = u3
    parts = []
    for g, w in enumerate(POOL_WINDOWS):
        sl = slice(g * POOL_GROUP_DIM, (g + 1) * POOL_GROUP_DIM)
        acc = u3[:, :, sl]
        for back in range(1, w):
            acc = acc + uext_ref[:, 16 - back:16 - back + DEC_SEQ, sl]
        parts.append(acc / float(w) - u3[:, :, sl])
    d = jnp.concatenate(parts, axis=-1).reshape(SB * DEC_SEQ, POOL_WIDTH)
    po_ref[...] = uext_ref[:, 16 + DEC_SEQ - POOL_STATE:16 + DEC_SEQ, :]
    pooled = _pool_out(d, wpool_ref, pscale_ref, gpool_ref)
    h1_ref[...] = _mixer_tail(o_att, pooled, h_ref[...], gatt_ref, wout_ref)


def _full_spec(a, n_grid):
    nd = a.ndim
    return pl.BlockSpec(a.shape, lambda *_: (0,) * nd)


def _mixer_prompt(sinks, x2d, q, k, v, u, wts, w_exp):
    nb = SEQ // BQ
    row = lambda w: pl.BlockSpec((BQ, w), lambda b, j: (b * nb + j, 0))
    prev = lambda w: pl.BlockSpec(
        (WINDOW, w), lambda b, j: (jnp.maximum((b * nb + j) * (BQ // WINDOW) - 1, 0), 0))
    uprev = pl.BlockSpec((POOL_HALO, POOL_WIDTH),
                         lambda b, j: (jnp.maximum((b * nb + j) * (BQ // POOL_HALO) - 1, 0), 0))
    smem = pl.BlockSpec(memory_space=pltpu.SMEM)
    per_step = lambda r, c: pl.BlockSpec((EXPERTS_PER_MIX_STEP, r, c), lambda b, j: (b * nb + j, 0, 0))
    return pl.pallas_call(
        _mixer_prompt_kernel,
        grid=(BATCH, nb),
        in_specs=[smem, row(D_MODEL), row(ATT_WIDTH), row(KV_WIDTH), prev(KV_WIDTH), row(KV_WIDTH),
                  prev(KV_WIDTH), row(POOL_WIDTH), uprev] + [_full_spec(w, 2) for w in wts]
                 + [per_step(D_MODEL, D_EXPERT), per_step(D_MODEL, D_EXPERT), per_step(D_EXPERT, D_MODEL)],
        out_specs=[row(D_MODEL), per_step(D_MODEL, 2 * D_EXPERT), per_step(D_EXPERT, D_MODEL)],
        out_shape=[jax.ShapeDtypeStruct((T_ALL, D_MODEL), F32),
                   jax.ShapeDtypeStruct((N_EXPERTS, D_MODEL, 2 * D_EXPERT), BF16),
                   jax.ShapeDtypeStruct((N_EXPERTS, D_EXPERT, D_MODEL), BF16)],
        scratch_shapes=[pltpu.VMEM((POOL_HEAD + POOL_HALO + BQ, POOL_WIDTH), F32)],
        compiler_params=_cparams(("parallel", "parallel")),
        name="mixer_prompt",
    )(sinks, x2d, q, k, k, v, v, u, u, *wts, *w_exp)


def _mixer_sample(sinks, x2d, q, k, v, u, cache_k, cache_v, state, wts, h1_buf):
    rows = SB * DEC_SEQ
    row = lambda w: pl.BlockSpec((rows, w), lambda i: (i, 0))
    bat = lambda a: pl.BlockSpec((SB,) + a.shape[1:], lambda i: (i, 0, 0))
    smem = pl.BlockSpec(memory_space=pltpu.SMEM)
    h1_blocks_before = T_P // rows
    n_in = 9 + len(wts)
    q, k, v, u = (a.reshape(DEC_BATCH, DEC_SEQ, a.shape[-1]) for a in (q, k, v, u))
    st = pl.BlockSpec((None, SB) + state.shape[2:], lambda i: (0, i, 0, 0))
    return pl.pallas_call(
        _mixer_sample_kernel,
        grid=(DEC_BATCH // SB,),
        in_specs=[smem, row(D_MODEL), bat(q), bat(k), bat(v), bat(u),
                  bat(cache_k), bat(cache_v), st] + [_full_spec(w, 1) for w in wts]
                 + [pl.BlockSpec(memory_space=pl.ANY)],
        out_specs=[pl.BlockSpec((rows, D_MODEL), lambda i: (h1_blocks_before + i, 0)),
                   bat(cache_k), bat(cache_v), st],
        out_shape=[jax.ShapeDtypeStruct((T_ALL, D_MODEL), F32),
                   jax.ShapeDtypeStruct(cache_k.shape, F32),
                   jax.ShapeDtypeStruct(cache_v.shape, F32),
                   jax.ShapeDtypeStruct(state.shape, F32)],
        scratch_shapes=[pltpu.VMEM((SB, 16 + DEC_SEQ, POOL_WIDTH), F32)],
        input_output_aliases={n_in: 0},
        compiler_params=_cparams(("parallel",)),
        name="mixer_sample",
    )(sinks, x2d, q, k, v, u, cache_k, cache_v, state, *wts, h1_buf)


def _first_max(vals, iota, n):
    m = jnp.max(vals, axis=0, keepdims=True)
    idx = jnp.min(jnp.where(vals == m, iota, n), axis=0, keepdims=True)
    return m, idx


def _router_kernel(h1_ref, gffn_ref, wrt_ref, bias_ref, tri_ref, xp_ref, idx_ref, wts_ref, rank_ref, cnt_ref,
                   carry_ref):
    xn = _rms(h1_ref[...], gffn_ref[...])
    w = wrt_ref[...]
    w_hi, x_hi = w.astype(BF16), xn.astype(BF16)
    w_lo, x_lo = (w - w_hi.astype(F32)).astype(BF16), (xn - x_hi.astype(F32)).astype(BF16)
    logits = _nt_dot(w_hi, x_hi) + (_nt_dot(w_hi, x_lo) + _nt_dot(w_lo, x_hi))
    scores = _sigmoid(logits)
    biased = scores + bias_ref[...]
    n_tok = biased.shape[1]
    neg = -jnp.inf

    iota_g = lax.broadcasted_iota(I32, (GROUP_SIZE, n_tok), 0)
    grp_rows = []
    for g in range(N_EXPERT_GROUPS):
        blk = biased[g * GROUP_SIZE:(g + 1) * GROUP_SIZE, :]
        top1, i1 = _first_max(blk, iota_g, GROUP_SIZE)
        top2 = jnp.max(jnp.where(iota_g == i1, neg, blk), axis=0, keepdims=True)
        grp_rows.append(top1 + top2)
    gs = jnp.concatenate(grp_rows, axis=0)

    iota_n = lax.broadcasted_iota(I32, (N_EXPERT_GROUPS, n_tok), 0)
    gsel = jnp.zeros((N_EXPERT_GROUPS, n_tok), jnp.bool_)
    for _ in range(TOPK_GROUPS):
        _, gi = _first_max(gs, iota_n, N_EXPERT_GROUPS)
        hit = iota_n == gi
        gsel = gsel | hit
        gs = jnp.where(hit, neg, gs)
    emask = jnp.concatenate(
        [jnp.broadcast_to(gsel[g:g + 1, :], (GROUP_SIZE, n_tok)) for g in range(N_EXPERT_GROUPS)], axis=0)
    masked = jnp.where(emask, biased, neg)

    iota_e = lax.broadcasted_iota(I32, (N_EXPERTS, n_tok), 0)
    idx_rows, sel_rows = [], []
    for _ in range(TOP_K):
        _, ei = _first_max(masked, iota_e, N_EXPERTS)
        hit = iota_e == ei
        idx_rows.append(ei)
        sel_rows.append(jnp.sum(jnp.where(hit, scores, 0.0), axis=0, keepdims=True))
        masked = jnp.where(hit, neg, masked)
    sel = jnp.concatenate(sel_rows, axis=0)
    idx_ref[...] = jnp.concatenate(idx_rows, axis=0)
    wts_ref[...] = sel / jnp.sum(sel, axis=0, keepdims=True) * ROUTED_SCALE
    _store_chunks(xp_ref, _pack_bf16_pairs(xn))

    @pl.when(pl.program_id(0) == 0)
    def _():
        carry_ref[...] = jnp.zeros_like(carry_ref)

    member = jnp.zeros((N_EXPERTS, n_tok), F32)
    for ei in idx_rows:
        member = member + jnp.where(iota_e == ei, 1.0, 0.0)
    before = jnp.dot(member.astype(BF16), tri_ref[...], preferred_element_type=F32) + carry_ref[...]
    rank_ref[...] = jnp.concatenate(
        [jnp.sum(jnp.where(iota_e == ei, before, 0.0), axis=0, keepdims=True) for ei in idx_rows],
        axis=0).astype(I32)
    carry_ref[...] = carry_ref[...] + jnp.sum(member, axis=1, keepdims=True)
    cnt_ref[...] = carry_ref[...].astype(I32)


def _router(h1, group, g_ffn, w_router_t, bias_col, tri):
    blk0 = group * T_GRP // BM_R
    colblk = pl.BlockSpec((TOP_K, BM_R), lambda i: (0, i))
    ws = [g_ffn, w_router_t, bias_col, tri]
    return pl.pallas_call(
        _router_kernel,
        grid=(T_GRP // BM_R,),
        in_specs=[pl.BlockSpec((BM_R, D_MODEL), lambda i: (blk0 + i, 0))] + [_full_spec(w, 1) for w in ws],
        out_specs=[pl.BlockSpec((BM_R * ROW_CHUNKS, LANES), lambda i: (i, 0)), colblk, colblk, colblk,
                   pl.BlockSpec((N_EXPERTS, 1), lambda i: (0, 0))],
        out_shape=[jax.ShapeDtypeStruct((T_GRP * ROW_CHUNKS, LANES), U32),
                   jax.ShapeDtypeStruct((TOP_K, T_GRP), I32),
                   jax.ShapeDtypeStruct((TOP_K, T_GRP), F32),
                   jax.ShapeDtypeStruct((TOP_K, T_GRP), I32),
                   jax.ShapeDtypeStruct((N_EXPERTS, 1), I32)],
        scratch_shapes=[pltpu.VMEM((N_EXPERTS, 1), F32)],
        compiler_params=_cparams(("arbitrary",)),
        name="router",
    )(h1, *ws)


def _dest_kernel(idx_ref, rank_ref, cnt_ref, dest_ref, blk_e_ref, n_used_ref, blk_rows_ref):
    counts = cnt_ref[...]
    padded = (counts + (BM_E - 1)) // BM_E * BM_E
    r = lax.broadcasted_iota(I32, (N_EXPERTS, N_EXPERTS), 0)
    c = lax.broadcasted_iota(I32, (N_EXPERTS, N_EXPERTS), 1)
    padded_row = jnp.sum(jnp.where(r == c, padded, 0), axis=0, keepdims=True)
    pad_start = jnp.sum(jnp.where(c < r, padded_row, 0), axis=1, keepdims=True)

    idx = idx_ref[...]
    n_tok = idx.shape[1]
    iota_e = lax.broadcasted_iota(I32, (N_EXPERTS, n_tok), 0)
    rows = [jnp.sum(jnp.where(iota_e == idx[k:k + 1, :], pad_start, 0), axis=0, keepdims=True)
            for k in range(TOP_K)]
    dest_ref[...] = jnp.concatenate(rows, axis=0) + rank_ref[...]

    @pl.when(pl.program_id(0) == 0)
    def _():
        pad_end_row = jnp.sum(jnp.where(r <= c, padded, 0), axis=0, keepdims=True)
        b0 = lax.broadcasted_iota(I32, (N_BLOCKS_PAD, N_EXPERTS), 0) * BM_E
        be = jnp.minimum(jnp.sum(jnp.where(pad_end_row <= b0, 1, 0), axis=1, keepdims=True), N_EXPERTS - 1)
        blk_e_ref[...] = be
        n_used_ref[...] = pad_end_row[:, N_EXPERTS - 1:N_EXPERTS] // BM_E
        counts_row = jnp.sum(jnp.where(r == c, counts, 0), axis=0, keepdims=True)
        mine = lax.broadcasted_iota(I32, (N_BLOCKS_PAD, N_EXPERTS), 1) == be
        end_valid = jnp.sum(jnp.where(mine, pad_end_row - padded_row + counts_row, 0), axis=1, keepdims=True)
        blk_rows_ref[...] = jnp.clip(end_valid - b0[:, :1], 0, BM_E)


N_BLOCKS_PAD = (N_BLOCKS + 7) // 8 * 8


def _dest(idx_t, rank_t, counts):
    blk = pl.BlockSpec((TOP_K, BT_DEST), lambda i: (0, i))
    one = lambda s: pl.BlockSpec(s, lambda i: (0, 0))
    return pl.pallas_call(
        _dest_kernel,
        grid=(T_GRP // BT_DEST,),
        in_specs=[blk, blk, one((N_EXPERTS, 1))],
        out_specs=[blk, one((N_BLOCKS_PAD, 1)), one((1, 1)), one((N_BLOCKS_PAD, 1))],
        out_shape=[jax.ShapeDtypeStruct((TOP_K, T_GRP), I32),
                   jax.ShapeDtypeStruct((N_BLOCKS_PAD, 1), I32),
                   jax.ShapeDtypeStruct((1, 1), I32),
                   jax.ShapeDtypeStruct((N_BLOCKS_PAD, 1), I32)],
        compiler_params=_cparams(("arbitrary",)),
        name="dest",
    )(idx_t, rank_t, counts)


def _sc_mesh():
    return plsc.VectorSubcoreMesh(core_axis_name="c", subcore_axis_name="s")


def _sc_worker_id():
    return lax.axis_index("s") * SC_CORES + lax.axis_index("c")


def _dispatch_body(dest_hbm, xp_hbm, xs_hbm, idx_v, rows_v, sem_in, sem_out):
    n_chunks, _, n_tok = dest_hbm.shape
    per_worker = n_chunks // SC_WORKERS
    chunk0 = _sc_worker_id() * per_worker

    def loads(i):
        chunk = chunk0 + i
        t0 = pl.multiple_of(chunk * n_tok, n_tok)
        return (pltpu.make_async_copy(dest_hbm.at[chunk], idx_v.at[i % 2], sem_in.at[i % 2]),
                pltpu.make_async_copy(xp_hbm.at[pl.ds(t0, n_tok)], rows_v.at[i % 2], sem_in.at[i % 2]))

    def scatters(i):
        return [pltpu.make_async_copy(rows_v.at[i % 2], xs_hbm.at[idx_v.at[i % 2, k]], sem_out.at[i % 2])
                for k in range(TOP_K)]

    for cp in loads(0):
        cp.start()
    for i in range(per_worker):
        for cp in loads(i):
            cp.wait()
        if i >= 1:
            for cp in scatters(i - 1):
                cp.wait()
        if i + 1 < per_worker:
            for cp in loads(i + 1):
                cp.start()
        for cp in scatters(i):
            cp.start()
    for cp in scatters(per_worker - 1):
        cp.wait()


def _dispatch(dest_chunks, xp3):
    return pl.kernel(
        _dispatch_body,
        out_type=jax.ShapeDtypeStruct((N_SLOTS, ROW_CHUNKS, LANES), U32),
        mesh=_sc_mesh(),
        scratch_types=[pltpu.VMEM((2, TOP_K, SC_CHUNK), I32),
                       pltpu.VMEM((2, SC_CHUNK, ROW_CHUNKS, LANES), U32),
                       pltpu.SemaphoreType.DMA((2,)), pltpu.SemaphoreType.DMA((2,))],
        name="dispatch",
    )(dest_chunks, xp3)


def _gather_body(dest_hbm, ys_hbm, yt_hbm, idx_v, rows_v, sem_in, sem_out):
    n_chunks, _, n_tok = dest_hbm.shape
    per_worker = n_chunks // SC_WORKERS
    chunk0 = _sc_worker_id() * per_worker

    @pl.loop(0, per_worker)
    def _(i):
        chunk = chunk0 + i
        t0 = pl.multiple_of(chunk * n_tok, n_tok)
        pltpu.sync_copy(dest_hbm.at[chunk], idx_v)

        def gather(k):
            return pltpu.make_async_copy(ys_hbm.at[idx_v.at[k]], rows_v.at[k % SC_RING], sem_in.at[k % SC_RING])

        def store(k):
            return pltpu.make_async_copy(rows_v.at[k % SC_RING], yt_hbm.at[k, pl.ds(t0, n_tok)],
                                         sem_out.at[k % SC_RING])

        for k in range(SC_RING):
            gather(k).start()
        for k in range(TOP_K):
            gather(k).wait()
            store(k).start()
            if k + SC_RING < TOP_K:
                store(k).wait()
                gather(k + SC_RING).start()
        for k in range(TOP_K - SC_RING, TOP_K):
            store(k).wait()


def _gather(dest_chunks, ys3):
    n_chunks, _, n_tok = dest_chunks.shape
    assert n_chunks % SC_WORKERS == 0
    return pl.kernel(
        _gather_body,
        out_type=jax.ShapeDtypeStruct((TOP_K, n_chunks * n_tok, ROW_CHUNKS, LANES), U32),
        mesh=_sc_mesh(),
        scratch_types=[pltpu.VMEM((TOP_K, n_tok), I32),
                       pltpu.VMEM((SC_RING, n_tok, ROW_CHUNKS, LANES), U32),
                       pltpu.SemaphoreType.DMA((SC_RING,)), pltpu.SemaphoreType.DMA((SC_RING,))],
        name="gather",
    )(dest_chunks, ys3)


def _experts_kernel(blk_e_ref, n_used_ref, blk_rows_ref, xs_ref, wgu_ref, wd_ref, ys_ref):
    del blk_e_ref
    b = pl.program_id(0)

    def swiglu_rows(n_rows):
        chunks = _load_chunks(xs_ref, n_rows)
        x_lo = jnp.concatenate([_unpack_lo(p) for p in chunks], axis=-1).astype(BF16)
        x_hi = jnp.concatenate([_unpack_hi(p) for p in chunks], axis=-1).astype(BF16)
        gu = (jnp.dot(x_lo, wgu_ref[0, :HALF, :], preferred_element_type=F32)
              + jnp.dot(x_hi, wgu_ref[0, HALF:, :], preferred_element_type=F32))
        gate, up = gu[:, :D_EXPERT], gu[:, D_EXPERT:]
        hmid = (gate * _sigmoid(gate) * up).astype(BF16)
        _store_chunks(ys_ref, _pack_bf16_pairs(jnp.dot(hmid, wd_ref[0], preferred_element_type=F32)))

    @pl.when(b < n_used_ref[0])
    def _():
        valid = blk_rows_ref[b]
        for n_rows in range(E_STRIP, BM_E + 1, E_STRIP):
            @pl.when((valid > n_rows - E_STRIP) & (valid <= n_rows))
            def _(n_rows=n_rows):
                swiglu_rows(n_rows)


def _experts(blk_e, n_used, blk_rows, xs, wgu_bf, wd_bf):
    def blk(b, be, nu, nr):
        return jnp.minimum(b, nu[0] - 1)

    def by_expert(shape):
        return pl.BlockSpec((1,) + shape, lambda b, be, nu, nr: (be[blk(b, be, nu, nr)], 0, 0))

    tile = pl.BlockSpec((BM_E * ROW_CHUNKS, LANES), lambda b, be, nu, nr: (blk(b, be, nu, nr), 0))
    grid_spec = pltpu.PrefetchScalarGridSpec(
        num_scalar_prefetch=3,
        grid=(N_BLOCKS,),
        in_specs=[tile, by_expert((D_MODEL, 2 * D_EXPERT)), by_expert((D_EXPERT, D_MODEL))],
        out_specs=tile,
    )
    return pl.pallas_call(
        _experts_kernel,
        grid_spec=grid_spec,
        out_shape=jax.ShapeDtypeStruct((N_SLOTS * ROW_CHUNKS, LANES), U32),
        compiler_params=_cparams(("arbitrary",)),
        name="experts",
    )(blk_e, n_used, blk_rows, xs, wgu_bf, wd_bf)


def _combine_kernel(yt_ref, wts_ref, h1_ref, p_ref, gffn_ref, wsgu_ref, wsd_ref, gple_ref, wpg_ref, wpp_ref,
                    gfin_ref, *y_refs):
    y_ref = y_refs[-1]
    h1 = h1_ref[...]
    gu = jnp.dot(_rms(h1, gffn_ref[...]).astype(BF16), wsgu_ref[...], preferred_element_type=F32)
    sgate, sup = gu[:, :D_SHARED], gu[:, D_SHARED:]
    hsh = h1 + jnp.dot((sgate * _sigmoid(sgate) * sup).astype(BF16), wsd_ref[...], preferred_element_type=F32)
    wts = jnp.transpose(wts_ref[...])
    lo = [jnp.zeros((BT_COMB, LANES), F32) for _ in range(ROW_CHUNKS)]
    hi = [jnp.zeros((BT_COMB, LANES), F32) for _ in range(ROW_CHUNKS)]
    for k in range(TOP_K):
        w = wts[:, k:k + 1]
        for c, p in enumerate(_load_chunks(yt_ref, BT_COMB, lead=(k,))):
            lo[c] = lo[c] + w * _unpack_lo(p)
            hi[c] = hi[c] + w * _unpack_hi(p)
    h2 = hsh + jnp.concatenate(lo + hi, axis=-1)
    gate = _sigmoid(jnp.dot(_rms(h2, gple_ref[...]).astype(BF16), wpg_ref[...], preferred_element_type=F32))
    proj = jnp.dot(p_ref[...].astype(BF16), wpp_ref[...], preferred_element_type=F32)
    y_ref[...] = _rms(h2 + proj * gate, gfin_ref[...])


def _combine(yt, yt_row0, wts_t, wts_row0, h1, tok_row0, n_rows, p2d, p_row0, ws, y_prev, out_rows, out_row0):
    assert all(r % BT_COMB == 0 for r in (yt_row0, wts_row0, tok_row0, n_rows, p_row0, out_row0))
    g0, w0, t0, p0, o0 = (r // BT_COMB for r in (yt_row0, wts_row0, tok_row0, p_row0, out_row0))
    in_specs = [pl.BlockSpec((TOP_K, BT_COMB * ROW_CHUNKS, LANES), lambda i: (0, g0 + i, 0)),
                pl.BlockSpec((TOP_K, BT_COMB), lambda i: (0, w0 + i)),
                pl.BlockSpec((BT_COMB, D_MODEL), lambda i: (t0 + i, 0)),
                pl.BlockSpec((BT_COMB, PLE_DIM), lambda i: (p0 + i, 0))] + [_full_spec(w, 1) for w in ws]
    args = [yt, wts_t, h1, p2d, *ws]
    aliases = {}
    if y_prev is not None:
        in_specs.append(pl.BlockSpec(memory_space=pl.ANY))
        aliases = {len(args): 0}
        args.append(y_prev)
    return pl.pallas_call(
        _combine_kernel,
        grid=(n_rows // BT_COMB,),
        in_specs=in_specs,
        out_specs=pl.BlockSpec((BT_COMB, D_MODEL), lambda i: (o0 + i, 0)),
        out_shape=jax.ShapeDtypeStruct((out_rows, D_MODEL), F32),
        input_output_aliases=aliases,
        compiler_params=_cparams(("parallel",)),
        name="combine",
    )(*args)


def kernel(x_prompt, x_sample, cache_k, cache_v, state_pool, p_prompt, p_sample, g_mix, w_in, attn_sinks,
           w_pool, pool_scale, g_att_out, g_pool_out, w_out, g_ffn, w_router, router_bias, w_exp_gate,
           w_exp_up, w_exp_down, w_sh_gate, w_sh_up, w_sh_down, g_ple, w_ple_gate, w_ple_proj, g_final):
    row = lambda a: a.reshape(1, -1)
    xp2d = x_prompt.reshape(T_P, D_MODEL)
    xs2d = x_sample.reshape(T_S, D_MODEL)
    w_in_bf = w_in[0].astype(BF16)
    mixer_wts = [w_pool[0].astype(BF16), row(pool_scale[0]), row(g_att_out[0]), row(g_pool_out[0]),
                 w_out[0].astype(BF16)]

    tab_p = _rope_tables(np.arange(SEQ))
    tab_s = _rope_tables(PAST_LEN + np.arange(DEC_SEQ), reps=BM_IN // DEC_SEQ)

    q_p, k_p, v_p, u_p = _inproj(xp2d, row(g_mix[0]), w_in_bf, tab_p, BF16)
    q_s, k_s, v_s, u_s = _inproj(xs2d, row(g_mix[0]), w_in_bf, tab_s, F32)

    h1, wgu_bf, wd_bf = _mixer_prompt(attn_sinks[0], xp2d, q_p, k_p, v_p, u_p, mixer_wts,
                                      (w_exp_gate[0], w_exp_up[0], w_exp_down[0]))
    h1, k_sample, v_sample, pool_sample = _mixer_sample(
        attn_sinks[0], xs2d, q_s, k_s, v_s, u_s,
        cache_k[0].reshape(DEC_BATCH, WINDOW, KV_WIDTH), cache_v[0].reshape(DEC_BATCH, WINDOW, KV_WIDTH),
        state_pool, mixer_wts, h1)

    g_ffn_row = row(g_ffn[0])
    tri = jnp.asarray(np.triu(np.ones((BM_R, BM_R), np.float32), k=1), BF16)
    router_wts = (g_ffn_row, w_router[0].T, router_bias[0].reshape(N_EXPERTS, 1), tri)

    def index_chunks(d, n_tok):
        return d.reshape(TOP_K, d.shape[1] // n_tok, n_tok).transpose(1, 0, 2)

    groups = []
    for g in range(N_GROUPS):
        xp, idx_t, wts_t, rank_t, counts = _router(h1, g, *router_wts)
        dest_t, *plan = _dest(idx_t, rank_t, counts)
        xs = _dispatch(index_chunks(dest_t, SC_CHUNK), xp.reshape(T_GRP, ROW_CHUNKS, LANES))
        groups.append((wts_t, index_chunks(dest_t, SC_CHUNK_G), xs, [a.reshape(-1) for a in plan]))

    ple_wts = [g_ffn_row, jnp.concatenate([w_sh_gate[0], w_sh_up[0]], axis=1).astype(BF16),
               w_sh_down[0].astype(BF16),
               row(g_ple[0]), w_ple_gate[0].astype(BF16), w_ple_proj[0].astype(BF16), row(g_final)]
    pp2d = p_prompt[0].reshape(T_P, PLE_DIM)
    ps2d = p_sample[0].reshape(T_S, PLE_DIM)
    y_p = y_s = None
    sub_chunks = T_SUB // SC_CHUNK_G
    for g, (wts_t, gather_chunks, xs, plan) in enumerate(groups):
        lo, hi = g * T_GRP, (g + 1) * T_GRP
        ys = _experts(*plan, xs.reshape(N_SLOTS * ROW_CHUNKS, LANES), wgu_bf, wd_bf)
        ys3 = ys.reshape(N_SLOTS, ROW_CHUNKS, LANES)
        for s in range(N_SUB):
            a, b = lo + s * T_SUB, lo + (s + 1) * T_SUB
            yt = _gather(gather_chunks[s * sub_chunks:(s + 1) * sub_chunks], ys3)
            yt = yt.reshape(TOP_K, T_SUB * ROW_CHUNKS, LANES)
            if a < T_P:
                n = min(b, T_P) - a
                y_p = _combine(yt, 0, wts_t, a - lo, h1, a, n, pp2d, a, ple_wts, y_p, T_P, a)
            if b > T_P:
                s0 = max(a, T_P)
                y_s = _combine(yt, s0 - a, wts_t, s0 - lo, h1, s0, b - s0, ps2d, s0 - T_P, ple_wts, y_s, T_S,
                               s0 - T_P)

    kv5 = lambda a, b: a.reshape(1, b, WINDOW, N_KV_HEADS, HEAD_DIM)
    k_prompt = kv5(k_p.reshape(BATCH, SEQ, KV_WIDTH)[:, SEQ - WINDOW:], BATCH)
    v_prompt = kv5(v_p.reshape(BATCH, SEQ, KV_WIDTH)[:, SEQ - WINDOW:], BATCH)
    pool_prompt = u_p.reshape(BATCH, SEQ, POOL_WIDTH)[:, SEQ - POOL_STATE:][None]
    return (y_p.reshape(BATCH, SEQ, D_MODEL), y_s.reshape(DEC_BATCH, DEC_SEQ, D_MODEL),
            k_prompt, v_prompt, pool_prompt,
            kv5(k_sample, DEC_BATCH), kv5(v_sample, DEC_BATCH), pool_sample)
```

```python
import functools

import numpy as np
import jax
import jax.numpy as jnp
from jax import lax
from jax.experimental import pallas as pl
from jax.experimental.pallas import tpu as pltpu
from jax.experimental.pallas import tpu_sc as plsc

F32 = jnp.float32
BF16 = jnp.bfloat16
U32 = jnp.uint32
I32 = jnp.int32

D_MODEL = 1024
BATCH = 8
SEQ = 2048
DEC_BATCH = 128
DEC_SEQ = 8
PAST_LEN = 16384
N_Q_HEADS = 8
N_KV_HEADS = 2
HEAD_DIM = 64
GQA_GROUP = N_Q_HEADS // N_KV_HEADS
ATT_WIDTH = N_Q_HEADS * HEAD_DIM
KV_WIDTH = N_KV_HEADS * HEAD_DIM
WINDOW = 128
ROPE_THETA = 500000.0
ROT_DIM = HEAD_DIM // 4
POOL_WINDOWS = (2, 4, 8, 16)
POOL_GROUPS = 4
POOL_WIDTH = D_MODEL - ATT_WIDTH
POOL_GROUP_DIM = POOL_WIDTH // POOL_GROUPS
POOL_STATE = 15
IN_WIDTH = ATT_WIDTH + 2 * KV_WIDTH + POOL_WIDTH
N_EXPERTS = 64
TOP_K = 8
N_EXPERT_GROUPS = 8
GROUP_SIZE = N_EXPERTS // N_EXPERT_GROUPS
TOPK_GROUPS = 4
D_EXPERT = 256
D_SHARED = 256
ROUTED_SCALE = 2.5
PLE_DIM = 256
EPS = 1e-6

T_P = BATCH * SEQ
T_S = DEC_BATCH * DEC_SEQ
T_ALL = T_P + T_S
HALF = D_MODEL // 2
LANES = 128
VMEM_LIMIT = 48 * 1024 * 1024

BM_IN = 1024
BQ = 2 * WINDOW
EXPERTS_PER_MIX_STEP = N_EXPERTS * BQ // T_P
assert EXPERTS_PER_MIX_STEP * T_P == N_EXPERTS * BQ
SB = 16
BM_R = 512
BT_RANK = 512
BT_DEST = 2176
BT_COMB = 256
N_GROUPS = 2
T_GRP = T_ALL // N_GROUPS
assert T_GRP * N_GROUPS == T_ALL
BM_E = 1280
E_STRIP = 128
assert BM_E % E_STRIP == 0
N_ASSIGN = T_GRP * TOP_K
N_BLOCKS = -(-N_ASSIGN // BM_E) + N_EXPERTS
N_SLOTS = N_BLOCKS * BM_E

ROW_CHUNKS = HALF // LANES
SC_CORES = 2
SC_SUBCORES = 16
SC_WORKERS = SC_CORES * SC_SUBCORES
SC_CHUNK = 16
SC_RING = 4
SC_STEP = SC_WORKERS * SC_CHUNK
assert T_GRP % SC_STEP == 0
N_SUB = 2
SUB_BOUNDS = [T_GRP // SC_STEP * s // N_SUB * SC_STEP for s in range(N_SUB + 1)]


def _load_chunks(ref, n_rows, lead=()):
    return [ref[lead + (pl.ds(c, n_rows, stride=ROW_CHUNKS), slice(None))] for c in range(ROW_CHUNKS)]


def _store_chunks(ref, packed):
    n_rows = packed.shape[0]
    for c in range(ROW_CHUNKS):
        ref[pl.ds(c, n_rows, stride=ROW_CHUNKS), :] = packed[:, c * LANES:(c + 1) * LANES]


def _cparams(sem):
    return pltpu.CompilerParams(dimension_semantics=sem, vmem_limit_bytes=VMEM_LIMIT)


def _rms(x, g):
    return x * lax.rsqrt(jnp.mean(x * x, axis=-1, keepdims=True) + EPS) * g


def _sigmoid(x):
    return 1.0 / (1.0 + jnp.exp(-x))


def _pack_bf16_pairs(x):
    h = x.shape[-1] // 2
    return pltpu.pack_elementwise([x[:, :h], x[:, h:]], packed_dtype=BF16)


def _unpack_lo(p):
    return pltpu.bitcast(p << 16, F32)


def _unpack_hi(p):
    return pltpu.bitcast(p & jnp.uint32(0xFFFF0000), F32)


def _inproj_kernel(x_ref, g_ref, w_ref, c_ref, s1_ref, s2_ref, q_ref, k_ref, v_ref, u_ref):
    xn = _rms(x_ref[...], g_ref[...]).astype(BF16)
    z = jnp.dot(xn, w_ref[...], preferred_element_type=F32)
    c, s1, s2 = c_ref[...], s1_ref[...], s2_ref[...]

    def rope(t):
        return t * c + pltpu.roll(t, LANES - ROT_DIM // 2, 1) * s1 + pltpu.roll(t, ROT_DIM // 2, 1) * s2

    for i in range(ATT_WIDTH // LANES):
        sl = slice(i * LANES, (i + 1) * LANES)
        q_ref[:, sl] = (rope(z[:, sl]) * (HEAD_DIM ** -0.5)).astype(q_ref.dtype)
    k_ref[...] = rope(z[:, ATT_WIDTH:ATT_WIDTH + KV_WIDTH])
    v_ref[...] = z[:, ATT_WIDTH + KV_WIDTH:ATT_WIDTH + 2 * KV_WIDTH]
    u_ref[...] = z[:, ATT_WIDTH + 2 * KV_WIDTH:]


def _rope_tables(pos, reps=1):
    f32 = np.float32
    half = ROT_DIM // 2
    inv = np.power(f32(ROPE_THETA), -np.arange(half, dtype=f32) * f32(2.0) / f32(ROT_DIM)).astype(f32)
    ang = np.asarray(pos, f32)[:, None] * inv[None, :]
    cos, sin = np.cos(ang).astype(f32), np.sin(ang).astype(f32)
    n = len(pos)
    ones = np.ones((n, HEAD_DIM - ROT_DIM), f32)
    zeros = np.zeros((n, HEAD_DIM - ROT_DIM), f32)
    zh = np.zeros((n, half), f32)
    c = np.concatenate([cos, cos, ones], axis=1)
    s1 = np.concatenate([-sin, zh, zeros], axis=1)
    s2 = np.concatenate([zh, sin, zeros], axis=1)
    tile = lambda a: np.tile(a, (reps, LANES // HEAD_DIM))
    return tile(c), tile(s1), tile(s2)


def _inproj(x2d, g_mix, w_in_bf, tables, q_dtype):
    rows = x2d.shape[0]
    n_tab = tables[0].shape[0] // BM_IN
    row_spec = lambda w: pl.BlockSpec((BM_IN, w), lambda i: (i, 0))
    tab_spec = pl.BlockSpec((BM_IN, LANES), lambda i: (i % n_tab, 0))
    full = lambda a: pl.BlockSpec(a.shape, lambda i: (0,) * a.ndim)
    return pl.pallas_call(
        _inproj_kernel,
        grid=(rows // BM_IN,),
        in_specs=[row_spec(D_MODEL), full(g_mix), full(w_in_bf), tab_spec, tab_spec, tab_spec],
        out_specs=[row_spec(ATT_WIDTH), row_spec(KV_WIDTH), row_spec(KV_WIDTH), row_spec(POOL_WIDTH)],
        out_shape=[jax.ShapeDtypeStruct((rows, ATT_WIDTH), q_dtype),
                   jax.ShapeDtypeStruct((rows, KV_WIDTH), F32),
                   jax.ShapeDtypeStruct((rows, KV_WIDTH), F32),
                   jax.ShapeDtypeStruct((rows, POOL_WIDTH), F32)],
        compiler_params=_cparams(("parallel",)),
        name="inproj",
    )(x2d, g_mix, w_in_bf, *tables)


def _sink_column(sinks_ref, kv_head, rows_per_head):
    n = GQA_GROUP * rows_per_head
    grp = lax.broadcasted_iota(I32, (n, 1), 0) // rows_per_head
    col = jnp.full((n, 1), sinks_ref[kv_head * GQA_GROUP], F32)
    for g in range(1, GQA_GROUP):
        col = jnp.where(grp == g, sinks_ref[kv_head * GQA_GROUP + g], col)
    return col


def _band_mask(n_rows, rows_per_head, n_keys):
    i = lax.broadcasted_iota(I32, (n_rows, n_keys), 0) % rows_per_head
    c = lax.broadcasted_iota(I32, (n_rows, n_keys), 1)
    return (c >= i) & (c <= i + WINDOW), c


def _stack_heads(q, kv_head):
    return jnp.concatenate(
        [q[:, (kv_head * GQA_GROUP + g) * HEAD_DIM:(kv_head * GQA_GROUP + g + 1) * HEAD_DIM]
         for g in range(GQA_GROUP)], axis=0)


def _nt_dot(a, b):
    return lax.dot_general(a, b, (((1,), (1,)), ((), ())), preferred_element_type=F32)


POOL_HALO = 16
POOL_HEAD = 8
assert all(w == 2 << g for g, w in enumerate(POOL_WINDOWS)) and POOL_WINDOWS[-1] - 1 <= POOL_HALO


def _window_sums(ext_ref, n):
    lo, hi = POOL_HEAD, POOL_HEAD + POOL_HALO + n
    for p in range(POOL_GROUPS):
        lanes = slice(p * POOL_GROUP_DIM, POOL_WIDTH)
        ext_ref[lo:hi, lanes] = ext_ref[lo:hi, lanes] + ext_ref[lo - (1 << p):hi - (1 << p), lanes]


def _pool_out(d, wpool_ref, pscale_ref, gpool_ref):
    parts = [jnp.dot(d[:, g * POOL_GROUP_DIM:(g + 1) * POOL_GROUP_DIM].astype(BF16), wpool_ref[g],
                     preferred_element_type=F32) for g in range(POOL_GROUPS)]
    return _rms(jnp.concatenate(parts, axis=-1) * pscale_ref[...], gpool_ref[...])


def _mixer_tail(o_att, pooled, h, gatt_ref, wout_ref):
    mixed = jnp.concatenate([_rms(o_att, gatt_ref[...]), pooled], axis=-1)
    return h + jnp.dot(mixed.astype(BF16), wout_ref[...], preferred_element_type=F32)


def _mixer_prompt_kernel(sinks_ref, h_ref, q_ref, kc_ref, kp_ref, vc_ref, vp_ref, uc_ref, up_ref,
                         wpool_ref, pscale_ref, gatt_ref, gpool_ref, wout_ref, weg_ref, weu_ref, wed_ref,
                         h1_ref, wgu_bf_ref, wd_bf_ref, uext_ref):
    j = pl.program_id(1)
    wgu_bf_ref[:, :, :D_EXPERT] = weg_ref[...].astype(BF16)
    wgu_bf_ref[:, :, D_EXPERT:] = weu_ref[...].astype(BF16)
    wd_bf_ref[...] = wed_ref[...].astype(BF16)

    u = uc_ref[...]
    base = POOL_HEAD + POOL_HALO
    uext_ref[0:POOL_HEAD, :] = jnp.zeros((POOL_HEAD, POOL_WIDTH), F32)
    uext_ref[POOL_HEAD:base, :] = jnp.where(j > 0, up_ref[...], 0.0)
    uext_ref[base:base + BQ, :] = u
    _window_sums(uext_ref, BQ)
    pos = j * BQ + lax.broadcasted_iota(I32, (BQ, 1), 0)
    parts = []
    for g, w in enumerate(POOL_WINDOWS):
        sl = slice(g * POOL_GROUP_DIM, (g + 1) * POOL_GROUP_DIM)
        parts.append(uext_ref[base:base + BQ, sl] / jnp.minimum(pos + 1, w).astype(F32) - u[:, sl])
    pooled = _pool_out(jnp.concatenate(parts, axis=-1), wpool_ref, pscale_ref, gpool_ref)

    q = q_ref[...]
    k_all = jnp.concatenate([kp_ref[...], kc_ref[...]], axis=0).astype(BF16)
    v_all = jnp.concatenate([vp_ref[...], vc_ref[...]], axis=0).astype(BF16)
    ones = jnp.ones((WINDOW + BQ, HEAD_DIM), BF16)
    v_ones = [jnp.concatenate([v_all[:, hk * HEAD_DIM:(hk + 1) * HEAD_DIM], ones], axis=1)
              for hk in range(N_KV_HEADS)]
    band, col = _band_mask(GQA_GROUP * WINDOW, WINDOW, 2 * WINDOW)
    sinks = [_sink_column(sinks_ref, hk, WINDOW) for hk in range(N_KV_HEADS)]
    bands = []
    for b in range(BQ // WINDOW):
        rows = slice(b * WINDOW, (b + 1) * WINDOW)
        keys = slice(b * WINDOW, (b + 2) * WINDOW)
        mask = band & ((col >= WINDOW) | (j > 0)) if b == 0 else band
        heads = []
        for hk in range(N_KV_HEADS):
            sl = slice(hk * HEAD_DIM, (hk + 1) * HEAD_DIM)
            s = jnp.where(mask, _nt_dot(_stack_heads(q[rows], hk), k_all[keys, sl]), -jnp.inf)
            m = jnp.maximum(jnp.max(s, axis=-1, keepdims=True), sinks[hk])
            e = jnp.exp(s - m).astype(BF16)
            ov = jnp.dot(e, v_ones[hk][keys], preferred_element_type=F32)
            den = ov[:, HEAD_DIM:HEAD_DIM + 1] + jnp.exp(sinks[hk] - m)
            o = ov[:, :HEAD_DIM] / den
            heads += [o[g * WINDOW:(g + 1) * WINDOW] for g in range(GQA_GROUP)]
        bands.append(jnp.concatenate(heads, axis=-1))
    o_att = jnp.concatenate(bands, axis=0)
    h1_ref[...] = _mixer_tail(o_att, pooled, h_ref[...], gatt_ref, wout_ref)


def _mixer_sample_kernel(sinks_ref, h_ref, q_ref, kn_ref, vn_ref, u_ref, ck_ref, cv_ref, st_ref,
                         wpool_ref, pscale_ref, gatt_ref, gpool_ref, wout_ref, h1_in_ref,
                         h1_ref, ko_ref, vo_ref, po_ref, uext_ref):
    del h1_in_ref
    n_q = GQA_GROUP * DEC_SEQ
    n_keys = 2 * WINDOW
    band, col = _band_mask(n_q, DEC_SEQ, n_keys)
    mask = (band & (col < WINDOW + DEC_SEQ))[None]
    q3, kn3, vn3, u3 = q_ref[...], kn_ref[...], vn_ref[...], u_ref[...]
    ck, cv = ck_ref[...], cv_ref[...]
    ko_ref[:, 0:WINDOW - DEC_SEQ, :] = ck[:, DEC_SEQ:, :]
    ko_ref[:, WINDOW - DEC_SEQ:WINDOW, :] = kn3
    vo_ref[:, 0:WINDOW - DEC_SEQ, :] = cv[:, DEC_SEQ:, :]
    vo_ref[:, WINDOW - DEC_SEQ:WINDOW, :] = vn3
    pad = jnp.zeros((SB, WINDOW - DEC_SEQ, KV_WIDTH), F32)
    k_all = jnp.concatenate([ck, kn3, pad], axis=1).astype(BF16)
    v_all = jnp.concatenate([cv, vn3, pad], axis=1).astype(BF16)
    heads = []
    for hk in range(N_KV_HEADS):
        sl = slice(hk * HEAD_DIM, (hk + 1) * HEAD_DIM)
        qs = jnp.concatenate(
            [q3[:, :, (hk * GQA_GROUP + g) * HEAD_DIM:(hk * GQA_GROUP + g + 1) * HEAD_DIM]
             for g in range(GQA_GROUP)], axis=1).astype(BF16)
        sink = _sink_column(sinks_ref, hk, DEC_SEQ)[None]
        s = jnp.einsum("bqd,bkd->bqk", qs, k_all[:, :, sl], preferred_element_type=F32)
        s = jnp.where(mask, s, -jnp.inf)
        m = jnp.maximum(jnp.max(s, axis=-1, keepdims=True), sink)
        e = jnp.exp(s - m)
        den = jnp.sum(e, axis=-1, keepdims=True) + jnp.exp(sink - m)
        o = jnp.einsum("bqk,bkd->bqd", e.astype(BF16), v_all[:, :, sl], preferred_element_type=F32) / den
        heads += [o[:, g * DEC_SEQ:(g + 1) * DEC_SEQ, :] for g in range(GQA_GROUP)]
    o_att = jnp.concatenate(heads, axis=-1).reshape(SB * DEC_SEQ, ATT_WIDTH)

    uext_ref[:, 1:16, :] = st_ref[...]
    uext_ref[:, 16:16 + DEC_SEQ, :] = u3
    parts = []
    for g, w in enumerate(POOL_WINDOWS):
        sl = slice(g * POOL_GROUP_DIM, (g + 1) * POOL_GROUP_DIM)
        acc = u3[:, :, sl]
        for back in range(1, w):
            acc = acc + uext_ref[:, 16 - back:16 - back + DEC_SEQ, sl]
        parts.append(acc / float(w) - u3[:, :, sl])
    d = jnp.concatenate(parts, axis=-1).reshape(SB * DEC_SEQ, POOL_WIDTH)
    po_ref[...] = uext_ref[:, 16 + DEC_SEQ - POOL_STATE:16 + DEC_SEQ, :]
    pooled = _pool_out(d, wpool_ref, pscale_ref, gpool_ref)
    h1_ref[...] = _mixer_tail(o_att, pooled, h_ref[...], gatt_ref, wout_ref)


def _full_spec(a, n_grid):
    nd = a.ndim
    return pl.BlockSpec(a.shape, lambda *_: (0,) * nd)


def _mixer_prompt(sinks, x2d, q, k, v, u, wts, w_exp):
    nb = SEQ // BQ
    row = lambda w: pl.BlockSpec((BQ, w), lambda b, j: (b * nb + j, 0))
    prev = lambda w: pl.BlockSpec(
        (WINDOW, w), lambda b, j: (jnp.maximum((b * nb + j) * (BQ // WINDOW) - 1, 0), 0))
    uprev = pl.BlockSpec((POOL_HALO, POOL_WIDTH),
                         lambda b, j: (jnp.maximum((b * nb + j) * (BQ // POOL_HALO) - 1, 0), 0))
    smem = pl.BlockSpec(memory_space=pltpu.SMEM)
    per_step = lambda r, c: pl.BlockSpec((EXPERTS_PER_MIX_STEP, r, c), lambda b, j: (b * nb + j, 0, 0))
    return pl.pallas_call(
        _mixer_prompt_kernel,
        grid=(BATCH, nb),
        in_specs=[smem, row(D_MODEL), row(ATT_WIDTH), row(KV_WIDTH), prev(KV_WIDTH), row(KV_WIDTH),
                  prev(KV_WIDTH), row(POOL_WIDTH), uprev] + [_full_spec(w, 2) for w in wts]
                 + [per_step(D_MODEL, D_EXPERT), per_step(D_MODEL, D_EXPERT), per_step(D_EXPERT, D_MODEL)],
        out_specs=[row(D_MODEL), per_step(D_MODEL, 2 * D_EXPERT), per_step(D_EXPERT, D_MODEL)],
        out_shape=[jax.ShapeDtypeStruct((T_ALL, D_MODEL), F32),
                   jax.ShapeDtypeStruct((N_EXPERTS, D_MODEL, 2 * D_EXPERT), BF16),
                   jax.ShapeDtypeStruct((N_EXPERTS, D_EXPERT, D_MODEL), BF16)],
        scratch_shapes=[pltpu.VMEM((POOL_HEAD + POOL_HALO + BQ, POOL_WIDTH), F32)],
        compiler_params=_cparams(("parallel", "parallel")),
        name="mixer_prompt",
    )(sinks, x2d, q, k, k, v, v, u, u, *wts, *w_exp)


def _mixer_sample(sinks, x2d, q, k, v, u, cache_k, cache_v, state, wts, h1_buf):
    rows = SB * DEC_SEQ
    row = lambda w: pl.BlockSpec((rows, w), lambda i: (i, 0))
    bat = lambda a: pl.BlockSpec((SB,) + a.shape[1:], lambda i: (i, 0, 0))
    smem = pl.BlockSpec(memory_space=pltpu.SMEM)
    h1_blocks_before = T_P // rows
    n_in = 9 + len(wts)
    q, k, v, u = (a.reshape(DEC_BATCH, DEC_SEQ, a.shape[-1]) for a in (q, k, v, u))
    st = pl.BlockSpec((None, SB) + state.shape[2:], lambda i: (0, i, 0, 0))
    return pl.pallas_call(
        _mixer_sample_kernel,
        grid=(DEC_BATCH // SB,),
        in_specs=[smem, row(D_MODEL), bat(q), bat(k), bat(v), bat(u),
                  bat(cache_k), bat(cache_v), st] + [_full_spec(w, 1) for w in wts]
                 + [pl.BlockSpec(memory_space=pl.ANY)],
        out_specs=[pl.BlockSpec((rows, D_MODEL), lambda i: (h1_blocks_before + i, 0)),
                   bat(cache_k), bat(cache_v), st],
        out_shape=[jax.ShapeDtypeStruct((T_ALL, D_MODEL), F32),
                   jax.ShapeDtypeStruct(cache_k.shape, F32),
                   jax.ShapeDtypeStruct(cache_v.shape, F32),
                   jax.ShapeDtypeStruct(state.shape, F32)],
        scratch_shapes=[pltpu.VMEM((SB, 16 + DEC_SEQ, POOL_WIDTH), F32)],
        input_output_aliases={n_in: 0},
        compiler_params=_cparams(("parallel",)),
        name="mixer_sample",
    )(sinks, x2d, q, k, v, u, cache_k, cache_v, state, *wts, h1_buf)


def _first_max(vals, iota, n):
    m = jnp.max(vals, axis=0, keepdims=True)
    idx = jnp.min(jnp.where(vals == m, iota, n), axis=0, keepdims=True)
    return m, idx


def _router_kernel(h1_ref, gffn_ref, wrt_ref, bias_ref, xp_ref, idx_ref, wts_ref):
    xn = _rms(h1_ref[...], gffn_ref[...])
    w = wrt_ref[...]
    w_hi, x_hi = w.astype(BF16), xn.astype(BF16)
    w_lo, x_lo = (w - w_hi.astype(F32)).astype(BF16), (xn - x_hi.astype(F32)).astype(BF16)
    logits = _nt_dot(w_hi, x_hi) + (_nt_dot(w_hi, x_lo) + _nt_dot(w_lo, x_hi))
    scores = _sigmoid(logits)
    biased = scores + bias_ref[...]
    n_tok = biased.shape[1]
    neg = -jnp.inf

    iota_g = lax.broadcasted_iota(I32, (GROUP_SIZE, n_tok), 0)
    grp_rows = []
    for g in range(N_EXPERT_GROUPS):
        blk = biased[g * GROUP_SIZE:(g + 1) * GROUP_SIZE, :]
        top1, i1 = _first_max(blk, iota_g, GROUP_SIZE)
        top2 = jnp.max(jnp.where(iota_g == i1, neg, blk), axis=0, keepdims=True)
        grp_rows.append(top1 + top2)
    gs = jnp.concatenate(grp_rows, axis=0)

    iota_n = lax.broadcasted_iota(I32, (N_EXPERT_GROUPS, n_tok), 0)
    gsel = jnp.zeros((N_EXPERT_GROUPS, n_tok), jnp.bool_)
    for _ in range(TOPK_GROUPS):
        _, gi = _first_max(gs, iota_n, N_EXPERT_GROUPS)
        hit = iota_n == gi
        gsel = gsel | hit
        gs = jnp.where(hit, neg, gs)
    emask = jnp.concatenate(
        [jnp.broadcast_to(gsel[g:g + 1, :], (GROUP_SIZE, n_tok)) for g in range(N_EXPERT_GROUPS)], axis=0)
    masked = jnp.where(emask, biased, neg)

    iota_e = lax.broadcasted_iota(I32, (N_EXPERTS, n_tok), 0)
    idx_rows, sel_rows = [], []
    for _ in range(TOP_K):
        _, ei = _first_max(masked, iota_e, N_EXPERTS)
        hit = iota_e == ei
        idx_rows.append(ei)
        sel_rows.append(jnp.sum(jnp.where(hit, scores, 0.0), axis=0, keepdims=True))
        masked = jnp.where(hit, neg, masked)
    sel = jnp.concatenate(sel_rows, axis=0)
    idx_ref[...] = jnp.concatenate(idx_rows, axis=0)
    wts_ref[...] = sel / jnp.sum(sel, axis=0, keepdims=True) * ROUTED_SCALE
    _store_chunks(xp_ref, _pack_bf16_pairs(xn))


def _router(h1, group, g_ffn, w_router_t, bias_col):
    blk0 = group * T_GRP // BM_R
    colblk = pl.BlockSpec((TOP_K, BM_R), lambda i: (0, i))
    ws = [g_ffn, w_router_t, bias_col]
    return pl.pallas_call(
        _router_kernel,
        grid=(T_GRP // BM_R,),
        in_specs=[pl.BlockSpec((BM_R, D_MODEL), lambda i: (blk0 + i, 0))] + [_full_spec(w, 1) for w in ws],
        out_specs=[pl.BlockSpec((BM_R * ROW_CHUNKS, LANES), lambda i: (i, 0)), colblk, colblk],
        out_shape=[jax.ShapeDtypeStruct((T_GRP * ROW_CHUNKS, LANES), U32),
                   jax.ShapeDtypeStruct((TOP_K, T_GRP), I32),
                   jax.ShapeDtypeStruct((TOP_K, T_GRP), F32)],
        compiler_params=_cparams(("parallel",)),
        name="router",
    )(h1, *ws)


def _rank_kernel(idx_ref, tri_ref, rank_ref, cnt_ref, carry_ref):
    @pl.when(pl.program_id(0) == 0)
    def _():
        carry_ref[...] = jnp.zeros_like(carry_ref)

    idx = idx_ref[...]
    n_tok = idx.shape[1]
    iota_e = lax.broadcasted_iota(I32, (N_EXPERTS, n_tok), 0)
    member = jnp.zeros((N_EXPERTS, n_tok), F32)
    for k in range(TOP_K):
        member = member + jnp.where(iota_e == idx[k:k + 1, :], 1.0, 0.0)
    before = jnp.dot(member.astype(BF16), tri_ref[...], preferred_element_type=F32) + carry_ref[...]
    rows = [jnp.sum(jnp.where(iota_e == idx[k:k + 1, :], before, 0.0), axis=0, keepdims=True)
            for k in range(TOP_K)]
    rank_ref[...] = jnp.concatenate(rows, axis=0).astype(I32)
    carry_ref[...] = carry_ref[...] + jnp.sum(member, axis=1, keepdims=True)
    cnt_ref[...] = carry_ref[...].astype(I32)


def _rank(idx_t, tri):
    blk = pl.BlockSpec((TOP_K, BT_RANK), lambda i: (0, i))
    return pl.pallas_call(
        _rank_kernel,
        grid=(T_GRP // BT_RANK,),
        in_specs=[blk, _full_spec(tri, 1)],
        out_specs=[blk, pl.BlockSpec((N_EXPERTS, 1), lambda i: (0, 0))],
        out_shape=[jax.ShapeDtypeStruct((TOP_K, T_GRP), I32),
                   jax.ShapeDtypeStruct((N_EXPERTS, 1), I32)],
        scratch_shapes=[pltpu.VMEM((N_EXPERTS, 1), F32)],
        compiler_params=_cparams(("arbitrary",)),
        name="rank",
    )(idx_t, tri)


def _dest_kernel(idx_ref, rank_ref, cnt_ref, dest_ref, blk_e_ref, n_used_ref, blk_rows_ref):
    counts = cnt_ref[...]
    padded = (counts + (BM_E - 1)) // BM_E * BM_E
    r = lax.broadcasted_iota(I32, (N_EXPERTS, N_EXPERTS), 0)
    c = lax.broadcasted_iota(I32, (N_EXPERTS, N_EXPERTS), 1)
    padded_row = jnp.sum(jnp.where(r == c, padded, 0), axis=0, keepdims=True)
    pad_start = jnp.sum(jnp.where(c < r, padded_row, 0), axis=1, keepdims=True)

    idx = idx_ref[...]
    n_tok = idx.shape[1]
    iota_e = lax.broadcasted_iota(I32, (N_EXPERTS, n_tok), 0)
    rows = [jnp.sum(jnp.where(iota_e == idx[k:k + 1, :], pad_start, 0), axis=0, keepdims=True)
            for k in range(TOP_K)]
    dest_ref[...] = jnp.concatenate(rows, axis=0) + rank_ref[...]

    @pl.when(pl.program_id(0) == 0)
    def _():
        pad_end_row = jnp.sum(jnp.where(r <= c, padded, 0), axis=0, keepdims=True)
        b0 = lax.broadcasted_iota(I32, (N_BLOCKS_PAD, N_EXPERTS), 0) * BM_E
        be = jnp.minimum(jnp.sum(jnp.where(pad_end_row <= b0, 1, 0), axis=1, keepdims=True), N_EXPERTS - 1)
        blk_e_ref[...] = be
        n_used_ref[...] = pad_end_row[:, N_EXPERTS - 1:N_EXPERTS] // BM_E
        counts_row = jnp.sum(jnp.where(r == c, counts, 0), axis=0, keepdims=True)
        mine = lax.broadcasted_iota(I32, (N_BLOCKS_PAD, N_EXPERTS), 1) == be
        end_valid = jnp.sum(jnp.where(mine, pad_end_row - padded_row + counts_row, 0), axis=1, keepdims=True)
        blk_rows_ref[...] = jnp.clip(end_valid - b0[:, :1], 0, BM_E)


N_BLOCKS_PAD = (N_BLOCKS + 7) // 8 * 8


def _dest(idx_t, rank_t, counts):
    blk = pl.BlockSpec((TOP_K, BT_DEST), lambda i: (0, i))
    one = lambda s: pl.BlockSpec(s, lambda i: (0, 0))
    return pl.pallas_call(
        _dest_kernel,
        grid=(T_GRP // BT_DEST,),
        in_specs=[blk, blk, one((N_EXPERTS, 1))],
        out_specs=[blk, one((N_BLOCKS_PAD, 1)), one((1, 1)), one((N_BLOCKS_PAD, 1))],
        out_shape=[jax.ShapeDtypeStruct((TOP_K, T_GRP), I32),
                   jax.ShapeDtypeStruct((N_BLOCKS_PAD, 1), I32),
                   jax.ShapeDtypeStruct((1, 1), I32),
                   jax.ShapeDtypeStruct((N_BLOCKS_PAD, 1), I32)],
        compiler_params=_cparams(("arbitrary",)),
        name="dest",
    )(idx_t, rank_t, counts)


def _sc_mesh():
    return plsc.VectorSubcoreMesh(core_axis_name="c", subcore_axis_name="s")


def _sc_worker_id():
    return lax.axis_index("s") * SC_CORES + lax.axis_index("c")


def _dispatch_body(dest_hbm, xp_hbm, xs_hbm, idx_v, rows_v, sem_in, sem_out):
    n_chunks, _, n_tok = dest_hbm.shape
    per_worker = n_chunks // SC_WORKERS
    chunk0 = _sc_worker_id() * per_worker

    def loads(i):
        chunk = chunk0 + i
        t0 = pl.multiple_of(chunk * n_tok, n_tok)
        return (pltpu.make_async_copy(dest_hbm.at[chunk], idx_v.at[i % 2], sem_in.at[i % 2]),
                pltpu.make_async_copy(xp_hbm.at[pl.ds(t0, n_tok)], rows_v.at[i % 2], sem_in.at[i % 2]))

    def scatters(i):
        return [pltpu.make_async_copy(rows_v.at[i % 2], xs_hbm.at[idx_v.at[i % 2, k]], sem_out.at[i % 2])
                for k in range(TOP_K)]

    for cp in loads(0):
        cp.start()
    for i in range(per_worker):
        for cp in loads(i):
            cp.wait()
        if i >= 1:
            for cp in scatters(i - 1):
                cp.wait()
        if i + 1 < per_worker:
            for cp in loads(i + 1):
                cp.start()
        for cp in scatters(i):
            cp.start()
    for cp in scatters(per_worker - 1):
        cp.wait()


def _dispatch(dest_chunks, xp3):
    return pl.kernel(
        _dispatch_body,
        out_type=jax.ShapeDtypeStruct((N_SLOTS, ROW_CHUNKS, LANES), U32),
        mesh=_sc_mesh(),
        scratch_types=[pltpu.VMEM((2, TOP_K, SC_CHUNK), I32),
                       pltpu.VMEM((2, SC_CHUNK, ROW_CHUNKS, LANES), U32),
                       pltpu.SemaphoreType.DMA((2,)), pltpu.SemaphoreType.DMA((2,))],
        name="dispatch",
    )(dest_chunks, xp3)


def _gather_body(dest_hbm, ys_hbm, yt_hbm, idx_v, rows_v, sem_in, sem_out):
    n_chunks, _, n_tok = dest_hbm.shape
    per_worker = n_chunks // SC_WORKERS
    chunk0 = _sc_worker_id() * per_worker

    @pl.loop(0, per_worker)
    def _(i):
        chunk = chunk0 + i
        t0 = pl.multiple_of(chunk * n_tok, n_tok)
        pltpu.sync_copy(dest_hbm.at[chunk], idx_v)

        def gather(k):
            return pltpu.make_async_copy(ys_hbm.at[idx_v.at[k]], rows_v.at[k % SC_RING], sem_in.at[k % SC_RING])

        def store(k):
            return pltpu.make_async_copy(rows_v.at[k % SC_RING], yt_hbm.at[k, pl.ds(t0, n_tok)],
                                         sem_out.at[k % SC_RING])

        for k in range(SC_RING):
            gather(k).start()
        for k in range(TOP_K):
            gather(k).wait()
            store(k).start()
            if k + SC_RING < TOP_K:
                store(k).wait()
                gather(k + SC_RING).start()
        for k in range(TOP_K - SC_RING, TOP_K):
            store(k).wait()


def _gather(dest_chunks, ys3):
    n_chunks, _, n_tok = dest_chunks.shape
    assert n_chunks % SC_WORKERS == 0
    return pl.kernel(
        _gather_body,
        out_type=jax.ShapeDtypeStruct((TOP_K, n_chunks * n_tok, ROW_CHUNKS, LANES), U32),
        mesh=_sc_mesh(),
        scratch_types=[pltpu.VMEM((TOP_K, n_tok), I32),
                       pltpu.VMEM((SC_RING, n_tok, ROW_CHUNKS, LANES), U32),
                       pltpu.SemaphoreType.DMA((SC_RING,)), pltpu.SemaphoreType.DMA((SC_RING,))],
        name="gather",
    )(dest_chunks, ys3)


def _experts_kernel(blk_e_ref, n_used_ref, blk_rows_ref, xs_ref, wgu_ref, wd_ref, ys_ref):
    del blk_e_ref
    b = pl.program_id(0)

    def swiglu_rows(n_rows):
        chunks = _load_chunks(xs_ref, n_rows)
        x_lo = jnp.concatenate([_unpack_lo(p) for p in chunks], axis=-1).astype(BF16)
        x_hi = jnp.concatenate([_unpack_hi(p) for p in chunks], axis=-1).astype(BF16)
        gu = (jnp.dot(x_lo, wgu_ref[0, :HALF, :], preferred_element_type=F32)
              + jnp.dot(x_hi, wgu_ref[0, HALF:, :], preferred_element_type=F32))
        gate, up = gu[:, :D_EXPERT], gu[:, D_EXPERT:]
        hmid = (gate * _sigmoid(gate) * up).astype(BF16)
        _store_chunks(ys_ref, _pack_bf16_pairs(jnp.dot(hmid, wd_ref[0], preferred_element_type=F32)))

    @pl.when(b < n_used_ref[0])
    def _():
        valid = blk_rows_ref[b]
        for n_rows in range(E_STRIP, BM_E + 1, E_STRIP):
            @pl.when((valid > n_rows - E_STRIP) & (valid <= n_rows))
            def _(n_rows=n_rows):
                swiglu_rows(n_rows)


def _experts(blk_e, n_used, blk_rows, xs, wgu_bf, wd_bf):
    def blk(b, be, nu, nr):
        return jnp.minimum(b, nu[0] - 1)

    def by_expert(shape):
        return pl.BlockSpec((1,) + shape, lambda b, be, nu, nr: (be[blk(b, be, nu, nr)], 0, 0))

    tile = pl.BlockSpec((BM_E * ROW_CHUNKS, LANES), lambda b, be, nu, nr: (blk(b, be, nu, nr), 0))
    grid_spec = pltpu.PrefetchScalarGridSpec(
        num_scalar_prefetch=3,
        grid=(N_BLOCKS,),
        in_specs=[tile, by_expert((D_MODEL, 2 * D_EXPERT)), by_expert((D_EXPERT, D_MODEL))],
        out_specs=tile,
    )
    return pl.pallas_call(
        _experts_kernel,
        grid_spec=grid_spec,
        out_shape=jax.ShapeDtypeStruct((N_SLOTS * ROW_CHUNKS, LANES), U32),
        compiler_params=_cparams(("arbitrary",)),
        name="experts",
    )(blk_e, n_used, blk_rows, xs, wgu_bf, wd_bf)


def _combine_kernel(yt_ref, wts_ref, h1_ref, p_ref, gffn_ref, wsgu_ref, wsd_ref, gple_ref, wpg_ref, wpp_ref,
                    gfin_ref, *y_refs):
    y_ref = y_refs[-1]
    h1 = h1_ref[...]
    gu = jnp.dot(_rms(h1, gffn_ref[...]).astype(BF16), wsgu_ref[...], preferred_element_type=F32)
    sgate, sup = gu[:, :D_SHARED], gu[:, D_SHARED:]
    hsh = h1 + jnp.dot((sgate * _sigmoid(sgate) * sup).astype(BF16), wsd_ref[...], preferred_element_type=F32)
    wts = jnp.transpose(wts_ref[...])
    lo = [jnp.zeros((BT_COMB, LANES), F32) for _ in range(ROW_CHUNKS)]
    hi = [jnp.zeros((BT_COMB, LANES), F32) for _ in range(ROW_CHUNKS)]
    for k in range(TOP_K):
        w = wts[:, k:k + 1]
        for c, p in enumerate(_load_chunks(yt_ref, BT_COMB, lead=(k,))):
            lo[c] = lo[c] + w * _unpack_lo(p)
            hi[c] = hi[c] + w * _unpack_hi(p)
    h2 = hsh + jnp.concatenate(lo + hi, axis=-1)
    gate = _sigmoid(jnp.dot(_rms(h2, gple_ref[...]).astype(BF16), wpg_ref[...], preferred_element_type=F32))
    proj = jnp.dot(p_ref[...].astype(BF16), wpp_ref[...], preferred_element_type=F32)
    y_ref[...] = _rms(h2 + proj * gate, gfin_ref[...])


def _combine(yt, yt_row0, wts_t, wts_row0, h1, tok_row0, n_rows, p2d, p_row0, ws, y_prev, out_rows, out_row0):
    assert all(r % BT_COMB == 0 for r in (yt_row0, wts_row0, tok_row0, n_rows, p_row0, out_row0))
    g0, w0, t0, p0, o0 = (r // BT_COMB for r in (yt_row0, wts_row0, tok_row0, p_row0, out_row0))
    in_specs = [pl.BlockSpec((TOP_K, BT_COMB * ROW_CHUNKS, LANES), lambda i: (0, g0 + i, 0)),
                pl.BlockSpec((TOP_K, BT_COMB), lambda i: (0, w0 + i)),
                pl.BlockSpec((BT_COMB, D_MODEL), lambda i: (t0 + i, 0)),
                pl.BlockSpec((BT_COMB, PLE_DIM), lambda i: (p0 + i, 0))] + [_full_spec(w, 1) for w in ws]
    args = [yt, wts_t, h1, p2d, *ws]
    aliases = {}
    if y_prev is not None:
        in_specs.append(pl.BlockSpec(memory_space=pl.ANY))
        aliases = {len(args): 0}
        args.append(y_prev)
    return pl.pallas_call(
        _combine_kernel,
        grid=(n_rows // BT_COMB,),
        in_specs=in_specs,
        out_specs=pl.BlockSpec((BT_COMB, D_MODEL), lambda i: (o0 + i, 0)),
        out_shape=jax.ShapeDtypeStruct((out_rows, D_MODEL), F32),
        input_output_aliases=aliases,
        compiler_params=_cparams(("parallel",)),
        name="combine",
    )(*args)


def kernel(x_prompt, x_sample, cache_k, cache_v, state_pool, p_prompt, p_sample, g_mix, w_in, attn_sinks,
           w_pool, pool_scale, g_att_out, g_pool_out, w_out, g_ffn, w_router, router_bias, w_exp_gate,
           w_exp_up, w_exp_down, w_sh_gate, w_sh_up, w_sh_down, g_ple, w_ple_gate, w_ple_proj, g_final):
    row = lambda a: a.reshape(1, -1)
    xp2d = x_prompt.reshape(T_P, D_MODEL)
    xs2d = x_sample.reshape(T_S, D_MODEL)
    w_in_bf = w_in[0].astype(BF16)
    mixer_wts = [w_pool[0].astype(BF16), row(pool_scale[0]), row(g_att_out[0]), row(g_pool_out[0]),
                 w_out[0].astype(BF16)]

    tab_p = _rope_tables(np.arange(SEQ))
    tab_s = _rope_tables(PAST_LEN + np.arange(DEC_SEQ), reps=BM_IN // DEC_SEQ)

    q_p, k_p, v_p, u_p = _inproj(xp2d, row(g_mix[0]), w_in_bf, tab_p, BF16)
    q_s, k_s, v_s, u_s = _inproj(xs2d, row(g_mix[0]), w_in_bf, tab_s, F32)

    h1, wgu_bf, wd_bf = _mixer_prompt(attn_sinks[0], xp2d, q_p, k_p, v_p, u_p, mixer_wts,
                                      (w_exp_gate[0], w_exp_up[0], w_exp_down[0]))
    h1, k_sample, v_sample, pool_sample = _mixer_sample(
        attn_sinks[0], xs2d, q_s, k_s, v_s, u_s,
        cache_k[0].reshape(DEC_BATCH, WINDOW, KV_WIDTH), cache_v[0].reshape(DEC_BATCH, WINDOW, KV_WIDTH),
        state_pool, mixer_wts, h1)

    g_ffn_row = row(g_ffn[0])
    router_wts = (g_ffn_row, w_router[0].T, router_bias[0].reshape(N_EXPERTS, 1))
    tri = (lax.broadcasted_iota(I32, (BT_RANK, BT_RANK), 0)
           < lax.broadcasted_iota(I32, (BT_RANK, BT_RANK), 1)).astype(BF16)

    def index_chunks(d, n_tok):
        return d.reshape(TOP_K, d.shape[1] // n_tok, n_tok).transpose(1, 0, 2)

    groups = []
    for g in range(N_GROUPS):
        xp, idx_t, wts_t = _router(h1, g, *router_wts)
        rank_t, counts = _rank(idx_t, tri)
        dest_t, *plan = _dest(idx_t, rank_t, counts)
        dest_chunks = index_chunks(dest_t, SC_CHUNK)
        xs = _dispatch(dest_chunks, xp.reshape(T_GRP, ROW_CHUNKS, LANES))
        groups.append((wts_t, dest_chunks, xs, [a.reshape(-1) for a in plan]))

    ple_wts = [g_ffn_row, jnp.concatenate([w_sh_gate[0], w_sh_up[0]], axis=1).astype(BF16),
               w_sh_down[0].astype(BF16),
               row(g_ple[0]), w_ple_gate[0].astype(BF16), w_ple_proj[0].astype(BF16), row(g_final)]
    pp2d = p_prompt[0].reshape(T_P, PLE_DIM)
    ps2d = p_sample[0].reshape(T_S, PLE_DIM)
    y_p = y_s = None
    for g, (wts_t, dest_chunks, xs, plan) in enumerate(groups):
        lo = g * T_GRP
        ys = _experts(*plan, xs.reshape(N_SLOTS * ROW_CHUNKS, LANES), wgu_bf, wd_bf)
        ys3 = ys.reshape(N_SLOTS, ROW_CHUNKS, LANES)
        for s in range(N_SUB):
            a, b = lo + SUB_BOUNDS[s], lo + SUB_BOUNDS[s + 1]
            yt = _gather(dest_chunks[SUB_BOUNDS[s] // SC_CHUNK:SUB_BOUNDS[s + 1] // SC_CHUNK], ys3)
            yt = yt.reshape(TOP_K, (b - a) * ROW_CHUNKS, LANES)
            if a < T_P:
                n = min(b, T_P) - a
                y_p = _combine(yt, 0, wts_t, a - lo, h1, a, n, pp2d, a, ple_wts, y_p, T_P, a)
            if b > T_P:
                s0 = max(a, T_P)
                y_s = _combine(yt, s0 - a, wts_t, s0 - lo, h1, s0, b - s0, ps2d, s0 - T_P, ple_wts, y_s, T_S,
                               s0 - T_P)

    kv5 = lambda a, b: a.reshape(1, b, WINDOW, N_KV_HEADS, HEAD_DIM)
    k_prompt = kv5(k_p.reshape(BATCH, SEQ, KV_WIDTH)[:, SEQ - WINDOW:], BATCH)
    v_prompt = kv5(v_p.reshape(BATCH, SEQ, KV_WIDTH)[:, SEQ - WINDOW:], BATCH)
    pool_prompt = u_p.reshape(BATCH, SEQ, POOL_WIDTH)[:, SEQ - POOL_STATE:][None]
    return (y_p.reshape(BATCH, SEQ, D_MODEL), y_s.reshape(DEC_BATCH, DEC_SEQ, D_MODEL),
            k_prompt, v_prompt, pool_prompt,
            kv5(k_sample, DEC_BATCH), kv5(v_sample, DEC_BATCH), pool_sample)
```

```python
import functools

import numpy as np
import jax
import jax.numpy as jnp
from jax import lax
from jax.experimental import pallas as pl
from jax.experimental.pallas import tpu as pltpu
from jax.experimental.pallas import tpu_sc as plsc

F32 = jnp.float32
BF16 = jnp.bfloat16
U32 = jnp.uint32
I32 = jnp.int32

D_MODEL = 1024
BATCH = 8
SEQ = 2048
DEC_BATCH = 128
DEC_SEQ = 8
PAST_LEN = 16384
N_Q_HEADS = 8
N_KV_HEADS = 2
HEAD_DIM = 64
GQA_GROUP = N_Q_HEADS // N_KV_HEADS
ATT_WIDTH = N_Q_HEADS * HEAD_DIM
KV_WIDTH = N_KV_HEADS * HEAD_DIM
WINDOW = 128
ROPE_THETA = 500000.0
ROT_DIM = HEAD_DIM // 4
POOL_WINDOWS = (2, 4, 8, 16)
POOL_GROUPS = 4
POOL_WIDTH = D_MODEL - ATT_WIDTH
POOL_GROUP_DIM = POOL_WIDTH // POOL_GROUPS
POOL_STATE = 15
IN_WIDTH = ATT_WIDTH + 2 * KV_WIDTH + POOL_WIDTH
N_EXPERTS = 64
TOP_K = 8
N_EXPERT_GROUPS = 8
GROUP_SIZE = N_EXPERTS // N_EXPERT_GROUPS
TOPK_GROUPS = 4
D_EXPERT = 256
D_SHARED = 256
ROUTED_SCALE = 2.5
PLE_DIM = 256
EPS = 1e-6

T_P = BATCH * SEQ
T_S = DEC_BATCH * DEC_SEQ
T_ALL = T_P + T_S
HALF = D_MODEL // 2
LANES = 128
VMEM_LIMIT = 48 * 1024 * 1024

BM_IN = 1024
BQ = 2 * WINDOW
EXPERTS_PER_MIX_STEP = N_EXPERTS * BQ // T_P
assert EXPERTS_PER_MIX_STEP * T_P == N_EXPERTS * BQ
SB = 16
BM_R = 512
BT_RANK = 512
BT_DEST = 2176
BT_COMB = 512
N_GROUPS = 2
T_GRP = T_ALL // N_GROUPS
assert T_GRP * N_GROUPS == T_ALL
BM_E = 1280
E_STRIP = 128
assert BM_E % E_STRIP == 0
N_ASSIGN = T_GRP * TOP_K
N_BLOCKS = -(-N_ASSIGN // BM_E) + N_EXPERTS
N_SLOTS = N_BLOCKS * BM_E

ROW_CHUNKS = HALF // LANES
SC_CORES = 2
SC_SUBCORES = 16
SC_WORKERS = SC_CORES * SC_SUBCORES
SC_CHUNK = 16
SC_RING = 4
SC_STEP = SC_WORKERS * SC_CHUNK
assert T_GRP % SC_STEP == 0
N_SUB = 2
SUB_BOUNDS = [T_GRP // SC_STEP * s // N_SUB * SC_STEP for s in range(N_SUB + 1)]


def _load_chunks(ref, n_rows, lead=()):
    return [ref[lead + (pl.ds(c, n_rows, stride=ROW_CHUNKS), slice(None))] for c in range(ROW_CHUNKS)]


def _store_chunks(ref, packed):
    n_rows = packed.shape[0]
    for c in range(ROW_CHUNKS):
        ref[pl.ds(c, n_rows, stride=ROW_CHUNKS), :] = packed[:, c * LANES:(c + 1) * LANES]


def _cparams(sem):
    return pltpu.CompilerParams(dimension_semantics=sem, vmem_limit_bytes=VMEM_LIMIT)


def _rms(x, g):
    return x * lax.rsqrt(jnp.mean(x * x, axis=-1, keepdims=True) + EPS) * g


def _sigmoid(x):
    return 1.0 / (1.0 + jnp.exp(-x))


def _pack_bf16_pairs(x):
    h = x.shape[-1] // 2
    return pltpu.pack_elementwise([x[:, :h], x[:, h:]], packed_dtype=BF16)


def _unpack_lo(p):
    return pltpu.bitcast(p << 16, F32)


def _unpack_hi(p):
    return pltpu.bitcast(p & jnp.uint32(0xFFFF0000), F32)


def _inproj_kernel(x_ref, g_ref, w_ref, c_ref, s1_ref, s2_ref, q_ref, k_ref, v_ref, u_ref):
    xn = _rms(x_ref[...], g_ref[...]).astype(BF16)
    z = jnp.dot(xn, w_ref[...], preferred_element_type=F32)
    c, s1, s2 = c_ref[...], s1_ref[...], s2_ref[...]

    def rope(t):
        return t * c + pltpu.roll(t, LANES - ROT_DIM // 2, 1) * s1 + pltpu.roll(t, ROT_DIM // 2, 1) * s2

    for i in range(ATT_WIDTH // LANES):
        sl = slice(i * LANES, (i + 1) * LANES)
        q_ref[:, sl] = (rope(z[:, sl]) * (HEAD_DIM ** -0.5)).astype(q_ref.dtype)
    k_ref[...] = rope(z[:, ATT_WIDTH:ATT_WIDTH + KV_WIDTH])
    v_ref[...] = z[:, ATT_WIDTH + KV_WIDTH:ATT_WIDTH + 2 * KV_WIDTH]
    u_ref[...] = z[:, ATT_WIDTH + 2 * KV_WIDTH:]


def _rope_tables(pos, reps=1):
    f32 = np.float32
    half = ROT_DIM // 2
    inv = np.power(f32(ROPE_THETA), -np.arange(half, dtype=f32) * f32(2.0) / f32(ROT_DIM)).astype(f32)
    ang = np.asarray(pos, f32)[:, None] * inv[None, :]
    cos, sin = np.cos(ang).astype(f32), np.sin(ang).astype(f32)
    n = len(pos)
    ones = np.ones((n, HEAD_DIM - ROT_DIM), f32)
    zeros = np.zeros((n, HEAD_DIM - ROT_DIM), f32)
    zh = np.zeros((n, half), f32)
    c = np.concatenate([cos, cos, ones], axis=1)
    s1 = np.concatenate([-sin, zh, zeros], axis=1)
    s2 = np.concatenate([zh, sin, zeros], axis=1)
    tile = lambda a: np.tile(a, (reps, LANES // HEAD_DIM))
    return tile(c), tile(s1), tile(s2)


def _inproj(x2d, g_mix, w_in_bf, tables, q_dtype):
    rows = x2d.shape[0]
    n_tab = tables[0].shape[0] // BM_IN
    row_spec = lambda w: pl.BlockSpec((BM_IN, w), lambda i: (i, 0))
    tab_spec = pl.BlockSpec((BM_IN, LANES), lambda i: (i % n_tab, 0))
    full = lambda a: pl.BlockSpec(a.shape, lambda i: (0,) * a.ndim)
    return pl.pallas_call(
        _inproj_kernel,
        grid=(rows // BM_IN,),
        in_specs=[row_spec(D_MODEL), full(g_mix), full(w_in_bf), tab_spec, tab_spec, tab_spec],
        out_specs=[row_spec(ATT_WIDTH), row_spec(KV_WIDTH), row_spec(KV_WIDTH), row_spec(POOL_WIDTH)],
        out_shape=[jax.ShapeDtypeStruct((rows, ATT_WIDTH), q_dtype),
                   jax.ShapeDtypeStruct((rows, KV_WIDTH), F32),
                   jax.ShapeDtypeStruct((rows, KV_WIDTH), F32),
                   jax.ShapeDtypeStruct((rows, POOL_WIDTH), F32)],
        compiler_params=_cparams(("parallel",)),
        name="inproj",
    )(x2d, g_mix, w_in_bf, *tables)


def _sink_column(sinks_ref, kv_head, rows_per_head):
    n = GQA_GROUP * rows_per_head
    grp = lax.broadcasted_iota(I32, (n, 1), 0) // rows_per_head
    col = jnp.full((n, 1), sinks_ref[kv_head * GQA_GROUP], F32)
    for g in range(1, GQA_GROUP):
        col = jnp.where(grp == g, sinks_ref[kv_head * GQA_GROUP + g], col)
    return col


def _band_mask(n_rows, rows_per_head, n_keys):
    i = lax.broadcasted_iota(I32, (n_rows, n_keys), 0) % rows_per_head
    c = lax.broadcasted_iota(I32, (n_rows, n_keys), 1)
    return (c >= i) & (c <= i + WINDOW), c


def _stack_heads(q, kv_head):
    return jnp.concatenate(
        [q[:, (kv_head * GQA_GROUP + g) * HEAD_DIM:(kv_head * GQA_GROUP + g + 1) * HEAD_DIM]
         for g in range(GQA_GROUP)], axis=0)


def _nt_dot(a, b):
    return lax.dot_general(a, b, (((1,), (1,)), ((), ())), preferred_element_type=F32)


POOL_HALO = 16
POOL_HEAD = 8
assert all(w == 2 << g for g, w in enumerate(POOL_WINDOWS)) and POOL_WINDOWS[-1] - 1 <= POOL_HALO


def _window_sums(ext_ref, n):
    lo, hi = POOL_HEAD, POOL_HEAD + POOL_HALO + n
    for p in range(POOL_GROUPS):
        lanes = slice(p * POOL_GROUP_DIM, POOL_WIDTH)
        ext_ref[lo:hi, lanes] = ext_ref[lo:hi, lanes] + ext_ref[lo - (1 << p):hi - (1 << p), lanes]


def _pool_out(d, wpool_ref, pscale_ref, gpool_ref):
    parts = [jnp.dot(d[:, g * POOL_GROUP_DIM:(g + 1) * POOL_GROUP_DIM].astype(BF16), wpool_ref[g],
                     preferred_element_type=F32) for g in range(POOL_GROUPS)]
    return _rms(jnp.concatenate(parts, axis=-1) * pscale_ref[...], gpool_ref[...])


def _mixer_tail(o_att, pooled, h, gatt_ref, wout_ref):
    mixed = jnp.concatenate([_rms(o_att, gatt_ref[...]), pooled], axis=-1)
    return h + jnp.dot(mixed.astype(BF16), wout_ref[...], preferred_element_type=F32)


def _mixer_prompt_kernel(sinks_ref, h_ref, q_ref, kc_ref, kp_ref, vc_ref, vp_ref, uc_ref, up_ref,
                         wpool_ref, pscale_ref, gatt_ref, gpool_ref, wout_ref, weg_ref, weu_ref, wed_ref,
                         h1_ref, wgu_bf_ref, wd_bf_ref, uext_ref):
    j = pl.program_id(1)
    wgu_bf_ref[:, :, :D_EXPERT] = weg_ref[...].astype(BF16)
    wgu_bf_ref[:, :, D_EXPERT:] = weu_ref[...].astype(BF16)
    wd_bf_ref[...] = wed_ref[...].astype(BF16)

    u = uc_ref[...]
    base = POOL_HEAD + POOL_HALO
    uext_ref[0:POOL_HEAD, :] = jnp.zeros((POOL_HEAD, POOL_WIDTH), F32)
    uext_ref[POOL_HEAD:base, :] = jnp.where(j > 0, up_ref[...], 0.0)
    uext_ref[base:base + BQ, :] = u
    _window_sums(uext_ref, BQ)
    pos = j * BQ + lax.broadcasted_iota(I32, (BQ, 1), 0)
    parts = []
    for g, w in enumerate(POOL_WINDOWS):
        sl = slice(g * POOL_GROUP_DIM, (g + 1) * POOL_GROUP_DIM)
        parts.append(uext_ref[base:base + BQ, sl] / jnp.minimum(pos + 1, w).astype(F32) - u[:, sl])
    pooled = _pool_out(jnp.concatenate(parts, axis=-1), wpool_ref, pscale_ref, gpool_ref)

    q = q_ref[...]
    k_all = jnp.concatenate([kp_ref[...], kc_ref[...]], axis=0).astype(BF16)
    v_all = jnp.concatenate([vp_ref[...], vc_ref[...]], axis=0).astype(BF16)
    ones = jnp.ones((WINDOW + BQ, HEAD_DIM), BF16)
    v_ones = [jnp.concatenate([v_all[:, hk * HEAD_DIM:(hk + 1) * HEAD_DIM], ones], axis=1)
              for hk in range(N_KV_HEADS)]
    band, col = _band_mask(GQA_GROUP * WINDOW, WINDOW, 2 * WINDOW)
    sinks = [_sink_column(sinks_ref, hk, WINDOW) for hk in range(N_KV_HEADS)]
    bands = []
    for b in range(BQ // WINDOW):
        rows = slice(b * WINDOW, (b + 1) * WINDOW)
        keys = slice(b * WINDOW, (b + 2) * WINDOW)
        mask = band & ((col >= WINDOW) | (j > 0)) if b == 0 else band
        heads = []
        for hk in range(N_KV_HEADS):
            sl = slice(hk * HEAD_DIM, (hk + 1) * HEAD_DIM)
            s = jnp.where(mask, _nt_dot(_stack_heads(q[rows], hk), k_all[keys, sl]), -jnp.inf)
            m = jnp.maximum(jnp.max(s, axis=-1, keepdims=True), sinks[hk])
            e = jnp.exp(s - m).astype(BF16)
            ov = jnp.dot(e, v_ones[hk][keys], preferred_element_type=F32)
            den = ov[:, HEAD_DIM:HEAD_DIM + 1] + jnp.exp(sinks[hk] - m)
            o = ov[:, :HEAD_DIM] / den
            heads += [o[g * WINDOW:(g + 1) * WINDOW] for g in range(GQA_GROUP)]
        bands.append(jnp.concatenate(heads, axis=-1))
    o_att = jnp.concatenate(bands, axis=0)
    h1_ref[...] = _mixer_tail(o_att, pooled, h_ref[...], gatt_ref, wout_ref)


def _mixer_sample_kernel(sinks_ref, h_ref, q_ref, kn_ref, vn_ref, u_ref, ck_ref, cv_ref, st_ref,
                         wpool_ref, pscale_ref, gatt_ref, gpool_ref, wout_ref, h1_in_ref,
                         h1_ref, ko_ref, vo_ref, po_ref, uext_ref):
    del h1_in_ref
    n_q = GQA_GROUP * DEC_SEQ
    n_keys = 2 * WINDOW
    band, col = _band_mask(n_q, DEC_SEQ, n_keys)
    mask = (band & (col < WINDOW + DEC_SEQ))[None]
    q3, kn3, vn3, u3 = q_ref[...], kn_ref[...], vn_ref[...], u_ref[...]
    ck, cv = ck_ref[...], cv_ref[...]
    ko_ref[:, 0:WINDOW - DEC_SEQ, :] = ck[:, DEC_SEQ:, :]
    ko_ref[:, WINDOW - DEC_SEQ:WINDOW, :] = kn3
    vo_ref[:, 0:WINDOW - DEC_SEQ, :] = cv[:, DEC_SEQ:, :]
    vo_ref[:, WINDOW - DEC_SEQ:WINDOW, :] = vn3
    pad = jnp.zeros((SB, WINDOW - DEC_SEQ, KV_WIDTH), F32)
    k_all = jnp.concatenate([ck, kn3, pad], axis=1).astype(BF16)
    v_all = jnp.concatenate([cv, vn3, pad], axis=1).astype(BF16)
    heads = []
    for hk in range(N_KV_HEADS):
        sl = slice(hk * HEAD_DIM, (hk + 1) * HEAD_DIM)
        qs = jnp.concatenate(
            [q3[:, :, (hk * GQA_GROUP + g) * HEAD_DIM:(hk * GQA_GROUP + g + 1) * HEAD_DIM]
             for g in range(GQA_GROUP)], axis=1).astype(BF16)
        sink = _sink_column(sinks_ref, hk, DEC_SEQ)[None]
        s = jnp.einsum("bqd,bkd->bqk", qs, k_all[:, :, sl], preferred_element_type=F32)
        s = jnp.where(mask, s, -jnp.inf)
        m = jnp.maximum(jnp.max(s, axis=-1, keepdims=True), sink)
        e = jnp.exp(s - m)
        den = jnp.sum(e, axis=-1, keepdims=True) + jnp.exp(sink - m)
        o = jnp.einsum("bqk,bkd->bqd", e.astype(BF16), v_all[:, :, sl], preferred_element_type=F32) / den
        heads += [o[:, g * DEC_SEQ:(g + 1) * DEC_SEQ, :] for g in range(GQA_GROUP)]
    o_att = jnp.concatenate(heads, axis=-1).reshape(SB * DEC_SEQ, ATT_WIDTH)

    uext_ref[:, 1:16, :] = st_ref[...]
    uext_ref[:, 16:16 + DEC_SEQ, :] = u3
    parts = []
    for g, w in enumerate(POOL_WINDOWS):
        sl = slice(g * POOL_GROUP_DIM, (g + 1) * POOL_GROUP_DIM)
        acc = u3[:, :, sl]
        for back in range(1, w):
            acc = acc + uext_ref[:, 16 - back:16 - back + DEC_SEQ, sl]
        parts.append(acc / float(w) - u3[:, :, sl])
    d = jnp.concatenate(parts, axis=-1).reshape(SB * DEC_SEQ, POOL_WIDTH)
    po_ref[...] = uext_ref[:, 16 + DEC_SEQ - POOL_STATE:16 + DEC_SEQ, :]
    pooled = _pool_out(d, wpool_ref, pscale_ref, gpool_ref)
    h1_ref[...] = _mixer_tail(o_att, pooled, h_ref[...], gatt_ref, wout_ref)


def _full_spec(a, n_grid):
    nd = a.ndim
    return pl.BlockSpec(a.shape, lambda *_: (0,) * nd)


def _mixer_prompt(sinks, x2d, q, k, v, u, wts, w_exp):
    nb = SEQ // BQ
    row = lambda w: pl.BlockSpec((BQ, w), lambda b, j: (b * nb + j, 0))
    prev = lambda w: pl.BlockSpec(
        (WINDOW, w), lambda b, j: (jnp.maximum((b * nb + j) * (BQ // WINDOW) - 1, 0), 0))
    uprev = pl.BlockSpec((POOL_HALO, POOL_WIDTH),
                         lambda b, j: (jnp.maximum((b * nb + j) * (BQ // POOL_HALO) - 1, 0), 0))
    smem = pl.BlockSpec(memory_space=pltpu.SMEM)
    per_step = lambda r, c: pl.BlockSpec((EXPERTS_PER_MIX_STEP, r, c), lambda b, j: (b * nb + j, 0, 0))
    return pl.pallas_call(
        _mixer_prompt_kernel,
        grid=(BATCH, nb),
        in_specs=[smem, row(D_MODEL), row(ATT_WIDTH), row(KV_WIDTH), prev(KV_WIDTH), row(KV_WIDTH),
                  prev(KV_WIDTH), row(POOL_WIDTH), uprev] + [_full_spec(w, 2) for w in wts]
                 + [per_step(D_MODEL, D_EXPERT), per_step(D_MODEL, D_EXPERT), per_step(D_EXPERT, D_MODEL)],
        out_specs=[row(D_MODEL), per_step(D_MODEL, 2 * D_EXPERT), per_step(D_EXPERT, D_MODEL)],
        out_shape=[jax.ShapeDtypeStruct((T_ALL, D_MODEL), F32),
                   jax.ShapeDtypeStruct((N_EXPERTS, D_MODEL, 2 * D_EXPERT), BF16),
                   jax.ShapeDtypeStruct((N_EXPERTS, D_EXPERT, D_MODEL), BF16)],
        scratch_shapes=[pltpu.VMEM((POOL_HEAD + POOL_HALO + BQ, POOL_WIDTH), F32)],
        compiler_params=_cparams(("parallel", "parallel")),
        name="mixer_prompt",
    )(sinks, x2d, q, k, k, v, v, u, u, *wts, *w_exp)


def _mixer_sample(sinks, x2d, q, k, v, u, cache_k, cache_v, state, wts, h1_buf):
    rows = SB * DEC_SEQ
    row = lambda w: pl.BlockSpec((rows, w), lambda i: (i, 0))
    bat = lambda a: pl.BlockSpec((SB,) + a.shape[1:], lambda i: (i, 0, 0))
    smem = pl.BlockSpec(memory_space=pltpu.SMEM)
    h1_blocks_before = T_P // rows
    n_in = 9 + len(wts)
    q, k, v, u = (a.reshape(DEC_BATCH, DEC_SEQ, a.shape[-1]) for a in (q, k, v, u))
    st = pl.BlockSpec((None, SB) + state.shape[2:], lambda i: (0, i, 0, 0))
    return pl.pallas_call(
        _mixer_sample_kernel,
        grid=(DEC_BATCH // SB,),
        in_specs=[smem, row(D_MODEL), bat(q), bat(k), bat(v), bat(u),
                  bat(cache_k), bat(cache_v), st] + [_full_spec(w, 1) for w in wts]
                 + [pl.BlockSpec(memory_space=pl.ANY)],
        out_specs=[pl.BlockSpec((rows, D_MODEL), lambda i: (h1_blocks_before + i, 0)),
                   bat(cache_k), bat(cache_v), st],
        out_shape=[jax.ShapeDtypeStruct((T_ALL, D_MODEL), F32),
                   jax.ShapeDtypeStruct(cache_k.shape, F32),
                   jax.ShapeDtypeStruct(cache_v.shape, F32),
                   jax.ShapeDtypeStruct(state.shape, F32)],
        scratch_shapes=[pltpu.VMEM((SB, 16 + DEC_SEQ, POOL_WIDTH), F32)],
        input_output_aliases={n_in: 0},
        compiler_params=_cparams(("parallel",)),
        name="mixer_sample",
    )(sinks, x2d, q, k, v, u, cache_k, cache_v, state, *wts, h1_buf)


def _first_max(vals, iota, n):
    m = jnp.max(vals, axis=0, keepdims=True)
    idx = jnp.min(jnp.where(vals == m, iota, n), axis=0, keepdims=True)
    return m, idx


def _router_kernel(h1_ref, gffn_ref, wrt_ref, bias_ref, xp_ref, idx_ref, wts_ref):
    xn = _rms(h1_ref[...], gffn_ref[...])
    w = wrt_ref[...]
    w_hi, x_hi = w.astype(BF16), xn.astype(BF16)
    w_lo, x_lo = (w - w_hi.astype(F32)).astype(BF16), (xn - x_hi.astype(F32)).astype(BF16)
    logits = _nt_dot(w_hi, x_hi) + (_nt_dot(w_hi, x_lo) + _nt_dot(w_lo, x_hi))
    scores = _sigmoid(logits)
    biased = scores + bias_ref[...]
    n_tok = biased.shape[1]
    neg = -jnp.inf

    iota_g = lax.broadcasted_iota(I32, (GROUP_SIZE, n_tok), 0)
    grp_rows = []
    for g in range(N_EXPERT_GROUPS):
        blk = biased[g * GROUP_SIZE:(g + 1) * GROUP_SIZE, :]
        top1, i1 = _first_max(blk, iota_g, GROUP_SIZE)
        top2 = jnp.max(jnp.where(iota_g == i1, neg, blk), axis=0, keepdims=True)
        grp_rows.append(top1 + top2)
    gs = jnp.concatenate(grp_rows, axis=0)

    iota_n = lax.broadcasted_iota(I32, (N_EXPERT_GROUPS, n_tok), 0)
    gsel = jnp.zeros((N_EXPERT_GROUPS, n_tok), jnp.bool_)
    for _ in range(TOPK_GROUPS):
        _, gi = _first_max(gs, iota_n, N_EXPERT_GROUPS)
        hit = iota_n == gi
        gsel = gsel | hit
        gs = jnp.where(hit, neg, gs)
    emask = jnp.concatenate(
        [jnp.broadcast_to(gsel[g:g + 1, :], (GROUP_SIZE, n_tok)) for g in range(N_EXPERT_GROUPS)], axis=0)
    masked = jnp.where(emask, biased, neg)

    iota_e = lax.broadcasted_iota(I32, (N_EXPERTS, n_tok), 0)
    idx_rows, sel_rows = [], []
    for _ in range(TOP_K):
        _, ei = _first_max(masked, iota_e, N_EXPERTS)
        hit = iota_e == ei
        idx_rows.append(ei)
        sel_rows.append(jnp.sum(jnp.where(hit, scores, 0.0), axis=0, keepdims=True))
        masked = jnp.where(hit, neg, masked)
    sel = jnp.concatenate(sel_rows, axis=0)
    idx_ref[...] = jnp.concatenate(idx_rows, axis=0)
    wts_ref[...] = sel / jnp.sum(sel, axis=0, keepdims=True) * ROUTED_SCALE
    _store_chunks(xp_ref, _pack_bf16_pairs(xn))


def _router(h1, group, g_ffn, w_router_t, bias_col):
    blk0 = group * T_GRP // BM_R
    colblk = pl.BlockSpec((TOP_K, BM_R), lambda i: (0, i))
    ws = [g_ffn, w_router_t, bias_col]
    return pl.pallas_call(
        _router_kernel,
        grid=(T_GRP // BM_R,),
        in_specs=[pl.BlockSpec((BM_R, D_MODEL), lambda i: (blk0 + i, 0))] + [_full_spec(w, 1) for w in ws],
        out_specs=[pl.BlockSpec((BM_R * ROW_CHUNKS, LANES), lambda i: (i, 0)), colblk, colblk],
        out_shape=[jax.ShapeDtypeStruct((T_GRP * ROW_CHUNKS, LANES), U32),
                   jax.ShapeDtypeStruct((TOP_K, T_GRP), I32),
                   jax.ShapeDtypeStruct((TOP_K, T_GRP), F32)],
        compiler_params=_cparams(("parallel",)),
        name="router",
    )(h1, *ws)


def _rank_kernel(idx_ref, tri_ref, rank_ref, cnt_ref, carry_ref):
    @pl.when(pl.program_id(0) == 0)
    def _():
        carry_ref[...] = jnp.zeros_like(carry_ref)

    idx = idx_ref[...]
    n_tok = idx.shape[1]
    iota_e = lax.broadcasted_iota(I32, (N_EXPERTS, n_tok), 0)
    member = jnp.zeros((N_EXPERTS, n_tok), F32)
    for k in range(TOP_K):
        member = member + jnp.where(iota_e == idx[k:k + 1, :], 1.0, 0.0)
    before = jnp.dot(member.astype(BF16), tri_ref[...], preferred_element_type=F32) + carry_ref[...]
    rows = [jnp.sum(jnp.where(iota_e == idx[k:k + 1, :], before, 0.0), axis=0, keepdims=True)
            for k in range(TOP_K)]
    rank_ref[...] = jnp.concatenate(rows, axis=0).astype(I32)
    carry_ref[...] = carry_ref[...] + jnp.sum(member, axis=1, keepdims=True)
    cnt_ref[...] = carry_ref[...].astype(I32)


def _rank(idx_t, tri):
    blk = pl.BlockSpec((TOP_K, BT_RANK), lambda i: (0, i))
    return pl.pallas_call(
        _rank_kernel,
        grid=(T_GRP // BT_RANK,),
        in_specs=[blk, _full_spec(tri, 1)],
        out_specs=[blk, pl.BlockSpec((N_EXPERTS, 1), lambda i: (0, 0))],
        out_shape=[jax.ShapeDtypeStruct((TOP_K, T_GRP), I32),
                   jax.ShapeDtypeStruct((N_EXPERTS, 1), I32)],
        scratch_shapes=[pltpu.VMEM((N_EXPERTS, 1), F32)],
        compiler_params=_cparams(("arbitrary",)),
        name="rank",
    )(idx_t, tri)


def _dest_kernel(idx_ref, rank_ref, cnt_ref, dest_ref, blk_e_ref, n_used_ref, blk_rows_ref):
    counts = cnt_ref[...]
    padded = (counts + (BM_E - 1)) // BM_E * BM_E
    r = lax.broadcasted_iota(I32, (N_EXPERTS, N_EXPERTS), 0)
    c = lax.broadcasted_iota(I32, (N_EXPERTS, N_EXPERTS), 1)
    padded_row = jnp.sum(jnp.where(r == c, padded, 0), axis=0, keepdims=True)
    pad_start = jnp.sum(jnp.where(c < r, padded_row, 0), axis=1, keepdims=True)

    idx = idx_ref[...]
    n_tok = idx.shape[1]
    iota_e = lax.broadcasted_iota(I32, (N_EXPERTS, n_tok), 0)
    rows = [jnp.sum(jnp.where(iota_e == idx[k:k + 1, :], pad_start, 0), axis=0, keepdims=True)
            for k in range(TOP_K)]
    dest_ref[...] = jnp.concatenate(rows, axis=0) + rank_ref[...]

    @pl.when(pl.program_id(0) == 0)
    def _():
        pad_end_row = jnp.sum(jnp.where(r <= c, padded, 0), axis=0, keepdims=True)
        b0 = lax.broadcasted_iota(I32, (N_BLOCKS_PAD, N_EXPERTS), 0) * BM_E
        be = jnp.minimum(jnp.sum(jnp.where(pad_end_row <= b0, 1, 0), axis=1, keepdims=True), N_EXPERTS - 1)
        blk_e_ref[...] = be
        n_used_ref[...] = pad_end_row[:, N_EXPERTS - 1:N_EXPERTS] // BM_E
        counts_row = jnp.sum(jnp.where(r == c, counts, 0), axis=0, keepdims=True)
        mine = lax.broadcasted_iota(I32, (N_BLOCKS_PAD, N_EXPERTS), 1) == be
        end_valid = jnp.sum(jnp.where(mine, pad_end_row - padded_row + counts_row, 0), axis=1, keepdims=True)
        blk_rows_ref[...] = jnp.clip(end_valid - b0[:, :1], 0, BM_E)


N_BLOCKS_PAD = (N_BLOCKS + 7) // 8 * 8


def _dest(idx_t, rank_t, counts):
    blk = pl.BlockSpec((TOP_K, BT_DEST), lambda i: (0, i))
    one = lambda s: pl.BlockSpec(s, lambda i: (0, 0))
    return pl.pallas_call(
        _dest_kernel,
        grid=(T_GRP // BT_DEST,),
        in_specs=[blk, blk, one((N_EXPERTS, 1))],
        out_specs=[blk, one((N_BLOCKS_PAD, 1)), one((1, 1)), one((N_BLOCKS_PAD, 1))],
        out_shape=[jax.ShapeDtypeStruct((TOP_K, T_GRP), I32),
                   jax.ShapeDtypeStruct((N_BLOCKS_PAD, 1), I32),
                   jax.ShapeDtypeStruct((1, 1), I32),
                   jax.ShapeDtypeStruct((N_BLOCKS_PAD, 1), I32)],
        compiler_params=_cparams(("arbitrary",)),
        name="dest",
    )(idx_t, rank_t, counts)


def _sc_mesh():
    return plsc.VectorSubcoreMesh(core_axis_name="c", subcore_axis_name="s")


def _sc_worker_id():
    return lax.axis_index("s") * SC_CORES + lax.axis_index("c")


def _dispatch_body(dest_hbm, xp_hbm, xs_hbm, idx_v, rows_v, sem_in, sem_out):
    n_chunks, _, n_tok = dest_hbm.shape
    per_worker = n_chunks // SC_WORKERS
    chunk0 = _sc_worker_id() * per_worker

    def loads(i):
        chunk = chunk0 + i
        t0 = pl.multiple_of(chunk * n_tok, n_tok)
        return (pltpu.make_async_copy(dest_hbm.at[chunk], idx_v.at[i % 2], sem_in.at[i % 2]),
                pltpu.make_async_copy(xp_hbm.at[pl.ds(t0, n_tok)], rows_v.at[i % 2], sem_in.at[i % 2]))

    def scatters(i):
        return [pltpu.make_async_copy(rows_v.at[i % 2], xs_hbm.at[idx_v.at[i % 2, k]], sem_out.at[i % 2])
                for k in range(TOP_K)]

    for cp in loads(0):
        cp.start()
    for i in range(per_worker):
        for cp in loads(i):
            cp.wait()
        if i >= 1:
            for cp in scatters(i - 1):
                cp.wait()
        if i + 1 < per_worker:
            for cp in loads(i + 1):
                cp.start()
        for cp in scatters(i):
            cp.start()
    for cp in scatters(per_worker - 1):
        cp.wait()


def _dispatch(dest_chunks, xp3):
    return pl.kernel(
        _dispatch_body,
        out_type=jax.ShapeDtypeStruct((N_SLOTS, ROW_CHUNKS, LANES), U32),
        mesh=_sc_mesh(),
        scratch_types=[pltpu.VMEM((2, TOP_K, SC_CHUNK), I32),
                       pltpu.VMEM((2, SC_CHUNK, ROW_CHUNKS, LANES), U32),
                       pltpu.SemaphoreType.DMA((2,)), pltpu.SemaphoreType.DMA((2,))],
        name="dispatch",
    )(dest_chunks, xp3)


def _gather_body(dest_hbm, ys_hbm, yt_hbm, idx_v, rows_v, sem_in, sem_out):
    n_chunks, _, n_tok = dest_hbm.shape
    per_worker = n_chunks // SC_WORKERS
    chunk0 = _sc_worker_id() * per_worker

    @pl.loop(0, per_worker)
    def _(i):
        chunk = chunk0 + i
        t0 = pl.multiple_of(chunk * n_tok, n_tok)
        pltpu.sync_copy(dest_hbm.at[chunk], idx_v)

        def gather(k):
            return pltpu.make_async_copy(ys_hbm.at[idx_v.at[k]], rows_v.at[k % SC_RING], sem_in.at[k % SC_RING])

        def store(k):
            return pltpu.make_async_copy(rows_v.at[k % SC_RING], yt_hbm.at[k, pl.ds(t0, n_tok)],
                                         sem_out.at[k % SC_RING])

        for k in range(SC_RING):
            gather(k).start()
        for k in range(TOP_K):
            gather(k).wait()
            store(k).start()
            if k + SC_RING < TOP_K:
                store(k).wait()
                gather(k + SC_RING).start()
        for k in range(TOP_K - SC_RING, TOP_K):
            store(k).wait()


def _gather(dest_chunks, ys3):
    n_chunks, _, n_tok = dest_chunks.shape
    assert n_chunks % SC_WORKERS == 0
    return pl.kernel(
        _gather_body,
        out_type=jax.ShapeDtypeStruct((TOP_K, n_chunks * n_tok, ROW_CHUNKS, LANES), U32),
        mesh=_sc_mesh(),
        scratch_types=[pltpu.VMEM((TOP_K, n_tok), I32),
                       pltpu.VMEM((SC_RING, n_tok, ROW_CHUNKS, LANES), U32),
                       pltpu.SemaphoreType.DMA((SC_RING,)), pltpu.SemaphoreType.DMA((SC_RING,))],
        name="gather",
    )(dest_chunks, ys3)


def _experts_kernel(blk_e_ref, n_used_ref, blk_rows_ref, xs_ref, wgu_ref, wd_ref, ys_ref):
    del blk_e_ref
    b = pl.program_id(0)

    def swiglu_rows(n_rows):
        chunks = _load_chunks(xs_ref, n_rows)
        x_lo = jnp.concatenate([_unpack_lo(p) for p in chunks], axis=-1).astype(BF16)
        x_hi = jnp.concatenate([_unpack_hi(p) for p in chunks], axis=-1).astype(BF16)
        gu = (jnp.dot(x_lo, wgu_ref[0, :HALF, :], preferred_element_type=F32)
              + jnp.dot(x_hi, wgu_ref[0, HALF:, :], preferred_element_type=F32))
        gate, up = gu[:, :D_EXPERT], gu[:, D_EXPERT:]
        hmid = (gate * _sigmoid(gate) * up).astype(BF16)
        _store_chunks(ys_ref, _pack_bf16_pairs(jnp.dot(hmid, wd_ref[0], preferred_element_type=F32)))

    @pl.when(b < n_used_ref[0])
    def _():
        valid = blk_rows_ref[b]
        for n_rows in range(E_STRIP, BM_E + 1, E_STRIP):
            @pl.when((valid > n_rows - E_STRIP) & (valid <= n_rows))
            def _(n_rows=n_rows):
                swiglu_rows(n_rows)


def _experts(blk_e, n_used, blk_rows, xs, wgu_bf, wd_bf):
    def blk(b, be, nu, nr):
        return jnp.minimum(b, nu[0] - 1)

    def by_expert(shape):
        return pl.BlockSpec((1,) + shape, lambda b, be, nu, nr: (be[blk(b, be, nu, nr)], 0, 0))

    tile = pl.BlockSpec((BM_E * ROW_CHUNKS, LANES), lambda b, be, nu, nr: (blk(b, be, nu, nr), 0))
    grid_spec = pltpu.PrefetchScalarGridSpec(
        num_scalar_prefetch=3,
        grid=(N_BLOCKS,),
        in_specs=[tile, by_expert((D_MODEL, 2 * D_EXPERT)), by_expert((D_EXPERT, D_MODEL))],
        out_specs=tile,
    )
    return pl.pallas_call(
        _experts_kernel,
        grid_spec=grid_spec,
        out_shape=jax.ShapeDtypeStruct((N_SLOTS * ROW_CHUNKS, LANES), U32),
        compiler_params=_cparams(("arbitrary",)),
        name="experts",
    )(blk_e, n_used, blk_rows, xs, wgu_bf, wd_bf)


def _combine_kernel(yt_ref, wts_ref, h1_ref, p_ref, gffn_ref, wsgu_ref, wsd_ref, gple_ref, wpg_ref, wpp_ref,
                    gfin_ref, *y_refs):
    y_ref = y_refs[-1]
    h1 = h1_ref[...]
    gu = jnp.dot(_rms(h1, gffn_ref[...]).astype(BF16), wsgu_ref[...], preferred_element_type=F32)
    sgate, sup = gu[:, :D_SHARED], gu[:, D_SHARED:]
    hsh = h1 + jnp.dot((sgate * _sigmoid(sgate) * sup).astype(BF16), wsd_ref[...], preferred_element_type=F32)
    wts = jnp.transpose(wts_ref[...])
    lo = [jnp.zeros((BT_COMB, LANES), F32) for _ in range(ROW_CHUNKS)]
    hi = [jnp.zeros((BT_COMB, LANES), F32) for _ in range(ROW_CHUNKS)]
    for k in range(TOP_K):
        w = wts[:, k:k + 1]
        for c, p in enumerate(_load_chunks(yt_ref, BT_COMB, lead=(k,))):
            lo[c] = lo[c] + w * _unpack_lo(p)
            hi[c] = hi[c] + w * _unpack_hi(p)
    h2 = hsh + jnp.concatenate(lo + hi, axis=-1)
    gate = _sigmoid(jnp.dot(_rms(h2, gple_ref[...]).astype(BF16), wpg_ref[...], preferred_element_type=F32))
    proj = jnp.dot(p_ref[...].astype(BF16), wpp_ref[...], preferred_element_type=F32)
    y_ref[...] = _rms(h2 + proj * gate, gfin_ref[...])


def _combine(yt, yt_row0, wts_t, wts_row0, h1, tok_row0, n_rows, p2d, p_row0, ws, y_prev, out_rows, out_row0):
    assert all(r % BT_COMB == 0 for r in (yt_row0, wts_row0, tok_row0, n_rows, p_row0, out_row0))
    g0, w0, t0, p0, o0 = (r // BT_COMB for r in (yt_row0, wts_row0, tok_row0, p_row0, out_row0))
    in_specs = [pl.BlockSpec((TOP_K, BT_COMB * ROW_CHUNKS, LANES), lambda i: (0, g0 + i, 0)),
                pl.BlockSpec((TOP_K, BT_COMB), lambda i: (0, w0 + i)),
                pl.BlockSpec((BT_COMB, D_MODEL), lambda i: (t0 + i, 0)),
                pl.BlockSpec((BT_COMB, PLE_DIM), lambda i: (p0 + i, 0))] + [_full_spec(w, 1) for w in ws]
    args = [yt, wts_t, h1, p2d, *ws]
    aliases = {}
    if y_prev is not None:
        in_specs.append(pl.BlockSpec(memory_space=pl.ANY))
        aliases = {len(args): 0}
        args.append(y_prev)
    return pl.pallas_call(
        _combine_kernel,
        grid=(n_rows // BT_COMB,),
        in_specs=in_specs,
        out_specs=pl.BlockSpec((BT_COMB, D_MODEL), lambda i: (o0 + i, 0)),
        out_shape=jax.ShapeDtypeStruct((out_rows, D_MODEL), F32),
        input_output_aliases=aliases,
        compiler_params=_cparams(("parallel",)),
        name="combine",
    )(*args)


def kernel(x_prompt, x_sample, cache_k, cache_v, state_pool, p_prompt, p_sample, g_mix, w_in, attn_sinks,
           w_pool, pool_scale, g_att_out, g_pool_out, w_out, g_ffn, w_router, router_bias, w_exp_gate,
           w_exp_up, w_exp_down, w_sh_gate, w_sh_up, w_sh_down, g_ple, w_ple_gate, w_ple_proj, g_final):
    row = lambda a: a.reshape(1, -1)
    xp2d = x_prompt.reshape(T_P, D_MODEL)
    xs2d = x_sample.reshape(T_S, D_MODEL)
    w_in_bf = w_in[0].astype(BF16)
    mixer_wts = [w_pool[0].astype(BF16), row(pool_scale[0]), row(g_att_out[0]), row(g_pool_out[0]),
                 w_out[0].astype(BF16)]

    tab_p = _rope_tables(np.arange(SEQ))
    tab_s = _rope_tables(PAST_LEN + np.arange(DEC_SEQ), reps=BM_IN // DEC_SEQ)

    q_p, k_p, v_p, u_p = _inproj(xp2d, row(g_mix[0]), w_in_bf, tab_p, BF16)
    q_s, k_s, v_s, u_s = _inproj(xs2d, row(g_mix[0]), w_in_bf, tab_s, F32)

    h1, wgu_bf, wd_bf = _mixer_prompt(attn_sinks[0], xp2d, q_p, k_p, v_p, u_p, mixer_wts,
                                      (w_exp_gate[0], w_exp_up[0], w_exp_down[0]))
    h1, k_sample, v_sample, pool_sample = _mixer_sample(
        attn_sinks[0], xs2d, q_s, k_s, v_s, u_s,
        cache_k[0].reshape(DEC_BATCH, WINDOW, KV_WIDTH), cache_v[0].reshape(DEC_BATCH, WINDOW, KV_WIDTH),
        state_pool, mixer_wts, h1)

    g_ffn_row = row(g_ffn[0])
    router_wts = (g_ffn_row, w_router[0].T, router_bias[0].reshape(N_EXPERTS, 1))
    tri = (lax.broadcasted_iota(I32, (BT_RANK, BT_RANK), 0)
           < lax.broadcasted_iota(I32, (BT_RANK, BT_RANK), 1)).astype(BF16)

    def index_chunks(d, n_tok):
        return d.reshape(TOP_K, d.shape[1] // n_tok, n_tok).transpose(1, 0, 2)

    groups = []
    for g in range(N_GROUPS):
        xp, idx_t, wts_t = _router(h1, g, *router_wts)
        rank_t, counts = _rank(idx_t, tri)
        dest_t, *plan = _dest(idx_t, rank_t, counts)
        dest_chunks = index_chunks(dest_t, SC_CHUNK)
        xs = _dispatch(dest_chunks, xp.reshape(T_GRP, ROW_CHUNKS, LANES))
        groups.append((wts_t, dest_chunks, xs, [a.reshape(-1) for a in plan]))

    ple_wts = [g_ffn_row, jnp.concatenate([w_sh_gate[0], w_sh_up[0]], axis=1).astype(BF16),
               w_sh_down[0].astype(BF16),
               row(g_ple[0]), w_ple_gate[0].astype(BF16), w_ple_proj[0].astype(BF16), row(g_final)]
    pp2d = p_prompt[0].reshape(T_P, PLE_DIM)
    ps2d = p_sample[0].reshape(T_S, PLE_DIM)
    y_p = y_s = None
    for g, (wts_t, dest_chunks, xs, plan) in enumerate(groups):
        lo = g * T_GRP
        ys = _experts(*plan, xs.reshape(N_SLOTS * ROW_CHUNKS, LANES), wgu_bf, wd_bf)
        ys3 = ys.reshape(N_SLOTS, ROW_CHUNKS, LANES)
        for s in range(N_SUB):
            a, b = lo + SUB_BOUNDS[s], lo + SUB_BOUNDS[s + 1]
            yt = _gather(dest_chunks[SUB_BOUNDS[s] // SC_CHUNK:SUB_BOUNDS[s + 1] // SC_CHUNK], ys3)
            yt = yt.reshape(TOP_K, (b - a) * ROW_CHUNKS, LANES)
            if a < T_P:
                n = min(b, T_P) - a
                y_p = _combine(yt, 0, wts_t, a - lo, h1, a, n, pp2d, a, ple_wts, y_p, T_P, a)
            if b > T_P:
                s0 = max(a, T_P)
                y_s = _combine(yt, s0 - a, wts_t, s0 - lo, h1, s0, b - s0, ps2d, s0 - T_P, ple_wts, y_s, T_S,
                               s0 - T_P)

    kv5 = lambda a, b: a.reshape(1, b, WINDOW, N_KV_HEADS, HEAD_DIM)
    k_prompt = kv5(k_p.reshape(BATCH, SEQ, KV_WIDTH)[:, SEQ - WINDOW:], BATCH)
    v_prompt = kv5(v_p.reshape(BATCH, SEQ, KV_WIDTH)[:, SEQ - WINDOW:], BATCH)
    pool_prompt = u_p.reshape(BATCH, SEQ, POOL_WIDTH)[:, SEQ - POOL_STATE:][None]
    return (y_p.reshape(BATCH, SEQ, D_MODEL), y_s.reshape(DEC_BATCH, DEC_SEQ, D_MODEL),
            k_prompt, v_prompt, pool_prompt,
            kv5(k_sample, DEC_BATCH), kv5(v_sample, DEC_BATCH), pool_sample)
```

```python
import functools

import numpy as np
import jax
import jax.numpy as jnp
from jax import lax
from jax.experimental import pallas as pl
from jax.experimental.pallas import tpu as pltpu
from jax.experimental.pallas import tpu_sc as plsc

F32 = jnp.float32
BF16 = jnp.bfloat16
U32 = jnp.uint32
I32 = jnp.int32

D_MODEL = 1024
BATCH = 8
SEQ = 2048
DEC_BATCH = 128
DEC_SEQ = 8
PAST_LEN = 16384
N_Q_HEADS = 8
N_KV_HEADS = 2
HEAD_DIM = 64
GQA_GROUP = N_Q_HEADS // N_KV_HEADS
ATT_WIDTH = N_Q_HEADS * HEAD_DIM
KV_WIDTH = N_KV_HEADS * HEAD_DIM
WINDOW = 128
ROPE_THETA = 500000.0
ROT_DIM = HEAD_DIM // 4
POOL_WINDOWS = (2, 4, 8, 16)
POOL_GROUPS = 4
POOL_WIDTH = D_MODEL - ATT_WIDTH
POOL_GROUP_DIM = POOL_WIDTH // POOL_GROUPS
POOL_STATE = 15
IN_WIDTH = ATT_WIDTH + 2 * KV_WIDTH + POOL_WIDTH
N_EXPERTS = 64
TOP_K = 8
N_EXPERT_GROUPS = 8
GROUP_SIZE = N_EXPERTS // N_EXPERT_GROUPS
TOPK_GROUPS = 4
D_EXPERT = 256
D_SHARED = 256
ROUTED_SCALE = 2.5
PLE_DIM = 256
EPS = 1e-6

T_P = BATCH * SEQ
T_S = DEC_BATCH * DEC_SEQ
T_ALL = T_P + T_S
HALF = D_MODEL // 2
LANES = 128
VMEM_LIMIT = 48 * 1024 * 1024

BM_IN = 1024
BQ = 2 * WINDOW
EXPERTS_PER_MIX_STEP = N_EXPERTS * BQ // T_P
assert EXPERTS_PER_MIX_STEP * T_P == N_EXPERTS * BQ
SB = 16
BM_R = 512
BT_RANK = 512
BT_DEST = 2176
BT_COMB = 512
N_GROUPS = 2
T_GRP = T_ALL // N_GROUPS
assert T_GRP * N_GROUPS == T_ALL
BM_E = 1280
E_STRIP = 128
assert BM_E % E_STRIP == 0
N_ASSIGN = T_GRP * TOP_K
N_BLOCKS = -(-N_ASSIGN // BM_E) + N_EXPERTS
N_SLOTS = N_BLOCKS * BM_E

ROW_CHUNKS = HALF // LANES
SC_CORES = 2
SC_SUBCORES = 16
SC_WORKERS = SC_CORES * SC_SUBCORES
SC_CHUNK = 16
SC_RING = 4
SC_STEP = SC_WORKERS * SC_CHUNK
assert T_GRP % SC_STEP == 0
N_SUB = 2
SUB_BOUNDS = [T_GRP // SC_STEP * s // N_SUB * SC_STEP for s in range(N_SUB + 1)]


def _load_chunks(ref, n_rows, lead=()):
    return [ref[lead + (pl.ds(c, n_rows, stride=ROW_CHUNKS), slice(None))] for c in range(ROW_CHUNKS)]


def _store_chunks(ref, packed):
    n_rows = packed.shape[0]
    for c in range(ROW_CHUNKS):
        ref[pl.ds(c, n_rows, stride=ROW_CHUNKS), :] = packed[:, c * LANES:(c + 1) * LANES]


def _cparams(sem):
    return pltpu.CompilerParams(dimension_semantics=sem, vmem_limit_bytes=VMEM_LIMIT)


def _rms(x, g):
    return x * lax.rsqrt(jnp.mean(x * x, axis=-1, keepdims=True) + EPS) * g


def _sigmoid(x):
    return 1.0 / (1.0 + jnp.exp(-x))


def _pack_bf16_pairs(x):
    h = x.shape[-1] // 2
    return pltpu.pack_elementwise([x[:, :h], x[:, h:]], packed_dtype=BF16)


def _unpack_lo(p):
    return pltpu.bitcast(p << 16, F32)


def _unpack_hi(p):
    return pltpu.bitcast(p & jnp.uint32(0xFFFF0000), F32)


def _inproj_kernel(x_ref, g_ref, w_ref, c_ref, s1_ref, s2_ref, q_ref, k_ref, v_ref, u_ref):
    xn = _rms(x_ref[...], g_ref[...]).astype(BF16)
    z = jnp.dot(xn, w_ref[...], preferred_element_type=F32)
    c, s1, s2 = c_ref[...], s1_ref[...], s2_ref[...]

    def rope(t):
        return t * c + pltpu.roll(t, LANES - ROT_DIM // 2, 1) * s1 + pltpu.roll(t, ROT_DIM // 2, 1) * s2

    for i in range(ATT_WIDTH // LANES):
        sl = slice(i * LANES, (i + 1) * LANES)
        q_ref[:, sl] = (rope(z[:, sl]) * (HEAD_DIM ** -0.5)).astype(q_ref.dtype)
    k_ref[...] = rope(z[:, ATT_WIDTH:ATT_WIDTH + KV_WIDTH])
    v_ref[...] = z[:, ATT_WIDTH + KV_WIDTH:ATT_WIDTH + 2 * KV_WIDTH]
    u_ref[...] = z[:, ATT_WIDTH + 2 * KV_WIDTH:]


def _rope_tables(pos, reps=1):
    f32 = np.float32
    half = ROT_DIM // 2
    inv = np.power(f32(ROPE_THETA), -np.arange(half, dtype=f32) * f32(2.0) / f32(ROT_DIM)).astype(f32)
    ang = np.asarray(pos, f32)[:, None] * inv[None, :]
    cos, sin = np.cos(ang).astype(f32), np.sin(ang).astype(f32)
    n = len(pos)
    ones = np.ones((n, HEAD_DIM - ROT_DIM), f32)
    zeros = np.zeros((n, HEAD_DIM - ROT_DIM), f32)
    zh = np.zeros((n, half), f32)
    c = np.concatenate([cos, cos, ones], axis=1)
    s1 = np.concatenate([-sin, zh, zeros], axis=1)
    s2 = np.concatenate([zh, sin, zeros], axis=1)
    tile = lambda a: np.tile(a, (reps, LANES // HEAD_DIM))
    return tile(c), tile(s1), tile(s2)


def _inproj(x2d, g_mix, w_in_bf, tables, q_dtype):
    rows = x2d.shape[0]
    n_tab = tables[0].shape[0] // BM_IN
    row_spec = lambda w: pl.BlockSpec((BM_IN, w), lambda i: (i, 0))
    tab_spec = pl.BlockSpec((BM_IN, LANES), lambda i: (i % n_tab, 0))
    full = lambda a: pl.BlockSpec(a.shape, lambda i: (0,) * a.ndim)
    return pl.pallas_call(
        _inproj_kernel,
        grid=(rows // BM_IN,),
        in_specs=[row_spec(D_MODEL), full(g_mix), full(w_in_bf), tab_spec, tab_spec, tab_spec],
        out_specs=[row_spec(ATT_WIDTH), row_spec(KV_WIDTH), row_spec(KV_WIDTH), row_spec(POOL_WIDTH)],
        out_shape=[jax.ShapeDtypeStruct((rows, ATT_WIDTH), q_dtype),
                   jax.ShapeDtypeStruct((rows, KV_WIDTH), F32),
                   jax.ShapeDtypeStruct((rows, KV_WIDTH), F32),
                   jax.ShapeDtypeStruct((rows, POOL_WIDTH), F32)],
        compiler_params=_cparams(("parallel",)),
        name="inproj",
    )(x2d, g_mix, w_in_bf, *tables)


def _sink_column(sinks_ref, kv_head, rows_per_head):
    n = GQA_GROUP * rows_per_head
    grp = lax.broadcasted_iota(I32, (n, 1), 0) // rows_per_head
    col = jnp.full((n, 1), sinks_ref[kv_head * GQA_GROUP], F32)
    for g in range(1, GQA_GROUP):
        col = jnp.where(grp == g, sinks_ref[kv_head * GQA_GROUP + g], col)
    return col


def _band_mask(n_rows, rows_per_head, n_keys):
    i = lax.broadcasted_iota(I32, (n_rows, n_keys), 0) % rows_per_head
    c = lax.broadcasted_iota(I32, (n_rows, n_keys), 1)
    return (c >= i) & (c <= i + WINDOW), c


def _stack_heads(q, kv_head):
    return jnp.concatenate(
        [q[:, (kv_head * GQA_GROUP + g) * HEAD_DIM:(kv_head * GQA_GROUP + g + 1) * HEAD_DIM]
         for g in range(GQA_GROUP)], axis=0)


def _nt_dot(a, b):
    return lax.dot_general(a, b, (((1,), (1,)), ((), ())), preferred_element_type=F32)


POOL_HALO = 16
POOL_HEAD = 8
assert all(w == 2 << g for g, w in enumerate(POOL_WINDOWS)) and POOL_WINDOWS[-1] - 1 <= POOL_HALO


def _window_sums(ext_ref, n):
    lo, hi = POOL_HEAD, POOL_HEAD + POOL_HALO + n
    for p in range(POOL_GROUPS):
        lanes = slice(p * POOL_GROUP_DIM, POOL_WIDTH)
        ext_ref[lo:hi, lanes] = ext_ref[lo:hi, lanes] + ext_ref[lo - (1 << p):hi - (1 << p), lanes]


def _pool_out(d, wpool_ref, pscale_ref, gpool_ref):
    parts = [jnp.dot(d[:, g * POOL_GROUP_DIM:(g + 1) * POOL_GROUP_DIM].astype(BF16), wpool_ref[g],
                     preferred_element_type=F32) for g in range(POOL_GROUPS)]
    return _rms(jnp.concatenate(parts, axis=-1) * pscale_ref[...], gpool_ref[...])


def _mixer_tail(o_att, pooled, h, gatt_ref, wout_ref):
    mixed = jnp.concatenate([_rms(o_att, gatt_ref[...]), pooled], axis=-1)
    return h + jnp.dot(mixed.astype(BF16), wout_ref[...], preferred_element_type=F32)


def _mixer_prompt_kernel(sinks_ref, h_ref, q_ref, kc_ref, kp_ref, vc_ref, vp_ref, uc_ref, up_ref,
                         wpool_ref, pscale_ref, gatt_ref, gpool_ref, wout_ref, weg_ref, weu_ref, wed_ref,
                         h1_ref, wgu_bf_ref, wd_bf_ref, uext_ref):
    j = pl.program_id(1)
    wgu_bf_ref[:, :, :D_EXPERT] = weg_ref[...].astype(BF16)
    wgu_bf_ref[:, :, D_EXPERT:] = weu_ref[...].astype(BF16)
    wd_bf_ref[...] = wed_ref[...].astype(BF16)

    u = uc_ref[...]
    base = POOL_HEAD + POOL_HALO
    uext_ref[0:POOL_HEAD, :] = jnp.zeros((POOL_HEAD, POOL_WIDTH), F32)
    uext_ref[POOL_HEAD:base, :] = jnp.where(j > 0, up_ref[...], 0.0)
    uext_ref[base:base + BQ, :] = u
    _window_sums(uext_ref, BQ)
    pos = j * BQ + lax.broadcasted_iota(I32, (BQ, 1), 0)
    parts = []
    for g, w in enumerate(POOL_WINDOWS):
        sl = slice(g * POOL_GROUP_DIM, (g + 1) * POOL_GROUP_DIM)
        parts.append(uext_ref[base:base + BQ, sl] / jnp.minimum(pos + 1, w).astype(F32) - u[:, sl])
    pooled = _pool_out(jnp.concatenate(parts, axis=-1), wpool_ref, pscale_ref, gpool_ref)

    q = q_ref[...]
    k_all = jnp.concatenate([kp_ref[...], kc_ref[...]], axis=0).astype(BF16)
    v_all = jnp.concatenate([vp_ref[...], vc_ref[...]], axis=0).astype(BF16)
    ones = jnp.ones((WINDOW + BQ, HEAD_DIM), BF16)
    v_ones = [jnp.concatenate([v_all[:, hk * HEAD_DIM:(hk + 1) * HEAD_DIM], ones], axis=1)
              for hk in range(N_KV_HEADS)]
    band, col = _band_mask(GQA_GROUP * WINDOW, WINDOW, 2 * WINDOW)
    sinks = [_sink_column(sinks_ref, hk, WINDOW) for hk in range(N_KV_HEADS)]
    bands = []
    for b in range(BQ // WINDOW):
        rows = slice(b * WINDOW, (b + 1) * WINDOW)
        keys = slice(b * WINDOW, (b + 2) * WINDOW)
        mask = band & ((col >= WINDOW) | (j > 0)) if b == 0 else band
        heads = []
        for hk in range(N_KV_HEADS):
            sl = slice(hk * HEAD_DIM, (hk + 1) * HEAD_DIM)
            s = jnp.where(mask, _nt_dot(_stack_heads(q[rows], hk), k_all[keys, sl]), -jnp.inf)
            m = jnp.maximum(jnp.max(s, axis=-1, keepdims=True), sinks[hk])
            e = jnp.exp(s - m).astype(BF16)
            ov = jnp.dot(e, v_ones[hk][keys], preferred_element_type=F32)
            den = ov[:, HEAD_DIM:HEAD_DIM + 1] + jnp.exp(sinks[hk] - m)
            o = ov[:, :HEAD_DIM] / den
            heads += [o[g * WINDOW:(g + 1) * WINDOW] for g in range(GQA_GROUP)]
        bands.append(jnp.concatenate(heads, axis=-1))
    o_att = jnp.concatenate(bands, axis=0)
    h1_ref[...] = _mixer_tail(o_att, pooled, h_ref[...], gatt_ref, wout_ref)


def _mixer_sample_kernel(sinks_ref, h_ref, q_ref, kn_ref, vn_ref, u_ref, ck_ref, cv_ref, st_ref,
                         wpool_ref, pscale_ref, gatt_ref, gpool_ref, wout_ref, h1_in_ref,
                         h1_ref, ko_ref, vo_ref, po_ref, uext_ref):
    del h1_in_ref
    n_q = GQA_GROUP * DEC_SEQ
    n_keys = 2 * WINDOW
    band, col = _band_mask(n_q, DEC_SEQ, n_keys)
    mask = (band & (col < WINDOW + DEC_SEQ))[None]
    q3, kn3, vn3, u3 = q_ref[...], kn_ref[...], vn_ref[...], u_ref[...]
    ck, cv = ck_ref[...], cv_ref[...]
    ko_ref[:, 0:WINDOW - DEC_SEQ, :] = ck[:, DEC_SEQ:, :]
    ko_ref[:, WINDOW - DEC_SEQ:WINDOW, :] = kn3
    vo_ref[:, 0:WINDOW - DEC_SEQ, :] = cv[:, DEC_SEQ:, :]
    vo_ref[:, WINDOW - DEC_SEQ:WINDOW, :] = vn3
    pad = jnp.zeros((SB, WINDOW - DEC_SEQ, KV_WIDTH), F32)
    k_all = jnp.concatenate([ck, kn3, pad], axis=1).astype(BF16)
    v_all = jnp.concatenate([cv, vn3, pad], axis=1).astype(BF16)
    heads = []
    for hk in range(N_KV_HEADS):
        sl = slice(hk * HEAD_DIM, (hk + 1) * HEAD_DIM)
        qs = jnp.concatenate(
            [q3[:, :, (hk * GQA_GROUP + g) * HEAD_DIM:(hk * GQA_GROUP + g + 1) * HEAD_DIM]
             for g in range(GQA_GROUP)], axis=1).astype(BF16)
        sink = _sink_column(sinks_ref, hk, DEC_SEQ)[None]
        s = jnp.einsum("bqd,bkd->bqk", qs, k_all[:, :, sl], preferred_element_type=F32)
        s = jnp.where(mask, s, -jnp.inf)
        m = jnp.maximum(jnp.max(s, axis=-1, keepdims=True), sink)
        e = jnp.exp(s - m)
        den = jnp.sum(e, axis=-1, keepdims=True) + jnp.exp(sink - m)
        o = jnp.einsum("bqk,bkd->bqd", e.astype(BF16), v_all[:, :, sl], preferred_element_type=F32) / den
        heads += [o[:, g * DEC_SEQ:(g + 1) * DEC_SEQ, :] for g in range(GQA_GROUP)]
    o_att = jnp.concatenate(heads, axis=-1).reshape(SB * DEC_SEQ, ATT_WIDTH)

    uext_ref[:, 1:16, :] = st_ref[...]
    uext_ref[:, 16:16 + DEC_SEQ, :] = u3
    parts = []
    for g, w in enumerate(POOL_WINDOWS):
        sl = slice(g * POOL_GROUP_DIM, (g + 1) * POOL_GROUP_DIM)
        acc = u3[:, :, sl]
        for back in range(1, w):
            acc = acc + uext_ref[:, 16 - back:16 - back + DEC_SEQ, sl]
        parts.append(acc / float(w) - u3[:, :, sl])
    d = jnp.concatenate(parts, axis=-1).reshape(SB * DEC_SEQ, POOL_WIDTH)
    po_ref[...] = uext_ref[:, 16 + DEC_SEQ - POOL_STATE:16 + DEC_SEQ, :]
    pooled = _pool_out(d, wpool_ref, pscale_ref, gpool_ref)
    h1_ref[...] = _mixer_tail(o_att, pooled, h_ref[...], gatt_ref, wout_ref)


def _full_spec(a, n_grid):
    nd = a.ndim
    return pl.BlockSpec(a.shape, lambda *_: (0,) * nd)


def _mixer_prompt(sinks, x2d, q, k, v, u, wts, w_exp):
    nb = SEQ // BQ
    row = lambda w: pl.BlockSpec((BQ, w), lambda b, j: (b * nb + j, 0))
    prev = lambda w: pl.BlockSpec(
        (WINDOW, w), lambda b, j: (jnp.maximum((b * nb + j) * (BQ // WINDOW) - 1, 0), 0))
    uprev = pl.BlockSpec((POOL_HALO, POOL_WIDTH),
                         lambda b, j: (jnp.maximum((b * nb + j) * (BQ // POOL_HALO) - 1, 0), 0))
    smem = pl.BlockSpec(memory_space=pltpu.SMEM)
    per_step = lambda r, c: pl.BlockSpec((EXPERTS_PER_MIX_STEP, r, c), lambda b, j: (b * nb + j, 0, 0))
    return pl.pallas_call(
        _mixer_prompt_kernel,
        grid=(BATCH, nb),
        in_specs=[smem, row(D_MODEL), row(ATT_WIDTH), row(KV_WIDTH), prev(KV_WIDTH), row(KV_WIDTH),
                  prev(KV_WIDTH), row(POOL_WIDTH), uprev] + [_full_spec(w, 2) for w in wts]
                 + [per_step(D_MODEL, D_EXPERT), per_step(D_MODEL, D_EXPERT), per_step(D_EXPERT, D_MODEL)],
        out_specs=[row(D_MODEL), per_step(D_MODEL, 2 * D_EXPERT), per_step(D_EXPERT, D_MODEL)],
        out_shape=[jax.ShapeDtypeStruct((T_ALL, D_MODEL), F32),
                   jax.ShapeDtypeStruct((N_EXPERTS, D_MODEL, 2 * D_EXPERT), BF16),
                   jax.ShapeDtypeStruct((N_EXPERTS, D_EXPERT, D_MODEL), BF16)],
        scratch_shapes=[pltpu.VMEM((POOL_HEAD + POOL_HALO + BQ, POOL_WIDTH), F32)],
        compiler_params=_cparams(("parallel", "parallel")),
        name="mixer_prompt",
    )(sinks, x2d, q, k, k, v, v, u, u, *wts, *w_exp)


def _mixer_sample(sinks, x2d, q, k, v, u, cache_k, cache_v, state, wts, h1_buf):
    rows = SB * DEC_SEQ
    row = lambda w: pl.BlockSpec((rows, w), lambda i: (i, 0))
    bat = lambda a: pl.BlockSpec((SB,) + a.shape[1:], lambda i: (i, 0, 0))
    smem = pl.BlockSpec(memory_space=pltpu.SMEM)
    h1_blocks_before = T_P // rows
    n_in = 9 + len(wts)
    q, k, v, u = (a.reshape(DEC_BATCH, DEC_SEQ, a.shape[-1]) for a in (q, k, v, u))
    st = pl.BlockSpec((None, SB) + state.shape[2:], lambda i: (0, i, 0, 0))
    return pl.pallas_call(
        _mixer_sample_kernel,
        grid=(DEC_BATCH // SB,),
        in_specs=[smem, row(D_MODEL), bat(q), bat(k), bat(v), bat(u),
                  bat(cache_k), bat(cache_v), st] + [_full_spec(w, 1) for w in wts]
                 + [pl.BlockSpec(memory_space=pl.ANY)],
        out_specs=[pl.BlockSpec((rows, D_MODEL), lambda i: (h1_blocks_before + i, 0)),
                   bat(cache_k), bat(cache_v), st],
        out_shape=[jax.ShapeDtypeStruct((T_ALL, D_MODEL), F32),
                   jax.ShapeDtypeStruct(cache_k.shape, F32),
                   jax.ShapeDtypeStruct(cache_v.shape, F32),
                   jax.ShapeDtypeStruct(state.shape, F32)],
        scratch_shapes=[pltpu.VMEM((SB, 16 + DEC_SEQ, POOL_WIDTH), F32)],
        input_output_aliases={n_in: 0},
        compiler_params=_cparams(("parallel",)),
        name="mixer_sample",
    )(sinks, x2d, q, k, v, u, cache_k, cache_v, state, *wts, h1_buf)


def _first_max(vals, iota, n):
    m = jnp.max(vals, axis=0, keepdims=True)
    idx = jnp.min(jnp.where(vals == m, iota, n), axis=0, keepdims=True)
    return m, idx


def _router_kernel(h1_ref, gffn_ref, wrt_ref, bias_ref, xp_ref, idx_ref, wts_ref):
    xn = _rms(h1_ref[...], gffn_ref[...])
    w = wrt_ref[...]
    w_hi, x_hi = w.astype(BF16), xn.astype(BF16)
    w_lo, x_lo = (w - w_hi.astype(F32)).astype(BF16), (xn - x_hi.astype(F32)).astype(BF16)
    logits = _nt_dot(w_hi, x_hi) + (_nt_dot(w_hi, x_lo) + _nt_dot(w_lo, x_hi))
    scores = _sigmoid(logits)
    biased = scores + bias_ref[...]
    n_tok = biased.shape[1]
    neg = -jnp.inf

    iota_g = lax.broadcasted_iota(I32, (GROUP_SIZE, n_tok), 0)
    grp_rows = []
    for g in range(N_EXPERT_GROUPS):
        blk = biased[g * GROUP_SIZE:(g + 1) * GROUP_SIZE, :]
        top1, i1 = _first_max(blk, iota_g, GROUP_SIZE)
        top2 = jnp.max(jnp.where(iota_g == i1, neg, blk), axis=0, keepdims=True)
        grp_rows.append(top1 + top2)
    gs = jnp.concatenate(grp_rows, axis=0)

    iota_n = lax.broadcasted_iota(I32, (N_EXPERT_GROUPS, n_tok), 0)
    gsel = jnp.zeros((N_EXPERT_GROUPS, n_tok), jnp.bool_)
    for _ in range(TOPK_GROUPS):
        _, gi = _first_max(gs, iota_n, N_EXPERT_GROUPS)
        hit = iota_n == gi
        gsel = gsel | hit
        gs = jnp.where(hit, neg, gs)
    emask = jnp.concatenate(
        [jnp.broadcast_to(gsel[g:g + 1, :], (GROUP_SIZE, n_tok)) for g in range(N_EXPERT_GROUPS)], axis=0)
    masked = jnp.where(emask, biased, neg)

    iota_e = lax.broadcasted_iota(I32, (N_EXPERTS, n_tok), 0)
    idx_rows, sel_rows = [], []
    for _ in range(TOP_K):
        _, ei = _first_max(masked, iota_e, N_EXPERTS)
        hit = iota_e == ei
        idx_rows.append(ei)
        sel_rows.append(jnp.sum(jnp.where(hit, scores, 0.0), axis=0, keepdims=True))
        masked = jnp.where(hit, neg, masked)
    sel = jnp.concatenate(sel_rows, axis=0)
    idx_ref[...] = jnp.concatenate(idx_rows, axis=0)
    wts_ref[...] = sel / jnp.sum(sel, axis=0, keepdims=True) * ROUTED_SCALE
    _store_chunks(xp_ref, _pack_bf16_pairs(xn))


def _router(h1, group, g_ffn, w_router_t, bias_col):
    blk0 = group * T_GRP // BM_R
    colblk = pl.BlockSpec((TOP_K, BM_R), lambda i: (0, i))
    ws = [g_ffn, w_router_t, bias_col]
    return pl.pallas_call(
        _router_kernel,
        grid=(T_GRP // BM_R,),
        in_specs=[pl.BlockSpec((BM_R, D_MODEL), lambda i: (blk0 + i, 0))] + [_full_spec(w, 1) for w in ws],
        out_specs=[pl.BlockSpec((BM_R * ROW_CHUNKS, LANES), lambda i: (i, 0)), colblk, colblk],
        out_shape=[jax.ShapeDtypeStruct((T_GRP * ROW_CHUNKS, LANES), U32),
                   jax.ShapeDtypeStruct((TOP_K, T_GRP), I32),
                   jax.ShapeDtypeStruct((TOP_K, T_GRP), F32)],
        compiler_params=_cparams(("parallel",)),
        name="router",
    )(h1, *ws)


def _rank_kernel(idx_ref, tri_ref, rank_ref, cnt_ref, carry_ref):
    @pl.when(pl.program_id(0) == 0)
    def _():
        carry_ref[...] = jnp.zeros_like(carry_ref)

    idx = idx_ref[...]
    n_tok = idx.shape[1]
    iota_e = lax.broadcasted_iota(I32, (N_EXPERTS, n_tok), 0)
    member = jnp.zeros((N_EXPERTS, n_tok), F32)
    for k in range(TOP_K):
        member = member + jnp.where(iota_e == idx[k:k + 1, :], 1.0, 0.0)
    before = jnp.dot(member.astype(BF16), tri_ref[...], preferred_element_type=F32) + carry_ref[...]
    rows = [jnp.sum(jnp.where(iota_e == idx[k:k + 1, :], before, 0.0), axis=0, keepdims=True)
            for k in range(TOP_K)]
    rank_ref[...] = jnp.concatenate(rows, axis=0).astype(I32)
    carry_ref[...] = carry_ref[...] + jnp.sum(member, axis=1, keepdims=True)
    cnt_ref[...] = carry_ref[...].astype(I32)


def _rank(idx_t, tri):
    blk = pl.BlockSpec((TOP_K, BT_RANK), lambda i: (0, i))
    return pl.pallas_call(
        _rank_kernel,
        grid=(T_GRP // BT_RANK,),
        in_specs=[blk, _full_spec(tri, 1)],
        out_specs=[blk, pl.BlockSpec((N_EXPERTS, 1), lambda i: (0, 0))],
        out_shape=[jax.ShapeDtypeStruct((TOP_K, T_GRP), I32),
                   jax.ShapeDtypeStruct((N_EXPERTS, 1), I32)],
        scratch_shapes=[pltpu.VMEM((N_EXPERTS, 1), F32)],
        compiler_params=_cparams(("arbitrary",)),
        name="rank",
    )(idx_t, tri)


def _dest_kernel(idx_ref, rank_ref, cnt_ref, dest_ref, chunks_ref, blk_e_ref, n_used_ref, blk_rows_ref):
    counts = cnt_ref[...]
    padded = (counts + (BM_E - 1)) // BM_E * BM_E
    r = lax.broadcasted_iota(I32, (N_EXPERTS, N_EXPERTS), 0)
    c = lax.broadcasted_iota(I32, (N_EXPERTS, N_EXPERTS), 1)
    padded_row = jnp.sum(jnp.where(r == c, padded, 0), axis=0, keepdims=True)
    pad_start = jnp.sum(jnp.where(c < r, padded_row, 0), axis=1, keepdims=True)

    idx = idx_ref[...]
    n_tok = idx.shape[1]
    iota_e = lax.broadcasted_iota(I32, (N_EXPERTS, n_tok), 0)
    rows = [jnp.sum(jnp.where(iota_e == idx[k:k + 1, :], pad_start, 0), axis=0, keepdims=True)
            for k in range(TOP_K)]
    dest = jnp.concatenate(rows, axis=0) + rank_ref[...]
    dest_ref[...] = dest
    for ch in range(n_tok // SC_CHUNK):
        chunks_ref[ch] = dest[:, ch * SC_CHUNK:(ch + 1) * SC_CHUNK]

    @pl.when(pl.program_id(0) == 0)
    def _():
        pad_end_row = jnp.sum(jnp.where(r <= c, padded, 0), axis=0, keepdims=True)
        b0 = lax.broadcasted_iota(I32, (N_BLOCKS_PAD, N_EXPERTS), 0) * BM_E
        be = jnp.minimum(jnp.sum(jnp.where(pad_end_row <= b0, 1, 0), axis=1, keepdims=True), N_EXPERTS - 1)
        blk_e_ref[...] = be
        n_used_ref[...] = pad_end_row[:, N_EXPERTS - 1:N_EXPERTS] // BM_E
        counts_row = jnp.sum(jnp.where(r == c, counts, 0), axis=0, keepdims=True)
        mine = lax.broadcasted_iota(I32, (N_BLOCKS_PAD, N_EXPERTS), 1) == be
        end_valid = jnp.sum(jnp.where(mine, pad_end_row - padded_row + counts_row, 0), axis=1, keepdims=True)
        blk_rows_ref[...] = jnp.clip(end_valid - b0[:, :1], 0, BM_E)


N_BLOCKS_PAD = (N_BLOCKS + 7) // 8 * 8


def _dest(idx_t, rank_t, counts):
    blk = pl.BlockSpec((TOP_K, BT_DEST), lambda i: (0, i))
    one = lambda s: pl.BlockSpec(s, lambda i: (0, 0))
    return pl.pallas_call(
        _dest_kernel,
        grid=(T_GRP // BT_DEST,),
        in_specs=[blk, blk, one((N_EXPERTS, 1))],
        out_specs=[blk, pl.BlockSpec((BT_DEST // SC_CHUNK, TOP_K, SC_CHUNK), lambda i: (i, 0, 0)),
                   one((N_BLOCKS_PAD, 1)), one((1, 1)), one((N_BLOCKS_PAD, 1))],
        out_shape=[jax.ShapeDtypeStruct((TOP_K, T_GRP), I32),
                   jax.ShapeDtypeStruct((T_GRP // SC_CHUNK, TOP_K, SC_CHUNK), I32),
                   jax.ShapeDtypeStruct((N_BLOCKS_PAD, 1), I32),
                   jax.ShapeDtypeStruct((1, 1), I32),
                   jax.ShapeDtypeStruct((N_BLOCKS_PAD, 1), I32)],
        compiler_params=_cparams(("arbitrary",)),
        name="dest",
    )(idx_t, rank_t, counts)


def _sc_mesh():
    return plsc.VectorSubcoreMesh(core_axis_name="c", subcore_axis_name="s")


def _sc_worker_id():
    return lax.axis_index("s") * SC_CORES + lax.axis_index("c")


def _dispatch_body(dest_hbm, xp_hbm, xs_hbm, idx_v, rows_v, sem_in, sem_out):
    n_chunks, _, n_tok = dest_hbm.shape
    per_worker = n_chunks // SC_WORKERS
    chunk0 = _sc_worker_id() * per_worker

    def loads(i):
        chunk = chunk0 + i
        t0 = pl.multiple_of(chunk * n_tok, n_tok)
        return (pltpu.make_async_copy(dest_hbm.at[chunk], idx_v.at[i % 2], sem_in.at[i % 2]),
                pltpu.make_async_copy(xp_hbm.at[pl.ds(t0, n_tok)], rows_v.at[i % 2], sem_in.at[i % 2]))

    def scatters(i):
        return [pltpu.make_async_copy(rows_v.at[i % 2], xs_hbm.at[idx_v.at[i % 2, k]], sem_out.at[i % 2])
                for k in range(TOP_K)]

    for cp in loads(0):
        cp.start()
    for i in range(per_worker):
        for cp in loads(i):
            cp.wait()
        if i >= 1:
            for cp in scatters(i - 1):
                cp.wait()
        if i + 1 < per_worker:
            for cp in loads(i + 1):
                cp.start()
        for cp in scatters(i):
            cp.start()
    for cp in scatters(per_worker - 1):
        cp.wait()


def _dispatch(dest_chunks, xp3):
    return pl.kernel(
        _dispatch_body,
        out_type=jax.ShapeDtypeStruct((N_SLOTS, ROW_CHUNKS, LANES), U32),
        mesh=_sc_mesh(),
        scratch_types=[pltpu.VMEM((2, TOP_K, SC_CHUNK), I32),
                       pltpu.VMEM((2, SC_CHUNK, ROW_CHUNKS, LANES), U32),
                       pltpu.SemaphoreType.DMA((2,)), pltpu.SemaphoreType.DMA((2,))],
        name="dispatch",
    )(dest_chunks, xp3)


def _gather_body(dest_hbm, ys_hbm, yt_hbm, idx_v, rows_v, sem_in, sem_out):
    n_chunks, _, n_tok = dest_hbm.shape
    per_worker = n_chunks // SC_WORKERS
    chunk0 = _sc_worker_id() * per_worker

    @pl.loop(0, per_worker)
    def _(i):
        chunk = chunk0 + i
        t0 = pl.multiple_of(chunk * n_tok, n_tok)
        pltpu.sync_copy(dest_hbm.at[chunk], idx_v)

        def gather(k):
            return pltpu.make_async_copy(ys_hbm.at[idx_v.at[k]], rows_v.at[k % SC_RING], sem_in.at[k % SC_RING])

        def store(k):
            return pltpu.make_async_copy(rows_v.at[k % SC_RING], yt_hbm.at[k, pl.ds(t0, n_tok)],
                                         sem_out.at[k % SC_RING])

        for k in range(SC_RING):
            gather(k).start()
        for k in range(TOP_K):
            gather(k).wait()
            store(k).start()
            if k + SC_RING < TOP_K:
                store(k).wait()
                gather(k + SC_RING).start()
        for k in range(TOP_K - SC_RING, TOP_K):
            store(k).wait()


def _gather(dest_chunks, ys3):
    n_chunks, _, n_tok = dest_chunks.shape
    assert n_chunks % SC_WORKERS == 0
    return pl.kernel(
        _gather_body,
        out_type=jax.ShapeDtypeStruct((TOP_K, n_chunks * n_tok, ROW_CHUNKS, LANES), U32),
        mesh=_sc_mesh(),
        scratch_types=[pltpu.VMEM((TOP_K, n_tok), I32),
                       pltpu.VMEM((SC_RING, n_tok, ROW_CHUNKS, LANES), U32),
                       pltpu.SemaphoreType.DMA((SC_RING,)), pltpu.SemaphoreType.DMA((SC_RING,))],
        name="gather",
    )(dest_chunks, ys3)


def _experts_kernel(blk_e_ref, n_used_ref, blk_rows_ref, xs_ref, wgu_ref, wd_ref, ys_ref):
    del blk_e_ref
    b = pl.program_id(0)

    def swiglu_rows(n_rows):
        chunks = _load_chunks(xs_ref, n_rows)
        x_lo = jnp.concatenate([_unpack_lo(p) for p in chunks], axis=-1).astype(BF16)
        x_hi = jnp.concatenate([_unpack_hi(p) for p in chunks], axis=-1).astype(BF16)
        gu = (jnp.dot(x_lo, wgu_ref[0, :HALF, :], preferred_element_type=F32)
              + jnp.dot(x_hi, wgu_ref[0, HALF:, :], preferred_element_type=F32))
        gate, up = gu[:, :D_EXPERT], gu[:, D_EXPERT:]
        hmid = (gate * _sigmoid(gate) * up).astype(BF16)
        _store_chunks(ys_ref, _pack_bf16_pairs(jnp.dot(hmid, wd_ref[0], preferred_element_type=F32)))

    @pl.when(b < n_used_ref[0])
    def _():
        valid = blk_rows_ref[b]
        for n_rows in range(E_STRIP, BM_E + 1, E_STRIP):
            @pl.when((valid > n_rows - E_STRIP) & (valid <= n_rows))
            def _(n_rows=n_rows):
                swiglu_rows(n_rows)


def _experts(blk_e, n_used, blk_rows, xs, wgu_bf, wd_bf):
    def blk(b, be, nu, nr):
        return jnp.minimum(b, nu[0] - 1)

    def by_expert(shape):
        return pl.BlockSpec((1,) + shape, lambda b, be, nu, nr: (be[blk(b, be, nu, nr)], 0, 0))

    tile = pl.BlockSpec((BM_E * ROW_CHUNKS, LANES), lambda b, be, nu, nr: (blk(b, be, nu, nr), 0))
    grid_spec = pltpu.PrefetchScalarGridSpec(
        num_scalar_prefetch=3,
        grid=(N_BLOCKS,),
        in_specs=[tile, by_expert((D_MODEL, 2 * D_EXPERT)), by_expert((D_EXPERT, D_MODEL))],
        out_specs=tile,
    )
    return pl.pallas_call(
        _experts_kernel,
        grid_spec=grid_spec,
        out_shape=jax.ShapeDtypeStruct((N_SLOTS * ROW_CHUNKS, LANES), U32),
        compiler_params=_cparams(("arbitrary",)),
        name="experts",
    )(blk_e, n_used, blk_rows, xs, wgu_bf, wd_bf)


def _combine_kernel(yt_ref, wts_ref, h1_ref, p_ref, gffn_ref, wsgu_ref, wsd_ref, gple_ref, wpg_ref, wpp_ref,
                    gfin_ref, *y_refs):
    y_ref = y_refs[-1]
    h1 = h1_ref[...]
    gu = jnp.dot(_rms(h1, gffn_ref[...]).astype(BF16), wsgu_ref[...], preferred_element_type=F32)
    sgate, sup = gu[:, :D_SHARED], gu[:, D_SHARED:]
    hsh = h1 + jnp.dot((sgate * _sigmoid(sgate) * sup).astype(BF16), wsd_ref[...], preferred_element_type=F32)
    wts = jnp.transpose(wts_ref[...])
    lo = [jnp.zeros((BT_COMB, LANES), F32) for _ in range(ROW_CHUNKS)]
    hi = [jnp.zeros((BT_COMB, LANES), F32) for _ in range(ROW_CHUNKS)]
    for k in range(TOP_K):
        w = wts[:, k:k + 1]
        for c, p in enumerate(_load_chunks(yt_ref, BT_COMB, lead=(k,))):
            lo[c] = lo[c] + w * _unpack_lo(p)
            hi[c] = hi[c] + w * _unpack_hi(p)
    h2 = hsh + jnp.concatenate(lo + hi, axis=-1)
    gate = _sigmoid(jnp.dot(_rms(h2, gple_ref[...]).astype(BF16), wpg_ref[...], preferred_element_type=F32))
    proj = jnp.dot(p_ref[...].astype(BF16), wpp_ref[...], preferred_element_type=F32)
    y_ref[...] = _rms(h2 + proj * gate, gfin_ref[...])


def _combine(yt, yt_row0, wts_t, wts_row0, h1, tok_row0, n_rows, p2d, p_row0, ws, y_prev, out_rows, out_row0):
    assert all(r % BT_COMB == 0 for r in (yt_row0, wts_row0, tok_row0, n_rows, p_row0, out_row0))
    g0, w0, t0, p0, o0 = (r // BT_COMB for r in (yt_row0, wts_row0, tok_row0, p_row0, out_row0))
    in_specs = [pl.BlockSpec((TOP_K, BT_COMB * ROW_CHUNKS, LANES), lambda i: (0, g0 + i, 0)),
                pl.BlockSpec((TOP_K, BT_COMB), lambda i: (0, w0 + i)),
                pl.BlockSpec((BT_COMB, D_MODEL), lambda i: (t0 + i, 0)),
                pl.BlockSpec((BT_COMB, PLE_DIM), lambda i: (p0 + i, 0))] + [_full_spec(w, 1) for w in ws]
    args = [yt, wts_t, h1, p2d, *ws]
    aliases = {}
    if y_prev is not None:
        in_specs.append(pl.BlockSpec(memory_space=pl.ANY))
        aliases = {len(args): 0}
        args.append(y_prev)
    return pl.pallas_call(
        _combine_kernel,
        grid=(n_rows // BT_COMB,),
        in_specs=in_specs,
        out_specs=pl.BlockSpec((BT_COMB, D_MODEL), lambda i: (o0 + i, 0)),
        out_shape=jax.ShapeDtypeStruct((out_rows, D_MODEL), F32),
        input_output_aliases=aliases,
        compiler_params=_cparams(("parallel",)),
        name="combine",
    )(*args)


def kernel(x_prompt, x_sample, cache_k, cache_v, state_pool, p_prompt, p_sample, g_mix, w_in, attn_sinks,
           w_pool, pool_scale, g_att_out, g_pool_out, w_out, g_ffn, w_router, router_bias, w_exp_gate,
           w_exp_up, w_exp_down, w_sh_gate, w_sh_up, w_sh_down, g_ple, w_ple_gate, w_ple_proj, g_final):
    row = lambda a: a.reshape(1, -1)
    xp2d = x_prompt.reshape(T_P, D_MODEL)
    xs2d = x_sample.reshape(T_S, D_MODEL)
    w_in_bf = w_in[0].astype(BF16)
    mixer_wts = [w_pool[0].astype(BF16), row(pool_scale[0]), row(g_att_out[0]), row(g_pool_out[0]),
                 w_out[0].astype(BF16)]

    tab_p = _rope_tables(np.arange(SEQ))
    tab_s = _rope_tables(PAST_LEN + np.arange(DEC_SEQ), reps=BM_IN // DEC_SEQ)

    q_p, k_p, v_p, u_p = _inproj(xp2d, row(g_mix[0]), w_in_bf, tab_p, BF16)
    q_s, k_s, v_s, u_s = _inproj(xs2d, row(g_mix[0]), w_in_bf, tab_s, F32)

    h1, wgu_bf, wd_bf = _mixer_prompt(attn_sinks[0], xp2d, q_p, k_p, v_p, u_p, mixer_wts,
                                      (w_exp_gate[0], w_exp_up[0], w_exp_down[0]))
    h1, k_sample, v_sample, pool_sample = _mixer_sample(
        attn_sinks[0], xs2d, q_s, k_s, v_s, u_s,
        cache_k[0].reshape(DEC_BATCH, WINDOW, KV_WIDTH), cache_v[0].reshape(DEC_BATCH, WINDOW, KV_WIDTH),
        state_pool, mixer_wts, h1)

    g_ffn_row = row(g_ffn[0])
    router_wts = (g_ffn_row, w_router[0].T, router_bias[0].reshape(N_EXPERTS, 1))
    tri = (lax.broadcasted_iota(I32, (BT_RANK, BT_RANK), 0)
           < lax.broadcasted_iota(I32, (BT_RANK, BT_RANK), 1)).astype(BF16)

    groups = []
    for g in range(N_GROUPS):
        xp, idx_t, wts_t = _router(h1, g, *router_wts)
        rank_t, counts = _rank(idx_t, tri)
        _, dest_chunks, *plan = _dest(idx_t, rank_t, counts)
        xs = _dispatch(dest_chunks, xp.reshape(T_GRP, ROW_CHUNKS, LANES))
        groups.append((wts_t, dest_chunks, xs, [a.reshape(-1) for a in plan]))

    ple_wts = [g_ffn_row, jnp.concatenate([w_sh_gate[0], w_sh_up[0]], axis=1).astype(BF16),
               w_sh_down[0].astype(BF16),
               row(g_ple[0]), w_ple_gate[0].astype(BF16), w_ple_proj[0].astype(BF16), row(g_final)]
    pp2d = p_prompt[0].reshape(T_P, PLE_DIM)
    ps2d = p_sample[0].reshape(T_S, PLE_DIM)
    y_p = y_s = None
    for g, (wts_t, dest_chunks, xs, plan) in enumerate(groups):
        lo = g * T_GRP
        ys = _experts(*plan, xs.reshape(N_SLOTS * ROW_CHUNKS, LANES), wgu_bf, wd_bf)
        ys3 = ys.reshape(N_SLOTS, ROW_CHUNKS, LANES)
        for s in range(N_SUB):
            a, b = lo + SUB_BOUNDS[s], lo + SUB_BOUNDS[s + 1]
            yt = _gather(dest_chunks[SUB_BOUNDS[s] // SC_CHUNK:SUB_BOUNDS[s + 1] // SC_CHUNK], ys3)
            yt = yt.reshape(TOP_K, (b - a) * ROW_CHUNKS, LANES)
            if a < T_P:
                n = min(b, T_P) - a
                y_p = _combine(yt, 0, wts_t, a - lo, h1, a, n, pp2d, a, ple_wts, y_p, T_P, a)
            if b > T_P:
                s0 = max(a, T_P)
                y_s = _combine(yt, s0 - a, wts_t, s0 - lo, h1, s0, b - s0, ps2d, s0 - T_P, ple_wts, y_s, T_S,
                               s0 - T_P)

    kv5 = lambda a, b: a.reshape(1, b, WINDOW, N_KV_HEADS, HEAD_DIM)
    k_prompt = kv5(k_p.reshape(BATCH, SEQ, KV_WIDTH)[:, SEQ - WINDOW:], BATCH)
    v_prompt = kv5(v_p.reshape(BATCH, SEQ, KV_WIDTH)[:, SEQ - WINDOW:], BATCH)
    pool_prompt = u_p.reshape(BATCH, SEQ, POOL_WIDTH)[:, SEQ - POOL_STATE:][None]
    return (y_p.reshape(BATCH, SEQ, D_MODEL), y_s.reshape(DEC_BATCH, DEC_SEQ, D_MODEL),
            k_prompt, v_prompt, pool_prompt,
            kv5(k_sample, DEC_BATCH), kv5(v_sample, DEC_BATCH), pool_sample)
```

```python
import functools

import numpy as np
import jax
import jax.numpy as jnp
from jax import lax
from jax.experimental import pallas as pl
from jax.experimental.pallas import tpu as pltpu
from jax.experimental.pallas import tpu_sc as plsc

F32 = jnp.float32
BF16 = jnp.bfloat16
U32 = jnp.uint32
I32 = jnp.int32

D_MODEL = 1024
BATCH = 8
SEQ = 2048
DEC_BATCH = 128
DEC_SEQ = 8
PAST_LEN = 16384
N_Q_HEADS = 8
N_KV_HEADS = 2
HEAD_DIM = 64
GQA_GROUP = N_Q_HEADS // N_KV_HEADS
ATT_WIDTH = N_Q_HEADS * HEAD_DIM
KV_WIDTH = N_KV_HEADS * HEAD_DIM
WINDOW = 128
ROPE_THETA = 500000.0
ROT_DIM = HEAD_DIM // 4
POOL_WINDOWS = (2, 4, 8, 16)
POOL_GROUPS = 4
POOL_WIDTH = D_MODEL - ATT_WIDTH
POOL_GROUP_DIM = POOL_WIDTH // POOL_GROUPS
POOL_STATE = 15
IN_WIDTH = ATT_WIDTH + 2 * KV_WIDTH + POOL_WIDTH
N_EXPERTS = 64
TOP_K = 8
N_EXPERT_GROUPS = 8
GROUP_SIZE = N_EXPERTS // N_EXPERT_GROUPS
TOPK_GROUPS = 4
D_EXPERT = 256
D_SHARED = 256
ROUTED_SCALE = 2.5
PLE_DIM = 256
EPS = 1e-6

T_P = BATCH * SEQ
T_S = DEC_BATCH * DEC_SEQ
T_ALL = T_P + T_S
HALF = D_MODEL // 2
LANES = 128
VMEM_LIMIT = 48 * 1024 * 1024

BM_IN = 1024
BQ = 2 * WINDOW
EXPERTS_PER_MIX_STEP = N_EXPERTS * BQ // T_P
assert EXPERTS_PER_MIX_STEP * T_P == N_EXPERTS * BQ
SB = 16
BM_R = 512
BT_RANK = 512
BT_DEST = 2176
BT_COMB = 512
N_GROUPS = 2
T_GRP = T_ALL // N_GROUPS
assert T_GRP * N_GROUPS == T_ALL
BM_E = 1280
E_STRIP = 128
assert BM_E % E_STRIP == 0
N_ASSIGN = T_GRP * TOP_K
N_BLOCKS = -(-N_ASSIGN // BM_E) + N_EXPERTS
N_SLOTS = N_BLOCKS * BM_E

ROW_CHUNKS = HALF // LANES
SC_CORES = 2
SC_SUBCORES = 16
SC_WORKERS = SC_CORES * SC_SUBCORES
SC_CHUNK = 16
SC_RING = 4
SC_STEP = SC_WORKERS * SC_CHUNK
assert T_GRP % SC_STEP == 0
N_SUB = 2
SUB_BOUNDS = [T_GRP // SC_STEP * s // N_SUB * SC_STEP for s in range(N_SUB + 1)]


def _load_chunks(ref, n_rows, lead=()):
    return [ref[lead + (pl.ds(c, n_rows, stride=ROW_CHUNKS), slice(None))] for c in range(ROW_CHUNKS)]


def _store_chunks(ref, packed):
    n_rows = packed.shape[0]
    for c in range(ROW_CHUNKS):
        ref[pl.ds(c, n_rows, stride=ROW_CHUNKS), :] = packed[:, c * LANES:(c + 1) * LANES]


def _cparams(sem):
    return pltpu.CompilerParams(dimension_semantics=sem, vmem_limit_bytes=VMEM_LIMIT)


def _rms(x, g):
    return x * lax.rsqrt(jnp.mean(x * x, axis=-1, keepdims=True) + EPS) * g


def _sigmoid(x):
    return 1.0 / (1.0 + jnp.exp(-x))


def _pack_bf16_pairs(x):
    h = x.shape[-1] // 2
    return pltpu.pack_elementwise([x[:, :h], x[:, h:]], packed_dtype=BF16)


def _unpack_lo(p):
    return pltpu.bitcast(p << 16, F32)


def _unpack_hi(p):
    return pltpu.bitcast(p & jnp.uint32(0xFFFF0000), F32)


def _inproj_kernel(x_ref, g_ref, w_ref, c_ref, s1_ref, s2_ref, q_ref, k_ref, v_ref, u_ref):
    xn = _rms(x_ref[...], g_ref[...]).astype(BF16)
    z = jnp.dot(xn, w_ref[...], preferred_element_type=F32)
    c, s1, s2 = c_ref[...], s1_ref[...], s2_ref[...]

    def rope(t):
        return t * c + pltpu.roll(t, LANES - ROT_DIM // 2, 1) * s1 + pltpu.roll(t, ROT_DIM // 2, 1) * s2

    for i in range(ATT_WIDTH // LANES):
        sl = slice(i * LANES, (i + 1) * LANES)
        q_ref[:, sl] = (rope(z[:, sl]) * (HEAD_DIM ** -0.5)).astype(q_ref.dtype)
    k_ref[...] = rope(z[:, ATT_WIDTH:ATT_WIDTH + KV_WIDTH])
    v_ref[...] = z[:, ATT_WIDTH + KV_WIDTH:ATT_WIDTH + 2 * KV_WIDTH]
    u_ref[...] = z[:, ATT_WIDTH + 2 * KV_WIDTH:]


def _rope_tables(pos, reps=1):
    f32 = np.float32
    half = ROT_DIM // 2
    inv = np.power(f32(ROPE_THETA), -np.arange(half, dtype=f32) * f32(2.0) / f32(ROT_DIM)).astype(f32)
    ang = np.asarray(pos, f32)[:, None] * inv[None, :]
    cos, sin = np.cos(ang).astype(f32), np.sin(ang).astype(f32)
    n = len(pos)
    ones = np.ones((n, HEAD_DIM - ROT_DIM), f32)
    zeros = np.zeros((n, HEAD_DIM - ROT_DIM), f32)
    zh = np.zeros((n, half), f32)
    c = np.concatenate([cos, cos, ones], axis=1)
    s1 = np.concatenate([-sin, zh, zeros], axis=1)
    s2 = np.concatenate([zh, sin, zeros], axis=1)
    tile = lambda a: np.tile(a, (reps, LANES // HEAD_DIM))
    return tile(c), tile(s1), tile(s2)


def _inproj(x2d, g_mix, w_in_bf, tables, q_dtype):
    rows = x2d.shape[0]
    n_tab = tables[0].shape[0] // BM_IN
    row_spec = lambda w: pl.BlockSpec((BM_IN, w), lambda i: (i, 0))
    tab_spec = pl.BlockSpec((BM_IN, LANES), lambda i: (i % n_tab, 0))
    full = lambda a: pl.BlockSpec(a.shape, lambda i: (0,) * a.ndim)
    return pl.pallas_call(
        _inproj_kernel,
        grid=(rows // BM_IN,),
        in_specs=[row_spec(D_MODEL), full(g_mix), full(w_in_bf), tab_spec, tab_spec, tab_spec],
        out_specs=[row_spec(ATT_WIDTH), row_spec(KV_WIDTH), row_spec(KV_WIDTH), row_spec(POOL_WIDTH)],
        out_shape=[jax.ShapeDtypeStruct((rows, ATT_WIDTH), q_dtype),
                   jax.ShapeDtypeStruct((rows, KV_WIDTH), F32),
                   jax.ShapeDtypeStruct((rows, KV_WIDTH), F32),
                   jax.ShapeDtypeStruct((rows, POOL_WIDTH), F32)],
        compiler_params=_cparams(("parallel",)),
        name="inproj",
    )(x2d, g_mix, w_in_bf, *tables)


def _sink_column(sinks_ref, kv_head, rows_per_head):
    n = GQA_GROUP * rows_per_head
    grp = lax.broadcasted_iota(I32, (n, 1), 0) // rows_per_head
    col = jnp.full((n, 1), sinks_ref[kv_head * GQA_GROUP], F32)
    for g in range(1, GQA_GROUP):
        col = jnp.where(grp == g, sinks_ref[kv_head * GQA_GROUP + g], col)
    return col


def _band_mask(n_rows, rows_per_head, n_keys):
    i = lax.broadcasted_iota(I32, (n_rows, n_keys), 0) % rows_per_head
    c = lax.broadcasted_iota(I32, (n_rows, n_keys), 1)
    return (c >= i) & (c <= i + WINDOW), c


def _stack_heads(q, kv_head):
    return jnp.concatenate(
        [q[:, (kv_head * GQA_GROUP + g) * HEAD_DIM:(kv_head * GQA_GROUP + g + 1) * HEAD_DIM]
         for g in range(GQA_GROUP)], axis=0)


def _nt_dot(a, b):
    return lax.dot_general(a, b, (((1,), (1,)), ((), ())), preferred_element_type=F32)


POOL_HALO = 16
POOL_HEAD = 8
assert all(w == 2 << g for g, w in enumerate(POOL_WINDOWS)) and POOL_WINDOWS[-1] - 1 <= POOL_HALO


def _window_sums(ext_ref, n):
    lo, hi = POOL_HEAD, POOL_HEAD + POOL_HALO + n
    for p in range(POOL_GROUPS):
        lanes = slice(p * POOL_GROUP_DIM, POOL_WIDTH)
        ext_ref[lo:hi, lanes] = ext_ref[lo:hi, lanes] + ext_ref[lo - (1 << p):hi - (1 << p), lanes]


def _pool_out(d, wpool_ref, pscale_ref, gpool_ref):
    parts = [jnp.dot(d[:, g * POOL_GROUP_DIM:(g + 1) * POOL_GROUP_DIM].astype(BF16), wpool_ref[g],
                     preferred_element_type=F32) for g in range(POOL_GROUPS)]
    return _rms(jnp.concatenate(parts, axis=-1) * pscale_ref[...], gpool_ref[...])


def _mixer_tail(o_att, pooled, h, gatt_ref, wout_ref):
    mixed = jnp.concatenate([_rms(o_att, gatt_ref[...]), pooled], axis=-1)
    return h + jnp.dot(mixed.astype(BF16), wout_ref[...], preferred_element_type=F32)


def _mixer_prompt_kernel(sinks_ref, h_ref, q_ref, kc_ref, kp_ref, vc_ref, vp_ref, uc_ref, up_ref,
                         wpool_ref, pscale_ref, gatt_ref, gpool_ref, wout_ref, weg_ref, weu_ref, wed_ref,
                         h1_ref, wgu_bf_ref, wd_bf_ref, uext_ref):
    j = pl.program_id(1)
    wgu_bf_ref[:, :, :D_EXPERT] = weg_ref[...].astype(BF16)
    wgu_bf_ref[:, :, D_EXPERT:] = weu_ref[...].astype(BF16)
    wd_bf_ref[...] = wed_ref[...].astype(BF16)

    u = uc_ref[...]
    base = POOL_HEAD + POOL_HALO
    uext_ref[0:POOL_HEAD, :] = jnp.zeros((POOL_HEAD, POOL_WIDTH), F32)
    uext_ref[POOL_HEAD:base, :] = jnp.where(j > 0, up_ref[...], 0.0)
    uext_ref[base:base + BQ, :] = u
    _window_sums(uext_ref, BQ)
    pos = j * BQ + lax.broadcasted_iota(I32, (BQ, 1), 0)
    parts = []
    for g, w in enumerate(POOL_WINDOWS):
        sl = slice(g * POOL_GROUP_DIM, (g + 1) * POOL_GROUP_DIM)
        parts.append(uext_ref[base:base + BQ, sl] / jnp.minimum(pos + 1, w).astype(F32) - u[:, sl])
    pooled = _pool_out(jnp.concatenate(parts, axis=-1), wpool_ref, pscale_ref, gpool_ref)

    q = q_ref[...]
    k_all = jnp.concatenate([kp_ref[...], kc_ref[...]], axis=0).astype(BF16)
    v_all = jnp.concatenate([vp_ref[...], vc_ref[...]], axis=0).astype(BF16)
    ones = jnp.ones((WINDOW + BQ, HEAD_DIM), BF16)
    v_ones = [jnp.concatenate([v_all[:, hk * HEAD_DIM:(hk + 1) * HEAD_DIM], ones], axis=1)
              for hk in range(N_KV_HEADS)]
    n_q, n_k = GQA_GROUP * WINDOW, 2 * WINDOW
    qi = lax.broadcasted_iota(I32, (n_k, n_q), 1) % WINDOW
    kc = lax.broadcasted_iota(I32, (n_k, n_q), 0)
    band = (kc >= qi) & (kc <= qi + WINDOW)
    lane_head = lax.broadcasted_iota(I32, (1, n_q), 1) // WINDOW
    sinks = []
    for hk in range(N_KV_HEADS):
        row = jnp.full((1, n_q), sinks_ref[hk * GQA_GROUP], F32)
        for g in range(1, GQA_GROUP):
            row = jnp.where(lane_head == g, sinks_ref[hk * GQA_GROUP + g], row)
        sinks.append(row)
    bands = []
    for b in range(BQ // WINDOW):
        rows = slice(b * WINDOW, (b + 1) * WINDOW)
        keys = slice(b * WINDOW, (b + 2) * WINDOW)
        mask = band & ((kc >= WINDOW) | (j > 0)) if b == 0 else band
        heads = []
        for hk in range(N_KV_HEADS):
            sl = slice(hk * HEAD_DIM, (hk + 1) * HEAD_DIM)
            s = jnp.where(mask, _nt_dot(k_all[keys, sl], _stack_heads(q[rows], hk)), -jnp.inf)
            m = jnp.maximum(jnp.max(s, axis=0, keepdims=True), sinks[hk])
            e = jnp.exp(s - m).astype(BF16)
            ov = lax.dot_general(v_ones[hk][keys], e, (((0,), (0,)), ((), ())), preferred_element_type=F32)
            den = ov[HEAD_DIM:HEAD_DIM + 1, :] + jnp.exp(sinks[hk] - m)
            o = jnp.transpose(ov / den)[:, :HEAD_DIM]
            heads += [o[g * WINDOW:(g + 1) * WINDOW] for g in range(GQA_GROUP)]
        bands.append(jnp.concatenate(heads, axis=-1))
    o_att = jnp.concatenate(bands, axis=0)
    h1_ref[...] = _mixer_tail(o_att, pooled, h_ref[...], gatt_ref, wout_ref)


def _mixer_sample_kernel(sinks_ref, h_ref, q_ref, kn_ref, vn_ref, u_ref, ck_ref, cv_ref, st_ref,
                         wpool_ref, pscale_ref, gatt_ref, gpool_ref, wout_ref, h1_in_ref,
                         h1_ref, ko_ref, vo_ref, po_ref, uext_ref):
    del h1_in_ref
    n_q = GQA_GROUP * DEC_SEQ
    n_keys = 2 * WINDOW
    band, col = _band_mask(n_q, DEC_SEQ, n_keys)
    mask = (band & (col < WINDOW + DEC_SEQ))[None]
    q3, kn3, vn3, u3 = q_ref[...], kn_ref[...], vn_ref[...], u_ref[...]
    ck, cv = ck_ref[...], cv_ref[...]
    ko_ref[:, 0:WINDOW - DEC_SEQ, :] = ck[:, DEC_SEQ:, :]
    ko_ref[:, WINDOW - DEC_SEQ:WINDOW, :] = kn3
    vo_ref[:, 0:WINDOW - DEC_SEQ, :] = cv[:, DEC_SEQ:, :]
    vo_ref[:, WINDOW - DEC_SEQ:WINDOW, :] = vn3
    pad = jnp.zeros((SB, WINDOW - DEC_SEQ, KV_WIDTH), F32)
    k_all = jnp.concatenate([ck, kn3, pad], axis=1).astype(BF16)
    v_all = jnp.concatenate([cv, vn3, pad], axis=1).astype(BF16)
    heads = []
    for hk in range(N_KV_HEADS):
        sl = slice(hk * HEAD_DIM, (hk + 1) * HEAD_DIM)
        qs = jnp.concatenate(
            [q3[:, :, (hk * GQA_GROUP + g) * HEAD_DIM:(hk * GQA_GROUP + g + 1) * HEAD_DIM]
             for g in range(GQA_GROUP)], axis=1).astype(BF16)
        sink = _sink_column(sinks_ref, hk, DEC_SEQ)[None]
        s = jnp.einsum("bqd,bkd->bqk", qs, k_all[:, :, sl], preferred_element_type=F32)
        s = jnp.where(mask, s, -jnp.inf)
        m = jnp.maximum(jnp.max(s, axis=-1, keepdims=True), sink)
        e = jnp.exp(s - m)
        den = jnp.sum(e, axis=-1, keepdims=True) + jnp.exp(sink - m)
        o = jnp.einsum("bqk,bkd->bqd", e.astype(BF16), v_all[:, :, sl], preferred_element_type=F32) / den
        heads += [o[:, g * DEC_SEQ:(g + 1) * DEC_SEQ, :] for g in range(GQA_GROUP)]
    o_att = jnp.concatenate(heads, axis=-1).reshape(SB * DEC_SEQ, ATT_WIDTH)

    uext_ref[:, 1:16, :] = st_ref[...]
    uext_ref[:, 16:16 + DEC_SEQ, :] = u3
    parts = []
    for g, w in enumerate(POOL_WINDOWS):
        sl = slice(g * POOL_GROUP_DIM, (g + 1) * POOL_GROUP_DIM)
        acc = u3[:, :, sl]
        for back in range(1, w):
            acc = acc + uext_ref[:, 16 - back:16 - back + DEC_SEQ, sl]
        parts.append(acc / float(w) - u3[:, :, sl])
    d = jnp.concatenate(parts, axis=-1).reshape(SB * DEC_SEQ, POOL_WIDTH)
    po_ref[...] = uext_ref[:, 16 + DEC_SEQ - POOL_STATE:16 + DEC_SEQ, :]
    pooled = _pool_out(d, wpool_ref, pscale_ref, gpool_ref)
    h1_ref[...] = _mixer_tail(o_att, pooled, h_ref[...], gatt_ref, wout_ref)


def _full_spec(a, n_grid):
    nd = a.ndim
    return pl.BlockSpec(a.shape, lambda *_: (0,) * nd)


def _mixer_prompt(sinks, x2d, q, k, v, u, wts, w_exp):
    nb = SEQ // BQ
    row = lambda w: pl.BlockSpec((BQ, w), lambda b, j: (b * nb + j, 0))
    prev = lambda w: pl.BlockSpec(
        (WINDOW, w), lambda b, j: (jnp.maximum((b * nb + j) * (BQ // WINDOW) - 1, 0), 0))
    uprev = pl.BlockSpec((POOL_HALO, POOL_WIDTH),
                         lambda b, j: (jnp.maximum((b * nb + j) * (BQ // POOL_HALO) - 1, 0), 0))
    smem = pl.BlockSpec(memory_space=pltpu.SMEM)
    per_step = lambda r, c: pl.BlockSpec((EXPERTS_PER_MIX_STEP, r, c), lambda b, j: (b * nb + j, 0, 0))
    return pl.pallas_call(
        _mixer_prompt_kernel,
        grid=(BATCH, nb),
        in_specs=[smem, row(D_MODEL), row(ATT_WIDTH), row(KV_WIDTH), prev(KV_WIDTH), row(KV_WIDTH),
                  prev(KV_WIDTH), row(POOL_WIDTH), uprev] + [_full_spec(w, 2) for w in wts]
                 + [per_step(D_MODEL, D_EXPERT), per_step(D_MODEL, D_EXPERT), per_step(D_EXPERT, D_MODEL)],
        out_specs=[row(D_MODEL), per_step(D_MODEL, 2 * D_EXPERT), per_step(D_EXPERT, D_MODEL)],
        out_shape=[jax.ShapeDtypeStruct((T_ALL, D_MODEL), F32),
                   jax.ShapeDtypeStruct((N_EXPERTS, D_MODEL, 2 * D_EXPERT), BF16),
                   jax.ShapeDtypeStruct((N_EXPERTS, D_EXPERT, D_MODEL), BF16)],
        scratch_shapes=[pltpu.VMEM((POOL_HEAD + POOL_HALO + BQ, POOL_WIDTH), F32)],
        compiler_params=_cparams(("parallel", "parallel")),
        name="mixer_prompt",
    )(sinks, x2d, q, k, k, v, v, u, u, *wts, *w_exp)


def _mixer_sample(sinks, x2d, q, k, v, u, cache_k, cache_v, state, wts, h1_buf):
    rows = SB * DEC_SEQ
    row = lambda w: pl.BlockSpec((rows, w), lambda i: (i, 0))
    bat = lambda a: pl.BlockSpec((SB,) + a.shape[1:], lambda i: (i, 0, 0))
    smem = pl.BlockSpec(memory_space=pltpu.SMEM)
    h1_blocks_before = T_P // rows
    n_in = 9 + len(wts)
    q, k, v, u = (a.reshape(DEC_BATCH, DEC_SEQ, a.shape[-1]) for a in (q, k, v, u))
    st = pl.BlockSpec((None, SB) + state.shape[2:], lambda i: (0, i, 0, 0))
    return pl.pallas_call(
        _mixer_sample_kernel,
        grid=(DEC_BATCH // SB,),
        in_specs=[smem, row(D_MODEL), bat(q), bat(k), bat(v), bat(u),
                  bat(cache_k), bat(cache_v), st] + [_full_spec(w, 1) for w in wts]
                 + [pl.BlockSpec(memory_space=pl.ANY)],
        out_specs=[pl.BlockSpec((rows, D_MODEL), lambda i: (h1_blocks_before + i, 0)),
                   bat(cache_k), bat(cache_v), st],
        out_shape=[jax.ShapeDtypeStruct((T_ALL, D_MODEL), F32),
                   jax.ShapeDtypeStruct(cache_k.shape, F32),
                   jax.ShapeDtypeStruct(cache_v.shape, F32),
                   jax.ShapeDtypeStruct(state.shape, F32)],
        scratch_shapes=[pltpu.VMEM((SB, 16 + DEC_SEQ, POOL_WIDTH), F32)],
        input_output_aliases={n_in: 0},
        compiler_params=_cparams(("parallel",)),
        name="mixer_sample",
    )(sinks, x2d, q, k, v, u, cache_k, cache_v, state, *wts, h1_buf)


def _first_max(vals, iota, n):
    m = jnp.max(vals, axis=0, keepdims=True)
    idx = jnp.min(jnp.where(vals == m, iota, n), axis=0, keepdims=True)
    return m, idx


def _router_kernel(h1_ref, gffn_ref, wrt_ref, bias_ref, xp_ref, idx_ref, wts_ref):
    xn = _rms(h1_ref[...], gffn_ref[...])
    w = wrt_ref[...]
    w_hi, x_hi = w.astype(BF16), xn.astype(BF16)
    w_lo, x_lo = (w - w_hi.astype(F32)).astype(BF16), (xn - x_hi.astype(F32)).astype(BF16)
    logits = _nt_dot(w_hi, x_hi) + (_nt_dot(w_hi, x_lo) + _nt_dot(w_lo, x_hi))
    scores = _sigmoid(logits)
    biased = scores + bias_ref[...]
    n_tok = biased.shape[1]
    neg = -jnp.inf

    iota_g = lax.broadcasted_iota(I32, (GROUP_SIZE, n_tok), 0)
    grp_rows = []
    for g in range(N_EXPERT_GROUPS):
        blk = biased[g * GROUP_SIZE:(g + 1) * GROUP_SIZE, :]
        top1, i1 = _first_max(blk, iota_g, GROUP_SIZE)
        top2 = jnp.max(jnp.where(iota_g == i1, neg, blk), axis=0, keepdims=True)
        grp_rows.append(top1 + top2)
    gs = jnp.concatenate(grp_rows, axis=0)

    iota_n = lax.broadcasted_iota(I32, (N_EXPERT_GROUPS, n_tok), 0)
    gsel = jnp.zeros((N_EXPERT_GROUPS, n_tok), jnp.bool_)
    for _ in range(TOPK_GROUPS):
        _, gi = _first_max(gs, iota_n, N_EXPERT_GROUPS)
        hit = iota_n == gi
        gsel = gsel | hit
        gs = jnp.where(hit, neg, gs)
    emask = jnp.concatenate(
        [jnp.broadcast_to(gsel[g:g + 1, :], (GROUP_SIZE, n_tok)) for g in range(N_EXPERT_GROUPS)], axis=0)
    masked = jnp.where(emask, biased, neg)

    iota_e = lax.broadcasted_iota(I32, (N_EXPERTS, n_tok), 0)
    idx_rows, sel_rows = [], []
    for _ in range(TOP_K):
        _, ei = _first_max(masked, iota_e, N_EXPERTS)
        hit = iota_e == ei
        idx_rows.append(ei)
        sel_rows.append(jnp.sum(jnp.where(hit, scores, 0.0), axis=0, keepdims=True))
        masked = jnp.where(hit, neg, masked)
    sel = jnp.concatenate(sel_rows, axis=0)
    idx_ref[...] = jnp.concatenate(idx_rows, axis=0)
    wts_ref[...] = sel / jnp.sum(sel, axis=0, keepdims=True) * ROUTED_SCALE
    _store_chunks(xp_ref, _pack_bf16_pairs(xn))


def _router(h1, group, g_ffn, w_router_t, bias_col):
    blk0 = group * T_GRP // BM_R
    colblk = pl.BlockSpec((TOP_K, BM_R), lambda i: (0, i))
    ws = [g_ffn, w_router_t, bias_col]
    return pl.pallas_call(
        _router_kernel,
        grid=(T_GRP // BM_R,),
        in_specs=[pl.BlockSpec((BM_R, D_MODEL), lambda i: (blk0 + i, 0))] + [_full_spec(w, 1) for w in ws],
        out_specs=[pl.BlockSpec((BM_R * ROW_CHUNKS, LANES), lambda i: (i, 0)), colblk, colblk],
        out_shape=[jax.ShapeDtypeStruct((T_GRP * ROW_CHUNKS, LANES), U32),
                   jax.ShapeDtypeStruct((TOP_K, T_GRP), I32),
                   jax.ShapeDtypeStruct((TOP_K, T_GRP), F32)],
        compiler_params=_cparams(("parallel",)),
        name="router",
    )(h1, *ws)


def _rank_kernel(idx_ref, tri_ref, rank_ref, cnt_ref, carry_ref):
    @pl.when(pl.program_id(0) == 0)
    def _():
        carry_ref[...] = jnp.zeros_like(carry_ref)

    idx = idx_ref[...]
    n_tok = idx.shape[1]
    iota_e = lax.broadcasted_iota(I32, (N_EXPERTS, n_tok), 0)
    member = jnp.zeros((N_EXPERTS, n_tok), F32)
    for k in range(TOP_K):
        member = member + jnp.where(iota_e == idx[k:k + 1, :], 1.0, 0.0)
    before = jnp.dot(member.astype(BF16), tri_ref[...], preferred_element_type=F32) + carry_ref[...]
    rows = [jnp.sum(jnp.where(iota_e == idx[k:k + 1, :], before, 0.0), axis=0, keepdims=True)
            for k in range(TOP_K)]
    rank_ref[...] = jnp.concatenate(rows, axis=0).astype(I32)
    carry_ref[...] = carry_ref[...] + jnp.sum(member, axis=1, keepdims=True)
    cnt_ref[...] = carry_ref[...].astype(I32)


def _rank(idx_t, tri):
    blk = pl.BlockSpec((TOP_K, BT_RANK), lambda i: (0, i))
    return pl.pallas_call(
        _rank_kernel,
        grid=(T_GRP // BT_RANK,),
        in_specs=[blk, _full_spec(tri, 1)],
        out_specs=[blk, pl.BlockSpec((N_EXPERTS, 1), lambda i: (0, 0))],
        out_shape=[jax.ShapeDtypeStruct((TOP_K, T_GRP), I32),
                   jax.ShapeDtypeStruct((N_EXPERTS, 1), I32)],
        scratch_shapes=[pltpu.VMEM((N_EXPERTS, 1), F32)],
        compiler_params=_cparams(("arbitrary",)),
        name="rank",
    )(idx_t, tri)


def _dest_kernel(idx_ref, rank_ref, cnt_ref, dest_ref, blk_e_ref, n_used_ref, blk_rows_ref):
    counts = cnt_ref[...]
    padded = (counts + (BM_E - 1)) // BM_E * BM_E
    r = lax.broadcasted_iota(I32, (N_EXPERTS, N_EXPERTS), 0)
    c = lax.broadcasted_iota(I32, (N_EXPERTS, N_EXPERTS), 1)
    padded_row = jnp.sum(jnp.where(r == c, padded, 0), axis=0, keepdims=True)
    pad_start = jnp.sum(jnp.where(c < r, padded_row, 0), axis=1, keepdims=True)

    idx = idx_ref[...]
    n_tok = idx.shape[1]
    iota_e = lax.broadcasted_iota(I32, (N_EXPERTS, n_tok), 0)
    rows = [jnp.sum(jnp.where(iota_e == idx[k:k + 1, :], pad_start, 0), axis=0, keepdims=True)
            for k in range(TOP_K)]
    dest_ref[...] = jnp.concatenate(rows, axis=0) + rank_ref[...]

    @pl.when(pl.program_id(0) == 0)
    def _():
        pad_end_row = jnp.sum(jnp.where(r <= c, padded, 0), axis=0, keepdims=True)
        b0 = lax.broadcasted_iota(I32, (N_BLOCKS_PAD, N_EXPERTS), 0) * BM_E
        be = jnp.minimum(jnp.sum(jnp.where(pad_end_row <= b0, 1, 0), axis=1, keepdims=True), N_EXPERTS - 1)
        blk_e_ref[...] = be
        n_used_ref[...] = pad_end_row[:, N_EXPERTS - 1:N_EXPERTS] // BM_E
        counts_row = jnp.sum(jnp.where(r == c, counts, 0), axis=0, keepdims=True)
        mine = lax.broadcasted_iota(I32, (N_BLOCKS_PAD, N_EXPERTS), 1) == be
        end_valid = jnp.sum(jnp.where(mine, pad_end_row - padded_row + counts_row, 0), axis=1, keepdims=True)
        blk_rows_ref[...] = jnp.clip(end_valid - b0[:, :1], 0, BM_E)


N_BLOCKS_PAD = (N_BLOCKS + 7) // 8 * 8


def _dest(idx_t, rank_t, counts):
    blk = pl.BlockSpec((TOP_K, BT_DEST), lambda i: (0, i))
    one = lambda s: pl.BlockSpec(s, lambda i: (0, 0))
    return pl.pallas_call(
        _dest_kernel,
        grid=(T_GRP // BT_DEST,),
        in_specs=[blk, blk, one((N_EXPERTS, 1))],
        out_specs=[blk, one((N_BLOCKS_PAD, 1)), one((1, 1)), one((N_BLOCKS_PAD, 1))],
        out_shape=[jax.ShapeDtypeStruct((TOP_K, T_GRP), I32),
                   jax.ShapeDtypeStruct((N_BLOCKS_PAD, 1), I32),
                   jax.ShapeDtypeStruct((1, 1), I32),
                   jax.ShapeDtypeStruct((N_BLOCKS_PAD, 1), I32)],
        compiler_params=_cparams(("arbitrary",)),
        name="dest",
    )(idx_t, rank_t, counts)


def _sc_mesh():
    return plsc.VectorSubcoreMesh(core_axis_name="c", subcore_axis_name="s")


def _sc_worker_id():
    return lax.axis_index("s") * SC_CORES + lax.axis_index("c")


def _dispatch_body(dest_hbm, xp_hbm, xs_hbm, idx_v, rows_v, sem_in, sem_out):
    n_chunks, _, n_tok = dest_hbm.shape
    per_worker = n_chunks // SC_WORKERS
    chunk0 = _sc_worker_id() * per_worker

    def loads(i):
        chunk = chunk0 + i
        t0 = pl.multiple_of(chunk * n_tok, n_tok)
        return (pltpu.make_async_copy(dest_hbm.at[chunk], idx_v.at[i % 2], sem_in.at[i % 2]),
                pltpu.make_async_copy(xp_hbm.at[pl.ds(t0, n_tok)], rows_v.at[i % 2], sem_in.at[i % 2]))

    def scatters(i):
        return [pltpu.make_async_copy(rows_v.at[i % 2], xs_hbm.at[idx_v.at[i % 2, k]], sem_out.at[i % 2])
                for k in range(TOP_K)]

    for cp in loads(0):
        cp.start()
    for i in range(per_worker):
        for cp in loads(i):
            cp.wait()
        if i >= 1:
            for cp in scatters(i - 1):
                cp.wait()
        if i + 1 < per_worker:
            for cp in loads(i + 1):
                cp.start()
        for cp in scatters(i):
            cp.start()
    for cp in scatters(per_worker - 1):
        cp.wait()


def _dispatch(dest_chunks, xp3):
    return pl.kernel(
        _dispatch_body,
        out_type=jax.ShapeDtypeStruct((N_SLOTS, ROW_CHUNKS, LANES), U32),
        mesh=_sc_mesh(),
        scratch_types=[pltpu.VMEM((2, TOP_K, SC_CHUNK), I32),
                       pltpu.VMEM((2, SC_CHUNK, ROW_CHUNKS, LANES), U32),
                       pltpu.SemaphoreType.DMA((2,)), pltpu.SemaphoreType.DMA((2,))],
        name="dispatch",
    )(dest_chunks, xp3)


def _gather_body(dest_hbm, ys_hbm, yt_hbm, idx_v, rows_v, sem_in, sem_out):
    n_chunks, _, n_tok = dest_hbm.shape
    per_worker = n_chunks // SC_WORKERS
    chunk0 = _sc_worker_id() * per_worker

    @pl.loop(0, per_worker)
    def _(i):
        chunk = chunk0 + i
        t0 = pl.multiple_of(chunk * n_tok, n_tok)
        pltpu.sync_copy(dest_hbm.at[chunk], idx_v)

        def gather(k):
            return pltpu.make_async_copy(ys_hbm.at[idx_v.at[k]], rows_v.at[k % SC_RING], sem_in.at[k % SC_RING])

        def store(k):
            return pltpu.make_async_copy(rows_v.at[k % SC_RING], yt_hbm.at[k, pl.ds(t0, n_tok)],
                                         sem_out.at[k % SC_RING])

        for k in range(SC_RING):
            gather(k).start()
        for k in range(TOP_K):
            gather(k).wait()
            store(k).start()
            if k + SC_RING < TOP_K:
                store(k).wait()
                gather(k + SC_RING).start()
        for k in range(TOP_K - SC_RING, TOP_K):
            store(k).wait()


def _gather(dest_chunks, ys3):
    n_chunks, _, n_tok = dest_chunks.shape
    assert n_chunks % SC_WORKERS == 0
    return pl.kernel(
        _gather_body,
        out_type=jax.ShapeDtypeStruct((TOP_K, n_chunks * n_tok, ROW_CHUNKS, LANES), U32),
        mesh=_sc_mesh(),
        scratch_types=[pltpu.VMEM((TOP_K, n_tok), I32),
                       pltpu.VMEM((SC_RING, n_tok, ROW_CHUNKS, LANES), U32),
                       pltpu.SemaphoreType.DMA((SC_RING,)), pltpu.SemaphoreType.DMA((SC_RING,))],
        name="gather",
    )(dest_chunks, ys3)


def _experts_kernel(blk_e_ref, n_used_ref, blk_rows_ref, xs_ref, wgu_ref, wd_ref, ys_ref):
    del blk_e_ref
    b = pl.program_id(0)

    def swiglu_rows(n_rows):
        chunks = _load_chunks(xs_ref, n_rows)
        x_lo = jnp.concatenate([_unpack_lo(p) for p in chunks], axis=-1).astype(BF16)
        x_hi = jnp.concatenate([_unpack_hi(p) for p in chunks], axis=-1).astype(BF16)
        gu = (jnp.dot(x_lo, wgu_ref[0, :HALF, :], preferred_element_type=F32)
              + jnp.dot(x_hi, wgu_ref[0, HALF:, :], preferred_element_type=F32))
        gate, up = gu[:, :D_EXPERT], gu[:, D_EXPERT:]
        hmid = (gate * _sigmoid(gate) * up).astype(BF16)
        _store_chunks(ys_ref, _pack_bf16_pairs(jnp.dot(hmid, wd_ref[0], preferred_element_type=F32)))

    @pl.when(b < n_used_ref[0])
    def _():
        valid = blk_rows_ref[b]
        for n_rows in range(E_STRIP, BM_E + 1, E_STRIP):
            @pl.when((valid > n_rows - E_STRIP) & (valid <= n_rows))
            def _(n_rows=n_rows):
                swiglu_rows(n_rows)


def _experts(blk_e, n_used, blk_rows, xs, wgu_bf, wd_bf):
    def blk(b, be, nu, nr):
        return jnp.minimum(b, nu[0] - 1)

    def by_expert(shape):
        return pl.BlockSpec((1,) + shape, lambda b, be, nu, nr: (be[blk(b, be, nu, nr)], 0, 0))

    tile = pl.BlockSpec((BM_E * ROW_CHUNKS, LANES), lambda b, be, nu, nr: (blk(b, be, nu, nr), 0))
    grid_spec = pltpu.PrefetchScalarGridSpec(
        num_scalar_prefetch=3,
        grid=(N_BLOCKS,),
        in_specs=[tile, by_expert((D_MODEL, 2 * D_EXPERT)), by_expert((D_EXPERT, D_MODEL))],
        out_specs=tile,
    )
    return pl.pallas_call(
        _experts_kernel,
        grid_spec=grid_spec,
        out_shape=jax.ShapeDtypeStruct((N_SLOTS * ROW_CHUNKS, LANES), U32),
        compiler_params=_cparams(("arbitrary",)),
        name="experts",
    )(blk_e, n_used, blk_rows, xs, wgu_bf, wd_bf)


def _combine_kernel(yt_ref, wts_ref, h1_ref, p_ref, gffn_ref, wsgu_ref, wsd_ref, gple_ref, wpg_ref, wpp_ref,
                    gfin_ref, *y_refs):
    y_ref = y_refs[-1]
    h1 = h1_ref[...]
    gu = jnp.dot(_rms(h1, gffn_ref[...]).astype(BF16), wsgu_ref[...], preferred_element_type=F32)
    sgate, sup = gu[:, :D_SHARED], gu[:, D_SHARED:]
    hsh = h1 + jnp.dot((sgate * _sigmoid(sgate) * sup).astype(BF16), wsd_ref[...], preferred_element_type=F32)
    wts = jnp.transpose(wts_ref[...])
    lo = [jnp.zeros((BT_COMB, LANES), F32) for _ in range(ROW_CHUNKS)]
    hi = [jnp.zeros((BT_COMB, LANES), F32) for _ in range(ROW_CHUNKS)]
    for k in range(TOP_K):
        w = wts[:, k:k + 1]
        for c, p in enumerate(_load_chunks(yt_ref, BT_COMB, lead=(k,))):
            lo[c] = lo[c] + w * _unpack_lo(p)
            hi[c] = hi[c] + w * _unpack_hi(p)
    h2 = hsh + jnp.concatenate(lo + hi, axis=-1)
    gate = _sigmoid(jnp.dot(_rms(h2, gple_ref[...]).astype(BF16), wpg_ref[...], preferred_element_type=F32))
    proj = jnp.dot(p_ref[...].astype(BF16), wpp_ref[...], preferred_element_type=F32)
    y_ref[...] = _rms(h2 + proj * gate, gfin_ref[...])


def _combine(yt, yt_row0, wts_t, wts_row0, h1, tok_row0, n_rows, p2d, p_row0, ws, y_prev, out_rows, out_row0):
    assert all(r % BT_COMB == 0 for r in (yt_row0, wts_row0, tok_row0, n_rows, p_row0, out_row0))
    g0, w0, t0, p0, o0 = (r // BT_COMB for r in (yt_row0, wts_row0, tok_row0, p_row0, out_row0))
    in_specs = [pl.BlockSpec((TOP_K, BT_COMB * ROW_CHUNKS, LANES), lambda i: (0, g0 + i, 0)),
                pl.BlockSpec((TOP_K, BT_COMB), lambda i: (0, w0 + i)),
                pl.BlockSpec((BT_COMB, D_MODEL), lambda i: (t0 + i, 0)),
                pl.BlockSpec((BT_COMB, PLE_DIM), lambda i: (p0 + i, 0))] + [_full_spec(w, 1) for w in ws]
    args = [yt, wts_t, h1, p2d, *ws]
    aliases = {}
    if y_prev is not None:
        in_specs.append(pl.BlockSpec(memory_space=pl.ANY))
        aliases = {len(args): 0}
        args.append(y_prev)
    return pl.pallas_call(
        _combine_kernel,
        grid=(n_rows // BT_COMB,),
        in_specs=in_specs,
        out_specs=pl.BlockSpec((BT_COMB, D_MODEL), lambda i: (o0 + i, 0)),
        out_shape=jax.ShapeDtypeStruct((out_rows, D_MODEL), F32),
        input_output_aliases=aliases,
        compiler_params=_cparams(("parallel",)),
        name="combine",
    )(*args)


def kernel(x_prompt, x_sample, cache_k, cache_v, state_pool, p_prompt, p_sample, g_mix, w_in, attn_sinks,
           w_pool, pool_scale, g_att_out, g_pool_out, w_out, g_ffn, w_router, router_bias, w_exp_gate,
           w_exp_up, w_exp_down, w_sh_gate, w_sh_up, w_sh_down, g_ple, w_ple_gate, w_ple_proj, g_final):
    row = lambda a: a.reshape(1, -1)
    xp2d = x_prompt.reshape(T_P, D_MODEL)
    xs2d = x_sample.reshape(T_S, D_MODEL)
    w_in_bf = w_in[0].astype(BF16)
    mixer_wts = [w_pool[0].astype(BF16), row(pool_scale[0]), row(g_att_out[0]), row(g_pool_out[0]),
                 w_out[0].astype(BF16)]

    tab_p = _rope_tables(np.arange(SEQ))
    tab_s = _rope_tables(PAST_LEN + np.arange(DEC_SEQ), reps=BM_IN // DEC_SEQ)

    q_p, k_p, v_p, u_p = _inproj(xp2d, row(g_mix[0]), w_in_bf, tab_p, BF16)
    q_s, k_s, v_s, u_s = _inproj(xs2d, row(g_mix[0]), w_in_bf, tab_s, F32)

    h1, wgu_bf, wd_bf = _mixer_prompt(attn_sinks[0], xp2d, q_p, k_p, v_p, u_p, mixer_wts,
                                      (w_exp_gate[0], w_exp_up[0], w_exp_down[0]))
    h1, k_sample, v_sample, pool_sample = _mixer_sample(
        attn_sinks[0], xs2d, q_s, k_s, v_s, u_s,
        cache_k[0].reshape(DEC_BATCH, WINDOW, KV_WIDTH), cache_v[0].reshape(DEC_BATCH, WINDOW, KV_WIDTH),
        state_pool, mixer_wts, h1)

    g_ffn_row = row(g_ffn[0])
    router_wts = (g_ffn_row, w_router[0].T, router_bias[0].reshape(N_EXPERTS, 1))
    tri = (lax.broadcasted_iota(I32, (BT_RANK, BT_RANK), 0)
           < lax.broadcasted_iota(I32, (BT_RANK, BT_RANK), 1)).astype(BF16)

    def index_chunks(d, n_tok):
        return d.reshape(TOP_K, d.shape[1] // n_tok, n_tok).transpose(1, 0, 2)

    groups = []
    for g in range(N_GROUPS):
        xp, idx_t, wts_t = _router(h1, g, *router_wts)
        rank_t, counts = _rank(idx_t, tri)
        dest_t, *plan = _dest(idx_t, rank_t, counts)
        dest_chunks = index_chunks(dest_t, SC_CHUNK)
        xs = _dispatch(dest_chunks, xp.reshape(T_GRP, ROW_CHUNKS, LANES))
        groups.append((wts_t, dest_chunks, xs, [a.reshape(-1) for a in plan]))

    ple_wts = [g_ffn_row, jnp.concatenate([w_sh_gate[0], w_sh_up[0]], axis=1).astype(BF16),
               w_sh_down[0].astype(BF16),
               row(g_ple[0]), w_ple_gate[0].astype(BF16), w_ple_proj[0].astype(BF16), row(g_final)]
    pp2d = p_prompt[0].reshape(T_P, PLE_DIM)
    ps2d = p_sample[0].reshape(T_S, PLE_DIM)
    y_p = y_s = None
    for g, (wts_t, dest_chunks, xs, plan) in enumerate(groups):
        lo = g * T_GRP
        ys = _experts(*plan, xs.reshape(N_SLOTS * ROW_CHUNKS, LANES), wgu_bf, wd_bf)
        ys3 = ys.reshape(N_SLOTS, ROW_CHUNKS, LANES)
        for s in range(N_SUB):
            a, b = lo + SUB_BOUNDS[s], lo + SUB_BOUNDS[s + 1]
            yt = _gather(dest_chunks[SUB_BOUNDS[s] // SC_CHUNK:SUB_BOUNDS[s + 1] // SC_CHUNK], ys3)
            yt = yt.reshape(TOP_K, (b - a) * ROW_CHUNKS, LANES)
            if a < T_P:
                n = min(b, T_P) - a
                y_p = _combine(yt, 0, wts_t, a - lo, h1, a, n, pp2d, a, ple_wts, y_p, T_P, a)
            if b > T_P:
                s0 = max(a, T_P)
                y_s = _combine(yt, s0 - a, wts_t, s0 - lo, h1, s0, b - s0, ps2d, s0 - T_P, ple_wts, y_s, T_S,
                               s0 - T_P)

    kv5 = lambda a, b: a.reshape(1, b, WINDOW, N_KV_HEADS, HEAD_DIM)
    k_prompt = kv5(k_p.reshape(BATCH, SEQ, KV_WIDTH)[:, SEQ - WINDOW:], BATCH)
    v_prompt = kv5(v_p.reshape(BATCH, SEQ, KV_WIDTH)[:, SEQ - WINDOW:], BATCH)
    pool_prompt = u_p.reshape(BATCH, SEQ, POOL_WIDTH)[:, SEQ - POOL_STATE:][None]
    return (y_p.reshape(BATCH, SEQ, D_MODEL), y_s.reshape(DEC_BATCH, DEC_SEQ, D_MODEL),
            k_prompt, v_prompt, pool_prompt,
            kv5(k_sample, DEC_BATCH), kv5(v_sample, DEC_BATCH), pool_sample)
```
